```python
import math
import jax, jax.numpy as jnp
from jax import lax
import numpy as np

D_MODEL = 1024
BATCH = 4
SEQ = 8192
DEPTH = 2

GRID_W = 64
CTX_LEN = 256
N_HEADS = 4
HEAD_DIM = 64
MIX_W = N_HEADS * HEAD_DIM
CHUNK = 128
ROPE_BASE = 10000.0
EPS = 1e-6
RET_DECAY_EXP0 = 5.0
SG_CHUNK = 128
CONV_W = 5
Q_LORA = 256
KV_LORA = 128
NOPE_DIM = 64
ROPE_DIM = 32
V_DIM = 64
N_EXPERTS = 16
N_GROUPS = 4
EXPERTS_PER_GROUP = N_EXPERTS // N_GROUPS
TOP_K = 2
D_EXPERT = 256
N_BRANCH = 4

RET_COLS = 4 * MIX_W
SG_COLS = 2 * MIX_W
DN_COLS = 4 * MIX_W + 4 * N_HEADS
MLA_COLS = Q_LORA + KV_LORA + ROPE_DIM
GATE_COLS = N_BRANCH * D_MODEL
OFF_RET = 0
OFF_SG = OFF_RET + RET_COLS
OFF_DN = OFF_SG + SG_COLS
OFF_MLA = OFF_DN + DN_COLS
OFF_GATE = OFF_MLA + MLA_COLS
IN_COLS = OFF_GATE + GATE_COLS

kernel_name = "hybrid_diffusion_ctx_prefix_block"


def rmsnorm(x, g):
    xf = x.astype(jnp.float32)
    xf = xf * lax.rsqrt(jnp.mean(xf * xf, axis=-1, keepdims=True) + EPS)
    return (xf * g.astype(jnp.float32)).astype(x.dtype)


def layernorm_noaffine(x):
    xf = x.astype(jnp.float32)
    mu = jnp.mean(xf, axis=-1, keepdims=True)
    var = jnp.mean(jnp.square(xf - mu), axis=-1, keepdims=True)
    return ((xf - mu) * lax.rsqrt(var + EPS)).astype(x.dtype)


def l2norm(x):
    xf = x.astype(jnp.float32)
    return xf * lax.rsqrt(jnp.sum(xf * xf, axis=-1, keepdims=True) + EPS)


def modulate(h, shift, scale):
    return h * (1 + scale) + shift


def adaln(cond, w, b):
    m = jax.nn.silu(cond) @ w + b
    m = m.reshape(m.shape[0], 1, 6, D_MODEL)
    return tuple(m[:, :, i] for i in range(6))


def rope_angles(pos, dim):
    half = dim // 2
    inv = ROPE_BASE ** (-jnp.arange(half, dtype=jnp.float32) / half)
    return pos.astype(jnp.float32)[:, None] * inv[None, :]


def rotate(x, ang):
    half = x.shape[-1] // 2
    cos = jnp.cos(ang).astype(x.dtype)
    sin = jnp.sin(ang).astype(x.dtype)
    x1, x2 = x[..., :half], x[..., half:]
    return jnp.concatenate([x1 * cos - x2 * sin, x2 * cos + x1 * sin], axis=-1)


def rotate_2d(x, ang_row, ang_col):
    half = x.shape[-1] // 2
    return jnp.concatenate([rotate(x[..., :half], ang_row), rotate(x[..., half:], ang_col)], axis=-1)


def to_heads(t):
    b_, n_, w_ = t.shape
    return t.reshape(b_, n_, N_HEADS, w_ // N_HEADS).transpose(0, 2, 1, 3)


def from_heads(t):
    b_, h_, n_, d_ = t.shape
    return t.transpose(0, 2, 1, 3).reshape(b_, n_, h_ * d_)


def to_chunks(t):
    b_, h_, n_ = t.shape[:3]
    t = t.reshape(b_, h_, n_ // CHUNK, CHUNK, *t.shape[3:])
    return jnp.moveaxis(t, 2, 0)


def from_chunks(t):
    t = jnp.moveaxis(t, 0, 2)
    return t.reshape(t.shape[0], t.shape[1], t.shape[2] * t.shape[3], *t.shape[4:])


def flip(t):
    return t[:, :, ::-1]


def retention_scan(q, k, v, log_g, s0, strict, need_out):
    idx = jnp.arange(CHUNK, dtype=jnp.float32)
    lg = log_g[:, None]
    qc, kc, vc = to_chunks(q), to_chunks(k), to_chunks(v)
    k_dec = jnp.exp(lg * (CHUNK - 1 - idx))[..., None]
    kv = jnp.einsum('nbhjd,nbhje->nbhde', kc * k_dec, vc)
    chunk_dec = jnp.exp(log_g * CHUNK)[:, None, None]

    def step(s, kv_n):
        return chunk_dec * s + kv_n, s

    s_fin, s_prev = lax.scan(step, s0, kv)
    if not need_out:
        return None, s_fin
    diff = idx[:, None] - idx[None, :]
    mask = (diff > 0) if strict else (diff >= 0)
    dmat = jnp.where(mask, jnp.exp(lg[..., None] * jnp.where(mask, diff, 0.0)), 0.0)
    q_dec = jnp.exp(lg * (idx + 1))[..., None]
    scores = jnp.einsum('nbhid,nbhjd->nbhij', qc, kc) * dmat
    o = (jnp.einsum('nbhij,nbhjd->nbhid', scores, vc)
         + jnp.einsum('nbhid,nbhde->nbhie', qc * q_dec, s_prev))
    return from_chunks(o), s_fin


def retention_branch(pc, pl, ang_t, ret_logit, ctx_out):
    f32 = jnp.float32
    out_dtype = pl.dtype

    def prep(p, rot):
        q, k, v, g = jnp.split(p, 4, axis=-1)
        q, k, v = (to_heads(t).astype(f32) for t in (q, k, v))
        k = k * HEAD_DIM ** -0.5
        if rot:
            q, k = rotate(q, ang_t), rotate(k, ang_t)
        return q, k, v, g

    qc, kc, vc, gc = prep(pc, False)
    ql, kl, vl, gl = prep(pl, True)
    log_g = jax.nn.log_sigmoid(ret_logit.astype(f32))
    s0 = jnp.zeros(qc.shape[:2] + (HEAD_DIM, HEAD_DIM), f32)
    oc_f, sc_f = retention_scan(qc, kc, vc, log_g[0], s0, False, ctx_out)
    oc_b, sc_b = retention_scan(flip(qc), flip(kc), flip(vc), log_g[1], s0, True, ctx_out)
    ol_f, _ = retention_scan(ql, kl, vl, log_g[0], sc_f, False, True)
    ol_b, _ = retention_scan(flip(ql), flip(kl), flip(vl), log_g[1], sc_b, True, True)

    def finish(o, g):
        return from_heads(layernorm_noaffine(o)).astype(out_dtype) * jax.nn.silu(g)

    y_l = finish(ol_f + flip(ol_b), gl)
    y_c = finish(oc_f + flip(oc_b), gc) if ctx_out else None
    return y_l, y_c


def spatial_gating_branch(p, norm_g, w_s, b_s):
    z = jax.nn.gelu(p)
    u, v = z[..., :MIX_W], z[..., MIX_W:]
    v = layernorm_noaffine(v) * norm_g
    b_, n_, _ = v.shape
    v = v.reshape(b_, n_ // SG_CHUNK, SG_CHUNK, N_HEADS, HEAD_DIM)
    mixed = jnp.einsum('gij,bnjgd->bnigd', w_s, v) + b_s.T[:, :, None]
    return u * mixed.reshape(b_, n_, MIX_W)


def short_conv(x, w):
    pad = w.shape[0] // 2
    n_ = x.shape[1]
    xp = jnp.pad(x, ((0, 0), (pad, pad), (0, 0)))
    y = xp[:, 0:n_] * w[0]
    for i in range(1, w.shape[0]):
        y = y + xp[:, i:i + n_] * w[i]
    return y


def deltanet_scan(q, k, v, g, beta, s0, need_out):
    qc, kc, vc = to_chunks(q), to_chunks(k), to_chunks(v)
    gc, bc = to_chunks(g), to_chunks(beta)
    G = jnp.cumsum(gc, axis=-1)
    idx = jnp.arange(CHUNK)
    strict = idx[:, None] > idx[None, :]
    incl = idx[:, None] >= idx[None, :]
    gdiff = G[..., :, None] - G[..., None, :]
    kk = jnp.einsum('nbhid,nbhjd->nbhij', kc, kc)
    a_mat = jnp.where(strict, jnp.exp(jnp.where(strict, gdiff, 0.0)) * kk, 0.0) * bc[..., None]
    a_mat = a_mat + jnp.eye(CHUNK, dtype=a_mat.dtype)
    u = lax.linalg.triangular_solve(a_mat, vc * bc[..., None], left_side=True, lower=True, unit_diagonal=True)
    wk = lax.linalg.triangular_solve(a_mat, kc * (bc * jnp.exp(G))[..., None], left_side=True, lower=True, unit_diagonal=True)
    g_last = G[..., -1:]
    k_dec = kc * jnp.exp(g_last - G)[..., None]

    def step(s, inp):
        u_n, wk_n, kdec_n, glast_n = inp
        w = u_n - jnp.einsum('bhik,bhkv->bhiv', wk_n, s)
        s_new = jnp.exp(glast_n)[..., None] * s + jnp.einsum('bhik,bhiv->bhkv', kdec_n, w)
        return s_new, (w, s)

    s_fin, (w_all, s_prev) = lax.scan(step, s0, (u, wk, k_dec, g_last))
    if not need_out:
        return None, s_fin
    attn = jnp.where(incl, jnp.exp(jnp.where(incl, gdiff, 0.0)), 0.0) * jnp.einsum('nbhid,nbhjd->nbhij', qc, kc)
    o = (jnp.exp(G)[..., None] * jnp.einsum('nbhik,nbhkv->nbhiv', qc, s_prev)
         + jnp.einsum('nbhij,nbhjv->nbhiv', attn, w_all))
    return from_chunks(o), s_fin


def deltanet_branch(pc, pl, conv_w, A_log, dt_bias, norm_g, ctx_out):
    f32 = jnp.float32
    neg_a = -jnp.exp(A_log.astype(f32))
    dtb = dt_bias.astype(f32)

    def prep(p):
        b_, n_, _ = p.shape
        qkv = jax.nn.silu(short_conv(p[..., :3 * MIX_W], conv_w))
        q, k, v = jnp.split(qkv, 3, axis=-1)
        q = l2norm(to_heads(q)) * HEAD_DIM ** -0.5
        k = l2norm(to_heads(k))
        v = to_heads(v).astype(f32)
        a = p[..., 4 * MIX_W:4 * MIX_W + 2 * N_HEADS].astype(f32).reshape(b_, n_, 2, N_HEADS)
        b = p[..., 4 * MIX_W + 2 * N_HEADS:].astype(f32).reshape(b_, n_, 2, N_HEADS)
        g = jnp.moveaxis(neg_a * jax.nn.softplus(a + dtb), 1, -1)
        beta = jnp.moveaxis(jax.nn.sigmoid(b), 1, -1)
        return q, k, v, g, beta, p[..., 3 * MIX_W:4 * MIX_W]

    qc, kc, vc, gc, bc, zc = prep(pc)
    ql, kl, vl, gl, bl, zl = prep(pl)
    s0 = jnp.zeros(qc.shape[:2] + (HEAD_DIM, HEAD_DIM), f32)
    oc_f, sc_f = deltanet_scan(qc, kc, vc, gc[:, 0], bc[:, 0], s0, ctx_out)
    oc_b, sc_b = deltanet_scan(flip(qc), flip(kc), flip(vc), flip(gc[:, 1]), flip(bc[:, 1]), s0, ctx_out)
    ol_f, _ = deltanet_scan(ql, kl, vl, gl[:, 0], bl[:, 0], sc_f, True)
    ol_b, _ = deltanet_scan(flip(ql), flip(kl), flip(vl), flip(gl[:, 1]), flip(bl[:, 1]), sc_b, True)

    def finish(o, z):
        return from_heads(rmsnorm(o, norm_g)).astype(z.dtype) * jax.nn.silu(z)

    y_l = finish(ol_f + flip(ol_b), zl)
    y_c = finish(oc_f + flip(oc_b), zc) if ctx_out else None
    return y_l, y_c


def mla_branch(pc, pl, ang_row, ang_col, q_norm_g, kv_norm_g, w_uq, w_ukv, ctx_out):
    scale = (NOPE_DIM + ROPE_DIM) ** -0.5

    def keys_values(p, rot):
        ckv = rmsnorm(p[..., Q_LORA:Q_LORA + KV_LORA], kv_norm_g)
        kv = to_heads(ckv @ w_ukv)
        k_nope, v = kv[..., :NOPE_DIM], kv[..., NOPE_DIM:]
        kr = p[..., Q_LORA + KV_LORA:][:, None]
        if rot:
            kr = rotate_2d(kr, ang_row, ang_col)
        k = jnp.concatenate([k_nope, jnp.broadcast_to(kr, k_nope.shape[:3] + (ROPE_DIM,))], axis=-1)
        return k, v

    def queries(p, rot):
        cq = rmsnorm(p[..., :Q_LORA], q_norm_g)
        q = to_heads(cq @ w_uq)
        if rot:
            q = jnp.concatenate([q[..., :NOPE_DIM], rotate_2d(q[..., NOPE_DIM:], ang_row, ang_col)], axis=-1)
        return q

    def attend(q, k, v):
        s = jnp.einsum('bhqd,bhkd->bhqk', q, k).astype(jnp.float32) * scale
        pr = jax.nn.softmax(s, axis=-1).astype(v.dtype)
        return jnp.einsum('bhqk,bhkd->bhqd', pr, v)

    kc, vc = keys_values(pc, False)
    kl, vl = keys_values(pl, True)
    k_all = jnp.concatenate([kc, kl], axis=2)
    v_all = jnp.concatenate([vc, vl], axis=2)
    ql = queries(pl, True)
    b_, h_, n_, dq = ql.shape
    qb = jnp.moveaxis(ql.reshape(b_, h_, n_ // CHUNK, CHUNK, dq), 2, 0)
    ob = lax.map(lambda qblk: attend(qblk, k_all, v_all), qb)
    y_l = from_heads(jnp.moveaxis(ob, 0, 2).reshape(b_, h_, n_, V_DIM))
    y_c = from_heads(attend(queries(pc, False), kc, vc)) if ctx_out else None
    return y_l, y_c


def hybrid_mixer(pc, pl, ang_t, ang_row, ang_col, ret_logit, sg_norm_g, sg_w, sg_b,
                 dn_conv_w, dn_A_log, dn_dt_bias, dn_norm_g, mla_q_norm_g, mla_kv_norm_g,
                 mla_w_uq, mla_w_ukv, w_branch, w_out, ctx_out):
    def cols(p, a, b):
        return p[..., a:b]

    ret_l, ret_c = retention_branch(cols(pc, OFF_RET, OFF_SG), cols(pl, OFF_RET, OFF_SG), ang_t, ret_logit, ctx_out)
    sg_l = spatial_gating_branch(cols(pl, OFF_SG, OFF_DN), sg_norm_g, sg_w, sg_b)
    dn_l, dn_c = deltanet_branch(cols(pc, OFF_DN, OFF_MLA), cols(pl, OFF_DN, OFF_MLA),
                                 dn_conv_w, dn_A_log, dn_dt_bias, dn_norm_g, ctx_out)
    mla_l, mla_c = mla_branch(cols(pc, OFF_MLA, OFF_GATE), cols(pl, OFF_MLA, OFF_GATE), ang_row, ang_col,
                              mla_q_norm_g, mla_kv_norm_g, mla_w_uq, mla_w_ukv, ctx_out)

    def merge(p, ys):
        acc = None
        for i, y in enumerate(ys):
            gate = jax.nn.sigmoid(cols(p, OFF_GATE + i * D_MODEL, OFF_GATE + (i + 1) * D_MODEL))
            term = gate * (y @ w_branch[i])
            acc = term if acc is None else acc + term
        return acc @ w_out

    y_l = merge(pl, (ret_l, sg_l, dn_l, mla_l))
    if not ctx_out:
        return y_l, None
    sg_c = spatial_gating_branch(cols(pc, OFF_SG, OFF_DN), sg_norm_g, sg_w, sg_b)
    y_c = merge(pc, (ret_c, sg_c, dn_c, mla_c))
    return y_l, y_c


def swiglu(t, w1, w3, w2):
    return (jax.nn.silu(t @ w1) * (t @ w3)) @ w2


def moe_ffn(h, router_w, router_bias, w1, w3, w2, sw1, sw3, sw2):
    shp = h.shape
    t = h.reshape(-1, shp[-1])
    aff = jax.nn.sigmoid((t @ router_w).astype(jnp.float32))
    sel = aff + router_bias.astype(jnp.float32)
    grp_score = lax.top_k(sel.reshape(-1, N_GROUPS, EXPERTS_PER_GROUP), TOP_K)[0].sum(-1)
    best = jnp.argmax(grp_score, axis=-1)
    in_grp = (jnp.arange(N_EXPERTS) // EXPERTS_PER_GROUP)[None, :] == best[:, None]
    _, idx = lax.top_k(jnp.where(in_grp, sel, -jnp.inf), TOP_K)
    wts = jnp.take_along_axis(aff, idx, axis=-1)
    wts = wts / jnp.sum(wts, axis=-1, keepdims=True)
    comb = jnp.sum(jax.nn.one_hot(idx, N_EXPERTS, dtype=jnp.float32) * wts[..., None], axis=1).astype(t.dtype)
    out = swiglu(t, sw1, sw3, sw2)
    for e in range(N_EXPERTS):
        out = out + comb[:, e:e + 1] * swiglu(t, w1[e], w3[e], w2[e])
    return out.reshape(shp)


def setup_inputs(seed: int = 0) -> dict:
    key = jax.random.key(seed)
    ks = iter(jax.random.split(key, 48))
    f32 = jnp.float32

    def nrm(shape, scale):
        return jax.random.normal(next(ks), shape, f32) * scale

    def gain(shape):
        return 1.0 + 0.05 * jax.random.normal(next(ks), shape, f32)

    L, D = DEPTH, D_MODEL
    x = nrm((BATCH, SEQ, D), 1.0)
    c = nrm((BATCH, D), 1.0)
    ctx = nrm((BATCH, CTX_LEN, D), 1.0)
    c_ctx = nrm((D,), 1.0)
    w_ada = nrm((L, D, 6 * D), 0.5 * D ** -0.5)
    b_ada = nrm((L, 6 * D), 0.01)
    g_pre1 = gain((L, D))
    g_post1 = gain((L, D))
    g_pre2 = gain((L, D))
    g_post2 = gain((L, D))
    w_in = nrm((L, D, IN_COLS), D ** -0.5)
    heads = jnp.arange(N_HEADS, dtype=f32)
    ret_decay_logit = jnp.log(2.0 ** (RET_DECAY_EXP0 + heads) - 1.0) + nrm((L, 2, N_HEADS), 0.05)
    sg_norm_g = gain((L, MIX_W))
    sg_w = nrm((L, N_HEADS, SG_CHUNK, SG_CHUNK), 0.5 * SG_CHUNK ** -0.5)
    sg_b = 1.0 + nrm((L, N_HEADS, SG_CHUNK), 0.02)
    dn_conv_w = nrm((L, CONV_W, 3 * MIX_W), CONV_W ** -0.5)
    dn_A_log = jnp.log(jax.random.uniform(next(ks), (L, 2, N_HEADS), f32, 1.0, 16.0))
    dt = jnp.exp(jax.random.uniform(next(ks), (L, 2, N_HEADS), f32, math.log(1e-3), math.log(1e-1)))
    dn_dt_bias = dt + jnp.log(-jnp.expm1(-dt))
    dn_norm_g = gain((L, HEAD_DIM))
    mla_q_norm_g = gain((L, Q_LORA))
    mla_kv_norm_g = gain((L, KV_LORA))
    mla_w_uq = nrm((L, Q_LORA, N_HEADS * (NOPE_DIM + ROPE_DIM)), Q_LORA ** -0.5)
    mla_w_ukv = nrm((L, KV_LORA, N_HEADS * (NOPE_DIM + V_DIM)), KV_LORA ** -0.5)
    w_branch = nrm((L, N_BRANCH, MIX_W, D), MIX_W ** -0.5)
    w_out = nrm((L, D, D), D ** -0.5)
    router_w = nrm((D, N_EXPERTS), D ** -0.5)
    router_bias = nrm((N_EXPERTS,), 0.01)
    moe_w1 = nrm((L, N_EXPERTS, D, D_EXPERT), D ** -0.5)
    moe_w3 = nrm((L, N_EXPERTS, D, D_EXPERT), D ** -0.5)
    moe_w2 = nrm((L, N_EXPERTS, D_EXPERT, D), D_EXPERT ** -0.5)
    shared_w1 = nrm((L, D, D_EXPERT), D ** -0.5)
    shared_w3 = nrm((L, D, D_EXPERT), D ** -0.5)
    shared_w2 = nrm((L, D_EXPERT, D), D_EXPERT ** -0.5)
    return {"x": x, "c": c, "ctx": ctx, "c_ctx": c_ctx, "w_ada": w_ada, "b_ada": b_ada,
            "g_pre1": g_pre1, "g_post1": g_post1, "g_pre2": g_pre2, "g_post2": g_post2,
            "w_in": w_in, "ret_decay_logit": ret_decay_logit, "sg_norm_g": sg_norm_g, "sg_w": sg_w,
            "sg_b": sg_b, "dn_conv_w": dn_conv_w, "dn_A_log": dn_A_log, "dn_dt_bias": dn_dt_bias,
            "dn_norm_g": dn_norm_g, "mla_q_norm_g": mla_q_norm_g, "mla_kv_norm_g": mla_kv_norm_g,
            "mla_w_uq": mla_w_uq, "mla_w_ukv": mla_w_ukv, "w_branch": w_branch, "w_out": w_out,
            "router_w": router_w, "router_bias": router_bias, "moe_w1": moe_w1, "moe_w3": moe_w3,
            "moe_w2": moe_w2, "shared_w1": shared_w1, "shared_w3": shared_w3, "shared_w2": shared_w2}


def reference(x, c, ctx, c_ctx, w_ada, b_ada, g_pre1, g_post1, g_pre2, g_post2, w_in,
              ret_decay_logit, sg_norm_g, sg_w, sg_b, dn_conv_w, dn_A_log, dn_dt_bias, dn_norm_g,
              mla_q_norm_g, mla_kv_norm_g, mla_w_uq, mla_w_ukv, w_branch, w_out,
              router_w, router_bias, moe_w1, moe_w3, moe_w2, shared_w1, shared_w3, shared_w2):
    n_lat = x.shape[1]
    rows = n_lat // GRID_W
    t_pos = jnp.arange(n_lat)
    row = jnp.repeat(jnp.arange(rows), GRID_W)
    col = jnp.tile(jnp.arange(GRID_W), rows)
    ang_t = rope_angles(t_pos, HEAD_DIM)
    ang_row = rope_angles(row, ROPE_DIM // 2)
    ang_col = rope_angles(col, ROPE_DIM // 2)

    xl, xc = x, ctx
    for l in range(DEPTH):
        last = l == DEPTH - 1
        ml = adaln(c, w_ada[l], b_ada[l])
        mc = adaln(c_ctx[None], w_ada[l], b_ada[l])
        pl = modulate(rmsnorm(xl, g_pre1[l]), ml[0], ml[1]) @ w_in[l]
        pc = modulate(rmsnorm(xc, g_pre1[l]), mc[0], mc[1]) @ w_in[l]
        yl, yc = hybrid_mixer(pc, pl, ang_t, ang_row, ang_col, ret_decay_logit[l], sg_norm_g[l], sg_w[l], sg_b[l],
                              dn_conv_w[l], dn_A_log[l], dn_dt_bias[l], dn_norm_g[l], mla_q_norm_g[l],
                              mla_kv_norm_g[l], mla_w_uq[l], mla_w_ukv[l], w_branch[l], w_out[l], not last)
        xl = xl + ml[2] * rmsnorm(yl, g_post1[l])
        hl = modulate(rmsnorm(xl, g_pre2[l]), ml[3], ml[4])
        xl = xl + ml[5] * rmsnorm(moe_ffn(hl, router_w, router_bias, moe_w1[l], moe_w3[l], moe_w2[l],
                                          shared_w1[l], shared_w3[l], shared_w2[l]), g_post2[l])
        if not last:
            xc = xc + mc[2] * rmsnorm(yc, g_post1[l])
            hc = modulate(rmsnorm(xc, g_pre2[l]), mc[3], mc[4])
            xc = xc + mc[5] * rmsnorm(moe_ffn(hc, router_w, router_bias, moe_w1[l], moe_w3[l], moe_w2[l],
                                              shared_w1[l], shared_w3[l], shared_w2[l]), g_post2[l])
    return xl
```

```python
import functools
import math

import jax
import jax.numpy as jnp
from jax import lax
from jax.experimental import pallas as pl
from jax.experimental.pallas import tpu as pltpu

F32 = jnp.float32
BF16 = jnp.bfloat16
HIGHEST = lax.Precision.HIGHEST

D_MODEL = 1024
GRID_W = 64
N_HEADS = 4
HEAD_DIM = 64
MIX_W = N_HEADS * HEAD_DIM
CHUNK = 128
ROPE_BASE = 10000.0
EPS = 1e-6
RET_DECAY_EXP0 = 5.0
CONV_W = 5
Q_LORA = 256
KV_LORA = 128
NOPE_DIM = 64
ROPE_DIM = 32
V_DIM = 64
N_EXPERTS = 16
N_GROUPS = 4
EXPERTS_PER_GROUP = N_EXPERTS // N_GROUPS
D_EXPERT = 256
N_BRANCH = 4

RET_COLS = 4 * MIX_W
SG_COLS = 2 * MIX_W
DN_COLS = 4 * MIX_W + 4 * N_HEADS
MLA_COLS = Q_LORA + KV_LORA + ROPE_DIM
GATE_COLS = N_BRANCH * D_MODEL
OFF_RET = 0
OFF_SG = OFF_RET + RET_COLS
OFF_DN = OFF_SG + SG_COLS
OFF_MLA = OFF_DN + DN_COLS
OFF_GATE = OFF_MLA + MLA_COLS

P_RET = 0
P_DN = 1024
P_SG = 2048
P_MLA = 2560
P_GATE = 3072
P_COLS = 7168
MLA_PAD = 512

VMEM_LIMIT = 56 * 1024 * 1024


def _dot(a, b, precision=None):
    return jnp.dot(a, b, preferred_element_type=F32, precision=precision)


def _dot_nt(a, b, precision=None):
    return lax.dot_general(a, b, (((1,), (1,)), ((), ())), preferred_element_type=F32, precision=precision)


def _dot_tn(a, b):
    return lax.dot_general(a, b, (((0,), (0,)), ((), ())), preferred_element_type=F32)


def _mm(a, b):
    return _dot(a.astype(BF16), b.astype(BF16))


def _params(*sem):
    return pltpu.CompilerParams(dimension_semantics=sem, vmem_limit_bytes=VMEM_LIMIT)


def _pick(n, cands):
    for c in cands:
        if n % c == 0:
            return c
    raise ValueError(f"no tile for {n}")


def _head_of_lane(width, group):
    return lax.broadcasted_iota(jnp.int32, (1, width), 1) // group


def _stack_heads(x):
    head = _head_of_lane(MIX_W, HEAD_DIM)
    xf = x.astype(F32)
    return jnp.concatenate([jnp.where(head == h, xf, 0.0).astype(BF16) for h in range(N_HEADS)], axis=0)


def _expand_heads(cols):
    head = _head_of_lane(MIX_W, HEAD_DIM)
    out = cols[:, N_HEADS - 1:N_HEADS]
    for h in range(N_HEADS - 2, -1, -1):
        out = jnp.where(head <= h, cols[:, h:h + 1], out)
    return out


def _head_sum(x, ones_bd):
    hi = x.astype(BF16)
    lo = (x - hi.astype(F32)).astype(BF16)
    return _dot(hi, ones_bd) + _dot(lo, ones_bd)


def _rot(x_bf, cos, sin, perm):
    return x_bf.astype(F32) * cos + _dot(x_bf, perm) * sin


def _norm_modulate(x, g, isc, mc_ref, ml_ref, shift_row, scale_row):
    h = x * lax.rsqrt(jnp.mean(x * x, axis=-1, keepdims=True) + EPS) * g
    shift = jnp.where(isc, mc_ref[0, shift_row:shift_row + 1, :], ml_ref[0, shift_row:shift_row + 1, :])
    scale = jnp.where(isc, mc_ref[0, scale_row:scale_row + 1, :], ml_ref[0, scale_row:scale_row + 1, :])
    return h * (1.0 + scale) + shift


def _adaln_kernel(c_ref, w_ref, b_ref, o_ref):
    c = c_ref[...]
    s = c * jax.nn.sigmoid(c)
    o_ref[0] = _dot(s, w_ref[0], precision=HIGHEST) + b_ref[0]


def _adaln(cond, w_ada, b_ada):
    n_l, d, d6 = w_ada.shape
    r = cond.shape[0]
    tn = 1024
    return pl.pallas_call(
        _adaln_kernel,
        grid=(n_l, d6 // tn),
        in_specs=[pl.BlockSpec((r, d), lambda l, j: (0, 0)),
                  pl.BlockSpec((1, d, tn), lambda l, j: (l, 0, j)),
                  pl.BlockSpec((1, 1, tn), lambda l, j: (l, 0, j))],
        out_specs=pl.BlockSpec((1, r, tn), lambda l, j: (l, 0, j)),
        out_shape=jax.ShapeDtypeStruct((n_l, r, d6), F32),
        compiler_params=_params("arbitrary", "arbitrary"),
        name="adaln",
    )(cond, w_ada, b_ada.reshape(n_l, 1, d6))


def _inproj_kernel(x_ref, ml_ref, mc_ref, g_ref, w_ref, wab_ref, p_ref, ab_ref, xn_ref, *, tm, rb, n_ctx):
    i = pl.program_id(1)
    j = pl.program_id(2)

    @pl.when(j == 0)
    def _():
        def blk(r, carry):
            r0 = pl.multiple_of(r * rb, rb)
            x = x_ref[0, pl.ds(r0, rb), :]
            rows = lax.broadcasted_iota(jnp.int32, (rb, 1), 0) + (i * tm + r0)
            hn = _norm_modulate(x, g_ref[...], rows < n_ctx, mc_ref, ml_ref, 0, 1)
            xn_ref[pl.ds(r0, rb), :] = hn.astype(BF16)
            return carry

        lax.fori_loop(0, tm // rb, blk, 0)
        ab_ref[0] = _dot_nt(wab_ref[...], xn_ref[...])

    p_ref[0] = _dot(xn_ref[...], w_ref[...]).astype(BF16)


def _inproj(xa, mod, g, w, wab, n_ctx):
    nb, nt, d = xa.shape
    tm = _pick(nt, (1408, 768, 384, 256, 128))
    tn = 1792
    kern = functools.partial(_inproj_kernel, tm=tm, rb=128, n_ctx=n_ctx)
    return pl.pallas_call(
        kern,
        grid=(nb, nt // tm, P_COLS // tn),
        in_specs=[pl.BlockSpec((1, tm, d), lambda b, i, j: (b, i, 0)),
                  pl.BlockSpec((1, 6, d), lambda b, i, j: (b, 0, 0)),
                  pl.BlockSpec((1, 6, d), lambda b, i, j: (nb, 0, 0)),
                  pl.BlockSpec((1, d), lambda b, i, j: (0, 0)),
                  pl.BlockSpec((d, tn), lambda b, i, j: (0, j)),
                  pl.BlockSpec((16, d), lambda b, i, j: (0, 0))],
        out_specs=[pl.BlockSpec((1, tm, tn), lambda b, i, j: (b, i, j)),
                   pl.BlockSpec((1, 16, tm), lambda b, i, j: (b, 0, i))],
        out_shape=[jax.ShapeDtypeStruct((nb, nt, P_COLS), BF16),
                   jax.ShapeDtypeStruct((nb, 16, nt), F32)],
        scratch_shapes=[pltpu.VMEM((tm, d), BF16)],
        compiler_params=_params("arbitrary", "arbitrary", "arbitrary"),
        name="inproj",
    )(xa, mod, mod, g, w, wab)


def _bwd_chunk(t, ncc, nc):
    return jnp.where(t < ncc, ncc - 1 - t, nc - 1 - (t - ncc))


def _ret_state_kernel(pf_ref, pb_ref, cf_ref, sf_ref, cb_ref, sb_ref, perm_ref, kd_ref, cd_ref, bd_ref,
                      of_ref, ob_ref, st_f, st_b):
    t = pl.program_id(1)

    @pl.when(t == 0)
    def _():
        st_f[...] = jnp.zeros_like(st_f)
        st_b[...] = jnp.zeros_like(st_b)

    def upd(p_ref, c_ref, s_ref, d, st, o_ref):
        k = p_ref[0, :, MIX_W:2 * MIX_W]
        v = p_ref[0, :, 2 * MIX_W:3 * MIX_W]
        kr = _rot(k, c_ref[...], s_ref[...], perm_ref[...]) * (HEAD_DIM ** -0.5)
        o_ref[0, 0] = st[...].astype(BF16)
        inc = _dot_tn((kr * kd_ref[d]).astype(BF16), v)
        st[...] = cd_ref[d] * st[...] + bd_ref[...] * inc

    upd(pf_ref, cf_ref, sf_ref, 0, st_f, of_ref)
    upd(pb_ref, cb_ref, sb_ref, 1, st_b, ob_ref)


def _ret_out_kernel(p_ref, c_ref, s_ref, sf_ref, sb_ref, perm_ref, dm_ref, qd_ref, ones_ref, y_ref):
    p = p_ref[0]
    q = p[:, 0:MIX_W]
    k = p[:, MIX_W:2 * MIX_W]
    v = p[:, 2 * MIX_W:3 * MIX_W]
    g = p[:, 3 * MIX_W:4 * MIX_W].astype(F32)
    cos, sin, perm = c_ref[...], s_ref[...], perm_ref[...]
    qr = _rot(q, cos, sin, perm)
    kr = _rot(k, cos, sin, perm) * (HEAD_DIM ** -0.5)
    sc = _dot_nt(qr.astype(BF16), _stack_heads(kr)) * dm_ref[...]
    o = _dot(sc.astype(BF16), _stack_heads(v))
    qs = jnp.concatenate([(qr * qd_ref[0]).astype(BF16), (qr * qd_ref[1]).astype(BF16)], axis=1)
    ss = jnp.concatenate([sf_ref[0, 0], sb_ref[0, 0]], axis=0)
    o = o + _dot(qs, ss)
    ones_bd = ones_ref[...]
    mu = _head_sum(o, ones_bd) * (1.0 / HEAD_DIM)
    oc = o - mu
    var = _head_sum(oc * oc, ones_bd) * (1.0 / HEAD_DIM)
    y = oc * lax.rsqrt(var + EPS)
    y_ref[0] = (y * (g * jax.nn.sigmoid(g))).astype(BF16)


def _retention(p, cos, sin, perm, tabs, ncc):
    nb, nt, _ = p.shape
    nc = nt // CHUNK
    kd, cd, qd, dm, bd, ones_bd = tabs
    fwd = lambda b, t: (b, t, 0)
    bwd = lambda b, t: (b, _bwd_chunk(t, ncc, nc), 0)
    tab_f = lambda b, t: (t, 0)
    tab_b = lambda b, t: (_bwd_chunk(t, ncc, nc), 0)
    c2 = lambda b, t: (0, 0)
    c3 = lambda b, t: (0, 0, 0)
    st_shape = jax.ShapeDtypeStruct((nb, nc, MIX_W, MIX_W), BF16)
    st_f, st_b = pl.pallas_call(
        _ret_state_kernel,
        grid=(nb, nc),
        in_specs=[pl.BlockSpec((1, CHUNK, RET_COLS), fwd),
                  pl.BlockSpec((1, CHUNK, RET_COLS), bwd),
                  pl.BlockSpec((CHUNK, MIX_W), tab_f), pl.BlockSpec((CHUNK, MIX_W), tab_f),
                  pl.BlockSpec((CHUNK, MIX_W), tab_b), pl.BlockSpec((CHUNK, MIX_W), tab_b),
                  pl.BlockSpec((MIX_W, MIX_W), c2),
                  pl.BlockSpec((2, CHUNK, MIX_W), c3),
                  pl.BlockSpec((2, 1, MIX_W), c3),
                  pl.BlockSpec((MIX_W, MIX_W), c2)],
        out_specs=[pl.BlockSpec((1, 1, MIX_W, MIX_W), lambda b, t: (b, t, 0, 0)),
                   pl.BlockSpec((1, 1, MIX_W, MIX_W), lambda b, t: (b, _bwd_chunk(t, ncc, nc), 0, 0))],
        out_shape=[st_shape, st_shape],
        scratch_shapes=[pltpu.VMEM((MIX_W, MIX_W), F32), pltpu.VMEM((MIX_W, MIX_W), F32)],
        compiler_params=_params("arbitrary", "arbitrary"),
        name="ret_state",
    )(p, p, cos, sin, cos, sin, perm, kd, cd, bd)
    return pl.pallas_call(
        _ret_out_kernel,
        grid=(nb, nc),
        in_specs=[pl.BlockSpec((1, CHUNK, RET_COLS), fwd),
                  pl.BlockSpec((CHUNK, MIX_W), tab_f), pl.BlockSpec((CHUNK, MIX_W), tab_f),
                  pl.BlockSpec((1, 1, MIX_W, MIX_W), lambda b, t: (b, t, 0, 0)),
                  pl.BlockSpec((1, 1, MIX_W, MIX_W), lambda b, t: (b, t, 0, 0)),
                  pl.BlockSpec((MIX_W, MIX_W), c2),
                  pl.BlockSpec((CHUNK, N_HEADS * CHUNK), c2),
                  pl.BlockSpec((2, CHUNK, MIX_W), c3),
                  pl.BlockSpec((MIX_W, MIX_W), c2)],
        out_specs=pl.BlockSpec((1, CHUNK, MIX_W), fwd),
        out_shape=jax.ShapeDtypeStruct((nb, nt, MIX_W), BF16),
        compiler_params=_params("arbitrary", "arbitrary"),
        name="ret_out",
    )(p, cos, sin, st_f, st_b, perm, dm, qd, ones_bd)


def _gelu_tanh(x):
    return 0.5 * x * (1.0 + jnp.tanh(math.sqrt(2.0 / math.pi) * (x + 0.044715 * (x * x * x))))


def _sgate_kernel(p_ref, ng_ref, w_ref, b_ref, y_ref):
    z = _gelu_tanh(p_ref[0].astype(F32))
    u = z[:, :MIX_W]
    v = z[:, MIX_W:]
    mu = jnp.mean(v, axis=-1, keepdims=True)
    vc = v - mu
    var = jnp.mean(vc * vc, axis=-1, keepdims=True)
    vn = vc * lax.rsqrt(var + EPS) * ng_ref[...]
    mixed = _dot(w_ref[...], _stack_heads(vn)) + b_ref[...]
    y_ref[0] = (u * mixed).astype(BF16)


def _sgate(p, ng, wcat, bias):
    nb, nt, _ = p.shape
    nc = nt // CHUNK
    c2 = lambda b, t: (0, 0)
    return pl.pallas_call(
        _sgate_kernel,
        grid=(nb, nc),
        in_specs=[pl.BlockSpec((1, CHUNK, SG_COLS), lambda b, t: (b, t, P_SG // SG_COLS)),
                  pl.BlockSpec((1, MIX_W), c2),
                  pl.BlockSpec((CHUNK, N_HEADS * CHUNK), c2),
                  pl.BlockSpec((CHUNK, MIX_W), c2)],
        out_specs=pl.BlockSpec((1, CHUNK, MIX_W), lambda b, t: (b, t, 0)),
        out_shape=jax.ShapeDtypeStruct((nb, nt, MIX_W), BF16),
        compiler_params=_params("arbitrary", "arbitrary"),
        name="sgate",
    )(p, ng, wcat, bias)


def _dn_prep_kernel(pc_ref, pp_ref, pn_ref, ab_ref, cw_ref, na_ref, dtb_ref, ones_ref, qkv_ref, gb_ref, xe_ref,
                    *, ncc, nc):
    t = pl.program_id(1)
    w3 = 3 * MIX_W
    prev_ok = jnp.where((t != 0) & (t != ncc), 1.0, 0.0)
    next_ok = jnp.where((t != ncc - 1) & (t != nc - 1), 1.0, 0.0)
    tail = pp_ref[0, CHUNK - 16:CHUNK, 0:w3].astype(F32)
    head = pn_ref[0, 0:16, 0:w3].astype(F32)
    xe_ref[0:8, :] = tail[8:16, :] * prev_ok
    xe_ref[8:8 + CHUNK, :] = pc_ref[0, :, 0:w3].astype(F32)
    xe_ref[8 + CHUNK:16 + CHUNK, :] = head[0:8, :] * next_ok
    pad = CONV_W // 2
    y = xe_ref[8 - pad:8 - pad + CHUNK, :] * cw_ref[0:1, :]
    for i in range(1, CONV_W):
        y = y + xe_ref[8 - pad + i:8 - pad + i + CHUNK, :] * cw_ref[i:i + 1, :]
    y = y * jax.nn.sigmoid(y)
    q = y[:, 0:MIX_W]
    k = y[:, MIX_W:2 * MIX_W]
    v = y[:, 2 * MIX_W:w3]
    ones_bd = ones_ref[...]
    qn = q * lax.rsqrt(_head_sum(q * q, ones_bd) + EPS) * (HEAD_DIM ** -0.5)
    kn = k * lax.rsqrt(_head_sum(k * k, ones_bd) + EPS)
    qkv_ref[0, :, 0:MIX_W] = qn.astype(BF16)
    qkv_ref[0, :, MIX_W:2 * MIX_W] = kn.astype(BF16)
    qkv_ref[0, :, 2 * MIX_W:w3] = v.astype(BF16)
    ab = ab_ref[0]
    a = ab[0:8, :] + dtb_ref[...]
    softplus = jnp.maximum(a, 0.0) + jnp.log1p(jnp.exp(-jnp.abs(a)))
    gb_ref[0, 0:8, :] = na_ref[...] * softplus
    gb_ref[0, 8:16, :] = jax.nn.sigmoid(ab[8:16, :])


def _tri_inverse(n_h, ii, jj):
    eye = jnp.where(ii == jj, 1.0, 0.0)
    nd = jnp.where((ii // 16) == (jj // 16), n_h, 0.0)
    p1 = _mm(nd, nd)
    p2 = _mm(p1, p1)
    p3 = _mm(p2, p2)
    m = eye - nd
    m = m + _mm(m, p1)
    m = m + _mm(m, p2)
    m = m + _mm(m, p3)
    for lvl in (16, 32, 64):
        off = jnp.where(((ii // (2 * lvl)) == (jj // (2 * lvl))) & ((ii // lvl) != (jj // lvl)), n_h, 0.0)
        m = m - _mm(m, _mm(off, m))
    return m


def _dn_direction(qkv, g, beta, st, bd, lower):
    c = CHUNK
    qn = qkv[:, 0:MIX_W]
    kn = qkv[:, MIX_W:2 * MIX_W]
    v = qkv[:, 2 * MIX_W:3 * MIX_W]
    ii = lax.broadcasted_iota(jnp.int32, (c, c), 0)
    jj = lax.broadcasted_iota(jnp.int32, (c, c), 1)
    incl = (ii >= jj) if lower else (ii <= jj)
    tri = jnp.where(incl, 1.0, 0.0)
    eye = jnp.where(ii == jj, 1.0, 0.0)
    g_row = _dot_nt(g, tri, precision=HIGHEST)
    x128 = jnp.concatenate([g, beta, jnp.zeros((c - 2 * N_HEADS, c), F32)], axis=0)
    cols = _dot_nt(jnp.concatenate([tri, eye], axis=0), x128, precision=HIGHEST)
    g_col = cols[0:c, 0:N_HEADS]
    b_col = cols[c:2 * c, N_HEADS:2 * N_HEADS]
    g_cols4 = jnp.concatenate([jnp.broadcast_to(g_col[:, h:h + 1], (c, c)) for h in range(N_HEADS)], axis=1)
    b_cols4 = jnp.concatenate([jnp.broadcast_to(b_col[:, h:h + 1], (c, c)) for h in range(N_HEADS)], axis=1)
    g_rows4 = jnp.concatenate([g_row[h:h + 1, :] for h in range(N_HEADS)], axis=1)
    incl4 = jnp.concatenate([incl] * N_HEADS, axis=1)
    diag4 = jnp.concatenate([ii == jj] * N_HEADS, axis=1)
    decay = jnp.where(incl4, jnp.exp(jnp.where(incl4, g_cols4 - g_rows4, 0.0)), 0.0)
    kstack = _stack_heads(kn)
    kk = _dot_nt(kn, kstack)
    qk = _dot_nt(qn, kstack)
    n_mat = jnp.where(diag4, 0.0, decay * kk * b_cols4)
    attn = decay * qk
    a_inv = jnp.concatenate([_tri_inverse(n_mat[:, h * c:(h + 1) * c], ii, jj) for h in range(N_HEADS)], axis=1)
    g256 = _expand_heads(g_col)
    eg256 = jnp.exp(g256)
    b256 = _expand_heads(b_col)
    vb = v.astype(F32) * b256
    kbg = kn.astype(F32) * b256 * eg256
    rhs = jnp.concatenate([_stack_heads(vb), _stack_heads(kbg)], axis=1)
    uw = _dot(a_inv.astype(BF16), rhs)
    u = uw[:, 0:MIX_W]
    wk = uw[:, MIX_W:2 * MIX_W]
    s_prev = st[...]
    s_bf = s_prev.astype(BF16)
    w = u - _dot(wk.astype(BF16), s_bf)
    o = eg256 * _dot(qn, s_bf) + _dot(attn.astype(BF16), _stack_heads(w))
    g_last = g256[c - 1:c, :] if lower else g256[0:1, :]
    kdec = kn.astype(F32) * jnp.exp(g_last - g256)
    st[...] = jnp.exp(g_last) * s_prev + bd * _dot_tn(kdec.astype(BF16), w.astype(BF16))
    return o


def _dn_scan_kernel(qf_ref, qb_ref, gf_ref, gb_ref, bd_ref, of_ref, ob_ref, st_f, st_b):
    t = pl.program_id(1)

    @pl.when(t == 0)
    def _():
        st_f[...] = jnp.zeros_like(st_f)
        st_b[...] = jnp.zeros_like(st_b)

    bd = bd_ref[...]
    gf = gf_ref[0]
    gb = gb_ref[0]
    of_ref[0] = _dn_direction(qf_ref[0], gf[0:4, :], gf[8:12, :], st_f, bd, True)
    ob_ref[0] = _dn_direction(qb_ref[0], gb[4:8, :], gb[12:16, :], st_b, bd, False)


def _deltanet(p, ab_t, conv_w, neg_a, dtb, bd, ones_bd, ncc):
    nb, nt, _ = p.shape
    nc = nt // CHUNK
    w3 = 3 * MIX_W
    c2 = lambda b, t: (0, 0)
    dn_blk = P_DN // RET_COLS
    qkv, gbeta = pl.pallas_call(
        functools.partial(_dn_prep_kernel, ncc=ncc, nc=nc),
        grid=(nb, nc),
        in_specs=[pl.BlockSpec((1, CHUNK, 4 * MIX_W), lambda b, t: (b, t, dn_blk)),
                  pl.BlockSpec((1, CHUNK, 4 * MIX_W), lambda b, t: (b, jnp.maximum(t - 1, 0), dn_blk)),
                  pl.BlockSpec((1, CHUNK, 4 * MIX_W), lambda b, t: (b, jnp.minimum(t + 1, nc - 1), dn_blk)),
                  pl.BlockSpec((1, 16, CHUNK), lambda b, t: (b, 0, t)),
                  pl.BlockSpec((8, w3), c2),
                  pl.BlockSpec((8, 1), c2),
                  pl.BlockSpec((8, 1), c2),
                  pl.BlockSpec((MIX_W, MIX_W), c2)],
        out_specs=[pl.BlockSpec((1, CHUNK, w3), lambda b, t: (b, t, 0)),
                   pl.BlockSpec((1, 16, CHUNK), lambda b, t: (b, 0, t))],
        out_shape=[jax.ShapeDtypeStruct((nb, nt, w3), BF16),
                   jax.ShapeDtypeStruct((nb, 16, nt), F32)],
        scratch_shapes=[pltpu.VMEM((CHUNK + 16, w3), F32)],
        compiler_params=_params("arbitrary", "arbitrary"),
        name="dn_prep",
    )(p, p, p, ab_t, conv_w, neg_a, dtb, ones_bd)
    fwd = lambda b, t: (b, t, 0)
    bwd = lambda b, t: (b, _bwd_chunk(t, ncc, nc), 0)
    o_shape = jax.ShapeDtypeStruct((nb, nt, MIX_W), F32)
    return pl.pallas_call(
        _dn_scan_kernel,
        grid=(nb, nc),
        in_specs=[pl.BlockSpec((1, CHUNK, w3), fwd),
                  pl.BlockSpec((1, CHUNK, w3), bwd),
                  pl.BlockSpec((1, 16, CHUNK), lambda b, t: (b, 0, t)),
                  pl.BlockSpec((1, 16, CHUNK), lambda b, t: (b, 0, _bwd_chunk(t, ncc, nc))),
                  pl.BlockSpec((MIX_W, MIX_W), c2)],
        out_specs=[pl.BlockSpec((1, CHUNK, MIX_W), fwd),
                   pl.BlockSpec((1, CHUNK, MIX_W), bwd)],
        out_shape=[o_shape, o_shape],
        scratch_shapes=[pltpu.VMEM((MIX_W, MIX_W), F32), pltpu.VMEM((MIX_W, MIX_W), F32)],
        compiler_params=_params("arbitrary", "arbitrary"),
        name="dn_scan",
    )(qkv, qkv, gbeta, gbeta, bd)


QK_W = 256


def _mla_prep_kernel(p_ref, c_ref, s_ref, perm_ref, qg_ref, kg_ref, wqn_ref, wqr_ref, wa_ref, selq_ref, selc_ref,
                     selr_ref, q_ref, kv_ref, *, scale):
    p = p_ref[0]
    cos, sin, perm = c_ref[...], s_ref[...], perm_ref[...]
    cq = p[:, 0:Q_LORA].astype(F32)
    cqn = (cq * lax.rsqrt(jnp.mean(cq * cq, axis=-1, keepdims=True) + EPS) * qg_ref[...]).astype(BF16)
    q_nope = _dot(cqn, wqn_ref[...]).astype(BF16)
    q_rope = _dot(cqn, wqr_ref[...]).astype(BF16)
    q_rot = (_rot(q_rope, cos, sin, perm) * scale).astype(BF16)
    q_nope_s = (q_nope.astype(F32) * scale).astype(BF16)
    for h in range(N_HEADS):
        q_ref[0, h] = (_dot(q_nope_s, wa_ref[h]) + _dot(q_rot, selq_ref[h])).astype(BF16)
    ckv = p[:, Q_LORA:Q_LORA + KV_LORA].astype(F32)
    ckvn = (ckv * lax.rsqrt(jnp.mean(ckv * ckv, axis=-1, keepdims=True) + EPS) * kg_ref[...]).astype(BF16)
    kr = p[:, Q_LORA + KV_LORA:MLA_PAD]
    kr_rot = _rot(kr, cos, sin, perm).astype(BF16)
    kv_ref[0] = (_dot(ckvn, selc_ref[...]) + _dot(kr_rot, selr_ref[...])).astype(BF16)


def _mla_attn_kernel(q_ref, kv_ref, wuv_ref, y_ref, m_ref, l_ref, acc_ref, *, tq, tk, n_ctx, nt):
    i = pl.program_id(1)
    rows = N_HEADS * tq
    q = q_ref[0].reshape(rows, QK_W)
    m_ref[...] = jnp.full_like(m_ref, -jnp.inf)
    l_ref[...] = jnp.zeros_like(l_ref)
    acc_ref[...] = jnp.zeros_like(acc_ref)
    n_kv = jnp.where((i + 1) * tq <= n_ctx, n_ctx // tk, nt // tk)

    def body(j, carry):
        kv = kv_ref[0, pl.ds(pl.multiple_of(j * tk, tk), tk), :]
        s = _dot_nt(q, kv)
        m_old = m_ref[...]
        m_new = jnp.maximum(m_old, jnp.max(s, axis=-1, keepdims=True))
        alpha = jnp.exp(m_old - m_new)
        pr = jnp.exp(s - m_new)
        l_ref[...] = alpha * l_ref[...] + jnp.sum(pr, axis=-1, keepdims=True)
        acc_ref[...] = alpha * acc_ref[...] + _dot(pr.astype(BF16), kv)
        m_ref[...] = m_new
        return carry

    lax.fori_loop(0, n_kv, body, 0)
    o = (acc_ref[...] / l_ref[...]).astype(BF16)
    y = _dot(o[0:tq, 0:KV_LORA], wuv_ref[0])
    for h in range(1, N_HEADS):
        y = y + _dot(o[h * tq:(h + 1) * tq, 0:KV_LORA], wuv_ref[h])
    y_ref[0] = y.astype(BF16)


def _mla(p, cos, sin, perm, qg, kg, wqn, wqr, wa, selq, selc, selr, wuv, n_ctx):
    nb, nt, _ = p.shape
    tm = _pick(nt, (768, 384, 256, 128))
    scale = (NOPE_DIM + ROPE_DIM) ** -0.5
    c2 = lambda b, i: (0, 0)
    c3 = lambda b, i: (0, 0, 0)
    q, kv = pl.pallas_call(
        functools.partial(_mla_prep_kernel, scale=scale),
        grid=(nb, nt // tm),
        in_specs=[pl.BlockSpec((1, tm, MLA_PAD), lambda b, i: (b, i, P_MLA // MLA_PAD)),
                  pl.BlockSpec((tm, 128), lambda b, i: (i, 0)),
                  pl.BlockSpec((tm, 128), lambda b, i: (i, 0)),
                  pl.BlockSpec((128, 128), c2),
                  pl.BlockSpec((1, Q_LORA), c2),
                  pl.BlockSpec((1, KV_LORA), c2),
                  pl.BlockSpec((Q_LORA, N_HEADS * NOPE_DIM), c2),
                  pl.BlockSpec((Q_LORA, N_HEADS * ROPE_DIM), c2),
                  pl.BlockSpec((N_HEADS, N_HEADS * NOPE_DIM, QK_W), c3),
                  pl.BlockSpec((N_HEADS, N_HEADS * ROPE_DIM, QK_W), c3),
                  pl.BlockSpec((KV_LORA, QK_W), c2),
                  pl.BlockSpec((128, QK_W), c2)],
        out_specs=[pl.BlockSpec((1, N_HEADS, tm, QK_W), lambda b, i: (b, 0, i, 0)),
                   pl.BlockSpec((1, tm, QK_W), lambda b, i: (b, i, 0))],
        out_shape=[jax.ShapeDtypeStruct((nb, N_HEADS, nt, QK_W), BF16),
                   jax.ShapeDtypeStruct((nb, nt, QK_W), BF16)],
        compiler_params=_params("arbitrary", "arbitrary"),
        name="mla_prep",
    )(p, cos, sin, perm, qg, kg, wqn, wqr, wa, selq, selc, selr)
    tq = 256
    tk = 256
    rows = N_HEADS * tq
    return pl.pallas_call(
        functools.partial(_mla_attn_kernel, tq=tq, tk=tk, n_ctx=n_ctx, nt=nt),
        grid=(nb, nt // tq),
        in_specs=[pl.BlockSpec((1, N_HEADS, tq, QK_W), lambda b, i: (b, 0, i, 0)),
                  pl.BlockSpec((1, nt, QK_W), lambda b, i: (b, 0, 0)),
                  pl.BlockSpec((N_HEADS, KV_LORA, MIX_W), c3)],
        out_specs=pl.BlockSpec((1, tq, MIX_W), lambda b, i: (b, i, 0)),
        out_shape=jax.ShapeDtypeStruct((nb, nt, MIX_W), BF16),
        scratch_shapes=[pltpu.VMEM((rows, 1), F32), pltpu.VMEM((rows, 1), F32), pltpu.VMEM((rows, QK_W), F32)],
        compiler_params=_params("arbitrary", "arbitrary"),
        name="mla_attn",
    )(q, kv, wuv)


def _merge_kernel(x_ref, yr_ref, ys_ref, of_ref, ob_ref, ym_ref, z_ref, g0_ref, g1_ref, g2_ref, g3_ref,
                  wb_ref, wo_ref, ng_ref, gp_ref, ml_ref, mc_ref, ones_ref, o_ref, *, tm, n_ctx):
    i = pl.program_id(1)
    od = of_ref[0] + ob_ref[0]
    ms = _head_sum(od * od, ones_ref[...]) * (1.0 / HEAD_DIM)
    z = z_ref[0].astype(F32)
    ydn = (od * lax.rsqrt(ms + EPS) * ng_ref[...]) * (z * jax.nn.sigmoid(z))
    ys = (yr_ref[0], ys_ref[0], ydn.astype(BF16), ym_ref[0])
    gates = (g0_ref, g1_ref, g2_ref, g3_ref)
    acc = None
    for b in range(N_BRANCH):
        term = jax.nn.sigmoid(gates[b][0].astype(F32)) * _dot(ys[b], wb_ref[b])
        acc = term if acc is None else acc + term
    y = _dot(acc.astype(BF16), wo_ref[...])
    r = y * lax.rsqrt(jnp.mean(y * y, axis=-1, keepdims=True) + EPS) * gp_ref[...]
    rows = lax.broadcasted_iota(jnp.int32, (tm, 1), 0) + i * tm
    gate = jnp.where(rows < n_ctx, mc_ref[0, 2:3, :], ml_ref[0, 2:3, :])
    o_ref[0] = x_ref[0] + gate * r


def _merge(xa, y_ret, y_sg, o_f, o_b, y_mla, p, wb, wo, ng, gp, mod, ones_bd, n_ctx):
    nb, nt, d = xa.shape
    tm = _pick(nt, (768, 384, 256, 128))
    row = lambda b, i: (b, i, 0)
    c2 = lambda b, i: (0, 0)
    y_spec = pl.BlockSpec((1, tm, MIX_W), row)
    gate_specs = [pl.BlockSpec((1, tm, d), functools.partial(lambda b, i, k: (b, i, k), k=P_GATE // d + k))
                  for k in range(N_BRANCH)]
    return pl.pallas_call(
        functools.partial(_merge_kernel, tm=tm, n_ctx=n_ctx),
        grid=(nb, nt // tm),
        in_specs=[pl.BlockSpec((1, tm, d), row), y_spec, y_spec, y_spec, y_spec, y_spec,
                  pl.BlockSpec((1, tm, MIX_W), lambda b, i: (b, i, (P_DN + 3 * MIX_W) // MIX_W)),
                  *gate_specs,
                  pl.BlockSpec((N_BRANCH, MIX_W, d), lambda b, i: (0, 0, 0)),
                  pl.BlockSpec((d, d), c2),
                  pl.BlockSpec((1, MIX_W), c2),
                  pl.BlockSpec((1, d), c2),
                  pl.BlockSpec((1, 6, d), lambda b, i: (b, 0, 0)),
                  pl.BlockSpec((1, 6, d), lambda b, i: (nb, 0, 0)),
                  pl.BlockSpec((MIX_W, MIX_W), c2)],
        out_specs=pl.BlockSpec((1, tm, d), row),
        out_shape=jax.ShapeDtypeStruct((nb, nt, d), F32),
        compiler_params=_params("arbitrary", "arbitrary"),
        name="merge",
    )(xa, y_ret, y_sg, o_f, o_b, y_mla, p, p, p, p, p, wb, wo, ng, gp, mod, mod, ones_bd)


def _route(sel, aff):
    rows = [sel[e:e + 1, :] for e in range(N_EXPERTS)]
    pairs = [(a, b) for a in range(EXPERTS_PER_GROUP) for b in range(a + 1, EXPERTS_PER_GROUP)]
    grp_score, grp_pair = [], []
    for g in range(N_GROUPS):
        base = g * EXPERTS_PER_GROUP
        best = rows[base + pairs[0][0]] + rows[base + pairs[0][1]]
        best_p = jnp.zeros_like(best, dtype=jnp.int32)
        for pi in range(1, len(pairs)):
            s = rows[base + pairs[pi][0]] + rows[base + pairs[pi][1]]
            take = s > best
            best = jnp.where(take, s, best)
            best_p = jnp.where(take, pi, best_p)
        grp_score.append(best)
        grp_pair.append(best_p)
    top = grp_score[0]
    top_g = jnp.zeros_like(grp_pair[0])
    top_p = grp_pair[0]
    for g in range(1, N_GROUPS):
        take = grp_score[g] > top
        top = jnp.where(take, grp_score[g], top)
        top_g = jnp.where(take, g, top_g)
        top_p = jnp.where(take, grp_pair[g], top_p)
    picked = []
    for e in range(N_EXPERTS):
        g, k = divmod(e, EXPERTS_PER_GROUP)
        in_pair = None
        for pi, (a, b) in enumerate(pairs):
            if k in (a, b):
                hit = top_p == pi
                in_pair = hit if in_pair is None else (in_pair | hit)
        picked.append(jnp.where((top_g == g) & in_pair, aff[e:e + 1, :], 0.0))
    denom = picked[0]
    for e in range(1, N_EXPERTS):
        denom = denom + picked[e]
    return [pk / denom for pk in picked]


def _moe_kernel(x_ref, ml_ref, mc_ref, g2_ref, gp_ref, rw_ref, rb_ref, w1_ref, w3_ref, w2_ref, o_ref,
                hn_ref, comb_t_ref, comb_ref, acc_ref, *, tm, rb, n_ctx):
    i = pl.program_id(1)
    e = pl.program_id(2)

    @pl.when(e == 0)
    def _():
        def blk(r, carry):
            r0 = pl.multiple_of(r * rb, rb)
            x = x_ref[0, pl.ds(r0, rb), :]
            rows = lax.broadcasted_iota(jnp.int32, (rb, 1), 0) + (i * tm + r0)
            hn = _norm_modulate(x, g2_ref[...], rows < n_ctx, mc_ref, ml_ref, 3, 4)
            hn_ref[pl.ds(r0, rb), :] = hn.astype(BF16)
            logit = _dot_nt(rw_ref[...], hn.astype(BF16))
            aff = jax.nn.sigmoid(logit)
            comb = _route(aff + rb_ref[...], aff)
            for k in range(N_EXPERTS):
                comb_t_ref[k:k + 1, pl.ds(r0, rb)] = comb[k]
            return carry

        comb_t_ref[...] = jnp.zeros_like(comb_t_ref)
        lax.fori_loop(0, tm // rb, blk, 0)
        comb_t_ref[N_EXPERTS:N_EXPERTS + 1, :] = jnp.ones((1, tm), F32)
        comb_ref[...] = comb_t_ref[...].T
        acc_ref[...] = jnp.zeros_like(acc_ref)

    hn = hn_ref[...]
    a = _dot(hn, w1_ref[0])
    h = (a * jax.nn.sigmoid(a)) * _dot(hn, w3_ref[0])
    lane = lax.broadcasted_iota(jnp.int32, (1, 128), 1)
    ce = jnp.sum(jnp.where(lane == e, comb_ref[...], 0.0), axis=-1, keepdims=True)
    acc_ref[...] += _dot((h * ce).astype(BF16), w2_ref[0])

    @pl.when(e == pl.num_programs(2) - 1)
    def _():
        y = acc_ref[...]
        r = y * lax.rsqrt(jnp.mean(y * y, axis=-1, keepdims=True) + EPS) * gp_ref[...]
        rows = lax.broadcasted_iota(jnp.int32, (tm, 1), 0) + i * tm
        gate = jnp.where(rows < n_ctx, mc_ref[0, 5:6, :], ml_ref[0, 5:6, :])
        o_ref[0] = x_ref[0] + gate * r


def _moe(xa, mod, g2, gp, rw_t, rbias, w1, w3, w2, n_ctx):
    nb, nt, d = xa.shape
    tm = _pick(nt, (768, 384, 256, 128))
    ne = w1.shape[0]
    row = lambda b, i, e: (b, i, 0)
    c2 = lambda b, i, e: (0, 0)
    return pl.pallas_call(
        functools.partial(_moe_kernel, tm=tm, rb=128, n_ctx=n_ctx),
        grid=(nb, nt // tm, ne),
        in_specs=[pl.BlockSpec((1, tm, d), row),
                  pl.BlockSpec((1, 6, d), lambda b, i, e: (b, 0, 0)),
                  pl.BlockSpec((1, 6, d), lambda b, i, e: (nb, 0, 0)),
                  pl.BlockSpec((1, d), c2),
                  pl.BlockSpec((1, d), c2),
                  pl.BlockSpec((N_EXPERTS, d), c2),
                  pl.BlockSpec((N_EXPERTS, 1), c2),
                  pl.BlockSpec((1, d, D_EXPERT), lambda b, i, e: (e, 0, 0)),
                  pl.BlockSpec((1, d, D_EXPERT), lambda b, i, e: (e, 0, 0)),
                  pl.BlockSpec((1, D_EXPERT, d), lambda b, i, e: (e, 0, 0))],
        out_specs=pl.BlockSpec((1, tm, d), row),
        out_shape=jax.ShapeDtypeStruct((nb, nt, d), F32),
        scratch_shapes=[pltpu.VMEM((tm, d), BF16), pltpu.VMEM((128, tm), F32), pltpu.VMEM((tm, 128), F32),
                        pltpu.VMEM((tm, d), F32)],
        compiler_params=_params("arbitrary", "arbitrary", "arbitrary"),
        name="moe",
    )(xa, mod, mod, g2, gp, rw_t, rbias, w1, w3, w2)


def _swap_perm(width, group):
    j = jnp.arange(width)
    src = jnp.where((j % group) < group // 2, j + group // 2, j - group // 2)
    return (jnp.arange(width)[:, None] == src[None, :]).astype(BF16)


def _rope_tables(n_lat, n_ctx):
    def angles(pos, dim):
        half = dim // 2
        inv = ROPE_BASE ** (-jnp.arange(half, dtype=F32) / half)
        return pos.astype(F32)[:, None] * inv[None, :]

    def tables(cos_parts, sin_parts, reps):
        cos = jnp.tile(jnp.concatenate(cos_parts, axis=-1), (1, reps))
        sin = jnp.tile(jnp.concatenate(sin_parts, axis=-1), (1, reps))
        w = cos.shape[1]
        return (jnp.concatenate([jnp.ones((n_ctx, w), F32), cos], axis=0),
                jnp.concatenate([jnp.zeros((n_ctx, w), F32), sin], axis=0))

    rows = n_lat // GRID_W
    ang_t = angles(jnp.arange(n_lat), HEAD_DIM)
    ang_r = angles(jnp.repeat(jnp.arange(rows), GRID_W), ROPE_DIM // 2)
    ang_c = angles(jnp.tile(jnp.arange(GRID_W), rows), ROPE_DIM // 2)
    ct, st = jnp.cos(ang_t), jnp.sin(ang_t)
    ret = tables([ct, ct], [-st, st], N_HEADS)
    cr, sr, cc, sc = jnp.cos(ang_r), jnp.sin(ang_r), jnp.cos(ang_c), jnp.sin(ang_c)
    mla = tables([cr, cr, cc, cc], [-sr, sr, -sc, sc], N_HEADS)
    return ret, mla


def _ret_tables(logit):
    log_g = jax.nn.log_sigmoid(logit.astype(F32))
    lane_lg = jnp.repeat(log_g, HEAD_DIM, axis=1)
    idx = jnp.arange(CHUNK, dtype=F32)[:, None]
    kd = jnp.stack([jnp.exp(lane_lg[0][None, :] * (CHUNK - 1 - idx)), jnp.exp(lane_lg[1][None, :] * idx)])
    qd = jnp.stack([jnp.exp(lane_lg[0][None, :] * (idx + 1)), jnp.exp(lane_lg[1][None, :] * (CHUNK - idx))])
    cd = jnp.exp(lane_lg * CHUNK)[:, None, :]
    diff = idx - idx.T
    blocks = []
    for h in range(N_HEADS):
        f = jnp.exp(log_g[0, h] * jnp.where(diff >= 0, diff, 0.0))
        b = jnp.exp(log_g[1, h] * jnp.where(diff < 0, -diff, 0.0))
        blocks.append(jnp.where(diff >= 0, f, b))
    dm = jnp.concatenate(blocks, axis=1)
    return kd, cd, qd, dm


def _pack_w_in(w_in):
    d = w_in.shape[0]
    mla = jnp.concatenate([w_in[:, OFF_MLA:OFF_MLA + MLA_COLS], jnp.zeros((d, MLA_PAD - MLA_COLS), w_in.dtype)], 1)
    w = jnp.concatenate([w_in[:, OFF_RET:OFF_RET + RET_COLS], w_in[:, OFF_DN:OFF_DN + 4 * MIX_W],
                         w_in[:, OFF_SG:OFF_SG + SG_COLS], mla, w_in[:, OFF_GATE:OFF_GATE + GATE_COLS]], axis=1)
    wab = w_in[:, OFF_DN + 4 * MIX_W:OFF_DN + DN_COLS].T
    return w.astype(BF16), wab.astype(BF16)


def _mla_weights(w_uq, w_ukv):
    dq = NOPE_DIM + ROPE_DIM
    dkv = NOPE_DIM + V_DIM
    wq = w_uq.reshape(Q_LORA, N_HEADS, dq)
    wqn = wq[:, :, :NOPE_DIM].reshape(Q_LORA, N_HEADS * NOPE_DIM)
    wqr = wq[:, :, NOPE_DIM:].reshape(Q_LORA, N_HEADS * ROPE_DIM)
    wkv = w_ukv.reshape(KV_LORA, N_HEADS, dkv)
    wa = jnp.zeros((N_HEADS, N_HEADS * NOPE_DIM, QK_W), F32)
    selq = jnp.zeros((N_HEADS, N_HEADS * ROPE_DIM, QK_W), F32)
    wuv = jnp.zeros((N_HEADS, KV_LORA, MIX_W), F32)
    for h in range(N_HEADS):
        wa = wa.at[h, h * NOPE_DIM:(h + 1) * NOPE_DIM, 0:KV_LORA].set(wkv[:, h, :NOPE_DIM].T)
        selq = selq.at[h, h * ROPE_DIM:(h + 1) * ROPE_DIM, KV_LORA:KV_LORA + ROPE_DIM].set(jnp.eye(ROPE_DIM))
        wuv = wuv.at[h, :, h * V_DIM:(h + 1) * V_DIM].set(wkv[:, h, NOPE_DIM:])
    selc = jnp.zeros((KV_LORA, QK_W), F32).at[:, 0:KV_LORA].set(jnp.eye(KV_LORA))
    selr = jnp.zeros((128, QK_W), F32).at[0:ROPE_DIM, KV_LORA:KV_LORA + ROPE_DIM].set(jnp.eye(ROPE_DIM))
    return tuple(a.astype(BF16) for a in (wqn, wqr, wa, selq, selc, selr, wuv))


def kernel(x, c, ctx, c_ctx, w_ada, b_ada, g_pre1, g_post1, g_pre2, g_post2, w_in, ret_decay_logit, sg_norm_g, sg_w, sg_b, dn_conv_w, dn_A_log, dn_dt_bias, dn_norm_g, mla_q_norm_g, mla_kv_norm_g, mla_w_uq, mla_w_ukv, w_branch, w_out, router_w, router_bias, moe_w1, moe_w3, moe_w2, shared_w1, shared_w3, shared_w2):
    nb, n_lat, d = x.shape
    n_ctx = ctx.shape[1]
    depth = w_in.shape[0]
    assert d == D_MODEL and n_lat % GRID_W == 0 and n_lat % CHUNK == 0 and n_ctx % 256 == 0
    ncc = n_ctx // CHUNK

    n_cond = -(-(nb + 1) // 8) * 8
    cond = jnp.concatenate([c, c_ctx[None], jnp.zeros((n_cond - nb - 1, d), F32)], axis=0)
    mod_all = _adaln(cond, w_ada, b_ada).reshape(depth, n_cond, 6, d)

    (ret_cos, ret_sin), (mla_cos, mla_sin) = _rope_tables(n_lat, n_ctx)
    perm_ret = _swap_perm(MIX_W, HEAD_DIM)
    perm_mla = _swap_perm(N_HEADS * ROPE_DIM, ROPE_DIM // 2)
    lane_head = jnp.arange(MIX_W) // HEAD_DIM
    bd = (lane_head[:, None] == lane_head[None, :]).astype(F32)
    ones_bd = bd.astype(BF16)
    rw_t = router_w.T.astype(BF16)
    rbias = router_bias.astype(F32)[:, None]

    xa = jnp.concatenate([ctx, x], axis=1)
    for l in range(depth):
        mod = mod_all[l]
        w_l, wab_l = _pack_w_in(w_in[l])
        p, ab_t = _inproj(xa, mod, g_pre1[l][None], w_l, wab_l, n_ctx)

        kd, cd, qd, dm = _ret_tables(ret_decay_logit[l])
        y_ret = _retention(p, ret_cos, ret_sin, perm_ret, (kd, cd, qd, dm, bd, ones_bd), ncc)

        wcat = jnp.concatenate([sg_w[l, h] for h in range(N_HEADS)], axis=1).astype(BF16)
        sg_bias = jnp.repeat(sg_b[l].T, HEAD_DIM, axis=1)
        y_sg = _sgate(p, sg_norm_g[l][None], wcat, sg_bias)

        neg_a = (-jnp.exp(dn_A_log[l].astype(F32))).reshape(2 * N_HEADS, 1)
        dtb = dn_dt_bias[l].astype(F32).reshape(2 * N_HEADS, 1)
        conv_w = jnp.concatenate([dn_conv_w[l], jnp.zeros((8 - CONV_W, 3 * MIX_W), F32)], axis=0)
        o_f, o_b = _deltanet(p, ab_t, conv_w, neg_a, dtb, bd, ones_bd, ncc)

        y_mla = _mla(p, mla_cos, mla_sin, perm_mla, mla_q_norm_g[l][None], mla_kv_norm_g[l][None],
                     *_mla_weights(mla_w_uq[l], mla_w_ukv[l]), n_ctx)

        xa = _merge(xa, y_ret, y_sg, o_f, o_b, y_mla, p, w_branch[l].astype(BF16), w_out[l].astype(BF16),
                    jnp.tile(dn_norm_g[l], N_HEADS)[None], g_post1[l][None], mod, ones_bd, n_ctx)

        w1 = jnp.concatenate([moe_w1[l], shared_w1[l][None]], axis=0).astype(BF16)
        w3 = jnp.concatenate([moe_w3[l], shared_w3[l][None]], axis=0).astype(BF16)
        w2 = jnp.concatenate([moe_w2[l], shared_w2[l][None]], axis=0).astype(BF16)
        xa = _moe(xa, mod, g_pre2[l][None], g_post2[l][None], rw_t, rbias, w1, w3, w2, n_ctx)
    return xa[:, n_ctx:]
```

```python
import functools
import math

import jax
import jax.numpy as jnp
from jax import lax
from jax.experimental import pallas as pl
from jax.experimental.pallas import tpu as pltpu

F32 = jnp.float32
BF16 = jnp.bfloat16
HIGHEST = lax.Precision.HIGHEST

D_MODEL = 1024
GRID_W = 64
N_HEADS = 4
HEAD_DIM = 64
MIX_W = N_HEADS * HEAD_DIM
CHUNK = 128
ROPE_BASE = 10000.0
EPS = 1e-6
RET_DECAY_EXP0 = 5.0
CONV_W = 5
Q_LORA = 256
KV_LORA = 128
NOPE_DIM = 64
ROPE_DIM = 32
V_DIM = 64
N_EXPERTS = 16
N_GROUPS = 4
EXPERTS_PER_GROUP = N_EXPERTS // N_GROUPS
D_EXPERT = 256
N_BRANCH = 4

RET_COLS = 4 * MIX_W
SG_COLS = 2 * MIX_W
DN_COLS = 4 * MIX_W + 4 * N_HEADS
MLA_COLS = Q_LORA + KV_LORA + ROPE_DIM
GATE_COLS = N_BRANCH * D_MODEL
OFF_RET = 0
OFF_SG = OFF_RET + RET_COLS
OFF_DN = OFF_SG + SG_COLS
OFF_MLA = OFF_DN + DN_COLS
OFF_GATE = OFF_MLA + MLA_COLS

P_RET = 0
P_DN = 1024
P_SG = 2048
P_MLA = 2560
P_GATE = 3072
P_COLS = 7168
MLA_PAD = 512

VMEM_LIMIT = 56 * 1024 * 1024


def _dot(a, b, precision=None):
    return jnp.dot(a, b, preferred_element_type=F32, precision=precision)


def _dot_nt(a, b, precision=None):
    return lax.dot_general(a, b, (((1,), (1,)), ((), ())), preferred_element_type=F32, precision=precision)


def _dot_tn(a, b):
    return lax.dot_general(a, b, (((0,), (0,)), ((), ())), preferred_element_type=F32)


def _mm(a, b):
    return _dot(a.astype(BF16), b.astype(BF16))


def _params(*sem):
    return pltpu.CompilerParams(dimension_semantics=sem, vmem_limit_bytes=VMEM_LIMIT)


def _pick(n, cands):
    for c in cands:
        if n % c == 0:
            return c
    raise ValueError(f"no tile for {n}")


def _head_of_lane(width, group):
    return lax.broadcasted_iota(jnp.int32, (1, width), 1) // group


def _stack_heads(x):
    head = _head_of_lane(MIX_W, HEAD_DIM)
    xf = x.astype(F32)
    return jnp.concatenate([jnp.where(head == h, xf, 0.0).astype(BF16) for h in range(N_HEADS)], axis=0)


def _expand_heads(cols):
    head = _head_of_lane(MIX_W, HEAD_DIM)
    out = cols[:, N_HEADS - 1:N_HEADS]
    for h in range(N_HEADS - 2, -1, -1):
        out = jnp.where(head <= h, cols[:, h:h + 1], out)
    return out


def _head_sum(x, ones_bd):
    hi = x.astype(BF16)
    lo = (x - hi.astype(F32)).astype(BF16)
    return _dot(hi, ones_bd) + _dot(lo, ones_bd)


def _rot(x_bf, cos, sin, perm):
    return x_bf.astype(F32) * cos + _dot(x_bf, perm) * sin


def _norm_modulate(x, g, isc, mc_ref, ml_ref, shift_row, scale_row):
    h = x * lax.rsqrt(jnp.mean(x * x, axis=-1, keepdims=True) + EPS) * g
    shift = jnp.where(isc, mc_ref[0, shift_row:shift_row + 1, :], ml_ref[0, shift_row:shift_row + 1, :])
    scale = jnp.where(isc, mc_ref[0, scale_row:scale_row + 1, :], ml_ref[0, scale_row:scale_row + 1, :])
    return h * (1.0 + scale) + shift


def _adaln_kernel(c_ref, w_ref, b_ref, o_ref):
    c = c_ref[...]
    s = c * jax.nn.sigmoid(c)
    o_ref[0] = _dot(s, w_ref[0], precision=HIGHEST) + b_ref[0]


def _adaln(cond, w_ada, b_ada):
    n_l, d, d6 = w_ada.shape
    r = cond.shape[0]
    tn = 1024
    return pl.pallas_call(
        _adaln_kernel,
        grid=(n_l, d6 // tn),
        in_specs=[pl.BlockSpec((r, d), lambda l, j: (0, 0)),
                  pl.BlockSpec((1, d, tn), lambda l, j: (l, 0, j)),
                  pl.BlockSpec((1, 1, tn), lambda l, j: (l, 0, j))],
        out_specs=pl.BlockSpec((1, r, tn), lambda l, j: (l, 0, j)),
        out_shape=jax.ShapeDtypeStruct((n_l, r, d6), F32),
        compiler_params=_params("arbitrary", "arbitrary"),
        name="adaln",
    )(cond, w_ada, b_ada.reshape(n_l, 1, d6))


def _inproj_kernel(x_ref, ml_ref, mc_ref, g_ref, w_ref, wab_ref, p_ref, ab_ref, xn_ref, *, tm, rb, n_ctx):
    i = pl.program_id(1)
    j = pl.program_id(2)

    @pl.when(j == 0)
    def _():
        def blk(r, carry):
            r0 = pl.multiple_of(r * rb, rb)
            x = x_ref[0, pl.ds(r0, rb), :]
            rows = lax.broadcasted_iota(jnp.int32, (rb, 1), 0) + (i * tm + r0)
            hn = _norm_modulate(x, g_ref[...], rows < n_ctx, mc_ref, ml_ref, 0, 1)
            xn_ref[pl.ds(r0, rb), :] = hn.astype(BF16)
            return carry

        lax.fori_loop(0, tm // rb, blk, 0)
        ab_ref[0] = _dot_nt(wab_ref[...], xn_ref[...])

    p_ref[0] = _dot(xn_ref[...], w_ref[...]).astype(BF16)


def _inproj(xa, mod, g, w, wab, n_ctx):
    nb, nt, d = xa.shape
    tm = _pick(nt, (1408, 768, 384, 256, 128))
    tn = 1792
    kern = functools.partial(_inproj_kernel, tm=tm, rb=128, n_ctx=n_ctx)
    return pl.pallas_call(
        kern,
        grid=(nb, nt // tm, P_COLS // tn),
        in_specs=[pl.BlockSpec((1, tm, d), lambda b, i, j: (b, i, 0)),
                  pl.BlockSpec((1, 6, d), lambda b, i, j: (b, 0, 0)),
                  pl.BlockSpec((1, 6, d), lambda b, i, j: (nb, 0, 0)),
                  pl.BlockSpec((1, d), lambda b, i, j: (0, 0)),
                  pl.BlockSpec((d, tn), lambda b, i, j: (0, j)),
                  pl.BlockSpec((16, d), lambda b, i, j: (0, 0))],
        out_specs=[pl.BlockSpec((1, tm, tn), lambda b, i, j: (b, i, j)),
                   pl.BlockSpec((1, 16, tm), lambda b, i, j: (b, 0, i))],
        out_shape=[jax.ShapeDtypeStruct((nb, nt, P_COLS), BF16),
                   jax.ShapeDtypeStruct((nb, 16, nt), F32)],
        scratch_shapes=[pltpu.VMEM((tm, d), BF16)],
        compiler_params=_params("arbitrary", "arbitrary", "arbitrary"),
        name="inproj",
    )(xa, mod, mod, g, w, wab)


def _bwd_chunk(t, ncc, nc):
    return jnp.where(t < ncc, ncc - 1 - t, nc - 1 - (t - ncc))


def _ret_state_kernel(pf_ref, pb_ref, cf_ref, sf_ref, cb_ref, sb_ref, perm_ref, kd_ref, cd_ref, bd_ref,
                      of_ref, ob_ref, st_f, st_b):
    t = pl.program_id(1)

    @pl.when(t == 0)
    def _():
        st_f[...] = jnp.zeros_like(st_f)
        st_b[...] = jnp.zeros_like(st_b)

    def upd(p_ref, c_ref, s_ref, d, st, o_ref):
        k = p_ref[0, :, MIX_W:2 * MIX_W]
        v = p_ref[0, :, 2 * MIX_W:3 * MIX_W]
        kr = _rot(k, c_ref[...], s_ref[...], perm_ref[...]) * (HEAD_DIM ** -0.5)
        o_ref[0, 0] = st[...].astype(BF16)
        inc = _dot_tn((kr * kd_ref[d]).astype(BF16), v)
        st[...] = cd_ref[d] * st[...] + bd_ref[...] * inc

    upd(pf_ref, cf_ref, sf_ref, 0, st_f, of_ref)
    upd(pb_ref, cb_ref, sb_ref, 1, st_b, ob_ref)


def _ret_out_kernel(p_ref, c_ref, s_ref, sf_ref, sb_ref, perm_ref, dm_ref, qd_ref, ones_ref, y_ref):
    p = p_ref[0]
    q = p[:, 0:MIX_W]
    k = p[:, MIX_W:2 * MIX_W]
    v = p[:, 2 * MIX_W:3 * MIX_W]
    g = p[:, 3 * MIX_W:4 * MIX_W].astype(F32)
    cos, sin, perm = c_ref[...], s_ref[...], perm_ref[...]
    qr = _rot(q, cos, sin, perm)
    kr = _rot(k, cos, sin, perm) * (HEAD_DIM ** -0.5)
    sc = _dot_nt(qr.astype(BF16), _stack_heads(kr)) * dm_ref[...]
    o = _dot(sc.astype(BF16), _stack_heads(v))
    qs = jnp.concatenate([(qr * qd_ref[0]).astype(BF16), (qr * qd_ref[1]).astype(BF16)], axis=1)
    ss = jnp.concatenate([sf_ref[0, 0], sb_ref[0, 0]], axis=0)
    o = o + _dot(qs, ss)
    ones_bd = ones_ref[...]
    mu = _head_sum(o, ones_bd) * (1.0 / HEAD_DIM)
    oc = o - mu
    var = _head_sum(oc * oc, ones_bd) * (1.0 / HEAD_DIM)
    y = oc * lax.rsqrt(var + EPS)
    y_ref[0] = (y * (g * jax.nn.sigmoid(g))).astype(BF16)


def _retention(p, cos, sin, perm, tabs, ncc):
    nb, nt, _ = p.shape
    nc = nt // CHUNK
    kd, cd, qd, dm, bd, ones_bd = tabs
    fwd = lambda b, t: (b, t, 0)
    bwd = lambda b, t: (b, _bwd_chunk(t, ncc, nc), 0)
    tab_f = lambda b, t: (t, 0)
    tab_b = lambda b, t: (_bwd_chunk(t, ncc, nc), 0)
    c2 = lambda b, t: (0, 0)
    c3 = lambda b, t: (0, 0, 0)
    st_shape = jax.ShapeDtypeStruct((nb, nc, MIX_W, MIX_W), BF16)
    st_f, st_b = pl.pallas_call(
        _ret_state_kernel,
        grid=(nb, nc),
        in_specs=[pl.BlockSpec((1, CHUNK, RET_COLS), fwd),
                  pl.BlockSpec((1, CHUNK, RET_COLS), bwd),
                  pl.BlockSpec((CHUNK, MIX_W), tab_f), pl.BlockSpec((CHUNK, MIX_W), tab_f),
                  pl.BlockSpec((CHUNK, MIX_W), tab_b), pl.BlockSpec((CHUNK, MIX_W), tab_b),
                  pl.BlockSpec((MIX_W, MIX_W), c2),
                  pl.BlockSpec((2, CHUNK, MIX_W), c3),
                  pl.BlockSpec((2, 1, MIX_W), c3),
                  pl.BlockSpec((MIX_W, MIX_W), c2)],
        out_specs=[pl.BlockSpec((1, 1, MIX_W, MIX_W), lambda b, t: (b, t, 0, 0)),
                   pl.BlockSpec((1, 1, MIX_W, MIX_W), lambda b, t: (b, _bwd_chunk(t, ncc, nc), 0, 0))],
        out_shape=[st_shape, st_shape],
        scratch_shapes=[pltpu.VMEM((MIX_W, MIX_W), F32), pltpu.VMEM((MIX_W, MIX_W), F32)],
        compiler_params=_params("arbitrary", "arbitrary"),
        name="ret_state",
    )(p, p, cos, sin, cos, sin, perm, kd, cd, bd)
    return pl.pallas_call(
        _ret_out_kernel,
        grid=(nb, nc),
        in_specs=[pl.BlockSpec((1, CHUNK, RET_COLS), fwd),
                  pl.BlockSpec((CHUNK, MIX_W), tab_f), pl.BlockSpec((CHUNK, MIX_W), tab_f),
                  pl.BlockSpec((1, 1, MIX_W, MIX_W), lambda b, t: (b, t, 0, 0)),
                  pl.BlockSpec((1, 1, MIX_W, MIX_W), lambda b, t: (b, t, 0, 0)),
                  pl.BlockSpec((MIX_W, MIX_W), c2),
                  pl.BlockSpec((CHUNK, N_HEADS * CHUNK), c2),
                  pl.BlockSpec((2, CHUNK, MIX_W), c3),
                  pl.BlockSpec((MIX_W, MIX_W), c2)],
        out_specs=pl.BlockSpec((1, CHUNK, MIX_W), fwd),
        out_shape=jax.ShapeDtypeStruct((nb, nt, MIX_W), BF16),
        compiler_params=_params("arbitrary", "arbitrary"),
        name="ret_out",
    )(p, cos, sin, st_f, st_b, perm, dm, qd, ones_bd)


def _gelu_tanh(x):
    return 0.5 * x * (1.0 + jnp.tanh(math.sqrt(2.0 / math.pi) * (x + 0.044715 * (x * x * x))))


def _sgate_kernel(p_ref, ng_ref, w_ref, b_ref, y_ref):
    z = _gelu_tanh(p_ref[0].astype(F32))
    u = z[:, :MIX_W]
    v = z[:, MIX_W:]
    mu = jnp.mean(v, axis=-1, keepdims=True)
    vc = v - mu
    var = jnp.mean(vc * vc, axis=-1, keepdims=True)
    vn = vc * lax.rsqrt(var + EPS) * ng_ref[...]
    mixed = _dot(w_ref[...], _stack_heads(vn)) + b_ref[...]
    y_ref[0] = (u * mixed).astype(BF16)


def _sgate(p, ng, wcat, bias):
    nb, nt, _ = p.shape
    nc = nt // CHUNK
    c2 = lambda b, t: (0, 0)
    return pl.pallas_call(
        _sgate_kernel,
        grid=(nb, nc),
        in_specs=[pl.BlockSpec((1, CHUNK, SG_COLS), lambda b, t: (b, t, P_SG // SG_COLS)),
                  pl.BlockSpec((1, MIX_W), c2),
                  pl.BlockSpec((CHUNK, N_HEADS * CHUNK), c2),
                  pl.BlockSpec((CHUNK, MIX_W), c2)],
        out_specs=pl.BlockSpec((1, CHUNK, MIX_W), lambda b, t: (b, t, 0)),
        out_shape=jax.ShapeDtypeStruct((nb, nt, MIX_W), BF16),
        compiler_params=_params("arbitrary", "arbitrary"),
        name="sgate",
    )(p, ng, wcat, bias)


def _dn_prep_kernel(pc_ref, pp_ref, pn_ref, ab_ref, cw_ref, na_ref, dtb_ref, ones_ref, qkv_ref, gb_ref, xe_ref,
                    *, ncc, nc):
    t = pl.program_id(1)
    w3 = 3 * MIX_W
    prev_ok = jnp.where((t != 0) & (t != ncc), 1.0, 0.0)
    next_ok = jnp.where((t != ncc - 1) & (t != nc - 1), 1.0, 0.0)
    tail = pp_ref[0, CHUNK - 16:CHUNK, 0:w3].astype(F32)
    head = pn_ref[0, 0:16, 0:w3].astype(F32)
    xe_ref[0:8, :] = tail[8:16, :] * prev_ok
    xe_ref[8:8 + CHUNK, :] = pc_ref[0, :, 0:w3].astype(F32)
    xe_ref[8 + CHUNK:16 + CHUNK, :] = head[0:8, :] * next_ok
    pad = CONV_W // 2
    y = xe_ref[8 - pad:8 - pad + CHUNK, :] * cw_ref[0:1, :]
    for i in range(1, CONV_W):
        y = y + xe_ref[8 - pad + i:8 - pad + i + CHUNK, :] * cw_ref[i:i + 1, :]
    y = y * jax.nn.sigmoid(y)
    q = y[:, 0:MIX_W]
    k = y[:, MIX_W:2 * MIX_W]
    v = y[:, 2 * MIX_W:w3]
    ones_bd = ones_ref[...]
    qn = q * lax.rsqrt(_head_sum(q * q, ones_bd) + EPS) * (HEAD_DIM ** -0.5)
    kn = k * lax.rsqrt(_head_sum(k * k, ones_bd) + EPS)
    qkv_ref[0, :, 0:MIX_W] = qn.astype(BF16)
    qkv_ref[0, :, MIX_W:2 * MIX_W] = kn.astype(BF16)
    qkv_ref[0, :, 2 * MIX_W:w3] = v.astype(BF16)
    ab = ab_ref[0]
    a = ab[0:8, :] + dtb_ref[...]
    softplus = jnp.maximum(a, 0.0) + jnp.log1p(jnp.exp(-jnp.abs(a)))
    gb_ref[0, 0:8, :] = na_ref[...] * softplus
    gb_ref[0, 8:16, :] = jax.nn.sigmoid(ab[8:16, :])


def _tri_inverse(mats, ii, jj):
    eye = jnp.where(ii == jj, 1.0, 0.0)
    nd = [jnp.where((ii // 16) == (jj // 16), n, 0.0) for n in mats]
    p1 = [_mm(x, x) for x in nd]
    m = [eye - x for x in nd]
    p2 = [_mm(x, x) for x in p1]
    m = [x + _mm(x, y) for x, y in zip(m, p1)]
    p3 = [_mm(x, x) for x in p2]
    m = [x + _mm(x, y) for x, y in zip(m, p2)]
    m = [x + _mm(x, y) for x, y in zip(m, p3)]
    for lvl in (16, 32, 64):
        off_mask = ((ii // (2 * lvl)) == (jj // (2 * lvl))) & ((ii // lvl) != (jj // lvl))
        t = [_mm(jnp.where(off_mask, n, 0.0), x) for n, x in zip(mats, m)]
        m = [x - _mm(x, y) for x, y in zip(m, t)]
    return m


def _dn_pre(qkv, g, beta, lower):
    c = CHUNK
    qn = qkv[:, 0:MIX_W]
    kn = qkv[:, MIX_W:2 * MIX_W]
    v = qkv[:, 2 * MIX_W:3 * MIX_W]
    ii = lax.broadcasted_iota(jnp.int32, (c, c), 0)
    jj = lax.broadcasted_iota(jnp.int32, (c, c), 1)
    incl = (ii >= jj) if lower else (ii <= jj)
    tri = jnp.where(incl, 1.0, 0.0)
    eye = jnp.where(ii == jj, 1.0, 0.0)
    g_row = _dot_nt(g, tri, precision=HIGHEST)
    x128 = jnp.concatenate([g, beta, jnp.zeros((c - 2 * N_HEADS, c), F32)], axis=0)
    cols = _dot_nt(jnp.concatenate([tri, eye], axis=0), x128, precision=HIGHEST)
    g_col = cols[0:c, 0:N_HEADS]
    b_col = cols[c:2 * c, N_HEADS:2 * N_HEADS]
    g_cols4 = jnp.concatenate([jnp.broadcast_to(g_col[:, h:h + 1], (c, c)) for h in range(N_HEADS)], axis=1)
    b_cols4 = jnp.concatenate([jnp.broadcast_to(b_col[:, h:h + 1], (c, c)) for h in range(N_HEADS)], axis=1)
    g_rows4 = jnp.concatenate([g_row[h:h + 1, :] for h in range(N_HEADS)], axis=1)
    incl4 = jnp.concatenate([incl] * N_HEADS, axis=1)
    diag4 = jnp.concatenate([ii == jj] * N_HEADS, axis=1)
    decay = jnp.where(incl4, jnp.exp(jnp.where(incl4, g_cols4 - g_rows4, 0.0)), 0.0)
    kstack = _stack_heads(kn)
    kk = _dot_nt(kn, kstack)
    qk = _dot_nt(qn, kstack)
    n_mat = jnp.where(diag4, 0.0, decay * kk * b_cols4)
    attn = (decay * qk).astype(BF16)
    g256 = _expand_heads(g_col)
    eg256 = jnp.exp(g256)
    b256 = _expand_heads(b_col)
    vb = v.astype(F32) * b256
    kbg = kn.astype(F32) * b256 * eg256
    rhs = jnp.concatenate([_stack_heads(vb), _stack_heads(kbg)], axis=1)
    g_last = g256[c - 1:c, :] if lower else g256[0:1, :]
    kdec = (kn.astype(F32) * jnp.exp(g_last - g256)).astype(BF16)
    n_heads = [n_mat[:, h * c:(h + 1) * c] for h in range(N_HEADS)]
    return n_heads, dict(qn=qn, attn=attn, rhs=rhs, eg256=eg256, kdec=kdec, s_decay=jnp.exp(g_last))


def _dn_post(pre, a_inv_heads, st, bd):
    a_inv = jnp.concatenate(a_inv_heads, axis=1).astype(BF16)
    uw = _dot(a_inv, pre["rhs"])
    u = uw[:, 0:MIX_W]
    wk = uw[:, MIX_W:2 * MIX_W]
    s_prev = st[...]
    s_bf = s_prev.astype(BF16)
    w = u - _dot(wk.astype(BF16), s_bf)
    o = pre["eg256"] * _dot(pre["qn"], s_bf) + _dot(pre["attn"], _stack_heads(w))
    st[...] = pre["s_decay"] * s_prev + bd * _dot_tn(pre["kdec"], w.astype(BF16))
    return o


def _dn_scan_kernel(qf_ref, qb_ref, gf_ref, gb_ref, bd_ref, of_ref, ob_ref, st_f, st_b):
    t = pl.program_id(1)

    @pl.when(t == 0)
    def _():
        st_f[...] = jnp.zeros_like(st_f)
        st_b[...] = jnp.zeros_like(st_b)

    bd = bd_ref[...]
    gf = gf_ref[0]
    gb = gb_ref[0]
    nf, pre_f = _dn_pre(qf_ref[0], gf[0:4, :], gf[8:12, :], True)
    nb_, pre_b = _dn_pre(qb_ref[0], gb[4:8, :], gb[12:16, :], False)
    ii = lax.broadcasted_iota(jnp.int32, (CHUNK, CHUNK), 0)
    jj = lax.broadcasted_iota(jnp.int32, (CHUNK, CHUNK), 1)
    inv = _tri_inverse(nf + nb_, ii, jj)
    of_ref[0] = _dn_post(pre_f, inv[0:N_HEADS], st_f, bd)
    ob_ref[0] = _dn_post(pre_b, inv[N_HEADS:2 * N_HEADS], st_b, bd)


def _deltanet(p, ab_t, conv_w, neg_a, dtb, bd, ones_bd, ncc):
    nb, nt, _ = p.shape
    nc = nt // CHUNK
    w3 = 3 * MIX_W
    c2 = lambda b, t: (0, 0)
    dn_blk = P_DN // RET_COLS
    qkv, gbeta = pl.pallas_call(
        functools.partial(_dn_prep_kernel, ncc=ncc, nc=nc),
        grid=(nb, nc),
        in_specs=[pl.BlockSpec((1, CHUNK, 4 * MIX_W), lambda b, t: (b, t, dn_blk)),
                  pl.BlockSpec((1, CHUNK, 4 * MIX_W), lambda b, t: (b, jnp.maximum(t - 1, 0), dn_blk)),
                  pl.BlockSpec((1, CHUNK, 4 * MIX_W), lambda b, t: (b, jnp.minimum(t + 1, nc - 1), dn_blk)),
                  pl.BlockSpec((1, 16, CHUNK), lambda b, t: (b, 0, t)),
                  pl.BlockSpec((8, w3), c2),
                  pl.BlockSpec((8, 1), c2),
                  pl.BlockSpec((8, 1), c2),
                  pl.BlockSpec((MIX_W, MIX_W), c2)],
        out_specs=[pl.BlockSpec((1, CHUNK, w3), lambda b, t: (b, t, 0)),
                   pl.BlockSpec((1, 16, CHUNK), lambda b, t: (b, 0, t))],
        out_shape=[jax.ShapeDtypeStruct((nb, nt, w3), BF16),
                   jax.ShapeDtypeStruct((nb, 16, nt), F32)],
        scratch_shapes=[pltpu.VMEM((CHUNK + 16, w3), F32)],
        compiler_params=_params("arbitrary", "arbitrary"),
        name="dn_prep",
    )(p, p, p, ab_t, conv_w, neg_a, dtb, ones_bd)
    fwd = lambda b, t: (b, t, 0)
    bwd = lambda b, t: (b, _bwd_chunk(t, ncc, nc), 0)
    o_shape = jax.ShapeDtypeStruct((nb, nt, MIX_W), F32)
    return pl.pallas_call(
        _dn_scan_kernel,
        grid=(nb, nc),
        in_specs=[pl.BlockSpec((1, CHUNK, w3), fwd),
                  pl.BlockSpec((1, CHUNK, w3), bwd),
                  pl.BlockSpec((1, 16, CHUNK), lambda b, t: (b, 0, t)),
                  pl.BlockSpec((1, 16, CHUNK), lambda b, t: (b, 0, _bwd_chunk(t, ncc, nc))),
                  pl.BlockSpec((MIX_W, MIX_W), c2)],
        out_specs=[pl.BlockSpec((1, CHUNK, MIX_W), fwd),
                   pl.BlockSpec((1, CHUNK, MIX_W), bwd)],
        out_shape=[o_shape, o_shape],
        scratch_shapes=[pltpu.VMEM((MIX_W, MIX_W), F32), pltpu.VMEM((MIX_W, MIX_W), F32)],
        compiler_params=_params("arbitrary", "arbitrary"),
        name="dn_scan",
    )(qkv, qkv, gbeta, gbeta, bd)


QK_W = 256


VT_ROWS = 144


def _mla_prep_kernel(p_ref, c_ref, s_ref, perm_ref, qg_ref, kg_ref, wqn_ref, wqr_ref, wa_ref, selq_ref, selc_ref,
                     selr_ref, selv_ref, one_ref, qt_ref, kv_ref, vt_ref, *, scale):
    p = p_ref[0]
    cos, sin, perm = c_ref[...], s_ref[...], perm_ref[...]
    cq = p[:, 0:Q_LORA].astype(F32)
    cqn = (cq * lax.rsqrt(jnp.mean(cq * cq, axis=-1, keepdims=True) + EPS) * qg_ref[...]).astype(BF16)
    q_nope = _dot(cqn, wqn_ref[...]).astype(BF16)
    q_rope = _dot(cqn, wqr_ref[...]).astype(BF16)
    q_rot = (_rot(q_rope, cos, sin, perm) * scale).astype(BF16)
    q_nope_s = (q_nope.astype(F32) * scale).astype(BF16)
    for h in range(N_HEADS):
        qt_ref[0, h] = (_dot_nt(wa_ref[h], q_nope_s) + _dot_nt(selq_ref[h], q_rot)).astype(BF16)
    ckv = p[:, Q_LORA:Q_LORA + KV_LORA].astype(F32)
    ckvn = (ckv * lax.rsqrt(jnp.mean(ckv * ckv, axis=-1, keepdims=True) + EPS) * kg_ref[...]).astype(BF16)
    kr = p[:, Q_LORA + KV_LORA:MLA_PAD]
    kr_rot = _rot(kr, cos, sin, perm).astype(BF16)
    kv_ref[0] = (_dot(ckvn, selc_ref[...]) + _dot(kr_rot, selr_ref[...])).astype(BF16)
    vt_ref[0] = (_dot_nt(selv_ref[...], ckvn) + one_ref[...]).astype(BF16)


def _mla_attn_kernel(qt_ref, kv_ref, vt_ref, wuv_ref, y_ref, m_ref, acc_ref, *, tq, tk, n_ctx, nt):
    i = pl.program_id(1)
    m_ref[...] = jnp.full_like(m_ref, -jnp.inf)
    acc_ref[...] = jnp.zeros_like(acc_ref)

    def key_tile(j0, size):
        k = kv_ref[0, pl.ds(j0, size), :]
        vt = vt_ref[0, :, pl.ds(j0, size)]
        heads = range(N_HEADS)
        s = [_dot(k, qt_ref[0, h]) for h in heads]
        m_old = [m_ref[h] for h in heads]
        m_new = [jnp.maximum(m_old[h], jnp.max(s[h], axis=0, keepdims=True)) for h in heads]
        pr = [jnp.exp2(s[h] - m_new[h]).astype(BF16) for h in heads]
        pv = [_dot(vt, pr[h]) for h in heads]
        for h in heads:
            acc_ref[h] = jnp.exp2(m_old[h] - m_new[h]) * acc_ref[h] + pv[h]
            m_ref[h] = m_new[h]

    key_tile(0, n_ctx)

    @pl.when((i + 1) * tq > n_ctx)
    def _():
        def body(j, carry):
            key_tile(pl.multiple_of(n_ctx + j * tk, 256), tk)
            return carry

        lax.fori_loop(0, (nt - n_ctx) // tk, body, 0, unroll=2)

    y = None
    for h in range(N_HEADS):
        acc = acc_ref[h]
        o = (acc[0:KV_LORA, :] / acc[KV_LORA:KV_LORA + 1, :]).astype(BF16)
        term = _dot_tn(o, wuv_ref[h])
        y = term if y is None else y + term
    y_ref[0] = y.astype(BF16)


def _mla(p, cos, sin, perm, qg, kg, wqn, wqr, wa, selq, selc, selr, selv, one_col, wuv, n_ctx):
    nb, nt, _ = p.shape
    tm = _pick(nt, (768, 384, 256, 128))
    scale = (NOPE_DIM + ROPE_DIM) ** -0.5 * math.log2(math.e)
    c2 = lambda b, i: (0, 0)
    c3 = lambda b, i: (0, 0, 0)
    qt, kv, vt = pl.pallas_call(
        functools.partial(_mla_prep_kernel, scale=scale),
        grid=(nb, nt // tm),
        in_specs=[pl.BlockSpec((1, tm, MLA_PAD), lambda b, i: (b, i, P_MLA // MLA_PAD)),
                  pl.BlockSpec((tm, 128), lambda b, i: (i, 0)),
                  pl.BlockSpec((tm, 128), lambda b, i: (i, 0)),
                  pl.BlockSpec((128, 128), c2),
                  pl.BlockSpec((1, Q_LORA), c2),
                  pl.BlockSpec((1, KV_LORA), c2),
                  pl.BlockSpec((Q_LORA, N_HEADS * NOPE_DIM), c2),
                  pl.BlockSpec((Q_LORA, N_HEADS * ROPE_DIM), c2),
                  pl.BlockSpec((N_HEADS, QK_W, N_HEADS * NOPE_DIM), c3),
                  pl.BlockSpec((N_HEADS, QK_W, N_HEADS * ROPE_DIM), c3),
                  pl.BlockSpec((KV_LORA, QK_W), c2),
                  pl.BlockSpec((128, QK_W), c2),
                  pl.BlockSpec((VT_ROWS, KV_LORA), c2),
                  pl.BlockSpec((VT_ROWS, 1), c2)],
        out_specs=[pl.BlockSpec((1, N_HEADS, QK_W, tm), lambda b, i: (b, 0, 0, i)),
                   pl.BlockSpec((1, tm, QK_W), lambda b, i: (b, i, 0)),
                   pl.BlockSpec((1, VT_ROWS, tm), lambda b, i: (b, 0, i))],
        out_shape=[jax.ShapeDtypeStruct((nb, N_HEADS, QK_W, nt), BF16),
                   jax.ShapeDtypeStruct((nb, nt, QK_W), BF16),
                   jax.ShapeDtypeStruct((nb, VT_ROWS, nt), BF16)],
        compiler_params=_params("arbitrary", "arbitrary"),
        name="mla_prep",
    )(p, cos, sin, perm, qg, kg, wqn, wqr, wa, selq, selc, selr, selv, one_col)
    tq = 256
    tk = _pick(nt - n_ctx, (512, 256))
    return pl.pallas_call(
        functools.partial(_mla_attn_kernel, tq=tq, tk=tk, n_ctx=n_ctx, nt=nt),
        grid=(nb, nt // tq),
        in_specs=[pl.BlockSpec((1, N_HEADS, QK_W, tq), lambda b, i: (b, 0, 0, i)),
                  pl.BlockSpec((1, nt, QK_W), lambda b, i: (b, 0, 0)),
                  pl.BlockSpec((1, VT_ROWS, nt), lambda b, i: (b, 0, 0)),
                  pl.BlockSpec((N_HEADS, KV_LORA, MIX_W), c3)],
        out_specs=pl.BlockSpec((1, tq, MIX_W), lambda b, i: (b, i, 0)),
        out_shape=jax.ShapeDtypeStruct((nb, nt, MIX_W), BF16),
        scratch_shapes=[pltpu.VMEM((N_HEADS, 1, tq), F32), pltpu.VMEM((N_HEADS, VT_ROWS, tq), F32)],
        compiler_params=_params("arbitrary", "arbitrary"),
        name="mla_attn",
    )(qt, kv, vt, wuv)


def _merge_kernel(x_ref, yr_ref, ys_ref, of_ref, ob_ref, ym_ref, z_ref, g0_ref, g1_ref, g2_ref, g3_ref,
                  wb_ref, wo_ref, ng_ref, gp_ref, ml_ref, mc_ref, ones_ref, o_ref, *, tm, n_ctx):
    i = pl.program_id(1)
    od = of_ref[0] + ob_ref[0]
    ms = _head_sum(od * od, ones_ref[...]) * (1.0 / HEAD_DIM)
    z = z_ref[0].astype(F32)
    ydn = (od * lax.rsqrt(ms + EPS) * ng_ref[...]) * (z * jax.nn.sigmoid(z))
    ys = (yr_ref[0], ys_ref[0], ydn.astype(BF16), ym_ref[0])
    gates = (g0_ref, g1_ref, g2_ref, g3_ref)
    acc = None
    for b in range(N_BRANCH):
        term = jax.nn.sigmoid(gates[b][0].astype(F32)) * _dot(ys[b], wb_ref[b])
        acc = term if acc is None else acc + term
    y = _dot(acc.astype(BF16), wo_ref[...])
    r = y * lax.rsqrt(jnp.mean(y * y, axis=-1, keepdims=True) + EPS) * gp_ref[...]
    rows = lax.broadcasted_iota(jnp.int32, (tm, 1), 0) + i * tm
    gate = jnp.where(rows < n_ctx, mc_ref[0, 2:3, :], ml_ref[0, 2:3, :])
    o_ref[0] = x_ref[0] + gate * r


def _merge(xa, y_ret, y_sg, o_f, o_b, y_mla, p, wb, wo, ng, gp, mod, ones_bd, n_ctx):
    nb, nt, d = xa.shape
    tm = _pick(nt, (768, 384, 256, 128))
    row = lambda b, i: (b, i, 0)
    c2 = lambda b, i: (0, 0)
    y_spec = pl.BlockSpec((1, tm, MIX_W), row)
    gate_specs = [pl.BlockSpec((1, tm, d), functools.partial(lambda b, i, k: (b, i, k), k=P_GATE // d + k))
                  for k in range(N_BRANCH)]
    return pl.pallas_call(
        functools.partial(_merge_kernel, tm=tm, n_ctx=n_ctx),
        grid=(nb, nt // tm),
        in_specs=[pl.BlockSpec((1, tm, d), row), y_spec, y_spec, y_spec, y_spec, y_spec,
                  pl.BlockSpec((1, tm, MIX_W), lambda b, i: (b, i, (P_DN + 3 * MIX_W) // MIX_W)),
                  *gate_specs,
                  pl.BlockSpec((N_BRANCH, MIX_W, d), lambda b, i: (0, 0, 0)),
                  pl.BlockSpec((d, d), c2),
                  pl.BlockSpec((1, MIX_W), c2),
                  pl.BlockSpec((1, d), c2),
                  pl.BlockSpec((1, 6, d), lambda b, i: (b, 0, 0)),
                  pl.BlockSpec((1, 6, d), lambda b, i: (nb, 0, 0)),
                  pl.BlockSpec((MIX_W, MIX_W), c2)],
        out_specs=pl.BlockSpec((1, tm, d), row),
        out_shape=jax.ShapeDtypeStruct((nb, nt, d), F32),
        compiler_params=_params("arbitrary", "arbitrary"),
        name="merge",
    )(xa, y_ret, y_sg, o_f, o_b, y_mla, p, p, p, p, p, wb, wo, ng, gp, mod, mod, ones_bd)


def _route(sel, aff):
    rows = [sel[e:e + 1, :] for e in range(N_EXPERTS)]
    pairs = [(a, b) for a in range(EXPERTS_PER_GROUP) for b in range(a + 1, EXPERTS_PER_GROUP)]
    grp_score, grp_pair = [], []
    for g in range(N_GROUPS):
        base = g * EXPERTS_PER_GROUP
        best = rows[base + pairs[0][0]] + rows[base + pairs[0][1]]
        best_p = jnp.zeros_like(best, dtype=jnp.int32)
        for pi in range(1, len(pairs)):
            s = rows[base + pairs[pi][0]] + rows[base + pairs[pi][1]]
            take = s > best
            best = jnp.where(take, s, best)
            best_p = jnp.where(take, pi, best_p)
        grp_score.append(best)
        grp_pair.append(best_p)
    top = grp_score[0]
    top_g = jnp.zeros_like(grp_pair[0])
    top_p = grp_pair[0]
    for g in range(1, N_GROUPS):
        take = grp_score[g] > top
        top = jnp.where(take, grp_score[g], top)
        top_g = jnp.where(take, g, top_g)
        top_p = jnp.where(take, grp_pair[g], top_p)
    picked = []
    for e in range(N_EXPERTS):
        g, k = divmod(e, EXPERTS_PER_GROUP)
        in_pair = None
        for pi, (a, b) in enumerate(pairs):
            if k in (a, b):
                hit = top_p == pi
                in_pair = hit if in_pair is None else (in_pair | hit)
        picked.append(jnp.where((top_g == g) & in_pair, aff[e:e + 1, :], 0.0))
    denom = picked[0]
    for e in range(1, N_EXPERTS):
        denom = denom + picked[e]
    return [pk / denom for pk in picked]


def _moe_kernel(x_ref, ml_ref, mc_ref, g2_ref, gp_ref, rw_ref, rb_ref, w1_ref, w3_ref, w2_ref, o_ref,
                hn_ref, comb_t_ref, comb_ref, acc_ref, *, tm, rb, n_ctx):
    i = pl.program_id(1)
    e = pl.program_id(2)

    @pl.when(e == 0)
    def _():
        def blk(r, carry):
            r0 = pl.multiple_of(r * rb, rb)
            x = x_ref[0, pl.ds(r0, rb), :]
            rows = lax.broadcasted_iota(jnp.int32, (rb, 1), 0) + (i * tm + r0)
            hn = _norm_modulate(x, g2_ref[...], rows < n_ctx, mc_ref, ml_ref, 3, 4)
            hn_ref[pl.ds(r0, rb), :] = hn.astype(BF16)
            logit = _dot_nt(rw_ref[...], hn.astype(BF16))
            aff = jax.nn.sigmoid(logit)
            comb = _route(aff + rb_ref[...], aff)
            for k in range(N_EXPERTS):
                comb_t_ref[k:k + 1, pl.ds(r0, rb)] = comb[k]
            return carry

        comb_t_ref[...] = jnp.zeros_like(comb_t_ref)
        lax.fori_loop(0, tm // rb, blk, 0)
        comb_t_ref[N_EXPERTS:N_EXPERTS + 1, :] = jnp.ones((1, tm), F32)
        comb_ref[...] = comb_t_ref[...].T
        acc_ref[...] = jnp.zeros_like(acc_ref)

    hn = hn_ref[...]
    a = _dot(hn, w1_ref[0])
    h = (a * jax.nn.sigmoid(a)) * _dot(hn, w3_ref[0])
    lane = lax.broadcasted_iota(jnp.int32, (1, 128), 1)
    ce = jnp.sum(jnp.where(lane == e, comb_ref[...], 0.0), axis=-1, keepdims=True)
    acc_ref[...] += _dot((h * ce).astype(BF16), w2_ref[0])

    @pl.when(e == pl.num_programs(2) - 1)
    def _():
        y = acc_ref[...]
        r = y * lax.rsqrt(jnp.mean(y * y, axis=-1, keepdims=True) + EPS) * gp_ref[...]
        rows = lax.broadcasted_iota(jnp.int32, (tm, 1), 0) + i * tm
        gate = jnp.where(rows < n_ctx, mc_ref[0, 5:6, :], ml_ref[0, 5:6, :])
        o_ref[0] = x_ref[0] + gate * r


def _moe(xa, mod, g2, gp, rw_t, rbias, w1, w3, w2, n_ctx):
    nb, nt, d = xa.shape
    tm = _pick(nt, (768, 384, 256, 128))
    ne = w1.shape[0]
    row = lambda b, i, e: (b, i, 0)
    c2 = lambda b, i, e: (0, 0)
    return pl.pallas_call(
        functools.partial(_moe_kernel, tm=tm, rb=128, n_ctx=n_ctx),
        grid=(nb, nt // tm, ne),
        in_specs=[pl.BlockSpec((1, tm, d), row),
                  pl.BlockSpec((1, 6, d), lambda b, i, e: (b, 0, 0)),
                  pl.BlockSpec((1, 6, d), lambda b, i, e: (nb, 0, 0)),
                  pl.BlockSpec((1, d), c2),
                  pl.BlockSpec((1, d), c2),
                  pl.BlockSpec((N_EXPERTS, d), c2),
                  pl.BlockSpec((N_EXPERTS, 1), c2),
                  pl.BlockSpec((1, d, D_EXPERT), lambda b, i, e: (e, 0, 0)),
                  pl.BlockSpec((1, d, D_EXPERT), lambda b, i, e: (e, 0, 0)),
                  pl.BlockSpec((1, D_EXPERT, d), lambda b, i, e: (e, 0, 0))],
        out_specs=pl.BlockSpec((1, tm, d), row),
        out_shape=jax.ShapeDtypeStruct((nb, nt, d), F32),
        scratch_shapes=[pltpu.VMEM((tm, d), BF16), pltpu.VMEM((128, tm), F32), pltpu.VMEM((tm, 128), F32),
                        pltpu.VMEM((tm, d), F32)],
        compiler_params=_params("arbitrary", "arbitrary", "arbitrary"),
        name="moe",
    )(xa, mod, mod, g2, gp, rw_t, rbias, w1, w3, w2)


def _swap_perm(width, group):
    j = jnp.arange(width)
    src = jnp.where((j % group) < group // 2, j + group // 2, j - group // 2)
    return (jnp.arange(width)[:, None] == src[None, :]).astype(BF16)


def _rope_tables(n_lat, n_ctx):
    def angles(pos, dim):
        half = dim // 2
        inv = ROPE_BASE ** (-jnp.arange(half, dtype=F32) / half)
        return pos.astype(F32)[:, None] * inv[None, :]

    def tables(cos_parts, sin_parts, reps):
        cos = jnp.tile(jnp.concatenate(cos_parts, axis=-1), (1, reps))
        sin = jnp.tile(jnp.concatenate(sin_parts, axis=-1), (1, reps))
        w = cos.shape[1]
        return (jnp.concatenate([jnp.ones((n_ctx, w), F32), cos], axis=0),
                jnp.concatenate([jnp.zeros((n_ctx, w), F32), sin], axis=0))

    rows = n_lat // GRID_W
    ang_t = angles(jnp.arange(n_lat), HEAD_DIM)
    ang_r = angles(jnp.repeat(jnp.arange(rows), GRID_W), ROPE_DIM // 2)
    ang_c = angles(jnp.tile(jnp.arange(GRID_W), rows), ROPE_DIM // 2)
    ct, st = jnp.cos(ang_t), jnp.sin(ang_t)
    ret = tables([ct, ct], [-st, st], N_HEADS)
    cr, sr, cc, sc = jnp.cos(ang_r), jnp.sin(ang_r), jnp.cos(ang_c), jnp.sin(ang_c)
    mla = tables([cr, cr, cc, cc], [-sr, sr, -sc, sc], N_HEADS)
    return ret, mla


def _ret_tables(logit):
    log_g = jax.nn.log_sigmoid(logit.astype(F32))
    lane_lg = jnp.repeat(log_g, HEAD_DIM, axis=1)
    idx = jnp.arange(CHUNK, dtype=F32)[:, None]
    kd = jnp.stack([jnp.exp(lane_lg[0][None, :] * (CHUNK - 1 - idx)), jnp.exp(lane_lg[1][None, :] * idx)])
    qd = jnp.stack([jnp.exp(lane_lg[0][None, :] * (idx + 1)), jnp.exp(lane_lg[1][None, :] * (CHUNK - idx))])
    cd = jnp.exp(lane_lg * CHUNK)[:, None, :]
    diff = idx - idx.T
    blocks = []
    for h in range(N_HEADS):
        f = jnp.exp(log_g[0, h] * jnp.where(diff >= 0, diff, 0.0))
        b = jnp.exp(log_g[1, h] * jnp.where(diff < 0, -diff, 0.0))
        blocks.append(jnp.where(diff >= 0, f, b))
    dm = jnp.concatenate(blocks, axis=1)
    return kd, cd, qd, dm


def _pack_w_in(w_in):
    d = w_in.shape[0]
    mla = jnp.concatenate([w_in[:, OFF_MLA:OFF_MLA + MLA_COLS], jnp.zeros((d, MLA_PAD - MLA_COLS), w_in.dtype)], 1)
    w = jnp.concatenate([w_in[:, OFF_RET:OFF_RET + RET_COLS], w_in[:, OFF_DN:OFF_DN + 4 * MIX_W],
                         w_in[:, OFF_SG:OFF_SG + SG_COLS], mla, w_in[:, OFF_GATE:OFF_GATE + GATE_COLS]], axis=1)
    wab = w_in[:, OFF_DN + 4 * MIX_W:OFF_DN + DN_COLS].T
    return w.astype(BF16), wab.astype(BF16)


def _mla_weights(w_uq, w_ukv):
    dq = NOPE_DIM + ROPE_DIM
    dkv = NOPE_DIM + V_DIM
    wq = w_uq.reshape(Q_LORA, N_HEADS, dq)
    wqn = wq[:, :, :NOPE_DIM].reshape(Q_LORA, N_HEADS * NOPE_DIM)
    wqr = wq[:, :, NOPE_DIM:].reshape(Q_LORA, N_HEADS * ROPE_DIM)
    wkv = w_ukv.reshape(KV_LORA, N_HEADS, dkv)
    wa = jnp.zeros((N_HEADS, QK_W, N_HEADS * NOPE_DIM), F32)
    selq = jnp.zeros((N_HEADS, QK_W, N_HEADS * ROPE_DIM), F32)
    wuv = jnp.zeros((N_HEADS, KV_LORA, MIX_W), F32)
    for h in range(N_HEADS):
        wa = wa.at[h, 0:KV_LORA, h * NOPE_DIM:(h + 1) * NOPE_DIM].set(wkv[:, h, :NOPE_DIM])
        selq = selq.at[h, KV_LORA:KV_LORA + ROPE_DIM, h * ROPE_DIM:(h + 1) * ROPE_DIM].set(jnp.eye(ROPE_DIM))
        wuv = wuv.at[h, :, h * V_DIM:(h + 1) * V_DIM].set(wkv[:, h, NOPE_DIM:])
    selc = jnp.zeros((KV_LORA, QK_W), F32).at[:, 0:KV_LORA].set(jnp.eye(KV_LORA))
    selr = jnp.zeros((128, QK_W), F32).at[0:ROPE_DIM, KV_LORA:KV_LORA + ROPE_DIM].set(jnp.eye(ROPE_DIM))
    selv = jnp.zeros((VT_ROWS, KV_LORA), F32).at[0:KV_LORA, :].set(jnp.eye(KV_LORA))
    one_col = jnp.zeros((VT_ROWS, 1), F32).at[KV_LORA, 0].set(1.0)
    return tuple(a.astype(BF16) for a in (wqn, wqr, wa, selq, selc, selr, selv)) + (one_col, wuv.astype(BF16))


def kernel(x, c, ctx, c_ctx, w_ada, b_ada, g_pre1, g_post1, g_pre2, g_post2, w_in, ret_decay_logit, sg_norm_g, sg_w, sg_b, dn_conv_w, dn_A_log, dn_dt_bias, dn_norm_g, mla_q_norm_g, mla_kv_norm_g, mla_w_uq, mla_w_ukv, w_branch, w_out, router_w, router_bias, moe_w1, moe_w3, moe_w2, shared_w1, shared_w3, shared_w2):
    nb, n_lat, d = x.shape
    n_ctx = ctx.shape[1]
    depth = w_in.shape[0]
    assert d == D_MODEL and n_lat % GRID_W == 0 and n_lat % CHUNK == 0 and n_ctx % 256 == 0
    ncc = n_ctx // CHUNK

    n_cond = -(-(nb + 1) // 8) * 8
    cond = jnp.concatenate([c, c_ctx[None], jnp.zeros((n_cond - nb - 1, d), F32)], axis=0)
    mod_all = _adaln(cond, w_ada, b_ada).reshape(depth, n_cond, 6, d)

    (ret_cos, ret_sin), (mla_cos, mla_sin) = _rope_tables(n_lat, n_ctx)
    perm_ret = _swap_perm(MIX_W, HEAD_DIM)
    perm_mla = _swap_perm(N_HEADS * ROPE_DIM, ROPE_DIM // 2)
    lane_head = jnp.arange(MIX_W) // HEAD_DIM
    bd = (lane_head[:, None] == lane_head[None, :]).astype(F32)
    ones_bd = bd.astype(BF16)
    rw_t = router_w.T.astype(BF16)
    rbias = router_bias.astype(F32)[:, None]

    xa = jnp.concatenate([ctx, x], axis=1)
    for l in range(depth):
        mod = mod_all[l]
        w_l, wab_l = _pack_w_in(w_in[l])
        p, ab_t = _inproj(xa, mod, g_pre1[l][None], w_l, wab_l, n_ctx)

        kd, cd, qd, dm = _ret_tables(ret_decay_logit[l])
        y_ret = _retention(p, ret_cos, ret_sin, perm_ret, (kd, cd, qd, dm, bd, ones_bd), ncc)

        wcat = jnp.concatenate([sg_w[l, h] for h in range(N_HEADS)], axis=1).astype(BF16)
        sg_bias = jnp.repeat(sg_b[l].T, HEAD_DIM, axis=1)
        y_sg = _sgate(p, sg_norm_g[l][None], wcat, sg_bias)

        neg_a = (-jnp.exp(dn_A_log[l].astype(F32))).reshape(2 * N_HEADS, 1)
        dtb = dn_dt_bias[l].astype(F32).reshape(2 * N_HEADS, 1)
        conv_w = jnp.concatenate([dn_conv_w[l], jnp.zeros((8 - CONV_W, 3 * MIX_W), F32)], axis=0)
        o_f, o_b = _deltanet(p, ab_t, conv_w, neg_a, dtb, bd, ones_bd, ncc)

        y_mla = _mla(p, mla_cos, mla_sin, perm_mla, mla_q_norm_g[l][None], mla_kv_norm_g[l][None],
                     *_mla_weights(mla_w_uq[l], mla_w_ukv[l]), n_ctx)

        xa = _merge(xa, y_ret, y_sg, o_f, o_b, y_mla, p, w_branch[l].astype(BF16), w_out[l].astype(BF16),
                    jnp.tile(dn_norm_g[l], N_HEADS)[None], g_post1[l][None], mod, ones_bd, n_ctx)

        w1 = jnp.concatenate([moe_w1[l], shared_w1[l][None]], axis=0).astype(BF16)
        w3 = jnp.concatenate([moe_w3[l], shared_w3[l][None]], axis=0).astype(BF16)
        w2 = jnp.concatenate([moe_w2[l], shared_w2[l][None]], axis=0).astype(BF16)
        xa = _moe(xa, mod, g_pre2[l][None], g_post2[l][None], rw_t, rbias, w1, w3, w2, n_ctx)
    return xa[:, n_ctx:]
```

```python
import functools
import math

import jax
import jax.numpy as jnp
from jax import lax
from jax.experimental import pallas as pl
from jax.experimental.pallas import tpu as pltpu

F32 = jnp.float32
BF16 = jnp.bfloat16
HIGHEST = lax.Precision.HIGHEST

D_MODEL = 1024
GRID_W = 64
N_HEADS = 4
HEAD_DIM = 64
MIX_W = N_HEADS * HEAD_DIM
CHUNK = 128
ROPE_BASE = 10000.0
EPS = 1e-6
RET_DECAY_EXP0 = 5.0
CONV_W = 5
Q_LORA = 256
KV_LORA = 128
NOPE_DIM = 64
ROPE_DIM = 32
V_DIM = 64
N_EXPERTS = 16
N_GROUPS = 4
EXPERTS_PER_GROUP = N_EXPERTS // N_GROUPS
D_EXPERT = 256
N_BRANCH = 4

RET_COLS = 4 * MIX_W
SG_COLS = 2 * MIX_W
DN_COLS = 4 * MIX_W + 4 * N_HEADS
MLA_COLS = Q_LORA + KV_LORA + ROPE_DIM
GATE_COLS = N_BRANCH * D_MODEL
OFF_RET = 0
OFF_SG = OFF_RET + RET_COLS
OFF_DN = OFF_SG + SG_COLS
OFF_MLA = OFF_DN + DN_COLS
OFF_GATE = OFF_MLA + MLA_COLS

P_RET = 0
P_DN = 1024
P_SG = 2048
P_MLA = 2560
P_GATE = 3072
P_COLS = 7168
MLA_PAD = 512

VMEM_LIMIT = 56 * 1024 * 1024


def _dot(a, b, precision=None):
    return jnp.dot(a, b, preferred_element_type=F32, precision=precision)


def _dot_nt(a, b, precision=None):
    return lax.dot_general(a, b, (((1,), (1,)), ((), ())), preferred_element_type=F32, precision=precision)


def _dot_tn(a, b):
    return lax.dot_general(a, b, (((0,), (0,)), ((), ())), preferred_element_type=F32)


def _mm(a, b):
    return _dot(a.astype(BF16), b.astype(BF16))


def _params(*sem):
    return pltpu.CompilerParams(dimension_semantics=sem, vmem_limit_bytes=VMEM_LIMIT)


def _pick(n, cands):
    for c in cands:
        if n % c == 0:
            return c
    raise ValueError(f"no tile for {n}")


def _head_of_lane(width, group):
    return lax.broadcasted_iota(jnp.int32, (1, width), 1) // group


def _stack_heads(x):
    head = _head_of_lane(MIX_W, HEAD_DIM)
    xf = x.astype(F32)
    return jnp.concatenate([jnp.where(head == h, xf, 0.0).astype(BF16) for h in range(N_HEADS)], axis=0)


def _expand_heads(cols):
    head = _head_of_lane(MIX_W, HEAD_DIM)
    out = cols[:, N_HEADS - 1:N_HEADS]
    for h in range(N_HEADS - 2, -1, -1):
        out = jnp.where(head <= h, cols[:, h:h + 1], out)
    return out


def _head_sum(x, ones_bd):
    hi = x.astype(BF16)
    lo = (x - hi.astype(F32)).astype(BF16)
    return _dot(hi, ones_bd) + _dot(lo, ones_bd)


def _rot(x_bf, cos, sin, perm):
    return x_bf.astype(F32) * cos + _dot(x_bf, perm) * sin


def _norm_modulate(x, g, isc, mc_ref, ml_ref, shift_row, scale_row):
    h = x * lax.rsqrt(jnp.mean(x * x, axis=-1, keepdims=True) + EPS) * g
    shift = jnp.where(isc, mc_ref[0, shift_row:shift_row + 1, :], ml_ref[0, shift_row:shift_row + 1, :])
    scale = jnp.where(isc, mc_ref[0, scale_row:scale_row + 1, :], ml_ref[0, scale_row:scale_row + 1, :])
    return h * (1.0 + scale) + shift


def _adaln_kernel(c_ref, w_ref, b_ref, o_ref):
    c = c_ref[...]
    s = c * jax.nn.sigmoid(c)
    o_ref[0] = _dot(s, w_ref[0], precision=HIGHEST) + b_ref[0]


def _adaln(cond, w_ada, b_ada):
    n_l, d, d6 = w_ada.shape
    r = cond.shape[0]
    tn = 1024
    return pl.pallas_call(
        _adaln_kernel,
        grid=(n_l, d6 // tn),
        in_specs=[pl.BlockSpec((r, d), lambda l, j: (0, 0)),
                  pl.BlockSpec((1, d, tn), lambda l, j: (l, 0, j)),
                  pl.BlockSpec((1, 1, tn), lambda l, j: (l, 0, j))],
        out_specs=pl.BlockSpec((1, r, tn), lambda l, j: (l, 0, j)),
        out_shape=jax.ShapeDtypeStruct((n_l, r, d6), F32),
        compiler_params=_params("arbitrary", "arbitrary"),
        name="adaln",
    )(cond, w_ada, b_ada.reshape(n_l, 1, d6))


def _inproj_kernel(x_ref, ml_ref, mc_ref, g_ref, w_ref, wab_ref, wabc_ref, p_ref, ab_ref, abc_ref, xn_ref,
                   *, tm, rb, n_ctx):
    i = pl.program_id(1)
    j = pl.program_id(2)

    @pl.when(j == 0)
    def _():
        def blk(r, carry):
            r0 = pl.multiple_of(r * rb, rb)
            x = x_ref[0, pl.ds(r0, rb), :]
            rows = lax.broadcasted_iota(jnp.int32, (rb, 1), 0) + (i * tm + r0)
            hn = _norm_modulate(x, g_ref[...], rows < n_ctx, mc_ref, ml_ref, 0, 1)
            xn_ref[pl.ds(r0, rb), :] = hn.astype(BF16)
            return carry

        lax.fori_loop(0, tm // rb, blk, 0)
        ab_ref[0] = _dot_nt(wab_ref[...], xn_ref[...])
        abc_ref[0] = _dot(xn_ref[...], wabc_ref[...])

    p_ref[0] = _dot(xn_ref[...], w_ref[...]).astype(BF16)


def _inproj(xa, mod, g, w, wab, wabc, n_ctx):
    nb, nt, d = xa.shape
    tm = _pick(nt, (1408, 768, 384, 256, 128))
    tn = 1792
    kern = functools.partial(_inproj_kernel, tm=tm, rb=128, n_ctx=n_ctx)
    return pl.pallas_call(
        kern,
        grid=(nb, nt // tm, P_COLS // tn),
        in_specs=[pl.BlockSpec((1, tm, d), lambda b, i, j: (b, i, 0)),
                  pl.BlockSpec((1, 6, d), lambda b, i, j: (b, 0, 0)),
                  pl.BlockSpec((1, 6, d), lambda b, i, j: (nb, 0, 0)),
                  pl.BlockSpec((1, d), lambda b, i, j: (0, 0)),
                  pl.BlockSpec((d, tn), lambda b, i, j: (0, j)),
                  pl.BlockSpec((16, d), lambda b, i, j: (0, 0)),
                  pl.BlockSpec((d, 128), lambda b, i, j: (0, 0))],
        out_specs=[pl.BlockSpec((1, tm, tn), lambda b, i, j: (b, i, j)),
                   pl.BlockSpec((1, 16, tm), lambda b, i, j: (b, 0, i)),
                   pl.BlockSpec((1, tm, 128), lambda b, i, j: (b, i, 0))],
        out_shape=[jax.ShapeDtypeStruct((nb, nt, P_COLS), BF16),
                   jax.ShapeDtypeStruct((nb, 16, nt), F32),
                   jax.ShapeDtypeStruct((nb, nt, 128), F32)],
        scratch_shapes=[pltpu.VMEM((tm, d), BF16)],
        compiler_params=_params("arbitrary", "arbitrary", "arbitrary"),
        name="inproj",
    )(xa, mod, mod, g, w, wab, wabc)


def _bwd_chunk(t, ncc, nc):
    return jnp.where(t < ncc, ncc - 1 - t, nc - 1 - (t - ncc))


def _ret_state_kernel(pf_ref, pb_ref, cf_ref, sf_ref, cb_ref, sb_ref, perm_ref, kd_ref, cd_ref, bd_ref,
                      of_ref, ob_ref, st_f, st_b):
    t = pl.program_id(1)

    @pl.when(t == 0)
    def _():
        st_f[...] = jnp.zeros_like(st_f)
        st_b[...] = jnp.zeros_like(st_b)

    def upd(p_ref, c_ref, s_ref, d, st, o_ref):
        k = p_ref[0, :, MIX_W:2 * MIX_W]
        v = p_ref[0, :, 2 * MIX_W:3 * MIX_W]
        kr = _rot(k, c_ref[...], s_ref[...], perm_ref[...]) * (HEAD_DIM ** -0.5)
        o_ref[0, 0] = st[...].astype(BF16)
        inc = _dot_tn((kr * kd_ref[d]).astype(BF16), v)
        st[...] = cd_ref[d] * st[...] + bd_ref[...] * inc

    upd(pf_ref, cf_ref, sf_ref, 0, st_f, of_ref)
    upd(pb_ref, cb_ref, sb_ref, 1, st_b, ob_ref)


def _ret_out_kernel(p_ref, c_ref, s_ref, sf_ref, sb_ref, perm_ref, dm_ref, qd_ref, ones_ref, y_ref):
    p = p_ref[0]
    q = p[:, 0:MIX_W]
    k = p[:, MIX_W:2 * MIX_W]
    v = p[:, 2 * MIX_W:3 * MIX_W]
    g = p[:, 3 * MIX_W:4 * MIX_W].astype(F32)
    cos, sin, perm = c_ref[...], s_ref[...], perm_ref[...]
    qr = _rot(q, cos, sin, perm)
    kr = _rot(k, cos, sin, perm) * (HEAD_DIM ** -0.5)
    sc = _dot_nt(qr.astype(BF16), _stack_heads(kr)) * dm_ref[...]
    o = _dot(sc.astype(BF16), _stack_heads(v))
    qs = jnp.concatenate([(qr * qd_ref[0]).astype(BF16), (qr * qd_ref[1]).astype(BF16)], axis=1)
    ss = jnp.concatenate([sf_ref[0, 0], sb_ref[0, 0]], axis=0)
    o = o + _dot(qs, ss)
    ones_bd = ones_ref[...]
    mu = _head_sum(o, ones_bd) * (1.0 / HEAD_DIM)
    oc = o - mu
    var = _head_sum(oc * oc, ones_bd) * (1.0 / HEAD_DIM)
    y = oc * lax.rsqrt(var + EPS)
    y_ref[0] = (y * (g * jax.nn.sigmoid(g))).astype(BF16)


def _retention(p, cos, sin, perm, tabs, ncc):
    nb, nt, _ = p.shape
    nc = nt // CHUNK
    kd, cd, qd, dm, bd, ones_bd = tabs
    fwd = lambda b, t: (b, t, 0)
    bwd = lambda b, t: (b, _bwd_chunk(t, ncc, nc), 0)
    tab_f = lambda b, t: (t, 0)
    tab_b = lambda b, t: (_bwd_chunk(t, ncc, nc), 0)
    c2 = lambda b, t: (0, 0)
    c3 = lambda b, t: (0, 0, 0)
    st_shape = jax.ShapeDtypeStruct((nb, nc, MIX_W, MIX_W), BF16)
    st_f, st_b = pl.pallas_call(
        _ret_state_kernel,
        grid=(nb, nc),
        in_specs=[pl.BlockSpec((1, CHUNK, RET_COLS), fwd),
                  pl.BlockSpec((1, CHUNK, RET_COLS), bwd),
                  pl.BlockSpec((CHUNK, MIX_W), tab_f), pl.BlockSpec((CHUNK, MIX_W), tab_f),
                  pl.BlockSpec((CHUNK, MIX_W), tab_b), pl.BlockSpec((CHUNK, MIX_W), tab_b),
                  pl.BlockSpec((MIX_W, MIX_W), c2),
                  pl.BlockSpec((2, CHUNK, MIX_W), c3),
                  pl.BlockSpec((2, 1, MIX_W), c3),
                  pl.BlockSpec((MIX_W, MIX_W), c2)],
        out_specs=[pl.BlockSpec((1, 1, MIX_W, MIX_W), lambda b, t: (b, t, 0, 0)),
                   pl.BlockSpec((1, 1, MIX_W, MIX_W), lambda b, t: (b, _bwd_chunk(t, ncc, nc), 0, 0))],
        out_shape=[st_shape, st_shape],
        scratch_shapes=[pltpu.VMEM((MIX_W, MIX_W), F32), pltpu.VMEM((MIX_W, MIX_W), F32)],
        compiler_params=_params("arbitrary", "arbitrary"),
        name="ret_state",
    )(p, p, cos, sin, cos, sin, perm, kd, cd, bd)
    return pl.pallas_call(
        _ret_out_kernel,
        grid=(nb, nc),
        in_specs=[pl.BlockSpec((1, CHUNK, RET_COLS), fwd),
                  pl.BlockSpec((CHUNK, MIX_W), tab_f), pl.BlockSpec((CHUNK, MIX_W), tab_f),
                  pl.BlockSpec((1, 1, MIX_W, MIX_W), lambda b, t: (b, t, 0, 0)),
                  pl.BlockSpec((1, 1, MIX_W, MIX_W), lambda b, t: (b, t, 0, 0)),
                  pl.BlockSpec((MIX_W, MIX_W), c2),
                  pl.BlockSpec((CHUNK, N_HEADS * CHUNK), c2),
                  pl.BlockSpec((2, CHUNK, MIX_W), c3),
                  pl.BlockSpec((MIX_W, MIX_W), c2)],
        out_specs=pl.BlockSpec((1, CHUNK, MIX_W), fwd),
        out_shape=jax.ShapeDtypeStruct((nb, nt, MIX_W), BF16),
        compiler_params=_params("arbitrary", "arbitrary"),
        name="ret_out",
    )(p, cos, sin, st_f, st_b, perm, dm, qd, ones_bd)


def _gelu_tanh(x):
    return 0.5 * x * (1.0 + jnp.tanh(math.sqrt(2.0 / math.pi) * (x + 0.044715 * (x * x * x))))


def _sgate_kernel(p_ref, ng_ref, w_ref, b_ref, y_ref):
    z = _gelu_tanh(p_ref[0].astype(F32))
    u = z[:, :MIX_W]
    v = z[:, MIX_W:]
    mu = jnp.mean(v, axis=-1, keepdims=True)
    vc = v - mu
    var = jnp.mean(vc * vc, axis=-1, keepdims=True)
    vn = vc * lax.rsqrt(var + EPS) * ng_ref[...]
    mixed = _dot(w_ref[...], _stack_heads(vn)) + b_ref[...]
    y_ref[0] = (u * mixed).astype(BF16)


def _sgate(p, ng, wcat, bias):
    nb, nt, _ = p.shape
    nc = nt // CHUNK
    c2 = lambda b, t: (0, 0)
    return pl.pallas_call(
        _sgate_kernel,
        grid=(nb, nc),
        in_specs=[pl.BlockSpec((1, CHUNK, SG_COLS), lambda b, t: (b, t, P_SG // SG_COLS)),
                  pl.BlockSpec((1, MIX_W), c2),
                  pl.BlockSpec((CHUNK, N_HEADS * CHUNK), c2),
                  pl.BlockSpec((CHUNK, MIX_W), c2)],
        out_specs=pl.BlockSpec((1, CHUNK, MIX_W), lambda b, t: (b, t, 0)),
        out_shape=jax.ShapeDtypeStruct((nb, nt, MIX_W), BF16),
        compiler_params=_params("arbitrary", "arbitrary"),
        name="sgate",
    )(p, ng, wcat, bias)


def _softplus(a):
    return jnp.maximum(a, 0.0) + jnp.log1p(jnp.exp(-jnp.abs(a)))


def _dn_prep_kernel(pc_ref, pp_ref, pn_ref, ab_ref, abc_ref, cw_ref, na_ref, dtb_ref, nar_ref, dtbr_ref, ones_ref,
                    qkv_ref, gb_ref, gbc_ref, xe_ref, *, ncc, nc):
    t = pl.program_id(1)
    w3 = 3 * MIX_W
    prev_ok = jnp.where((t != 0) & (t != ncc), 1.0, 0.0)
    next_ok = jnp.where((t != ncc - 1) & (t != nc - 1), 1.0, 0.0)
    tail = pp_ref[0, CHUNK - 16:CHUNK, 0:w3].astype(F32)
    head = pn_ref[0, 0:16, 0:w3].astype(F32)
    xe_ref[0:8, :] = tail[8:16, :] * prev_ok
    xe_ref[8:8 + CHUNK, :] = pc_ref[0, :, 0:w3].astype(F32)
    xe_ref[8 + CHUNK:16 + CHUNK, :] = head[0:8, :] * next_ok
    pad = CONV_W // 2
    y = xe_ref[8 - pad:8 - pad + CHUNK, :] * cw_ref[0:1, :]
    for i in range(1, CONV_W):
        y = y + xe_ref[8 - pad + i:8 - pad + i + CHUNK, :] * cw_ref[i:i + 1, :]
    y = y * jax.nn.sigmoid(y)
    q = y[:, 0:MIX_W]
    k = y[:, MIX_W:2 * MIX_W]
    v = y[:, 2 * MIX_W:w3]
    ones_bd = ones_ref[...]
    qn = q * lax.rsqrt(_head_sum(q * q, ones_bd) + EPS) * (HEAD_DIM ** -0.5)
    kn = k * lax.rsqrt(_head_sum(k * k, ones_bd) + EPS)
    qkv_ref[0, :, 0:MIX_W] = qn.astype(BF16)
    qkv_ref[0, :, MIX_W:2 * MIX_W] = kn.astype(BF16)
    qkv_ref[0, :, 2 * MIX_W:w3] = v.astype(BF16)
    ab = ab_ref[0]
    gb_ref[0, 0:8, :] = na_ref[...] * _softplus(ab[0:8, :] + dtb_ref[...])
    gb_ref[0, 8:16, :] = jax.nn.sigmoid(ab[8:16, :])
    abc = abc_ref[0]
    lane = lax.broadcasted_iota(jnp.int32, (1, 128), 1)
    g_c = nar_ref[...] * _softplus(abc + dtbr_ref[...])
    gbc_ref[0] = jnp.where(lane < 8, g_c, jnp.where(lane < 16, jax.nn.sigmoid(abc), 0.0))


def _split3(x):
    hi = x.astype(BF16)
    r = x - hi.astype(F32)
    mid = r.astype(BF16)
    lo = (r - mid.astype(F32)).astype(BF16)
    return hi, mid, lo


def _tri_inverse(mats, ii, jj):
    eye = jnp.where(ii == jj, 1.0, 0.0)
    nd = [jnp.where((ii // 16) == (jj // 16), n, 0.0) for n in mats]
    p1 = [_mm(x, x) for x in nd]
    m = [eye - x for x in nd]
    p2 = [_mm(x, x) for x in p1]
    m = [x + _mm(x, y) for x, y in zip(m, p1)]
    p3 = [_mm(x, x) for x in p2]
    m = [x + _mm(x, y) for x, y in zip(m, p2)]
    m = [x + _mm(x, y) for x, y in zip(m, p3)]
    for lvl in (16, 32, 64):
        off_mask = ((ii // (2 * lvl)) == (jj // (2 * lvl))) & ((ii // lvl) != (jj // lvl))
        t = [_mm(jnp.where(off_mask, n, 0.0), x) for n, x in zip(mats, m)]
        m = [x - _mm(x, y) for x, y in zip(m, t)]
    return m


def _dn_pre(qkv, g, gbc, d, lower):
    c = CHUNK
    qn = qkv[:, 0:MIX_W]
    kn = qkv[:, MIX_W:2 * MIX_W]
    v = qkv[:, 2 * MIX_W:3 * MIX_W]
    ii = lax.broadcasted_iota(jnp.int32, (c, c), 0)
    jj = lax.broadcasted_iota(jnp.int32, (c, c), 1)
    incl = (ii >= jj) if lower else (ii <= jj)
    tri = jnp.where(incl, 1.0, 0.0).astype(BF16)
    g_row = sum(_dot_nt(part, tri) for part in _split3(g))[N_HEADS * d:N_HEADS * (d + 1), :]
    cum = sum(_dot(tri, part) for part in _split3(gbc))
    g_col = cum[:, N_HEADS * d:N_HEADS * (d + 1)]
    b_col = gbc[:, 2 * N_HEADS + N_HEADS * d:2 * N_HEADS + N_HEADS * (d + 1)]
    g_cols4 = jnp.concatenate([jnp.broadcast_to(g_col[:, h:h + 1], (c, c)) for h in range(N_HEADS)], axis=1)
    b_cols4 = jnp.concatenate([jnp.broadcast_to(b_col[:, h:h + 1], (c, c)) for h in range(N_HEADS)], axis=1)
    g_rows4 = jnp.concatenate([g_row[h:h + 1, :] for h in range(N_HEADS)], axis=1)
    incl4 = jnp.concatenate([incl] * N_HEADS, axis=1)
    diag4 = jnp.concatenate([ii == jj] * N_HEADS, axis=1)
    decay = jnp.where(incl4, jnp.exp(jnp.where(incl4, g_cols4 - g_rows4, 0.0)), 0.0)
    kstack = _stack_heads(kn)
    kk = _dot_nt(kn, kstack)
    qk = _dot_nt(qn, kstack)
    n_mat = jnp.where(diag4, 0.0, decay * kk * b_cols4)
    attn = (decay * qk).astype(BF16)
    g256 = _expand_heads(g_col)
    eg256 = jnp.exp(g256)
    b256 = _expand_heads(b_col)
    vb = v.astype(F32) * b256
    kbg = kn.astype(F32) * b256 * eg256
    rhs = jnp.concatenate([_stack_heads(vb), _stack_heads(kbg)], axis=1)
    g_last = g256[c - 1:c, :] if lower else g256[0:1, :]
    kdec = (kn.astype(F32) * jnp.exp(g_last - g256)).astype(BF16)
    n_heads = [n_mat[:, h * c:(h + 1) * c] for h in range(N_HEADS)]
    return n_heads, dict(qn=qn, attn=attn, rhs=rhs, eg=eg256, kdec=kdec, sdec=jnp.exp(g_last))


def _dn_post(z, st, bd):
    uw = _dot(z["ainv"], z["rhs"])
    u = uw[:, 0:MIX_W]
    wk = uw[:, MIX_W:2 * MIX_W]
    s_prev = st[...]
    s_bf = s_prev.astype(BF16)
    w = u - _dot(wk.astype(BF16), s_bf)
    o = z["eg"] * _dot(z["qn"], s_bf) + _dot(z["attn"], _stack_heads(w))
    st[...] = z["sdec"] * s_prev + bd * _dot_tn(z["kdec"], w.astype(BF16))
    return o


_DN_STAGED = ("ainv", "rhs", "attn", "qn", "eg", "kdec", "sdec")


def _dn_scan_kernel(qf_ref, qb_ref, gf_ref, gb_ref, gcf_ref, gcb_ref, bd_ref, of_ref, ob_ref, st_f, st_b, *staged):
    t = pl.program_id(1)
    stage = dict(zip(_DN_STAGED, staged))

    @pl.when(t == 0)
    def _():
        st_f[...] = jnp.zeros_like(st_f)
        st_b[...] = jnp.zeros_like(st_b)
        for ref in staged:
            ref[...] = jnp.zeros_like(ref)

    bd = bd_ref[...]
    rd = 2 * ((t + 1) % 2)
    wr = 2 * (t % 2)
    prev = [{k: (stage[k][rd + d, 0:1, :] if k == "sdec" else stage[k][rd + d]) for k in _DN_STAGED}
            for d in range(2)]
    nf, pre_f = _dn_pre(qf_ref[0], gf_ref[0], gcf_ref[0], 0, True)
    nb_, pre_b = _dn_pre(qb_ref[0], gb_ref[0], gcb_ref[0], 1, False)
    ii = lax.broadcasted_iota(jnp.int32, (CHUNK, CHUNK), 0)
    jj = lax.broadcasted_iota(jnp.int32, (CHUNK, CHUNK), 1)
    inv = _tri_inverse(nf + nb_, ii, jj)
    of_ref[0] = _dn_post(prev[0], st_f, bd)
    ob_ref[0] = _dn_post(prev[1], st_b, bd)
    for d, pre in enumerate((pre_f, pre_b)):
        pre["ainv"] = jnp.concatenate(inv[N_HEADS * d:N_HEADS * (d + 1)], axis=1).astype(BF16)
        for k in _DN_STAGED:
            if k == "sdec":
                stage[k][wr + d, 0:1, :] = pre[k]
            else:
                stage[k][wr + d] = pre[k]


def _deltanet(p, ab_t, ab_c, conv_w, neg_a, dtb, bd, ones_bd, ncc):
    nb, nt, _ = p.shape
    nc = nt // CHUNK
    w3 = 3 * MIX_W
    c2 = lambda b, t: (0, 0)
    dn_blk = P_DN // RET_COLS
    pad_lanes = lambda col: jnp.concatenate([col.reshape(1, -1), jnp.zeros((1, 128 - col.size), F32)], axis=1)
    qkv, gbeta, gbeta_c = pl.pallas_call(
        functools.partial(_dn_prep_kernel, ncc=ncc, nc=nc),
        grid=(nb, nc),
        in_specs=[pl.BlockSpec((1, CHUNK, 4 * MIX_W), lambda b, t: (b, t, dn_blk)),
                  pl.BlockSpec((1, CHUNK, 4 * MIX_W), lambda b, t: (b, jnp.maximum(t - 1, 0), dn_blk)),
                  pl.BlockSpec((1, CHUNK, 4 * MIX_W), lambda b, t: (b, jnp.minimum(t + 1, nc - 1), dn_blk)),
                  pl.BlockSpec((1, 16, CHUNK), lambda b, t: (b, 0, t)),
                  pl.BlockSpec((1, CHUNK, 128), lambda b, t: (b, t, 0)),
                  pl.BlockSpec((8, w3), c2),
                  pl.BlockSpec((8, 1), c2),
                  pl.BlockSpec((8, 1), c2),
                  pl.BlockSpec((1, 128), c2),
                  pl.BlockSpec((1, 128), c2),
                  pl.BlockSpec((MIX_W, MIX_W), c2)],
        out_specs=[pl.BlockSpec((1, CHUNK, w3), lambda b, t: (b, t, 0)),
                   pl.BlockSpec((1, 16, CHUNK), lambda b, t: (b, 0, t)),
                   pl.BlockSpec((1, CHUNK, 128), lambda b, t: (b, t, 0))],
        out_shape=[jax.ShapeDtypeStruct((nb, nt, w3), BF16),
                   jax.ShapeDtypeStruct((nb, 16, nt), F32),
                   jax.ShapeDtypeStruct((nb, nt, 128), F32)],
        scratch_shapes=[pltpu.VMEM((CHUNK + 16, w3), F32)],
        compiler_params=_params("arbitrary", "arbitrary"),
        name="dn_prep",
    )(p, p, p, ab_t, ab_c, conv_w, neg_a, dtb, pad_lanes(neg_a), pad_lanes(dtb), ones_bd)
    cur_f = lambda t: jnp.minimum(t, nc - 1)
    cur_b = lambda t: _bwd_chunk(jnp.minimum(t, nc - 1), ncc, nc)
    out_f = lambda t: jnp.maximum(t - 1, 0)
    out_b = lambda t: _bwd_chunk(jnp.maximum(t - 1, 0), ncc, nc)
    o_shape = jax.ShapeDtypeStruct((nb, nt, MIX_W), F32)
    c4 = N_HEADS * CHUNK
    return pl.pallas_call(
        _dn_scan_kernel,
        grid=(nb, nc + 1),
        in_specs=[pl.BlockSpec((1, CHUNK, w3), lambda b, t: (b, cur_f(t), 0)),
                  pl.BlockSpec((1, CHUNK, w3), lambda b, t: (b, cur_b(t), 0)),
                  pl.BlockSpec((1, 16, CHUNK), lambda b, t: (b, 0, cur_f(t))),
                  pl.BlockSpec((1, 16, CHUNK), lambda b, t: (b, 0, cur_b(t))),
                  pl.BlockSpec((1, CHUNK, 128), lambda b, t: (b, cur_f(t), 0)),
                  pl.BlockSpec((1, CHUNK, 128), lambda b, t: (b, cur_b(t), 0)),
                  pl.BlockSpec((MIX_W, MIX_W), c2)],
        out_specs=[pl.BlockSpec((1, CHUNK, MIX_W), lambda b, t: (b, out_f(t), 0)),
                   pl.BlockSpec((1, CHUNK, MIX_W), lambda b, t: (b, out_b(t), 0))],
        out_shape=[o_shape, o_shape],
        scratch_shapes=[pltpu.VMEM((MIX_W, MIX_W), F32), pltpu.VMEM((MIX_W, MIX_W), F32),
                        pltpu.VMEM((4, CHUNK, c4), BF16),
                        pltpu.VMEM((4, c4, 2 * MIX_W), BF16),
                        pltpu.VMEM((4, CHUNK, c4), BF16),
                        pltpu.VMEM((4, CHUNK, MIX_W), BF16),
                        pltpu.VMEM((4, CHUNK, MIX_W), F32),
                        pltpu.VMEM((4, CHUNK, MIX_W), BF16),
                        pltpu.VMEM((4, 8, MIX_W), F32)],
        compiler_params=_params("arbitrary", "arbitrary"),
        name="dn_scan",
    )(qkv, qkv, gbeta, gbeta, gbeta_c, gbeta_c, bd)


QK_W = 256


VT_ROWS = 144


def _mla_prep_kernel(p_ref, c_ref, s_ref, perm_ref, qg_ref, kg_ref, wqn_ref, wqr_ref, wa_ref, selq_ref, selc_ref,
                     selr_ref, selv_ref, one_ref, qt_ref, kv_ref, vt_ref, *, scale):
    p = p_ref[0]
    cos, sin, perm = c_ref[...], s_ref[...], perm_ref[...]
    cq = p[:, 0:Q_LORA].astype(F32)
    cqn = (cq * lax.rsqrt(jnp.mean(cq * cq, axis=-1, keepdims=True) + EPS) * qg_ref[...]).astype(BF16)
    q_nope = _dot(cqn, wqn_ref[...]).astype(BF16)
    q_rope = _dot(cqn, wqr_ref[...]).astype(BF16)
    q_rot = (_rot(q_rope, cos, sin, perm) * scale).astype(BF16)
    q_nope_s = (q_nope.astype(F32) * scale).astype(BF16)
    for h in range(N_HEADS):
        qt_ref[0, h] = (_dot_nt(wa_ref[h], q_nope_s) + _dot_nt(selq_ref[h], q_rot)).astype(BF16)
    ckv = p[:, Q_LORA:Q_LORA + KV_LORA].astype(F32)
    ckvn = (ckv * lax.rsqrt(jnp.mean(ckv * ckv, axis=-1, keepdims=True) + EPS) * kg_ref[...]).astype(BF16)
    kr = p[:, Q_LORA + KV_LORA:MLA_PAD]
    kr_rot = _rot(kr, cos, sin, perm).astype(BF16)
    kv_ref[0] = (_dot(ckvn, selc_ref[...]) + _dot(kr_rot, selr_ref[...])).astype(BF16)
    vt_ref[0] = (_dot_nt(selv_ref[...], ckvn) + one_ref[...]).astype(BF16)


def _mla_attn_kernel(qt_ref, kv_ref, vt_ref, wuv_ref, y_ref, m_ref, acc_ref, s_ref, *, tq, tk, n_ctx, nt):
    i = pl.program_id(1)
    heads = range(N_HEADS)
    m_ref[...] = jnp.full_like(m_ref, -jnp.inf)
    acc_ref[...] = jnp.zeros_like(acc_ref)

    def scores(j0, size, slot):
        k = kv_ref[0, pl.ds(j0, size), :]
        for h in heads:
            s_ref[slot, h, 0:size, :] = _dot(k, qt_ref[0, h])

    def softmax_pv(j0, size, slot):
        vt = vt_ref[0, :, pl.ds(j0, size)]
        s = [s_ref[slot, h, 0:size, :] for h in heads]
        m_old = [m_ref[h] for h in heads]
        m_new = [jnp.maximum(m_old[h], jnp.max(s[h], axis=0, keepdims=True)) for h in heads]
        pr = [jnp.exp2(s[h] - m_new[h]).astype(BF16) for h in heads]
        pv = [_dot(vt, pr[h]) for h in heads]
        for h in heads:
            acc_ref[h] = jnp.exp2(m_old[h] - m_new[h]) * acc_ref[h] + pv[h]
            m_ref[h] = m_new[h]

    scores(0, n_ctx, 0)
    is_latent = (i + 1) * tq > n_ctx

    @pl.when(jnp.logical_not(is_latent))
    def _():
        softmax_pv(0, n_ctx, 0)

    @pl.when(is_latent)
    def _():
        n_tiles = (nt - n_ctx) // tk
        last = n_ctx + (n_tiles - 1) * tk
        scores(n_ctx, tk, 1)
        softmax_pv(0, n_ctx, 0)

        def body(jj, carry):
            t0 = pl.multiple_of(n_ctx + 2 * jj * tk, 256)
            t1 = pl.multiple_of(jnp.minimum(t0 + tk, last), 256)
            t2 = pl.multiple_of(jnp.minimum(t0 + 2 * tk, last), 256)
            scores(t1, tk, 0)
            softmax_pv(t0, tk, 1)
            scores(t2, tk, 1)
            softmax_pv(t1, tk, 0)
            return carry

        lax.fori_loop(0, n_tiles // 2, body, 0)
        if n_tiles % 2:
            softmax_pv(last, tk, 1)

    y = None
    for h in range(N_HEADS):
        acc = acc_ref[h]
        o = (acc[0:KV_LORA, :] / acc[KV_LORA:KV_LORA + 1, :]).astype(BF16)
        term = _dot_tn(o, wuv_ref[h])
        y = term if y is None else y + term
    y_ref[0] = y.astype(BF16)


def _mla(p, cos, sin, perm, qg, kg, wqn, wqr, wa, selq, selc, selr, selv, one_col, wuv, n_ctx):
    nb, nt, _ = p.shape
    tm = _pick(nt, (768, 384, 256, 128))
    scale = (NOPE_DIM + ROPE_DIM) ** -0.5 * math.log2(math.e)
    c2 = lambda b, i: (0, 0)
    c3 = lambda b, i: (0, 0, 0)
    qt, kv, vt = pl.pallas_call(
        functools.partial(_mla_prep_kernel, scale=scale),
        grid=(nb, nt // tm),
        in_specs=[pl.BlockSpec((1, tm, MLA_PAD), lambda b, i: (b, i, P_MLA // MLA_PAD)),
                  pl.BlockSpec((tm, 128), lambda b, i: (i, 0)),
                  pl.BlockSpec((tm, 128), lambda b, i: (i, 0)),
                  pl.BlockSpec((128, 128), c2),
                  pl.BlockSpec((1, Q_LORA), c2),
                  pl.BlockSpec((1, KV_LORA), c2),
                  pl.BlockSpec((Q_LORA, N_HEADS * NOPE_DIM), c2),
                  pl.BlockSpec((Q_LORA, N_HEADS * ROPE_DIM), c2),
                  pl.BlockSpec((N_HEADS, QK_W, N_HEADS * NOPE_DIM), c3),
                  pl.BlockSpec((N_HEADS, QK_W, N_HEADS * ROPE_DIM), c3),
                  pl.BlockSpec((KV_LORA, QK_W), c2),
                  pl.BlockSpec((128, QK_W), c2),
                  pl.BlockSpec((VT_ROWS, KV_LORA), c2),
                  pl.BlockSpec((VT_ROWS, 1), c2)],
        out_specs=[pl.BlockSpec((1, N_HEADS, QK_W, tm), lambda b, i: (b, 0, 0, i)),
                   pl.BlockSpec((1, tm, QK_W), lambda b, i: (b, i, 0)),
                   pl.BlockSpec((1, VT_ROWS, tm), lambda b, i: (b, 0, i))],
        out_shape=[jax.ShapeDtypeStruct((nb, N_HEADS, QK_W, nt), BF16),
                   jax.ShapeDtypeStruct((nb, nt, QK_W), BF16),
                   jax.ShapeDtypeStruct((nb, VT_ROWS, nt), BF16)],
        compiler_params=_params("arbitrary", "arbitrary"),
        name="mla_prep",
    )(p, cos, sin, perm, qg, kg, wqn, wqr, wa, selq, selc, selr, selv, one_col)
    tq = 256
    tk = _pick(nt - n_ctx, (512, 256))
    return pl.pallas_call(
        functools.partial(_mla_attn_kernel, tq=tq, tk=tk, n_ctx=n_ctx, nt=nt),
        grid=(nb, nt // tq),
        in_specs=[pl.BlockSpec((1, N_HEADS, QK_W, tq), lambda b, i: (b, 0, 0, i)),
                  pl.BlockSpec((1, nt, QK_W), lambda b, i: (b, 0, 0)),
                  pl.BlockSpec((1, VT_ROWS, nt), lambda b, i: (b, 0, 0)),
                  pl.BlockSpec((N_HEADS, KV_LORA, MIX_W), c3)],
        out_specs=pl.BlockSpec((1, tq, MIX_W), lambda b, i: (b, i, 0)),
        out_shape=jax.ShapeDtypeStruct((nb, nt, MIX_W), BF16),
        scratch_shapes=[pltpu.VMEM((N_HEADS, 1, tq), F32), pltpu.VMEM((N_HEADS, VT_ROWS, tq), F32),
                        pltpu.VMEM((2, N_HEADS, max(tk, n_ctx), tq), F32)],
        compiler_params=_params("arbitrary", "arbitrary"),
        name="mla_attn",
    )(qt, kv, vt, wuv)


def _merge_kernel(x_ref, yr_ref, ys_ref, of_ref, ob_ref, ym_ref, z_ref, g0_ref, g1_ref, g2_ref, g3_ref,
                  wb_ref, wo_ref, ng_ref, gp_ref, ml_ref, mc_ref, ones_ref, o_ref, *, tm, n_ctx):
    i = pl.program_id(1)
    od = of_ref[0] + ob_ref[0]
    ms = _head_sum(od * od, ones_ref[...]) * (1.0 / HEAD_DIM)
    z = z_ref[0].astype(F32)
    ydn = (od * lax.rsqrt(ms + EPS) * ng_ref[...]) * (z * jax.nn.sigmoid(z))
    ys = (yr_ref[0], ys_ref[0], ydn.astype(BF16), ym_ref[0])
    gates = (g0_ref, g1_ref, g2_ref, g3_ref)
    acc = None
    for b in range(N_BRANCH):
        term = jax.nn.sigmoid(gates[b][0].astype(F32)) * _dot(ys[b], wb_ref[b])
        acc = term if acc is None else acc + term
    y = _dot(acc.astype(BF16), wo_ref[...])
    r = y * lax.rsqrt(jnp.mean(y * y, axis=-1, keepdims=True) + EPS) * gp_ref[...]
    rows = lax.broadcasted_iota(jnp.int32, (tm, 1), 0) + i * tm
    gate = jnp.where(rows < n_ctx, mc_ref[0, 2:3, :], ml_ref[0, 2:3, :])
    o_ref[0] = x_ref[0] + gate * r


def _merge(xa, y_ret, y_sg, o_f, o_b, y_mla, p, wb, wo, ng, gp, mod, ones_bd, n_ctx):
    nb, nt, d = xa.shape
    tm = _pick(nt, (768, 384, 256, 128))
    row = lambda b, i: (b, i, 0)
    c2 = lambda b, i: (0, 0)
    y_spec = pl.BlockSpec((1, tm, MIX_W), row)
    gate_specs = [pl.BlockSpec((1, tm, d), functools.partial(lambda b, i, k: (b, i, k), k=P_GATE // d + k))
                  for k in range(N_BRANCH)]
    return pl.pallas_call(
        functools.partial(_merge_kernel, tm=tm, n_ctx=n_ctx),
        grid=(nb, nt // tm),
        in_specs=[pl.BlockSpec((1, tm, d), row), y_spec, y_spec, y_spec, y_spec, y_spec,
                  pl.BlockSpec((1, tm, MIX_W), lambda b, i: (b, i, (P_DN + 3 * MIX_W) // MIX_W)),
                  *gate_specs,
                  pl.BlockSpec((N_BRANCH, MIX_W, d), lambda b, i: (0, 0, 0)),
                  pl.BlockSpec((d, d), c2),
                  pl.BlockSpec((1, MIX_W), c2),
                  pl.BlockSpec((1, d), c2),
                  pl.BlockSpec((1, 6, d), lambda b, i: (b, 0, 0)),
                  pl.BlockSpec((1, 6, d), lambda b, i: (nb, 0, 0)),
                  pl.BlockSpec((MIX_W, MIX_W), c2)],
        out_specs=pl.BlockSpec((1, tm, d), row),
        out_shape=jax.ShapeDtypeStruct((nb, nt, d), F32),
        compiler_params=_params("arbitrary", "arbitrary"),
        name="merge",
    )(xa, y_ret, y_sg, o_f, o_b, y_mla, p, p, p, p, p, wb, wo, ng, gp, mod, mod, ones_bd)


def _route(sel, aff):
    rows = [sel[e:e + 1, :] for e in range(N_EXPERTS)]
    pairs = [(a, b) for a in range(EXPERTS_PER_GROUP) for b in range(a + 1, EXPERTS_PER_GROUP)]
    grp_score, grp_pair = [], []
    for g in range(N_GROUPS):
        base = g * EXPERTS_PER_GROUP
        best = rows[base + pairs[0][0]] + rows[base + pairs[0][1]]
        best_p = jnp.zeros_like(best, dtype=jnp.int32)
        for pi in range(1, len(pairs)):
            s = rows[base + pairs[pi][0]] + rows[base + pairs[pi][1]]
            take = s > best
            best = jnp.where(take, s, best)
            best_p = jnp.where(take, pi, best_p)
        grp_score.append(best)
        grp_pair.append(best_p)
    top = grp_score[0]
    top_g = jnp.zeros_like(grp_pair[0])
    top_p = grp_pair[0]
    for g in range(1, N_GROUPS):
        take = grp_score[g] > top
        top = jnp.where(take, grp_score[g], top)
        top_g = jnp.where(take, g, top_g)
        top_p = jnp.where(take, grp_pair[g], top_p)
    picked = []
    for e in range(N_EXPERTS):
        g, k = divmod(e, EXPERTS_PER_GROUP)
        in_pair = None
        for pi, (a, b) in enumerate(pairs):
            if k in (a, b):
                hit = top_p == pi
                in_pair = hit if in_pair is None else (in_pair | hit)
        picked.append(jnp.where((top_g == g) & in_pair, aff[e:e + 1, :], 0.0))
    denom = picked[0]
    for e in range(1, N_EXPERTS):
        denom = denom + picked[e]
    return [pk / denom for pk in picked]


def _moe_kernel(x_ref, ml_ref, mc_ref, g2_ref, gp_ref, rw_ref, rb_ref, w1_ref, w3_ref, w2_ref, o_ref,
                hn_ref, comb_t_ref, comb_ref, acc_ref, *, tm, rb, n_ctx):
    i = pl.program_id(1)
    e = pl.program_id(2)

    @pl.when(e == 0)
    def _():
        def blk(r, carry):
            r0 = pl.multiple_of(r * rb, rb)
            x = x_ref[0, pl.ds(r0, rb), :]
            rows = lax.broadcasted_iota(jnp.int32, (rb, 1), 0) + (i * tm + r0)
            hn = _norm_modulate(x, g2_ref[...], rows < n_ctx, mc_ref, ml_ref, 3, 4)
            hn_ref[pl.ds(r0, rb), :] = hn.astype(BF16)
            logit = _dot_nt(rw_ref[...], hn.astype(BF16))
            aff = jax.nn.sigmoid(logit)
            comb = _route(aff + rb_ref[...], aff)
            for k in range(N_EXPERTS):
                comb_t_ref[k:k + 1, pl.ds(r0, rb)] = comb[k]
            return carry

        comb_t_ref[...] = jnp.zeros_like(comb_t_ref)
        lax.fori_loop(0, tm // rb, blk, 0)
        comb_t_ref[N_EXPERTS:N_EXPERTS + 1, :] = jnp.ones((1, tm), F32)
        comb_ref[...] = comb_t_ref[...].T
        acc_ref[...] = jnp.zeros_like(acc_ref)

    hn = hn_ref[...]
    a = _dot(hn, w1_ref[0])
    h = (a * jax.nn.sigmoid(a)) * _dot(hn, w3_ref[0])
    lane = lax.broadcasted_iota(jnp.int32, (1, 128), 1)
    ce = jnp.sum(jnp.where(lane == e, comb_ref[...], 0.0), axis=-1, keepdims=True)
    acc_ref[...] += _dot((h * ce).astype(BF16), w2_ref[0])

    @pl.when(e == pl.num_programs(2) - 1)
    def _():
        y = acc_ref[...]
        r = y * lax.rsqrt(jnp.mean(y * y, axis=-1, keepdims=True) + EPS) * gp_ref[...]
        rows = lax.broadcasted_iota(jnp.int32, (tm, 1), 0) + i * tm
        gate = jnp.where(rows < n_ctx, mc_ref[0, 5:6, :], ml_ref[0, 5:6, :])
        o_ref[0] = x_ref[0] + gate * r


def _moe(xa, mod, g2, gp, rw_t, rbias, w1, w3, w2, n_ctx):
    nb, nt, d = xa.shape
    tm = _pick(nt, (768, 384, 256, 128))
    ne = w1.shape[0]
    row = lambda b, i, e: (b, i, 0)
    c2 = lambda b, i, e: (0, 0)
    return pl.pallas_call(
        functools.partial(_moe_kernel, tm=tm, rb=128, n_ctx=n_ctx),
        grid=(nb, nt // tm, ne),
        in_specs=[pl.BlockSpec((1, tm, d), row),
                  pl.BlockSpec((1, 6, d), lambda b, i, e: (b, 0, 0)),
                  pl.BlockSpec((1, 6, d), lambda b, i, e: (nb, 0, 0)),
                  pl.BlockSpec((1, d), c2),
                  pl.BlockSpec((1, d), c2),
                  pl.BlockSpec((N_EXPERTS, d), c2),
                  pl.BlockSpec((N_EXPERTS, 1), c2),
                  pl.BlockSpec((1, d, D_EXPERT), lambda b, i, e: (e, 0, 0)),
                  pl.BlockSpec((1, d, D_EXPERT), lambda b, i, e: (e, 0, 0)),
                  pl.BlockSpec((1, D_EXPERT, d), lambda b, i, e: (e, 0, 0))],
        out_specs=pl.BlockSpec((1, tm, d), row),
        out_shape=jax.ShapeDtypeStruct((nb, nt, d), F32),
        scratch_shapes=[pltpu.VMEM((tm, d), BF16), pltpu.VMEM((128, tm), F32), pltpu.VMEM((tm, 128), F32),
                        pltpu.VMEM((tm, d), F32)],
        compiler_params=_params("arbitrary", "arbitrary", "arbitrary"),
        name="moe",
    )(xa, mod, mod, g2, gp, rw_t, rbias, w1, w3, w2)


def _swap_perm(width, group):
    j = jnp.arange(width)
    src = jnp.where((j % group) < group // 2, j + group // 2, j - group // 2)
    return (jnp.arange(width)[:, None] == src[None, :]).astype(BF16)


def _rope_tables(n_lat, n_ctx):
    def angles(pos, dim):
        half = dim // 2
        inv = ROPE_BASE ** (-jnp.arange(half, dtype=F32) / half)
        return pos.astype(F32)[:, None] * inv[None, :]

    def tables(cos_parts, sin_parts, reps):
        cos = jnp.tile(jnp.concatenate(cos_parts, axis=-1), (1, reps))
        sin = jnp.tile(jnp.concatenate(sin_parts, axis=-1), (1, reps))
        w = cos.shape[1]
        return (jnp.concatenate([jnp.ones((n_ctx, w), F32), cos], axis=0),
                jnp.concatenate([jnp.zeros((n_ctx, w), F32), sin], axis=0))

    rows = n_lat // GRID_W
    ang_t = angles(jnp.arange(n_lat), HEAD_DIM)
    ang_r = angles(jnp.repeat(jnp.arange(rows), GRID_W), ROPE_DIM // 2)
    ang_c = angles(jnp.tile(jnp.arange(GRID_W), rows), ROPE_DIM // 2)
    ct, st = jnp.cos(ang_t), jnp.sin(ang_t)
    ret = tables([ct, ct], [-st, st], N_HEADS)
    cr, sr, cc, sc = jnp.cos(ang_r), jnp.sin(ang_r), jnp.cos(ang_c), jnp.sin(ang_c)
    mla = tables([cr, cr, cc, cc], [-sr, sr, -sc, sc], N_HEADS)
    return ret, mla


def _ret_tables(logit):
    log_g = jax.nn.log_sigmoid(logit.astype(F32))
    lane_lg = jnp.repeat(log_g, HEAD_DIM, axis=1)
    idx = jnp.arange(CHUNK, dtype=F32)[:, None]
    kd = jnp.stack([jnp.exp(lane_lg[0][None, :] * (CHUNK - 1 - idx)), jnp.exp(lane_lg[1][None, :] * idx)])
    qd = jnp.stack([jnp.exp(lane_lg[0][None, :] * (idx + 1)), jnp.exp(lane_lg[1][None, :] * (CHUNK - idx))])
    cd = jnp.exp(lane_lg * CHUNK)[:, None, :]
    diff = idx - idx.T
    blocks = []
    for h in range(N_HEADS):
        f = jnp.exp(log_g[0, h] * jnp.where(diff >= 0, diff, 0.0))
        b = jnp.exp(log_g[1, h] * jnp.where(diff < 0, -diff, 0.0))
        blocks.append(jnp.where(diff >= 0, f, b))
    dm = jnp.concatenate(blocks, axis=1)
    return kd, cd, qd, dm


def _pack_w_in(w_in):
    d = w_in.shape[0]
    mla = jnp.concatenate([w_in[:, OFF_MLA:OFF_MLA + MLA_COLS], jnp.zeros((d, MLA_PAD - MLA_COLS), w_in.dtype)], 1)
    w = jnp.concatenate([w_in[:, OFF_RET:OFF_RET + RET_COLS], w_in[:, OFF_DN:OFF_DN + 4 * MIX_W],
                         w_in[:, OFF_SG:OFF_SG + SG_COLS], mla, w_in[:, OFF_GATE:OFF_GATE + GATE_COLS]], axis=1)
    wab = w_in[:, OFF_DN + 4 * MIX_W:OFF_DN + DN_COLS]
    wabc = jnp.concatenate([wab, jnp.zeros((d, 128 - 4 * N_HEADS), w_in.dtype)], axis=1)
    return w.astype(BF16), wab.T.astype(BF16), wabc.astype(BF16)


def _mla_weights(w_uq, w_ukv):
    dq = NOPE_DIM + ROPE_DIM
    dkv = NOPE_DIM + V_DIM
    wq = w_uq.reshape(Q_LORA, N_HEADS, dq)
    wqn = wq[:, :, :NOPE_DIM].reshape(Q_LORA, N_HEADS * NOPE_DIM)
    wqr = wq[:, :, NOPE_DIM:].reshape(Q_LORA, N_HEADS * ROPE_DIM)
    wkv = w_ukv.reshape(KV_LORA, N_HEADS, dkv)
    wa = jnp.zeros((N_HEADS, QK_W, N_HEADS * NOPE_DIM), F32)
    selq = jnp.zeros((N_HEADS, QK_W, N_HEADS * ROPE_DIM), F32)
    wuv = jnp.zeros((N_HEADS, KV_LORA, MIX_W), F32)
    for h in range(N_HEADS):
        wa = wa.at[h, 0:KV_LORA, h * NOPE_DIM:(h + 1) * NOPE_DIM].set(wkv[:, h, :NOPE_DIM])
        selq = selq.at[h, KV_LORA:KV_LORA + ROPE_DIM, h * ROPE_DIM:(h + 1) * ROPE_DIM].set(jnp.eye(ROPE_DIM))
        wuv = wuv.at[h, :, h * V_DIM:(h + 1) * V_DIM].set(wkv[:, h, NOPE_DIM:])
    selc = jnp.zeros((KV_LORA, QK_W), F32).at[:, 0:KV_LORA].set(jnp.eye(KV_LORA))
    selr = jnp.zeros((128, QK_W), F32).at[0:ROPE_DIM, KV_LORA:KV_LORA + ROPE_DIM].set(jnp.eye(ROPE_DIM))
    selv = jnp.zeros((VT_ROWS, KV_LORA), F32).at[0:KV_LORA, :].set(jnp.eye(KV_LORA))
    one_col = jnp.zeros((VT_ROWS, 1), F32).at[KV_LORA, 0].set(1.0)
    return tuple(a.astype(BF16) for a in (wqn, wqr, wa, selq, selc, selr, selv)) + (one_col, wuv.astype(BF16))


def kernel(x, c, ctx, c_ctx, w_ada, b_ada, g_pre1, g_post1, g_pre2, g_post2, w_in, ret_decay_logit, sg_norm_g, sg_w, sg_b, dn_conv_w, dn_A_log, dn_dt_bias, dn_norm_g, mla_q_norm_g, mla_kv_norm_g, mla_w_uq, mla_w_ukv, w_branch, w_out, router_w, router_bias, moe_w1, moe_w3, moe_w2, shared_w1, shared_w3, shared_w2):
    nb, n_lat, d = x.shape
    n_ctx = ctx.shape[1]
    depth = w_in.shape[0]
    assert d == D_MODEL and n_lat % GRID_W == 0 and n_lat % CHUNK == 0 and n_ctx % 256 == 0
    ncc = n_ctx // CHUNK

    n_cond = -(-(nb + 1) // 8) * 8
    cond = jnp.concatenate([c, c_ctx[None], jnp.zeros((n_cond - nb - 1, d), F32)], axis=0)
    mod_all = _adaln(cond, w_ada, b_ada).reshape(depth, n_cond, 6, d)

    (ret_cos, ret_sin), (mla_cos, mla_sin) = _rope_tables(n_lat, n_ctx)
    perm_ret = _swap_perm(MIX_W, HEAD_DIM)
    perm_mla = _swap_perm(N_HEADS * ROPE_DIM, ROPE_DIM // 2)
    lane_head = jnp.arange(MIX_W) // HEAD_DIM
    bd = (lane_head[:, None] == lane_head[None, :]).astype(F32)
    ones_bd = bd.astype(BF16)
    rw_t = router_w.T.astype(BF16)
    rbias = router_bias.astype(F32)[:, None]

    xa = jnp.concatenate([ctx, x], axis=1)
    for l in range(depth):
        mod = mod_all[l]
        w_l, wab_l, wabc_l = _pack_w_in(w_in[l])
        p, ab_t, ab_c = _inproj(xa, mod, g_pre1[l][None], w_l, wab_l, wabc_l, n_ctx)

        kd, cd, qd, dm = _ret_tables(ret_decay_logit[l])
        y_ret = _retention(p, ret_cos, ret_sin, perm_ret, (kd, cd, qd, dm, bd, ones_bd), ncc)

        wcat = jnp.concatenate([sg_w[l, h] for h in range(N_HEADS)], axis=1).astype(BF16)
        sg_bias = jnp.repeat(sg_b[l].T, HEAD_DIM, axis=1)
        y_sg = _sgate(p, sg_norm_g[l][None], wcat, sg_bias)

        neg_a = (-jnp.exp(dn_A_log[l].astype(F32))).reshape(2 * N_HEADS, 1)
        dtb = dn_dt_bias[l].astype(F32).reshape(2 * N_HEADS, 1)
        conv_w = jnp.concatenate([dn_conv_w[l], jnp.zeros((8 - CONV_W, 3 * MIX_W), F32)], axis=0)
        o_f, o_b = _deltanet(p, ab_t, ab_c, conv_w, neg_a, dtb, bd, ones_bd, ncc)

        y_mla = _mla(p, mla_cos, mla_sin, perm_mla, mla_q_norm_g[l][None], mla_kv_norm_g[l][None],
                     *_mla_weights(mla_w_uq[l], mla_w_ukv[l]), n_ctx)

        xa = _merge(xa, y_ret, y_sg, o_f, o_b, y_mla, p, w_branch[l].astype(BF16), w_out[l].astype(BF16),
                    jnp.tile(dn_norm_g[l], N_HEADS)[None], g_post1[l][None], mod, ones_bd, n_ctx)

        w1 = jnp.concatenate([moe_w1[l], shared_w1[l][None]], axis=0).astype(BF16)
        w3 = jnp.concatenate([moe_w3[l], shared_w3[l][None]], axis=0).astype(BF16)
        w2 = jnp.concatenate([moe_w2[l], shared_w2[l][None]], axis=0).astype(BF16)
        xa = _moe(xa, mod, g_pre2[l][None], g_post2[l][None], rw_t, rbias, w1, w3, w2, n_ctx)
    return xa[:, n_ctx:]
```

```python
import functools
import math

import jax
import jax.numpy as jnp
from jax import lax
from jax.experimental import pallas as pl
from jax.experimental.pallas import tpu as pltpu

F32 = jnp.float32
BF16 = jnp.bfloat16
HIGHEST = lax.Precision.HIGHEST

D_MODEL = 1024
GRID_W = 64
N_HEADS = 4
HEAD_DIM = 64
MIX_W = N_HEADS * HEAD_DIM
CHUNK = 128
ROPE_BASE = 10000.0
EPS = 1e-6
RET_DECAY_EXP0 = 5.0
CONV_W = 5
Q_LORA = 256
KV_LORA = 128
NOPE_DIM = 64
ROPE_DIM = 32
V_DIM = 64
N_EXPERTS = 16
N_GROUPS = 4
EXPERTS_PER_GROUP = N_EXPERTS // N_GROUPS
D_EXPERT = 256
N_BRANCH = 4

RET_COLS = 4 * MIX_W
SG_COLS = 2 * MIX_W
DN_COLS = 4 * MIX_W + 4 * N_HEADS
MLA_COLS = Q_LORA + KV_LORA + ROPE_DIM
GATE_COLS = N_BRANCH * D_MODEL
OFF_RET = 0
OFF_SG = OFF_RET + RET_COLS
OFF_DN = OFF_SG + SG_COLS
OFF_MLA = OFF_DN + DN_COLS
OFF_GATE = OFF_MLA + MLA_COLS

P_RET = 0
P_DN = 1024
P_SG = 2048
P_MLA = 2560
P_GATE = 3072
P_COLS = 7168
MLA_PAD = 512

VMEM_LIMIT = 56 * 1024 * 1024


def _dot(a, b, precision=None):
    return jnp.dot(a, b, preferred_element_type=F32, precision=precision)


def _dot_nt(a, b, precision=None):
    return lax.dot_general(a, b, (((1,), (1,)), ((), ())), preferred_element_type=F32, precision=precision)


def _dot_tn(a, b):
    return lax.dot_general(a, b, (((0,), (0,)), ((), ())), preferred_element_type=F32)


def _mm(a, b):
    return _dot(a.astype(BF16), b.astype(BF16))


def _params(*sem):
    return pltpu.CompilerParams(dimension_semantics=sem, vmem_limit_bytes=VMEM_LIMIT)


def _pick(n, cands):
    for c in cands:
        if n % c == 0:
            return c
    raise ValueError(f"no tile for {n}")


def _head_of_lane(width, group):
    return lax.broadcasted_iota(jnp.int32, (1, width), 1) // group


def _stack_heads(x):
    head = _head_of_lane(MIX_W, HEAD_DIM)
    xf = x.astype(F32)
    return jnp.concatenate([jnp.where(head == h, xf, 0.0).astype(BF16) for h in range(N_HEADS)], axis=0)


def _expand_heads(cols):
    head = _head_of_lane(MIX_W, HEAD_DIM)
    out = cols[:, N_HEADS - 1:N_HEADS]
    for h in range(N_HEADS - 2, -1, -1):
        out = jnp.where(head <= h, cols[:, h:h + 1], out)
    return out


def _head_sum(x, ones_bd):
    hi = x.astype(BF16)
    lo = (x - hi.astype(F32)).astype(BF16)
    return _dot(hi, ones_bd) + _dot(lo, ones_bd)


def _rot(x_bf, cos, sin, perm):
    return x_bf.astype(F32) * cos + _dot(x_bf, perm) * sin


def _norm_modulate(x, g, isc, mc_ref, ml_ref, shift_row, scale_row):
    h = x * lax.rsqrt(jnp.mean(x * x, axis=-1, keepdims=True) + EPS) * g
    shift = jnp.where(isc, mc_ref[0, shift_row:shift_row + 1, :], ml_ref[0, shift_row:shift_row + 1, :])
    scale = jnp.where(isc, mc_ref[0, scale_row:scale_row + 1, :], ml_ref[0, scale_row:scale_row + 1, :])
    return h * (1.0 + scale) + shift


def _adaln_kernel(c_ref, w_ref, b_ref, o_ref):
    c = c_ref[...]
    s = c * jax.nn.sigmoid(c)
    o_ref[0] = _dot(s, w_ref[0], precision=HIGHEST) + b_ref[0]


def _adaln(cond, w_ada, b_ada):
    n_l, d, d6 = w_ada.shape
    r = cond.shape[0]
    tn = 1024
    return pl.pallas_call(
        _adaln_kernel,
        grid=(n_l, d6 // tn),
        in_specs=[pl.BlockSpec((r, d), lambda l, j: (0, 0)),
                  pl.BlockSpec((1, d, tn), lambda l, j: (l, 0, j)),
                  pl.BlockSpec((1, 1, tn), lambda l, j: (l, 0, j))],
        out_specs=pl.BlockSpec((1, r, tn), lambda l, j: (l, 0, j)),
        out_shape=jax.ShapeDtypeStruct((n_l, r, d6), F32),
        compiler_params=_params("arbitrary", "arbitrary"),
        name="adaln",
    )(cond, w_ada, b_ada.reshape(n_l, 1, d6))


def _inproj_kernel(x_ref, ml_ref, mc_ref, g_ref, w_ref, wab_ref, wabc_ref, p_ref, ab_ref, abc_ref, xn_ref,
                   *, tm, rb, n_ctx):
    i = pl.program_id(1)
    j = pl.program_id(2)

    @pl.when(j == 0)
    def _():
        def blk(r, carry):
            r0 = pl.multiple_of(r * rb, rb)
            x = x_ref[0, pl.ds(r0, rb), :]
            rows = lax.broadcasted_iota(jnp.int32, (rb, 1), 0) + (i * tm + r0)
            hn = _norm_modulate(x, g_ref[...], rows < n_ctx, mc_ref, ml_ref, 0, 1)
            xn_ref[pl.ds(r0, rb), :] = hn.astype(BF16)
            return carry

        lax.fori_loop(0, tm // rb, blk, 0)
        ab_ref[0] = _dot_nt(wab_ref[...], xn_ref[...])
        abc_ref[0] = _dot(xn_ref[...], wabc_ref[...])

    p_ref[0] = _dot(xn_ref[...], w_ref[...]).astype(BF16)


def _inproj(xa, mod, g, w, wab, wabc, n_ctx):
    nb, nt, d = xa.shape
    tm = _pick(nt, (1408, 768, 384, 256, 128))
    tn = 1792
    kern = functools.partial(_inproj_kernel, tm=tm, rb=128, n_ctx=n_ctx)
    return pl.pallas_call(
        kern,
        grid=(nb, nt // tm, P_COLS // tn),
        in_specs=[pl.BlockSpec((1, tm, d), lambda b, i, j: (b, i, 0)),
                  pl.BlockSpec((1, 6, d), lambda b, i, j: (b, 0, 0)),
                  pl.BlockSpec((1, 6, d), lambda b, i, j: (nb, 0, 0)),
                  pl.BlockSpec((1, d), lambda b, i, j: (0, 0)),
                  pl.BlockSpec((d, tn), lambda b, i, j: (0, j)),
                  pl.BlockSpec((16, d), lambda b, i, j: (0, 0)),
                  pl.BlockSpec((d, 128), lambda b, i, j: (0, 0))],
        out_specs=[pl.BlockSpec((1, tm, tn), lambda b, i, j: (b, i, j)),
                   pl.BlockSpec((1, 16, tm), lambda b, i, j: (b, 0, i)),
                   pl.BlockSpec((1, tm, 128), lambda b, i, j: (b, i, 0))],
        out_shape=[jax.ShapeDtypeStruct((nb, nt, P_COLS), BF16),
                   jax.ShapeDtypeStruct((nb, 16, nt), F32),
                   jax.ShapeDtypeStruct((nb, nt, 128), F32)],
        scratch_shapes=[pltpu.VMEM((tm, d), BF16)],
        compiler_params=_params("arbitrary", "arbitrary", "arbitrary"),
        name="inproj",
    )(xa, mod, mod, g, w, wab, wabc)


def _bwd_chunk(t, ncc, nc):
    return jnp.where(t < ncc, ncc - 1 - t, nc - 1 - (t - ncc))


def _ret_state_kernel(pf_ref, pb_ref, cf_ref, sf_ref, cb_ref, sb_ref, perm_ref, kd_ref, cd_ref, bd_ref,
                      of_ref, ob_ref, st_f, st_b, *, cb):
    t = pl.program_id(1)

    @pl.when(t == 0)
    def _():
        st_f[...] = jnp.zeros_like(st_f)
        st_b[...] = jnp.zeros_like(st_b)

    def increments(p_ref, c_ref, s_ref, d):
        out = []
        for i in range(cb):
            r = slice(i * CHUNK, (i + 1) * CHUNK)
            kr = _rot(p_ref[0, r, MIX_W:2 * MIX_W], c_ref[r, :], s_ref[r, :], perm_ref[...]) * (HEAD_DIM ** -0.5)
            out.append(bd_ref[...] * _dot_tn((kr * kd_ref[d]).astype(BF16), p_ref[0, r, 2 * MIX_W:3 * MIX_W]))
        return out

    inc_f = increments(pf_ref, cf_ref, sf_ref, 0)
    inc_b = increments(pb_ref, cb_ref, sb_ref, 1)
    s = st_f[...]
    for i in range(cb):
        of_ref[0, i] = s.astype(BF16)
        s = cd_ref[0] * s + inc_f[i]
    st_f[...] = s
    s = st_b[...]
    for i in reversed(range(cb)):
        ob_ref[0, i] = s.astype(BF16)
        s = cd_ref[1] * s + inc_b[i]
    st_b[...] = s


def _gelu_tanh(x):
    return 0.5 * x * (1.0 + jnp.tanh(math.sqrt(2.0 / math.pi) * (x + 0.044715 * (x * x * x))))


def _mix_out_kernel(p_ref, pg_ref, c_ref, s_ref, sf_ref, sb_ref, perm_ref, dm_ref, qd_ref, ones_ref,
                    ng_ref, wg_ref, bg_ref, y_ref, ysg_ref, *, cb):
    chunks = range(cb)
    rows = [slice(i * CHUNK, (i + 1) * CHUNK) for i in chunks]
    perm, dm, ones_bd = perm_ref[...], dm_ref[...], ones_ref[...]
    p = [p_ref[0, r, :] for r in rows]
    cos = [c_ref[r, :] for r in rows]
    sin = [s_ref[r, :] for r in rows]
    qr = [_rot(p[i][:, 0:MIX_W], cos[i], sin[i], perm) for i in chunks]
    kr = [_rot(p[i][:, MIX_W:2 * MIX_W], cos[i], sin[i], perm) * (HEAD_DIM ** -0.5) for i in chunks]
    z = [_gelu_tanh(pg_ref[0, r, :].astype(F32)) for r in rows]
    vg = [x[:, MIX_W:] for x in z]
    mu_g = [jnp.mean(x, axis=-1, keepdims=True) for x in vg]
    vgc = [x - m for x, m in zip(vg, mu_g)]
    var_g = [jnp.mean(x * x, axis=-1, keepdims=True) for x in vgc]
    vn = [x * lax.rsqrt(s + EPS) * ng_ref[...] for x, s in zip(vgc, var_g)]
    sc = [_dot_nt(qr[i].astype(BF16), _stack_heads(kr[i])) * dm for i in chunks]
    mixed = [_dot(wg_ref[...], _stack_heads(x)) for x in vn]
    o = [_dot(sc[i].astype(BF16), _stack_heads(p[i][:, 2 * MIX_W:3 * MIX_W])) for i in chunks]
    qs = [jnp.concatenate([(qr[i] * qd_ref[0]).astype(BF16), (qr[i] * qd_ref[1]).astype(BF16)], axis=1)
          for i in chunks]
    ss = [jnp.concatenate([sf_ref[0, i], sb_ref[0, i]], axis=0) for i in chunks]
    o = [o[i] + _dot(qs[i], ss[i]) for i in chunks]
    for i in chunks:
        ysg_ref[0, rows[i], :] = (z[i][:, :MIX_W] * (mixed[i] + bg_ref[...])).astype(BF16)
    mu = [_head_sum(x, ones_bd) * (1.0 / HEAD_DIM) for x in o]
    oc = [x - m for x, m in zip(o, mu)]
    var = [_head_sum(x * x, ones_bd) * (1.0 / HEAD_DIM) for x in oc]
    for i in chunks:
        g = p[i][:, 3 * MIX_W:4 * MIX_W].astype(F32)
        y_ref[0, rows[i], :] = (oc[i] * lax.rsqrt(var[i] + EPS) * (g * jax.nn.sigmoid(g))).astype(BF16)


def _retention_and_sgate(p, cos, sin, perm, tabs, sg_ng, sg_w, sg_bias, ncc):
    nb, nt, _ = p.shape
    nc = nt // CHUNK
    kd, cd, qd, dm, bd, ones_bd = tabs
    cb = 2
    assert nc % cb == 0 and ncc % cb == 0
    nblk, ncb = nc // cb, ncc // cb
    fwd = lambda b, t: (b, t, 0)
    bwd = lambda b, t: (b, _bwd_chunk(t, ncb, nblk), 0)
    tab_f = lambda b, t: (t, 0)
    tab_b = lambda b, t: (_bwd_chunk(t, ncb, nblk), 0)
    c2 = lambda b, t: (0, 0)
    c3 = lambda b, t: (0, 0, 0)
    st_shape = jax.ShapeDtypeStruct((nb, nc, MIX_W, MIX_W), BF16)
    st_f, st_b = pl.pallas_call(
        functools.partial(_ret_state_kernel, cb=cb),
        grid=(nb, nblk),
        in_specs=[pl.BlockSpec((1, cb * CHUNK, RET_COLS), fwd),
                  pl.BlockSpec((1, cb * CHUNK, RET_COLS), bwd),
                  pl.BlockSpec((cb * CHUNK, MIX_W), tab_f), pl.BlockSpec((cb * CHUNK, MIX_W), tab_f),
                  pl.BlockSpec((cb * CHUNK, MIX_W), tab_b), pl.BlockSpec((cb * CHUNK, MIX_W), tab_b),
                  pl.BlockSpec((MIX_W, MIX_W), c2),
                  pl.BlockSpec((2, CHUNK, MIX_W), c3),
                  pl.BlockSpec((2, 1, MIX_W), c3),
                  pl.BlockSpec((MIX_W, MIX_W), c2)],
        out_specs=[pl.BlockSpec((1, cb, MIX_W, MIX_W), lambda b, t: (b, t, 0, 0)),
                   pl.BlockSpec((1, cb, MIX_W, MIX_W), lambda b, t: (b, _bwd_chunk(t, ncb, nblk), 0, 0))],
        out_shape=[st_shape, st_shape],
        scratch_shapes=[pltpu.VMEM((MIX_W, MIX_W), F32), pltpu.VMEM((MIX_W, MIX_W), F32)],
        compiler_params=_params("arbitrary", "arbitrary"),
        name="ret_state",
    )(p, p, cos, sin, cos, sin, perm, kd, cd, bd)
    blk = lambda b, t: (b, t, 0)
    y_shape = jax.ShapeDtypeStruct((nb, nt, MIX_W), BF16)
    return pl.pallas_call(
        functools.partial(_mix_out_kernel, cb=cb),
        grid=(nb, nc // cb),
        in_specs=[pl.BlockSpec((1, cb * CHUNK, RET_COLS), blk),
                  pl.BlockSpec((1, cb * CHUNK, SG_COLS), lambda b, t: (b, t, P_SG // SG_COLS)),
                  pl.BlockSpec((cb * CHUNK, MIX_W), tab_f), pl.BlockSpec((cb * CHUNK, MIX_W), tab_f),
                  pl.BlockSpec((1, cb, MIX_W, MIX_W), lambda b, t: (b, t, 0, 0)),
                  pl.BlockSpec((1, cb, MIX_W, MIX_W), lambda b, t: (b, t, 0, 0)),
                  pl.BlockSpec((MIX_W, MIX_W), c2),
                  pl.BlockSpec((CHUNK, N_HEADS * CHUNK), c2),
                  pl.BlockSpec((2, CHUNK, MIX_W), c3),
                  pl.BlockSpec((MIX_W, MIX_W), c2),
                  pl.BlockSpec((1, MIX_W), c2),
                  pl.BlockSpec((CHUNK, N_HEADS * CHUNK), c2),
                  pl.BlockSpec((CHUNK, MIX_W), c2)],
        out_specs=[pl.BlockSpec((1, cb * CHUNK, MIX_W), blk), pl.BlockSpec((1, cb * CHUNK, MIX_W), blk)],
        out_shape=[y_shape, y_shape],
        compiler_params=_params("arbitrary", "arbitrary"),
        name="mix_out",
    )(p, p, cos, sin, st_f, st_b, perm, dm, qd, ones_bd, sg_ng, sg_w, sg_bias)


def _softplus(a):
    return jnp.maximum(a, 0.0) + jnp.log1p(jnp.exp(-jnp.abs(a)))


def _dn_prep_kernel(pc_ref, pp_ref, pn_ref, ab_ref, abc_ref, cw_ref, na_ref, dtb_ref, nar_ref, dtbr_ref, ones_ref,
                    qkv_ref, gb_ref, gbc_ref, xe_ref, *, ncc, nc):
    t = pl.program_id(1)
    w3 = 3 * MIX_W
    prev_ok = jnp.where((t != 0) & (t != ncc), 1.0, 0.0)
    next_ok = jnp.where((t != ncc - 1) & (t != nc - 1), 1.0, 0.0)
    tail = pp_ref[0, CHUNK - 16:CHUNK, 0:w3].astype(F32)
    head = pn_ref[0, 0:16, 0:w3].astype(F32)
    xe_ref[0:8, :] = tail[8:16, :] * prev_ok
    xe_ref[8:8 + CHUNK, :] = pc_ref[0, :, 0:w3].astype(F32)
    xe_ref[8 + CHUNK:16 + CHUNK, :] = head[0:8, :] * next_ok
    pad = CONV_W // 2
    y = xe_ref[8 - pad:8 - pad + CHUNK, :] * cw_ref[0:1, :]
    for i in range(1, CONV_W):
        y = y + xe_ref[8 - pad + i:8 - pad + i + CHUNK, :] * cw_ref[i:i + 1, :]
    y = y * jax.nn.sigmoid(y)
    q = y[:, 0:MIX_W]
    k = y[:, MIX_W:2 * MIX_W]
    v = y[:, 2 * MIX_W:w3]
    ones_bd = ones_ref[...]
    qn = q * lax.rsqrt(_head_sum(q * q, ones_bd) + EPS) * (HEAD_DIM ** -0.5)
    kn = k * lax.rsqrt(_head_sum(k * k, ones_bd) + EPS)
    qkv_ref[0, :, 0:MIX_W] = qn.astype(BF16)
    qkv_ref[0, :, MIX_W:2 * MIX_W] = kn.astype(BF16)
    qkv_ref[0, :, 2 * MIX_W:w3] = v.astype(BF16)
    ab = ab_ref[0]
    gb_ref[0, 0:8, :] = na_ref[...] * _softplus(ab[0:8, :] + dtb_ref[...])
    gb_ref[0, 8:16, :] = jax.nn.sigmoid(ab[8:16, :])
    abc = abc_ref[0]
    lane = lax.broadcasted_iota(jnp.int32, (1, 128), 1)
    g_c = nar_ref[...] * _softplus(abc + dtbr_ref[...])
    gbc_ref[0] = jnp.where(lane < 8, g_c, jnp.where(lane < 16, jax.nn.sigmoid(abc), 0.0))


def _split3(x):
    hi = x.astype(BF16)
    r = x - hi.astype(F32)
    mid = r.astype(BF16)
    lo = (r - mid.astype(F32)).astype(BF16)
    return hi, mid, lo


def _tri_inverse(mats, ii, jj):
    eye = jnp.where(ii == jj, 1.0, 0.0)
    nd = [jnp.where((ii // 16) == (jj // 16), n, 0.0) for n in mats]
    p1 = [_mm(x, x) for x in nd]
    m = [eye - x for x in nd]
    p2 = [_mm(x, x) for x in p1]
    m = [x + _mm(x, y) for x, y in zip(m, p1)]
    p3 = [_mm(x, x) for x in p2]
    m = [x + _mm(x, y) for x, y in zip(m, p2)]
    m = [x + _mm(x, y) for x, y in zip(m, p3)]
    for lvl in (16, 32, 64):
        off_mask = ((ii // (2 * lvl)) == (jj // (2 * lvl))) & ((ii // lvl) != (jj // lvl))
        t = [_mm(jnp.where(off_mask, n, 0.0), x) for n, x in zip(mats, m)]
        m = [x - _mm(x, y) for x, y in zip(m, t)]
    return m


def _dn_pre(qkv, g, gbc, d, lower):
    c = CHUNK
    qn = qkv[:, 0:MIX_W]
    kn = qkv[:, MIX_W:2 * MIX_W]
    v = qkv[:, 2 * MIX_W:3 * MIX_W]
    ii = lax.broadcasted_iota(jnp.int32, (c, c), 0)
    jj = lax.broadcasted_iota(jnp.int32, (c, c), 1)
    incl = (ii >= jj) if lower else (ii <= jj)
    tri = jnp.where(incl, 1.0, 0.0).astype(BF16)
    g_row = sum(_dot_nt(part, tri) for part in _split3(g))[N_HEADS * d:N_HEADS * (d + 1), :]
    cum = sum(_dot(tri, part) for part in _split3(gbc))
    g_col = cum[:, N_HEADS * d:N_HEADS * (d + 1)]
    b_col = gbc[:, 2 * N_HEADS + N_HEADS * d:2 * N_HEADS + N_HEADS * (d + 1)]
    g_cols4 = jnp.concatenate([jnp.broadcast_to(g_col[:, h:h + 1], (c, c)) for h in range(N_HEADS)], axis=1)
    b_cols4 = jnp.concatenate([jnp.broadcast_to(b_col[:, h:h + 1], (c, c)) for h in range(N_HEADS)], axis=1)
    g_rows4 = jnp.concatenate([g_row[h:h + 1, :] for h in range(N_HEADS)], axis=1)
    incl4 = jnp.concatenate([incl] * N_HEADS, axis=1)
    diag4 = jnp.concatenate([ii == jj] * N_HEADS, axis=1)
    decay = jnp.where(incl4, jnp.exp(jnp.where(incl4, g_cols4 - g_rows4, 0.0)), 0.0)
    kstack = _stack_heads(kn)
    kk = _dot_nt(kn, kstack)
    qk = _dot_nt(qn, kstack)
    n_mat = jnp.where(diag4, 0.0, decay * kk * b_cols4)
    attn = (decay * qk).astype(BF16)
    g256 = _expand_heads(g_col)
    eg256 = jnp.exp(g256)
    b256 = _expand_heads(b_col)
    vb = v.astype(F32) * b256
    kbg = kn.astype(F32) * b256 * eg256
    rhs = jnp.concatenate([_stack_heads(vb), _stack_heads(kbg)], axis=1)
    g_last = g256[c - 1:c, :] if lower else g256[0:1, :]
    kdec = (kn.astype(F32) * jnp.exp(g_last - g256)).astype(BF16)
    n_heads = [n_mat[:, h * c:(h + 1) * c] for h in range(N_HEADS)]
    return n_heads, dict(qn=qn, attn=attn, rhs=rhs, eg=eg256, kdec=kdec, sdec=jnp.exp(g_last))


def _dn_post(z, st, bd):
    uw = _dot(z["ainv"], z["rhs"])
    u = uw[:, 0:MIX_W]
    wk = uw[:, MIX_W:2 * MIX_W]
    s_prev = st[...]
    s_bf = s_prev.astype(BF16)
    w = u - _dot(wk.astype(BF16), s_bf)
    o = z["eg"] * _dot(z["qn"], s_bf) + _dot(z["attn"], _stack_heads(w))
    st[...] = z["sdec"] * s_prev + bd * _dot_tn(z["kdec"], w.astype(BF16))
    return o


def _dn_scan_kernel(qf_ref, qb_ref, gf_ref, gb_ref, gcf_ref, gcb_ref, bd_ref, of_ref, ob_ref, st_f, st_b):
    t = pl.program_id(1)

    @pl.when(t == 0)
    def _():
        st_f[...] = jnp.zeros_like(st_f)
        st_b[...] = jnp.zeros_like(st_b)

    bd = bd_ref[...]
    nf, pre_f = _dn_pre(qf_ref[0], gf_ref[0], gcf_ref[0], 0, True)
    nb_, pre_b = _dn_pre(qb_ref[0], gb_ref[0], gcb_ref[0], 1, False)
    ii = lax.broadcasted_iota(jnp.int32, (CHUNK, CHUNK), 0)
    jj = lax.broadcasted_iota(jnp.int32, (CHUNK, CHUNK), 1)
    inv = _tri_inverse(nf + nb_, ii, jj)
    for d, pre in enumerate((pre_f, pre_b)):
        pre["ainv"] = jnp.concatenate(inv[N_HEADS * d:N_HEADS * (d + 1)], axis=1).astype(BF16)
    of_ref[0] = _dn_post(pre_f, st_f, bd)
    ob_ref[0] = _dn_post(pre_b, st_b, bd)


def _deltanet(p, ab_t, ab_c, conv_w, neg_a, dtb, bd, ones_bd, ncc):
    nb, nt, _ = p.shape
    nc = nt // CHUNK
    w3 = 3 * MIX_W
    c2 = lambda b, t: (0, 0)
    dn_blk = P_DN // RET_COLS
    pad_lanes = lambda col: jnp.concatenate([col.reshape(1, -1), jnp.zeros((1, 128 - col.size), F32)], axis=1)
    qkv, gbeta, gbeta_c = pl.pallas_call(
        functools.partial(_dn_prep_kernel, ncc=ncc, nc=nc),
        grid=(nb, nc),
        in_specs=[pl.BlockSpec((1, CHUNK, 4 * MIX_W), lambda b, t: (b, t, dn_blk)),
                  pl.BlockSpec((1, CHUNK, 4 * MIX_W), lambda b, t: (b, jnp.maximum(t - 1, 0), dn_blk)),
                  pl.BlockSpec((1, CHUNK, 4 * MIX_W), lambda b, t: (b, jnp.minimum(t + 1, nc - 1), dn_blk)),
                  pl.BlockSpec((1, 16, CHUNK), lambda b, t: (b, 0, t)),
                  pl.BlockSpec((1, CHUNK, 128), lambda b, t: (b, t, 0)),
                  pl.BlockSpec((8, w3), c2),
                  pl.BlockSpec((8, 1), c2),
                  pl.BlockSpec((8, 1), c2),
                  pl.BlockSpec((1, 128), c2),
                  pl.BlockSpec((1, 128), c2),
                  pl.BlockSpec((MIX_W, MIX_W), c2)],
        out_specs=[pl.BlockSpec((1, CHUNK, w3), lambda b, t: (b, t, 0)),
                   pl.BlockSpec((1, 16, CHUNK), lambda b, t: (b, 0, t)),
                   pl.BlockSpec((1, CHUNK, 128), lambda b, t: (b, t, 0))],
        out_shape=[jax.ShapeDtypeStruct((nb, nt, w3), BF16),
                   jax.ShapeDtypeStruct((nb, 16, nt), F32),
                   jax.ShapeDtypeStruct((nb, nt, 128), F32)],
        scratch_shapes=[pltpu.VMEM((CHUNK + 16, w3), F32)],
        compiler_params=_params("arbitrary", "arbitrary"),
        name="dn_prep",
    )(p, p, p, ab_t, ab_c, conv_w, neg_a, dtb, pad_lanes(neg_a), pad_lanes(dtb), ones_bd)
    cur_b = lambda t: _bwd_chunk(t, ncc, nc)
    o_shape = jax.ShapeDtypeStruct((nb, nt, MIX_W), F32)
    return pl.pallas_call(
        _dn_scan_kernel,
        grid=(nb, nc),
        in_specs=[pl.BlockSpec((1, CHUNK, w3), lambda b, t: (b, t, 0)),
                  pl.BlockSpec((1, CHUNK, w3), lambda b, t: (b, cur_b(t), 0)),
                  pl.BlockSpec((1, 16, CHUNK), lambda b, t: (b, 0, t)),
                  pl.BlockSpec((1, 16, CHUNK), lambda b, t: (b, 0, cur_b(t))),
                  pl.BlockSpec((1, CHUNK, 128), lambda b, t: (b, t, 0)),
                  pl.BlockSpec((1, CHUNK, 128), lambda b, t: (b, cur_b(t), 0)),
                  pl.BlockSpec((MIX_W, MIX_W), c2)],
        out_specs=[pl.BlockSpec((1, CHUNK, MIX_W), lambda b, t: (b, t, 0)),
                   pl.BlockSpec((1, CHUNK, MIX_W), lambda b, t: (b, cur_b(t), 0))],
        out_shape=[o_shape, o_shape],
        scratch_shapes=[pltpu.VMEM((MIX_W, MIX_W), F32), pltpu.VMEM((MIX_W, MIX_W), F32)],
        compiler_params=_params("arbitrary", "arbitrary"),
        name="dn_scan",
    )(qkv, qkv, gbeta, gbeta, gbeta_c, gbeta_c, bd)


QK_W = 256


VT_ROWS = 144


def _mla_prep_kernel(p_ref, c_ref, s_ref, perm_ref, qg_ref, kg_ref, wqn_ref, wqr_ref, wa_ref, selq_ref, selc_ref,
                     selr_ref, selv_ref, one_ref, qt_ref, kv_ref, vt_ref, *, scale):
    p = p_ref[0]
    cos, sin, perm = c_ref[...], s_ref[...], perm_ref[...]
    cq = p[:, 0:Q_LORA].astype(F32)
    cqn = (cq * lax.rsqrt(jnp.mean(cq * cq, axis=-1, keepdims=True) + EPS) * qg_ref[...]).astype(BF16)
    q_nope = _dot(cqn, wqn_ref[...]).astype(BF16)
    q_rope = _dot(cqn, wqr_ref[...]).astype(BF16)
    q_rot = (_rot(q_rope, cos, sin, perm) * scale).astype(BF16)
    q_nope_s = (q_nope.astype(F32) * scale).astype(BF16)
    for h in range(N_HEADS):
        qt_ref[0, h] = (_dot_nt(wa_ref[h], q_nope_s) + _dot_nt(selq_ref[h], q_rot)).astype(BF16)
    ckv = p[:, Q_LORA:Q_LORA + KV_LORA].astype(F32)
    ckvn = (ckv * lax.rsqrt(jnp.mean(ckv * ckv, axis=-1, keepdims=True) + EPS) * kg_ref[...]).astype(BF16)
    kr = p[:, Q_LORA + KV_LORA:MLA_PAD]
    kr_rot = _rot(kr, cos, sin, perm).astype(BF16)
    kv_ref[0] = (_dot(ckvn, selc_ref[...]) + _dot(kr_rot, selr_ref[...])).astype(BF16)
    vt_ref[0] = (_dot_nt(selv_ref[...], ckvn) + one_ref[...]).astype(BF16)


def _mla_attn_kernel(qt_ref, kv_ref, vt_ref, wuv_ref, y_ref, m_ref, acc_ref, s_ref, *, tq, tk, n_ctx, nt):
    i = pl.program_id(1)
    heads = range(N_HEADS)
    m_ref[...] = jnp.full_like(m_ref, -jnp.inf)
    acc_ref[...] = jnp.zeros_like(acc_ref)

    def scores(j0, size, slot):
        k = kv_ref[0, pl.ds(j0, size), :]
        for h in heads:
            s_ref[slot, h, 0:size, :] = _dot(k, qt_ref[0, h])

    def softmax_pv(j0, size, slot):
        vt = vt_ref[0, :, pl.ds(j0, size)]
        s = [s_ref[slot, h, 0:size, :] for h in heads]
        m_old = [m_ref[h] for h in heads]
        m_new = [jnp.maximum(m_old[h], jnp.max(s[h], axis=0, keepdims=True)) for h in heads]
        pr = [jnp.exp2(s[h] - m_new[h]).astype(BF16) for h in heads]
        pv = [_dot(vt, pr[h]) for h in heads]
        for h in heads:
            acc_ref[h] = jnp.exp2(m_old[h] - m_new[h]) * acc_ref[h] + pv[h]
            m_ref[h] = m_new[h]

    scores(0, n_ctx, 0)
    is_latent = (i + 1) * tq > n_ctx

    @pl.when(jnp.logical_not(is_latent))
    def _():
        softmax_pv(0, n_ctx, 0)

    @pl.when(is_latent)
    def _():
        n_tiles = (nt - n_ctx) // tk
        last = n_ctx + (n_tiles - 1) * tk
        scores(n_ctx, tk, 1)
        softmax_pv(0, n_ctx, 0)

        def body(jj, carry):
            t0 = pl.multiple_of(n_ctx + 2 * jj * tk, 256)
            t1 = pl.multiple_of(jnp.minimum(t0 + tk, last), 256)
            t2 = pl.multiple_of(jnp.minimum(t0 + 2 * tk, last), 256)
            scores(t1, tk, 0)
            softmax_pv(t0, tk, 1)
            scores(t2, tk, 1)
            softmax_pv(t1, tk, 0)
            return carry

        lax.fori_loop(0, n_tiles // 2, body, 0)
        if n_tiles % 2:
            softmax_pv(last, tk, 1)

    y = None
    for h in range(N_HEADS):
        acc = acc_ref[h]
        o = (acc[0:KV_LORA, :] / acc[KV_LORA:KV_LORA + 1, :]).astype(BF16)
        term = _dot_tn(o, wuv_ref[h])
        y = term if y is None else y + term
    y_ref[0] = y.astype(BF16)


def _mla(p, cos, sin, perm, qg, kg, wqn, wqr, wa, selq, selc, selr, selv, one_col, wuv, n_ctx):
    nb, nt, _ = p.shape
    tm = _pick(nt, (768, 384, 256, 128))
    scale = (NOPE_DIM + ROPE_DIM) ** -0.5 * math.log2(math.e)
    c2 = lambda b, i: (0, 0)
    c3 = lambda b, i: (0, 0, 0)
    qt, kv, vt = pl.pallas_call(
        functools.partial(_mla_prep_kernel, scale=scale),
        grid=(nb, nt // tm),
        in_specs=[pl.BlockSpec((1, tm, MLA_PAD), lambda b, i: (b, i, P_MLA // MLA_PAD)),
                  pl.BlockSpec((tm, 128), lambda b, i: (i, 0)),
                  pl.BlockSpec((tm, 128), lambda b, i: (i, 0)),
                  pl.BlockSpec((128, 128), c2),
                  pl.BlockSpec((1, Q_LORA), c2),
                  pl.BlockSpec((1, KV_LORA), c2),
                  pl.BlockSpec((Q_LORA, N_HEADS * NOPE_DIM), c2),
                  pl.BlockSpec((Q_LORA, N_HEADS * ROPE_DIM), c2),
                  pl.BlockSpec((N_HEADS, QK_W, N_HEADS * NOPE_DIM), c3),
                  pl.BlockSpec((N_HEADS, QK_W, N_HEADS * ROPE_DIM), c3),
                  pl.BlockSpec((KV_LORA, QK_W), c2),
                  pl.BlockSpec((128, QK_W), c2),
                  pl.BlockSpec((VT_ROWS, KV_LORA), c2),
                  pl.BlockSpec((VT_ROWS, 1), c2)],
        out_specs=[pl.BlockSpec((1, N_HEADS, QK_W, tm), lambda b, i: (b, 0, 0, i)),
                   pl.BlockSpec((1, tm, QK_W), lambda b, i: (b, i, 0)),
                   pl.BlockSpec((1, VT_ROWS, tm), lambda b, i: (b, 0, i))],
        out_shape=[jax.ShapeDtypeStruct((nb, N_HEADS, QK_W, nt), BF16),
                   jax.ShapeDtypeStruct((nb, nt, QK_W), BF16),
                   jax.ShapeDtypeStruct((nb, VT_ROWS, nt), BF16)],
        compiler_params=_params("arbitrary", "arbitrary"),
        name="mla_prep",
    )(p, cos, sin, perm, qg, kg, wqn, wqr, wa, selq, selc, selr, selv, one_col)
    tq = 256
    tk = _pick(nt - n_ctx, (512, 256))
    return pl.pallas_call(
        functools.partial(_mla_attn_kernel, tq=tq, tk=tk, n_ctx=n_ctx, nt=nt),
        grid=(nb, nt // tq),
        in_specs=[pl.BlockSpec((1, N_HEADS, QK_W, tq), lambda b, i: (b, 0, 0, i)),
                  pl.BlockSpec((1, nt, QK_W), lambda b, i: (b, 0, 0)),
                  pl.BlockSpec((1, VT_ROWS, nt), lambda b, i: (b, 0, 0)),
                  pl.BlockSpec((N_HEADS, KV_LORA, MIX_W), c3)],
        out_specs=pl.BlockSpec((1, tq, MIX_W), lambda b, i: (b, i, 0)),
        out_shape=jax.ShapeDtypeStruct((nb, nt, MIX_W), BF16),
        scratch_shapes=[pltpu.VMEM((N_HEADS, 1, tq), F32), pltpu.VMEM((N_HEADS, VT_ROWS, tq), F32),
                        pltpu.VMEM((2, N_HEADS, max(tk, n_ctx), tq), F32)],
        compiler_params=_params("arbitrary", "arbitrary"),
        name="mla_attn",
    )(qt, kv, vt, wuv)


def _merge_kernel(x_ref, yr_ref, ys_ref, of_ref, ob_ref, ym_ref, z_ref, g0_ref, g1_ref, g2_ref, g3_ref,
                  wb_ref, wo_ref, ng_ref, gp_ref, ml_ref, mc_ref, ones_ref, o_ref, *, tm, n_ctx):
    i = pl.program_id(1)
    od = of_ref[0] + ob_ref[0]
    ms = _head_sum(od * od, ones_ref[...]) * (1.0 / HEAD_DIM)
    z = z_ref[0].astype(F32)
    ydn = (od * lax.rsqrt(ms + EPS) * ng_ref[...]) * (z * jax.nn.sigmoid(z))
    ys = (yr_ref[0], ys_ref[0], ydn.astype(BF16), ym_ref[0])
    gates = (g0_ref, g1_ref, g2_ref, g3_ref)
    acc = None
    for b in range(N_BRANCH):
        term = jax.nn.sigmoid(gates[b][0].astype(F32)) * _dot(ys[b], wb_ref[b])
        acc = term if acc is None else acc + term
    y = _dot(acc.astype(BF16), wo_ref[...])
    r = y * lax.rsqrt(jnp.mean(y * y, axis=-1, keepdims=True) + EPS) * gp_ref[...]
    rows = lax.broadcasted_iota(jnp.int32, (tm, 1), 0) + i * tm
    gate = jnp.where(rows < n_ctx, mc_ref[0, 2:3, :], ml_ref[0, 2:3, :])
    o_ref[0] = x_ref[0] + gate * r


def _merge(xa, y_ret, y_sg, o_f, o_b, y_mla, p, wb, wo, ng, gp, mod, ones_bd, n_ctx):
    nb, nt, d = xa.shape
    tm = _pick(nt, (768, 384, 256, 128))
    row = lambda b, i: (b, i, 0)
    c2 = lambda b, i: (0, 0)
    y_spec = pl.BlockSpec((1, tm, MIX_W), row)
    gate_specs = [pl.BlockSpec((1, tm, d), functools.partial(lambda b, i, k: (b, i, k), k=P_GATE // d + k))
                  for k in range(N_BRANCH)]
    return pl.pallas_call(
        functools.partial(_merge_kernel, tm=tm, n_ctx=n_ctx),
        grid=(nb, nt // tm),
        in_specs=[pl.BlockSpec((1, tm, d), row), y_spec, y_spec, y_spec, y_spec, y_spec,
                  pl.BlockSpec((1, tm, MIX_W), lambda b, i: (b, i, (P_DN + 3 * MIX_W) // MIX_W)),
                  *gate_specs,
                  pl.BlockSpec((N_BRANCH, MIX_W, d), lambda b, i: (0, 0, 0)),
                  pl.BlockSpec((d, d), c2),
                  pl.BlockSpec((1, MIX_W), c2),
                  pl.BlockSpec((1, d), c2),
                  pl.BlockSpec((1, 6, d), lambda b, i: (b, 0, 0)),
                  pl.BlockSpec((1, 6, d), lambda b, i: (nb, 0, 0)),
                  pl.BlockSpec((MIX_W, MIX_W), c2)],
        out_specs=pl.BlockSpec((1, tm, d), row),
        out_shape=jax.ShapeDtypeStruct((nb, nt, d), F32),
        compiler_params=_params("arbitrary", "arbitrary"),
        name="merge",
    )(xa, y_ret, y_sg, o_f, o_b, y_mla, p, p, p, p, p, wb, wo, ng, gp, mod, mod, ones_bd)


def _route(sel, aff):
    rows = [sel[e:e + 1, :] for e in range(N_EXPERTS)]
    pairs = [(a, b) for a in range(EXPERTS_PER_GROUP) for b in range(a + 1, EXPERTS_PER_GROUP)]
    grp_score, grp_pair = [], []
    for g in range(N_GROUPS):
        base = g * EXPERTS_PER_GROUP
        best = rows[base + pairs[0][0]] + rows[base + pairs[0][1]]
        best_p = jnp.zeros_like(best, dtype=jnp.int32)
        for pi in range(1, len(pairs)):
            s = rows[base + pairs[pi][0]] + rows[base + pairs[pi][1]]
            take = s > best
            best = jnp.where(take, s, best)
            best_p = jnp.where(take, pi, best_p)
        grp_score.append(best)
        grp_pair.append(best_p)
    top = grp_score[0]
    top_g = jnp.zeros_like(grp_pair[0])
    top_p = grp_pair[0]
    for g in range(1, N_GROUPS):
        take = grp_score[g] > top
        top = jnp.where(take, grp_score[g], top)
        top_g = jnp.where(take, g, top_g)
        top_p = jnp.where(take, grp_pair[g], top_p)
    picked = []
    for e in range(N_EXPERTS):
        g, k = divmod(e, EXPERTS_PER_GROUP)
        in_pair = None
        for pi, (a, b) in enumerate(pairs):
            if k in (a, b):
                hit = top_p == pi
                in_pair = hit if in_pair is None else (in_pair | hit)
        picked.append(jnp.where((top_g == g) & in_pair, aff[e:e + 1, :], 0.0))
    denom = picked[0]
    for e in range(1, N_EXPERTS):
        denom = denom + picked[e]
    return [pk / denom for pk in picked]


def _swiglu_pair(hn, w13, w2, scale):
    r = _dot(hn, w13)
    half = r.shape[1] // 2
    a = r[:, 0:half]
    h = (a * jax.nn.sigmoid(a)) * r[:, half:]
    if scale is not None:
        h = h * scale
    return _dot(h.astype(BF16), w2)


def _moe_kernel(x_ref, ml_ref, mc_ref, g2_ref, gp_ref, rw_ref, rb_ref, ws13_ref, ws2_ref, w13_ref, w2_ref, o_ref,
                hn_ref, comb_t_ref, comb_ref, acc_ref, *, tm, rb, n_ctx):
    i = pl.program_id(1)
    e = pl.program_id(2)

    @pl.when(e == 0)
    def _():
        def blk(r, carry):
            r0 = pl.multiple_of(r * rb, rb)
            x = x_ref[0, pl.ds(r0, rb), :]
            rows = lax.broadcasted_iota(jnp.int32, (rb, 1), 0) + (i * tm + r0)
            hn = _norm_modulate(x, g2_ref[...], rows < n_ctx, mc_ref, ml_ref, 3, 4)
            hn_ref[pl.ds(r0, rb), :] = hn.astype(BF16)
            logit = _dot_nt(rw_ref[...], hn.astype(BF16))
            aff = jax.nn.sigmoid(logit)
            comb = _route(aff + rb_ref[...], aff)
            for k in range(N_EXPERTS):
                comb_t_ref[k:k + 1, pl.ds(r0, rb)] = comb[k]
            return carry

        comb_t_ref[...] = jnp.zeros_like(comb_t_ref)
        lax.fori_loop(0, tm // rb, blk, 0)
        comb_ref[...] = comb_t_ref[...].T
        acc_ref[...] = _swiglu_pair(hn_ref[...], ws13_ref[...], ws2_ref[...], None)

    @pl.when(e > 0)
    def _():
        lane = lax.broadcasted_iota(jnp.int32, (1, 128), 1)
        comb = comb_ref[...]
        first = 2 * (e - 1)
        c_a = jnp.sum(jnp.where(lane == first, comb, 0.0), axis=-1, keepdims=True)
        c_b = jnp.sum(jnp.where(lane == first + 1, comb, 0.0), axis=-1, keepdims=True)
        lane2 = lax.broadcasted_iota(jnp.int32, (1, 2 * D_EXPERT), 1)
        scale = jnp.where(lane2 < D_EXPERT, c_a, c_b)
        acc_ref[...] += _swiglu_pair(hn_ref[...], w13_ref[0], w2_ref[0], scale)

    @pl.when(e == pl.num_programs(2) - 1)
    def _():
        y = acc_ref[...]
        r = y * lax.rsqrt(jnp.mean(y * y, axis=-1, keepdims=True) + EPS) * gp_ref[...]
        rows = lax.broadcasted_iota(jnp.int32, (tm, 1), 0) + i * tm
        gate = jnp.where(rows < n_ctx, mc_ref[0, 5:6, :], ml_ref[0, 5:6, :])
        o_ref[0] = x_ref[0] + gate * r


def _pair_experts(w1, w3, w2):
    ne, d, f = w1.shape
    pair = lambda w: w.reshape(ne // 2, 2, d, f).transpose(0, 2, 1, 3).reshape(ne // 2, d, 2 * f)
    w13 = jnp.concatenate([pair(w1), pair(w3)], axis=2)
    return w13.astype(BF16), w2.reshape(ne // 2, 2 * f, d).astype(BF16)


def _moe(xa, mod, g2, gp, rw_t, rbias, ws13, ws2, w13, w2, n_ctx):
    nb, nt, d = xa.shape
    tm = _pick(nt, (768, 384, 256, 128))
    n_pairs = w13.shape[0]
    row = lambda b, i, e: (b, i, 0)
    c2 = lambda b, i, e: (0, 0)
    pair_blk = lambda b, i, e: (jnp.maximum(e - 1, 0), 0, 0)
    return pl.pallas_call(
        functools.partial(_moe_kernel, tm=tm, rb=128, n_ctx=n_ctx),
        grid=(nb, nt // tm, n_pairs + 1),
        in_specs=[pl.BlockSpec((1, tm, d), row),
                  pl.BlockSpec((1, 6, d), lambda b, i, e: (b, 0, 0)),
                  pl.BlockSpec((1, 6, d), lambda b, i, e: (nb, 0, 0)),
                  pl.BlockSpec((1, d), c2),
                  pl.BlockSpec((1, d), c2),
                  pl.BlockSpec((N_EXPERTS, d), c2),
                  pl.BlockSpec((N_EXPERTS, 1), c2),
                  pl.BlockSpec((d, 2 * D_EXPERT), c2),
                  pl.BlockSpec((D_EXPERT, d), c2),
                  pl.BlockSpec((1, d, 4 * D_EXPERT), pair_blk),
                  pl.BlockSpec((1, 2 * D_EXPERT, d), pair_blk)],
        out_specs=pl.BlockSpec((1, tm, d), row),
        out_shape=jax.ShapeDtypeStruct((nb, nt, d), F32),
        scratch_shapes=[pltpu.VMEM((tm, d), BF16), pltpu.VMEM((128, tm), F32), pltpu.VMEM((tm, 128), F32),
                        pltpu.VMEM((tm, d), F32)],
        compiler_params=_params("arbitrary", "arbitrary", "arbitrary"),
        name="moe",
    )(xa, mod, mod, g2, gp, rw_t, rbias, ws13, ws2, w13, w2)


def _swap_perm(width, group):
    j = jnp.arange(width)
    src = jnp.where((j % group) < group // 2, j + group // 2, j - group // 2)
    return (jnp.arange(width)[:, None] == src[None, :]).astype(BF16)


def _rope_tables(n_lat, n_ctx):
    def angles(pos, dim):
        half = dim // 2
        inv = ROPE_BASE ** (-jnp.arange(half, dtype=F32) / half)
        return pos.astype(F32)[:, None] * inv[None, :]

    def tables(cos_parts, sin_parts, reps):
        cos = jnp.tile(jnp.concatenate(cos_parts, axis=-1), (1, reps))
        sin = jnp.tile(jnp.concatenate(sin_parts, axis=-1), (1, reps))
        w = cos.shape[1]
        return (jnp.concatenate([jnp.ones((n_ctx, w), F32), cos], axis=0),
                jnp.concatenate([jnp.zeros((n_ctx, w), F32), sin], axis=0))

    rows = n_lat // GRID_W
    ang_t = angles(jnp.arange(n_lat), HEAD_DIM)
    ang_r = angles(jnp.repeat(jnp.arange(rows), GRID_W), ROPE_DIM // 2)
    ang_c = angles(jnp.tile(jnp.arange(GRID_W), rows), ROPE_DIM // 2)
    ct, st = jnp.cos(ang_t), jnp.sin(ang_t)
    ret = tables([ct, ct], [-st, st], N_HEADS)
    cr, sr, cc, sc = jnp.cos(ang_r), jnp.sin(ang_r), jnp.cos(ang_c), jnp.sin(ang_c)
    mla = tables([cr, cr, cc, cc], [-sr, sr, -sc, sc], N_HEADS)
    return ret, mla


def _ret_tables(logit):
    log_g = jax.nn.log_sigmoid(logit.astype(F32))
    lane_lg = jnp.repeat(log_g, HEAD_DIM, axis=1)
    idx = jnp.arange(CHUNK, dtype=F32)[:, None]
    kd = jnp.stack([jnp.exp(lane_lg[0][None, :] * (CHUNK - 1 - idx)), jnp.exp(lane_lg[1][None, :] * idx)])
    qd = jnp.stack([jnp.exp(lane_lg[0][None, :] * (idx + 1)), jnp.exp(lane_lg[1][None, :] * (CHUNK - idx))])
    cd = jnp.exp(lane_lg * CHUNK)[:, None, :]
    diff = idx - idx.T
    blocks = []
    for h in range(N_HEADS):
        f = jnp.exp(log_g[0, h] * jnp.where(diff >= 0, diff, 0.0))
        b = jnp.exp(log_g[1, h] * jnp.where(diff < 0, -diff, 0.0))
        blocks.append(jnp.where(diff >= 0, f, b))
    dm = jnp.concatenate(blocks, axis=1)
    return kd, cd, qd, dm


def _pack_w_in(w_in):
    d = w_in.shape[0]
    mla = jnp.concatenate([w_in[:, OFF_MLA:OFF_MLA + MLA_COLS], jnp.zeros((d, MLA_PAD - MLA_COLS), w_in.dtype)], 1)
    w = jnp.concatenate([w_in[:, OFF_RET:OFF_RET + RET_COLS], w_in[:, OFF_DN:OFF_DN + 4 * MIX_W],
                         w_in[:, OFF_SG:OFF_SG + SG_COLS], mla, w_in[:, OFF_GATE:OFF_GATE + GATE_COLS]], axis=1)
    wab = w_in[:, OFF_DN + 4 * MIX_W:OFF_DN + DN_COLS]
    wabc = jnp.concatenate([wab, jnp.zeros((d, 128 - 4 * N_HEADS), w_in.dtype)], axis=1)
    return w.astype(BF16), wab.T.astype(BF16), wabc.astype(BF16)


def _mla_weights(w_uq, w_ukv):
    dq = NOPE_DIM + ROPE_DIM
    dkv = NOPE_DIM + V_DIM
    wq = w_uq.reshape(Q_LORA, N_HEADS, dq)
    wqn = wq[:, :, :NOPE_DIM].reshape(Q_LORA, N_HEADS * NOPE_DIM)
    wqr = wq[:, :, NOPE_DIM:].reshape(Q_LORA, N_HEADS * ROPE_DIM)
    wkv = w_ukv.reshape(KV_LORA, N_HEADS, dkv)
    wa = jnp.zeros((N_HEADS, QK_W, N_HEADS * NOPE_DIM), F32)
    selq = jnp.zeros((N_HEADS, QK_W, N_HEADS * ROPE_DIM), F32)
    wuv = jnp.zeros((N_HEADS, KV_LORA, MIX_W), F32)
    for h in range(N_HEADS):
        wa = wa.at[h, 0:KV_LORA, h * NOPE_DIM:(h + 1) * NOPE_DIM].set(wkv[:, h, :NOPE_DIM])
        selq = selq.at[h, KV_LORA:KV_LORA + ROPE_DIM, h * ROPE_DIM:(h + 1) * ROPE_DIM].set(jnp.eye(ROPE_DIM))
        wuv = wuv.at[h, :, h * V_DIM:(h + 1) * V_DIM].set(wkv[:, h, NOPE_DIM:])
    selc = jnp.zeros((KV_LORA, QK_W), F32).at[:, 0:KV_LORA].set(jnp.eye(KV_LORA))
    selr = jnp.zeros((128, QK_W), F32).at[0:ROPE_DIM, KV_LORA:KV_LORA + ROPE_DIM].set(jnp.eye(ROPE_DIM))
    selv = jnp.zeros((VT_ROWS, KV_LORA), F32).at[0:KV_LORA, :].set(jnp.eye(KV_LORA))
    one_col = jnp.zeros((VT_ROWS, 1), F32).at[KV_LORA, 0].set(1.0)
    return tuple(a.astype(BF16) for a in (wqn, wqr, wa, selq, selc, selr, selv)) + (one_col, wuv.astype(BF16))


def kernel(x, c, ctx, c_ctx, w_ada, b_ada, g_pre1, g_post1, g_pre2, g_post2, w_in, ret_decay_logit, sg_norm_g, sg_w, sg_b, dn_conv_w, dn_A_log, dn_dt_bias, dn_norm_g, mla_q_norm_g, mla_kv_norm_g, mla_w_uq, mla_w_ukv, w_branch, w_out, router_w, router_bias, moe_w1, moe_w3, moe_w2, shared_w1, shared_w3, shared_w2):
    nb, n_lat, d = x.shape
    n_ctx = ctx.shape[1]
    depth = w_in.shape[0]
    assert d == D_MODEL and n_lat % GRID_W == 0 and n_lat % CHUNK == 0 and n_ctx % 256 == 0
    ncc = n_ctx // CHUNK

    n_cond = -(-(nb + 1) // 8) * 8
    cond = jnp.concatenate([c, c_ctx[None], jnp.zeros((n_cond - nb - 1, d), F32)], axis=0)
    mod_all = _adaln(cond, w_ada, b_ada).reshape(depth, n_cond, 6, d)

    (ret_cos, ret_sin), (mla_cos, mla_sin) = _rope_tables(n_lat, n_ctx)
    perm_ret = _swap_perm(MIX_W, HEAD_DIM)
    perm_mla = _swap_perm(N_HEADS * ROPE_DIM, ROPE_DIM // 2)
    lane_head = jnp.arange(MIX_W) // HEAD_DIM
    bd = (lane_head[:, None] == lane_head[None, :]).astype(F32)
    ones_bd = bd.astype(BF16)
    rw_t = router_w.T.astype(BF16)
    rbias = router_bias.astype(F32)[:, None]

    xa = jnp.concatenate([ctx, x], axis=1)
    for l in range(depth):
        mod = mod_all[l]
        w_l, wab_l, wabc_l = _pack_w_in(w_in[l])
        p, ab_t, ab_c = _inproj(xa, mod, g_pre1[l][None], w_l, wab_l, wabc_l, n_ctx)

        kd, cd, qd, dm = _ret_tables(ret_decay_logit[l])
        wcat = jnp.concatenate([sg_w[l, h] for h in range(N_HEADS)], axis=1).astype(BF16)
        sg_bias = jnp.repeat(sg_b[l].T, HEAD_DIM, axis=1)
        y_ret, y_sg = _retention_and_sgate(p, ret_cos, ret_sin, perm_ret, (kd, cd, qd, dm, bd, ones_bd),
                                           sg_norm_g[l][None], wcat, sg_bias, ncc)

        neg_a = (-jnp.exp(dn_A_log[l].astype(F32))).reshape(2 * N_HEADS, 1)
        dtb = dn_dt_bias[l].astype(F32).reshape(2 * N_HEADS, 1)
        conv_w = jnp.concatenate([dn_conv_w[l], jnp.zeros((8 - CONV_W, 3 * MIX_W), F32)], axis=0)
        o_f, o_b = _deltanet(p, ab_t, ab_c, conv_w, neg_a, dtb, bd, ones_bd, ncc)

        y_mla = _mla(p, mla_cos, mla_sin, perm_mla, mla_q_norm_g[l][None], mla_kv_norm_g[l][None],
                     *_mla_weights(mla_w_uq[l], mla_w_ukv[l]), n_ctx)

        xa = _merge(xa, y_ret, y_sg, o_f, o_b, y_mla, p, w_branch[l].astype(BF16), w_out[l].astype(BF16),
                    jnp.tile(dn_norm_g[l], N_HEADS)[None], g_post1[l][None], mod, ones_bd, n_ctx)

        ws13 = jnp.concatenate([shared_w1[l], shared_w3[l]], axis=1).astype(BF16)
        w13, w2 = _pair_experts(moe_w1[l], moe_w3[l], moe_w2[l])
        xa = _moe(xa, mod, g_pre2[l][None], g_post2[l][None], rw_t, rbias, ws13, shared_w2[l].astype(BF16),
                  w13, w2, n_ctx)
    return xa[:, n_ctx:]
```

```python
import functools
import math

import jax
import jax.numpy as jnp
import numpy as np
from jax import lax
from jax.experimental import pallas as pl
from jax.experimental.pallas import tpu as pltpu

F32 = jnp.float32
BF16 = jnp.bfloat16
HIGHEST = lax.Precision.HIGHEST

D_MODEL = 1024
GRID_W = 64
N_HEADS = 4
HEAD_DIM = 64
MIX_W = N_HEADS * HEAD_DIM
CHUNK = 128
ROPE_BASE = 10000.0
EPS = 1e-6
RET_DECAY_EXP0 = 5.0
CONV_W = 5
Q_LORA = 256
KV_LORA = 128
NOPE_DIM = 64
ROPE_DIM = 32
V_DIM = 64
N_EXPERTS = 16
N_GROUPS = 4
EXPERTS_PER_GROUP = N_EXPERTS // N_GROUPS
D_EXPERT = 256
N_BRANCH = 4

RET_COLS = 4 * MIX_W
SG_COLS = 2 * MIX_W
DN_COLS = 4 * MIX_W + 4 * N_HEADS
MLA_COLS = Q_LORA + KV_LORA + ROPE_DIM
GATE_COLS = N_BRANCH * D_MODEL
OFF_RET = 0
OFF_SG = OFF_RET + RET_COLS
OFF_DN = OFF_SG + SG_COLS
OFF_MLA = OFF_DN + DN_COLS
OFF_GATE = OFF_MLA + MLA_COLS

P_RET = 0
P_DN = 1024
P_SG = 2048
P_MLA = 2560
P_GATE = 3072
P_COLS = 7168
MLA_PAD = 512

VMEM_LIMIT = 56 * 1024 * 1024


def _dot(a, b, precision=None):
    return jnp.dot(a, b, preferred_element_type=F32, precision=precision)


def _dot_nt(a, b, precision=None):
    return lax.dot_general(a, b, (((1,), (1,)), ((), ())), preferred_element_type=F32, precision=precision)


def _dot_tn(a, b):
    return lax.dot_general(a, b, (((0,), (0,)), ((), ())), preferred_element_type=F32)


def _mm(a, b):
    return _dot(a.astype(BF16), b.astype(BF16))


def _params(*sem):
    return pltpu.CompilerParams(dimension_semantics=sem, vmem_limit_bytes=VMEM_LIMIT)


def _pick(n, cands):
    for c in cands:
        if n % c == 0:
            return c
    raise ValueError(f"no tile for {n}")


def _head_of_lane(width, group):
    return lax.broadcasted_iota(jnp.int32, (1, width), 1) // group


def _stack_heads(x):
    head = _head_of_lane(MIX_W, HEAD_DIM)
    xf = x.astype(F32)
    return jnp.concatenate([jnp.where(head == h, xf, 0.0).astype(BF16) for h in range(N_HEADS)], axis=0)


def _expand_heads(cols):
    head = _head_of_lane(MIX_W, HEAD_DIM)
    out = cols[:, N_HEADS - 1:N_HEADS]
    for h in range(N_HEADS - 2, -1, -1):
        out = jnp.where(head <= h, cols[:, h:h + 1], out)
    return out


def _head_sum(x, ones_bd):
    hi = x.astype(BF16)
    lo = (x - hi.astype(F32)).astype(BF16)
    return _dot(hi, ones_bd) + _dot(lo, ones_bd)


def _rot(x_bf, cos, sin, perm):
    return x_bf.astype(F32) * cos + _dot(x_bf, perm) * sin


def _norm_modulate(x, g, isc, mc_ref, ml_ref, shift_row, scale_row):
    h = x * lax.rsqrt(jnp.mean(x * x, axis=-1, keepdims=True) + EPS) * g
    shift = jnp.where(isc, mc_ref[0, shift_row:shift_row + 1, :], ml_ref[0, shift_row:shift_row + 1, :])
    scale = jnp.where(isc, mc_ref[0, scale_row:scale_row + 1, :], ml_ref[0, scale_row:scale_row + 1, :])
    return h * (1.0 + scale) + shift


def _adaln_kernel(c_ref, w_ref, b_ref, o_ref):
    c = c_ref[...]
    s = c * jax.nn.sigmoid(c)
    o_ref[0] = _dot(s, w_ref[0], precision=HIGHEST) + b_ref[0]


def _adaln(cond, w_ada, b_ada):
    n_l, d, d6 = w_ada.shape
    r = cond.shape[0]
    tn = 1024
    return pl.pallas_call(
        _adaln_kernel,
        grid=(n_l, d6 // tn),
        in_specs=[pl.BlockSpec((r, d), lambda l, j: (0, 0)),
                  pl.BlockSpec((1, d, tn), lambda l, j: (l, 0, j)),
                  pl.BlockSpec((1, 1, tn), lambda l, j: (l, 0, j))],
        out_specs=pl.BlockSpec((1, r, tn), lambda l, j: (l, 0, j)),
        out_shape=jax.ShapeDtypeStruct((n_l, r, d6), F32),
        compiler_params=_params("arbitrary", "arbitrary"),
        name="adaln",
    )(cond, w_ada, b_ada.reshape(n_l, 1, d6))


def _inproj_kernel(x_ref, ml_ref, mc_ref, g_ref, w_ref, wab_ref, wabc_ref, p_ref, ab_ref, abc_ref, xn_ref,
                   *, tm, rb, n_ctx):
    i = pl.program_id(1)
    j = pl.program_id(2)

    @pl.when(j == 0)
    def _():
        def blk(r, carry):
            r0 = pl.multiple_of(r * rb, rb)
            x = x_ref[0, pl.ds(r0, rb), :]
            rows = lax.broadcasted_iota(jnp.int32, (rb, 1), 0) + (i * tm + r0)
            hn = _norm_modulate(x, g_ref[...], rows < n_ctx, mc_ref, ml_ref, 0, 1)
            xn_ref[pl.ds(r0, rb), :] = hn.astype(BF16)
            return carry

        lax.fori_loop(0, tm // rb, blk, 0)
        ab_ref[0] = _dot_nt(wab_ref[...], xn_ref[...])
        abc_ref[0] = _dot(xn_ref[...], wabc_ref[...])

    p_ref[0] = _dot(xn_ref[...], w_ref[...]).astype(BF16)


def _inproj(xa, mod, g, w, wab, wabc, n_ctx):
    nb, nt, d = xa.shape
    tm = _pick(nt, (1408, 768, 384, 256, 128))
    tn = 1792
    kern = functools.partial(_inproj_kernel, tm=tm, rb=128, n_ctx=n_ctx)
    return pl.pallas_call(
        kern,
        grid=(nb, nt // tm, P_COLS // tn),
        in_specs=[pl.BlockSpec((1, tm, d), lambda b, i, j: (b, i, 0)),
                  pl.BlockSpec((1, 6, d), lambda b, i, j: (b, 0, 0)),
                  pl.BlockSpec((1, 6, d), lambda b, i, j: (nb, 0, 0)),
                  pl.BlockSpec((1, d), lambda b, i, j: (0, 0)),
                  pl.BlockSpec((d, tn), lambda b, i, j: (0, j)),
                  pl.BlockSpec((16, d), lambda b, i, j: (0, 0)),
                  pl.BlockSpec((d, 128), lambda b, i, j: (0, 0))],
        out_specs=[pl.BlockSpec((1, tm, tn), lambda b, i, j: (b, i, j)),
                   pl.BlockSpec((1, 16, tm), lambda b, i, j: (b, 0, i)),
                   pl.BlockSpec((1, tm, 128), lambda b, i, j: (b, i, 0))],
        out_shape=[jax.ShapeDtypeStruct((nb, nt, P_COLS), BF16),
                   jax.ShapeDtypeStruct((nb, 16, nt), F32),
                   jax.ShapeDtypeStruct((nb, nt, 128), F32)],
        scratch_shapes=[pltpu.VMEM((tm, d), BF16)],
        compiler_params=_params("arbitrary", "arbitrary", "arbitrary"),
        name="inproj",
    )(xa, mod, mod, g, w, wab, wabc)


def _bwd_chunk(t, ncc, nc):
    return jnp.where(t < ncc, ncc - 1 - t, nc - 1 - (t - ncc))


def _ret_state_kernel(pf_ref, pb_ref, cf_ref, sf_ref, cb_ref, sb_ref, perm_ref, kd_ref, cd_ref, bd_ref,
                      of_ref, ob_ref, st_f, st_b, *, cb):
    t = pl.program_id(1)

    @pl.when(t == 0)
    def _():
        st_f[...] = jnp.zeros_like(st_f)
        st_b[...] = jnp.zeros_like(st_b)

    def increments(p_ref, c_ref, s_ref, d):
        out = []
        for i in range(cb):
            r = slice(i * CHUNK, (i + 1) * CHUNK)
            kr = _rot(p_ref[0, r, MIX_W:2 * MIX_W], c_ref[r, :], s_ref[r, :], perm_ref[...]) * (HEAD_DIM ** -0.5)
            out.append(bd_ref[...] * _dot_tn((kr * kd_ref[d]).astype(BF16), p_ref[0, r, 2 * MIX_W:3 * MIX_W]))
        return out

    inc_f = increments(pf_ref, cf_ref, sf_ref, 0)
    inc_b = increments(pb_ref, cb_ref, sb_ref, 1)
    s = st_f[...]
    for i in range(cb):
        of_ref[0, i] = s.astype(BF16)
        s = cd_ref[0] * s + inc_f[i]
    st_f[...] = s
    s = st_b[...]
    for i in reversed(range(cb)):
        ob_ref[0, i] = s.astype(BF16)
        s = cd_ref[1] * s + inc_b[i]
    st_b[...] = s


def _gelu_tanh(x):
    return 0.5 * x * (1.0 + jnp.tanh(math.sqrt(2.0 / math.pi) * (x + 0.044715 * (x * x * x))))


def _mix_out_kernel(p_ref, pg_ref, c_ref, s_ref, sf_ref, sb_ref, perm_ref, dm_ref, qd_ref, ones_ref,
                    ng_ref, wg_ref, bg_ref, y_ref, ysg_ref, *, cb):
    chunks = range(cb)
    rows = [slice(i * CHUNK, (i + 1) * CHUNK) for i in chunks]
    perm, dm, ones_bd = perm_ref[...], dm_ref[...], ones_ref[...]
    p = [p_ref[0, r, :] for r in rows]
    cos = [c_ref[r, :] for r in rows]
    sin = [s_ref[r, :] for r in rows]
    qr = [_rot(p[i][:, 0:MIX_W], cos[i], sin[i], perm) for i in chunks]
    kr = [_rot(p[i][:, MIX_W:2 * MIX_W], cos[i], sin[i], perm) * (HEAD_DIM ** -0.5) for i in chunks]
    z = [_gelu_tanh(pg_ref[0, r, :].astype(F32)) for r in rows]
    vg = [x[:, MIX_W:] for x in z]
    mu_g = [jnp.mean(x, axis=-1, keepdims=True) for x in vg]
    vgc = [x - m for x, m in zip(vg, mu_g)]
    var_g = [jnp.mean(x * x, axis=-1, keepdims=True) for x in vgc]
    vn = [x * lax.rsqrt(s + EPS) * ng_ref[...] for x, s in zip(vgc, var_g)]
    sc = [_dot_nt(qr[i].astype(BF16), _stack_heads(kr[i])) * dm for i in chunks]
    mixed = [_dot(wg_ref[...], _stack_heads(x)) for x in vn]
    o = [_dot(sc[i].astype(BF16), _stack_heads(p[i][:, 2 * MIX_W:3 * MIX_W])) for i in chunks]
    qs = [jnp.concatenate([(qr[i] * qd_ref[0]).astype(BF16), (qr[i] * qd_ref[1]).astype(BF16)], axis=1)
          for i in chunks]
    ss = [jnp.concatenate([sf_ref[0, i], sb_ref[0, i]], axis=0) for i in chunks]
    o = [o[i] + _dot(qs[i], ss[i]) for i in chunks]
    for i in chunks:
        ysg_ref[0, rows[i], :] = (z[i][:, :MIX_W] * (mixed[i] + bg_ref[...])).astype(BF16)
    mu = [_head_sum(x, ones_bd) * (1.0 / HEAD_DIM) for x in o]
    oc = [x - m for x, m in zip(o, mu)]
    var = [_head_sum(x * x, ones_bd) * (1.0 / HEAD_DIM) for x in oc]
    for i in chunks:
        g = p[i][:, 3 * MIX_W:4 * MIX_W].astype(F32)
        y_ref[0, rows[i], :] = (oc[i] * lax.rsqrt(var[i] + EPS) * (g * jax.nn.sigmoid(g))).astype(BF16)


def _retention_and_sgate(p, cos, sin, perm, tabs, sg_ng, sg_w, sg_bias, ncc):
    nb, nt, _ = p.shape
    nc = nt // CHUNK
    kd, cd, qd, dm, bd, ones_bd = tabs
    cb = 2
    assert nc % cb == 0 and ncc % cb == 0
    nblk, ncb = nc // cb, ncc // cb
    fwd = lambda b, t: (b, t, 0)
    bwd = lambda b, t: (b, _bwd_chunk(t, ncb, nblk), 0)
    tab_f = lambda b, t: (t, 0)
    tab_b = lambda b, t: (_bwd_chunk(t, ncb, nblk), 0)
    c2 = lambda b, t: (0, 0)
    c3 = lambda b, t: (0, 0, 0)
    st_shape = jax.ShapeDtypeStruct((nb, nc, MIX_W, MIX_W), BF16)
    st_f, st_b = pl.pallas_call(
        functools.partial(_ret_state_kernel, cb=cb),
        grid=(nb, nblk),
        in_specs=[pl.BlockSpec((1, cb * CHUNK, RET_COLS), fwd),
                  pl.BlockSpec((1, cb * CHUNK, RET_COLS), bwd),
                  pl.BlockSpec((cb * CHUNK, MIX_W), tab_f), pl.BlockSpec((cb * CHUNK, MIX_W), tab_f),
                  pl.BlockSpec((cb * CHUNK, MIX_W), tab_b), pl.BlockSpec((cb * CHUNK, MIX_W), tab_b),
                  pl.BlockSpec((MIX_W, MIX_W), c2),
                  pl.BlockSpec((2, CHUNK, MIX_W), c3),
                  pl.BlockSpec((2, 1, MIX_W), c3),
                  pl.BlockSpec((MIX_W, MIX_W), c2)],
        out_specs=[pl.BlockSpec((1, cb, MIX_W, MIX_W), lambda b, t: (b, t, 0, 0)),
                   pl.BlockSpec((1, cb, MIX_W, MIX_W), lambda b, t: (b, _bwd_chunk(t, ncb, nblk), 0, 0))],
        out_shape=[st_shape, st_shape],
        scratch_shapes=[pltpu.VMEM((MIX_W, MIX_W), F32), pltpu.VMEM((MIX_W, MIX_W), F32)],
        compiler_params=_params("arbitrary", "arbitrary"),
        name="ret_state",
    )(p, p, cos, sin, cos, sin, perm, kd, cd, bd)
    blk = lambda b, t: (b, t, 0)
    y_shape = jax.ShapeDtypeStruct((nb, nt, MIX_W), BF16)
    return pl.pallas_call(
        functools.partial(_mix_out_kernel, cb=cb),
        grid=(nb, nc // cb),
        in_specs=[pl.BlockSpec((1, cb * CHUNK, RET_COLS), blk),
                  pl.BlockSpec((1, cb * CHUNK, SG_COLS), lambda b, t: (b, t, P_SG // SG_COLS)),
                  pl.BlockSpec((cb * CHUNK, MIX_W), tab_f), pl.BlockSpec((cb * CHUNK, MIX_W), tab_f),
                  pl.BlockSpec((1, cb, MIX_W, MIX_W), lambda b, t: (b, t, 0, 0)),
                  pl.BlockSpec((1, cb, MIX_W, MIX_W), lambda b, t: (b, t, 0, 0)),
                  pl.BlockSpec((MIX_W, MIX_W), c2),
                  pl.BlockSpec((CHUNK, N_HEADS * CHUNK), c2),
                  pl.BlockSpec((2, CHUNK, MIX_W), c3),
                  pl.BlockSpec((MIX_W, MIX_W), c2),
                  pl.BlockSpec((1, MIX_W), c2),
                  pl.BlockSpec((CHUNK, N_HEADS * CHUNK), c2),
                  pl.BlockSpec((CHUNK, MIX_W), c2)],
        out_specs=[pl.BlockSpec((1, cb * CHUNK, MIX_W), blk), pl.BlockSpec((1, cb * CHUNK, MIX_W), blk)],
        out_shape=[y_shape, y_shape],
        compiler_params=_params("arbitrary", "arbitrary"),
        name="mix_out",
    )(p, p, cos, sin, st_f, st_b, perm, dm, qd, ones_bd, sg_ng, sg_w, sg_bias)


def _softplus(a):
    return jnp.maximum(a, 0.0) + jnp.log1p(jnp.exp(-jnp.abs(a)))


def _dn_prep_kernel(pc_ref, pp_ref, pn_ref, ab_ref, abc_ref, cw_ref, na_ref, dtb_ref, nar_ref, dtbr_ref, ones_ref,
                    qkv_ref, gb_ref, gbc_ref, xe_ref, *, ncc, nc):
    t = pl.program_id(1)
    w3 = 3 * MIX_W
    prev_ok = jnp.where((t != 0) & (t != ncc), 1.0, 0.0)
    next_ok = jnp.where((t != ncc - 1) & (t != nc - 1), 1.0, 0.0)
    tail = pp_ref[0, CHUNK - 16:CHUNK, 0:w3].astype(F32)
    head = pn_ref[0, 0:16, 0:w3].astype(F32)
    xe_ref[0:8, :] = tail[8:16, :] * prev_ok
    xe_ref[8:8 + CHUNK, :] = pc_ref[0, :, 0:w3].astype(F32)
    xe_ref[8 + CHUNK:16 + CHUNK, :] = head[0:8, :] * next_ok
    pad = CONV_W // 2
    y = xe_ref[8 - pad:8 - pad + CHUNK, :] * cw_ref[0:1, :]
    for i in range(1, CONV_W):
        y = y + xe_ref[8 - pad + i:8 - pad + i + CHUNK, :] * cw_ref[i:i + 1, :]
    y = y * jax.nn.sigmoid(y)
    q = y[:, 0:MIX_W]
    k = y[:, MIX_W:2 * MIX_W]
    v = y[:, 2 * MIX_W:w3]
    ones_bd = ones_ref[...]
    qn = q * lax.rsqrt(_head_sum(q * q, ones_bd) + EPS) * (HEAD_DIM ** -0.5)
    kn = k * lax.rsqrt(_head_sum(k * k, ones_bd) + EPS)
    qkv_ref[0, :, 0:MIX_W] = qn.astype(BF16)
    qkv_ref[0, :, MIX_W:2 * MIX_W] = kn.astype(BF16)
    qkv_ref[0, :, 2 * MIX_W:w3] = v.astype(BF16)
    ab = ab_ref[0]
    gb_ref[0, 0:8, :] = na_ref[...] * _softplus(ab[0:8, :] + dtb_ref[...])
    gb_ref[0, 8:16, :] = jax.nn.sigmoid(ab[8:16, :])
    abc = abc_ref[0]
    lane = lax.broadcasted_iota(jnp.int32, (1, 128), 1)
    g_c = nar_ref[...] * _softplus(abc + dtbr_ref[...])
    gbc_ref[0] = jnp.where(lane < 8, g_c, jnp.where(lane < 16, jax.nn.sigmoid(abc), 0.0))


def _split3(x):
    hi = x.astype(BF16)
    r = x - hi.astype(F32)
    mid = r.astype(BF16)
    lo = (r - mid.astype(F32)).astype(BF16)
    return hi, mid, lo


def _tri_inverse(mats, ii, jj):
    eye = jnp.where(ii == jj, 1.0, 0.0)
    nd = [jnp.where((ii // 16) == (jj // 16), n, 0.0) for n in mats]
    p1 = [_mm(x, x) for x in nd]
    m = [eye - x for x in nd]
    p2 = [_mm(x, x) for x in p1]
    m = [x + _mm(x, y) for x, y in zip(m, p1)]
    p3 = [_mm(x, x) for x in p2]
    m = [x + _mm(x, y) for x, y in zip(m, p2)]
    m = [x + _mm(x, y) for x, y in zip(m, p3)]
    for lvl in (16, 32, 64):
        off_mask = ((ii // (2 * lvl)) == (jj // (2 * lvl))) & ((ii // lvl) != (jj // lvl))
        t = [_mm(jnp.where(off_mask, n, 0.0), x) for n, x in zip(mats, m)]
        m = [x - _mm(x, y) for x, y in zip(m, t)]
    return m


def _dn_pre(qkv, g, gbc, d, lower):
    c = CHUNK
    qn = qkv[:, 0:MIX_W]
    kn = qkv[:, MIX_W:2 * MIX_W]
    v = qkv[:, 2 * MIX_W:3 * MIX_W]
    ii = lax.broadcasted_iota(jnp.int32, (c, c), 0)
    jj = lax.broadcasted_iota(jnp.int32, (c, c), 1)
    incl = (ii >= jj) if lower else (ii <= jj)
    tri = jnp.where(incl, 1.0, 0.0).astype(BF16)
    g_row = sum(_dot_nt(part, tri) for part in _split3(g))[N_HEADS * d:N_HEADS * (d + 1), :]
    cum = sum(_dot(tri, part) for part in _split3(gbc))
    g_col = cum[:, N_HEADS * d:N_HEADS * (d + 1)]
    b_col = gbc[:, 2 * N_HEADS + N_HEADS * d:2 * N_HEADS + N_HEADS * (d + 1)]
    g_cols4 = jnp.concatenate([jnp.broadcast_to(g_col[:, h:h + 1], (c, c)) for h in range(N_HEADS)], axis=1)
    b_cols4 = jnp.concatenate([jnp.broadcast_to(b_col[:, h:h + 1], (c, c)) for h in range(N_HEADS)], axis=1)
    g_rows4 = jnp.concatenate([g_row[h:h + 1, :] for h in range(N_HEADS)], axis=1)
    incl4 = jnp.concatenate([incl] * N_HEADS, axis=1)
    diag4 = jnp.concatenate([ii == jj] * N_HEADS, axis=1)
    decay = jnp.where(incl4, jnp.exp(jnp.where(incl4, g_cols4 - g_rows4, 0.0)), 0.0)
    kstack = _stack_heads(kn)
    kk = _dot_nt(kn, kstack)
    qk = _dot_nt(qn, kstack)
    n_mat = jnp.where(diag4, 0.0, decay * kk * b_cols4)
    attn = (decay * qk).astype(BF16)
    g256 = _expand_heads(g_col)
    eg256 = jnp.exp(g256)
    b256 = _expand_heads(b_col)
    vb = v.astype(F32) * b256
    kbg = kn.astype(F32) * b256 * eg256
    rhs = jnp.concatenate([_stack_heads(vb), _stack_heads(kbg)], axis=1)
    g_last = g256[c - 1:c, :] if lower else g256[0:1, :]
    kdec = (kn.astype(F32) * jnp.exp(g_last - g256)).astype(BF16)
    n_heads = [n_mat[:, h * c:(h + 1) * c] for h in range(N_HEADS)]
    return n_heads, dict(qn=qn, attn=attn, rhs=rhs, eg=eg256, kdec=kdec, sdec=jnp.exp(g_last))


def _dn_post(z, s_prev, bd):
    s_bf = s_prev.astype(BF16)
    w = z["u"] - _dot(z["wk"], s_bf)
    o = z["eg"] * _dot(z["qn"], s_bf) + _dot(z["attn"], _stack_heads(w))
    s_next = z["sdec"] * s_prev + bd * _dot_tn(z["kdec"], w.astype(BF16))
    return o, s_next


def _dn_scan_kernel(qf_ref, qb_ref, gf_ref, gb_ref, gcf_ref, gcb_ref, bd_ref, of_ref, ob_ref, st_f, st_b, *, cb):
    t = pl.program_id(1)

    @pl.when(t == 0)
    def _():
        st_f[...] = jnp.zeros_like(st_f)
        st_b[...] = jnp.zeros_like(st_b)

    bd = bd_ref[...]
    rows = [slice(i * CHUNK, (i + 1) * CHUNK) for i in range(cb)]
    mats, pres = [], []
    for d, (q_ref, g_ref, gc_ref) in enumerate(((qf_ref, gf_ref, gcf_ref), (qb_ref, gb_ref, gcb_ref))):
        for r in rows:
            n_heads, pre = _dn_pre(q_ref[0, r, :], g_ref[0, :, r], gc_ref[0, r, :], d, d == 0)
            mats += n_heads
            pres.append(pre)
    ii = lax.broadcasted_iota(jnp.int32, (CHUNK, CHUNK), 0)
    jj = lax.broadcasted_iota(jnp.int32, (CHUNK, CHUNK), 1)
    inv = _tri_inverse(mats, ii, jj)
    for n, pre in enumerate(pres):
        a_inv = jnp.concatenate(inv[N_HEADS * n:N_HEADS * (n + 1)], axis=1).astype(BF16)
        uw = _dot(a_inv, pre["rhs"])
        pre["u"] = uw[:, 0:MIX_W]
        pre["wk"] = uw[:, MIX_W:2 * MIX_W].astype(BF16)
    s_f, s_b = st_f[...], st_b[...]
    for k in range(cb):
        o, s_f = _dn_post(pres[k], s_f, bd)
        of_ref[0, rows[k], :] = o
        o, s_b = _dn_post(pres[cb + cb - 1 - k], s_b, bd)
        ob_ref[0, rows[cb - 1 - k], :] = o
    st_f[...] = s_f
    st_b[...] = s_b


def _deltanet(p, ab_t, ab_c, conv_w, neg_a, dtb, bd, ones_bd, ncc):
    nb, nt, _ = p.shape
    nc = nt // CHUNK
    w3 = 3 * MIX_W
    c2 = lambda b, t: (0, 0)
    dn_blk = P_DN // RET_COLS
    pad_lanes = lambda col: jnp.concatenate([col.reshape(1, -1), jnp.zeros((1, 128 - col.size), F32)], axis=1)
    qkv, gbeta, gbeta_c = pl.pallas_call(
        functools.partial(_dn_prep_kernel, ncc=ncc, nc=nc),
        grid=(nb, nc),
        in_specs=[pl.BlockSpec((1, CHUNK, 4 * MIX_W), lambda b, t: (b, t, dn_blk)),
                  pl.BlockSpec((1, CHUNK, 4 * MIX_W), lambda b, t: (b, jnp.maximum(t - 1, 0), dn_blk)),
                  pl.BlockSpec((1, CHUNK, 4 * MIX_W), lambda b, t: (b, jnp.minimum(t + 1, nc - 1), dn_blk)),
                  pl.BlockSpec((1, 16, CHUNK), lambda b, t: (b, 0, t)),
                  pl.BlockSpec((1, CHUNK, 128), lambda b, t: (b, t, 0)),
                  pl.BlockSpec((8, w3), c2),
                  pl.BlockSpec((8, 1), c2),
                  pl.BlockSpec((8, 1), c2),
                  pl.BlockSpec((1, 128), c2),
                  pl.BlockSpec((1, 128), c2),
                  pl.BlockSpec((MIX_W, MIX_W), c2)],
        out_specs=[pl.BlockSpec((1, CHUNK, w3), lambda b, t: (b, t, 0)),
                   pl.BlockSpec((1, 16, CHUNK), lambda b, t: (b, 0, t)),
                   pl.BlockSpec((1, CHUNK, 128), lambda b, t: (b, t, 0))],
        out_shape=[jax.ShapeDtypeStruct((nb, nt, w3), BF16),
                   jax.ShapeDtypeStruct((nb, 16, nt), F32),
                   jax.ShapeDtypeStruct((nb, nt, 128), F32)],
        scratch_shapes=[pltpu.VMEM((CHUNK + 16, w3), F32)],
        compiler_params=_params("arbitrary", "arbitrary"),
        name="dn_prep",
    )(p, p, p, ab_t, ab_c, conv_w, neg_a, dtb, pad_lanes(neg_a), pad_lanes(dtb), ones_bd)
    cb = 2
    assert nc % cb == 0 and ncc % cb == 0
    rows = cb * CHUNK
    cur_b = lambda t: _bwd_chunk(t, ncc // cb, nc // cb)
    o_shape = jax.ShapeDtypeStruct((nb, nt, MIX_W), F32)
    return pl.pallas_call(
        functools.partial(_dn_scan_kernel, cb=cb),
        grid=(nb, nc // cb),
        in_specs=[pl.BlockSpec((1, rows, w3), lambda b, t: (b, t, 0)),
                  pl.BlockSpec((1, rows, w3), lambda b, t: (b, cur_b(t), 0)),
                  pl.BlockSpec((1, 16, rows), lambda b, t: (b, 0, t)),
                  pl.BlockSpec((1, 16, rows), lambda b, t: (b, 0, cur_b(t))),
                  pl.BlockSpec((1, rows, 128), lambda b, t: (b, t, 0)),
                  pl.BlockSpec((1, rows, 128), lambda b, t: (b, cur_b(t), 0)),
                  pl.BlockSpec((MIX_W, MIX_W), c2)],
        out_specs=[pl.BlockSpec((1, rows, MIX_W), lambda b, t: (b, t, 0)),
                   pl.BlockSpec((1, rows, MIX_W), lambda b, t: (b, cur_b(t), 0))],
        out_shape=[o_shape, o_shape],
        scratch_shapes=[pltpu.VMEM((MIX_W, MIX_W), F32), pltpu.VMEM((MIX_W, MIX_W), F32)],
        compiler_params=_params("arbitrary", "arbitrary"),
        name="dn_scan",
    )(qkv, qkv, gbeta, gbeta, gbeta_c, gbeta_c, bd)


QK_W = 256


VT_ROWS = 144


def _mla_prep_kernel(p_ref, c_ref, s_ref, perm_ref, qg_ref, kg_ref, wqn_ref, wqr_ref, wa_ref, selq_ref, selc_ref,
                     selr_ref, selv_ref, one_ref, qt_ref, kv_ref, vt_ref, *, scale):
    p = p_ref[0]
    cos, sin, perm = c_ref[...], s_ref[...], perm_ref[...]
    cq = p[:, 0:Q_LORA].astype(F32)
    cqn = (cq * lax.rsqrt(jnp.mean(cq * cq, axis=-1, keepdims=True) + EPS) * qg_ref[...]).astype(BF16)
    q_nope = _dot(cqn, wqn_ref[...]).astype(BF16)
    q_rope = _dot(cqn, wqr_ref[...]).astype(BF16)
    q_rot = (_rot(q_rope, cos, sin, perm) * scale).astype(BF16)
    q_nope_s = (q_nope.astype(F32) * scale).astype(BF16)
    for h in range(N_HEADS):
        qt_ref[0, h] = (_dot_nt(wa_ref[h], q_nope_s) + _dot_nt(selq_ref[h], q_rot)).astype(BF16)
    ckv = p[:, Q_LORA:Q_LORA + KV_LORA].astype(F32)
    ckvn = (ckv * lax.rsqrt(jnp.mean(ckv * ckv, axis=-1, keepdims=True) + EPS) * kg_ref[...]).astype(BF16)
    kr = p[:, Q_LORA + KV_LORA:MLA_PAD]
    kr_rot = _rot(kr, cos, sin, perm).astype(BF16)
    kv_ref[0] = (_dot(ckvn, selc_ref[...]) + _dot(kr_rot, selr_ref[...])).astype(BF16)
    vt_ref[0] = (_dot_nt(selv_ref[...], ckvn) + one_ref[...]).astype(BF16)


def _mla_attn_kernel(qt_ref, kv_ref, vt_ref, wuv_ref, y_ref, m_ref, acc_ref, s_ref, *, tq, tk, n_ctx, nt):
    i = pl.program_id(1)
    heads = range(N_HEADS)
    m_ref[...] = jnp.full_like(m_ref, -jnp.inf)
    acc_ref[...] = jnp.zeros_like(acc_ref)

    def scores(j0, size, slot):
        k = kv_ref[0, pl.ds(j0, size), :]
        for h in heads:
            s_ref[slot, h, 0:size, :] = _dot(k, qt_ref[0, h])

    def softmax_pv(j0, size, slot):
        vt = vt_ref[0, :, pl.ds(j0, size)]
        s = [s_ref[slot, h, 0:size, :] for h in heads]
        m_old = [m_ref[h] for h in heads]
        m_new = [jnp.maximum(m_old[h], jnp.max(s[h], axis=0, keepdims=True)) for h in heads]
        pr = [jnp.exp2(s[h] - m_new[h]).astype(BF16) for h in heads]
        pv = [_dot(vt, pr[h]) for h in heads]
        for h in heads:
            acc_ref[h] = jnp.exp2(m_old[h] - m_new[h]) * acc_ref[h] + pv[h]
            m_ref[h] = m_new[h]

    scores(0, n_ctx, 0)
    is_latent = (i + 1) * tq > n_ctx

    @pl.when(jnp.logical_not(is_latent))
    def _():
        softmax_pv(0, n_ctx, 0)

    @pl.when(is_latent)
    def _():
        n_tiles = (nt - n_ctx) // tk
        last = n_ctx + (n_tiles - 1) * tk
        scores(n_ctx, tk, 1)
        softmax_pv(0, n_ctx, 0)

        def body(jj, carry):
            t0 = pl.multiple_of(n_ctx + 2 * jj * tk, 256)
            t1 = pl.multiple_of(jnp.minimum(t0 + tk, last), 256)
            t2 = pl.multiple_of(jnp.minimum(t0 + 2 * tk, last), 256)
            scores(t1, tk, 0)
            softmax_pv(t0, tk, 1)
            scores(t2, tk, 1)
            softmax_pv(t1, tk, 0)
            return carry

        lax.fori_loop(0, n_tiles // 2, body, 0)
        if n_tiles % 2:
            softmax_pv(last, tk, 1)

    y = None
    for h in range(N_HEADS):
        acc = acc_ref[h]
        o = (acc[0:KV_LORA, :] / acc[KV_LORA:KV_LORA + 1, :]).astype(BF16)
        term = _dot_tn(o, wuv_ref[h])
        y = term if y is None else y + term
    y_ref[0] = y.astype(BF16)


def _mla(p, cos, sin, perm, qg, kg, wqn, wqr, wa, selq, selc, selr, selv, one_col, wuv, n_ctx):
    nb, nt, _ = p.shape
    tm = _pick(nt, (768, 384, 256, 128))
    scale = (NOPE_DIM + ROPE_DIM) ** -0.5 * math.log2(math.e)
    c2 = lambda b, i: (0, 0)
    c3 = lambda b, i: (0, 0, 0)
    qt, kv, vt = pl.pallas_call(
        functools.partial(_mla_prep_kernel, scale=scale),
        grid=(nb, nt // tm),
        in_specs=[pl.BlockSpec((1, tm, MLA_PAD), lambda b, i: (b, i, P_MLA // MLA_PAD)),
                  pl.BlockSpec((tm, 128), lambda b, i: (i, 0)),
                  pl.BlockSpec((tm, 128), lambda b, i: (i, 0)),
                  pl.BlockSpec((128, 128), c2),
                  pl.BlockSpec((1, Q_LORA), c2),
                  pl.BlockSpec((1, KV_LORA), c2),
                  pl.BlockSpec((Q_LORA, N_HEADS * NOPE_DIM), c2),
                  pl.BlockSpec((Q_LORA, N_HEADS * ROPE_DIM), c2),
                  pl.BlockSpec((N_HEADS, QK_W, N_HEADS * NOPE_DIM), c3),
                  pl.BlockSpec((N_HEADS, QK_W, N_HEADS * ROPE_DIM), c3),
                  pl.BlockSpec((KV_LORA, QK_W), c2),
                  pl.BlockSpec((128, QK_W), c2),
                  pl.BlockSpec((VT_ROWS, KV_LORA), c2),
                  pl.BlockSpec((VT_ROWS, 1), c2)],
        out_specs=[pl.BlockSpec((1, N_HEADS, QK_W, tm), lambda b, i: (b, 0, 0, i)),
                   pl.BlockSpec((1, tm, QK_W), lambda b, i: (b, i, 0)),
                   pl.BlockSpec((1, VT_ROWS, tm), lambda b, i: (b, 0, i))],
        out_shape=[jax.ShapeDtypeStruct((nb, N_HEADS, QK_W, nt), BF16),
                   jax.ShapeDtypeStruct((nb, nt, QK_W), BF16),
                   jax.ShapeDtypeStruct((nb, VT_ROWS, nt), BF16)],
        compiler_params=_params("arbitrary", "arbitrary"),
        name="mla_prep",
    )(p, cos, sin, perm, qg, kg, wqn, wqr, wa, selq, selc, selr, selv, one_col)
    tq = 256
    tk = _pick(nt - n_ctx, (512, 256))
    return pl.pallas_call(
        functools.partial(_mla_attn_kernel, tq=tq, tk=tk, n_ctx=n_ctx, nt=nt),
        grid=(nb, nt // tq),
        in_specs=[pl.BlockSpec((1, N_HEADS, QK_W, tq), lambda b, i: (b, 0, 0, i)),
                  pl.BlockSpec((1, nt, QK_W), lambda b, i: (b, 0, 0)),
                  pl.BlockSpec((1, VT_ROWS, nt), lambda b, i: (b, 0, 0)),
                  pl.BlockSpec((N_HEADS, KV_LORA, MIX_W), c3)],
        out_specs=pl.BlockSpec((1, tq, MIX_W), lambda b, i: (b, i, 0)),
        out_shape=jax.ShapeDtypeStruct((nb, nt, MIX_W), BF16),
        scratch_shapes=[pltpu.VMEM((N_HEADS, 1, tq), F32), pltpu.VMEM((N_HEADS, VT_ROWS, tq), F32),
                        pltpu.VMEM((2, N_HEADS, max(tk, n_ctx), tq), F32)],
        compiler_params=_params("arbitrary", "arbitrary"),
        name="mla_attn",
    )(qt, kv, vt, wuv)


def _merge_kernel(x_ref, yr_ref, ys_ref, of_ref, ob_ref, ym_ref, z_ref, g0_ref, g1_ref, g2_ref, g3_ref,
                  wb_ref, wo_ref, ng_ref, gp_ref, ml_ref, mc_ref, ones_ref, o_ref, *, tm, n_ctx):
    i = pl.program_id(1)
    od = of_ref[0] + ob_ref[0]
    ms = _head_sum(od * od, ones_ref[...]) * (1.0 / HEAD_DIM)
    z = z_ref[0].astype(F32)
    ydn = (od * lax.rsqrt(ms + EPS) * ng_ref[...]) * (z * jax.nn.sigmoid(z))
    ys = (yr_ref[0], ys_ref[0], ydn.astype(BF16), ym_ref[0])
    gates = (g0_ref, g1_ref, g2_ref, g3_ref)
    acc = None
    for b in range(N_BRANCH):
        term = jax.nn.sigmoid(gates[b][0].astype(F32)) * _dot(ys[b], wb_ref[b])
        acc = term if acc is None else acc + term
    y = _dot(acc.astype(BF16), wo_ref[...])
    r = y * lax.rsqrt(jnp.mean(y * y, axis=-1, keepdims=True) + EPS) * gp_ref[...]
    rows = lax.broadcasted_iota(jnp.int32, (tm, 1), 0) + i * tm
    gate = jnp.where(rows < n_ctx, mc_ref[0, 2:3, :], ml_ref[0, 2:3, :])
    o_ref[0] = x_ref[0] + gate * r


def _merge(xa, y_ret, y_sg, o_f, o_b, y_mla, p, wb, wo, ng, gp, mod, ones_bd, n_ctx):
    nb, nt, d = xa.shape
    tm = _pick(nt, (768, 384, 256, 128))
    row = lambda b, i: (b, i, 0)
    c2 = lambda b, i: (0, 0)
    y_spec = pl.BlockSpec((1, tm, MIX_W), row)
    gate_specs = [pl.BlockSpec((1, tm, d), functools.partial(lambda b, i, k: (b, i, k), k=P_GATE // d + k))
                  for k in range(N_BRANCH)]
    return pl.pallas_call(
        functools.partial(_merge_kernel, tm=tm, n_ctx=n_ctx),
        grid=(nb, nt // tm),
        in_specs=[pl.BlockSpec((1, tm, d), row), y_spec, y_spec, y_spec, y_spec, y_spec,
                  pl.BlockSpec((1, tm, MIX_W), lambda b, i: (b, i, (P_DN + 3 * MIX_W) // MIX_W)),
                  *gate_specs,
                  pl.BlockSpec((N_BRANCH, MIX_W, d), lambda b, i: (0, 0, 0)),
                  pl.BlockSpec((d, d), c2),
                  pl.BlockSpec((1, MIX_W), c2),
                  pl.BlockSpec((1, d), c2),
                  pl.BlockSpec((1, 6, d), lambda b, i: (b, 0, 0)),
                  pl.BlockSpec((1, 6, d), lambda b, i: (nb, 0, 0)),
                  pl.BlockSpec((MIX_W, MIX_W), c2)],
        out_specs=pl.BlockSpec((1, tm, d), row),
        out_shape=jax.ShapeDtypeStruct((nb, nt, d), F32),
        compiler_params=_params("arbitrary", "arbitrary"),
        name="merge",
    )(xa, y_ret, y_sg, o_f, o_b, y_mla, p, p, p, p, p, wb, wo, ng, gp, mod, mod, ones_bd)


def _route(sel, aff):
    rows = [sel[e:e + 1, :] for e in range(N_EXPERTS)]
    pairs = [(a, b) for a in range(EXPERTS_PER_GROUP) for b in range(a + 1, EXPERTS_PER_GROUP)]
    grp_score, grp_pair = [], []
    for g in range(N_GROUPS):
        base = g * EXPERTS_PER_GROUP
        best = rows[base + pairs[0][0]] + rows[base + pairs[0][1]]
        best_p = jnp.zeros_like(best, dtype=jnp.int32)
        for pi in range(1, len(pairs)):
            s = rows[base + pairs[pi][0]] + rows[base + pairs[pi][1]]
            take = s > best
            best = jnp.where(take, s, best)
            best_p = jnp.where(take, pi, best_p)
        grp_score.append(best)
        grp_pair.append(best_p)
    top = grp_score[0]
    top_g = jnp.zeros_like(grp_pair[0])
    top_p = grp_pair[0]
    for g in range(1, N_GROUPS):
        take = grp_score[g] > top
        top = jnp.where(take, grp_score[g], top)
        top_g = jnp.where(take, g, top_g)
        top_p = jnp.where(take, grp_pair[g], top_p)
    picked = []
    for e in range(N_EXPERTS):
        g, k = divmod(e, EXPERTS_PER_GROUP)
        in_pair = None
        for pi, (a, b) in enumerate(pairs):
            if k in (a, b):
                hit = top_p == pi
                in_pair = hit if in_pair is None else (in_pair | hit)
        picked.append(jnp.where((top_g == g) & in_pair, aff[e:e + 1, :], 0.0))
    denom = picked[0]
    for e in range(1, N_EXPERTS):
        denom = denom + picked[e]
    return [pk / denom for pk in picked]


def _swiglu(hn, w1, w3, w2, scale):
    a = _dot(hn, w1.astype(BF16))
    h = (a * jax.nn.sigmoid(a)) * _dot(hn, w3.astype(BF16))
    if scale is not None:
        h = h * scale
    return _dot(h.astype(BF16), w2.astype(BF16))


def _moe_kernel(x_ref, ml_ref, mc_ref, g2_ref, gp_ref, rw_ref, rb_ref, ws1_ref, ws3_ref, ws2_ref,
                w1_ref, w3_ref, w2_ref, o_ref, hn_ref, comb_t_ref, comb_ref, acc_ref, *, tm, rb, n_ctx):
    i = pl.program_id(1)
    e = pl.program_id(2)

    @pl.when(e == 0)
    def _():
        def blk(r, carry):
            r0 = pl.multiple_of(r * rb, rb)
            x = x_ref[0, pl.ds(r0, rb), :]
            rows = lax.broadcasted_iota(jnp.int32, (rb, 1), 0) + (i * tm + r0)
            hn = _norm_modulate(x, g2_ref[...], rows < n_ctx, mc_ref, ml_ref, 3, 4)
            hn_ref[pl.ds(r0, rb), :] = hn.astype(BF16)
            logit = _dot_nt(rw_ref[...], hn.astype(BF16))
            aff = jax.nn.sigmoid(logit)
            comb = _route(aff + rb_ref[...], aff)
            for k in range(N_EXPERTS):
                comb_t_ref[k:k + 1, pl.ds(r0, rb)] = comb[k]
            return carry

        comb_t_ref[...] = jnp.zeros_like(comb_t_ref)
        lax.fori_loop(0, tm // rb, blk, 0)
        comb_ref[...] = comb_t_ref[...].T
        acc_ref[...] = _swiglu(hn_ref[...], ws1_ref[0], ws3_ref[0], ws2_ref[0], None)

    @pl.when(e > 0)
    def _():
        lane = lax.broadcasted_iota(jnp.int32, (1, 128), 1)
        comb = comb_ref[...]
        hn = hn_ref[...]
        first = 2 * (e - 1)
        y = None
        for k in range(2):
            c_k = jnp.sum(jnp.where(lane == first + k, comb, 0.0), axis=-1, keepdims=True)
            term = _swiglu(hn, w1_ref[0, k], w3_ref[0, k], w2_ref[0, k], c_k)
            y = term if y is None else y + term
        acc_ref[...] += y

    @pl.when(e == pl.num_programs(2) - 1)
    def _():
        y = acc_ref[...]
        r = y * lax.rsqrt(jnp.mean(y * y, axis=-1, keepdims=True) + EPS) * gp_ref[...]
        rows = lax.broadcasted_iota(jnp.int32, (tm, 1), 0) + i * tm
        gate = jnp.where(rows < n_ctx, mc_ref[0, 5:6, :], ml_ref[0, 5:6, :])
        o_ref[0] = x_ref[0] + gate * r


def _moe(xa, mod, g2, gp, rw_t, rbias, ws1, ws3, ws2, w1, w3, w2, layer, n_ctx):
    nb, nt, d = xa.shape
    tm = _pick(nt, (768, 384, 256, 128))
    n_pairs = w1.shape[1] // 2
    row = lambda b, i, e: (b, i, 0)
    c2 = lambda b, i, e: (0, 0)
    shared_blk = lambda b, i, e: (layer, 0, 0)
    pair_blk = lambda b, i, e: (layer, jnp.maximum(e - 1, 0), 0, 0)
    return pl.pallas_call(
        functools.partial(_moe_kernel, tm=tm, rb=128, n_ctx=n_ctx),
        grid=(nb, nt // tm, n_pairs + 1),
        in_specs=[pl.BlockSpec((1, tm, d), row),
                  pl.BlockSpec((1, 6, d), lambda b, i, e: (b, 0, 0)),
                  pl.BlockSpec((1, 6, d), lambda b, i, e: (nb, 0, 0)),
                  pl.BlockSpec((1, d), c2),
                  pl.BlockSpec((1, d), c2),
                  pl.BlockSpec((N_EXPERTS, d), c2),
                  pl.BlockSpec((N_EXPERTS, 1), c2),
                  pl.BlockSpec((1, d, D_EXPERT), shared_blk),
                  pl.BlockSpec((1, d, D_EXPERT), shared_blk),
                  pl.BlockSpec((1, D_EXPERT, d), shared_blk),
                  pl.BlockSpec((1, 2, d, D_EXPERT), pair_blk),
                  pl.BlockSpec((1, 2, d, D_EXPERT), pair_blk),
                  pl.BlockSpec((1, 2, D_EXPERT, d), pair_blk)],
        out_specs=pl.BlockSpec((1, tm, d), row),
        out_shape=jax.ShapeDtypeStruct((nb, nt, d), F32),
        scratch_shapes=[pltpu.VMEM((tm, d), BF16), pltpu.VMEM((128, tm), F32), pltpu.VMEM((tm, 128), F32),
                        pltpu.VMEM((tm, d), F32)],
        compiler_params=_params("arbitrary", "arbitrary", "arbitrary"),
        name="moe",
    )(xa, mod, mod, g2, gp, rw_t, rbias, ws1, ws3, ws2, w1, w3, w2)


def _swap_perm(width, group):
    j = np.arange(width)
    src = np.where((j % group) < group // 2, j + group // 2, j - group // 2)
    return jnp.asarray(np.arange(width)[:, None] == src[None, :], BF16)


def _rope_tables(n_lat, n_ctx):
    def angles(pos, dim):
        half = dim // 2
        inv = ROPE_BASE ** (-jnp.arange(half, dtype=F32) / half)
        return pos.astype(F32)[:, None] * inv[None, :]

    def tables(cos_parts, sin_parts, reps):
        cos = jnp.tile(jnp.concatenate(cos_parts, axis=-1), (1, reps))
        sin = jnp.tile(jnp.concatenate(sin_parts, axis=-1), (1, reps))
        w = cos.shape[1]
        return (jnp.concatenate([jnp.ones((n_ctx, w), F32), cos], axis=0),
                jnp.concatenate([jnp.zeros((n_ctx, w), F32), sin], axis=0))

    rows = n_lat // GRID_W
    ang_t = angles(jnp.arange(n_lat), HEAD_DIM)
    ang_r = angles(jnp.repeat(jnp.arange(rows), GRID_W), ROPE_DIM // 2)
    ang_c = angles(jnp.tile(jnp.arange(GRID_W), rows), ROPE_DIM // 2)
    ct, st = jnp.cos(ang_t), jnp.sin(ang_t)
    ret = tables([ct, ct], [-st, st], N_HEADS)
    cr, sr, cc, sc = jnp.cos(ang_r), jnp.sin(ang_r), jnp.cos(ang_c), jnp.sin(ang_c)
    mla = tables([cr, cr, cc, cc], [-sr, sr, -sc, sc], N_HEADS)
    return ret, mla


def _ret_tables(logit):
    log_g = jax.nn.log_sigmoid(logit.astype(F32))
    lane_lg = jnp.repeat(log_g, HEAD_DIM, axis=1)
    idx = jnp.arange(CHUNK, dtype=F32)[:, None]
    kd = jnp.stack([jnp.exp(lane_lg[0][None, :] * (CHUNK - 1 - idx)), jnp.exp(lane_lg[1][None, :] * idx)])
    qd = jnp.stack([jnp.exp(lane_lg[0][None, :] * (idx + 1)), jnp.exp(lane_lg[1][None, :] * (CHUNK - idx))])
    cd = jnp.exp(lane_lg * CHUNK)[:, None, :]
    diff = idx - idx.T
    blocks = []
    for h in range(N_HEADS):
        f = jnp.exp(log_g[0, h] * jnp.where(diff >= 0, diff, 0.0))
        b = jnp.exp(log_g[1, h] * jnp.where(diff < 0, -diff, 0.0))
        blocks.append(jnp.where(diff >= 0, f, b))
    dm = jnp.concatenate(blocks, axis=1)
    return kd, cd, qd, dm


def _pack_w_in(w_in):
    d = w_in.shape[0]
    mla = jnp.concatenate([w_in[:, OFF_MLA:OFF_MLA + MLA_COLS], jnp.zeros((d, MLA_PAD - MLA_COLS), w_in.dtype)], 1)
    w = jnp.concatenate([w_in[:, OFF_RET:OFF_RET + RET_COLS], w_in[:, OFF_DN:OFF_DN + 4 * MIX_W],
                         w_in[:, OFF_SG:OFF_SG + SG_COLS], mla, w_in[:, OFF_GATE:OFF_GATE + GATE_COLS]], axis=1)
    wab = w_in[:, OFF_DN + 4 * MIX_W:OFF_DN + DN_COLS]
    wabc = jnp.concatenate([wab, jnp.zeros((d, 128 - 4 * N_HEADS), w_in.dtype)], axis=1)
    return w.astype(BF16), wab.T.astype(BF16), wabc.astype(BF16)


def _mla_weights(w_uq, w_ukv):
    dq = NOPE_DIM + ROPE_DIM
    dkv = NOPE_DIM + V_DIM
    wq = w_uq.reshape(Q_LORA, N_HEADS, dq)
    wqn = wq[:, :, :NOPE_DIM].reshape(Q_LORA, N_HEADS * NOPE_DIM)
    wqr = wq[:, :, NOPE_DIM:].reshape(Q_LORA, N_HEADS * ROPE_DIM)
    wkv = w_ukv.reshape(KV_LORA, N_HEADS, dkv)
    head_eye = jnp.eye(N_HEADS, dtype=F32)
    wa = jnp.einsum("chd,hg->hcgd", wkv[:, :, :NOPE_DIM], head_eye).reshape(N_HEADS, KV_LORA, N_HEADS * NOPE_DIM)
    wa = jnp.pad(wa, ((0, 0), (0, QK_W - KV_LORA), (0, 0)))
    wuv = jnp.einsum("chd,hg->hcgd", wkv[:, :, NOPE_DIM:], head_eye).reshape(N_HEADS, KV_LORA, MIX_W)
    selq = np.zeros((N_HEADS, QK_W, N_HEADS * ROPE_DIM), np.float32)
    for h in range(N_HEADS):
        selq[h, KV_LORA:KV_LORA + ROPE_DIM, h * ROPE_DIM:(h + 1) * ROPE_DIM] = np.eye(ROPE_DIM)
    selc = np.zeros((KV_LORA, QK_W), np.float32)
    selc[:, 0:KV_LORA] = np.eye(KV_LORA)
    selr = np.zeros((128, QK_W), np.float32)
    selr[0:ROPE_DIM, KV_LORA:KV_LORA + ROPE_DIM] = np.eye(ROPE_DIM)
    selv = np.zeros((VT_ROWS, KV_LORA), np.float32)
    selv[0:KV_LORA, :] = np.eye(KV_LORA)
    one_col = np.zeros((VT_ROWS, 1), np.float32)
    one_col[KV_LORA, 0] = 1.0
    return (tuple(jnp.asarray(a, BF16) for a in (wqn, wqr, wa, selq, selc, selr, selv))
            + (jnp.asarray(one_col), wuv.astype(BF16)))


def kernel(x, c, ctx, c_ctx, w_ada, b_ada, g_pre1, g_post1, g_pre2, g_post2, w_in, ret_decay_logit, sg_norm_g, sg_w, sg_b, dn_conv_w, dn_A_log, dn_dt_bias, dn_norm_g, mla_q_norm_g, mla_kv_norm_g, mla_w_uq, mla_w_ukv, w_branch, w_out, router_w, router_bias, moe_w1, moe_w3, moe_w2, shared_w1, shared_w3, shared_w2):
    nb, n_lat, d = x.shape
    n_ctx = ctx.shape[1]
    depth = w_in.shape[0]
    assert d == D_MODEL and n_lat % GRID_W == 0 and n_lat % CHUNK == 0 and n_ctx % 256 == 0
    ncc = n_ctx // CHUNK

    n_cond = -(-(nb + 1) // 8) * 8
    cond = jnp.concatenate([c, c_ctx[None], jnp.zeros((n_cond - nb - 1, d), F32)], axis=0)
    mod_all = _adaln(cond, w_ada, b_ada).reshape(depth, n_cond, 6, d)

    (ret_cos, ret_sin), (mla_cos, mla_sin) = _rope_tables(n_lat, n_ctx)
    perm_ret = _swap_perm(MIX_W, HEAD_DIM)
    perm_mla = _swap_perm(N_HEADS * ROPE_DIM, ROPE_DIM // 2)
    lane_head = jnp.arange(MIX_W) // HEAD_DIM
    bd = (lane_head[:, None] == lane_head[None, :]).astype(F32)
    ones_bd = bd.astype(BF16)
    rw_t = router_w.T.astype(BF16)
    rbias = router_bias.astype(F32)[:, None]

    xa = jnp.concatenate([ctx, x], axis=1)
    for l in range(depth):
        mod = mod_all[l]
        w_l, wab_l, wabc_l = _pack_w_in(w_in[l])
        p, ab_t, ab_c = _inproj(xa, mod, g_pre1[l][None], w_l, wab_l, wabc_l, n_ctx)

        kd, cd, qd, dm = _ret_tables(ret_decay_logit[l])
        wcat = jnp.concatenate([sg_w[l, h] for h in range(N_HEADS)], axis=1).astype(BF16)
        sg_bias = jnp.repeat(sg_b[l].T, HEAD_DIM, axis=1)
        y_ret, y_sg = _retention_and_sgate(p, ret_cos, ret_sin, perm_ret, (kd, cd, qd, dm, bd, ones_bd),
                                           sg_norm_g[l][None], wcat, sg_bias, ncc)

        neg_a = (-jnp.exp(dn_A_log[l].astype(F32))).reshape(2 * N_HEADS, 1)
        dtb = dn_dt_bias[l].astype(F32).reshape(2 * N_HEADS, 1)
        conv_w = jnp.concatenate([dn_conv_w[l], jnp.zeros((8 - CONV_W, 3 * MIX_W), F32)], axis=0)
        o_f, o_b = _deltanet(p, ab_t, ab_c, conv_w, neg_a, dtb, bd, ones_bd, ncc)

        y_mla = _mla(p, mla_cos, mla_sin, perm_mla, mla_q_norm_g[l][None], mla_kv_norm_g[l][None],
                     *_mla_weights(mla_w_uq[l], mla_w_ukv[l]), n_ctx)

        xa = _merge(xa, y_ret, y_sg, o_f, o_b, y_mla, p, w_branch[l].astype(BF16), w_out[l].astype(BF16),
                    jnp.tile(dn_norm_g[l], N_HEADS)[None], g_post1[l][None], mod, ones_bd, n_ctx)

        xa = _moe(xa, mod, g_pre2[l][None], g_post2[l][None], rw_t, rbias, shared_w1, shared_w3, shared_w2,
                  moe_w1, moe_w3, moe_w2, l, n_ctx)
    return xa[:, n_ctx:]
```

```python
import functools
import math

import jax
import jax.numpy as jnp
import numpy as np
from jax import lax
from jax.experimental import pallas as pl
from jax.experimental.pallas import tpu as pltpu

F32 = jnp.float32
BF16 = jnp.bfloat16
HIGHEST = lax.Precision.HIGHEST

D_MODEL = 1024
GRID_W = 64
N_HEADS = 4
HEAD_DIM = 64
MIX_W = N_HEADS * HEAD_DIM
CHUNK = 128
ROPE_BASE = 10000.0
EPS = 1e-6
RET_DECAY_EXP0 = 5.0
CONV_W = 5
Q_LORA = 256
KV_LORA = 128
NOPE_DIM = 64
ROPE_DIM = 32
V_DIM = 64
N_EXPERTS = 16
N_GROUPS = 4
EXPERTS_PER_GROUP = N_EXPERTS // N_GROUPS
D_EXPERT = 256
N_BRANCH = 4

RET_COLS = 4 * MIX_W
SG_COLS = 2 * MIX_W
DN_COLS = 4 * MIX_W + 4 * N_HEADS
MLA_COLS = Q_LORA + KV_LORA + ROPE_DIM
GATE_COLS = N_BRANCH * D_MODEL
OFF_RET = 0
OFF_SG = OFF_RET + RET_COLS
OFF_DN = OFF_SG + SG_COLS
OFF_MLA = OFF_DN + DN_COLS
OFF_GATE = OFF_MLA + MLA_COLS

P_RET = 0
P_DN = 1024
P_SG = 2048
P_MLA = 2560
P_GATE = 3072
P_COLS = 7168
MLA_PAD = 512

VMEM_LIMIT = 56 * 1024 * 1024


def _dot(a, b, precision=None):
    return jnp.dot(a, b, preferred_element_type=F32, precision=precision)


def _dot_nt(a, b, precision=None):
    return lax.dot_general(a, b, (((1,), (1,)), ((), ())), preferred_element_type=F32, precision=precision)


def _dot_tn(a, b):
    return lax.dot_general(a, b, (((0,), (0,)), ((), ())), preferred_element_type=F32)


def _mm(a, b):
    return _dot(a.astype(BF16), b.astype(BF16))


def _params(*sem):
    return pltpu.CompilerParams(dimension_semantics=sem, vmem_limit_bytes=VMEM_LIMIT)


def _pick(n, cands):
    for c in cands:
        if n % c == 0:
            return c
    raise ValueError(f"no tile for {n}")


def _head_of_lane(width, group):
    return lax.broadcasted_iota(jnp.int32, (1, width), 1) // group


def _stack_heads(x):
    head = _head_of_lane(MIX_W, HEAD_DIM)
    xf = x.astype(F32)
    return jnp.concatenate([jnp.where(head == h, xf, 0.0).astype(BF16) for h in range(N_HEADS)], axis=0)


def _expand_heads(cols):
    head = _head_of_lane(MIX_W, HEAD_DIM)
    out = cols[:, N_HEADS - 1:N_HEADS]
    for h in range(N_HEADS - 2, -1, -1):
        out = jnp.where(head <= h, cols[:, h:h + 1], out)
    return out


def _head_sum(x, ones_bd):
    hi = x.astype(BF16)
    lo = (x - hi.astype(F32)).astype(BF16)
    return _dot(hi, ones_bd) + _dot(lo, ones_bd)


def _rot(x_bf, cos, sin, perm):
    return x_bf.astype(F32) * cos + _dot(x_bf, perm) * sin


def _norm_modulate(x, g, is_ctx, mc_ref, ml_ref, shift_row, scale_row):
    shift = jnp.where(is_ctx, mc_ref[0, shift_row:shift_row + 1, :], ml_ref[0, shift_row:shift_row + 1, :])
    scale = jnp.where(is_ctx, mc_ref[0, scale_row:scale_row + 1, :], ml_ref[0, scale_row:scale_row + 1, :])
    gain = g * (1.0 + scale)
    return x * lax.rsqrt(jnp.mean(x * x, axis=-1, keepdims=True) + EPS) * gain + shift


def _adaln_kernel(c_ref, w_ref, b_ref, o_ref):
    c = c_ref[...]
    s = c * jax.nn.sigmoid(c)
    o_ref[0] = _dot(s, w_ref[0], precision=HIGHEST) + b_ref[0]


def _adaln(cond, w_ada, b_ada):
    n_l, d, d6 = w_ada.shape
    r = cond.shape[0]
    tn = 1024
    return pl.pallas_call(
        _adaln_kernel,
        grid=(n_l, d6 // tn),
        in_specs=[pl.BlockSpec((r, d), lambda l, j: (0, 0)),
                  pl.BlockSpec((1, d, tn), lambda l, j: (l, 0, j)),
                  pl.BlockSpec((1, 1, tn), lambda l, j: (l, 0, j))],
        out_specs=pl.BlockSpec((1, r, tn), lambda l, j: (l, 0, j)),
        out_shape=jax.ShapeDtypeStruct((n_l, r, d6), F32),
        compiler_params=_params("arbitrary", "arbitrary"),
        name="adaln",
    )(cond, w_ada, b_ada.reshape(n_l, 1, d6))


def _inproj_kernel(x_ref, ml_ref, mc_ref, g_ref, w_ref, wab_ref, wabc_ref, p_ref, ab_ref, abc_ref, xn_ref,
                   *, tm, rb, n_ctx):
    i = pl.program_id(1)
    j = pl.program_id(2)

    @pl.when(j == 0)
    def _():
        def blk(r, carry):
            r0 = pl.multiple_of(r * rb, rb)
            x = x_ref[0, pl.ds(r0, rb), :]
            hn = _norm_modulate(x, g_ref[...], i * tm + r0 < n_ctx, mc_ref, ml_ref, 0, 1)
            xn_ref[pl.ds(r0, rb), :] = hn.astype(BF16)
            return carry

        lax.fori_loop(0, tm // rb, blk, 0)
        ab_ref[0] = _dot_nt(wab_ref[...], xn_ref[...])
        abc_ref[0] = _dot(xn_ref[...], wabc_ref[...])

    p_ref[0] = _dot(xn_ref[...], w_ref[...]).astype(BF16)


def _inproj(xa, mod, g, w, wab, wabc, n_ctx):
    nb, nt, d = xa.shape
    tm = _pick(nt, (1408, 768, 384, 256, 128))
    tn = 1792
    kern = functools.partial(_inproj_kernel, tm=tm, rb=128, n_ctx=n_ctx)
    return pl.pallas_call(
        kern,
        grid=(nb, nt // tm, P_COLS // tn),
        in_specs=[pl.BlockSpec((1, tm, d), lambda b, i, j: (b, i, 0)),
                  pl.BlockSpec((1, 6, d), lambda b, i, j: (b, 0, 0)),
                  pl.BlockSpec((1, 6, d), lambda b, i, j: (nb, 0, 0)),
                  pl.BlockSpec((1, d), lambda b, i, j: (0, 0)),
                  pl.BlockSpec((d, tn), lambda b, i, j: (0, j)),
                  pl.BlockSpec((16, d), lambda b, i, j: (0, 0)),
                  pl.BlockSpec((d, 128), lambda b, i, j: (0, 0))],
        out_specs=[pl.BlockSpec((1, tm, tn), lambda b, i, j: (b, i, j)),
                   pl.BlockSpec((1, 16, tm), lambda b, i, j: (b, 0, i)),
                   pl.BlockSpec((1, tm, 128), lambda b, i, j: (b, i, 0))],
        out_shape=[jax.ShapeDtypeStruct((nb, nt, P_COLS), BF16),
                   jax.ShapeDtypeStruct((nb, 16, nt), F32),
                   jax.ShapeDtypeStruct((nb, nt, 128), F32)],
        scratch_shapes=[pltpu.VMEM((tm, d), BF16)],
        compiler_params=_params("arbitrary", "arbitrary", "arbitrary"),
        name="inproj",
    )(xa, mod, mod, g, w, wab, wabc)


def _bwd_chunk(t, ncc, nc):
    return jnp.where(t < ncc, ncc - 1 - t, nc - 1 - (t - ncc))


def _ret_state_kernel(pf_ref, pb_ref, cf_ref, sf_ref, cb_ref, sb_ref, perm_ref, kd_ref, cd_ref, bd_ref,
                      of_ref, ob_ref, st_f, st_b, *, cb):
    t = pl.program_id(1)

    @pl.when(t == 0)
    def _():
        st_f[...] = jnp.zeros_like(st_f)
        st_b[...] = jnp.zeros_like(st_b)

    def increments(p_ref, c_ref, s_ref, d):
        out = []
        for i in range(cb):
            r = slice(i * CHUNK, (i + 1) * CHUNK)
            kr = _rot(p_ref[0, r, MIX_W:2 * MIX_W], c_ref[r, :], s_ref[r, :], perm_ref[...]) * (HEAD_DIM ** -0.5)
            out.append(bd_ref[...] * _dot_tn((kr * kd_ref[d]).astype(BF16), p_ref[0, r, 2 * MIX_W:3 * MIX_W]))
        return out

    inc_f = increments(pf_ref, cf_ref, sf_ref, 0)
    inc_b = increments(pb_ref, cb_ref, sb_ref, 1)
    s = st_f[...]
    for i in range(cb):
        of_ref[0, i] = s.astype(BF16)
        s = cd_ref[0] * s + inc_f[i]
    st_f[...] = s
    s = st_b[...]
    for i in reversed(range(cb)):
        ob_ref[0, i] = s.astype(BF16)
        s = cd_ref[1] * s + inc_b[i]
    st_b[...] = s


def _gelu_tanh(x):
    return 0.5 * x * (1.0 + jnp.tanh(math.sqrt(2.0 / math.pi) * (x + 0.044715 * (x * x * x))))


def _mix_out_kernel(p_ref, pg_ref, c_ref, s_ref, sf_ref, sb_ref, perm_ref, dm_ref, qd_ref, ones_ref,
                    ng_ref, wg_ref, bg_ref, y_ref, ysg_ref, *, cb):
    chunks = range(cb)
    rows = [slice(i * CHUNK, (i + 1) * CHUNK) for i in chunks]
    perm, dm, ones_bd = perm_ref[...], dm_ref[...], ones_ref[...]
    p = [p_ref[0, r, :] for r in rows]
    cos = [c_ref[r, :] for r in rows]
    sin = [s_ref[r, :] for r in rows]
    qr = [_rot(p[i][:, 0:MIX_W], cos[i], sin[i], perm) for i in chunks]
    kr = [_rot(p[i][:, MIX_W:2 * MIX_W], cos[i], sin[i], perm) * (HEAD_DIM ** -0.5) for i in chunks]
    z = [_gelu_tanh(pg_ref[0, r, :].astype(F32)) for r in rows]
    vg = [x[:, MIX_W:] for x in z]
    mu_g = [jnp.mean(x, axis=-1, keepdims=True) for x in vg]
    vgc = [x - m for x, m in zip(vg, mu_g)]
    var_g = [jnp.mean(x * x, axis=-1, keepdims=True) for x in vgc]
    vn = [x * lax.rsqrt(s + EPS) * ng_ref[...] for x, s in zip(vgc, var_g)]
    sc = [_dot_nt(qr[i].astype(BF16), _stack_heads(kr[i])) * dm for i in chunks]
    mixed = [_dot(wg_ref[...], _stack_heads(x)) for x in vn]
    o = [_dot(sc[i].astype(BF16), _stack_heads(p[i][:, 2 * MIX_W:3 * MIX_W])) for i in chunks]
    qs = [jnp.concatenate([(qr[i] * qd_ref[0]).astype(BF16), (qr[i] * qd_ref[1]).astype(BF16)], axis=1)
          for i in chunks]
    ss = [jnp.concatenate([sf_ref[0, i], sb_ref[0, i]], axis=0) for i in chunks]
    o = [o[i] + _dot(qs[i], ss[i]) for i in chunks]
    for i in chunks:
        ysg_ref[0, rows[i], :] = (z[i][:, :MIX_W] * (mixed[i] + bg_ref[...])).astype(BF16)
    mu = [_head_sum(x, ones_bd) * (1.0 / HEAD_DIM) for x in o]
    oc = [x - m for x, m in zip(o, mu)]
    var = [_head_sum(x * x, ones_bd) * (1.0 / HEAD_DIM) for x in oc]
    for i in chunks:
        g = p[i][:, 3 * MIX_W:4 * MIX_W].astype(F32)
        y_ref[0, rows[i], :] = (oc[i] * lax.rsqrt(var[i] + EPS) * (g * jax.nn.sigmoid(g))).astype(BF16)


def _retention_and_sgate(p, cos, sin, perm, tabs, sg_ng, sg_w, sg_bias, ncc):
    nb, nt, _ = p.shape
    nc = nt // CHUNK
    kd, cd, qd, dm, bd, ones_bd = tabs
    cb = 2
    assert nc % cb == 0 and ncc % cb == 0
    nblk, ncb = nc // cb, ncc // cb
    fwd = lambda b, t: (b, t, 0)
    bwd = lambda b, t: (b, _bwd_chunk(t, ncb, nblk), 0)
    tab_f = lambda b, t: (t, 0)
    tab_b = lambda b, t: (_bwd_chunk(t, ncb, nblk), 0)
    c2 = lambda b, t: (0, 0)
    c3 = lambda b, t: (0, 0, 0)
    st_shape = jax.ShapeDtypeStruct((nb, nc, MIX_W, MIX_W), BF16)
    st_f, st_b = pl.pallas_call(
        functools.partial(_ret_state_kernel, cb=cb),
        grid=(nb, nblk),
        in_specs=[pl.BlockSpec((1, cb * CHUNK, RET_COLS), fwd),
                  pl.BlockSpec((1, cb * CHUNK, RET_COLS), bwd),
                  pl.BlockSpec((cb * CHUNK, MIX_W), tab_f), pl.BlockSpec((cb * CHUNK, MIX_W), tab_f),
                  pl.BlockSpec((cb * CHUNK, MIX_W), tab_b), pl.BlockSpec((cb * CHUNK, MIX_W), tab_b),
                  pl.BlockSpec((MIX_W, MIX_W), c2),
                  pl.BlockSpec((2, CHUNK, MIX_W), c3),
                  pl.BlockSpec((2, 1, MIX_W), c3),
                  pl.BlockSpec((MIX_W, MIX_W), c2)],
        out_specs=[pl.BlockSpec((1, cb, MIX_W, MIX_W), lambda b, t: (b, t, 0, 0)),
                   pl.BlockSpec((1, cb, MIX_W, MIX_W), lambda b, t: (b, _bwd_chunk(t, ncb, nblk), 0, 0))],
        out_shape=[st_shape, st_shape],
        scratch_shapes=[pltpu.VMEM((MIX_W, MIX_W), F32), pltpu.VMEM((MIX_W, MIX_W), F32)],
        compiler_params=_params("arbitrary", "arbitrary"),
        name="ret_state",
    )(p, p, cos, sin, cos, sin, perm, kd, cd, bd)
    blk = lambda b, t: (b, t, 0)
    y_shape = jax.ShapeDtypeStruct((nb, nt, MIX_W), BF16)
    return pl.pallas_call(
        functools.partial(_mix_out_kernel, cb=cb),
        grid=(nb, nc // cb),
        in_specs=[pl.BlockSpec((1, cb * CHUNK, RET_COLS), blk),
                  pl.BlockSpec((1, cb * CHUNK, SG_COLS), lambda b, t: (b, t, P_SG // SG_COLS)),
                  pl.BlockSpec((cb * CHUNK, MIX_W), tab_f), pl.BlockSpec((cb * CHUNK, MIX_W), tab_f),
                  pl.BlockSpec((1, cb, MIX_W, MIX_W), lambda b, t: (b, t, 0, 0)),
                  pl.BlockSpec((1, cb, MIX_W, MIX_W), lambda b, t: (b, t, 0, 0)),
                  pl.BlockSpec((MIX_W, MIX_W), c2),
                  pl.BlockSpec((CHUNK, N_HEADS * CHUNK), c2),
                  pl.BlockSpec((2, CHUNK, MIX_W), c3),
                  pl.BlockSpec((MIX_W, MIX_W), c2),
                  pl.BlockSpec((1, MIX_W), c2),
                  pl.BlockSpec((CHUNK, N_HEADS * CHUNK), c2),
                  pl.BlockSpec((CHUNK, MIX_W), c2)],
        out_specs=[pl.BlockSpec((1, cb * CHUNK, MIX_W), blk), pl.BlockSpec((1, cb * CHUNK, MIX_W), blk)],
        out_shape=[y_shape, y_shape],
        compiler_params=_params("arbitrary", "arbitrary"),
        name="mix_out",
    )(p, p, cos, sin, st_f, st_b, perm, dm, qd, ones_bd, sg_ng, sg_w, sg_bias)


def _softplus(a):
    return jnp.maximum(a, 0.0) + jnp.log1p(jnp.exp(-jnp.abs(a)))


def _dn_prep_kernel(pc_ref, pp_ref, pn_ref, ab_ref, abc_ref, cw_ref, na_ref, dtb_ref, nar_ref, dtbr_ref, ones_ref,
                    qkv_ref, gb_ref, gbc_ref, xe_ref, *, ncc, nc):
    t = pl.program_id(1)
    w3 = 3 * MIX_W
    prev_ok = jnp.where((t != 0) & (t != ncc), 1.0, 0.0)
    next_ok = jnp.where((t != ncc - 1) & (t != nc - 1), 1.0, 0.0)
    tail = pp_ref[0, CHUNK - 16:CHUNK, 0:w3].astype(F32)
    head = pn_ref[0, 0:16, 0:w3].astype(F32)
    xe_ref[0:8, :] = tail[8:16, :] * prev_ok
    xe_ref[8:8 + CHUNK, :] = pc_ref[0, :, 0:w3].astype(F32)
    xe_ref[8 + CHUNK:16 + CHUNK, :] = head[0:8, :] * next_ok
    pad = CONV_W // 2
    y = xe_ref[8 - pad:8 - pad + CHUNK, :] * cw_ref[0:1, :]
    for i in range(1, CONV_W):
        y = y + xe_ref[8 - pad + i:8 - pad + i + CHUNK, :] * cw_ref[i:i + 1, :]
    y = y * jax.nn.sigmoid(y)
    q = y[:, 0:MIX_W]
    k = y[:, MIX_W:2 * MIX_W]
    v = y[:, 2 * MIX_W:w3]
    ones_bd = ones_ref[...]
    qn = q * lax.rsqrt(_head_sum(q * q, ones_bd) + EPS) * (HEAD_DIM ** -0.5)
    kn = k * lax.rsqrt(_head_sum(k * k, ones_bd) + EPS)
    qkv_ref[0, :, 0:MIX_W] = qn.astype(BF16)
    qkv_ref[0, :, MIX_W:2 * MIX_W] = kn.astype(BF16)
    qkv_ref[0, :, 2 * MIX_W:w3] = v.astype(BF16)
    ab = ab_ref[0]
    gb_ref[0, 0:8, :] = na_ref[...] * _softplus(ab[0:8, :] + dtb_ref[...])
    gb_ref[0, 8:16, :] = jax.nn.sigmoid(ab[8:16, :])
    abc = abc_ref[0]
    lane = lax.broadcasted_iota(jnp.int32, (1, 128), 1)
    g_c = nar_ref[...] * _softplus(abc + dtbr_ref[...])
    gbc_ref[0] = jnp.where(lane < 8, g_c, jnp.where(lane < 16, jax.nn.sigmoid(abc), 0.0))


def _split3(x):
    hi = x.astype(BF16)
    r = x - hi.astype(F32)
    mid = r.astype(BF16)
    lo = (r - mid.astype(F32)).astype(BF16)
    return hi, mid, lo


def _tri_inverse(mats, ii, jj):
    eye = jnp.where(ii == jj, 1.0, 0.0)
    nd = [jnp.where((ii // 16) == (jj // 16), n, 0.0) for n in mats]
    p1 = [_mm(x, x) for x in nd]
    m = [eye - x for x in nd]
    p2 = [_mm(x, x) for x in p1]
    m = [x + _mm(x, y) for x, y in zip(m, p1)]
    p3 = [_mm(x, x) for x in p2]
    m = [x + _mm(x, y) for x, y in zip(m, p2)]
    m = [x + _mm(x, y) for x, y in zip(m, p3)]
    for lvl in (16, 32, 64):
        off_mask = ((ii // (2 * lvl)) == (jj // (2 * lvl))) & ((ii // lvl) != (jj // lvl))
        t = [_mm(jnp.where(off_mask, n, 0.0), x) for n, x in zip(mats, m)]
        m = [x - _mm(x, y) for x, y in zip(m, t)]
    return m


def _dn_pre(qkv, g, gbc, d, lower):
    c = CHUNK
    qn = qkv[:, 0:MIX_W]
    kn = qkv[:, MIX_W:2 * MIX_W]
    v = qkv[:, 2 * MIX_W:3 * MIX_W]
    ii = lax.broadcasted_iota(jnp.int32, (c, c), 0)
    jj = lax.broadcasted_iota(jnp.int32, (c, c), 1)
    incl = (ii >= jj) if lower else (ii <= jj)
    tri = jnp.where(incl, 1.0, 0.0).astype(BF16)
    g_row = sum(_dot_nt(part, tri) for part in _split3(g))[N_HEADS * d:N_HEADS * (d + 1), :]
    cum = sum(_dot(tri, part) for part in _split3(gbc))
    g_col = cum[:, N_HEADS * d:N_HEADS * (d + 1)]
    b_col = gbc[:, 2 * N_HEADS + N_HEADS * d:2 * N_HEADS + N_HEADS * (d + 1)]
    g_cols4 = jnp.concatenate([jnp.broadcast_to(g_col[:, h:h + 1], (c, c)) for h in range(N_HEADS)], axis=1)
    b_cols4 = jnp.concatenate([jnp.broadcast_to(b_col[:, h:h + 1], (c, c)) for h in range(N_HEADS)], axis=1)
    g_rows4 = jnp.concatenate([g_row[h:h + 1, :] for h in range(N_HEADS)], axis=1)
    incl4 = jnp.concatenate([incl] * N_HEADS, axis=1)
    diag4 = jnp.concatenate([ii == jj] * N_HEADS, axis=1)
    decay = jnp.where(incl4, jnp.exp(jnp.where(incl4, g_cols4 - g_rows4, 0.0)), 0.0)
    kstack = _stack_heads(kn)
    kk = _dot_nt(kn, kstack)
    qk = _dot_nt(qn, kstack)
    n_mat = jnp.where(diag4, 0.0, decay * kk * b_cols4)
    attn = (decay * qk).astype(BF16)
    g256 = _expand_heads(g_col)
    eg256 = jnp.exp(g256)
    b256 = _expand_heads(b_col)
    vb = v.astype(F32) * b256
    kbg = kn.astype(F32) * b256 * eg256
    rhs = jnp.concatenate([_stack_heads(vb), _stack_heads(kbg)], axis=1)
    g_last = g256[c - 1:c, :] if lower else g256[0:1, :]
    kdec = (kn.astype(F32) * jnp.exp(g_last - g256)).astype(BF16)
    n_heads = [n_mat[:, h * c:(h + 1) * c] for h in range(N_HEADS)]
    return n_heads, dict(qn=qn, attn=attn, rhs=rhs, eg=eg256, kdec=kdec, sdec=jnp.exp(g_last))


def _dn_post(z, s_prev, bd):
    s_bf = s_prev.astype(BF16)
    w = z["u"] - _dot(z["wk"], s_bf)
    o = z["eg"] * _dot(z["qn"], s_bf) + _dot(z["attn"], _stack_heads(w))
    s_next = z["sdec"] * s_prev + bd * _dot_tn(z["kdec"], w.astype(BF16))
    return o, s_next


def _dn_scan_kernel(qf_ref, qb_ref, gf_ref, gb_ref, gcf_ref, gcb_ref, bd_ref, of_ref, ob_ref, st_f, st_b, *, cb):
    t = pl.program_id(1)

    @pl.when(t == 0)
    def _():
        st_f[...] = jnp.zeros_like(st_f)
        st_b[...] = jnp.zeros_like(st_b)

    bd = bd_ref[...]
    rows = [slice(i * CHUNK, (i + 1) * CHUNK) for i in range(cb)]
    mats, pres = [], []
    for d, (q_ref, g_ref, gc_ref) in enumerate(((qf_ref, gf_ref, gcf_ref), (qb_ref, gb_ref, gcb_ref))):
        for r in rows:
            n_heads, pre = _dn_pre(q_ref[0, r, :], g_ref[0, :, r], gc_ref[0, r, :], d, d == 0)
            mats += n_heads
            pres.append(pre)
    ii = lax.broadcasted_iota(jnp.int32, (CHUNK, CHUNK), 0)
    jj = lax.broadcasted_iota(jnp.int32, (CHUNK, CHUNK), 1)
    inv = _tri_inverse(mats, ii, jj)
    for n, pre in enumerate(pres):
        a_inv = jnp.concatenate(inv[N_HEADS * n:N_HEADS * (n + 1)], axis=1).astype(BF16)
        uw = _dot(a_inv, pre["rhs"])
        pre["u"] = uw[:, 0:MIX_W]
        pre["wk"] = uw[:, MIX_W:2 * MIX_W].astype(BF16)
    s_f, s_b = st_f[...], st_b[...]
    for k in range(cb):
        o, s_f = _dn_post(pres[k], s_f, bd)
        of_ref[0, rows[k], :] = o
        o, s_b = _dn_post(pres[cb + cb - 1 - k], s_b, bd)
        ob_ref[0, rows[cb - 1 - k], :] = o
    st_f[...] = s_f
    st_b[...] = s_b


def _deltanet(p, ab_t, ab_c, conv_w, neg_a, dtb, bd, ones_bd, ncc):
    nb, nt, _ = p.shape
    nc = nt // CHUNK
    w3 = 3 * MIX_W
    c2 = lambda b, t: (0, 0)
    dn_blk = P_DN // RET_COLS
    pad_lanes = lambda col: jnp.concatenate([col.reshape(1, -1), jnp.zeros((1, 128 - col.size), F32)], axis=1)
    qkv, gbeta, gbeta_c = pl.pallas_call(
        functools.partial(_dn_prep_kernel, ncc=ncc, nc=nc),
        grid=(nb, nc),
        in_specs=[pl.BlockSpec((1, CHUNK, 4 * MIX_W), lambda b, t: (b, t, dn_blk)),
                  pl.BlockSpec((1, CHUNK, 4 * MIX_W), lambda b, t: (b, jnp.maximum(t - 1, 0), dn_blk)),
                  pl.BlockSpec((1, CHUNK, 4 * MIX_W), lambda b, t: (b, jnp.minimum(t + 1, nc - 1), dn_blk)),
                  pl.BlockSpec((1, 16, CHUNK), lambda b, t: (b, 0, t)),
                  pl.BlockSpec((1, CHUNK, 128), lambda b, t: (b, t, 0)),
                  pl.BlockSpec((8, w3), c2),
                  pl.BlockSpec((8, 1), c2),
                  pl.BlockSpec((8, 1), c2),
                  pl.BlockSpec((1, 128), c2),
                  pl.BlockSpec((1, 128), c2),
                  pl.BlockSpec((MIX_W, MIX_W), c2)],
        out_specs=[pl.BlockSpec((1, CHUNK, w3), lambda b, t: (b, t, 0)),
                   pl.BlockSpec((1, 16, CHUNK), lambda b, t: (b, 0, t)),
                   pl.BlockSpec((1, CHUNK, 128), lambda b, t: (b, t, 0))],
        out_shape=[jax.ShapeDtypeStruct((nb, nt, w3), BF16),
                   jax.ShapeDtypeStruct((nb, 16, nt), F32),
                   jax.ShapeDtypeStruct((nb, nt, 128), F32)],
        scratch_shapes=[pltpu.VMEM((CHUNK + 16, w3), F32)],
        compiler_params=_params("arbitrary", "arbitrary"),
        name="dn_prep",
    )(p, p, p, ab_t, ab_c, conv_w, neg_a, dtb, pad_lanes(neg_a), pad_lanes(dtb), ones_bd)
    cb = 2
    assert nc % cb == 0 and ncc % cb == 0
    rows = cb * CHUNK
    cur_b = lambda t: _bwd_chunk(t, ncc // cb, nc // cb)
    o_shape = jax.ShapeDtypeStruct((nb, nt, MIX_W), F32)
    return pl.pallas_call(
        functools.partial(_dn_scan_kernel, cb=cb),
        grid=(nb, nc // cb),
        in_specs=[pl.BlockSpec((1, rows, w3), lambda b, t: (b, t, 0)),
                  pl.BlockSpec((1, rows, w3), lambda b, t: (b, cur_b(t), 0)),
                  pl.BlockSpec((1, 16, rows), lambda b, t: (b, 0, t)),
                  pl.BlockSpec((1, 16, rows), lambda b, t: (b, 0, cur_b(t))),
                  pl.BlockSpec((1, rows, 128), lambda b, t: (b, t, 0)),
                  pl.BlockSpec((1, rows, 128), lambda b, t: (b, cur_b(t), 0)),
                  pl.BlockSpec((MIX_W, MIX_W), c2)],
        out_specs=[pl.BlockSpec((1, rows, MIX_W), lambda b, t: (b, t, 0)),
                   pl.BlockSpec((1, rows, MIX_W), lambda b, t: (b, cur_b(t), 0))],
        out_shape=[o_shape, o_shape],
        scratch_shapes=[pltpu.VMEM((MIX_W, MIX_W), F32), pltpu.VMEM((MIX_W, MIX_W), F32)],
        compiler_params=_params("arbitrary", "arbitrary"),
        name="dn_scan",
    )(qkv, qkv, gbeta, gbeta, gbeta_c, gbeta_c, bd)


QK_W = 256


VT_ROWS = 144


def _mla_prep_kernel(p_ref, c_ref, s_ref, perm_ref, qg_ref, kg_ref, wqn_ref, wqr_ref, wa_ref, selq_ref, selc_ref,
                     selr_ref, selv_ref, one_ref, qt_ref, kv_ref, vt_ref, *, scale):
    p = p_ref[0]
    cos, sin, perm = c_ref[...], s_ref[...], perm_ref[...]
    cq = p[:, 0:Q_LORA].astype(F32)
    cqn = (cq * lax.rsqrt(jnp.mean(cq * cq, axis=-1, keepdims=True) + EPS) * qg_ref[...]).astype(BF16)
    q_nope = _dot(cqn, wqn_ref[...]).astype(BF16)
    q_rope = _dot(cqn, wqr_ref[...]).astype(BF16)
    q_rot = (_rot(q_rope, cos, sin, perm) * scale).astype(BF16)
    q_nope_s = (q_nope.astype(F32) * scale).astype(BF16)
    for h in range(N_HEADS):
        qt_ref[0, h] = (_dot_nt(wa_ref[h], q_nope_s) + _dot_nt(selq_ref[h], q_rot)).astype(BF16)
    ckv = p[:, Q_LORA:Q_LORA + KV_LORA].astype(F32)
    ckvn = (ckv * lax.rsqrt(jnp.mean(ckv * ckv, axis=-1, keepdims=True) + EPS) * kg_ref[...]).astype(BF16)
    kr = p[:, Q_LORA + KV_LORA:MLA_PAD]
    kr_rot = _rot(kr, cos, sin, perm).astype(BF16)
    kv_ref[0] = (_dot(ckvn, selc_ref[...]) + _dot(kr_rot, selr_ref[...])).astype(BF16)
    vt_ref[0] = (_dot_nt(selv_ref[...], ckvn) + one_ref[...]).astype(BF16)


def _mla_attn_kernel(qt_ref, kv_ref, vt_ref, wuv_ref, y_ref, m_ref, acc_ref, s_ref, *, tq, tk, n_ctx, nt):
    i = pl.program_id(1)
    heads = range(N_HEADS)
    m_ref[...] = jnp.full_like(m_ref, -jnp.inf)
    acc_ref[...] = jnp.zeros_like(acc_ref)

    def scores(j0, size, slot):
        k = kv_ref[0, pl.ds(j0, size), :]
        for h in heads:
            s_ref[slot, h, 0:size, :] = _dot(k, qt_ref[0, h])

    def softmax_pv(j0, size, slot):
        vt = vt_ref[0, :, pl.ds(j0, size)]
        s = [s_ref[slot, h, 0:size, :] for h in heads]
        m_old = [m_ref[h] for h in heads]
        m_new = [jnp.maximum(m_old[h], jnp.max(s[h], axis=0, keepdims=True)) for h in heads]
        pr = [jnp.exp2(s[h] - m_new[h]).astype(BF16) for h in heads]
        pv = [_dot(vt, pr[h]) for h in heads]
        for h in heads:
            acc_ref[h] = jnp.exp2(m_old[h] - m_new[h]) * acc_ref[h] + pv[h]
            m_ref[h] = m_new[h]

    scores(0, n_ctx, 0)
    is_latent = (i + 1) * tq > n_ctx

    @pl.when(jnp.logical_not(is_latent))
    def _():
        softmax_pv(0, n_ctx, 0)

    @pl.when(is_latent)
    def _():
        n_tiles = (nt - n_ctx) // tk
        last = n_ctx + (n_tiles - 1) * tk
        scores(n_ctx, tk, 1)
        softmax_pv(0, n_ctx, 0)

        def body(jj, carry):
            t0 = pl.multiple_of(n_ctx + 2 * jj * tk, 256)
            t1 = pl.multiple_of(jnp.minimum(t0 + tk, last), 256)
            t2 = pl.multiple_of(jnp.minimum(t0 + 2 * tk, last), 256)
            scores(t1, tk, 0)
            softmax_pv(t0, tk, 1)
            scores(t2, tk, 1)
            softmax_pv(t1, tk, 0)
            return carry

        lax.fori_loop(0, n_tiles // 2, body, 0)
        if n_tiles % 2:
            softmax_pv(last, tk, 1)

    y = None
    for h in range(N_HEADS):
        acc = acc_ref[h]
        o = (acc[0:KV_LORA, :] / acc[KV_LORA:KV_LORA + 1, :]).astype(BF16)
        term = _dot_tn(o, wuv_ref[h])
        y = term if y is None else y + term
    y_ref[0] = y.astype(BF16)


def _mla(p, cos, sin, perm, qg, kg, wqn, wqr, wa, selq, selc, selr, selv, one_col, wuv, n_ctx):
    nb, nt, _ = p.shape
    tm = _pick(nt, (768, 384, 256, 128))
    scale = (NOPE_DIM + ROPE_DIM) ** -0.5 * math.log2(math.e)
    c2 = lambda b, i: (0, 0)
    c3 = lambda b, i: (0, 0, 0)
    qt, kv, vt = pl.pallas_call(
        functools.partial(_mla_prep_kernel, scale=scale),
        grid=(nb, nt // tm),
        in_specs=[pl.BlockSpec((1, tm, MLA_PAD), lambda b, i: (b, i, P_MLA // MLA_PAD)),
                  pl.BlockSpec((tm, 128), lambda b, i: (i, 0)),
                  pl.BlockSpec((tm, 128), lambda b, i: (i, 0)),
                  pl.BlockSpec((128, 128), c2),
                  pl.BlockSpec((1, Q_LORA), c2),
                  pl.BlockSpec((1, KV_LORA), c2),
                  pl.BlockSpec((Q_LORA, N_HEADS * NOPE_DIM), c2),
                  pl.BlockSpec((Q_LORA, N_HEADS * ROPE_DIM), c2),
                  pl.BlockSpec((N_HEADS, QK_W, N_HEADS * NOPE_DIM), c3),
                  pl.BlockSpec((N_HEADS, QK_W, N_HEADS * ROPE_DIM), c3),
                  pl.BlockSpec((KV_LORA, QK_W), c2),
                  pl.BlockSpec((128, QK_W), c2),
                  pl.BlockSpec((VT_ROWS, KV_LORA), c2),
                  pl.BlockSpec((VT_ROWS, 1), c2)],
        out_specs=[pl.BlockSpec((1, N_HEADS, QK_W, tm), lambda b, i: (b, 0, 0, i)),
                   pl.BlockSpec((1, tm, QK_W), lambda b, i: (b, i, 0)),
                   pl.BlockSpec((1, VT_ROWS, tm), lambda b, i: (b, 0, i))],
        out_shape=[jax.ShapeDtypeStruct((nb, N_HEADS, QK_W, nt), BF16),
                   jax.ShapeDtypeStruct((nb, nt, QK_W), BF16),
                   jax.ShapeDtypeStruct((nb, VT_ROWS, nt), BF16)],
        compiler_params=_params("arbitrary", "arbitrary"),
        name="mla_prep",
    )(p, cos, sin, perm, qg, kg, wqn, wqr, wa, selq, selc, selr, selv, one_col)
    tq = 256
    tk = _pick(nt - n_ctx, (512, 256))
    return pl.pallas_call(
        functools.partial(_mla_attn_kernel, tq=tq, tk=tk, n_ctx=n_ctx, nt=nt),
        grid=(nb, nt // tq),
        in_specs=[pl.BlockSpec((1, N_HEADS, QK_W, tq), lambda b, i: (b, 0, 0, i)),
                  pl.BlockSpec((1, nt, QK_W), lambda b, i: (b, 0, 0)),
                  pl.BlockSpec((1, VT_ROWS, nt), lambda b, i: (b, 0, 0)),
                  pl.BlockSpec((N_HEADS, KV_LORA, MIX_W), c3)],
        out_specs=pl.BlockSpec((1, tq, MIX_W), lambda b, i: (b, i, 0)),
        out_shape=jax.ShapeDtypeStruct((nb, nt, MIX_W), BF16),
        scratch_shapes=[pltpu.VMEM((N_HEADS, 1, tq), F32), pltpu.VMEM((N_HEADS, VT_ROWS, tq), F32),
                        pltpu.VMEM((2, N_HEADS, max(tk, n_ctx), tq), F32)],
        compiler_params=_params("arbitrary", "arbitrary"),
        name="mla_attn",
    )(qt, kv, vt, wuv)


def _merge_kernel(x_ref, yr_ref, ys_ref, of_ref, ob_ref, ym_ref, z_ref, g0_ref, g1_ref, g2_ref, g3_ref,
                  wb_ref, wo_ref, ng_ref, gp_ref, ml_ref, mc_ref, ones_ref, o_ref, *, tm, n_ctx):
    i = pl.program_id(1)
    od = of_ref[0] + ob_ref[0]
    ms = _head_sum(od * od, ones_ref[...]) * (1.0 / HEAD_DIM)
    z = z_ref[0].astype(F32)
    ydn = (od * lax.rsqrt(ms + EPS) * ng_ref[...]) * (z * jax.nn.sigmoid(z))
    ys = (yr_ref[0], ys_ref[0], ydn.astype(BF16), ym_ref[0])
    gates = (g0_ref, g1_ref, g2_ref, g3_ref)
    acc = None
    for b in range(N_BRANCH):
        term = jax.nn.sigmoid(gates[b][0].astype(F32)) * _dot(ys[b], wb_ref[b])
        acc = term if acc is None else acc + term
    y = _dot(acc.astype(BF16), wo_ref[...])
    r = y * lax.rsqrt(jnp.mean(y * y, axis=-1, keepdims=True) + EPS) * gp_ref[...]
    rows = lax.broadcasted_iota(jnp.int32, (tm, 1), 0) + i * tm
    gate = jnp.where(rows < n_ctx, mc_ref[0, 2:3, :], ml_ref[0, 2:3, :])
    o_ref[0] = x_ref[0] + gate * r


def _merge(xa, y_ret, y_sg, o_f, o_b, y_mla, p, wb, wo, ng, gp, mod, ones_bd, n_ctx):
    nb, nt, d = xa.shape
    tm = _pick(nt, (768, 384, 256, 128))
    row = lambda b, i: (b, i, 0)
    c2 = lambda b, i: (0, 0)
    y_spec = pl.BlockSpec((1, tm, MIX_W), row)
    gate_specs = [pl.BlockSpec((1, tm, d), functools.partial(lambda b, i, k: (b, i, k), k=P_GATE // d + k))
                  for k in range(N_BRANCH)]
    return pl.pallas_call(
        functools.partial(_merge_kernel, tm=tm, n_ctx=n_ctx),
        grid=(nb, nt // tm),
        in_specs=[pl.BlockSpec((1, tm, d), row), y_spec, y_spec, y_spec, y_spec, y_spec,
                  pl.BlockSpec((1, tm, MIX_W), lambda b, i: (b, i, (P_DN + 3 * MIX_W) // MIX_W)),
                  *gate_specs,
                  pl.BlockSpec((N_BRANCH, MIX_W, d), lambda b, i: (0, 0, 0)),
                  pl.BlockSpec((d, d), c2),
                  pl.BlockSpec((1, MIX_W), c2),
                  pl.BlockSpec((1, d), c2),
                  pl.BlockSpec((1, 6, d), lambda b, i: (b, 0, 0)),
                  pl.BlockSpec((1, 6, d), lambda b, i: (nb, 0, 0)),
                  pl.BlockSpec((MIX_W, MIX_W), c2)],
        out_specs=pl.BlockSpec((1, tm, d), row),
        out_shape=jax.ShapeDtypeStruct((nb, nt, d), F32),
        compiler_params=_params("arbitrary", "arbitrary"),
        name="merge",
    )(xa, y_ret, y_sg, o_f, o_b, y_mla, p, p, p, p, p, wb, wo, ng, gp, mod, mod, ones_bd)


def _route(sel, aff):
    rows = [sel[e:e + 1, :] for e in range(N_EXPERTS)]
    pairs = [(a, b) for a in range(EXPERTS_PER_GROUP) for b in range(a + 1, EXPERTS_PER_GROUP)]
    grp_score, grp_pair = [], []
    for g in range(N_GROUPS):
        base = g * EXPERTS_PER_GROUP
        best = rows[base + pairs[0][0]] + rows[base + pairs[0][1]]
        best_p = jnp.zeros_like(best, dtype=jnp.int32)
        for pi in range(1, len(pairs)):
            s = rows[base + pairs[pi][0]] + rows[base + pairs[pi][1]]
            take = s > best
            best = jnp.where(take, s, best)
            best_p = jnp.where(take, pi, best_p)
        grp_score.append(best)
        grp_pair.append(best_p)
    top = grp_score[0]
    top_g = jnp.zeros_like(grp_pair[0])
    top_p = grp_pair[0]
    for g in range(1, N_GROUPS):
        take = grp_score[g] > top
        top = jnp.where(take, grp_score[g], top)
        top_g = jnp.where(take, g, top_g)
        top_p = jnp.where(take, grp_pair[g], top_p)
    picked = []
    for e in range(N_EXPERTS):
        g, k = divmod(e, EXPERTS_PER_GROUP)
        in_pair = None
        for pi, (a, b) in enumerate(pairs):
            if k in (a, b):
                hit = top_p == pi
                in_pair = hit if in_pair is None else (in_pair | hit)
        picked.append(jnp.where((top_g == g) & in_pair, aff[e:e + 1, :], 0.0))
    denom = picked[0]
    for e in range(1, N_EXPERTS):
        denom = denom + picked[e]
    return [pk / denom for pk in picked]


def _swiglu(hn, w1, w3, w2, scale):
    a = _dot(hn, w1.astype(BF16))
    h = (a * jax.nn.sigmoid(a)) * _dot(hn, w3.astype(BF16))
    if scale is not None:
        h = h * scale
    return _dot(h.astype(BF16), w2.astype(BF16))


def _moe_kernel(x_ref, ml_ref, mc_ref, g2_ref, gp_ref, rw_ref, rb_ref, ws1_ref, ws3_ref, ws2_ref,
                w1_ref, w3_ref, w2_ref, o_ref, hn_ref, comb_t_ref, comb_ref, acc_ref, *, tm, rb, n_ctx):
    i = pl.program_id(1)
    e = pl.program_id(2)

    @pl.when(e == 0)
    def _():
        def blk(r, carry):
            r0 = pl.multiple_of(r * rb, rb)
            x = x_ref[0, pl.ds(r0, rb), :]
            hn = _norm_modulate(x, g2_ref[...], i * tm + r0 < n_ctx, mc_ref, ml_ref, 3, 4)
            hn_ref[pl.ds(r0, rb), :] = hn.astype(BF16)
            return carry

        lax.fori_loop(0, tm // rb, blk, 0)
        hn = hn_ref[...]
        aff = jax.nn.sigmoid(_dot_nt(rw_ref[...], hn))
        comb = _route(aff + rb_ref[...], aff)
        comb_t_ref[...] = jnp.zeros_like(comb_t_ref)
        for k in range(N_EXPERTS):
            comb_t_ref[k:k + 1, :] = comb[k]
        comb_ref[...] = comb_t_ref[...].T
        acc_ref[...] = _swiglu(hn, ws1_ref[0], ws3_ref[0], ws2_ref[0], None)

    @pl.when(e > 0)
    def _():
        lane = lax.broadcasted_iota(jnp.int32, (1, 128), 1)
        comb = comb_ref[...]
        hn = hn_ref[...]
        first = 2 * (e - 1)
        y = None
        for k in range(2):
            c_k = jnp.sum(jnp.where(lane == first + k, comb, 0.0), axis=-1, keepdims=True)
            term = _swiglu(hn, w1_ref[0, k], w3_ref[0, k], w2_ref[0, k], c_k)
            y = term if y is None else y + term
        acc_ref[...] += y

    @pl.when(e == pl.num_programs(2) - 1)
    def _():
        y = acc_ref[...]
        r = y * lax.rsqrt(jnp.mean(y * y, axis=-1, keepdims=True) + EPS) * gp_ref[...]
        rows = lax.broadcasted_iota(jnp.int32, (tm, 1), 0) + i * tm
        gate = jnp.where(rows < n_ctx, mc_ref[0, 5:6, :], ml_ref[0, 5:6, :])
        o_ref[0] = x_ref[0] + gate * r


def _moe(xa, mod, g2, gp, rw_t, rbias, ws1, ws3, ws2, w1, w3, w2, layer, n_ctx):
    nb, nt, d = xa.shape
    tm = _pick(nt, (768, 384, 256, 128))
    n_pairs = w1.shape[1] // 2
    row = lambda b, i, e: (b, i, 0)
    c2 = lambda b, i, e: (0, 0)
    shared_blk = lambda b, i, e: (layer, 0, 0)
    pair_blk = lambda b, i, e: (layer, jnp.maximum(e - 1, 0), 0, 0)
    return pl.pallas_call(
        functools.partial(_moe_kernel, tm=tm, rb=128, n_ctx=n_ctx),
        grid=(nb, nt // tm, n_pairs + 1),
        in_specs=[pl.BlockSpec((1, tm, d), row),
                  pl.BlockSpec((1, 6, d), lambda b, i, e: (b, 0, 0)),
                  pl.BlockSpec((1, 6, d), lambda b, i, e: (nb, 0, 0)),
                  pl.BlockSpec((1, d), c2),
                  pl.BlockSpec((1, d), c2),
                  pl.BlockSpec((N_EXPERTS, d), c2),
                  pl.BlockSpec((N_EXPERTS, 1), c2),
                  pl.BlockSpec((1, d, D_EXPERT), shared_blk),
                  pl.BlockSpec((1, d, D_EXPERT), shared_blk),
                  pl.BlockSpec((1, D_EXPERT, d), shared_blk),
                  pl.BlockSpec((1, 2, d, D_EXPERT), pair_blk),
                  pl.BlockSpec((1, 2, d, D_EXPERT), pair_blk),
                  pl.BlockSpec((1, 2, D_EXPERT, d), pair_blk)],
        out_specs=pl.BlockSpec((1, tm, d), row),
        out_shape=jax.ShapeDtypeStruct((nb, nt, d), F32),
        scratch_shapes=[pltpu.VMEM((tm, d), BF16), pltpu.VMEM((128, tm), F32), pltpu.VMEM((tm, 128), F32),
                        pltpu.VMEM((tm, d), F32)],
        compiler_params=_params("arbitrary", "arbitrary", "arbitrary"),
        name="moe",
    )(xa, mod, mod, g2, gp, rw_t, rbias, ws1, ws3, ws2, w1, w3, w2)


def _swap_perm(width, group):
    j = np.arange(width)
    src = np.where((j % group) < group // 2, j + group // 2, j - group // 2)
    return jnp.asarray(np.arange(width)[:, None] == src[None, :], BF16)


def _rope_tables(n_lat, n_ctx):
    def angles(pos, dim):
        half = dim // 2
        inv = ROPE_BASE ** (-jnp.arange(half, dtype=F32) / half)
        return pos.astype(F32)[:, None] * inv[None, :]

    def tables(cos_parts, sin_parts, reps):
        cos = jnp.tile(jnp.concatenate(cos_parts, axis=-1), (1, reps))
        sin = jnp.tile(jnp.concatenate(sin_parts, axis=-1), (1, reps))
        w = cos.shape[1]
        return (jnp.concatenate([jnp.ones((n_ctx, w), F32), cos], axis=0),
                jnp.concatenate([jnp.zeros((n_ctx, w), F32), sin], axis=0))

    rows = n_lat // GRID_W
    ang_t = angles(jnp.arange(n_lat), HEAD_DIM)
    ang_r = angles(jnp.repeat(jnp.arange(rows), GRID_W), ROPE_DIM // 2)
    ang_c = angles(jnp.tile(jnp.arange(GRID_W), rows), ROPE_DIM // 2)
    ct, st = jnp.cos(ang_t), jnp.sin(ang_t)
    ret = tables([ct, ct], [-st, st], N_HEADS)
    cr, sr, cc, sc = jnp.cos(ang_r), jnp.sin(ang_r), jnp.cos(ang_c), jnp.sin(ang_c)
    mla = tables([cr, cr, cc, cc], [-sr, sr, -sc, sc], N_HEADS)
    return ret, mla


def _ret_tables(logit):
    log_g = jax.nn.log_sigmoid(logit.astype(F32))
    lane_lg = jnp.repeat(log_g, HEAD_DIM, axis=1)
    idx = jnp.arange(CHUNK, dtype=F32)[:, None]
    kd = jnp.stack([jnp.exp(lane_lg[0][None, :] * (CHUNK - 1 - idx)), jnp.exp(lane_lg[1][None, :] * idx)])
    qd = jnp.stack([jnp.exp(lane_lg[0][None, :] * (idx + 1)), jnp.exp(lane_lg[1][None, :] * (CHUNK - idx))])
    cd = jnp.exp(lane_lg * CHUNK)[:, None, :]
    diff = idx - idx.T
    blocks = []
    for h in range(N_HEADS):
        f = jnp.exp(log_g[0, h] * jnp.where(diff >= 0, diff, 0.0))
        b = jnp.exp(log_g[1, h] * jnp.where(diff < 0, -diff, 0.0))
        blocks.append(jnp.where(diff >= 0, f, b))
    dm = jnp.concatenate(blocks, axis=1)
    return kd, cd, qd, dm


def _pack_w_in(w_in):
    d = w_in.shape[0]
    mla = jnp.concatenate([w_in[:, OFF_MLA:OFF_MLA + MLA_COLS], jnp.zeros((d, MLA_PAD - MLA_COLS), w_in.dtype)], 1)
    w = jnp.concatenate([w_in[:, OFF_RET:OFF_RET + RET_COLS], w_in[:, OFF_DN:OFF_DN + 4 * MIX_W],
                         w_in[:, OFF_SG:OFF_SG + SG_COLS], mla, w_in[:, OFF_GATE:OFF_GATE + GATE_COLS]], axis=1)
    wab = w_in[:, OFF_DN + 4 * MIX_W:OFF_DN + DN_COLS]
    wabc = jnp.concatenate([wab, jnp.zeros((d, 128 - 4 * N_HEADS), w_in.dtype)], axis=1)
    return w.astype(BF16), wab.T.astype(BF16), wabc.astype(BF16)


def _mla_weights(w_uq, w_ukv):
    dq = NOPE_DIM + ROPE_DIM
    dkv = NOPE_DIM + V_DIM
    wq = w_uq.reshape(Q_LORA, N_HEADS, dq)
    wqn = wq[:, :, :NOPE_DIM].reshape(Q_LORA, N_HEADS * NOPE_DIM)
    wqr = wq[:, :, NOPE_DIM:].reshape(Q_LORA, N_HEADS * ROPE_DIM)
    wkv = w_ukv.reshape(KV_LORA, N_HEADS, dkv)
    head_eye = jnp.eye(N_HEADS, dtype=F32)
    wa = jnp.einsum("chd,hg->hcgd", wkv[:, :, :NOPE_DIM], head_eye).reshape(N_HEADS, KV_LORA, N_HEADS * NOPE_DIM)
    wa = jnp.pad(wa, ((0, 0), (0, QK_W - KV_LORA), (0, 0)))
    wuv = jnp.einsum("chd,hg->hcgd", wkv[:, :, NOPE_DIM:], head_eye).reshape(N_HEADS, KV_LORA, MIX_W)
    selq = np.zeros((N_HEADS, QK_W, N_HEADS * ROPE_DIM), np.float32)
    for h in range(N_HEADS):
        selq[h, KV_LORA:KV_LORA + ROPE_DIM, h * ROPE_DIM:(h + 1) * ROPE_DIM] = np.eye(ROPE_DIM)
    selc = np.zeros((KV_LORA, QK_W), np.float32)
    selc[:, 0:KV_LORA] = np.eye(KV_LORA)
    selr = np.zeros((128, QK_W), np.float32)
    selr[0:ROPE_DIM, KV_LORA:KV_LORA + ROPE_DIM] = np.eye(ROPE_DIM)
    selv = np.zeros((VT_ROWS, KV_LORA), np.float32)
    selv[0:KV_LORA, :] = np.eye(KV_LORA)
    one_col = np.zeros((VT_ROWS, 1), np.float32)
    one_col[KV_LORA, 0] = 1.0
    return (tuple(jnp.asarray(a, BF16) for a in (wqn, wqr, wa, selq, selc, selr, selv))
            + (jnp.asarray(one_col), wuv.astype(BF16)))


def kernel(x, c, ctx, c_ctx, w_ada, b_ada, g_pre1, g_post1, g_pre2, g_post2, w_in, ret_decay_logit, sg_norm_g, sg_w, sg_b, dn_conv_w, dn_A_log, dn_dt_bias, dn_norm_g, mla_q_norm_g, mla_kv_norm_g, mla_w_uq, mla_w_ukv, w_branch, w_out, router_w, router_bias, moe_w1, moe_w3, moe_w2, shared_w1, shared_w3, shared_w2):
    nb, n_lat, d = x.shape
    n_ctx = ctx.shape[1]
    depth = w_in.shape[0]
    assert d == D_MODEL and n_lat % GRID_W == 0 and n_lat % CHUNK == 0 and n_ctx % 256 == 0
    ncc = n_ctx // CHUNK

    n_cond = -(-(nb + 1) // 8) * 8
    cond = jnp.concatenate([c, c_ctx[None], jnp.zeros((n_cond - nb - 1, d), F32)], axis=0)
    mod_all = _adaln(cond, w_ada, b_ada).reshape(depth, n_cond, 6, d)

    (ret_cos, ret_sin), (mla_cos, mla_sin) = _rope_tables(n_lat, n_ctx)
    perm_ret = _swap_perm(MIX_W, HEAD_DIM)
    perm_mla = _swap_perm(N_HEADS * ROPE_DIM, ROPE_DIM // 2)
    lane_head = jnp.arange(MIX_W) // HEAD_DIM
    bd = (lane_head[:, None] == lane_head[None, :]).astype(F32)
    ones_bd = bd.astype(BF16)
    rw_t = router_w.T.astype(BF16)
    rbias = router_bias.astype(F32)[:, None]

    xa = jnp.concatenate([ctx, x], axis=1)
    for l in range(depth):
        mod = mod_all[l]
        w_l, wab_l, wabc_l = _pack_w_in(w_in[l])
        p, ab_t, ab_c = _inproj(xa, mod, g_pre1[l][None], w_l, wab_l, wabc_l, n_ctx)

        kd, cd, qd, dm = _ret_tables(ret_decay_logit[l])
        wcat = jnp.concatenate([sg_w[l, h] for h in range(N_HEADS)], axis=1).astype(BF16)
        sg_bias = jnp.repeat(sg_b[l].T, HEAD_DIM, axis=1)
        y_ret, y_sg = _retention_and_sgate(p, ret_cos, ret_sin, perm_ret, (kd, cd, qd, dm, bd, ones_bd),
                                           sg_norm_g[l][None], wcat, sg_bias, ncc)

        neg_a = (-jnp.exp(dn_A_log[l].astype(F32))).reshape(2 * N_HEADS, 1)
        dtb = dn_dt_bias[l].astype(F32).reshape(2 * N_HEADS, 1)
        conv_w = jnp.concatenate([dn_conv_w[l], jnp.zeros((8 - CONV_W, 3 * MIX_W), F32)], axis=0)
        o_f, o_b = _deltanet(p, ab_t, ab_c, conv_w, neg_a, dtb, bd, ones_bd, ncc)

        y_mla = _mla(p, mla_cos, mla_sin, perm_mla, mla_q_norm_g[l][None], mla_kv_norm_g[l][None],
                     *_mla_weights(mla_w_uq[l], mla_w_ukv[l]), n_ctx)

        xa = _merge(xa, y_ret, y_sg, o_f, o_b, y_mla, p, w_branch[l].astype(BF16), w_out[l].astype(BF16),
                    jnp.tile(dn_norm_g[l], N_HEADS)[None], g_post1[l][None], mod, ones_bd, n_ctx)

        xa = _moe(xa, mod, g_pre2[l][None], g_post2[l][None], rw_t, rbias, shared_w1, shared_w3, shared_w2,
                  moe_w1, moe_w3, moe_w2, l, n_ctx)
    return xa[:, n_ctx:]
```

```python
import functools
import math

import jax
import jax.numpy as jnp
import numpy as np
from jax import lax
from jax.experimental import pallas as pl
from jax.experimental.pallas import tpu as pltpu

F32 = jnp.float32
BF16 = jnp.bfloat16
HIGHEST = lax.Precision.HIGHEST

D_MODEL = 1024
GRID_W = 64
N_HEADS = 4
HEAD_DIM = 64
MIX_W = N_HEADS * HEAD_DIM
CHUNK = 128
ROPE_BASE = 10000.0
EPS = 1e-6
RET_DECAY_EXP0 = 5.0
CONV_W = 5
Q_LORA = 256
KV_LORA = 128
NOPE_DIM = 64
ROPE_DIM = 32
V_DIM = 64
N_EXPERTS = 16
N_GROUPS = 4
EXPERTS_PER_GROUP = N_EXPERTS // N_GROUPS
D_EXPERT = 256
N_BRANCH = 4

RET_COLS = 4 * MIX_W
SG_COLS = 2 * MIX_W
DN_COLS = 4 * MIX_W + 4 * N_HEADS
MLA_COLS = Q_LORA + KV_LORA + ROPE_DIM
GATE_COLS = N_BRANCH * D_MODEL
OFF_RET = 0
OFF_SG = OFF_RET + RET_COLS
OFF_DN = OFF_SG + SG_COLS
OFF_MLA = OFF_DN + DN_COLS
OFF_GATE = OFF_MLA + MLA_COLS

P_RET = 0
P_DN = 1024
P_SG = 2048
P_MLA = 2560
P_GATE = 3072
P_COLS = 7168
MLA_PAD = 512

VMEM_LIMIT = 56 * 1024 * 1024


def _dot(a, b, precision=None):
    return jnp.dot(a, b, preferred_element_type=F32, precision=precision)


def _dot_nt(a, b, precision=None):
    return lax.dot_general(a, b, (((1,), (1,)), ((), ())), preferred_element_type=F32, precision=precision)


def _dot_tn(a, b):
    return lax.dot_general(a, b, (((0,), (0,)), ((), ())), preferred_element_type=F32)


def _mm(a, b):
    return _dot(a.astype(BF16), b.astype(BF16))


def _params(*sem):
    return pltpu.CompilerParams(dimension_semantics=sem, vmem_limit_bytes=VMEM_LIMIT)


def _pick(n, cands):
    for c in cands:
        if n % c == 0:
            return c
    raise ValueError(f"no tile for {n}")


def _head_of_lane(width, group):
    return lax.broadcasted_iota(jnp.int32, (1, width), 1) // group


def _stack_heads(x):
    head = _head_of_lane(MIX_W, HEAD_DIM)
    xf = x.astype(F32)
    return jnp.concatenate([jnp.where(head == h, xf, 0.0).astype(BF16) for h in range(N_HEADS)], axis=0)


def _expand_heads(cols):
    head = _head_of_lane(MIX_W, HEAD_DIM)
    out = cols[:, N_HEADS - 1:N_HEADS]
    for h in range(N_HEADS - 2, -1, -1):
        out = jnp.where(head <= h, cols[:, h:h + 1], out)
    return out


def _head_sum(x, ones_bd):
    hi = x.astype(BF16)
    lo = (x - hi.astype(F32)).astype(BF16)
    return _dot(hi, ones_bd) + _dot(lo, ones_bd)


def _rot(x_bf, cos, sin, perm):
    return x_bf.astype(F32) * cos + _dot(x_bf, perm) * sin


def _norm_modulate(x, g, is_ctx, mc_ref, ml_ref, shift_row, scale_row):
    shift = jnp.where(is_ctx, mc_ref[0, shift_row:shift_row + 1, :], ml_ref[0, shift_row:shift_row + 1, :])
    scale = jnp.where(is_ctx, mc_ref[0, scale_row:scale_row + 1, :], ml_ref[0, scale_row:scale_row + 1, :])
    gain = g * (1.0 + scale)
    return x * lax.rsqrt(jnp.mean(x * x, axis=-1, keepdims=True) + EPS) * gain + shift


def _adaln_kernel(c_ref, w_ref, b_ref, o_ref):
    c = c_ref[...]
    s = c * jax.nn.sigmoid(c)
    o_ref[0] = _dot(s, w_ref[0], precision=HIGHEST) + b_ref[0]


def _adaln(cond, w_ada, b_ada):
    n_l, d, d6 = w_ada.shape
    r = cond.shape[0]
    tn = 1024
    return pl.pallas_call(
        _adaln_kernel,
        grid=(n_l, d6 // tn),
        in_specs=[pl.BlockSpec((r, d), lambda l, j: (0, 0)),
                  pl.BlockSpec((1, d, tn), lambda l, j: (l, 0, j)),
                  pl.BlockSpec((1, 1, tn), lambda l, j: (l, 0, j))],
        out_specs=pl.BlockSpec((1, r, tn), lambda l, j: (l, 0, j)),
        out_shape=jax.ShapeDtypeStruct((n_l, r, d6), F32),
        compiler_params=_params("arbitrary", "arbitrary"),
        name="adaln",
    )(cond, w_ada, b_ada.reshape(n_l, 1, d6))


def _inproj_kernel(x_ref, ml_ref, mc_ref, g_ref, w_ref, wab_ref, wabc_ref, p_ref, ab_ref, abc_ref, xn_ref,
                   *, tm, rb, n_ctx):
    i = pl.program_id(1)
    j = pl.program_id(2)

    @pl.when(j == 0)
    def _():
        def blk(r, carry):
            r0 = pl.multiple_of(r * rb, rb)
            x = x_ref[0, pl.ds(r0, rb), :]
            hn = _norm_modulate(x, g_ref[...], i * tm + r0 < n_ctx, mc_ref, ml_ref, 0, 1)
            xn_ref[pl.ds(r0, rb), :] = hn.astype(BF16)
            return carry

        lax.fori_loop(0, tm // rb, blk, 0)
        ab_ref[0] = _dot_nt(wab_ref[...], xn_ref[...])
        abc_ref[0] = _dot(xn_ref[...], wabc_ref[...])

    p_ref[0] = _dot(xn_ref[...], w_ref[...]).astype(BF16)


def _inproj(xa, mod, g, w, wab, wabc, n_ctx):
    nb, nt, d = xa.shape
    tm = _pick(nt, (1408, 768, 384, 256, 128))
    tn = 1792
    kern = functools.partial(_inproj_kernel, tm=tm, rb=128, n_ctx=n_ctx)
    return pl.pallas_call(
        kern,
        grid=(nb, nt // tm, P_COLS // tn),
        in_specs=[pl.BlockSpec((1, tm, d), lambda b, i, j: (b, i, 0)),
                  pl.BlockSpec((1, 6, d), lambda b, i, j: (b, 0, 0)),
                  pl.BlockSpec((1, 6, d), lambda b, i, j: (nb, 0, 0)),
                  pl.BlockSpec((1, d), lambda b, i, j: (0, 0)),
                  pl.BlockSpec((d, tn), lambda b, i, j: (0, j)),
                  pl.BlockSpec((16, d), lambda b, i, j: (0, 0)),
                  pl.BlockSpec((d, 128), lambda b, i, j: (0, 0))],
        out_specs=[pl.BlockSpec((1, tm, tn), lambda b, i, j: (b, i, j)),
                   pl.BlockSpec((1, 16, tm), lambda b, i, j: (b, 0, i)),
                   pl.BlockSpec((1, tm, 128), lambda b, i, j: (b, i, 0))],
        out_shape=[jax.ShapeDtypeStruct((nb, nt, P_COLS), BF16),
                   jax.ShapeDtypeStruct((nb, 16, nt), F32),
                   jax.ShapeDtypeStruct((nb, nt, 128), F32)],
        scratch_shapes=[pltpu.VMEM((tm, d), BF16)],
        compiler_params=_params("arbitrary", "arbitrary", "arbitrary"),
        name="inproj",
    )(xa, mod, mod, g, w, wab, wabc)


def _bwd_chunk(t, ncc, nc):
    return jnp.where(t < ncc, ncc - 1 - t, nc - 1 - (t - ncc))


def _ret_state_kernel(pf_ref, pb_ref, cf_ref, sf_ref, cb_ref, sb_ref, perm_ref, kd_ref, cd_ref, bd_ref,
                      of_ref, ob_ref, st_f, st_b, *, cb):
    t = pl.program_id(1)

    @pl.when(t == 0)
    def _():
        st_f[...] = jnp.zeros_like(st_f)
        st_b[...] = jnp.zeros_like(st_b)

    def increments(p_ref, c_ref, s_ref, d):
        out = []
        for i in range(cb):
            r = slice(i * CHUNK, (i + 1) * CHUNK)
            kr = _rot(p_ref[0, r, MIX_W:2 * MIX_W], c_ref[r, :], s_ref[r, :], perm_ref[...]) * (HEAD_DIM ** -0.5)
            out.append(bd_ref[...] * _dot_tn((kr * kd_ref[d]).astype(BF16), p_ref[0, r, 2 * MIX_W:3 * MIX_W]))
        return out

    inc_f = increments(pf_ref, cf_ref, sf_ref, 0)
    inc_b = increments(pb_ref, cb_ref, sb_ref, 1)
    s = st_f[...]
    for i in range(cb):
        of_ref[0, i] = s.astype(BF16)
        s = cd_ref[0] * s + inc_f[i]
    st_f[...] = s
    s = st_b[...]
    for i in reversed(range(cb)):
        ob_ref[0, i] = s.astype(BF16)
        s = cd_ref[1] * s + inc_b[i]
    st_b[...] = s


def _gelu_tanh(x):
    return 0.5 * x * (1.0 + jnp.tanh(math.sqrt(2.0 / math.pi) * (x + 0.044715 * (x * x * x))))


def _mix_out_kernel(p_ref, pg_ref, c_ref, s_ref, sf_ref, sb_ref, perm_ref, dm_ref, qd_ref, ones_ref,
                    ng_ref, wg_ref, bg_ref, y_ref, ysg_ref, *, cb):
    chunks = range(cb)
    rows = [slice(i * CHUNK, (i + 1) * CHUNK) for i in chunks]
    perm, dm, ones_bd = perm_ref[...], dm_ref[...], ones_ref[...]
    p = [p_ref[0, r, :] for r in rows]
    cos = [c_ref[r, :] for r in rows]
    sin = [s_ref[r, :] for r in rows]
    qr = [_rot(p[i][:, 0:MIX_W], cos[i], sin[i], perm) for i in chunks]
    kr = [_rot(p[i][:, MIX_W:2 * MIX_W], cos[i], sin[i], perm) * (HEAD_DIM ** -0.5) for i in chunks]
    z = [_gelu_tanh(pg_ref[0, r, :].astype(F32)) for r in rows]
    vg = [x[:, MIX_W:] for x in z]
    mu_g = [jnp.mean(x, axis=-1, keepdims=True) for x in vg]
    vgc = [x - m for x, m in zip(vg, mu_g)]
    var_g = [jnp.mean(x * x, axis=-1, keepdims=True) for x in vgc]
    vn = [x * lax.rsqrt(s + EPS) * ng_ref[...] for x, s in zip(vgc, var_g)]
    sc = [_dot_nt(qr[i].astype(BF16), _stack_heads(kr[i])) * dm for i in chunks]
    mixed = [_dot(wg_ref[...], _stack_heads(x)) for x in vn]
    o = [_dot(sc[i].astype(BF16), _stack_heads(p[i][:, 2 * MIX_W:3 * MIX_W])) for i in chunks]
    qs = [jnp.concatenate([(qr[i] * qd_ref[0]).astype(BF16), (qr[i] * qd_ref[1]).astype(BF16)], axis=1)
          for i in chunks]
    ss = [jnp.concatenate([sf_ref[0, i], sb_ref[0, i]], axis=0) for i in chunks]
    o = [o[i] + _dot(qs[i], ss[i]) for i in chunks]
    for i in chunks:
        ysg_ref[0, rows[i], :] = (z[i][:, :MIX_W] * (mixed[i] + bg_ref[...])).astype(BF16)
    mu = [_head_sum(x, ones_bd) * (1.0 / HEAD_DIM) for x in o]
    oc = [x - m for x, m in zip(o, mu)]
    var = [_head_sum(x * x, ones_bd) * (1.0 / HEAD_DIM) for x in oc]
    for i in chunks:
        g = p[i][:, 3 * MIX_W:4 * MIX_W].astype(F32)
        y_ref[0, rows[i], :] = (oc[i] * lax.rsqrt(var[i] + EPS) * (g * jax.nn.sigmoid(g))).astype(BF16)


def _retention_and_sgate(p, cos, sin, perm, tabs, sg_ng, sg_w, sg_bias, ncc):
    nb, nt, _ = p.shape
    nc = nt // CHUNK
    kd, cd, qd, dm, bd, ones_bd = tabs
    cb = 2
    assert nc % cb == 0 and ncc % cb == 0
    nblk, ncb = nc // cb, ncc // cb
    fwd = lambda b, t: (b, t, 0)
    bwd = lambda b, t: (b, _bwd_chunk(t, ncb, nblk), 0)
    tab_f = lambda b, t: (t, 0)
    tab_b = lambda b, t: (_bwd_chunk(t, ncb, nblk), 0)
    c2 = lambda b, t: (0, 0)
    c3 = lambda b, t: (0, 0, 0)
    st_shape = jax.ShapeDtypeStruct((nb, nc, MIX_W, MIX_W), BF16)
    st_f, st_b = pl.pallas_call(
        functools.partial(_ret_state_kernel, cb=cb),
        grid=(nb, nblk),
        in_specs=[pl.BlockSpec((1, cb * CHUNK, RET_COLS), fwd),
                  pl.BlockSpec((1, cb * CHUNK, RET_COLS), bwd),
                  pl.BlockSpec((cb * CHUNK, MIX_W), tab_f), pl.BlockSpec((cb * CHUNK, MIX_W), tab_f),
                  pl.BlockSpec((cb * CHUNK, MIX_W), tab_b), pl.BlockSpec((cb * CHUNK, MIX_W), tab_b),
                  pl.BlockSpec((MIX_W, MIX_W), c2),
                  pl.BlockSpec((2, CHUNK, MIX_W), c3),
                  pl.BlockSpec((2, 1, MIX_W), c3),
                  pl.BlockSpec((MIX_W, MIX_W), c2)],
        out_specs=[pl.BlockSpec((1, cb, MIX_W, MIX_W), lambda b, t: (b, t, 0, 0)),
                   pl.BlockSpec((1, cb, MIX_W, MIX_W), lambda b, t: (b, _bwd_chunk(t, ncb, nblk), 0, 0))],
        out_shape=[st_shape, st_shape],
        scratch_shapes=[pltpu.VMEM((MIX_W, MIX_W), F32), pltpu.VMEM((MIX_W, MIX_W), F32)],
        compiler_params=_params("arbitrary", "arbitrary"),
        name="ret_state",
    )(p, p, cos, sin, cos, sin, perm, kd, cd, bd)
    blk = lambda b, t: (b, t, 0)
    y_shape = jax.ShapeDtypeStruct((nb, nt, MIX_W), BF16)
    return pl.pallas_call(
        functools.partial(_mix_out_kernel, cb=cb),
        grid=(nb, nc // cb),
        in_specs=[pl.BlockSpec((1, cb * CHUNK, RET_COLS), blk),
                  pl.BlockSpec((1, cb * CHUNK, SG_COLS), lambda b, t: (b, t, P_SG // SG_COLS)),
                  pl.BlockSpec((cb * CHUNK, MIX_W), tab_f), pl.BlockSpec((cb * CHUNK, MIX_W), tab_f),
                  pl.BlockSpec((1, cb, MIX_W, MIX_W), lambda b, t: (b, t, 0, 0)),
                  pl.BlockSpec((1, cb, MIX_W, MIX_W), lambda b, t: (b, t, 0, 0)),
                  pl.BlockSpec((MIX_W, MIX_W), c2),
                  pl.BlockSpec((CHUNK, N_HEADS * CHUNK), c2),
                  pl.BlockSpec((2, CHUNK, MIX_W), c3),
                  pl.BlockSpec((MIX_W, MIX_W), c2),
                  pl.BlockSpec((1, MIX_W), c2),
                  pl.BlockSpec((CHUNK, N_HEADS * CHUNK), c2),
                  pl.BlockSpec((CHUNK, MIX_W), c2)],
        out_specs=[pl.BlockSpec((1, cb * CHUNK, MIX_W), blk), pl.BlockSpec((1, cb * CHUNK, MIX_W), blk)],
        out_shape=[y_shape, y_shape],
        compiler_params=_params("arbitrary", "arbitrary"),
        name="mix_out",
    )(p, p, cos, sin, st_f, st_b, perm, dm, qd, ones_bd, sg_ng, sg_w, sg_bias)


def _softplus(a):
    return jnp.maximum(a, 0.0) + jnp.log1p(jnp.exp(-jnp.abs(a)))


def _dn_prep_kernel(pc_ref, pp_ref, pn_ref, ab_ref, abc_ref, cw_ref, na_ref, dtb_ref, nar_ref, dtbr_ref, ones_ref,
                    qkv_ref, gb_ref, gbc_ref, xe_ref, *, ncc, nc):
    t = pl.program_id(1)
    w3 = 3 * MIX_W
    prev_ok = jnp.where((t != 0) & (t != ncc), 1.0, 0.0)
    next_ok = jnp.where((t != ncc - 1) & (t != nc - 1), 1.0, 0.0)
    tail = pp_ref[0, CHUNK - 16:CHUNK, 0:w3].astype(F32)
    head = pn_ref[0, 0:16, 0:w3].astype(F32)
    xe_ref[0:8, :] = tail[8:16, :] * prev_ok
    xe_ref[8:8 + CHUNK, :] = pc_ref[0, :, 0:w3].astype(F32)
    xe_ref[8 + CHUNK:16 + CHUNK, :] = head[0:8, :] * next_ok
    pad = CONV_W // 2
    y = xe_ref[8 - pad:8 - pad + CHUNK, :] * cw_ref[0:1, :]
    for i in range(1, CONV_W):
        y = y + xe_ref[8 - pad + i:8 - pad + i + CHUNK, :] * cw_ref[i:i + 1, :]
    y = y * jax.nn.sigmoid(y)
    q = y[:, 0:MIX_W]
    k = y[:, MIX_W:2 * MIX_W]
    v = y[:, 2 * MIX_W:w3]
    ones_bd = ones_ref[...]
    qn = q * lax.rsqrt(_head_sum(q * q, ones_bd) + EPS) * (HEAD_DIM ** -0.5)
    kn = k * lax.rsqrt(_head_sum(k * k, ones_bd) + EPS)
    qkv_ref[0, :, 0:MIX_W] = qn.astype(BF16)
    qkv_ref[0, :, MIX_W:2 * MIX_W] = kn.astype(BF16)
    qkv_ref[0, :, 2 * MIX_W:w3] = v.astype(BF16)
    ab = ab_ref[0]
    gb_ref[0, 0:8, :] = na_ref[...] * _softplus(ab[0:8, :] + dtb_ref[...])
    gb_ref[0, 8:16, :] = jax.nn.sigmoid(ab[8:16, :])
    abc = abc_ref[0]
    lane = lax.broadcasted_iota(jnp.int32, (1, 128), 1)
    g_c = nar_ref[...] * _softplus(abc + dtbr_ref[...])
    gbc_ref[0] = jnp.where(lane < 8, g_c, jnp.where(lane < 16, jax.nn.sigmoid(abc), 0.0))


def _split3(x):
    hi = x.astype(BF16)
    r = x - hi.astype(F32)
    mid = r.astype(BF16)
    lo = (r - mid.astype(F32)).astype(BF16)
    return hi, mid, lo


def _tri_inverse(mats, ii, jj):
    eye = jnp.where(ii == jj, 1.0, 0.0)
    nd = [jnp.where((ii // 16) == (jj // 16), n, 0.0) for n in mats]
    p1 = [_mm(x, x) for x in nd]
    m = [eye - x for x in nd]
    p2 = [_mm(x, x) for x in p1]
    m = [x + _mm(x, y) for x, y in zip(m, p1)]
    p3 = [_mm(x, x) for x in p2]
    m = [x + _mm(x, y) for x, y in zip(m, p2)]
    m = [x + _mm(x, y) for x, y in zip(m, p3)]
    for lvl in (16, 32, 64):
        off_mask = ((ii // (2 * lvl)) == (jj // (2 * lvl))) & ((ii // lvl) != (jj // lvl))
        t = [_mm(jnp.where(off_mask, n, 0.0), x) for n, x in zip(mats, m)]
        m = [x - _mm(x, y) for x, y in zip(m, t)]
    return m


def _dn_pre(qkv, g, gbc, d, lower):
    c = CHUNK
    qn = qkv[:, 0:MIX_W]
    kn = qkv[:, MIX_W:2 * MIX_W]
    v = qkv[:, 2 * MIX_W:3 * MIX_W]
    ii = lax.broadcasted_iota(jnp.int32, (c, c), 0)
    jj = lax.broadcasted_iota(jnp.int32, (c, c), 1)
    incl = (ii >= jj) if lower else (ii <= jj)
    tri = jnp.where(incl, 1.0, 0.0).astype(BF16)
    g_row = sum(_dot_nt(part, tri) for part in _split3(g))[N_HEADS * d:N_HEADS * (d + 1), :]
    cum = sum(_dot(tri, part) for part in _split3(gbc))
    g_col = cum[:, N_HEADS * d:N_HEADS * (d + 1)]
    b_col = gbc[:, 2 * N_HEADS + N_HEADS * d:2 * N_HEADS + N_HEADS * (d + 1)]
    g_cols4 = jnp.concatenate([jnp.broadcast_to(g_col[:, h:h + 1], (c, c)) for h in range(N_HEADS)], axis=1)
    b_cols4 = jnp.concatenate([jnp.broadcast_to(b_col[:, h:h + 1], (c, c)) for h in range(N_HEADS)], axis=1)
    g_rows4 = jnp.concatenate([g_row[h:h + 1, :] for h in range(N_HEADS)], axis=1)
    incl4 = jnp.concatenate([incl] * N_HEADS, axis=1)
    diag4 = jnp.concatenate([ii == jj] * N_HEADS, axis=1)
    decay = jnp.where(incl4, jnp.exp(jnp.where(incl4, g_cols4 - g_rows4, 0.0)), 0.0)
    kstack = _stack_heads(kn)
    kk = _dot_nt(kn, kstack)
    qk = _dot_nt(qn, kstack)
    n_mat = jnp.where(diag4, 0.0, decay * kk * b_cols4)
    attn = (decay * qk).astype(BF16)
    g256 = _expand_heads(g_col)
    eg256 = jnp.exp(g256)
    b256 = _expand_heads(b_col)
    vb = v.astype(F32) * b256
    kbg = kn.astype(F32) * b256 * eg256
    rhs = jnp.concatenate([_stack_heads(vb), _stack_heads(kbg)], axis=1)
    g_last = g256[c - 1:c, :] if lower else g256[0:1, :]
    kdec = (kn.astype(F32) * jnp.exp(g_last - g256)).astype(BF16)
    n_heads = [n_mat[:, h * c:(h + 1) * c] for h in range(N_HEADS)]
    return n_heads, dict(qn=qn, attn=attn, rhs=rhs, eg=eg256, kdec=kdec, sdec=jnp.exp(g_last))


def _dn_post(z, s_prev, bd):
    s_bf = s_prev.astype(BF16)
    w = z["u"] - _dot(z["wk"], s_bf)
    o = z["eg"] * _dot(z["qn"], s_bf) + _dot(z["attn"], _stack_heads(w))
    s_next = z["sdec"] * s_prev + bd * _dot_tn(z["kdec"], w.astype(BF16))
    return o, s_next


def _dn_scan_kernel(qf_ref, qb_ref, gf_ref, gb_ref, gcf_ref, gcb_ref, bd_ref, of_ref, ob_ref, st_f, st_b, *, cb):
    t = pl.program_id(1)

    @pl.when(t == 0)
    def _():
        st_f[...] = jnp.zeros_like(st_f)
        st_b[...] = jnp.zeros_like(st_b)

    bd = bd_ref[...]
    rows = [slice(i * CHUNK, (i + 1) * CHUNK) for i in range(cb)]
    mats, pres = [], []
    for d, (q_ref, g_ref, gc_ref) in enumerate(((qf_ref, gf_ref, gcf_ref), (qb_ref, gb_ref, gcb_ref))):
        for r in rows:
            n_heads, pre = _dn_pre(q_ref[0, r, :], g_ref[0, :, r], gc_ref[0, r, :], d, d == 0)
            mats += n_heads
            pres.append(pre)
    ii = lax.broadcasted_iota(jnp.int32, (CHUNK, CHUNK), 0)
    jj = lax.broadcasted_iota(jnp.int32, (CHUNK, CHUNK), 1)
    inv = _tri_inverse(mats, ii, jj)
    for n, pre in enumerate(pres):
        a_inv = jnp.concatenate(inv[N_HEADS * n:N_HEADS * (n + 1)], axis=1).astype(BF16)
        uw = _dot(a_inv, pre["rhs"])
        pre["u"] = uw[:, 0:MIX_W]
        pre["wk"] = uw[:, MIX_W:2 * MIX_W].astype(BF16)
    s_f, s_b = st_f[...], st_b[...]
    for k in range(cb):
        o, s_f = _dn_post(pres[k], s_f, bd)
        of_ref[0, rows[k], :] = o
        o, s_b = _dn_post(pres[cb + cb - 1 - k], s_b, bd)
        ob_ref[0, rows[cb - 1 - k], :] = o
    st_f[...] = s_f
    st_b[...] = s_b


def _deltanet(p, ab_t, ab_c, conv_w, neg_a, dtb, bd, ones_bd, ncc):
    nb, nt, _ = p.shape
    nc = nt // CHUNK
    w3 = 3 * MIX_W
    c2 = lambda b, t: (0, 0)
    dn_blk = P_DN // RET_COLS
    pad_lanes = lambda col: jnp.concatenate([col.reshape(1, -1), jnp.zeros((1, 128 - col.size), F32)], axis=1)
    qkv, gbeta, gbeta_c = pl.pallas_call(
        functools.partial(_dn_prep_kernel, ncc=ncc, nc=nc),
        grid=(nb, nc),
        in_specs=[pl.BlockSpec((1, CHUNK, 4 * MIX_W), lambda b, t: (b, t, dn_blk)),
                  pl.BlockSpec((1, CHUNK, 4 * MIX_W), lambda b, t: (b, jnp.maximum(t - 1, 0), dn_blk)),
                  pl.BlockSpec((1, CHUNK, 4 * MIX_W), lambda b, t: (b, jnp.minimum(t + 1, nc - 1), dn_blk)),
                  pl.BlockSpec((1, 16, CHUNK), lambda b, t: (b, 0, t)),
                  pl.BlockSpec((1, CHUNK, 128), lambda b, t: (b, t, 0)),
                  pl.BlockSpec((8, w3), c2),
                  pl.BlockSpec((8, 1), c2),
                  pl.BlockSpec((8, 1), c2),
                  pl.BlockSpec((1, 128), c2),
                  pl.BlockSpec((1, 128), c2),
                  pl.BlockSpec((MIX_W, MIX_W), c2)],
        out_specs=[pl.BlockSpec((1, CHUNK, w3), lambda b, t: (b, t, 0)),
                   pl.BlockSpec((1, 16, CHUNK), lambda b, t: (b, 0, t)),
                   pl.BlockSpec((1, CHUNK, 128), lambda b, t: (b, t, 0))],
        out_shape=[jax.ShapeDtypeStruct((nb, nt, w3), BF16),
                   jax.ShapeDtypeStruct((nb, 16, nt), F32),
                   jax.ShapeDtypeStruct((nb, nt, 128), F32)],
        scratch_shapes=[pltpu.VMEM((CHUNK + 16, w3), F32)],
        compiler_params=_params("arbitrary", "arbitrary"),
        name="dn_prep",
    )(p, p, p, ab_t, ab_c, conv_w, neg_a, dtb, pad_lanes(neg_a), pad_lanes(dtb), ones_bd)
    cb = 2
    assert nc % cb == 0 and ncc % cb == 0
    rows = cb * CHUNK
    cur_b = lambda t: _bwd_chunk(t, ncc // cb, nc // cb)
    o_shape = jax.ShapeDtypeStruct((nb, nt, MIX_W), F32)
    return pl.pallas_call(
        functools.partial(_dn_scan_kernel, cb=cb),
        grid=(nb, nc // cb),
        in_specs=[pl.BlockSpec((1, rows, w3), lambda b, t: (b, t, 0)),
                  pl.BlockSpec((1, rows, w3), lambda b, t: (b, cur_b(t), 0)),
                  pl.BlockSpec((1, 16, rows), lambda b, t: (b, 0, t)),
                  pl.BlockSpec((1, 16, rows), lambda b, t: (b, 0, cur_b(t))),
                  pl.BlockSpec((1, rows, 128), lambda b, t: (b, t, 0)),
                  pl.BlockSpec((1, rows, 128), lambda b, t: (b, cur_b(t), 0)),
                  pl.BlockSpec((MIX_W, MIX_W), c2)],
        out_specs=[pl.BlockSpec((1, rows, MIX_W), lambda b, t: (b, t, 0)),
                   pl.BlockSpec((1, rows, MIX_W), lambda b, t: (b, cur_b(t), 0))],
        out_shape=[o_shape, o_shape],
        scratch_shapes=[pltpu.VMEM((MIX_W, MIX_W), F32), pltpu.VMEM((MIX_W, MIX_W), F32)],
        compiler_params=_params("arbitrary", "arbitrary"),
        name="dn_scan",
    )(qkv, qkv, gbeta, gbeta, gbeta_c, gbeta_c, bd)


QK_W = 256


VT_ROWS = 144


def _mla_prep_kernel(p_ref, c_ref, s_ref, perm_ref, qg_ref, kg_ref, wqn_ref, wqr_ref, wa_ref, selq_ref, selc_ref,
                     selr_ref, selv_ref, one_ref, qt_ref, kv_ref, vt_ref, *, scale):
    p = p_ref[0]
    cos, sin, perm = c_ref[...], s_ref[...], perm_ref[...]
    cq = p[:, 0:Q_LORA].astype(F32)
    cqn = (cq * lax.rsqrt(jnp.mean(cq * cq, axis=-1, keepdims=True) + EPS) * qg_ref[...]).astype(BF16)
    q_nope = _dot(cqn, wqn_ref[...]).astype(BF16)
    q_rope = _dot(cqn, wqr_ref[...]).astype(BF16)
    q_rot = (_rot(q_rope, cos, sin, perm) * scale).astype(BF16)
    q_nope_s = (q_nope.astype(F32) * scale).astype(BF16)
    for h in range(N_HEADS):
        qt_ref[0, h] = (_dot_nt(wa_ref[h], q_nope_s) + _dot_nt(selq_ref[h], q_rot)).astype(BF16)
    ckv = p[:, Q_LORA:Q_LORA + KV_LORA].astype(F32)
    ckvn = (ckv * lax.rsqrt(jnp.mean(ckv * ckv, axis=-1, keepdims=True) + EPS) * kg_ref[...]).astype(BF16)
    kr = p[:, Q_LORA + KV_LORA:MLA_PAD]
    kr_rot = _rot(kr, cos, sin, perm).astype(BF16)
    kv_ref[0] = (_dot(ckvn, selc_ref[...]) + _dot(kr_rot, selr_ref[...])).astype(BF16)
    vt_ref[0] = (_dot_nt(selv_ref[...], ckvn) + one_ref[...]).astype(BF16)


def _mla_attn_kernel(qt_ref, kv_ref, vt_ref, wuv_ref, y_ref, m_ref, acc_ref, s_ref, *, tq, tk, n_ctx, nt, q0):
    i = pl.program_id(1) + q0
    heads = range(N_HEADS)
    m_ref[...] = jnp.full_like(m_ref, -jnp.inf)
    acc_ref[...] = jnp.zeros_like(acc_ref)

    def scores(j0, size, slot):
        k = kv_ref[0, pl.ds(j0, size), :]
        for h in heads:
            s_ref[slot, h, 0:size, :] = _dot(k, qt_ref[0, h])

    def softmax_pv(j0, size, slot):
        vt = vt_ref[0, :, pl.ds(j0, size)]
        s = [s_ref[slot, h, 0:size, :] for h in heads]
        m_old = [m_ref[h] for h in heads]
        m_new = [jnp.maximum(m_old[h], jnp.max(s[h], axis=0, keepdims=True)) for h in heads]
        pr = [jnp.exp2(s[h] - m_new[h]).astype(BF16) for h in heads]
        pv = [_dot(vt, pr[h]) for h in heads]
        for h in heads:
            acc_ref[h] = jnp.exp2(m_old[h] - m_new[h]) * acc_ref[h] + pv[h]
            m_ref[h] = m_new[h]

    scores(0, n_ctx, 0)
    is_latent = (i + 1) * tq > n_ctx

    @pl.when(jnp.logical_not(is_latent))
    def _():
        softmax_pv(0, n_ctx, 0)

    @pl.when(is_latent)
    def _():
        n_tiles = (nt - n_ctx) // tk
        last = n_ctx + (n_tiles - 1) * tk
        scores(n_ctx, tk, 1)
        softmax_pv(0, n_ctx, 0)

        def body(jj, carry):
            t0 = pl.multiple_of(n_ctx + 2 * jj * tk, 256)
            t1 = pl.multiple_of(jnp.minimum(t0 + tk, last), 256)
            t2 = pl.multiple_of(jnp.minimum(t0 + 2 * tk, last), 256)
            scores(t1, tk, 0)
            softmax_pv(t0, tk, 1)
            scores(t2, tk, 1)
            softmax_pv(t1, tk, 0)
            return carry

        lax.fori_loop(0, n_tiles // 2, body, 0)
        if n_tiles % 2:
            softmax_pv(last, tk, 1)

    y = None
    for h in range(N_HEADS):
        acc = acc_ref[h]
        o = (acc[0:KV_LORA, :] / acc[KV_LORA:KV_LORA + 1, :]).astype(BF16)
        term = _dot_tn(o, wuv_ref[h])
        y = term if y is None else y + term
    y_ref[0] = y.astype(BF16)


def _mla(p, cos, sin, perm, qg, kg, wqn, wqr, wa, selq, selc, selr, selv, one_col, wuv, n_ctx, ctx_out):
    nb, nt, _ = p.shape
    tm = _pick(nt, (768, 384, 256, 128))
    scale = (NOPE_DIM + ROPE_DIM) ** -0.5 * math.log2(math.e)
    c2 = lambda b, i: (0, 0)
    c3 = lambda b, i: (0, 0, 0)
    qt, kv, vt = pl.pallas_call(
        functools.partial(_mla_prep_kernel, scale=scale),
        grid=(nb, nt // tm),
        in_specs=[pl.BlockSpec((1, tm, MLA_PAD), lambda b, i: (b, i, P_MLA // MLA_PAD)),
                  pl.BlockSpec((tm, 128), lambda b, i: (i, 0)),
                  pl.BlockSpec((tm, 128), lambda b, i: (i, 0)),
                  pl.BlockSpec((128, 128), c2),
                  pl.BlockSpec((1, Q_LORA), c2),
                  pl.BlockSpec((1, KV_LORA), c2),
                  pl.BlockSpec((Q_LORA, N_HEADS * NOPE_DIM), c2),
                  pl.BlockSpec((Q_LORA, N_HEADS * ROPE_DIM), c2),
                  pl.BlockSpec((N_HEADS, QK_W, N_HEADS * NOPE_DIM), c3),
                  pl.BlockSpec((N_HEADS, QK_W, N_HEADS * ROPE_DIM), c3),
                  pl.BlockSpec((KV_LORA, QK_W), c2),
                  pl.BlockSpec((128, QK_W), c2),
                  pl.BlockSpec((VT_ROWS, KV_LORA), c2),
                  pl.BlockSpec((VT_ROWS, 1), c2)],
        out_specs=[pl.BlockSpec((1, N_HEADS, QK_W, tm), lambda b, i: (b, 0, 0, i)),
                   pl.BlockSpec((1, tm, QK_W), lambda b, i: (b, i, 0)),
                   pl.BlockSpec((1, VT_ROWS, tm), lambda b, i: (b, 0, i))],
        out_shape=[jax.ShapeDtypeStruct((nb, N_HEADS, QK_W, nt), BF16),
                   jax.ShapeDtypeStruct((nb, nt, QK_W), BF16),
                   jax.ShapeDtypeStruct((nb, VT_ROWS, nt), BF16)],
        compiler_params=_params("arbitrary", "arbitrary"),
        name="mla_prep",
    )(p, cos, sin, perm, qg, kg, wqn, wqr, wa, selq, selc, selr, selv, one_col)
    tq = 256
    tk = _pick(nt - n_ctx, (512, 256))
    q0 = 0 if ctx_out else n_ctx // tq
    return pl.pallas_call(
        functools.partial(_mla_attn_kernel, tq=tq, tk=tk, n_ctx=n_ctx, nt=nt, q0=q0),
        grid=(nb, nt // tq - q0),
        in_specs=[pl.BlockSpec((1, N_HEADS, QK_W, tq), lambda b, i: (b, 0, 0, i + q0)),
                  pl.BlockSpec((1, nt, QK_W), lambda b, i: (b, 0, 0)),
                  pl.BlockSpec((1, VT_ROWS, nt), lambda b, i: (b, 0, 0)),
                  pl.BlockSpec((N_HEADS, KV_LORA, MIX_W), c3)],
        out_specs=pl.BlockSpec((1, tq, MIX_W), lambda b, i: (b, i + q0, 0)),
        out_shape=jax.ShapeDtypeStruct((nb, nt, MIX_W), BF16),
        scratch_shapes=[pltpu.VMEM((N_HEADS, 1, tq), F32), pltpu.VMEM((N_HEADS, VT_ROWS, tq), F32),
                        pltpu.VMEM((2, N_HEADS, max(tk, n_ctx), tq), F32)],
        compiler_params=_params("arbitrary", "arbitrary"),
        name="mla_attn",
    )(qt, kv, vt, wuv)


def _merge_kernel(x_ref, yr_ref, ys_ref, of_ref, ob_ref, ym_ref, z_ref, g0_ref, g1_ref, g2_ref, g3_ref,
                  wb_ref, wo_ref, ng_ref, gp_ref, ml_ref, mc_ref, ones_ref, o_ref, *, tm, n_ctx, row0):
    i = pl.program_id(1)
    od = of_ref[0] + ob_ref[0]
    ms = _head_sum(od * od, ones_ref[...]) * (1.0 / HEAD_DIM)
    z = z_ref[0].astype(F32)
    ydn = (od * lax.rsqrt(ms + EPS) * ng_ref[...]) * (z * jax.nn.sigmoid(z))
    ys = (yr_ref[0], ys_ref[0], ydn.astype(BF16), ym_ref[0])
    gates = (g0_ref, g1_ref, g2_ref, g3_ref)
    acc = None
    for b in range(N_BRANCH):
        term = jax.nn.sigmoid(gates[b][0].astype(F32)) * _dot(ys[b], wb_ref[b])
        acc = term if acc is None else acc + term
    y = _dot(acc.astype(BF16), wo_ref[...])
    r = y * lax.rsqrt(jnp.mean(y * y, axis=-1, keepdims=True) + EPS) * gp_ref[...]
    rows = lax.broadcasted_iota(jnp.int32, (tm, 1), 0) + (row0 + i * tm)
    gate = jnp.where(rows < n_ctx, mc_ref[0, 2:3, :], ml_ref[0, 2:3, :])
    o_ref[0] = x_ref[0] + gate * r


def _merge(xa, y_ret, y_sg, o_f, o_b, y_mla, p, wb, wo, ng, gp, mod, ones_bd, n_ctx, row0):
    nb, nt, d = xa.shape
    n_rows = nt - row0
    tm = _pick(n_rows, (768, 512, 384, 256, 128))
    c2 = lambda b, i: (0, 0)
    if row0 == 0:
        def window(width, col):
            return pl.BlockSpec((1, tm, width), lambda b, i: (b, i, col // width))
    else:
        def window(width, col):
            return pl.BlockSpec((pl.Element(1), pl.Element(tm), pl.Element(width)),
                                lambda b, i: (b, pl.multiple_of(row0 + i * tm, 128), col))
    y_spec = window(MIX_W, 0)
    gate_specs = [window(d, P_GATE + k * d) for k in range(N_BRANCH)]
    return pl.pallas_call(
        functools.partial(_merge_kernel, tm=tm, n_ctx=n_ctx, row0=row0),
        grid=(nb, n_rows // tm),
        in_specs=[window(d, 0), y_spec, y_spec, y_spec, y_spec, y_spec,
                  window(MIX_W, P_DN + 3 * MIX_W),
                  *gate_specs,
                  pl.BlockSpec((N_BRANCH, MIX_W, d), lambda b, i: (0, 0, 0)),
                  pl.BlockSpec((d, d), c2),
                  pl.BlockSpec((1, MIX_W), c2),
                  pl.BlockSpec((1, d), c2),
                  pl.BlockSpec((1, 6, d), lambda b, i: (b, 0, 0)),
                  pl.BlockSpec((1, 6, d), lambda b, i: (nb, 0, 0)),
                  pl.BlockSpec((MIX_W, MIX_W), c2)],
        out_specs=pl.BlockSpec((1, tm, d), lambda b, i: (b, i, 0)),
        out_shape=jax.ShapeDtypeStruct((nb, n_rows, d), F32),
        compiler_params=_params("arbitrary", "arbitrary"),
        name="merge",
    )(xa, y_ret, y_sg, o_f, o_b, y_mla, p, p, p, p, p, wb, wo, ng, gp, mod, mod, ones_bd)


def _route(sel, aff):
    rows = [sel[e:e + 1, :] for e in range(N_EXPERTS)]
    pairs = [(a, b) for a in range(EXPERTS_PER_GROUP) for b in range(a + 1, EXPERTS_PER_GROUP)]
    grp_score, grp_pair = [], []
    for g in range(N_GROUPS):
        base = g * EXPERTS_PER_GROUP
        best = rows[base + pairs[0][0]] + rows[base + pairs[0][1]]
        best_p = jnp.zeros_like(best, dtype=jnp.int32)
        for pi in range(1, len(pairs)):
            s = rows[base + pairs[pi][0]] + rows[base + pairs[pi][1]]
            take = s > best
            best = jnp.where(take, s, best)
            best_p = jnp.where(take, pi, best_p)
        grp_score.append(best)
        grp_pair.append(best_p)
    top = grp_score[0]
    top_g = jnp.zeros_like(grp_pair[0])
    top_p = grp_pair[0]
    for g in range(1, N_GROUPS):
        take = grp_score[g] > top
        top = jnp.where(take, grp_score[g], top)
        top_g = jnp.where(take, g, top_g)
        top_p = jnp.where(take, grp_pair[g], top_p)
    picked = []
    for e in range(N_EXPERTS):
        g, k = divmod(e, EXPERTS_PER_GROUP)
        in_pair = None
        for pi, (a, b) in enumerate(pairs):
            if k in (a, b):
                hit = top_p == pi
                in_pair = hit if in_pair is None else (in_pair | hit)
        picked.append(jnp.where((top_g == g) & in_pair, aff[e:e + 1, :], 0.0))
    denom = picked[0]
    for e in range(1, N_EXPERTS):
        denom = denom + picked[e]
    return [pk / denom for pk in picked]


def _swiglu(hn, w1, w3, w2, scale):
    a = _dot(hn, w1.astype(BF16))
    h = (a * jax.nn.sigmoid(a)) * _dot(hn, w3.astype(BF16))
    if scale is not None:
        h = h * scale
    return _dot(h.astype(BF16), w2.astype(BF16))


def _moe_kernel(x_ref, ml_ref, mc_ref, g2_ref, gp_ref, rw_ref, rb_ref, ws1_ref, ws3_ref, ws2_ref,
                w1_ref, w3_ref, w2_ref, o_ref, hn_ref, comb_t_ref, comb_ref, acc_ref, *, tm, rb, n_ctx):
    i = pl.program_id(1)
    e = pl.program_id(2)

    @pl.when(e == 0)
    def _():
        def blk(r, carry):
            r0 = pl.multiple_of(r * rb, rb)
            x = x_ref[0, pl.ds(r0, rb), :]
            hn = _norm_modulate(x, g2_ref[...], i * tm + r0 < n_ctx, mc_ref, ml_ref, 3, 4)
            hn_ref[pl.ds(r0, rb), :] = hn.astype(BF16)
            return carry

        lax.fori_loop(0, tm // rb, blk, 0)
        hn = hn_ref[...]
        aff = jax.nn.sigmoid(_dot_nt(rw_ref[...], hn))
        comb = _route(aff + rb_ref[...], aff)
        comb_t_ref[...] = jnp.zeros_like(comb_t_ref)
        for k in range(N_EXPERTS):
            comb_t_ref[k:k + 1, :] = comb[k]
        comb_ref[...] = comb_t_ref[...].T
        acc_ref[...] = _swiglu(hn, ws1_ref[0], ws3_ref[0], ws2_ref[0], None)

    @pl.when(e > 0)
    def _():
        lane = lax.broadcasted_iota(jnp.int32, (1, 128), 1)
        comb = comb_ref[...]
        hn = hn_ref[...]
        first = 2 * (e - 1)
        y = None
        for k in range(2):
            c_k = jnp.sum(jnp.where(lane == first + k, comb, 0.0), axis=-1, keepdims=True)
            term = _swiglu(hn, w1_ref[0, k], w3_ref[0, k], w2_ref[0, k], c_k)
            y = term if y is None else y + term
        acc_ref[...] += y

    @pl.when(e == pl.num_programs(2) - 1)
    def _():
        y = acc_ref[...]
        r = y * lax.rsqrt(jnp.mean(y * y, axis=-1, keepdims=True) + EPS) * gp_ref[...]
        rows = lax.broadcasted_iota(jnp.int32, (tm, 1), 0) + i * tm
        gate = jnp.where(rows < n_ctx, mc_ref[0, 5:6, :], ml_ref[0, 5:6, :])
        o_ref[0] = x_ref[0] + gate * r


def _moe(xa, mod, g2, gp, rw_t, rbias, ws1, ws3, ws2, w1, w3, w2, layer, n_ctx):
    nb, nt, d = xa.shape
    tm = _pick(nt, (1024, 768, 512, 384, 256, 128))
    n_pairs = w1.shape[1] // 2
    row = lambda b, i, e: (b, i, 0)
    c2 = lambda b, i, e: (0, 0)
    shared_blk = lambda b, i, e: (layer, 0, 0)
    pair_blk = lambda b, i, e: (layer, jnp.maximum(e - 1, 0), 0, 0)
    return pl.pallas_call(
        functools.partial(_moe_kernel, tm=tm, rb=128, n_ctx=n_ctx),
        grid=(nb, nt // tm, n_pairs + 1),
        in_specs=[pl.BlockSpec((1, tm, d), row),
                  pl.BlockSpec((1, 6, d), lambda b, i, e: (b, 0, 0)),
                  pl.BlockSpec((1, 6, d), lambda b, i, e: (nb, 0, 0)),
                  pl.BlockSpec((1, d), c2),
                  pl.BlockSpec((1, d), c2),
                  pl.BlockSpec((N_EXPERTS, d), c2),
                  pl.BlockSpec((N_EXPERTS, 1), c2),
                  pl.BlockSpec((1, d, D_EXPERT), shared_blk),
                  pl.BlockSpec((1, d, D_EXPERT), shared_blk),
                  pl.BlockSpec((1, D_EXPERT, d), shared_blk),
                  pl.BlockSpec((1, 2, d, D_EXPERT), pair_blk),
                  pl.BlockSpec((1, 2, d, D_EXPERT), pair_blk),
                  pl.BlockSpec((1, 2, D_EXPERT, d), pair_blk)],
        out_specs=pl.BlockSpec((1, tm, d), row),
        out_shape=jax.ShapeDtypeStruct((nb, nt, d), F32),
        scratch_shapes=[pltpu.VMEM((tm, d), BF16), pltpu.VMEM((128, tm), F32), pltpu.VMEM((tm, 128), F32),
                        pltpu.VMEM((tm, d), F32)],
        compiler_params=_params("arbitrary", "arbitrary", "arbitrary"),
        name="moe",
    )(xa, mod, mod, g2, gp, rw_t, rbias, ws1, ws3, ws2, w1, w3, w2)


def _swap_perm(width, group):
    j = np.arange(width)
    src = np.where((j % group) < group // 2, j + group // 2, j - group // 2)
    return jnp.asarray(np.arange(width)[:, None] == src[None, :], BF16)


def _rope_tables(n_lat, n_ctx):
    def angles(pos, dim):
        half = dim // 2
        inv = ROPE_BASE ** (-jnp.arange(half, dtype=F32) / half)
        return pos.astype(F32)[:, None] * inv[None, :]

    def tables(cos_parts, sin_parts, reps):
        cos = jnp.tile(jnp.concatenate(cos_parts, axis=-1), (1, reps))
        sin = jnp.tile(jnp.concatenate(sin_parts, axis=-1), (1, reps))
        w = cos.shape[1]
        return (jnp.concatenate([jnp.ones((n_ctx, w), F32), cos], axis=0),
                jnp.concatenate([jnp.zeros((n_ctx, w), F32), sin], axis=0))

    rows = n_lat // GRID_W
    ang_t = angles(jnp.arange(n_lat), HEAD_DIM)
    ang_r = angles(jnp.repeat(jnp.arange(rows), GRID_W), ROPE_DIM // 2)
    ang_c = angles(jnp.tile(jnp.arange(GRID_W), rows), ROPE_DIM // 2)
    ct, st = jnp.cos(ang_t), jnp.sin(ang_t)
    ret = tables([ct, ct], [-st, st], N_HEADS)
    cr, sr, cc, sc = jnp.cos(ang_r), jnp.sin(ang_r), jnp.cos(ang_c), jnp.sin(ang_c)
    mla = tables([cr, cr, cc, cc], [-sr, sr, -sc, sc], N_HEADS)
    return ret, mla


def _ret_tables(logit):
    log_g = jax.nn.log_sigmoid(logit.astype(F32))
    lane_lg = jnp.repeat(log_g, HEAD_DIM, axis=1)
    idx = jnp.arange(CHUNK, dtype=F32)[:, None]
    kd = jnp.stack([jnp.exp(lane_lg[0][None, :] * (CHUNK - 1 - idx)), jnp.exp(lane_lg[1][None, :] * idx)])
    qd = jnp.stack([jnp.exp(lane_lg[0][None, :] * (idx + 1)), jnp.exp(lane_lg[1][None, :] * (CHUNK - idx))])
    cd = jnp.exp(lane_lg * CHUNK)[:, None, :]
    diff = idx - idx.T
    blocks = []
    for h in range(N_HEADS):
        f = jnp.exp(log_g[0, h] * jnp.where(diff >= 0, diff, 0.0))
        b = jnp.exp(log_g[1, h] * jnp.where(diff < 0, -diff, 0.0))
        blocks.append(jnp.where(diff >= 0, f, b))
    dm = jnp.concatenate(blocks, axis=1)
    return kd, cd, qd, dm


def _pack_w_in(w_in):
    d = w_in.shape[0]
    mla = jnp.concatenate([w_in[:, OFF_MLA:OFF_MLA + MLA_COLS], jnp.zeros((d, MLA_PAD - MLA_COLS), w_in.dtype)], 1)
    w = jnp.concatenate([w_in[:, OFF_RET:OFF_RET + RET_COLS], w_in[:, OFF_DN:OFF_DN + 4 * MIX_W],
                         w_in[:, OFF_SG:OFF_SG + SG_COLS], mla, w_in[:, OFF_GATE:OFF_GATE + GATE_COLS]], axis=1)
    wab = w_in[:, OFF_DN + 4 * MIX_W:OFF_DN + DN_COLS]
    wabc = jnp.concatenate([wab, jnp.zeros((d, 128 - 4 * N_HEADS), w_in.dtype)], axis=1)
    return w.astype(BF16), wab.T.astype(BF16), wabc.astype(BF16)


def _mla_weights(w_uq, w_ukv):
    dq = NOPE_DIM + ROPE_DIM
    dkv = NOPE_DIM + V_DIM
    wq = w_uq.reshape(Q_LORA, N_HEADS, dq)
    wqn = wq[:, :, :NOPE_DIM].reshape(Q_LORA, N_HEADS * NOPE_DIM)
    wqr = wq[:, :, NOPE_DIM:].reshape(Q_LORA, N_HEADS * ROPE_DIM)
    wkv = w_ukv.reshape(KV_LORA, N_HEADS, dkv)
    head_eye = jnp.eye(N_HEADS, dtype=F32)
    wa = jnp.einsum("chd,hg->hcgd", wkv[:, :, :NOPE_DIM], head_eye).reshape(N_HEADS, KV_LORA, N_HEADS * NOPE_DIM)
    wa = jnp.pad(wa, ((0, 0), (0, QK_W - KV_LORA), (0, 0)))
    wuv = jnp.einsum("chd,hg->hcgd", wkv[:, :, NOPE_DIM:], head_eye).reshape(N_HEADS, KV_LORA, MIX_W)
    selq = np.zeros((N_HEADS, QK_W, N_HEADS * ROPE_DIM), np.float32)
    for h in range(N_HEADS):
        selq[h, KV_LORA:KV_LORA + ROPE_DIM, h * ROPE_DIM:(h + 1) * ROPE_DIM] = np.eye(ROPE_DIM)
    selc = np.zeros((KV_LORA, QK_W), np.float32)
    selc[:, 0:KV_LORA] = np.eye(KV_LORA)
    selr = np.zeros((128, QK_W), np.float32)
    selr[0:ROPE_DIM, KV_LORA:KV_LORA + ROPE_DIM] = np.eye(ROPE_DIM)
    selv = np.zeros((VT_ROWS, KV_LORA), np.float32)
    selv[0:KV_LORA, :] = np.eye(KV_LORA)
    one_col = np.zeros((VT_ROWS, 1), np.float32)
    one_col[KV_LORA, 0] = 1.0
    return (tuple(jnp.asarray(a, BF16) for a in (wqn, wqr, wa, selq, selc, selr, selv))
            + (jnp.asarray(one_col), wuv.astype(BF16)))


def kernel(x, c, ctx, c_ctx, w_ada, b_ada, g_pre1, g_post1, g_pre2, g_post2, w_in, ret_decay_logit, sg_norm_g, sg_w, sg_b, dn_conv_w, dn_A_log, dn_dt_bias, dn_norm_g, mla_q_norm_g, mla_kv_norm_g, mla_w_uq, mla_w_ukv, w_branch, w_out, router_w, router_bias, moe_w1, moe_w3, moe_w2, shared_w1, shared_w3, shared_w2):
    nb, n_lat, d = x.shape
    n_ctx = ctx.shape[1]
    depth = w_in.shape[0]
    assert d == D_MODEL and n_lat % GRID_W == 0 and n_lat % CHUNK == 0 and n_ctx % 256 == 0
    ncc = n_ctx // CHUNK

    n_cond = -(-(nb + 1) // 8) * 8
    cond = jnp.concatenate([c, c_ctx[None], jnp.zeros((n_cond - nb - 1, d), F32)], axis=0)
    mod_all = _adaln(cond, w_ada, b_ada).reshape(depth, n_cond, 6, d)

    (ret_cos, ret_sin), (mla_cos, mla_sin) = _rope_tables(n_lat, n_ctx)
    perm_ret = _swap_perm(MIX_W, HEAD_DIM)
    perm_mla = _swap_perm(N_HEADS * ROPE_DIM, ROPE_DIM // 2)
    lane_head = jnp.arange(MIX_W) // HEAD_DIM
    bd = (lane_head[:, None] == lane_head[None, :]).astype(F32)
    ones_bd = bd.astype(BF16)
    rw_t = router_w.T.astype(BF16)
    rbias = router_bias.astype(F32)[:, None]

    xa = jnp.concatenate([ctx, x], axis=1)
    for l in range(depth):
        mod = mod_all[l]
        w_l, wab_l, wabc_l = _pack_w_in(w_in[l])
        p, ab_t, ab_c = _inproj(xa, mod, g_pre1[l][None], w_l, wab_l, wabc_l, n_ctx)

        kd, cd, qd, dm = _ret_tables(ret_decay_logit[l])
        wcat = jnp.concatenate([sg_w[l, h] for h in range(N_HEADS)], axis=1).astype(BF16)
        sg_bias = jnp.repeat(sg_b[l].T, HEAD_DIM, axis=1)
        y_ret, y_sg = _retention_and_sgate(p, ret_cos, ret_sin, perm_ret, (kd, cd, qd, dm, bd, ones_bd),
                                           sg_norm_g[l][None], wcat, sg_bias, ncc)

        neg_a = (-jnp.exp(dn_A_log[l].astype(F32))).reshape(2 * N_HEADS, 1)
        dtb = dn_dt_bias[l].astype(F32).reshape(2 * N_HEADS, 1)
        conv_w = jnp.concatenate([dn_conv_w[l], jnp.zeros((8 - CONV_W, 3 * MIX_W), F32)], axis=0)
        o_f, o_b = _deltanet(p, ab_t, ab_c, conv_w, neg_a, dtb, bd, ones_bd, ncc)

        last = l == depth - 1
        row0 = n_ctx if last else 0
        y_mla = _mla(p, mla_cos, mla_sin, perm_mla, mla_q_norm_g[l][None], mla_kv_norm_g[l][None],
                     *_mla_weights(mla_w_uq[l], mla_w_ukv[l]), n_ctx, not last)

        xa = _merge(xa, y_ret, y_sg, o_f, o_b, y_mla, p, w_branch[l].astype(BF16), w_out[l].astype(BF16),
                    jnp.tile(dn_norm_g[l], N_HEADS)[None], g_post1[l][None], mod, ones_bd, n_ctx, row0)

        xa = _moe(xa, mod, g_pre2[l][None], g_post2[l][None], rw_t, rbias, shared_w1, shared_w3, shared_w2,
                  moe_w1, moe_w3, moe_w2, l, n_ctx - row0)
    return xa
```

```python
import functools
import math

import jax
import jax.numpy as jnp
import numpy as np
from jax import lax
from jax.experimental import pallas as pl
from jax.experimental.pallas import tpu as pltpu

F32 = jnp.float32
BF16 = jnp.bfloat16
HIGHEST = lax.Precision.HIGHEST

D_MODEL = 1024
GRID_W = 64
N_HEADS = 4
HEAD_DIM = 64
MIX_W = N_HEADS * HEAD_DIM
CHUNK = 128
ROPE_BASE = 10000.0
EPS = 1e-6
RET_DECAY_EXP0 = 5.0
CONV_W = 5
Q_LORA = 256
KV_LORA = 128
NOPE_DIM = 64
ROPE_DIM = 32
V_DIM = 64
N_EXPERTS = 16
N_GROUPS = 4
EXPERTS_PER_GROUP = N_EXPERTS // N_GROUPS
D_EXPERT = 256
N_BRANCH = 4

RET_COLS = 4 * MIX_W
SG_COLS = 2 * MIX_W
DN_COLS = 4 * MIX_W + 4 * N_HEADS
MLA_COLS = Q_LORA + KV_LORA + ROPE_DIM
GATE_COLS = N_BRANCH * D_MODEL
OFF_RET = 0
OFF_SG = OFF_RET + RET_COLS
OFF_DN = OFF_SG + SG_COLS
OFF_MLA = OFF_DN + DN_COLS
OFF_GATE = OFF_MLA + MLA_COLS

P_RET = 0
P_DN = 1024
P_SG = 2048
P_MLA = 2560
P_GATE = 3072
P_COLS = 7168
MLA_PAD = 512

VMEM_LIMIT = 56 * 1024 * 1024


def _dot(a, b, precision=None):
    return jnp.dot(a, b, preferred_element_type=F32, precision=precision)


def _dot_nt(a, b, precision=None):
    return lax.dot_general(a, b, (((1,), (1,)), ((), ())), preferred_element_type=F32, precision=precision)


def _dot_tn(a, b):
    return lax.dot_general(a, b, (((0,), (0,)), ((), ())), preferred_element_type=F32)


def _mm(a, b):
    return _dot(a.astype(BF16), b.astype(BF16))


def _params(*sem):
    return pltpu.CompilerParams(dimension_semantics=sem, vmem_limit_bytes=VMEM_LIMIT)


def _pick(n, cands):
    for c in cands:
        if n % c == 0:
            return c
    raise ValueError(f"no tile for {n}")


def _head_of_lane(width, group):
    return lax.broadcasted_iota(jnp.int32, (1, width), 1) // group


def _stack_heads(x):
    head = _head_of_lane(MIX_W, HEAD_DIM)
    xf = x.astype(F32)
    return jnp.concatenate([jnp.where(head == h, xf, 0.0).astype(BF16) for h in range(N_HEADS)], axis=0)


def _expand_heads(cols):
    head = _head_of_lane(MIX_W, HEAD_DIM)
    out = cols[:, N_HEADS - 1:N_HEADS]
    for h in range(N_HEADS - 2, -1, -1):
        out = jnp.where(head <= h, cols[:, h:h + 1], out)
    return out


def _head_sum(x, ones_bd):
    hi = x.astype(BF16)
    lo = (x - hi.astype(F32)).astype(BF16)
    return _dot(hi, ones_bd) + _dot(lo, ones_bd)


def _rot(x_bf, cos, sin, perm):
    return x_bf.astype(F32) * cos + _dot(x_bf, perm) * sin


def _norm_modulate(x, g, is_ctx, mc_ref, ml_ref, shift_row, scale_row):
    shift = jnp.where(is_ctx, mc_ref[0, shift_row:shift_row + 1, :], ml_ref[0, shift_row:shift_row + 1, :])
    scale = jnp.where(is_ctx, mc_ref[0, scale_row:scale_row + 1, :], ml_ref[0, scale_row:scale_row + 1, :])
    gain = g * (1.0 + scale)
    return x * lax.rsqrt(jnp.mean(x * x, axis=-1, keepdims=True) + EPS) * gain + shift


def _adaln_kernel(c_ref, w_ref, b_ref, o_ref):
    c = c_ref[...]
    s = c * jax.nn.sigmoid(c)
    o_ref[0] = _dot(s, w_ref[0], precision=HIGHEST) + b_ref[0]


def _adaln(cond, w_ada, b_ada):
    n_l, d, d6 = w_ada.shape
    r = cond.shape[0]
    tn = 1024
    return pl.pallas_call(
        _adaln_kernel,
        grid=(n_l, d6 // tn),
        in_specs=[pl.BlockSpec((r, d), lambda l, j: (0, 0)),
                  pl.BlockSpec((1, d, tn), lambda l, j: (l, 0, j)),
                  pl.BlockSpec((1, 1, tn), lambda l, j: (l, 0, j))],
        out_specs=pl.BlockSpec((1, r, tn), lambda l, j: (l, 0, j)),
        out_shape=jax.ShapeDtypeStruct((n_l, r, d6), F32),
        compiler_params=_params("arbitrary", "arbitrary"),
        name="adaln",
    )(cond, w_ada, b_ada.reshape(n_l, 1, d6))


def _inproj_kernel(x_ref, ml_ref, mc_ref, g_ref, w_ref, wab_ref, wabc_ref, p_ref, ab_ref, abc_ref, xn_ref,
                   *, tm, rb, n_ctx):
    i = pl.program_id(1)
    j = pl.program_id(2)

    @pl.when(j == 0)
    def _():
        def blk(r, carry):
            r0 = pl.multiple_of(r * rb, rb)
            x = x_ref[0, pl.ds(r0, rb), :]
            hn = _norm_modulate(x, g_ref[...], i * tm + r0 < n_ctx, mc_ref, ml_ref, 0, 1)
            xn_ref[pl.ds(r0, rb), :] = hn.astype(BF16)
            return carry

        lax.fori_loop(0, tm // rb, blk, 0)
        ab_ref[0] = _dot_nt(wab_ref[...], xn_ref[...])
        abc_ref[0] = _dot(xn_ref[...], wabc_ref[...])

    p_ref[0] = _dot(xn_ref[...], w_ref[...]).astype(BF16)


def _inproj(xa, mod, g, w, wab, wabc, n_ctx):
    nb, nt, d = xa.shape
    tm = _pick(nt, (1408, 768, 384, 256, 128))
    tn = 1792
    kern = functools.partial(_inproj_kernel, tm=tm, rb=128, n_ctx=n_ctx)
    return pl.pallas_call(
        kern,
        grid=(nb, nt // tm, P_COLS // tn),
        in_specs=[pl.BlockSpec((1, tm, d), lambda b, i, j: (b, i, 0)),
                  pl.BlockSpec((1, 6, d), lambda b, i, j: (b, 0, 0)),
                  pl.BlockSpec((1, 6, d), lambda b, i, j: (nb, 0, 0)),
                  pl.BlockSpec((1, d), lambda b, i, j: (0, 0)),
                  pl.BlockSpec((d, tn), lambda b, i, j: (0, j)),
                  pl.BlockSpec((16, d), lambda b, i, j: (0, 0)),
                  pl.BlockSpec((d, 128), lambda b, i, j: (0, 0))],
        out_specs=[pl.BlockSpec((1, tm, tn), lambda b, i, j: (b, i, j)),
                   pl.BlockSpec((1, 16, tm), lambda b, i, j: (b, 0, i)),
                   pl.BlockSpec((1, tm, 128), lambda b, i, j: (b, i, 0))],
        out_shape=[jax.ShapeDtypeStruct((nb, nt, P_COLS), BF16),
                   jax.ShapeDtypeStruct((nb, 16, nt), F32),
                   jax.ShapeDtypeStruct((nb, nt, 128), F32)],
        scratch_shapes=[pltpu.VMEM((tm, d), BF16)],
        compiler_params=_params("arbitrary", "arbitrary", "arbitrary"),
        name="inproj",
    )(xa, mod, mod, g, w, wab, wabc)


def _bwd_chunk(t, ncc, nc):
    return jnp.where(t < ncc, ncc - 1 - t, nc - 1 - (t - ncc))


def _ret_state_kernel(pf_ref, pb_ref, cf_ref, sf_ref, cb_ref, sb_ref, perm_ref, kd_ref, cd_ref, bd_ref,
                      of_ref, ob_ref, st_f, st_b, *, cb):
    t = pl.program_id(1)

    @pl.when(t == 0)
    def _():
        st_f[...] = jnp.zeros_like(st_f)
        st_b[...] = jnp.zeros_like(st_b)

    def increments(p_ref, c_ref, s_ref, d):
        out = []
        for i in range(cb):
            r = slice(i * CHUNK, (i + 1) * CHUNK)
            kr = _rot(p_ref[0, r, MIX_W:2 * MIX_W], c_ref[r, :], s_ref[r, :], perm_ref[...]) * (HEAD_DIM ** -0.5)
            out.append(bd_ref[...] * _dot_tn((kr * kd_ref[d]).astype(BF16), p_ref[0, r, 2 * MIX_W:3 * MIX_W]))
        return out

    inc_f = increments(pf_ref, cf_ref, sf_ref, 0)
    inc_b = increments(pb_ref, cb_ref, sb_ref, 1)
    s = st_f[...]
    for i in range(cb):
        of_ref[0, i] = s.astype(BF16)
        s = cd_ref[0] * s + inc_f[i]
    st_f[...] = s
    s = st_b[...]
    for i in reversed(range(cb)):
        ob_ref[0, i] = s.astype(BF16)
        s = cd_ref[1] * s + inc_b[i]
    st_b[...] = s


def _gelu_tanh(x):
    return 0.5 * x * (1.0 + jnp.tanh(math.sqrt(2.0 / math.pi) * (x + 0.044715 * (x * x * x))))


def _mix_out_kernel(p_ref, pg_ref, c_ref, s_ref, sf_ref, sb_ref, perm_ref, dm_ref, qd_ref, ones_ref,
                    ng_ref, wg_ref, bg_ref, y_ref, ysg_ref, *, cb):
    chunks = range(cb)
    rows = [slice(i * CHUNK, (i + 1) * CHUNK) for i in chunks]
    perm, dm, ones_bd = perm_ref[...], dm_ref[...], ones_ref[...]
    p = [p_ref[0, r, :] for r in rows]
    cos = [c_ref[r, :] for r in rows]
    sin = [s_ref[r, :] for r in rows]
    qr = [_rot(p[i][:, 0:MIX_W], cos[i], sin[i], perm) for i in chunks]
    kr = [_rot(p[i][:, MIX_W:2 * MIX_W], cos[i], sin[i], perm) * (HEAD_DIM ** -0.5) for i in chunks]
    z = [_gelu_tanh(pg_ref[0, r, :].astype(F32)) for r in rows]
    vg = [x[:, MIX_W:] for x in z]
    mu_g = [jnp.mean(x, axis=-1, keepdims=True) for x in vg]
    vgc = [x - m for x, m in zip(vg, mu_g)]
    var_g = [jnp.mean(x * x, axis=-1, keepdims=True) for x in vgc]
    vn = [x * lax.rsqrt(s + EPS) * ng_ref[...] for x, s in zip(vgc, var_g)]
    sc = [_dot_nt(qr[i].astype(BF16), _stack_heads(kr[i])) * dm for i in chunks]
    mixed = [_dot(wg_ref[...], _stack_heads(x)) for x in vn]
    o = [_dot(sc[i].astype(BF16), _stack_heads(p[i][:, 2 * MIX_W:3 * MIX_W])) for i in chunks]
    qs = [jnp.concatenate([(qr[i] * qd_ref[0]).astype(BF16), (qr[i] * qd_ref[1]).astype(BF16)], axis=1)
          for i in chunks]
    ss = [jnp.concatenate([sf_ref[0, i], sb_ref[0, i]], axis=0) for i in chunks]
    o = [o[i] + _dot(qs[i], ss[i]) for i in chunks]
    for i in chunks:
        ysg_ref[0, rows[i], :] = (z[i][:, :MIX_W] * (mixed[i] + bg_ref[...])).astype(BF16)
    mu = [_head_sum(x, ones_bd) * (1.0 / HEAD_DIM) for x in o]
    oc = [x - m for x, m in zip(o, mu)]
    var = [_head_sum(x * x, ones_bd) * (1.0 / HEAD_DIM) for x in oc]
    for i in chunks:
        g = p[i][:, 3 * MIX_W:4 * MIX_W].astype(F32)
        y_ref[0, rows[i], :] = (oc[i] * lax.rsqrt(var[i] + EPS) * (g * jax.nn.sigmoid(g))).astype(BF16)


def _retention_and_sgate(p, cos, sin, perm, tabs, sg_ng, sg_w, sg_bias, ncc):
    nb, nt, _ = p.shape
    nc = nt // CHUNK
    kd, cd, qd, dm, bd, ones_bd = tabs
    cb = 2
    assert nc % cb == 0 and ncc % cb == 0
    nblk, ncb = nc // cb, ncc // cb
    fwd = lambda b, t: (b, t, 0)
    bwd = lambda b, t: (b, _bwd_chunk(t, ncb, nblk), 0)
    tab_f = lambda b, t: (t, 0)
    tab_b = lambda b, t: (_bwd_chunk(t, ncb, nblk), 0)
    c2 = lambda b, t: (0, 0)
    c3 = lambda b, t: (0, 0, 0)
    st_shape = jax.ShapeDtypeStruct((nb, nc, MIX_W, MIX_W), BF16)
    st_f, st_b = pl.pallas_call(
        functools.partial(_ret_state_kernel, cb=cb),
        grid=(nb, nblk),
        in_specs=[pl.BlockSpec((1, cb * CHUNK, RET_COLS), fwd),
                  pl.BlockSpec((1, cb * CHUNK, RET_COLS), bwd),
                  pl.BlockSpec((cb * CHUNK, MIX_W), tab_f), pl.BlockSpec((cb * CHUNK, MIX_W), tab_f),
                  pl.BlockSpec((cb * CHUNK, MIX_W), tab_b), pl.BlockSpec((cb * CHUNK, MIX_W), tab_b),
                  pl.BlockSpec((MIX_W, MIX_W), c2),
                  pl.BlockSpec((2, CHUNK, MIX_W), c3),
                  pl.BlockSpec((2, 1, MIX_W), c3),
                  pl.BlockSpec((MIX_W, MIX_W), c2)],
        out_specs=[pl.BlockSpec((1, cb, MIX_W, MIX_W), lambda b, t: (b, t, 0, 0)),
                   pl.BlockSpec((1, cb, MIX_W, MIX_W), lambda b, t: (b, _bwd_chunk(t, ncb, nblk), 0, 0))],
        out_shape=[st_shape, st_shape],
        scratch_shapes=[pltpu.VMEM((MIX_W, MIX_W), F32), pltpu.VMEM((MIX_W, MIX_W), F32)],
        compiler_params=_params("arbitrary", "arbitrary"),
        name="ret_state",
    )(p, p, cos, sin, cos, sin, perm, kd, cd, bd)
    blk = lambda b, t: (b, t, 0)
    y_shape = jax.ShapeDtypeStruct((nb, nt, MIX_W), BF16)
    return pl.pallas_call(
        functools.partial(_mix_out_kernel, cb=cb),
        grid=(nb, nc // cb),
        in_specs=[pl.BlockSpec((1, cb * CHUNK, RET_COLS), blk),
                  pl.BlockSpec((1, cb * CHUNK, SG_COLS), lambda b, t: (b, t, P_SG // SG_COLS)),
                  pl.BlockSpec((cb * CHUNK, MIX_W), tab_f), pl.BlockSpec((cb * CHUNK, MIX_W), tab_f),
                  pl.BlockSpec((1, cb, MIX_W, MIX_W), lambda b, t: (b, t, 0, 0)),
                  pl.BlockSpec((1, cb, MIX_W, MIX_W), lambda b, t: (b, t, 0, 0)),
                  pl.BlockSpec((MIX_W, MIX_W), c2),
                  pl.BlockSpec((CHUNK, N_HEADS * CHUNK), c2),
                  pl.BlockSpec((2, CHUNK, MIX_W), c3),
                  pl.BlockSpec((MIX_W, MIX_W), c2),
                  pl.BlockSpec((1, MIX_W), c2),
                  pl.BlockSpec((CHUNK, N_HEADS * CHUNK), c2),
                  pl.BlockSpec((CHUNK, MIX_W), c2)],
        out_specs=[pl.BlockSpec((1, cb * CHUNK, MIX_W), blk), pl.BlockSpec((1, cb * CHUNK, MIX_W), blk)],
        out_shape=[y_shape, y_shape],
        compiler_params=_params("arbitrary", "arbitrary"),
        name="mix_out",
    )(p, p, cos, sin, st_f, st_b, perm, dm, qd, ones_bd, sg_ng, sg_w, sg_bias)


def _softplus(a):
    return jnp.maximum(a, 0.0) + jnp.log1p(jnp.exp(-jnp.abs(a)))


def _dn_prep_kernel(pc_ref, pp_ref, pn_ref, ab_ref, abc_ref, cw_ref, na_ref, dtb_ref, nar_ref, dtbr_ref, ones_ref,
                    qkv_ref, gb_ref, gbc_ref, xe_ref, *, ncc, nc):
    t = pl.program_id(1)
    w3 = 3 * MIX_W
    prev_ok = jnp.where((t != 0) & (t != ncc), 1.0, 0.0)
    next_ok = jnp.where((t != ncc - 1) & (t != nc - 1), 1.0, 0.0)
    tail = pp_ref[0, CHUNK - 16:CHUNK, 0:w3].astype(F32)
    head = pn_ref[0, 0:16, 0:w3].astype(F32)
    xe_ref[0:8, :] = tail[8:16, :] * prev_ok
    xe_ref[8:8 + CHUNK, :] = pc_ref[0, :, 0:w3].astype(F32)
    xe_ref[8 + CHUNK:16 + CHUNK, :] = head[0:8, :] * next_ok
    pad = CONV_W // 2
    y = xe_ref[8 - pad:8 - pad + CHUNK, :] * cw_ref[0:1, :]
    for i in range(1, CONV_W):
        y = y + xe_ref[8 - pad + i:8 - pad + i + CHUNK, :] * cw_ref[i:i + 1, :]
    y = y * jax.nn.sigmoid(y)
    q = y[:, 0:MIX_W]
    k = y[:, MIX_W:2 * MIX_W]
    v = y[:, 2 * MIX_W:w3]
    ones_bd = ones_ref[...]
    qn = q * lax.rsqrt(_head_sum(q * q, ones_bd) + EPS) * (HEAD_DIM ** -0.5)
    kn = k * lax.rsqrt(_head_sum(k * k, ones_bd) + EPS)
    qkv_ref[0, :, 0:MIX_W] = qn.astype(BF16)
    qkv_ref[0, :, MIX_W:2 * MIX_W] = kn.astype(BF16)
    qkv_ref[0, :, 2 * MIX_W:w3] = v.astype(BF16)
    ab = ab_ref[0]
    gb_ref[0, 0:8, :] = na_ref[...] * _softplus(ab[0:8, :] + dtb_ref[...])
    gb_ref[0, 8:16, :] = jax.nn.sigmoid(ab[8:16, :])
    abc = abc_ref[0]
    lane = lax.broadcasted_iota(jnp.int32, (1, 128), 1)
    g_c = nar_ref[...] * _softplus(abc + dtbr_ref[...])
    gbc_ref[0] = jnp.where(lane < 8, g_c, jnp.where(lane < 16, jax.nn.sigmoid(abc), 0.0))


def _split3(x):
    hi = x.astype(BF16)
    r = x - hi.astype(F32)
    mid = r.astype(BF16)
    lo = (r - mid.astype(F32)).astype(BF16)
    return hi, mid, lo


def _tri_inverse(mats, ii, jj):
    eye = jnp.where(ii == jj, 1.0, 0.0)
    nd = [jnp.where((ii // 16) == (jj // 16), n, 0.0) for n in mats]
    p1 = [_mm(x, x) for x in nd]
    m = [eye - x for x in nd]
    p2 = [_mm(x, x) for x in p1]
    m = [x + _mm(x, y) for x, y in zip(m, p1)]
    p3 = [_mm(x, x) for x in p2]
    m = [x + _mm(x, y) for x, y in zip(m, p2)]
    m = [x + _mm(x, y) for x, y in zip(m, p3)]
    for lvl in (16, 32, 64):
        off_mask = ((ii // (2 * lvl)) == (jj // (2 * lvl))) & ((ii // lvl) != (jj // lvl))
        t = [_mm(jnp.where(off_mask, n, 0.0), x) for n, x in zip(mats, m)]
        m = [x - _mm(x, y) for x, y in zip(m, t)]
    return m


def _dn_pre(qkv, g, gbc, d, lower):
    c = CHUNK
    qn = qkv[:, 0:MIX_W]
    kn = qkv[:, MIX_W:2 * MIX_W]
    v = qkv[:, 2 * MIX_W:3 * MIX_W]
    ii = lax.broadcasted_iota(jnp.int32, (c, c), 0)
    jj = lax.broadcasted_iota(jnp.int32, (c, c), 1)
    incl = (ii >= jj) if lower else (ii <= jj)
    tri = jnp.where(incl, 1.0, 0.0).astype(BF16)
    g_row = sum(_dot_nt(part, tri) for part in _split3(g))[N_HEADS * d:N_HEADS * (d + 1), :]
    cum = sum(_dot(tri, part) for part in _split3(gbc))
    g_col = cum[:, N_HEADS * d:N_HEADS * (d + 1)]
    b_col = gbc[:, 2 * N_HEADS + N_HEADS * d:2 * N_HEADS + N_HEADS * (d + 1)]
    g_cols4 = jnp.concatenate([jnp.broadcast_to(g_col[:, h:h + 1], (c, c)) for h in range(N_HEADS)], axis=1)
    b_cols4 = jnp.concatenate([jnp.broadcast_to(b_col[:, h:h + 1], (c, c)) for h in range(N_HEADS)], axis=1)
    g_rows4 = jnp.concatenate([g_row[h:h + 1, :] for h in range(N_HEADS)], axis=1)
    incl4 = jnp.concatenate([incl] * N_HEADS, axis=1)
    diag4 = jnp.concatenate([ii == jj] * N_HEADS, axis=1)
    decay = jnp.where(incl4, jnp.exp(jnp.where(incl4, g_cols4 - g_rows4, 0.0)), 0.0)
    kstack = _stack_heads(kn)
    kk = _dot_nt(kn, kstack)
    qk = _dot_nt(qn, kstack)
    n_mat = jnp.where(diag4, 0.0, decay * kk * b_cols4)
    attn = (decay * qk).astype(BF16)
    g256 = _expand_heads(g_col)
    eg256 = jnp.exp(g256)
    b256 = _expand_heads(b_col)
    vb = v.astype(F32) * b256
    kbg = kn.astype(F32) * b256 * eg256
    rhs = jnp.concatenate([_stack_heads(vb), _stack_heads(kbg)], axis=1)
    g_last = g256[c - 1:c, :] if lower else g256[0:1, :]
    kdec = (kn.astype(F32) * jnp.exp(g_last - g256)).astype(BF16)
    n_heads = [n_mat[:, h * c:(h + 1) * c] for h in range(N_HEADS)]
    return n_heads, dict(qn=qn, attn=attn, rhs=rhs, eg=eg256, kdec=kdec, sdec=jnp.exp(g_last))


def _dn_post(z, s_prev, bd):
    s_bf = s_prev.astype(BF16)
    w = z["u"] - _dot(z["wk"], s_bf)
    o = z["eg"] * _dot(z["qn"], s_bf) + _dot(z["attn"], _stack_heads(w))
    s_next = z["sdec"] * s_prev + bd * _dot_tn(z["kdec"], w.astype(BF16))
    return o, s_next


def _dn_scan_kernel(qf_ref, qb_ref, gf_ref, gb_ref, gcf_ref, gcb_ref, bd_ref, of_ref, ob_ref, st_f, st_b, *, cb):
    t = pl.program_id(1)

    @pl.when(t == 0)
    def _():
        st_f[...] = jnp.zeros_like(st_f)
        st_b[...] = jnp.zeros_like(st_b)

    bd = bd_ref[...]
    rows = [slice(i * CHUNK, (i + 1) * CHUNK) for i in range(cb)]
    mats, pres = [], []
    for d, (q_ref, g_ref, gc_ref) in enumerate(((qf_ref, gf_ref, gcf_ref), (qb_ref, gb_ref, gcb_ref))):
        for r in rows:
            n_heads, pre = _dn_pre(q_ref[0, r, :], g_ref[0, :, r], gc_ref[0, r, :], d, d == 0)
            mats += n_heads
            pres.append(pre)
    ii = lax.broadcasted_iota(jnp.int32, (CHUNK, CHUNK), 0)
    jj = lax.broadcasted_iota(jnp.int32, (CHUNK, CHUNK), 1)
    inv = _tri_inverse(mats, ii, jj)
    for n, pre in enumerate(pres):
        a_inv = jnp.concatenate(inv[N_HEADS * n:N_HEADS * (n + 1)], axis=1).astype(BF16)
        uw = _dot(a_inv, pre["rhs"])
        pre["u"] = uw[:, 0:MIX_W]
        pre["wk"] = uw[:, MIX_W:2 * MIX_W].astype(BF16)
    s_f, s_b = st_f[...], st_b[...]
    for k in range(cb):
        o, s_f = _dn_post(pres[k], s_f, bd)
        of_ref[0, rows[k], :] = o
        o, s_b = _dn_post(pres[cb + cb - 1 - k], s_b, bd)
        ob_ref[0, rows[cb - 1 - k], :] = o
    st_f[...] = s_f
    st_b[...] = s_b


def _deltanet(p, ab_t, ab_c, conv_w, neg_a, dtb, bd, ones_bd, ncc):
    nb, nt, _ = p.shape
    nc = nt // CHUNK
    w3 = 3 * MIX_W
    c2 = lambda b, t: (0, 0)
    dn_blk = P_DN // RET_COLS
    pad_lanes = lambda col: jnp.concatenate([col.reshape(1, -1), jnp.zeros((1, 128 - col.size), F32)], axis=1)
    qkv, gbeta, gbeta_c = pl.pallas_call(
        functools.partial(_dn_prep_kernel, ncc=ncc, nc=nc),
        grid=(nb, nc),
        in_specs=[pl.BlockSpec((1, CHUNK, 4 * MIX_W), lambda b, t: (b, t, dn_blk)),
                  pl.BlockSpec((1, CHUNK, 4 * MIX_W), lambda b, t: (b, jnp.maximum(t - 1, 0), dn_blk)),
                  pl.BlockSpec((1, CHUNK, 4 * MIX_W), lambda b, t: (b, jnp.minimum(t + 1, nc - 1), dn_blk)),
                  pl.BlockSpec((1, 16, CHUNK), lambda b, t: (b, 0, t)),
                  pl.BlockSpec((1, CHUNK, 128), lambda b, t: (b, t, 0)),
                  pl.BlockSpec((8, w3), c2),
                  pl.BlockSpec((8, 1), c2),
                  pl.BlockSpec((8, 1), c2),
                  pl.BlockSpec((1, 128), c2),
                  pl.BlockSpec((1, 128), c2),
                  pl.BlockSpec((MIX_W, MIX_W), c2)],
        out_specs=[pl.BlockSpec((1, CHUNK, w3), lambda b, t: (b, t, 0)),
                   pl.BlockSpec((1, 16, CHUNK), lambda b, t: (b, 0, t)),
                   pl.BlockSpec((1, CHUNK, 128), lambda b, t: (b, t, 0))],
        out_shape=[jax.ShapeDtypeStruct((nb, nt, w3), BF16),
                   jax.ShapeDtypeStruct((nb, 16, nt), F32),
                   jax.ShapeDtypeStruct((nb, nt, 128), F32)],
        scratch_shapes=[pltpu.VMEM((CHUNK + 16, w3), F32)],
        compiler_params=_params("arbitrary", "arbitrary"),
        name="dn_prep",
    )(p, p, p, ab_t, ab_c, conv_w, neg_a, dtb, pad_lanes(neg_a), pad_lanes(dtb), ones_bd)
    cb = 2
    assert nc % cb == 0 and ncc % cb == 0
    rows = cb * CHUNK
    cur_b = lambda t: _bwd_chunk(t, ncc // cb, nc // cb)
    o_shape = jax.ShapeDtypeStruct((nb, nt, MIX_W), F32)
    return pl.pallas_call(
        functools.partial(_dn_scan_kernel, cb=cb),
        grid=(nb, nc // cb),
        in_specs=[pl.BlockSpec((1, rows, w3), lambda b, t: (b, t, 0)),
                  pl.BlockSpec((1, rows, w3), lambda b, t: (b, cur_b(t), 0)),
                  pl.BlockSpec((1, 16, rows), lambda b, t: (b, 0, t)),
                  pl.BlockSpec((1, 16, rows), lambda b, t: (b, 0, cur_b(t))),
                  pl.BlockSpec((1, rows, 128), lambda b, t: (b, t, 0)),
                  pl.BlockSpec((1, rows, 128), lambda b, t: (b, cur_b(t), 0)),
                  pl.BlockSpec((MIX_W, MIX_W), c2)],
        out_specs=[pl.BlockSpec((1, rows, MIX_W), lambda b, t: (b, t, 0)),
                   pl.BlockSpec((1, rows, MIX_W), lambda b, t: (b, cur_b(t), 0))],
        out_shape=[o_shape, o_shape],
        scratch_shapes=[pltpu.VMEM((MIX_W, MIX_W), F32), pltpu.VMEM((MIX_W, MIX_W), F32)],
        compiler_params=_params("arbitrary", "arbitrary"),
        name="dn_scan",
    )(qkv, qkv, gbeta, gbeta, gbeta_c, gbeta_c, bd)


QK_W = 256


VT_ROWS = 144


def _mla_prep_kernel(p_ref, c_ref, s_ref, perm_ref, qg_ref, kg_ref, wqn_ref, wqr_ref, wa_ref, selq_ref, selc_ref,
                     selr_ref, selv_ref, one_ref, qt_ref, kv_ref, vt_ref, *, scale):
    p = p_ref[0]
    cos, sin, perm = c_ref[...], s_ref[...], perm_ref[...]
    cq = p[:, 0:Q_LORA].astype(F32)
    cqn = (cq * lax.rsqrt(jnp.mean(cq * cq, axis=-1, keepdims=True) + EPS) * qg_ref[...]).astype(BF16)
    q_nope = _dot(cqn, wqn_ref[...]).astype(BF16)
    q_rope = _dot(cqn, wqr_ref[...]).astype(BF16)
    q_rot = (_rot(q_rope, cos, sin, perm) * scale).astype(BF16)
    q_nope_s = (q_nope.astype(F32) * scale).astype(BF16)
    for h in range(N_HEADS):
        qt_ref[0, h] = (_dot_nt(wa_ref[h], q_nope_s) + _dot_nt(selq_ref[h], q_rot)).astype(BF16)
    ckv = p[:, Q_LORA:Q_LORA + KV_LORA].astype(F32)
    ckvn = (ckv * lax.rsqrt(jnp.mean(ckv * ckv, axis=-1, keepdims=True) + EPS) * kg_ref[...]).astype(BF16)
    kr = p[:, Q_LORA + KV_LORA:MLA_PAD]
    kr_rot = _rot(kr, cos, sin, perm).astype(BF16)
    kv_ref[0] = (_dot(ckvn, selc_ref[...]) + _dot(kr_rot, selr_ref[...])).astype(BF16)
    vt_ref[0] = (_dot_nt(selv_ref[...], ckvn) + one_ref[...]).astype(BF16)


def _mla_attn_kernel(qt_ref, kv_ref, vt_ref, wuv_ref, y_ref, m_ref, acc_ref, s_ref, *, tk, n_ctx, nt, latent):
    heads = range(N_HEADS)
    m_ref[...] = jnp.full_like(m_ref, -jnp.inf)
    acc_ref[...] = jnp.zeros_like(acc_ref)

    def scores(j0, size, slot):
        k = kv_ref[0, pl.ds(j0, size), :]
        for h in heads:
            s_ref[slot, h, 0:size, :] = _dot(k, qt_ref[0, h])

    def softmax_pv(j0, size, slot):
        vt = vt_ref[0, :, pl.ds(j0, size)]
        s = [s_ref[slot, h, 0:size, :] for h in heads]
        m_old = [m_ref[h] for h in heads]
        m_new = [jnp.maximum(m_old[h], jnp.max(s[h], axis=0, keepdims=True)) for h in heads]
        pr = [jnp.exp2(s[h] - m_new[h]).astype(BF16) for h in heads]
        pv = [_dot(vt, pr[h]) for h in heads]
        for h in heads:
            acc_ref[h] = jnp.exp2(m_old[h] - m_new[h]) * acc_ref[h] + pv[h]
            m_ref[h] = m_new[h]

    scores(0, n_ctx, 0)
    if not latent:
        softmax_pv(0, n_ctx, 0)
    else:
        n_tiles = (nt - n_ctx) // tk
        last = n_ctx + (n_tiles - 1) * tk
        scores(n_ctx, tk, 1)
        softmax_pv(0, n_ctx, 0)

        def body(jj, carry):
            t0 = pl.multiple_of(n_ctx + 2 * jj * tk, 256)
            t1 = pl.multiple_of(jnp.minimum(t0 + tk, last), 256)
            t2 = pl.multiple_of(jnp.minimum(t0 + 2 * tk, last), 256)
            scores(t1, tk, 0)
            softmax_pv(t0, tk, 1)
            scores(t2, tk, 1)
            softmax_pv(t1, tk, 0)
            return carry

        lax.fori_loop(0, n_tiles // 2, body, 0)
        if n_tiles % 2:
            softmax_pv(last, tk, 1)

    y = None
    for h in range(N_HEADS):
        acc = acc_ref[h]
        o = (acc[0:KV_LORA, :] / acc[KV_LORA:KV_LORA + 1, :]).astype(BF16)
        term = _dot_tn(o, wuv_ref[h])
        y = term if y is None else y + term
    y_ref[0] = y.astype(BF16)


def _mla(p, cos, sin, perm, qg, kg, wqn, wqr, wa, selq, selc, selr, selv, one_col, wuv, n_ctx, ctx_out):
    nb, nt, _ = p.shape
    n_lat = nt - n_ctx
    tm = 256
    assert n_ctx % tm == 0
    n_tiles, ctx_tiles = nt // tm, n_ctx // tm
    scale = (NOPE_DIM + ROPE_DIM) ** -0.5 * math.log2(math.e)
    c2 = lambda b, i: (0, 0)
    c3 = lambda b, i: (0, 0, 0)
    q_pos = lambda i: (i + n_tiles - ctx_tiles) % n_tiles
    qt, kv, vt = pl.pallas_call(
        functools.partial(_mla_prep_kernel, scale=scale),
        grid=(nb, nt // tm),
        in_specs=[pl.BlockSpec((1, tm, MLA_PAD), lambda b, i: (b, i, P_MLA // MLA_PAD)),
                  pl.BlockSpec((tm, 128), lambda b, i: (i, 0)),
                  pl.BlockSpec((tm, 128), lambda b, i: (i, 0)),
                  pl.BlockSpec((128, 128), c2),
                  pl.BlockSpec((1, Q_LORA), c2),
                  pl.BlockSpec((1, KV_LORA), c2),
                  pl.BlockSpec((Q_LORA, N_HEADS * NOPE_DIM), c2),
                  pl.BlockSpec((Q_LORA, N_HEADS * ROPE_DIM), c2),
                  pl.BlockSpec((N_HEADS, QK_W, N_HEADS * NOPE_DIM), c3),
                  pl.BlockSpec((N_HEADS, QK_W, N_HEADS * ROPE_DIM), c3),
                  pl.BlockSpec((KV_LORA, QK_W), c2),
                  pl.BlockSpec((128, QK_W), c2),
                  pl.BlockSpec((VT_ROWS, KV_LORA), c2),
                  pl.BlockSpec((VT_ROWS, 1), c2)],
        out_specs=[pl.BlockSpec((1, N_HEADS, QK_W, tm), lambda b, i: (b, 0, 0, q_pos(i))),
                   pl.BlockSpec((1, tm, QK_W), lambda b, i: (b, i, 0)),
                   pl.BlockSpec((1, VT_ROWS, tm), lambda b, i: (b, 0, i))],
        out_shape=[jax.ShapeDtypeStruct((nb, N_HEADS, QK_W, nt), BF16),
                   jax.ShapeDtypeStruct((nb, nt, QK_W), BF16),
                   jax.ShapeDtypeStruct((nb, VT_ROWS, nt), BF16)],
        compiler_params=_params("arbitrary", "arbitrary"),
        name="mla_prep",
    )(p, cos, sin, perm, qg, kg, wqn, wqr, wa, selq, selc, selr, selv, one_col)
    tk = _pick(n_lat, (512, 256))

    def attend(tq, first_col, n_q, latent):
        return pl.pallas_call(
            functools.partial(_mla_attn_kernel, tk=tk, n_ctx=n_ctx, nt=nt, latent=latent),
            grid=(nb, n_q // tq),
            in_specs=[pl.BlockSpec((1, N_HEADS, QK_W, tq), lambda b, i: (b, 0, 0, i + first_col // tq)),
                      pl.BlockSpec((1, nt, QK_W), lambda b, i: (b, 0, 0)),
                      pl.BlockSpec((1, VT_ROWS, nt), lambda b, i: (b, 0, 0)),
                      pl.BlockSpec((N_HEADS, KV_LORA, MIX_W), c3)],
            out_specs=pl.BlockSpec((1, tq, MIX_W), lambda b, i: (b, i, 0)),
            out_shape=jax.ShapeDtypeStruct((nb, n_q, MIX_W), BF16),
            scratch_shapes=[pltpu.VMEM((N_HEADS, 1, tq), F32), pltpu.VMEM((N_HEADS, VT_ROWS, tq), F32),
                            pltpu.VMEM((2, N_HEADS, max(tk, n_ctx) if latent else n_ctx, tq), F32)],
            compiler_params=_params("arbitrary", "arbitrary"),
            name="mla_attn" if latent else "mla_attn_ctx",
        )(qt, kv, vt, wuv)

    y_lat = attend(_pick(n_lat, (512, 256)), 0, n_lat, True)
    y_ctx = attend(tm, n_lat, n_ctx, False) if ctx_out else None
    return y_lat, y_ctx


def _merge_kernel(x_ref, yr_ref, ys_ref, of_ref, ob_ref, ym_ref, z_ref, g0_ref, g1_ref, g2_ref, g3_ref,
                  wb_ref, wo_ref, ng_ref, gp_ref, ml_ref, mc_ref, ones_ref, o_ref, *, tm, n_ctx, row0):
    i = pl.program_id(1)
    od = of_ref[0] + ob_ref[0]
    ms = _head_sum(od * od, ones_ref[...]) * (1.0 / HEAD_DIM)
    z = z_ref[0].astype(F32)
    ydn = (od * lax.rsqrt(ms + EPS) * ng_ref[...]) * (z * jax.nn.sigmoid(z))
    ys = (yr_ref[0], ys_ref[0], ydn.astype(BF16), ym_ref[0])
    gates = (g0_ref, g1_ref, g2_ref, g3_ref)
    acc = None
    for b in range(N_BRANCH):
        term = jax.nn.sigmoid(gates[b][0].astype(F32)) * _dot(ys[b], wb_ref[b])
        acc = term if acc is None else acc + term
    y = _dot(acc.astype(BF16), wo_ref[...])
    r = y * lax.rsqrt(jnp.mean(y * y, axis=-1, keepdims=True) + EPS) * gp_ref[...]
    rows = lax.broadcasted_iota(jnp.int32, (tm, 1), 0) + (row0 + i * tm)
    gate = jnp.where(rows < n_ctx, mc_ref[0, 2:3, :], ml_ref[0, 2:3, :])
    o_ref[0] = x_ref[0] + gate * r


def _merge(xa, y_ret, y_sg, o_f, o_b, y_mla, p, wb, wo, ng, gp, mod, ones_bd, n_ctx, row0):
    nb, nt, d = xa.shape
    n_rows = nt - row0
    tm = _pick(n_rows, (768, 512, 384, 256, 128))
    c2 = lambda b, i: (0, 0)
    if row0 == 0:
        def window(width, col):
            return pl.BlockSpec((1, tm, width), lambda b, i: (b, i, col // width))
    else:
        def window(width, col):
            return pl.BlockSpec((pl.Element(1), pl.Element(tm), pl.Element(width)),
                                lambda b, i: (b, pl.multiple_of(row0 + i * tm, 128), col))
    y_spec = window(MIX_W, 0)
    assert y_mla.shape[1] == n_rows
    mla_spec = pl.BlockSpec((1, tm, MIX_W), lambda b, i: (b, i, 0))
    gate_specs = [window(d, P_GATE + k * d) for k in range(N_BRANCH)]
    return pl.pallas_call(
        functools.partial(_merge_kernel, tm=tm, n_ctx=n_ctx, row0=row0),
        grid=(nb, n_rows // tm),
        in_specs=[window(d, 0), y_spec, y_spec, y_spec, y_spec, mla_spec,
                  window(MIX_W, P_DN + 3 * MIX_W),
                  *gate_specs,
                  pl.BlockSpec((N_BRANCH, MIX_W, d), lambda b, i: (0, 0, 0)),
                  pl.BlockSpec((d, d), c2),
                  pl.BlockSpec((1, MIX_W), c2),
                  pl.BlockSpec((1, d), c2),
                  pl.BlockSpec((1, 6, d), lambda b, i: (b, 0, 0)),
                  pl.BlockSpec((1, 6, d), lambda b, i: (nb, 0, 0)),
                  pl.BlockSpec((MIX_W, MIX_W), c2)],
        out_specs=pl.BlockSpec((1, tm, d), lambda b, i: (b, i, 0)),
        out_shape=jax.ShapeDtypeStruct((nb, n_rows, d), F32),
        compiler_params=_params("arbitrary", "arbitrary"),
        name="merge",
    )(xa, y_ret, y_sg, o_f, o_b, y_mla, p, p, p, p, p, wb, wo, ng, gp, mod, mod, ones_bd)


def _route(sel, aff):
    rows = [sel[e:e + 1, :] for e in range(N_EXPERTS)]
    pairs = [(a, b) for a in range(EXPERTS_PER_GROUP) for b in range(a + 1, EXPERTS_PER_GROUP)]
    grp_score, grp_pair = [], []
    for g in range(N_GROUPS):
        base = g * EXPERTS_PER_GROUP
        best = rows[base + pairs[0][0]] + rows[base + pairs[0][1]]
        best_p = jnp.zeros_like(best, dtype=jnp.int32)
        for pi in range(1, len(pairs)):
            s = rows[base + pairs[pi][0]] + rows[base + pairs[pi][1]]
            take = s > best
            best = jnp.where(take, s, best)
            best_p = jnp.where(take, pi, best_p)
        grp_score.append(best)
        grp_pair.append(best_p)
    top = grp_score[0]
    top_g = jnp.zeros_like(grp_pair[0])
    top_p = grp_pair[0]
    for g in range(1, N_GROUPS):
        take = grp_score[g] > top
        top = jnp.where(take, grp_score[g], top)
        top_g = jnp.where(take, g, top_g)
        top_p = jnp.where(take, grp_pair[g], top_p)
    picked = []
    for e in range(N_EXPERTS):
        g, k = divmod(e, EXPERTS_PER_GROUP)
        in_pair = None
        for pi, (a, b) in enumerate(pairs):
            if k in (a, b):
                hit = top_p == pi
                in_pair = hit if in_pair is None else (in_pair | hit)
        picked.append(jnp.where((top_g == g) & in_pair, aff[e:e + 1, :], 0.0))
    denom = picked[0]
    for e in range(1, N_EXPERTS):
        denom = denom + picked[e]
    return [pk / denom for pk in picked]


def _swiglu(hn, w1, w3, w2, scale):
    a = _dot(hn, w1.astype(BF16))
    h = (a * jax.nn.sigmoid(a)) * _dot(hn, w3.astype(BF16))
    if scale is not None:
        h = h * scale
    return _dot(h.astype(BF16), w2.astype(BF16))


def _moe_kernel(x_ref, ml_ref, mc_ref, g2_ref, gp_ref, rw_ref, rb_ref, ws1_ref, ws3_ref, ws2_ref,
                w1_ref, w3_ref, w2_ref, o_ref, hn_ref, comb_t_ref, comb_ref, acc_ref, *, tm, rb, n_ctx):
    i = pl.program_id(1)
    e = pl.program_id(2)

    @pl.when(e == 0)
    def _():
        def blk(r, carry):
            r0 = pl.multiple_of(r * rb, rb)
            x = x_ref[0, pl.ds(r0, rb), :]
            hn = _norm_modulate(x, g2_ref[...], i * tm + r0 < n_ctx, mc_ref, ml_ref, 3, 4)
            hn_ref[pl.ds(r0, rb), :] = hn.astype(BF16)
            return carry

        lax.fori_loop(0, tm // rb, blk, 0)
        hn = hn_ref[...]
        aff = jax.nn.sigmoid(_dot_nt(rw_ref[...], hn))
        comb = _route(aff + rb_ref[...], aff)
        comb_t_ref[...] = jnp.zeros_like(comb_t_ref)
        for k in range(N_EXPERTS):
            comb_t_ref[k:k + 1, :] = comb[k]
        comb_ref[...] = comb_t_ref[...].T
        acc_ref[...] = _swiglu(hn, ws1_ref[0], ws3_ref[0], ws2_ref[0], None)

    @pl.when(e > 0)
    def _():
        lane = lax.broadcasted_iota(jnp.int32, (1, 128), 1)
        comb = comb_ref[...]
        hn = hn_ref[...]
        first = 2 * (e - 1)
        y = None
        for k in range(2):
            c_k = jnp.sum(jnp.where(lane == first + k, comb, 0.0), axis=-1, keepdims=True)
            term = _swiglu(hn, w1_ref[0, k], w3_ref[0, k], w2_ref[0, k], c_k)
            y = term if y is None else y + term
        acc_ref[...] += y

    @pl.when(e == pl.num_programs(2) - 1)
    def _():
        y = acc_ref[...]
        r = y * lax.rsqrt(jnp.mean(y * y, axis=-1, keepdims=True) + EPS) * gp_ref[...]
        rows = lax.broadcasted_iota(jnp.int32, (tm, 1), 0) + i * tm
        gate = jnp.where(rows < n_ctx, mc_ref[0, 5:6, :], ml_ref[0, 5:6, :])
        o_ref[0] = x_ref[0] + gate * r


def _moe(xa, mod, g2, gp, rw_t, rbias, ws1, ws3, ws2, w1, w3, w2, layer, n_ctx):
    nb, nt, d = xa.shape
    tm = _pick(nt, (1024, 768, 512, 384, 256, 128))
    n_pairs = w1.shape[1] // 2
    row = lambda b, i, e: (b, i, 0)
    c2 = lambda b, i, e: (0, 0)
    shared_blk = lambda b, i, e: (layer, 0, 0)
    pair_blk = lambda b, i, e: (layer, jnp.maximum(e - 1, 0), 0, 0)
    return pl.pallas_call(
        functools.partial(_moe_kernel, tm=tm, rb=128, n_ctx=n_ctx),
        grid=(nb, nt // tm, n_pairs + 1),
        in_specs=[pl.BlockSpec((1, tm, d), row),
                  pl.BlockSpec((1, 6, d), lambda b, i, e: (b, 0, 0)),
                  pl.BlockSpec((1, 6, d), lambda b, i, e: (nb, 0, 0)),
                  pl.BlockSpec((1, d), c2),
                  pl.BlockSpec((1, d), c2),
                  pl.BlockSpec((N_EXPERTS, d), c2),
                  pl.BlockSpec((N_EXPERTS, 1), c2),
                  pl.BlockSpec((1, d, D_EXPERT), shared_blk),
                  pl.BlockSpec((1, d, D_EXPERT), shared_blk),
                  pl.BlockSpec((1, D_EXPERT, d), shared_blk),
                  pl.BlockSpec((1, 2, d, D_EXPERT), pair_blk),
                  pl.BlockSpec((1, 2, d, D_EXPERT), pair_blk),
                  pl.BlockSpec((1, 2, D_EXPERT, d), pair_blk)],
        out_specs=pl.BlockSpec((1, tm, d), row),
        out_shape=jax.ShapeDtypeStruct((nb, nt, d), F32),
        scratch_shapes=[pltpu.VMEM((tm, d), BF16), pltpu.VMEM((128, tm), F32), pltpu.VMEM((tm, 128), F32),
                        pltpu.VMEM((tm, d), F32)],
        compiler_params=_params("arbitrary", "arbitrary", "arbitrary"),
        name="moe",
    )(xa, mod, mod, g2, gp, rw_t, rbias, ws1, ws3, ws2, w1, w3, w2)


def _swap_perm(width, group):
    j = np.arange(width)
    src = np.where((j % group) < group // 2, j + group // 2, j - group // 2)
    return jnp.asarray(np.arange(width)[:, None] == src[None, :], BF16)


def _rope_tables(n_lat, n_ctx):
    def angles(pos, dim):
        half = dim // 2
        inv = ROPE_BASE ** (-jnp.arange(half, dtype=F32) / half)
        return pos.astype(F32)[:, None] * inv[None, :]

    def tables(cos_parts, sin_parts, reps):
        cos = jnp.tile(jnp.concatenate(cos_parts, axis=-1), (1, reps))
        sin = jnp.tile(jnp.concatenate(sin_parts, axis=-1), (1, reps))
        w = cos.shape[1]
        return (jnp.concatenate([jnp.ones((n_ctx, w), F32), cos], axis=0),
                jnp.concatenate([jnp.zeros((n_ctx, w), F32), sin], axis=0))

    rows = n_lat // GRID_W
    ang_t = angles(jnp.arange(n_lat), HEAD_DIM)
    ang_r = angles(jnp.repeat(jnp.arange(rows), GRID_W), ROPE_DIM // 2)
    ang_c = angles(jnp.tile(jnp.arange(GRID_W), rows), ROPE_DIM // 2)
    ct, st = jnp.cos(ang_t), jnp.sin(ang_t)
    ret = tables([ct, ct], [-st, st], N_HEADS)
    cr, sr, cc, sc = jnp.cos(ang_r), jnp.sin(ang_r), jnp.cos(ang_c), jnp.sin(ang_c)
    mla = tables([cr, cr, cc, cc], [-sr, sr, -sc, sc], N_HEADS)
    return ret, mla


def _ret_tables(logit):
    log_g = jax.nn.log_sigmoid(logit.astype(F32))
    lane_lg = jnp.repeat(log_g, HEAD_DIM, axis=1)
    idx = jnp.arange(CHUNK, dtype=F32)[:, None]
    kd = jnp.stack([jnp.exp(lane_lg[0][None, :] * (CHUNK - 1 - idx)), jnp.exp(lane_lg[1][None, :] * idx)])
    qd = jnp.stack([jnp.exp(lane_lg[0][None, :] * (idx + 1)), jnp.exp(lane_lg[1][None, :] * (CHUNK - idx))])
    cd = jnp.exp(lane_lg * CHUNK)[:, None, :]
    diff = idx - idx.T
    blocks = []
    for h in range(N_HEADS):
        f = jnp.exp(log_g[0, h] * jnp.where(diff >= 0, diff, 0.0))
        b = jnp.exp(log_g[1, h] * jnp.where(diff < 0, -diff, 0.0))
        blocks.append(jnp.where(diff >= 0, f, b))
    dm = jnp.concatenate(blocks, axis=1)
    return kd, cd, qd, dm


def _pack_w_in(w_in):
    d = w_in.shape[0]
    mla = jnp.concatenate([w_in[:, OFF_MLA:OFF_MLA + MLA_COLS], jnp.zeros((d, MLA_PAD - MLA_COLS), w_in.dtype)], 1)
    w = jnp.concatenate([w_in[:, OFF_RET:OFF_RET + RET_COLS], w_in[:, OFF_DN:OFF_DN + 4 * MIX_W],
                         w_in[:, OFF_SG:OFF_SG + SG_COLS], mla, w_in[:, OFF_GATE:OFF_GATE + GATE_COLS]], axis=1)
    wab = w_in[:, OFF_DN + 4 * MIX_W:OFF_DN + DN_COLS]
    wabc = jnp.concatenate([wab, jnp.zeros((d, 128 - 4 * N_HEADS), w_in.dtype)], axis=1)
    return w.astype(BF16), wab.T.astype(BF16), wabc.astype(BF16)


def _mla_weights(w_uq, w_ukv):
    dq = NOPE_DIM + ROPE_DIM
    dkv = NOPE_DIM + V_DIM
    wq = w_uq.reshape(Q_LORA, N_HEADS, dq)
    wqn = wq[:, :, :NOPE_DIM].reshape(Q_LORA, N_HEADS * NOPE_DIM)
    wqr = wq[:, :, NOPE_DIM:].reshape(Q_LORA, N_HEADS * ROPE_DIM)
    wkv = w_ukv.reshape(KV_LORA, N_HEADS, dkv)
    head_eye = jnp.eye(N_HEADS, dtype=F32)
    wa = jnp.einsum("chd,hg->hcgd", wkv[:, :, :NOPE_DIM], head_eye).reshape(N_HEADS, KV_LORA, N_HEADS * NOPE_DIM)
    wa = jnp.pad(wa, ((0, 0), (0, QK_W - KV_LORA), (0, 0)))
    wuv = jnp.einsum("chd,hg->hcgd", wkv[:, :, NOPE_DIM:], head_eye).reshape(N_HEADS, KV_LORA, MIX_W)
    selq = np.zeros((N_HEADS, QK_W, N_HEADS * ROPE_DIM), np.float32)
    for h in range(N_HEADS):
        selq[h, KV_LORA:KV_LORA + ROPE_DIM, h * ROPE_DIM:(h + 1) * ROPE_DIM] = np.eye(ROPE_DIM)
    selc = np.zeros((KV_LORA, QK_W), np.float32)
    selc[:, 0:KV_LORA] = np.eye(KV_LORA)
    selr = np.zeros((128, QK_W), np.float32)
    selr[0:ROPE_DIM, KV_LORA:KV_LORA + ROPE_DIM] = np.eye(ROPE_DIM)
    selv = np.zeros((VT_ROWS, KV_LORA), np.float32)
    selv[0:KV_LORA, :] = np.eye(KV_LORA)
    one_col = np.zeros((VT_ROWS, 1), np.float32)
    one_col[KV_LORA, 0] = 1.0
    return (tuple(jnp.asarray(a, BF16) for a in (wqn, wqr, wa, selq, selc, selr, selv))
            + (jnp.asarray(one_col), wuv.astype(BF16)))


def kernel(x, c, ctx, c_ctx, w_ada, b_ada, g_pre1, g_post1, g_pre2, g_post2, w_in, ret_decay_logit, sg_norm_g, sg_w, sg_b, dn_conv_w, dn_A_log, dn_dt_bias, dn_norm_g, mla_q_norm_g, mla_kv_norm_g, mla_w_uq, mla_w_ukv, w_branch, w_out, router_w, router_bias, moe_w1, moe_w3, moe_w2, shared_w1, shared_w3, shared_w2):
    nb, n_lat, d = x.shape
    n_ctx = ctx.shape[1]
    depth = w_in.shape[0]
    assert d == D_MODEL and n_lat % GRID_W == 0 and n_lat % CHUNK == 0 and n_ctx % 256 == 0
    ncc = n_ctx // CHUNK

    n_cond = -(-(nb + 1) // 8) * 8
    cond = jnp.concatenate([c, c_ctx[None], jnp.zeros((n_cond - nb - 1, d), F32)], axis=0)
    mod_all = _adaln(cond, w_ada, b_ada).reshape(depth, n_cond, 6, d)

    (ret_cos, ret_sin), (mla_cos, mla_sin) = _rope_tables(n_lat, n_ctx)
    perm_ret = _swap_perm(MIX_W, HEAD_DIM)
    perm_mla = _swap_perm(N_HEADS * ROPE_DIM, ROPE_DIM // 2)
    lane_head = jnp.arange(MIX_W) // HEAD_DIM
    bd = (lane_head[:, None] == lane_head[None, :]).astype(F32)
    ones_bd = bd.astype(BF16)
    rw_t = router_w.T.astype(BF16)
    rbias = router_bias.astype(F32)[:, None]

    xa = jnp.concatenate([ctx, x], axis=1)
    for l in range(depth):
        mod = mod_all[l]
        w_l, wab_l, wabc_l = _pack_w_in(w_in[l])
        p, ab_t, ab_c = _inproj(xa, mod, g_pre1[l][None], w_l, wab_l, wabc_l, n_ctx)

        kd, cd, qd, dm = _ret_tables(ret_decay_logit[l])
        wcat = jnp.concatenate([sg_w[l, h] for h in range(N_HEADS)], axis=1).astype(BF16)
        sg_bias = jnp.repeat(sg_b[l].T, HEAD_DIM, axis=1)
        y_ret, y_sg = _retention_and_sgate(p, ret_cos, ret_sin, perm_ret, (kd, cd, qd, dm, bd, ones_bd),
                                           sg_norm_g[l][None], wcat, sg_bias, ncc)

        neg_a = (-jnp.exp(dn_A_log[l].astype(F32))).reshape(2 * N_HEADS, 1)
        dtb = dn_dt_bias[l].astype(F32).reshape(2 * N_HEADS, 1)
        conv_w = jnp.concatenate([dn_conv_w[l], jnp.zeros((8 - CONV_W, 3 * MIX_W), F32)], axis=0)
        o_f, o_b = _deltanet(p, ab_t, ab_c, conv_w, neg_a, dtb, bd, ones_bd, ncc)

        last = l == depth - 1
        row0 = n_ctx if last else 0
        y_mla, y_mla_ctx = _mla(p, mla_cos, mla_sin, perm_mla, mla_q_norm_g[l][None], mla_kv_norm_g[l][None],
                                *_mla_weights(mla_w_uq[l], mla_w_ukv[l]), n_ctx, not last)
        if not last:
            y_mla = jnp.concatenate([y_mla_ctx, y_mla], axis=1)

        xa = _merge(xa, y_ret, y_sg, o_f, o_b, y_mla, p, w_branch[l].astype(BF16), w_out[l].astype(BF16),
                    jnp.tile(dn_norm_g[l], N_HEADS)[None], g_post1[l][None], mod, ones_bd, n_ctx, row0)

        xa = _moe(xa, mod, g_pre2[l][None], g_post2[l][None], rw_t, rbias, shared_w1, shared_w3, shared_w2,
                  moe_w1, moe_w3, moe_w2, l, n_ctx - row0)
    return xa
```

```python
import functools
import math

import jax
import jax.numpy as jnp
import numpy as np
from jax import lax
from jax.experimental import pallas as pl
from jax.experimental.pallas import tpu as pltpu

F32 = jnp.float32
BF16 = jnp.bfloat16
HIGHEST = lax.Precision.HIGHEST

D_MODEL = 1024
GRID_W = 64
N_HEADS = 4
HEAD_DIM = 64
MIX_W = N_HEADS * HEAD_DIM
CHUNK = 128
ROPE_BASE = 10000.0
EPS = 1e-6
RET_DECAY_EXP0 = 5.0
CONV_W = 5
Q_LORA = 256
KV_LORA = 128
NOPE_DIM = 64
ROPE_DIM = 32
V_DIM = 64
N_EXPERTS = 16
N_GROUPS = 4
EXPERTS_PER_GROUP = N_EXPERTS // N_GROUPS
D_EXPERT = 256
N_BRANCH = 4

RET_COLS = 4 * MIX_W
SG_COLS = 2 * MIX_W
DN_COLS = 4 * MIX_W + 4 * N_HEADS
MLA_COLS = Q_LORA + KV_LORA + ROPE_DIM
GATE_COLS = N_BRANCH * D_MODEL
OFF_RET = 0
OFF_SG = OFF_RET + RET_COLS
OFF_DN = OFF_SG + SG_COLS
OFF_MLA = OFF_DN + DN_COLS
OFF_GATE = OFF_MLA + MLA_COLS

P_RET = 0
P_DN = 1024
P_SG = 2048
P_MLA = 2560
P_GATE = 3072
P_COLS = 7168
MLA_PAD = 512

VMEM_LIMIT = 56 * 1024 * 1024


def _dot(a, b, precision=None):
    return jnp.dot(a, b, preferred_element_type=F32, precision=precision)


def _dot_nt(a, b, precision=None):
    return lax.dot_general(a, b, (((1,), (1,)), ((), ())), preferred_element_type=F32, precision=precision)


def _dot_tn(a, b):
    return lax.dot_general(a, b, (((0,), (0,)), ((), ())), preferred_element_type=F32)


def _mm(a, b):
    return _dot(a.astype(BF16), b.astype(BF16))


def _params(*sem):
    return pltpu.CompilerParams(dimension_semantics=sem, vmem_limit_bytes=VMEM_LIMIT)


def _pick(n, cands):
    for c in cands:
        if n % c == 0:
            return c
    raise ValueError(f"no tile for {n}")


def _head_of_lane(width, group):
    return lax.broadcasted_iota(jnp.int32, (1, width), 1) // group


def _stack_heads(x):
    head = _head_of_lane(MIX_W, HEAD_DIM)
    xf = x.astype(F32)
    return jnp.concatenate([jnp.where(head == h, xf, 0.0).astype(BF16) for h in range(N_HEADS)], axis=0)


def _expand_heads(cols):
    head = _head_of_lane(MIX_W, HEAD_DIM)
    out = cols[:, N_HEADS - 1:N_HEADS]
    for h in range(N_HEADS - 2, -1, -1):
        out = jnp.where(head <= h, cols[:, h:h + 1], out)
    return out


def _head_sum(x, ones_bd):
    hi = x.astype(BF16)
    lo = (x - hi.astype(F32)).astype(BF16)
    return _dot(hi, ones_bd) + _dot(lo, ones_bd)


def _rot(x_bf, cos, sin, perm):
    return x_bf.astype(F32) * cos + _dot(x_bf, perm) * sin


def _norm_modulate(x, g, is_ctx, mc_ref, ml_ref, shift_row, scale_row):
    shift = jnp.where(is_ctx, mc_ref[0, shift_row:shift_row + 1, :], ml_ref[0, shift_row:shift_row + 1, :])
    scale = jnp.where(is_ctx, mc_ref[0, scale_row:scale_row + 1, :], ml_ref[0, scale_row:scale_row + 1, :])
    gain = g * (1.0 + scale)
    return x * lax.rsqrt(jnp.mean(x * x, axis=-1, keepdims=True) + EPS) * gain + shift


def _adaln_kernel(c_ref, w_ref, b_ref, o_ref):
    c = c_ref[...]
    s = c * jax.nn.sigmoid(c)
    o_ref[0] = _dot(s, w_ref[0], precision=HIGHEST) + b_ref[0]


def _adaln(cond, w_ada, b_ada):
    n_l, d, d6 = w_ada.shape
    r = cond.shape[0]
    tn = 1024
    return pl.pallas_call(
        _adaln_kernel,
        grid=(n_l, d6 // tn),
        in_specs=[pl.BlockSpec((r, d), lambda l, j: (0, 0)),
                  pl.BlockSpec((1, d, tn), lambda l, j: (l, 0, j)),
                  pl.BlockSpec((1, 1, tn), lambda l, j: (l, 0, j))],
        out_specs=pl.BlockSpec((1, r, tn), lambda l, j: (l, 0, j)),
        out_shape=jax.ShapeDtypeStruct((n_l, r, d6), F32),
        compiler_params=_params("arbitrary", "arbitrary"),
        name="adaln",
    )(cond, w_ada, b_ada.reshape(n_l, 1, d6))


def _inproj_kernel(x_ref, ctx_ref, ml_ref, mc_ref, g_ref, w_ref, wab_ref, wabc_ref, p_ref, ab_ref, abc_ref, xn_ref,
                   *, tm, rb, n_ctx, split):
    i = pl.program_id(1)
    j = pl.program_id(2)

    @pl.when(j == 0)
    def _():
        def blk(r, carry):
            r0 = pl.multiple_of(r * rb, rb)
            is_ctx = i * tm + r0 < n_ctx
            if split:
                x_off = pl.multiple_of(jnp.maximum(jnp.where(i == 0, r0 - n_ctx, r0), 0), rb)
                c_off = pl.multiple_of(jnp.minimum(r0, n_ctx - rb), rb)
                x = jnp.where(is_ctx, ctx_ref[0, pl.ds(c_off, rb), :], x_ref[0, pl.ds(x_off, rb), :])
            else:
                x = x_ref[0, pl.ds(r0, rb), :]
            hn = _norm_modulate(x, g_ref[...], is_ctx, mc_ref, ml_ref, 0, 1)
            xn_ref[pl.ds(r0, rb), :] = hn.astype(BF16)
            return carry

        lax.fori_loop(0, tm // rb, blk, 0)
        ab_ref[0] = _dot_nt(wab_ref[...], xn_ref[...])
        abc_ref[0] = _dot(xn_ref[...], wabc_ref[...])

    p_ref[0] = _dot(xn_ref[...], w_ref[...]).astype(BF16)


def _token_sources(tokens, n_ctx, tm):
    if isinstance(tokens, tuple):
        ctx, x = tokens
        d = x.shape[-1]
        if x.shape[1] >= tm and tm > n_ctx:
            x_spec = pl.BlockSpec(
                (pl.Element(1), pl.Element(tm), pl.Element(d)),
                lambda b, i, *_: (b, pl.multiple_of(jnp.maximum(i * tm - n_ctx, 0), 128), 0))
            ctx_spec = pl.BlockSpec((1, n_ctx, d), lambda b, i, *_: (b, 0, 0))
            return (x, ctx), (x_spec, ctx_spec), True
        tokens = jnp.concatenate([ctx, x], axis=1)
    nb, _, d = tokens.shape
    dummy = jnp.zeros((nb, 8, d), tokens.dtype)
    return ((tokens, dummy), (pl.BlockSpec((1, tm, d), lambda b, i, *_: (b, i, 0)),
                              pl.BlockSpec((1, 8, d), lambda b, i, *_: (b, 0, 0))), False)


def _inproj(tokens, nt, mod, g, w, wab, wabc, n_ctx):
    tm = _pick(nt, (1408, 768, 384, 256, 128))
    tn = 1792
    (xa, ctx), (x_spec, ctx_spec), split = _token_sources(tokens, n_ctx, tm)
    nb, _, d = xa.shape
    kern = functools.partial(_inproj_kernel, tm=tm, rb=128, n_ctx=n_ctx, split=split)
    return pl.pallas_call(
        kern,
        grid=(nb, nt // tm, P_COLS // tn),
        in_specs=[x_spec, ctx_spec,
                  pl.BlockSpec((1, 6, d), lambda b, i, j: (b, 0, 0)),
                  pl.BlockSpec((1, 6, d), lambda b, i, j: (nb, 0, 0)),
                  pl.BlockSpec((1, d), lambda b, i, j: (0, 0)),
                  pl.BlockSpec((d, tn), lambda b, i, j: (0, j)),
                  pl.BlockSpec((16, d), lambda b, i, j: (0, 0)),
                  pl.BlockSpec((d, 128), lambda b, i, j: (0, 0))],
        out_specs=[pl.BlockSpec((1, tm, tn), lambda b, i, j: (b, i, j)),
                   pl.BlockSpec((1, 16, tm), lambda b, i, j: (b, 0, i)),
                   pl.BlockSpec((1, tm, 128), lambda b, i, j: (b, i, 0))],
        out_shape=[jax.ShapeDtypeStruct((nb, nt, P_COLS), BF16),
                   jax.ShapeDtypeStruct((nb, 16, nt), F32),
                   jax.ShapeDtypeStruct((nb, nt, 128), F32)],
        scratch_shapes=[pltpu.VMEM((tm, d), BF16)],
        compiler_params=_params("arbitrary", "arbitrary", "arbitrary"),
        name="inproj",
    )(xa, ctx, mod, mod, g, w, wab, wabc)


def _bwd_chunk(t, ncc, nc):
    return jnp.where(t < ncc, ncc - 1 - t, nc - 1 - (t - ncc))


def _ret_state_kernel(pf_ref, pb_ref, cf_ref, sf_ref, cb_ref, sb_ref, perm_ref, kd_ref, cd_ref, bd_ref,
                      of_ref, ob_ref, st_f, st_b, *, cb):
    t = pl.program_id(1)

    @pl.when(t == 0)
    def _():
        st_f[...] = jnp.zeros_like(st_f)
        st_b[...] = jnp.zeros_like(st_b)

    def increments(p_ref, c_ref, s_ref, d):
        out = []
        for i in range(cb):
            r = slice(i * CHUNK, (i + 1) * CHUNK)
            kr = _rot(p_ref[0, r, MIX_W:2 * MIX_W], c_ref[r, :], s_ref[r, :], perm_ref[...]) * (HEAD_DIM ** -0.5)
            out.append(bd_ref[...] * _dot_tn((kr * kd_ref[d]).astype(BF16), p_ref[0, r, 2 * MIX_W:3 * MIX_W]))
        return out

    inc_f = increments(pf_ref, cf_ref, sf_ref, 0)
    inc_b = increments(pb_ref, cb_ref, sb_ref, 1)
    s = st_f[...]
    for i in range(cb):
        of_ref[0, i] = s.astype(BF16)
        s = cd_ref[0] * s + inc_f[i]
    st_f[...] = s
    s = st_b[...]
    for i in reversed(range(cb)):
        ob_ref[0, i] = s.astype(BF16)
        s = cd_ref[1] * s + inc_b[i]
    st_b[...] = s


def _gelu_tanh(x):
    return 0.5 * x * (1.0 + jnp.tanh(math.sqrt(2.0 / math.pi) * (x + 0.044715 * (x * x * x))))


def _mix_out_kernel(p_ref, pg_ref, c_ref, s_ref, sf_ref, sb_ref, perm_ref, dm_ref, qd_ref, ones_ref,
                    ng_ref, wg_ref, bg_ref, y_ref, ysg_ref, *, cb):
    chunks = range(cb)
    rows = [slice(i * CHUNK, (i + 1) * CHUNK) for i in chunks]
    perm, dm, ones_bd = perm_ref[...], dm_ref[...], ones_ref[...]
    p = [p_ref[0, r, :] for r in rows]
    cos = [c_ref[r, :] for r in rows]
    sin = [s_ref[r, :] for r in rows]
    qr = [_rot(p[i][:, 0:MIX_W], cos[i], sin[i], perm) for i in chunks]
    kr = [_rot(p[i][:, MIX_W:2 * MIX_W], cos[i], sin[i], perm) * (HEAD_DIM ** -0.5) for i in chunks]
    z = [_gelu_tanh(pg_ref[0, r, :].astype(F32)) for r in rows]
    vg = [x[:, MIX_W:] for x in z]
    mu_g = [jnp.mean(x, axis=-1, keepdims=True) for x in vg]
    vgc = [x - m for x, m in zip(vg, mu_g)]
    var_g = [jnp.mean(x * x, axis=-1, keepdims=True) for x in vgc]
    vn = [x * lax.rsqrt(s + EPS) * ng_ref[...] for x, s in zip(vgc, var_g)]
    sc = [_dot_nt(qr[i].astype(BF16), _stack_heads(kr[i])) * dm for i in chunks]
    mixed = [_dot(wg_ref[...], _stack_heads(x)) for x in vn]
    o = [_dot(sc[i].astype(BF16), _stack_heads(p[i][:, 2 * MIX_W:3 * MIX_W])) for i in chunks]
    qs = [jnp.concatenate([(qr[i] * qd_ref[0]).astype(BF16), (qr[i] * qd_ref[1]).astype(BF16)], axis=1)
          for i in chunks]
    ss = [jnp.concatenate([sf_ref[0, i], sb_ref[0, i]], axis=0) for i in chunks]
    o = [o[i] + _dot(qs[i], ss[i]) for i in chunks]
    for i in chunks:
        ysg_ref[0, rows[i], :] = (z[i][:, :MIX_W] * (mixed[i] + bg_ref[...])).astype(BF16)
    mu = [_head_sum(x, ones_bd) * (1.0 / HEAD_DIM) for x in o]
    oc = [x - m for x, m in zip(o, mu)]
    var = [_head_sum(x * x, ones_bd) * (1.0 / HEAD_DIM) for x in oc]
    for i in chunks:
        g = p[i][:, 3 * MIX_W:4 * MIX_W].astype(F32)
        y_ref[0, rows[i], :] = (oc[i] * lax.rsqrt(var[i] + EPS) * (g * jax.nn.sigmoid(g))).astype(BF16)


def _retention_and_sgate(p, cos, sin, perm, tabs, sg_ng, sg_w, sg_bias, ncc):
    nb, nt, _ = p.shape
    nc = nt // CHUNK
    kd, cd, qd, dm, bd, ones_bd = tabs
    cb = 2
    assert nc % cb == 0 and ncc % cb == 0
    nblk, ncb = nc // cb, ncc // cb
    fwd = lambda b, t: (b, t, 0)
    bwd = lambda b, t: (b, _bwd_chunk(t, ncb, nblk), 0)
    tab_f = lambda b, t: (t, 0)
    tab_b = lambda b, t: (_bwd_chunk(t, ncb, nblk), 0)
    c2 = lambda b, t: (0, 0)
    c3 = lambda b, t: (0, 0, 0)
    st_shape = jax.ShapeDtypeStruct((nb, nc, MIX_W, MIX_W), BF16)
    st_f, st_b = pl.pallas_call(
        functools.partial(_ret_state_kernel, cb=cb),
        grid=(nb, nblk),
        in_specs=[pl.BlockSpec((1, cb * CHUNK, RET_COLS), fwd),
                  pl.BlockSpec((1, cb * CHUNK, RET_COLS), bwd),
                  pl.BlockSpec((cb * CHUNK, MIX_W), tab_f), pl.BlockSpec((cb * CHUNK, MIX_W), tab_f),
                  pl.BlockSpec((cb * CHUNK, MIX_W), tab_b), pl.BlockSpec((cb * CHUNK, MIX_W), tab_b),
                  pl.BlockSpec((MIX_W, MIX_W), c2),
                  pl.BlockSpec((2, CHUNK, MIX_W), c3),
                  pl.BlockSpec((2, 1, MIX_W), c3),
                  pl.BlockSpec((MIX_W, MIX_W), c2)],
        out_specs=[pl.BlockSpec((1, cb, MIX_W, MIX_W), lambda b, t: (b, t, 0, 0)),
                   pl.BlockSpec((1, cb, MIX_W, MIX_W), lambda b, t: (b, _bwd_chunk(t, ncb, nblk), 0, 0))],
        out_shape=[st_shape, st_shape],
        scratch_shapes=[pltpu.VMEM((MIX_W, MIX_W), F32), pltpu.VMEM((MIX_W, MIX_W), F32)],
        compiler_params=_params("arbitrary", "arbitrary"),
        name="ret_state",
    )(p, p, cos, sin, cos, sin, perm, kd, cd, bd)
    blk = lambda b, t: (b, t, 0)
    y_shape = jax.ShapeDtypeStruct((nb, nt, MIX_W), BF16)
    return pl.pallas_call(
        functools.partial(_mix_out_kernel, cb=cb),
        grid=(nb, nc // cb),
        in_specs=[pl.BlockSpec((1, cb * CHUNK, RET_COLS), blk),
                  pl.BlockSpec((1, cb * CHUNK, SG_COLS), lambda b, t: (b, t, P_SG // SG_COLS)),
                  pl.BlockSpec((cb * CHUNK, MIX_W), tab_f), pl.BlockSpec((cb * CHUNK, MIX_W), tab_f),
                  pl.BlockSpec((1, cb, MIX_W, MIX_W), lambda b, t: (b, t, 0, 0)),
                  pl.BlockSpec((1, cb, MIX_W, MIX_W), lambda b, t: (b, t, 0, 0)),
                  pl.BlockSpec((MIX_W, MIX_W), c2),
                  pl.BlockSpec((CHUNK, N_HEADS * CHUNK), c2),
                  pl.BlockSpec((2, CHUNK, MIX_W), c3),
                  pl.BlockSpec((MIX_W, MIX_W), c2),
                  pl.BlockSpec((1, MIX_W), c2),
                  pl.BlockSpec((CHUNK, N_HEADS * CHUNK), c2),
                  pl.BlockSpec((CHUNK, MIX_W), c2)],
        out_specs=[pl.BlockSpec((1, cb * CHUNK, MIX_W), blk), pl.BlockSpec((1, cb * CHUNK, MIX_W), blk)],
        out_shape=[y_shape, y_shape],
        compiler_params=_params("arbitrary", "arbitrary"),
        name="mix_out",
    )(p, p, cos, sin, st_f, st_b, perm, dm, qd, ones_bd, sg_ng, sg_w, sg_bias)


def _softplus(a):
    return jnp.maximum(a, 0.0) + jnp.log1p(jnp.exp(-jnp.abs(a)))


def _dn_prep_kernel(pc_ref, pp_ref, pn_ref, ab_ref, abc_ref, cw_ref, na_ref, dtb_ref, nar_ref, dtbr_ref, ones_ref,
                    qkv_ref, gb_ref, gbc_ref, xe_ref, *, ncc, nc):
    t = pl.program_id(1)
    w3 = 3 * MIX_W
    prev_ok = jnp.where((t != 0) & (t != ncc), 1.0, 0.0)
    next_ok = jnp.where((t != ncc - 1) & (t != nc - 1), 1.0, 0.0)
    tail = pp_ref[0, CHUNK - 16:CHUNK, 0:w3].astype(F32)
    head = pn_ref[0, 0:16, 0:w3].astype(F32)
    xe_ref[0:8, :] = tail[8:16, :] * prev_ok
    xe_ref[8:8 + CHUNK, :] = pc_ref[0, :, 0:w3].astype(F32)
    xe_ref[8 + CHUNK:16 + CHUNK, :] = head[0:8, :] * next_ok
    pad = CONV_W // 2
    y = xe_ref[8 - pad:8 - pad + CHUNK, :] * cw_ref[0:1, :]
    for i in range(1, CONV_W):
        y = y + xe_ref[8 - pad + i:8 - pad + i + CHUNK, :] * cw_ref[i:i + 1, :]
    y = y * jax.nn.sigmoid(y)
    q = y[:, 0:MIX_W]
    k = y[:, MIX_W:2 * MIX_W]
    v = y[:, 2 * MIX_W:w3]
    ones_bd = ones_ref[...]
    qn = q * lax.rsqrt(_head_sum(q * q, ones_bd) + EPS) * (HEAD_DIM ** -0.5)
    kn = k * lax.rsqrt(_head_sum(k * k, ones_bd) + EPS)
    qkv_ref[0, :, 0:MIX_W] = qn.astype(BF16)
    qkv_ref[0, :, MIX_W:2 * MIX_W] = kn.astype(BF16)
    qkv_ref[0, :, 2 * MIX_W:w3] = v.astype(BF16)
    ab = ab_ref[0]
    gb_ref[0, 0:8, :] = na_ref[...] * _softplus(ab[0:8, :] + dtb_ref[...])
    gb_ref[0, 8:16, :] = jax.nn.sigmoid(ab[8:16, :])
    abc = abc_ref[0]
    lane = lax.broadcasted_iota(jnp.int32, (1, 128), 1)
    g_c = nar_ref[...] * _softplus(abc + dtbr_ref[...])
    gbc_ref[0] = jnp.where(lane < 8, g_c, jnp.where(lane < 16, jax.nn.sigmoid(abc), 0.0))


def _split3(x):
    hi = x.astype(BF16)
    r = x - hi.astype(F32)
    mid = r.astype(BF16)
    lo = (r - mid.astype(F32)).astype(BF16)
    return hi, mid, lo


def _tri_inverse(mats, ii, jj):
    eye = jnp.where(ii == jj, 1.0, 0.0)
    nd = [jnp.where((ii // 16) == (jj // 16), n, 0.0) for n in mats]
    p1 = [_mm(x, x) for x in nd]
    m = [eye - x for x in nd]
    p2 = [_mm(x, x) for x in p1]
    m = [x + _mm(x, y) for x, y in zip(m, p1)]
    p3 = [_mm(x, x) for x in p2]
    m = [x + _mm(x, y) for x, y in zip(m, p2)]
    m = [x + _mm(x, y) for x, y in zip(m, p3)]
    for lvl in (16, 32, 64):
        off_mask = ((ii // (2 * lvl)) == (jj // (2 * lvl))) & ((ii // lvl) != (jj // lvl))
        t = [_mm(jnp.where(off_mask, n, 0.0), x) for n, x in zip(mats, m)]
        m = [x - _mm(x, y) for x, y in zip(m, t)]
    return m


def _dn_pre(qkv, g, gbc, d, lower):
    c = CHUNK
    qn = qkv[:, 0:MIX_W]
    kn = qkv[:, MIX_W:2 * MIX_W]
    v = qkv[:, 2 * MIX_W:3 * MIX_W]
    ii = lax.broadcasted_iota(jnp.int32, (c, c), 0)
    jj = lax.broadcasted_iota(jnp.int32, (c, c), 1)
    incl = (ii >= jj) if lower else (ii <= jj)
    tri = jnp.where(incl, 1.0, 0.0).astype(BF16)
    g_row = sum(_dot_nt(part, tri) for part in _split3(g))[N_HEADS * d:N_HEADS * (d + 1), :]
    cum = sum(_dot(tri, part) for part in _split3(gbc))
    g_col = cum[:, N_HEADS * d:N_HEADS * (d + 1)]
    b_col = gbc[:, 2 * N_HEADS + N_HEADS * d:2 * N_HEADS + N_HEADS * (d + 1)]
    g_cols4 = jnp.concatenate([jnp.broadcast_to(g_col[:, h:h + 1], (c, c)) for h in range(N_HEADS)], axis=1)
    b_cols4 = jnp.concatenate([jnp.broadcast_to(b_col[:, h:h + 1], (c, c)) for h in range(N_HEADS)], axis=1)
    g_rows4 = jnp.concatenate([g_row[h:h + 1, :] for h in range(N_HEADS)], axis=1)
    incl4 = jnp.concatenate([incl] * N_HEADS, axis=1)
    diag4 = jnp.concatenate([ii == jj] * N_HEADS, axis=1)
    decay = jnp.where(incl4, jnp.exp(jnp.where(incl4, g_cols4 - g_rows4, 0.0)), 0.0)
    kstack = _stack_heads(kn)
    kk = _dot_nt(kn, kstack)
    qk = _dot_nt(qn, kstack)
    n_mat = jnp.where(diag4, 0.0, decay * kk * b_cols4)
    attn = (decay * qk).astype(BF16)
    g256 = _expand_heads(g_col)
    eg256 = jnp.exp(g256)
    b256 = _expand_heads(b_col)
    vb = v.astype(F32) * b256
    kbg = kn.astype(F32) * b256 * eg256
    rhs = jnp.concatenate([_stack_heads(vb), _stack_heads(kbg)], axis=1)
    g_last = g256[c - 1:c, :] if lower else g256[0:1, :]
    kdec = (kn.astype(F32) * jnp.exp(g_last - g256)).astype(BF16)
    n_heads = [n_mat[:, h * c:(h + 1) * c] for h in range(N_HEADS)]
    return n_heads, dict(qn=qn, attn=attn, rhs=rhs, eg=eg256, kdec=kdec, sdec=jnp.exp(g_last))


def _dn_post(z, s_prev, bd):
    s_bf = s_prev.astype(BF16)
    w = z["u"] - _dot(z["wk"], s_bf)
    o = z["eg"] * _dot(z["qn"], s_bf) + _dot(z["attn"], _stack_heads(w))
    s_next = z["sdec"] * s_prev + bd * _dot_tn(z["kdec"], w.astype(BF16))
    return o, s_next


def _dn_scan_kernel(qf_ref, qb_ref, gf_ref, gb_ref, gcf_ref, gcb_ref, bd_ref, of_ref, ob_ref, st_f, st_b, *, cb):
    t = pl.program_id(1)

    @pl.when(t == 0)
    def _():
        st_f[...] = jnp.zeros_like(st_f)
        st_b[...] = jnp.zeros_like(st_b)

    bd = bd_ref[...]
    rows = [slice(i * CHUNK, (i + 1) * CHUNK) for i in range(cb)]
    mats, pres = [], []
    for d, (q_ref, g_ref, gc_ref) in enumerate(((qf_ref, gf_ref, gcf_ref), (qb_ref, gb_ref, gcb_ref))):
        for r in rows:
            n_heads, pre = _dn_pre(q_ref[0, r, :], g_ref[0, :, r], gc_ref[0, r, :], d, d == 0)
            mats += n_heads
            pres.append(pre)
    ii = lax.broadcasted_iota(jnp.int32, (CHUNK, CHUNK), 0)
    jj = lax.broadcasted_iota(jnp.int32, (CHUNK, CHUNK), 1)
    inv = _tri_inverse(mats, ii, jj)
    for n, pre in enumerate(pres):
        a_inv = jnp.concatenate(inv[N_HEADS * n:N_HEADS * (n + 1)], axis=1).astype(BF16)
        uw = _dot(a_inv, pre["rhs"])
        pre["u"] = uw[:, 0:MIX_W]
        pre["wk"] = uw[:, MIX_W:2 * MIX_W].astype(BF16)
    s_f, s_b = st_f[...], st_b[...]
    for k in range(cb):
        o, s_f = _dn_post(pres[k], s_f, bd)
        of_ref[0, rows[k], :] = o
        o, s_b = _dn_post(pres[cb + cb - 1 - k], s_b, bd)
        ob_ref[0, rows[cb - 1 - k], :] = o
    st_f[...] = s_f
    st_b[...] = s_b


def _deltanet(p, ab_t, ab_c, conv_w, neg_a, dtb, bd, ones_bd, ncc):
    nb, nt, _ = p.shape
    nc = nt // CHUNK
    w3 = 3 * MIX_W
    c2 = lambda b, t: (0, 0)
    dn_blk = P_DN // RET_COLS
    pad_lanes = lambda col: jnp.concatenate([col.reshape(1, -1), jnp.zeros((1, 128 - col.size), F32)], axis=1)
    qkv, gbeta, gbeta_c = pl.pallas_call(
        functools.partial(_dn_prep_kernel, ncc=ncc, nc=nc),
        grid=(nb, nc),
        in_specs=[pl.BlockSpec((1, CHUNK, 4 * MIX_W), lambda b, t: (b, t, dn_blk)),
                  pl.BlockSpec((1, CHUNK, 4 * MIX_W), lambda b, t: (b, jnp.maximum(t - 1, 0), dn_blk)),
                  pl.BlockSpec((1, CHUNK, 4 * MIX_W), lambda b, t: (b, jnp.minimum(t + 1, nc - 1), dn_blk)),
                  pl.BlockSpec((1, 16, CHUNK), lambda b, t: (b, 0, t)),
                  pl.BlockSpec((1, CHUNK, 128), lambda b, t: (b, t, 0)),
                  pl.BlockSpec((8, w3), c2),
                  pl.BlockSpec((8, 1), c2),
                  pl.BlockSpec((8, 1), c2),
                  pl.BlockSpec((1, 128), c2),
                  pl.BlockSpec((1, 128), c2),
                  pl.BlockSpec((MIX_W, MIX_W), c2)],
        out_specs=[pl.BlockSpec((1, CHUNK, w3), lambda b, t: (b, t, 0)),
                   pl.BlockSpec((1, 16, CHUNK), lambda b, t: (b, 0, t)),
                   pl.BlockSpec((1, CHUNK, 128), lambda b, t: (b, t, 0))],
        out_shape=[jax.ShapeDtypeStruct((nb, nt, w3), BF16),
                   jax.ShapeDtypeStruct((nb, 16, nt), F32),
                   jax.ShapeDtypeStruct((nb, nt, 128), F32)],
        scratch_shapes=[pltpu.VMEM((CHUNK + 16, w3), F32)],
        compiler_params=_params("arbitrary", "arbitrary"),
        name="dn_prep",
    )(p, p, p, ab_t, ab_c, conv_w, neg_a, dtb, pad_lanes(neg_a), pad_lanes(dtb), ones_bd)
    cb = 2
    assert nc % cb == 0 and ncc % cb == 0
    rows = cb * CHUNK
    cur_b = lambda t: _bwd_chunk(t, ncc // cb, nc // cb)
    o_shape = jax.ShapeDtypeStruct((nb, nt, MIX_W), F32)
    return pl.pallas_call(
        functools.partial(_dn_scan_kernel, cb=cb),
        grid=(nb, nc // cb),
        in_specs=[pl.BlockSpec((1, rows, w3), lambda b, t: (b, t, 0)),
                  pl.BlockSpec((1, rows, w3), lambda b, t: (b, cur_b(t), 0)),
                  pl.BlockSpec((1, 16, rows), lambda b, t: (b, 0, t)),
                  pl.BlockSpec((1, 16, rows), lambda b, t: (b, 0, cur_b(t))),
                  pl.BlockSpec((1, rows, 128), lambda b, t: (b, t, 0)),
                  pl.BlockSpec((1, rows, 128), lambda b, t: (b, cur_b(t), 0)),
                  pl.BlockSpec((MIX_W, MIX_W), c2)],
        out_specs=[pl.BlockSpec((1, rows, MIX_W), lambda b, t: (b, t, 0)),
                   pl.BlockSpec((1, rows, MIX_W), lambda b, t: (b, cur_b(t), 0))],
        out_shape=[o_shape, o_shape],
        scratch_shapes=[pltpu.VMEM((MIX_W, MIX_W), F32), pltpu.VMEM((MIX_W, MIX_W), F32)],
        compiler_params=_params("arbitrary", "arbitrary"),
        name="dn_scan",
    )(qkv, qkv, gbeta, gbeta, gbeta_c, gbeta_c, bd)


QK_W = 256


VT_ROWS = 144


def _mla_prep_kernel(p_ref, c_ref, s_ref, perm_ref, qg_ref, kg_ref, wqn_ref, wqr_ref, wa_ref, selq_ref, selc_ref,
                     selr_ref, selv_ref, one_ref, qt_ref, kv_ref, vt_ref, *, scale):
    p = p_ref[0]
    cos, sin, perm = c_ref[...], s_ref[...], perm_ref[...]
    cq = p[:, 0:Q_LORA].astype(F32)
    cqn = (cq * lax.rsqrt(jnp.mean(cq * cq, axis=-1, keepdims=True) + EPS) * qg_ref[...]).astype(BF16)
    q_nope = _dot(cqn, wqn_ref[...]).astype(BF16)
    q_rope = _dot(cqn, wqr_ref[...]).astype(BF16)
    q_rot = (_rot(q_rope, cos, sin, perm) * scale).astype(BF16)
    q_nope_s = (q_nope.astype(F32) * scale).astype(BF16)
    for h in range(N_HEADS):
        qt_ref[0, h] = (_dot_nt(wa_ref[h], q_nope_s) + _dot_nt(selq_ref[h], q_rot)).astype(BF16)
    ckv = p[:, Q_LORA:Q_LORA + KV_LORA].astype(F32)
    ckvn = (ckv * lax.rsqrt(jnp.mean(ckv * ckv, axis=-1, keepdims=True) + EPS) * kg_ref[...]).astype(BF16)
    kr = p[:, Q_LORA + KV_LORA:MLA_PAD]
    kr_rot = _rot(kr, cos, sin, perm).astype(BF16)
    kv_ref[0] = (_dot(ckvn, selc_ref[...]) + _dot(kr_rot, selr_ref[...])).astype(BF16)
    vt_ref[0] = (_dot_nt(selv_ref[...], ckvn) + one_ref[...]).astype(BF16)


def _mla_attn_kernel(qt_ref, kv_ref, vt_ref, wuv_ref, y_ref, m_ref, acc_ref, s_ref, *, tk, n_ctx, nt, latent):
    heads = range(N_HEADS)
    m_ref[...] = jnp.full_like(m_ref, -jnp.inf)
    acc_ref[...] = jnp.zeros_like(acc_ref)

    def scores(j0, size, slot):
        k = kv_ref[0, pl.ds(j0, size), :]
        for h in heads:
            s_ref[slot, h, 0:size, :] = _dot(k, qt_ref[0, h])

    def softmax_pv(j0, size, slot):
        vt = vt_ref[0, :, pl.ds(j0, size)]
        s = [s_ref[slot, h, 0:size, :] for h in heads]
        m_old = [m_ref[h] for h in heads]
        m_new = [jnp.maximum(m_old[h], jnp.max(s[h], axis=0, keepdims=True)) for h in heads]
        pr = [jnp.exp2(s[h] - m_new[h]).astype(BF16) for h in heads]
        pv = [_dot(vt, pr[h]) for h in heads]
        for h in heads:
            acc_ref[h] = jnp.exp2(m_old[h] - m_new[h]) * acc_ref[h] + pv[h]
            m_ref[h] = m_new[h]

    scores(0, n_ctx, 0)
    if not latent:
        softmax_pv(0, n_ctx, 0)
    else:
        n_tiles = (nt - n_ctx) // tk
        last = n_ctx + (n_tiles - 1) * tk
        scores(n_ctx, tk, 1)
        softmax_pv(0, n_ctx, 0)

        def body(jj, carry):
            t0 = pl.multiple_of(n_ctx + 2 * jj * tk, 256)
            t1 = pl.multiple_of(jnp.minimum(t0 + tk, last), 256)
            t2 = pl.multiple_of(jnp.minimum(t0 + 2 * tk, last), 256)
            scores(t1, tk, 0)
            softmax_pv(t0, tk, 1)
            scores(t2, tk, 1)
            softmax_pv(t1, tk, 0)
            return carry

        lax.fori_loop(0, n_tiles // 2, body, 0)
        if n_tiles % 2:
            softmax_pv(last, tk, 1)

    y = None
    for h in range(N_HEADS):
        acc = acc_ref[h]
        o = (acc[0:KV_LORA, :] / acc[KV_LORA:KV_LORA + 1, :]).astype(BF16)
        term = _dot_tn(o, wuv_ref[h])
        y = term if y is None else y + term
    y_ref[0] = y.astype(BF16)


def _mla(p, cos, sin, perm, qg, kg, wqn, wqr, wa, selq, selc, selr, selv, one_col, wuv, n_ctx, ctx_out):
    nb, nt, _ = p.shape
    n_lat = nt - n_ctx
    tm = 256
    assert n_ctx % tm == 0
    n_tiles, ctx_tiles = nt // tm, n_ctx // tm
    scale = (NOPE_DIM + ROPE_DIM) ** -0.5 * math.log2(math.e)
    c2 = lambda b, i: (0, 0)
    c3 = lambda b, i: (0, 0, 0)
    q_pos = lambda i: (i + n_tiles - ctx_tiles) % n_tiles
    qt, kv, vt = pl.pallas_call(
        functools.partial(_mla_prep_kernel, scale=scale),
        grid=(nb, nt // tm),
        in_specs=[pl.BlockSpec((1, tm, MLA_PAD), lambda b, i: (b, i, P_MLA // MLA_PAD)),
                  pl.BlockSpec((tm, 128), lambda b, i: (i, 0)),
                  pl.BlockSpec((tm, 128), lambda b, i: (i, 0)),
                  pl.BlockSpec((128, 128), c2),
                  pl.BlockSpec((1, Q_LORA), c2),
                  pl.BlockSpec((1, KV_LORA), c2),
                  pl.BlockSpec((Q_LORA, N_HEADS * NOPE_DIM), c2),
                  pl.BlockSpec((Q_LORA, N_HEADS * ROPE_DIM), c2),
                  pl.BlockSpec((N_HEADS, QK_W, N_HEADS * NOPE_DIM), c3),
                  pl.BlockSpec((N_HEADS, QK_W, N_HEADS * ROPE_DIM), c3),
                  pl.BlockSpec((KV_LORA, QK_W), c2),
                  pl.BlockSpec((128, QK_W), c2),
                  pl.BlockSpec((VT_ROWS, KV_LORA), c2),
                  pl.BlockSpec((VT_ROWS, 1), c2)],
        out_specs=[pl.BlockSpec((1, N_HEADS, QK_W, tm), lambda b, i: (b, 0, 0, q_pos(i))),
                   pl.BlockSpec((1, tm, QK_W), lambda b, i: (b, i, 0)),
                   pl.BlockSpec((1, VT_ROWS, tm), lambda b, i: (b, 0, i))],
        out_shape=[jax.ShapeDtypeStruct((nb, N_HEADS, QK_W, nt), BF16),
                   jax.ShapeDtypeStruct((nb, nt, QK_W), BF16),
                   jax.ShapeDtypeStruct((nb, VT_ROWS, nt), BF16)],
        compiler_params=_params("arbitrary", "arbitrary"),
        name="mla_prep",
    )(p, cos, sin, perm, qg, kg, wqn, wqr, wa, selq, selc, selr, selv, one_col)
    tk = _pick(n_lat, (512, 256))

    def attend(tq, first_col, n_q, latent):
        return pl.pallas_call(
            functools.partial(_mla_attn_kernel, tk=tk, n_ctx=n_ctx, nt=nt, latent=latent),
            grid=(nb, n_q // tq),
            in_specs=[pl.BlockSpec((1, N_HEADS, QK_W, tq), lambda b, i: (b, 0, 0, i + first_col // tq)),
                      pl.BlockSpec((1, nt, QK_W), lambda b, i: (b, 0, 0)),
                      pl.BlockSpec((1, VT_ROWS, nt), lambda b, i: (b, 0, 0)),
                      pl.BlockSpec((N_HEADS, KV_LORA, MIX_W), c3)],
            out_specs=pl.BlockSpec((1, tq, MIX_W), lambda b, i: (b, i, 0)),
            out_shape=jax.ShapeDtypeStruct((nb, n_q, MIX_W), BF16),
            scratch_shapes=[pltpu.VMEM((N_HEADS, 1, tq), F32), pltpu.VMEM((N_HEADS, VT_ROWS, tq), F32),
                            pltpu.VMEM((2, N_HEADS, max(tk, n_ctx) if latent else n_ctx, tq), F32)],
            compiler_params=_params("arbitrary", "arbitrary"),
            name="mla_attn" if latent else "mla_attn_ctx",
        )(qt, kv, vt, wuv)

    y_lat = attend(_pick(n_lat, (512, 256)), 0, n_lat, True)
    y_ctx = attend(tm, n_lat, n_ctx, False) if ctx_out else None
    return y_lat, y_ctx


def _merge_kernel(x_ref, ctx_ref, yr_ref, ys_ref, of_ref, ob_ref, ym_ref, z_ref, g0_ref, g1_ref, g2_ref, g3_ref,
                  wb_ref, wo_ref, ng_ref, gp_ref, ml_ref, mc_ref, ones_ref, o_ref, *, tm, n_ctx, row0, split):
    i = pl.program_id(1)
    if split:
        first = jnp.concatenate([ctx_ref[0], x_ref[0, 0:tm - n_ctx, :]], axis=0)
        x_res = jnp.where(i == 0, first, x_ref[0])
    else:
        x_res = x_ref[0]
    od = of_ref[0] + ob_ref[0]
    ms = _head_sum(od * od, ones_ref[...]) * (1.0 / HEAD_DIM)
    z = z_ref[0].astype(F32)
    ydn = (od * lax.rsqrt(ms + EPS) * ng_ref[...]) * (z * jax.nn.sigmoid(z))
    ys = (yr_ref[0], ys_ref[0], ydn.astype(BF16), ym_ref[0])
    gates = (g0_ref, g1_ref, g2_ref, g3_ref)
    acc = None
    for b in range(N_BRANCH):
        term = jax.nn.sigmoid(gates[b][0].astype(F32)) * _dot(ys[b], wb_ref[b])
        acc = term if acc is None else acc + term
    y = _dot(acc.astype(BF16), wo_ref[...])
    r = y * lax.rsqrt(jnp.mean(y * y, axis=-1, keepdims=True) + EPS) * gp_ref[...]
    rows = lax.broadcasted_iota(jnp.int32, (tm, 1), 0) + (row0 + i * tm)
    gate = jnp.where(rows < n_ctx, mc_ref[0, 2:3, :], ml_ref[0, 2:3, :])
    o_ref[0] = x_res + gate * r


def _merge(tokens, y_ret, y_sg, o_f, o_b, y_mla, p, wb, wo, ng, gp, mod, ones_bd, n_ctx, row0):
    nb, nt, _ = p.shape
    d = D_MODEL
    n_rows = nt - row0
    tm = _pick(n_rows, (768, 512, 384, 256, 128))
    c2 = lambda b, i: (0, 0)
    if row0 == 0:
        def window(width, col):
            return pl.BlockSpec((1, tm, width), lambda b, i: (b, i, col // width))

        (xa, ctx), (x_spec, ctx_spec), split = _token_sources(tokens, n_ctx, tm)
    else:
        def window(width, col):
            return pl.BlockSpec((pl.Element(1), pl.Element(tm), pl.Element(width)),
                                lambda b, i: (b, pl.multiple_of(row0 + i * tm, 128), col))

        xa, x_spec, split = tokens, window(d, 0), False
        ctx, ctx_spec = jnp.zeros((nb, 8, d), F32), pl.BlockSpec((1, 8, d), lambda b, i: (b, 0, 0))
    y_spec = window(MIX_W, 0)
    assert y_mla.shape[1] == n_rows
    mla_spec = pl.BlockSpec((1, tm, MIX_W), lambda b, i: (b, i, 0))
    gate_specs = [window(d, P_GATE + k * d) for k in range(N_BRANCH)]
    return pl.pallas_call(
        functools.partial(_merge_kernel, tm=tm, n_ctx=n_ctx, row0=row0, split=split),
        grid=(nb, n_rows // tm),
        in_specs=[x_spec, ctx_spec, y_spec, y_spec, y_spec, y_spec, mla_spec,
                  window(MIX_W, P_DN + 3 * MIX_W),
                  *gate_specs,
                  pl.BlockSpec((N_BRANCH, MIX_W, d), lambda b, i: (0, 0, 0)),
                  pl.BlockSpec((d, d), c2),
                  pl.BlockSpec((1, MIX_W), c2),
                  pl.BlockSpec((1, d), c2),
                  pl.BlockSpec((1, 6, d), lambda b, i: (b, 0, 0)),
                  pl.BlockSpec((1, 6, d), lambda b, i: (nb, 0, 0)),
                  pl.BlockSpec((MIX_W, MIX_W), c2)],
        out_specs=pl.BlockSpec((1, tm, d), lambda b, i: (b, i, 0)),
        out_shape=jax.ShapeDtypeStruct((nb, n_rows, d), F32),
        compiler_params=_params("arbitrary", "arbitrary"),
        name="merge",
    )(xa, ctx, y_ret, y_sg, o_f, o_b, y_mla, p, p, p, p, p, wb, wo, ng, gp, mod, mod, ones_bd)


def _route(sel, aff):
    rows = [sel[e:e + 1, :] for e in range(N_EXPERTS)]
    pairs = [(a, b) for a in range(EXPERTS_PER_GROUP) for b in range(a + 1, EXPERTS_PER_GROUP)]
    grp_score, grp_pair = [], []
    for g in range(N_GROUPS):
        base = g * EXPERTS_PER_GROUP
        best = rows[base + pairs[0][0]] + rows[base + pairs[0][1]]
        best_p = jnp.zeros_like(best, dtype=jnp.int32)
        for pi in range(1, len(pairs)):
            s = rows[base + pairs[pi][0]] + rows[base + pairs[pi][1]]
            take = s > best
            best = jnp.where(take, s, best)
            best_p = jnp.where(take, pi, best_p)
        grp_score.append(best)
        grp_pair.append(best_p)
    top = grp_score[0]
    top_g = jnp.zeros_like(grp_pair[0])
    top_p = grp_pair[0]
    for g in range(1, N_GROUPS):
        take = grp_score[g] > top
        top = jnp.where(take, grp_score[g], top)
        top_g = jnp.where(take, g, top_g)
        top_p = jnp.where(take, grp_pair[g], top_p)
    picked = []
    for e in range(N_EXPERTS):
        g, k = divmod(e, EXPERTS_PER_GROUP)
        in_pair = None
        for pi, (a, b) in enumerate(pairs):
            if k in (a, b):
                hit = top_p == pi
                in_pair = hit if in_pair is None else (in_pair | hit)
        picked.append(jnp.where((top_g == g) & in_pair, aff[e:e + 1, :], 0.0))
    denom = picked[0]
    for e in range(1, N_EXPERTS):
        denom = denom + picked[e]
    return [pk / denom for pk in picked]


def _swiglu(hn, w1, w3, w2, scale):
    a = _dot(hn, w1.astype(BF16))
    h = (a * jax.nn.sigmoid(a)) * _dot(hn, w3.astype(BF16))
    if scale is not None:
        h = h * scale
    return _dot(h.astype(BF16), w2.astype(BF16))


def _moe_kernel(x_ref, ml_ref, mc_ref, g2_ref, gp_ref, rw_ref, rb_ref, ws1_ref, ws3_ref, ws2_ref,
                w1_ref, w3_ref, w2_ref, o_ref, hn_ref, comb_t_ref, comb_ref, acc_ref, *, tm, rb, n_ctx):
    i = pl.program_id(1)
    e = pl.program_id(2)

    @pl.when(e == 0)
    def _():
        def blk(r, carry):
            r0 = pl.multiple_of(r * rb, rb)
            x = x_ref[0, pl.ds(r0, rb), :]
            hn = _norm_modulate(x, g2_ref[...], i * tm + r0 < n_ctx, mc_ref, ml_ref, 3, 4)
            hn_ref[pl.ds(r0, rb), :] = hn.astype(BF16)
            return carry

        lax.fori_loop(0, tm // rb, blk, 0)
        hn = hn_ref[...]
        aff = jax.nn.sigmoid(_dot_nt(rw_ref[...], hn))
        comb = _route(aff + rb_ref[...], aff)
        comb_t_ref[...] = jnp.zeros_like(comb_t_ref)
        for k in range(N_EXPERTS):
            comb_t_ref[k:k + 1, :] = comb[k]
        comb_ref[...] = comb_t_ref[...].T
        acc_ref[...] = _swiglu(hn, ws1_ref[0], ws3_ref[0], ws2_ref[0], None)

    @pl.when(e > 0)
    def _():
        lane = lax.broadcasted_iota(jnp.int32, (1, 128), 1)
        comb = comb_ref[...]
        hn = hn_ref[...]
        first = 2 * (e - 1)
        y = None
        for k in range(2):
            c_k = jnp.sum(jnp.where(lane == first + k, comb, 0.0), axis=-1, keepdims=True)
            term = _swiglu(hn, w1_ref[0, k], w3_ref[0, k], w2_ref[0, k], c_k)
            y = term if y is None else y + term
        acc_ref[...] += y

    @pl.when(e == pl.num_programs(2) - 1)
    def _():
        y = acc_ref[...]
        r = y * lax.rsqrt(jnp.mean(y * y, axis=-1, keepdims=True) + EPS) * gp_ref[...]
        rows = lax.broadcasted_iota(jnp.int32, (tm, 1), 0) + i * tm
        gate = jnp.where(rows < n_ctx, mc_ref[0, 5:6, :], ml_ref[0, 5:6, :])
        o_ref[0] = x_ref[0] + gate * r


def _moe(xa, mod, g2, gp, rw_t, rbias, ws1, ws3, ws2, w1, w3, w2, layer, n_ctx):
    nb, nt, d = xa.shape
    tm = _pick(nt, (1024, 768, 512, 384, 256, 128))
    n_pairs = w1.shape[1] // 2
    row = lambda b, i, e: (b, i, 0)
    c2 = lambda b, i, e: (0, 0)
    shared_blk = lambda b, i, e: (layer, 0, 0)
    pair_blk = lambda b, i, e: (layer, jnp.maximum(e - 1, 0), 0, 0)
    return pl.pallas_call(
        functools.partial(_moe_kernel, tm=tm, rb=128, n_ctx=n_ctx),
        grid=(nb, nt // tm, n_pairs + 1),
        in_specs=[pl.BlockSpec((1, tm, d), row),
                  pl.BlockSpec((1, 6, d), lambda b, i, e: (b, 0, 0)),
                  pl.BlockSpec((1, 6, d), lambda b, i, e: (nb, 0, 0)),
                  pl.BlockSpec((1, d), c2),
                  pl.BlockSpec((1, d), c2),
                  pl.BlockSpec((N_EXPERTS, d), c2),
                  pl.BlockSpec((N_EXPERTS, 1), c2),
                  pl.BlockSpec((1, d, D_EXPERT), shared_blk),
                  pl.BlockSpec((1, d, D_EXPERT), shared_blk),
                  pl.BlockSpec((1, D_EXPERT, d), shared_blk),
                  pl.BlockSpec((1, 2, d, D_EXPERT), pair_blk),
                  pl.BlockSpec((1, 2, d, D_EXPERT), pair_blk),
                  pl.BlockSpec((1, 2, D_EXPERT, d), pair_blk)],
        out_specs=pl.BlockSpec((1, tm, d), row),
        out_shape=jax.ShapeDtypeStruct((nb, nt, d), F32),
        scratch_shapes=[pltpu.VMEM((tm, d), BF16), pltpu.VMEM((128, tm), F32), pltpu.VMEM((tm, 128), F32),
                        pltpu.VMEM((tm, d), F32)],
        compiler_params=_params("arbitrary", "arbitrary", "arbitrary"),
        name="moe",
    )(xa, mod, mod, g2, gp, rw_t, rbias, ws1, ws3, ws2, w1, w3, w2)


def _swap_perm(width, group):
    j = np.arange(width)
    src = np.where((j % group) < group // 2, j + group // 2, j - group // 2)
    return jnp.asarray(np.arange(width)[:, None] == src[None, :], BF16)


def _rope_tables(n_lat, n_ctx):
    def angles(pos, dim):
        half = dim // 2
        inv = ROPE_BASE ** (-jnp.arange(half, dtype=F32) / half)
        return pos.astype(F32)[:, None] * inv[None, :]

    def tables(cos_parts, sin_parts, reps):
        cos = jnp.tile(jnp.concatenate(cos_parts, axis=-1), (1, reps))
        sin = jnp.tile(jnp.concatenate(sin_parts, axis=-1), (1, reps))
        w = cos.shape[1]
        return (jnp.concatenate([jnp.ones((n_ctx, w), F32), cos], axis=0),
                jnp.concatenate([jnp.zeros((n_ctx, w), F32), sin], axis=0))

    rows = n_lat // GRID_W
    ang_t = angles(jnp.arange(n_lat), HEAD_DIM)
    ang_r = angles(jnp.repeat(jnp.arange(rows), GRID_W), ROPE_DIM // 2)
    ang_c = angles(jnp.tile(jnp.arange(GRID_W), rows), ROPE_DIM // 2)
    ct, st = jnp.cos(ang_t), jnp.sin(ang_t)
    ret = tables([ct, ct], [-st, st], N_HEADS)
    cr, sr, cc, sc = jnp.cos(ang_r), jnp.sin(ang_r), jnp.cos(ang_c), jnp.sin(ang_c)
    mla = tables([cr, cr, cc, cc], [-sr, sr, -sc, sc], N_HEADS)
    return ret, mla


def _ret_tables(logit):
    log_g = jax.nn.log_sigmoid(logit.astype(F32))
    lane_lg = jnp.repeat(log_g, HEAD_DIM, axis=1)
    idx = jnp.arange(CHUNK, dtype=F32)[:, None]
    kd = jnp.stack([jnp.exp(lane_lg[0][None, :] * (CHUNK - 1 - idx)), jnp.exp(lane_lg[1][None, :] * idx)])
    qd = jnp.stack([jnp.exp(lane_lg[0][None, :] * (idx + 1)), jnp.exp(lane_lg[1][None, :] * (CHUNK - idx))])
    cd = jnp.exp(lane_lg * CHUNK)[:, None, :]
    diff = idx - idx.T
    blocks = []
    for h in range(N_HEADS):
        f = jnp.exp(log_g[0, h] * jnp.where(diff >= 0, diff, 0.0))
        b = jnp.exp(log_g[1, h] * jnp.where(diff < 0, -diff, 0.0))
        blocks.append(jnp.where(diff >= 0, f, b))
    dm = jnp.concatenate(blocks, axis=1)
    return kd, cd, qd, dm


def _pack_w_in(w_in):
    d = w_in.shape[0]
    mla = jnp.concatenate([w_in[:, OFF_MLA:OFF_MLA + MLA_COLS], jnp.zeros((d, MLA_PAD - MLA_COLS), w_in.dtype)], 1)
    w = jnp.concatenate([w_in[:, OFF_RET:OFF_RET + RET_COLS], w_in[:, OFF_DN:OFF_DN + 4 * MIX_W],
                         w_in[:, OFF_SG:OFF_SG + SG_COLS], mla, w_in[:, OFF_GATE:OFF_GATE + GATE_COLS]], axis=1)
    wab = w_in[:, OFF_DN + 4 * MIX_W:OFF_DN + DN_COLS]
    wabc = jnp.concatenate([wab, jnp.zeros((d, 128 - 4 * N_HEADS), w_in.dtype)], axis=1)
    return w.astype(BF16), wab.T.astype(BF16), wabc.astype(BF16)


def _mla_weights(w_uq, w_ukv):
    dq = NOPE_DIM + ROPE_DIM
    dkv = NOPE_DIM + V_DIM
    wq = w_uq.reshape(Q_LORA, N_HEADS, dq)
    wqn = wq[:, :, :NOPE_DIM].reshape(Q_LORA, N_HEADS * NOPE_DIM)
    wqr = wq[:, :, NOPE_DIM:].reshape(Q_LORA, N_HEADS * ROPE_DIM)
    wkv = w_ukv.reshape(KV_LORA, N_HEADS, dkv)
    head_eye = jnp.eye(N_HEADS, dtype=F32)
    wa = jnp.einsum("chd,hg->hcgd", wkv[:, :, :NOPE_DIM], head_eye).reshape(N_HEADS, KV_LORA, N_HEADS * NOPE_DIM)
    wa = jnp.pad(wa, ((0, 0), (0, QK_W - KV_LORA), (0, 0)))
    wuv = jnp.einsum("chd,hg->hcgd", wkv[:, :, NOPE_DIM:], head_eye).reshape(N_HEADS, KV_LORA, MIX_W)
    selq = np.zeros((N_HEADS, QK_W, N_HEADS * ROPE_DIM), np.float32)
    for h in range(N_HEADS):
        selq[h, KV_LORA:KV_LORA + ROPE_DIM, h * ROPE_DIM:(h + 1) * ROPE_DIM] = np.eye(ROPE_DIM)
    selc = np.zeros((KV_LORA, QK_W), np.float32)
    selc[:, 0:KV_LORA] = np.eye(KV_LORA)
    selr = np.zeros((128, QK_W), np.float32)
    selr[0:ROPE_DIM, KV_LORA:KV_LORA + ROPE_DIM] = np.eye(ROPE_DIM)
    selv = np.zeros((VT_ROWS, KV_LORA), np.float32)
    selv[0:KV_LORA, :] = np.eye(KV_LORA)
    one_col = np.zeros((VT_ROWS, 1), np.float32)
    one_col[KV_LORA, 0] = 1.0
    return (tuple(jnp.asarray(a, BF16) for a in (wqn, wqr, wa, selq, selc, selr, selv))
            + (jnp.asarray(one_col), wuv.astype(BF16)))


def kernel(x, c, ctx, c_ctx, w_ada, b_ada, g_pre1, g_post1, g_pre2, g_post2, w_in, ret_decay_logit, sg_norm_g, sg_w, sg_b, dn_conv_w, dn_A_log, dn_dt_bias, dn_norm_g, mla_q_norm_g, mla_kv_norm_g, mla_w_uq, mla_w_ukv, w_branch, w_out, router_w, router_bias, moe_w1, moe_w3, moe_w2, shared_w1, shared_w3, shared_w2):
    nb, n_lat, d = x.shape
    n_ctx = ctx.shape[1]
    depth = w_in.shape[0]
    assert d == D_MODEL and n_lat % GRID_W == 0 and n_lat % CHUNK == 0 and n_ctx % 256 == 0
    ncc = n_ctx // CHUNK

    n_cond = -(-(nb + 1) // 8) * 8
    cond = jnp.concatenate([c, c_ctx[None], jnp.zeros((n_cond - nb - 1, d), F32)], axis=0)
    mod_all = _adaln(cond, w_ada, b_ada).reshape(depth, n_cond, 6, d)

    (ret_cos, ret_sin), (mla_cos, mla_sin) = _rope_tables(n_lat, n_ctx)
    perm_ret = _swap_perm(MIX_W, HEAD_DIM)
    perm_mla = _swap_perm(N_HEADS * ROPE_DIM, ROPE_DIM // 2)
    lane_head = jnp.arange(MIX_W) // HEAD_DIM
    bd = (lane_head[:, None] == lane_head[None, :]).astype(F32)
    ones_bd = bd.astype(BF16)
    rw_t = router_w.T.astype(BF16)
    rbias = router_bias.astype(F32)[:, None]

    xa = (ctx, x)
    for l in range(depth):
        last = l == depth - 1
        if last and isinstance(xa, tuple):
            xa = jnp.concatenate(xa, axis=1)
        mod = mod_all[l]
        w_l, wab_l, wabc_l = _pack_w_in(w_in[l])
        p, ab_t, ab_c = _inproj(xa, n_ctx + n_lat, mod, g_pre1[l][None], w_l, wab_l, wabc_l, n_ctx)

        kd, cd, qd, dm = _ret_tables(ret_decay_logit[l])
        wcat = jnp.concatenate([sg_w[l, h] for h in range(N_HEADS)], axis=1).astype(BF16)
        sg_bias = jnp.repeat(sg_b[l].T, HEAD_DIM, axis=1)
        y_ret, y_sg = _retention_and_sgate(p, ret_cos, ret_sin, perm_ret, (kd, cd, qd, dm, bd, ones_bd),
                                           sg_norm_g[l][None], wcat, sg_bias, ncc)

        neg_a = (-jnp.exp(dn_A_log[l].astype(F32))).reshape(2 * N_HEADS, 1)
        dtb = dn_dt_bias[l].astype(F32).reshape(2 * N_HEADS, 1)
        conv_w = jnp.concatenate([dn_conv_w[l], jnp.zeros((8 - CONV_W, 3 * MIX_W), F32)], axis=0)
        o_f, o_b = _deltanet(p, ab_t, ab_c, conv_w, neg_a, dtb, bd, ones_bd, ncc)

        row0 = n_ctx if last else 0
        y_mla, y_mla_ctx = _mla(p, mla_cos, mla_sin, perm_mla, mla_q_norm_g[l][None], mla_kv_norm_g[l][None],
                                *_mla_weights(mla_w_uq[l], mla_w_ukv[l]), n_ctx, not last)
        if not last:
            y_mla = jnp.concatenate([y_mla_ctx, y_mla], axis=1)

        xa = _merge(xa, y_ret, y_sg, o_f, o_b, y_mla, p, w_branch[l].astype(BF16), w_out[l].astype(BF16),
                    jnp.tile(dn_norm_g[l], N_HEADS)[None], g_post1[l][None], mod, ones_bd, n_ctx, row0)

        xa = _moe(xa, mod, g_pre2[l][None], g_post2[l][None], rw_t, rbias, shared_w1, shared_w3, shared_w2,
                  moe_w1, moe_w3, moe_w2, l, n_ctx - row0)
    return xa
```

```python
import functools
import math

import jax
import jax.numpy as jnp
import numpy as np
from jax import lax
from jax.experimental import pallas as pl
from jax.experimental.pallas import tpu as pltpu

F32 = jnp.float32
BF16 = jnp.bfloat16
HIGHEST = lax.Precision.HIGHEST

D_MODEL = 1024
GRID_W = 64
N_HEADS = 4
HEAD_DIM = 64
MIX_W = N_HEADS * HEAD_DIM
CHUNK = 128
ROPE_BASE = 10000.0
EPS = 1e-6
RET_DECAY_EXP0 = 5.0
CONV_W = 5
Q_LORA = 256
KV_LORA = 128
NOPE_DIM = 64
ROPE_DIM = 32
V_DIM = 64
N_EXPERTS = 16
N_GROUPS = 4
EXPERTS_PER_GROUP = N_EXPERTS // N_GROUPS
D_EXPERT = 256
N_BRANCH = 4

RET_COLS = 4 * MIX_W
SG_COLS = 2 * MIX_W
DN_COLS = 4 * MIX_W + 4 * N_HEADS
MLA_COLS = Q_LORA + KV_LORA + ROPE_DIM
GATE_COLS = N_BRANCH * D_MODEL
OFF_RET = 0
OFF_SG = OFF_RET + RET_COLS
OFF_DN = OFF_SG + SG_COLS
OFF_MLA = OFF_DN + DN_COLS
OFF_GATE = OFF_MLA + MLA_COLS

P_RET = 0
P_DN = 1024
P_SG = 2048
P_MLA = 2560
P_GATE = 3072
P_COLS = 7168
MLA_PAD = 512

VMEM_LIMIT = 56 * 1024 * 1024


def _dot(a, b, precision=None):
    return jnp.dot(a, b, preferred_element_type=F32, precision=precision)


def _dot_nt(a, b, precision=None):
    return lax.dot_general(a, b, (((1,), (1,)), ((), ())), preferred_element_type=F32, precision=precision)


def _dot_tn(a, b):
    return lax.dot_general(a, b, (((0,), (0,)), ((), ())), preferred_element_type=F32)


def _mm(a, b):
    return _dot(a.astype(BF16), b.astype(BF16))


def _params(*sem):
    return pltpu.CompilerParams(dimension_semantics=sem, vmem_limit_bytes=VMEM_LIMIT)


def _pick(n, cands):
    for c in cands:
        if n % c == 0:
            return c
    raise ValueError(f"no tile for {n}")


def _head_of_lane(width, group):
    return lax.broadcasted_iota(jnp.int32, (1, width), 1) // group


def _stack_heads(x):
    head = _head_of_lane(MIX_W, HEAD_DIM)
    xf = x.astype(F32)
    return jnp.concatenate([jnp.where(head == h, xf, 0.0).astype(BF16) for h in range(N_HEADS)], axis=0)


def _expand_heads(cols):
    head = _head_of_lane(MIX_W, HEAD_DIM)
    out = cols[:, N_HEADS - 1:N_HEADS]
    for h in range(N_HEADS - 2, -1, -1):
        out = jnp.where(head <= h, cols[:, h:h + 1], out)
    return out


def _head_sum(x, ones_bd):
    hi = x.astype(BF16)
    lo = (x - hi.astype(F32)).astype(BF16)
    return _dot(hi, ones_bd) + _dot(lo, ones_bd)


def _rot(x_bf, cos, sin, perm):
    return x_bf.astype(F32) * cos + _dot(x_bf, perm) * sin


def _norm_modulate(x, g, is_ctx, mc_ref, ml_ref, shift_row, scale_row):
    shift = jnp.where(is_ctx, mc_ref[0, shift_row:shift_row + 1, :], ml_ref[0, shift_row:shift_row + 1, :])
    scale = jnp.where(is_ctx, mc_ref[0, scale_row:scale_row + 1, :], ml_ref[0, scale_row:scale_row + 1, :])
    gain = g * (1.0 + scale)
    return x * lax.rsqrt(jnp.mean(x * x, axis=-1, keepdims=True) + EPS) * gain + shift


def _adaln_kernel(c_ref, w_ref, b_ref, o_ref):
    c = c_ref[...]
    s = c * jax.nn.sigmoid(c)
    o_ref[0] = _dot(s, w_ref[0], precision=HIGHEST) + b_ref[0]


def _adaln(cond, w_ada, b_ada):
    n_l, d, d6 = w_ada.shape
    r = cond.shape[0]
    tn = 1024
    return pl.pallas_call(
        _adaln_kernel,
        grid=(n_l, d6 // tn),
        in_specs=[pl.BlockSpec((r, d), lambda l, j: (0, 0)),
                  pl.BlockSpec((1, d, tn), lambda l, j: (l, 0, j)),
                  pl.BlockSpec((1, 1, tn), lambda l, j: (l, 0, j))],
        out_specs=pl.BlockSpec((1, r, tn), lambda l, j: (l, 0, j)),
        out_shape=jax.ShapeDtypeStruct((n_l, r, d6), F32),
        compiler_params=_params("arbitrary", "arbitrary"),
        name="adaln",
    )(cond, w_ada, b_ada.reshape(n_l, 1, d6))


def _inproj_kernel(x_ref, ctx_ref, ml_ref, mc_ref, g_ref, w_ref, wab_ref, wabc_ref, p_ref, ab_ref, abc_ref, xn_ref,
                   *, tm, rb, n_ctx, split):
    i = pl.program_id(1)
    j = pl.program_id(2)

    @pl.when(j == 0)
    def _():
        def blk(r, carry):
            r0 = pl.multiple_of(r * rb, rb)
            is_ctx = i * tm + r0 < n_ctx
            if split:
                x_off = pl.multiple_of(jnp.maximum(jnp.where(i == 0, r0 - n_ctx, r0), 0), rb)
                c_off = pl.multiple_of(jnp.minimum(r0, n_ctx - rb), rb)
                x = jnp.where(is_ctx, ctx_ref[0, pl.ds(c_off, rb), :], x_ref[0, pl.ds(x_off, rb), :])
            else:
                x = x_ref[0, pl.ds(r0, rb), :]
            hn = _norm_modulate(x, g_ref[...], is_ctx, mc_ref, ml_ref, 0, 1)
            xn_ref[pl.ds(r0, rb), :] = hn.astype(BF16)
            return carry

        lax.fori_loop(0, tm // rb, blk, 0)
        ab_ref[0] = _dot_nt(wab_ref[...], xn_ref[...])
        abc_ref[0] = _dot(xn_ref[...], wabc_ref[...])

    p_ref[0] = _dot(xn_ref[...], w_ref[...]).astype(BF16)


def _token_sources(tokens, n_ctx, tm):
    if isinstance(tokens, tuple):
        ctx, x = tokens
        d = x.shape[-1]
        if x.shape[1] >= tm and tm > n_ctx:
            x_spec = pl.BlockSpec(
                (pl.Element(1), pl.Element(tm), pl.Element(d)),
                lambda b, i, *_: (b, pl.multiple_of(jnp.maximum(i * tm - n_ctx, 0), 128), 0))
            ctx_spec = pl.BlockSpec((1, n_ctx, d), lambda b, i, *_: (b, 0, 0))
            return (x, ctx), (x_spec, ctx_spec), True
        tokens = jnp.concatenate([ctx, x], axis=1)
    nb, _, d = tokens.shape
    dummy = jnp.zeros((nb, 8, d), tokens.dtype)
    return ((tokens, dummy), (pl.BlockSpec((1, tm, d), lambda b, i, *_: (b, i, 0)),
                              pl.BlockSpec((1, 8, d), lambda b, i, *_: (b, 0, 0))), False)


def _inproj(tokens, nt, mod, g, w, wab, wabc, n_ctx):
    tm = _pick(nt, (1408, 768, 384, 256, 128))
    tn = 1792
    (xa, ctx), (x_spec, ctx_spec), split = _token_sources(tokens, n_ctx, tm)
    nb, _, d = xa.shape
    kern = functools.partial(_inproj_kernel, tm=tm, rb=128, n_ctx=n_ctx, split=split)
    return pl.pallas_call(
        kern,
        grid=(nb, nt // tm, P_COLS // tn),
        in_specs=[x_spec, ctx_spec,
                  pl.BlockSpec((1, 6, d), lambda b, i, j: (b, 0, 0)),
                  pl.BlockSpec((1, 6, d), lambda b, i, j: (nb, 0, 0)),
                  pl.BlockSpec((1, d), lambda b, i, j: (0, 0)),
                  pl.BlockSpec((d, tn), lambda b, i, j: (0, j)),
                  pl.BlockSpec((16, d), lambda b, i, j: (0, 0)),
                  pl.BlockSpec((d, 128), lambda b, i, j: (0, 0))],
        out_specs=[pl.BlockSpec((1, tm, tn), lambda b, i, j: (b, i, j)),
                   pl.BlockSpec((1, 16, tm), lambda b, i, j: (b, 0, i)),
                   pl.BlockSpec((1, tm, 128), lambda b, i, j: (b, i, 0))],
        out_shape=[jax.ShapeDtypeStruct((nb, nt, P_COLS), BF16),
                   jax.ShapeDtypeStruct((nb, 16, nt), F32),
                   jax.ShapeDtypeStruct((nb, nt, 128), F32)],
        scratch_shapes=[pltpu.VMEM((tm, d), BF16)],
        compiler_params=_params("arbitrary", "arbitrary", "arbitrary"),
        name="inproj",
    )(xa, ctx, mod, mod, g, w, wab, wabc)


def _bwd_chunk(t, ncc, nc):
    return jnp.where(t < ncc, ncc - 1 - t, nc - 1 - (t - ncc))


def _ret_state_kernel(pf_ref, pb_ref, cf_ref, sf_ref, cb_ref, sb_ref, perm_ref, kd_ref, cd_ref, bd_ref,
                      of_ref, ob_ref, st_f, st_b, *, cb):
    t = pl.program_id(1)

    @pl.when(t == 0)
    def _():
        st_f[...] = jnp.zeros_like(st_f)
        st_b[...] = jnp.zeros_like(st_b)

    def increments(p_ref, c_ref, s_ref, d):
        out = []
        for i in range(cb):
            r = slice(i * CHUNK, (i + 1) * CHUNK)
            kr = _rot(p_ref[0, r, MIX_W:2 * MIX_W], c_ref[r, :], s_ref[r, :], perm_ref[...]) * (HEAD_DIM ** -0.5)
            out.append(bd_ref[...] * _dot_tn((kr * kd_ref[d]).astype(BF16), p_ref[0, r, 2 * MIX_W:3 * MIX_W]))
        return out

    inc_f = increments(pf_ref, cf_ref, sf_ref, 0)
    inc_b = increments(pb_ref, cb_ref, sb_ref, 1)
    s = st_f[...]
    for i in range(cb):
        of_ref[0, i] = s.astype(BF16)
        s = cd_ref[0] * s + inc_f[i]
    st_f[...] = s
    s = st_b[...]
    for i in reversed(range(cb)):
        ob_ref[0, i] = s.astype(BF16)
        s = cd_ref[1] * s + inc_b[i]
    st_b[...] = s


def _gelu_tanh(x):
    return 0.5 * x * (1.0 + jnp.tanh(math.sqrt(2.0 / math.pi) * (x + 0.044715 * (x * x * x))))


def _mix_out_kernel(p_ref, pg_ref, c_ref, s_ref, sf_ref, sb_ref, perm_ref, dm_ref, qd_ref, ones_ref,
                    ng_ref, wg_ref, bg_ref, y_ref, ysg_ref, *, cb):
    chunks = range(cb)
    rows = [slice(i * CHUNK, (i + 1) * CHUNK) for i in chunks]
    perm, dm, ones_bd = perm_ref[...], dm_ref[...], ones_ref[...]
    p = [p_ref[0, r, :] for r in rows]
    cos = [c_ref[r, :] for r in rows]
    sin = [s_ref[r, :] for r in rows]
    qr = [_rot(p[i][:, 0:MIX_W], cos[i], sin[i], perm) for i in chunks]
    kr = [_rot(p[i][:, MIX_W:2 * MIX_W], cos[i], sin[i], perm) * (HEAD_DIM ** -0.5) for i in chunks]
    z = [_gelu_tanh(pg_ref[0, r, :].astype(F32)) for r in rows]
    vg = [x[:, MIX_W:] for x in z]
    mu_g = [jnp.mean(x, axis=-1, keepdims=True) for x in vg]
    vgc = [x - m for x, m in zip(vg, mu_g)]
    var_g = [jnp.mean(x * x, axis=-1, keepdims=True) for x in vgc]
    vn = [x * lax.rsqrt(s + EPS) * ng_ref[...] for x, s in zip(vgc, var_g)]
    sc = [_dot_nt(qr[i].astype(BF16), _stack_heads(kr[i])) * dm for i in chunks]
    mixed = [_dot(wg_ref[...], _stack_heads(x)) for x in vn]
    o = [_dot(sc[i].astype(BF16), _stack_heads(p[i][:, 2 * MIX_W:3 * MIX_W])) for i in chunks]
    qs = [jnp.concatenate([(qr[i] * qd_ref[0]).astype(BF16), (qr[i] * qd_ref[1]).astype(BF16)], axis=1)
          for i in chunks]
    ss = [jnp.concatenate([sf_ref[0, i], sb_ref[0, i]], axis=0) for i in chunks]
    o = [o[i] + _dot(qs[i], ss[i]) for i in chunks]
    for i in chunks:
        ysg_ref[0, rows[i], :] = (z[i][:, :MIX_W] * (mixed[i] + bg_ref[...])).astype(BF16)
    mu = [_head_sum(x, ones_bd) * (1.0 / HEAD_DIM) for x in o]
    oc = [x - m for x, m in zip(o, mu)]
    var = [_head_sum(x * x, ones_bd) * (1.0 / HEAD_DIM) for x in oc]
    for i in chunks:
        g = p[i][:, 3 * MIX_W:4 * MIX_W].astype(F32)
        y_ref[0, rows[i], :] = (oc[i] * lax.rsqrt(var[i] + EPS) * (g * jax.nn.sigmoid(g))).astype(BF16)


def _retention_and_sgate(p, cos, sin, perm, tabs, sg_ng, sg_w, sg_bias, ncc):
    nb, nt, _ = p.shape
    nc = nt // CHUNK
    kd, cd, qd, dm, bd, ones_bd = tabs
    cb = 2
    assert nc % cb == 0 and ncc % cb == 0
    nblk, ncb = nc // cb, ncc // cb
    fwd = lambda b, t: (b, t, 0)
    bwd = lambda b, t: (b, _bwd_chunk(t, ncb, nblk), 0)
    tab_f = lambda b, t: (t, 0)
    tab_b = lambda b, t: (_bwd_chunk(t, ncb, nblk), 0)
    c2 = lambda b, t: (0, 0)
    c3 = lambda b, t: (0, 0, 0)
    st_shape = jax.ShapeDtypeStruct((nb, nc, MIX_W, MIX_W), BF16)
    st_f, st_b = pl.pallas_call(
        functools.partial(_ret_state_kernel, cb=cb),
        grid=(nb, nblk),
        in_specs=[pl.BlockSpec((1, cb * CHUNK, RET_COLS), fwd),
                  pl.BlockSpec((1, cb * CHUNK, RET_COLS), bwd),
                  pl.BlockSpec((cb * CHUNK, MIX_W), tab_f), pl.BlockSpec((cb * CHUNK, MIX_W), tab_f),
                  pl.BlockSpec((cb * CHUNK, MIX_W), tab_b), pl.BlockSpec((cb * CHUNK, MIX_W), tab_b),
                  pl.BlockSpec((MIX_W, MIX_W), c2),
                  pl.BlockSpec((2, CHUNK, MIX_W), c3),
                  pl.BlockSpec((2, 1, MIX_W), c3),
                  pl.BlockSpec((MIX_W, MIX_W), c2)],
        out_specs=[pl.BlockSpec((1, cb, MIX_W, MIX_W), lambda b, t: (b, t, 0, 0)),
                   pl.BlockSpec((1, cb, MIX_W, MIX_W), lambda b, t: (b, _bwd_chunk(t, ncb, nblk), 0, 0))],
        out_shape=[st_shape, st_shape],
        scratch_shapes=[pltpu.VMEM((MIX_W, MIX_W), F32), pltpu.VMEM((MIX_W, MIX_W), F32)],
        compiler_params=_params("arbitrary", "arbitrary"),
        name="ret_state",
    )(p, p, cos, sin, cos, sin, perm, kd, cd, bd)
    blk = lambda b, t: (b, t, 0)
    y_shape = jax.ShapeDtypeStruct((nb, nt, MIX_W), BF16)
    return pl.pallas_call(
        functools.partial(_mix_out_kernel, cb=cb),
        grid=(nb, nc // cb),
        in_specs=[pl.BlockSpec((1, cb * CHUNK, RET_COLS), blk),
                  pl.BlockSpec((1, cb * CHUNK, SG_COLS), lambda b, t: (b, t, P_SG // SG_COLS)),
                  pl.BlockSpec((cb * CHUNK, MIX_W), tab_f), pl.BlockSpec((cb * CHUNK, MIX_W), tab_f),
                  pl.BlockSpec((1, cb, MIX_W, MIX_W), lambda b, t: (b, t, 0, 0)),
                  pl.BlockSpec((1, cb, MIX_W, MIX_W), lambda b, t: (b, t, 0, 0)),
                  pl.BlockSpec((MIX_W, MIX_W), c2),
                  pl.BlockSpec((CHUNK, N_HEADS * CHUNK), c2),
                  pl.BlockSpec((2, CHUNK, MIX_W), c3),
                  pl.BlockSpec((MIX_W, MIX_W), c2),
                  pl.BlockSpec((1, MIX_W), c2),
                  pl.BlockSpec((CHUNK, N_HEADS * CHUNK), c2),
                  pl.BlockSpec((CHUNK, MIX_W), c2)],
        out_specs=[pl.BlockSpec((1, cb * CHUNK, MIX_W), blk), pl.BlockSpec((1, cb * CHUNK, MIX_W), blk)],
        out_shape=[y_shape, y_shape],
        compiler_params=_params("arbitrary", "arbitrary"),
        name="mix_out",
    )(p, p, cos, sin, st_f, st_b, perm, dm, qd, ones_bd, sg_ng, sg_w, sg_bias)


def _softplus(a):
    return jnp.maximum(a, 0.0) + jnp.log1p(jnp.exp(-jnp.abs(a)))


def _dn_prep_kernel(pc_ref, pp_ref, pn_ref, ab_ref, abc_ref, cw_ref, na_ref, dtb_ref, nar_ref, dtbr_ref, ones_ref,
                    qkv_ref, gb_ref, gbc_ref, xe_ref, *, rows, ctx_blocks, n_blocks):
    t = pl.program_id(1)
    w3 = 3 * MIX_W
    prev_ok = jnp.where((t != 0) & (t != ctx_blocks), 1.0, 0.0)
    next_ok = jnp.where((t != ctx_blocks - 1) & (t != n_blocks - 1), 1.0, 0.0)
    tail = pp_ref[0, rows - 16:rows, 0:w3].astype(F32)
    head = pn_ref[0, 0:16, 0:w3].astype(F32)
    xe_ref[0:8, :] = tail[8:16, :] * prev_ok
    xe_ref[8:8 + rows, :] = pc_ref[0, :, 0:w3].astype(F32)
    xe_ref[8 + rows:16 + rows, :] = head[0:8, :] * next_ok
    pad = CONV_W // 2
    y = xe_ref[8 - pad:8 - pad + rows, :] * cw_ref[0:1, :]
    for i in range(1, CONV_W):
        y = y + xe_ref[8 - pad + i:8 - pad + i + rows, :] * cw_ref[i:i + 1, :]
    y = y * jax.nn.sigmoid(y)
    q = y[:, 0:MIX_W]
    k = y[:, MIX_W:2 * MIX_W]
    v = y[:, 2 * MIX_W:w3]
    ones_bd = ones_ref[...]
    qn = q * lax.rsqrt(_head_sum(q * q, ones_bd) + EPS) * (HEAD_DIM ** -0.5)
    kn = k * lax.rsqrt(_head_sum(k * k, ones_bd) + EPS)
    qkv_ref[0, :, 0:MIX_W] = qn.astype(BF16)
    qkv_ref[0, :, MIX_W:2 * MIX_W] = kn.astype(BF16)
    qkv_ref[0, :, 2 * MIX_W:w3] = v.astype(BF16)
    ab = ab_ref[0]
    gb_ref[0, 0:8, :] = na_ref[...] * _softplus(ab[0:8, :] + dtb_ref[...])
    gb_ref[0, 8:16, :] = jax.nn.sigmoid(ab[8:16, :])
    abc = abc_ref[0]
    lane = lax.broadcasted_iota(jnp.int32, (1, 128), 1)
    g_c = nar_ref[...] * _softplus(abc + dtbr_ref[...])
    gbc_ref[0] = jnp.where(lane < 8, g_c, jnp.where(lane < 16, jax.nn.sigmoid(abc), 0.0))


def _split3(x):
    hi = x.astype(BF16)
    r = x - hi.astype(F32)
    mid = r.astype(BF16)
    lo = (r - mid.astype(F32)).astype(BF16)
    return hi, mid, lo


def _tri_inverse(mats, ii, jj):
    eye = jnp.where(ii == jj, 1.0, 0.0)
    nd = [jnp.where((ii // 16) == (jj // 16), n, 0.0) for n in mats]
    p1 = [_mm(x, x) for x in nd]
    m = [eye - x for x in nd]
    p2 = [_mm(x, x) for x in p1]
    m = [x + _mm(x, y) for x, y in zip(m, p1)]
    p3 = [_mm(x, x) for x in p2]
    m = [x + _mm(x, y) for x, y in zip(m, p2)]
    m = [x + _mm(x, y) for x, y in zip(m, p3)]
    for lvl in (16, 32, 64):
        off_mask = ((ii // (2 * lvl)) == (jj // (2 * lvl))) & ((ii // lvl) != (jj // lvl))
        t = [_mm(jnp.where(off_mask, n, 0.0), x) for n, x in zip(mats, m)]
        m = [x - _mm(x, y) for x, y in zip(m, t)]
    return m


def _dn_pre(qkv, g, gbc, d, lower):
    c = CHUNK
    qn = qkv[:, 0:MIX_W]
    kn = qkv[:, MIX_W:2 * MIX_W]
    v = qkv[:, 2 * MIX_W:3 * MIX_W]
    ii = lax.broadcasted_iota(jnp.int32, (c, c), 0)
    jj = lax.broadcasted_iota(jnp.int32, (c, c), 1)
    incl = (ii >= jj) if lower else (ii <= jj)
    tri = jnp.where(incl, 1.0, 0.0).astype(BF16)
    g_row = sum(_dot_nt(part, tri) for part in _split3(g))[N_HEADS * d:N_HEADS * (d + 1), :]
    cum = sum(_dot(tri, part) for part in _split3(gbc))
    g_col = cum[:, N_HEADS * d:N_HEADS * (d + 1)]
    b_col = gbc[:, 2 * N_HEADS + N_HEADS * d:2 * N_HEADS + N_HEADS * (d + 1)]
    g_cols4 = jnp.concatenate([jnp.broadcast_to(g_col[:, h:h + 1], (c, c)) for h in range(N_HEADS)], axis=1)
    b_cols4 = jnp.concatenate([jnp.broadcast_to(b_col[:, h:h + 1], (c, c)) for h in range(N_HEADS)], axis=1)
    g_rows4 = jnp.concatenate([g_row[h:h + 1, :] for h in range(N_HEADS)], axis=1)
    incl4 = jnp.concatenate([incl] * N_HEADS, axis=1)
    diag4 = jnp.concatenate([ii == jj] * N_HEADS, axis=1)
    decay = jnp.where(incl4, jnp.exp(jnp.where(incl4, g_cols4 - g_rows4, 0.0)), 0.0)
    kstack = _stack_heads(kn)
    kk = _dot_nt(kn, kstack)
    qk = _dot_nt(qn, kstack)
    n_mat = jnp.where(diag4, 0.0, decay * kk * b_cols4)
    attn = (decay * qk).astype(BF16)
    g256 = _expand_heads(g_col)
    eg256 = jnp.exp(g256)
    b256 = _expand_heads(b_col)
    vb = v.astype(F32) * b256
    kbg = kn.astype(F32) * b256 * eg256
    rhs = jnp.concatenate([_stack_heads(vb), _stack_heads(kbg)], axis=1)
    g_last = g256[c - 1:c, :] if lower else g256[0:1, :]
    kdec = (kn.astype(F32) * jnp.exp(g_last - g256)).astype(BF16)
    n_heads = [n_mat[:, h * c:(h + 1) * c] for h in range(N_HEADS)]
    return n_heads, dict(qn=qn, attn=attn, rhs=rhs, eg=eg256, kdec=kdec, sdec=jnp.exp(g_last))


def _dn_post(z, s_prev, bd):
    s_bf = s_prev.astype(BF16)
    w = z["u"] - _dot(z["wk"], s_bf)
    o = z["eg"] * _dot(z["qn"], s_bf) + _dot(z["attn"], _stack_heads(w))
    s_next = z["sdec"] * s_prev + bd * _dot_tn(z["kdec"], w.astype(BF16))
    return o, s_next


def _dn_scan_kernel(qf_ref, qb_ref, gf_ref, gb_ref, gcf_ref, gcb_ref, bd_ref, of_ref, ob_ref, st_f, st_b, *, cb):
    t = pl.program_id(1)

    @pl.when(t == 0)
    def _():
        st_f[...] = jnp.zeros_like(st_f)
        st_b[...] = jnp.zeros_like(st_b)

    bd = bd_ref[...]
    rows = [slice(i * CHUNK, (i + 1) * CHUNK) for i in range(cb)]
    mats, pres = [], []
    for d, (q_ref, g_ref, gc_ref) in enumerate(((qf_ref, gf_ref, gcf_ref), (qb_ref, gb_ref, gcb_ref))):
        for r in rows:
            n_heads, pre = _dn_pre(q_ref[0, r, :], g_ref[0, :, r], gc_ref[0, r, :], d, d == 0)
            mats += n_heads
            pres.append(pre)
    ii = lax.broadcasted_iota(jnp.int32, (CHUNK, CHUNK), 0)
    jj = lax.broadcasted_iota(jnp.int32, (CHUNK, CHUNK), 1)
    inv = _tri_inverse(mats, ii, jj)
    for n, pre in enumerate(pres):
        a_inv = jnp.concatenate(inv[N_HEADS * n:N_HEADS * (n + 1)], axis=1).astype(BF16)
        uw = _dot(a_inv, pre["rhs"])
        pre["u"] = uw[:, 0:MIX_W]
        pre["wk"] = uw[:, MIX_W:2 * MIX_W].astype(BF16)
    s_f, s_b = st_f[...], st_b[...]
    for k in range(cb):
        o, s_f = _dn_post(pres[k], s_f, bd)
        of_ref[0, rows[k], :] = o
        o, s_b = _dn_post(pres[cb + cb - 1 - k], s_b, bd)
        ob_ref[0, rows[cb - 1 - k], :] = o
    st_f[...] = s_f
    st_b[...] = s_b


def _deltanet(p, ab_t, ab_c, conv_w, neg_a, dtb, bd, ones_bd, ncc):
    nb, nt, _ = p.shape
    nc = nt // CHUNK
    w3 = 3 * MIX_W
    c2 = lambda b, t: (0, 0)
    dn_blk = P_DN // RET_COLS
    pad_lanes = lambda col: jnp.concatenate([col.reshape(1, -1), jnp.zeros((1, 128 - col.size), F32)], axis=1)
    pr = math.gcd(math.gcd(ncc * CHUNK, nt), 256)
    n_pb, ctx_pb = nt // pr, ncc * CHUNK // pr
    qkv, gbeta, gbeta_c = pl.pallas_call(
        functools.partial(_dn_prep_kernel, rows=pr, ctx_blocks=ctx_pb, n_blocks=n_pb),
        grid=(nb, n_pb),
        in_specs=[pl.BlockSpec((1, pr, 4 * MIX_W), lambda b, t: (b, t, dn_blk)),
                  pl.BlockSpec((1, pr, 4 * MIX_W), lambda b, t: (b, jnp.maximum(t - 1, 0), dn_blk)),
                  pl.BlockSpec((1, pr, 4 * MIX_W), lambda b, t: (b, jnp.minimum(t + 1, n_pb - 1), dn_blk)),
                  pl.BlockSpec((1, 16, pr), lambda b, t: (b, 0, t)),
                  pl.BlockSpec((1, pr, 128), lambda b, t: (b, t, 0)),
                  pl.BlockSpec((8, w3), c2),
                  pl.BlockSpec((8, 1), c2),
                  pl.BlockSpec((8, 1), c2),
                  pl.BlockSpec((1, 128), c2),
                  pl.BlockSpec((1, 128), c2),
                  pl.BlockSpec((MIX_W, MIX_W), c2)],
        out_specs=[pl.BlockSpec((1, pr, w3), lambda b, t: (b, t, 0)),
                   pl.BlockSpec((1, 16, pr), lambda b, t: (b, 0, t)),
                   pl.BlockSpec((1, pr, 128), lambda b, t: (b, t, 0))],
        out_shape=[jax.ShapeDtypeStruct((nb, nt, w3), BF16),
                   jax.ShapeDtypeStruct((nb, 16, nt), F32),
                   jax.ShapeDtypeStruct((nb, nt, 128), F32)],
        scratch_shapes=[pltpu.VMEM((pr + 16, w3), F32)],
        compiler_params=_params("arbitrary", "arbitrary"),
        name="dn_prep",
    )(p, p, p, ab_t, ab_c, conv_w, neg_a, dtb, pad_lanes(neg_a), pad_lanes(dtb), ones_bd)
    cb = 2
    assert nc % cb == 0 and ncc % cb == 0
    rows = cb * CHUNK
    cur_b = lambda t: _bwd_chunk(t, ncc // cb, nc // cb)
    o_shape = jax.ShapeDtypeStruct((nb, nt, MIX_W), F32)
    return pl.pallas_call(
        functools.partial(_dn_scan_kernel, cb=cb),
        grid=(nb, nc // cb),
        in_specs=[pl.BlockSpec((1, rows, w3), lambda b, t: (b, t, 0)),
                  pl.BlockSpec((1, rows, w3), lambda b, t: (b, cur_b(t), 0)),
                  pl.BlockSpec((1, 16, rows), lambda b, t: (b, 0, t)),
                  pl.BlockSpec((1, 16, rows), lambda b, t: (b, 0, cur_b(t))),
                  pl.BlockSpec((1, rows, 128), lambda b, t: (b, t, 0)),
                  pl.BlockSpec((1, rows, 128), lambda b, t: (b, cur_b(t), 0)),
                  pl.BlockSpec((MIX_W, MIX_W), c2)],
        out_specs=[pl.BlockSpec((1, rows, MIX_W), lambda b, t: (b, t, 0)),
                   pl.BlockSpec((1, rows, MIX_W), lambda b, t: (b, cur_b(t), 0))],
        out_shape=[o_shape, o_shape],
        scratch_shapes=[pltpu.VMEM((MIX_W, MIX_W), F32), pltpu.VMEM((MIX_W, MIX_W), F32)],
        compiler_params=_params("arbitrary", "arbitrary"),
        name="dn_scan",
    )(qkv, qkv, gbeta, gbeta, gbeta_c, gbeta_c, bd)


QK_W = 256


VT_ROWS = 144


def _mla_prep_kernel(p_ref, c_ref, s_ref, perm_ref, qg_ref, kg_ref, wqn_ref, wqr_ref, wa_ref, selq_ref, selc_ref,
                     selr_ref, selv_ref, one_ref, qt_ref, kv_ref, vt_ref, *, scale):
    p = p_ref[0]
    cos, sin, perm = c_ref[...], s_ref[...], perm_ref[...]
    cq = p[:, 0:Q_LORA].astype(F32)
    cqn = (cq * lax.rsqrt(jnp.mean(cq * cq, axis=-1, keepdims=True) + EPS) * qg_ref[...]).astype(BF16)
    q_nope = _dot(cqn, wqn_ref[...]).astype(BF16)
    q_rope = _dot(cqn, wqr_ref[...]).astype(BF16)
    q_rot = (_rot(q_rope, cos, sin, perm) * scale).astype(BF16)
    q_nope_s = (q_nope.astype(F32) * scale).astype(BF16)
    for h in range(N_HEADS):
        qt_ref[0, h] = (_dot_nt(wa_ref[h], q_nope_s) + _dot_nt(selq_ref[h], q_rot)).astype(BF16)
    ckv = p[:, Q_LORA:Q_LORA + KV_LORA].astype(F32)
    ckvn = (ckv * lax.rsqrt(jnp.mean(ckv * ckv, axis=-1, keepdims=True) + EPS) * kg_ref[...]).astype(BF16)
    kr = p[:, Q_LORA + KV_LORA:MLA_PAD]
    kr_rot = _rot(kr, cos, sin, perm).astype(BF16)
    kv_ref[0] = (_dot(ckvn, selc_ref[...]) + _dot(kr_rot, selr_ref[...])).astype(BF16)
    vt_ref[0] = (_dot_nt(selv_ref[...], ckvn) + one_ref[...]).astype(BF16)


def _mla_attn_kernel(qt_ref, kv_ref, vt_ref, wuv_ref, y_ref, m_ref, acc_ref, s_ref, *, tk, n_ctx, nt, latent):
    heads = range(N_HEADS)
    m_ref[...] = jnp.full_like(m_ref, -jnp.inf)
    acc_ref[...] = jnp.zeros_like(acc_ref)

    def scores(j0, size, slot):
        k = kv_ref[0, pl.ds(j0, size), :]
        for h in heads:
            s_ref[slot, h, 0:size, :] = _dot(k, qt_ref[0, h])

    def softmax_pv(j0, size, slot):
        vt = vt_ref[0, :, pl.ds(j0, size)]
        s = [s_ref[slot, h, 0:size, :] for h in heads]
        m_old = [m_ref[h] for h in heads]
        m_new = [jnp.maximum(m_old[h], jnp.max(s[h], axis=0, keepdims=True)) for h in heads]
        pr = [jnp.exp2(s[h] - m_new[h]).astype(BF16) for h in heads]
        pv = [_dot(vt, pr[h]) for h in heads]
        for h in heads:
            acc_ref[h] = jnp.exp2(m_old[h] - m_new[h]) * acc_ref[h] + pv[h]
            m_ref[h] = m_new[h]

    scores(0, n_ctx, 0)
    if not latent:
        softmax_pv(0, n_ctx, 0)
    else:
        n_tiles = (nt - n_ctx) // tk
        last = n_ctx + (n_tiles - 1) * tk
        scores(n_ctx, tk, 1)
        softmax_pv(0, n_ctx, 0)

        def body(jj, carry):
            t0 = pl.multiple_of(n_ctx + 2 * jj * tk, 256)
            t1 = pl.multiple_of(jnp.minimum(t0 + tk, last), 256)
            t2 = pl.multiple_of(jnp.minimum(t0 + 2 * tk, last), 256)
            scores(t1, tk, 0)
            softmax_pv(t0, tk, 1)
            scores(t2, tk, 1)
            softmax_pv(t1, tk, 0)
            return carry

        lax.fori_loop(0, n_tiles // 2, body, 0)
        if n_tiles % 2:
            softmax_pv(last, tk, 1)

    y = None
    for h in range(N_HEADS):
        acc = acc_ref[h]
        o = (acc[0:KV_LORA, :] / acc[KV_LORA:KV_LORA + 1, :]).astype(BF16)
        term = _dot_tn(o, wuv_ref[h])
        y = term if y is None else y + term
    y_ref[0] = y.astype(BF16)


def _mla(p, cos, sin, perm, qg, kg, wqn, wqr, wa, selq, selc, selr, selv, one_col, wuv, n_ctx, ctx_out):
    nb, nt, _ = p.shape
    n_lat = nt - n_ctx
    tm = _pick(nt, (768, 384, 256, 128))
    scale = (NOPE_DIM + ROPE_DIM) ** -0.5 * math.log2(math.e)
    c2 = lambda b, i: (0, 0)
    c3 = lambda b, i: (0, 0, 0)
    qt, kv, vt = pl.pallas_call(
        functools.partial(_mla_prep_kernel, scale=scale),
        grid=(nb, nt // tm),
        in_specs=[pl.BlockSpec((1, tm, MLA_PAD), lambda b, i: (b, i, P_MLA // MLA_PAD)),
                  pl.BlockSpec((tm, 128), lambda b, i: (i, 0)),
                  pl.BlockSpec((tm, 128), lambda b, i: (i, 0)),
                  pl.BlockSpec((128, 128), c2),
                  pl.BlockSpec((1, Q_LORA), c2),
                  pl.BlockSpec((1, KV_LORA), c2),
                  pl.BlockSpec((Q_LORA, N_HEADS * NOPE_DIM), c2),
                  pl.BlockSpec((Q_LORA, N_HEADS * ROPE_DIM), c2),
                  pl.BlockSpec((N_HEADS, QK_W, N_HEADS * NOPE_DIM), c3),
                  pl.BlockSpec((N_HEADS, QK_W, N_HEADS * ROPE_DIM), c3),
                  pl.BlockSpec((KV_LORA, QK_W), c2),
                  pl.BlockSpec((128, QK_W), c2),
                  pl.BlockSpec((VT_ROWS, KV_LORA), c2),
                  pl.BlockSpec((VT_ROWS, 1), c2)],
        out_specs=[pl.BlockSpec((1, N_HEADS, QK_W, tm), lambda b, i: (b, 0, 0, i)),
                   pl.BlockSpec((1, tm, QK_W), lambda b, i: (b, i, 0)),
                   pl.BlockSpec((1, VT_ROWS, tm), lambda b, i: (b, 0, i))],
        out_shape=[jax.ShapeDtypeStruct((nb, N_HEADS, QK_W, nt), BF16),
                   jax.ShapeDtypeStruct((nb, nt, QK_W), BF16),
                   jax.ShapeDtypeStruct((nb, VT_ROWS, nt), BF16)],
        compiler_params=_params("arbitrary", "arbitrary"),
        name="mla_prep",
    )(p, cos, sin, perm, qg, kg, wqn, wqr, wa, selq, selc, selr, selv, one_col)
    tk = _pick(n_lat, (512, 256))

    def attend(tq, first_col, n_q, latent):
        if first_col % tq == 0:
            q_spec = pl.BlockSpec((1, N_HEADS, QK_W, tq), lambda b, i: (b, 0, 0, i + first_col // tq))
        else:
            q_spec = pl.BlockSpec((pl.Element(1), pl.Element(N_HEADS), pl.Element(QK_W), pl.Element(tq)),
                                  lambda b, i: (b, 0, 0, pl.multiple_of(first_col + i * tq, 128)))
        return pl.pallas_call(
            functools.partial(_mla_attn_kernel, tk=tk, n_ctx=n_ctx, nt=nt, latent=latent),
            grid=(nb, n_q // tq),
            in_specs=[q_spec,
                      pl.BlockSpec((1, nt, QK_W), lambda b, i: (b, 0, 0)),
                      pl.BlockSpec((1, VT_ROWS, nt), lambda b, i: (b, 0, 0)),
                      pl.BlockSpec((N_HEADS, KV_LORA, MIX_W), c3)],
            out_specs=pl.BlockSpec((1, tq, MIX_W), lambda b, i: (b, i, 0)),
            out_shape=jax.ShapeDtypeStruct((nb, n_q, MIX_W), BF16),
            scratch_shapes=[pltpu.VMEM((N_HEADS, 1, tq), F32), pltpu.VMEM((N_HEADS, VT_ROWS, tq), F32),
                            pltpu.VMEM((2, N_HEADS, max(tk, n_ctx) if latent else n_ctx, tq), F32)],
            compiler_params=_params("arbitrary", "arbitrary"),
            name="mla_attn" if latent else "mla_attn_ctx",
        )(qt, kv, vt, wuv)

    y_lat = attend(_pick(n_lat, (512, 256)), n_ctx, n_lat, True)
    y_ctx = attend(_pick(n_ctx, (256, 128)), 0, n_ctx, False) if ctx_out else None
    return y_lat, y_ctx


def _merge_kernel(x_ref, ctx_ref, yr_ref, ys_ref, of_ref, ob_ref, ym_ref, z_ref, g0_ref, g1_ref, g2_ref, g3_ref,
                  wb_ref, wo_ref, ng_ref, gp_ref, ml_ref, mc_ref, ones_ref, o_ref, *, tm, n_ctx, row0, split):
    i = pl.program_id(1)
    if split:
        first = jnp.concatenate([ctx_ref[0], x_ref[0, 0:tm - n_ctx, :]], axis=0)
        x_res = jnp.where(i == 0, first, x_ref[0])
    else:
        x_res = x_ref[0]
    od = of_ref[0] + ob_ref[0]
    ms = _head_sum(od * od, ones_ref[...]) * (1.0 / HEAD_DIM)
    z = z_ref[0].astype(F32)
    ydn = (od * lax.rsqrt(ms + EPS) * ng_ref[...]) * (z * jax.nn.sigmoid(z))
    ys = (yr_ref[0], ys_ref[0], ydn.astype(BF16), ym_ref[0])
    gates = (g0_ref, g1_ref, g2_ref, g3_ref)
    acc = None
    for b in range(N_BRANCH):
        term = jax.nn.sigmoid(gates[b][0].astype(F32)) * _dot(ys[b], wb_ref[b])
        acc = term if acc is None else acc + term
    y = _dot(acc.astype(BF16), wo_ref[...])
    r = y * lax.rsqrt(jnp.mean(y * y, axis=-1, keepdims=True) + EPS) * gp_ref[...]
    rows = lax.broadcasted_iota(jnp.int32, (tm, 1), 0) + (row0 + i * tm)
    gate = jnp.where(rows < n_ctx, mc_ref[0, 2:3, :], ml_ref[0, 2:3, :])
    o_ref[0] = x_res + gate * r


def _merge(tokens, y_ret, y_sg, o_f, o_b, y_mla, p, wb, wo, ng, gp, mod, ones_bd, n_ctx, row0):
    nb, nt, _ = p.shape
    d = D_MODEL
    n_rows = nt - row0
    tm = _pick(n_rows, (768, 512, 384, 256, 128))
    c2 = lambda b, i: (0, 0)
    if row0 == 0:
        def window(width, col):
            return pl.BlockSpec((1, tm, width), lambda b, i: (b, i, col // width))

        (xa, ctx), (x_spec, ctx_spec), split = _token_sources(tokens, n_ctx, tm)
    else:
        def window(width, col):
            return pl.BlockSpec((pl.Element(1), pl.Element(tm), pl.Element(width)),
                                lambda b, i: (b, pl.multiple_of(row0 + i * tm, 128), col))

        xa, x_spec, split = tokens, window(d, 0), False
        ctx, ctx_spec = jnp.zeros((nb, 8, d), F32), pl.BlockSpec((1, 8, d), lambda b, i: (b, 0, 0))
    y_spec = window(MIX_W, 0)
    assert y_mla.shape[1] == n_rows
    mla_spec = pl.BlockSpec((1, tm, MIX_W), lambda b, i: (b, i, 0))
    gate_specs = [window(d, P_GATE + k * d) for k in range(N_BRANCH)]
    return pl.pallas_call(
        functools.partial(_merge_kernel, tm=tm, n_ctx=n_ctx, row0=row0, split=split),
        grid=(nb, n_rows // tm),
        in_specs=[x_spec, ctx_spec, y_spec, y_spec, y_spec, y_spec, mla_spec,
                  window(MIX_W, P_DN + 3 * MIX_W),
                  *gate_specs,
                  pl.BlockSpec((N_BRANCH, MIX_W, d), lambda b, i: (0, 0, 0)),
                  pl.BlockSpec((d, d), c2),
                  pl.BlockSpec((1, MIX_W), c2),
                  pl.BlockSpec((1, d), c2),
                  pl.BlockSpec((1, 6, d), lambda b, i: (b, 0, 0)),
                  pl.BlockSpec((1, 6, d), lambda b, i: (nb, 0, 0)),
                  pl.BlockSpec((MIX_W, MIX_W), c2)],
        out_specs=pl.BlockSpec((1, tm, d), lambda b, i: (b, i, 0)),
        out_shape=jax.ShapeDtypeStruct((nb, n_rows, d), F32),
        compiler_params=_params("arbitrary", "arbitrary"),
        name="merge",
    )(xa, ctx, y_ret, y_sg, o_f, o_b, y_mla, p, p, p, p, p, wb, wo, ng, gp, mod, mod, ones_bd)


def _route(sel, aff):
    rows = [sel[e:e + 1, :] for e in range(N_EXPERTS)]
    pairs = [(a, b) for a in range(EXPERTS_PER_GROUP) for b in range(a + 1, EXPERTS_PER_GROUP)]
    grp_score, grp_pair = [], []
    for g in range(N_GROUPS):
        base = g * EXPERTS_PER_GROUP
        best = rows[base + pairs[0][0]] + rows[base + pairs[0][1]]
        best_p = jnp.zeros_like(best, dtype=jnp.int32)
        for pi in range(1, len(pairs)):
            s = rows[base + pairs[pi][0]] + rows[base + pairs[pi][1]]
            take = s > best
            best = jnp.where(take, s, best)
            best_p = jnp.where(take, pi, best_p)
        grp_score.append(best)
        grp_pair.append(best_p)
    top = grp_score[0]
    top_g = jnp.zeros_like(grp_pair[0])
    top_p = grp_pair[0]
    for g in range(1, N_GROUPS):
        take = grp_score[g] > top
        top = jnp.where(take, grp_score[g], top)
        top_g = jnp.where(take, g, top_g)
        top_p = jnp.where(take, grp_pair[g], top_p)
    picked = []
    for e in range(N_EXPERTS):
        g, k = divmod(e, EXPERTS_PER_GROUP)
        in_pair = None
        for pi, (a, b) in enumerate(pairs):
            if k in (a, b):
                hit = top_p == pi
                in_pair = hit if in_pair is None else (in_pair | hit)
        picked.append(jnp.where((top_g == g) & in_pair, aff[e:e + 1, :], 0.0))
    denom = picked[0]
    for e in range(1, N_EXPERTS):
        denom = denom + picked[e]
    return [pk / denom for pk in picked]


def _swiglu(hn, w1, w3, w2, scale):
    a = _dot(hn, w1.astype(BF16))
    h = (a * jax.nn.sigmoid(a)) * _dot(hn, w3.astype(BF16))
    if scale is not None:
        h = h * scale
    return _dot(h.astype(BF16), w2.astype(BF16))


def _moe_kernel(x_ref, ml_ref, mc_ref, g2_ref, gp_ref, rw_ref, rb_ref, ws1_ref, ws3_ref, ws2_ref,
                w1_ref, w3_ref, w2_ref, o_ref, hn_ref, comb_t_ref, comb_ref, acc_ref, *, tm, rb, n_ctx):
    i = pl.program_id(1)
    e = pl.program_id(2)

    @pl.when(e == 0)
    def _():
        def blk(r, carry):
            r0 = pl.multiple_of(r * rb, rb)
            x = x_ref[0, pl.ds(r0, rb), :]
            hn = _norm_modulate(x, g2_ref[...], i * tm + r0 < n_ctx, mc_ref, ml_ref, 3, 4)
            hn_ref[pl.ds(r0, rb), :] = hn.astype(BF16)
            return carry

        lax.fori_loop(0, tm // rb, blk, 0)
        hn = hn_ref[...]
        aff = jax.nn.sigmoid(_dot_nt(rw_ref[...], hn))
        comb = _route(aff + rb_ref[...], aff)
        comb_t_ref[...] = jnp.zeros_like(comb_t_ref)
        for k in range(N_EXPERTS):
            comb_t_ref[k:k + 1, :] = comb[k]
        comb_ref[...] = comb_t_ref[...].T
        acc_ref[...] = _swiglu(hn, ws1_ref[0], ws3_ref[0], ws2_ref[0], None)

    @pl.when(e > 0)
    def _():
        lane = lax.broadcasted_iota(jnp.int32, (1, 128), 1)
        comb = comb_ref[...]
        hn = hn_ref[...]
        first = 2 * (e - 1)
        y = None
        for k in range(2):
            c_k = jnp.sum(jnp.where(lane == first + k, comb, 0.0), axis=-1, keepdims=True)
            term = _swiglu(hn, w1_ref[0, k], w3_ref[0, k], w2_ref[0, k], c_k)
            y = term if y is None else y + term
        acc_ref[...] += y

    @pl.when(e == pl.num_programs(2) - 1)
    def _():
        y = acc_ref[...]
        r = y * lax.rsqrt(jnp.mean(y * y, axis=-1, keepdims=True) + EPS) * gp_ref[...]
        rows = lax.broadcasted_iota(jnp.int32, (tm, 1), 0) + i * tm
        gate = jnp.where(rows < n_ctx, mc_ref[0, 5:6, :], ml_ref[0, 5:6, :])
        o_ref[0] = x_ref[0] + gate * r


def _moe(xa, mod, g2, gp, rw_t, rbias, ws1, ws3, ws2, w1, w3, w2, layer, n_ctx):
    nb, nt, d = xa.shape
    tm = _pick(nt, (1024, 768, 512, 384, 256, 128))
    n_pairs = w1.shape[1] // 2
    row = lambda b, i, e: (b, i, 0)
    c2 = lambda b, i, e: (0, 0)
    shared_blk = lambda b, i, e: (layer, 0, 0)
    pair_blk = lambda b, i, e: (layer, jnp.maximum(e - 1, 0), 0, 0)
    return pl.pallas_call(
        functools.partial(_moe_kernel, tm=tm, rb=128, n_ctx=n_ctx),
        grid=(nb, nt // tm, n_pairs + 1),
        in_specs=[pl.BlockSpec((1, tm, d), row),
                  pl.BlockSpec((1, 6, d), lambda b, i, e: (b, 0, 0)),
                  pl.BlockSpec((1, 6, d), lambda b, i, e: (nb, 0, 0)),
                  pl.BlockSpec((1, d), c2),
                  pl.BlockSpec((1, d), c2),
                  pl.BlockSpec((N_EXPERTS, d), c2),
                  pl.BlockSpec((N_EXPERTS, 1), c2),
                  pl.BlockSpec((1, d, D_EXPERT), shared_blk),
                  pl.BlockSpec((1, d, D_EXPERT), shared_blk),
                  pl.BlockSpec((1, D_EXPERT, d), shared_blk),
                  pl.BlockSpec((1, 2, d, D_EXPERT), pair_blk),
                  pl.BlockSpec((1, 2, d, D_EXPERT), pair_blk),
                  pl.BlockSpec((1, 2, D_EXPERT, d), pair_blk)],
        out_specs=pl.BlockSpec((1, tm, d), row),
        out_shape=jax.ShapeDtypeStruct((nb, nt, d), F32),
        scratch_shapes=[pltpu.VMEM((tm, d), BF16), pltpu.VMEM((128, tm), F32), pltpu.VMEM((tm, 128), F32),
                        pltpu.VMEM((tm, d), F32)],
        compiler_params=_params("arbitrary", "arbitrary", "arbitrary"),
        name="moe",
    )(xa, mod, mod, g2, gp, rw_t, rbias, ws1, ws3, ws2, w1, w3, w2)


def _swap_perm(width, group):
    j = np.arange(width)
    src = np.where((j % group) < group // 2, j + group // 2, j - group // 2)
    return jnp.asarray(np.arange(width)[:, None] == src[None, :], BF16)


def _rope_tables(n_lat, n_ctx):
    def angles(pos, dim):
        half = dim // 2
        inv = ROPE_BASE ** (-jnp.arange(half, dtype=F32) / half)
        return pos.astype(F32)[:, None] * inv[None, :]

    def tables(cos_parts, sin_parts, reps):
        cos = jnp.tile(jnp.concatenate(cos_parts, axis=-1), (1, reps))
        sin = jnp.tile(jnp.concatenate(sin_parts, axis=-1), (1, reps))
        w = cos.shape[1]
        return (jnp.concatenate([jnp.ones((n_ctx, w), F32), cos], axis=0),
                jnp.concatenate([jnp.zeros((n_ctx, w), F32), sin], axis=0))

    rows = n_lat // GRID_W
    ang_t = angles(jnp.arange(n_lat), HEAD_DIM)
    ang_r = angles(jnp.repeat(jnp.arange(rows), GRID_W), ROPE_DIM // 2)
    ang_c = angles(jnp.tile(jnp.arange(GRID_W), rows), ROPE_DIM // 2)
    ct, st = jnp.cos(ang_t), jnp.sin(ang_t)
    ret = tables([ct, ct], [-st, st], N_HEADS)
    cr, sr, cc, sc = jnp.cos(ang_r), jnp.sin(ang_r), jnp.cos(ang_c), jnp.sin(ang_c)
    mla = tables([cr, cr, cc, cc], [-sr, sr, -sc, sc], N_HEADS)
    return ret, mla


def _ret_tables(logit):
    log_g = jax.nn.log_sigmoid(logit.astype(F32))
    lane_lg = jnp.repeat(log_g, HEAD_DIM, axis=1)
    idx = jnp.arange(CHUNK, dtype=F32)[:, None]
    kd = jnp.stack([jnp.exp(lane_lg[0][None, :] * (CHUNK - 1 - idx)), jnp.exp(lane_lg[1][None, :] * idx)])
    qd = jnp.stack([jnp.exp(lane_lg[0][None, :] * (idx + 1)), jnp.exp(lane_lg[1][None, :] * (CHUNK - idx))])
    cd = jnp.exp(lane_lg * CHUNK)[:, None, :]
    diff = idx - idx.T
    blocks = []
    for h in range(N_HEADS):
        f = jnp.exp(log_g[0, h] * jnp.where(diff >= 0, diff, 0.0))
        b = jnp.exp(log_g[1, h] * jnp.where(diff < 0, -diff, 0.0))
        blocks.append(jnp.where(diff >= 0, f, b))
    dm = jnp.concatenate(blocks, axis=1)
    return kd, cd, qd, dm


def _pack_w_in(w_in):
    d = w_in.shape[0]
    mla = jnp.concatenate([w_in[:, OFF_MLA:OFF_MLA + MLA_COLS], jnp.zeros((d, MLA_PAD - MLA_COLS), w_in.dtype)], 1)
    w = jnp.concatenate([w_in[:, OFF_RET:OFF_RET + RET_COLS], w_in[:, OFF_DN:OFF_DN + 4 * MIX_W],
                         w_in[:, OFF_SG:OFF_SG + SG_COLS], mla, w_in[:, OFF_GATE:OFF_GATE + GATE_COLS]], axis=1)
    wab = w_in[:, OFF_DN + 4 * MIX_W:OFF_DN + DN_COLS]
    wabc = jnp.concatenate([wab, jnp.zeros((d, 128 - 4 * N_HEADS), w_in.dtype)], axis=1)
    return w.astype(BF16), wab.T.astype(BF16), wabc.astype(BF16)


def _mla_weights(w_uq, w_ukv):
    dq = NOPE_DIM + ROPE_DIM
    dkv = NOPE_DIM + V_DIM
    wq = w_uq.reshape(Q_LORA, N_HEADS, dq)
    wqn = wq[:, :, :NOPE_DIM].reshape(Q_LORA, N_HEADS * NOPE_DIM)
    wqr = wq[:, :, NOPE_DIM:].reshape(Q_LORA, N_HEADS * ROPE_DIM)
    wkv = w_ukv.reshape(KV_LORA, N_HEADS, dkv)
    head_eye = jnp.eye(N_HEADS, dtype=F32)
    wa = jnp.einsum("chd,hg->hcgd", wkv[:, :, :NOPE_DIM], head_eye).reshape(N_HEADS, KV_LORA, N_HEADS * NOPE_DIM)
    wa = jnp.pad(wa, ((0, 0), (0, QK_W - KV_LORA), (0, 0)))
    wuv = jnp.einsum("chd,hg->hcgd", wkv[:, :, NOPE_DIM:], head_eye).reshape(N_HEADS, KV_LORA, MIX_W)
    selq = np.zeros((N_HEADS, QK_W, N_HEADS * ROPE_DIM), np.float32)
    for h in range(N_HEADS):
        selq[h, KV_LORA:KV_LORA + ROPE_DIM, h * ROPE_DIM:(h + 1) * ROPE_DIM] = np.eye(ROPE_DIM)
    selc = np.zeros((KV_LORA, QK_W), np.float32)
    selc[:, 0:KV_LORA] = np.eye(KV_LORA)
    selr = np.zeros((128, QK_W), np.float32)
    selr[0:ROPE_DIM, KV_LORA:KV_LORA + ROPE_DIM] = np.eye(ROPE_DIM)
    selv = np.zeros((VT_ROWS, KV_LORA), np.float32)
    selv[0:KV_LORA, :] = np.eye(KV_LORA)
    one_col = np.zeros((VT_ROWS, 1), np.float32)
    one_col[KV_LORA, 0] = 1.0
    return (tuple(jnp.asarray(a, BF16) for a in (wqn, wqr, wa, selq, selc, selr, selv))
            + (jnp.asarray(one_col), wuv.astype(BF16)))


def kernel(x, c, ctx, c_ctx, w_ada, b_ada, g_pre1, g_post1, g_pre2, g_post2, w_in, ret_decay_logit, sg_norm_g, sg_w, sg_b, dn_conv_w, dn_A_log, dn_dt_bias, dn_norm_g, mla_q_norm_g, mla_kv_norm_g, mla_w_uq, mla_w_ukv, w_branch, w_out, router_w, router_bias, moe_w1, moe_w3, moe_w2, shared_w1, shared_w3, shared_w2):
    nb, n_lat, d = x.shape
    n_ctx = ctx.shape[1]
    depth = w_in.shape[0]
    assert d == D_MODEL and n_lat % GRID_W == 0 and n_lat % CHUNK == 0 and n_ctx % 256 == 0
    ncc = n_ctx // CHUNK

    n_cond = -(-(nb + 1) // 8) * 8
    cond = jnp.concatenate([c, c_ctx[None], jnp.zeros((n_cond - nb - 1, d), F32)], axis=0)
    mod_all = _adaln(cond, w_ada, b_ada).reshape(depth, n_cond, 6, d)

    (ret_cos, ret_sin), (mla_cos, mla_sin) = _rope_tables(n_lat, n_ctx)
    perm_ret = _swap_perm(MIX_W, HEAD_DIM)
    perm_mla = _swap_perm(N_HEADS * ROPE_DIM, ROPE_DIM // 2)
    lane_head = jnp.arange(MIX_W) // HEAD_DIM
    bd = (lane_head[:, None] == lane_head[None, :]).astype(F32)
    ones_bd = bd.astype(BF16)
    rw_t = router_w.T.astype(BF16)
    rbias = router_bias.astype(F32)[:, None]

    xa = (ctx, x)
    for l in range(depth):
        last = l == depth - 1
        if last and isinstance(xa, tuple):
            xa = jnp.concatenate(xa, axis=1)
        mod = mod_all[l]
        w_l, wab_l, wabc_l = _pack_w_in(w_in[l])
        p, ab_t, ab_c = _inproj(xa, n_ctx + n_lat, mod, g_pre1[l][None], w_l, wab_l, wabc_l, n_ctx)

        kd, cd, qd, dm = _ret_tables(ret_decay_logit[l])
        wcat = jnp.concatenate([sg_w[l, h] for h in range(N_HEADS)], axis=1).astype(BF16)
        sg_bias = jnp.repeat(sg_b[l].T, HEAD_DIM, axis=1)
        y_ret, y_sg = _retention_and_sgate(p, ret_cos, ret_sin, perm_ret, (kd, cd, qd, dm, bd, ones_bd),
                                           sg_norm_g[l][None], wcat, sg_bias, ncc)

        neg_a = (-jnp.exp(dn_A_log[l].astype(F32))).reshape(2 * N_HEADS, 1)
        dtb = dn_dt_bias[l].astype(F32).reshape(2 * N_HEADS, 1)
        conv_w = jnp.concatenate([dn_conv_w[l], jnp.zeros((8 - CONV_W, 3 * MIX_W), F32)], axis=0)
        o_f, o_b = _deltanet(p, ab_t, ab_c, conv_w, neg_a, dtb, bd, ones_bd, ncc)

        row0 = n_ctx if last else 0
        y_mla, y_mla_ctx = _mla(p, mla_cos, mla_sin, perm_mla, mla_q_norm_g[l][None], mla_kv_norm_g[l][None],
                                *_mla_weights(mla_w_uq[l], mla_w_ukv[l]), n_ctx, not last)
        if not last:
            y_mla = jnp.concatenate([y_mla_ctx, y_mla], axis=1)

        xa = _merge(xa, y_ret, y_sg, o_f, o_b, y_mla, p, w_branch[l].astype(BF16), w_out[l].astype(BF16),
                    jnp.tile(dn_norm_g[l], N_HEADS)[None], g_post1[l][None], mod, ones_bd, n_ctx, row0)

        xa = _moe(xa, mod, g_pre2[l][None], g_post2[l][None], rw_t, rbias, shared_w1, shared_w3, shared_w2,
                  moe_w1, moe_w3, moe_w2, l, n_ctx - row0)
    return xa
```

```python
import functools
import math

import jax
import jax.numpy as jnp
import numpy as np
from jax import lax
from jax.experimental import pallas as pl
from jax.experimental.pallas import tpu as pltpu

F32 = jnp.float32
BF16 = jnp.bfloat16
HIGHEST = lax.Precision.HIGHEST

D_MODEL = 1024
GRID_W = 64
N_HEADS = 4
HEAD_DIM = 64
MIX_W = N_HEADS * HEAD_DIM
CHUNK = 128
ROPE_BASE = 10000.0
EPS = 1e-6
RET_DECAY_EXP0 = 5.0
CONV_W = 5
Q_LORA = 256
KV_LORA = 128
NOPE_DIM = 64
ROPE_DIM = 32
V_DIM = 64
N_EXPERTS = 16
N_GROUPS = 4
EXPERTS_PER_GROUP = N_EXPERTS // N_GROUPS
D_EXPERT = 256
N_BRANCH = 4

RET_COLS = 4 * MIX_W
SG_COLS = 2 * MIX_W
DN_COLS = 4 * MIX_W + 4 * N_HEADS
MLA_COLS = Q_LORA + KV_LORA + ROPE_DIM
GATE_COLS = N_BRANCH * D_MODEL
OFF_RET = 0
OFF_SG = OFF_RET + RET_COLS
OFF_DN = OFF_SG + SG_COLS
OFF_MLA = OFF_DN + DN_COLS
OFF_GATE = OFF_MLA + MLA_COLS

P_RET = 0
P_DN = 1024
P_SG = 2048
P_MLA = 2560
P_GATE = 3072
P_COLS = 7168
MLA_PAD = 512

VMEM_LIMIT = 56 * 1024 * 1024


def _dot(a, b, precision=None):
    return jnp.dot(a, b, preferred_element_type=F32, precision=precision)


def _dot_nt(a, b, precision=None):
    return lax.dot_general(a, b, (((1,), (1,)), ((), ())), preferred_element_type=F32, precision=precision)


def _dot_tn(a, b):
    return lax.dot_general(a, b, (((0,), (0,)), ((), ())), preferred_element_type=F32)


def _mm(a, b):
    return _dot(a.astype(BF16), b.astype(BF16))


def _params(*sem):
    return pltpu.CompilerParams(dimension_semantics=sem, vmem_limit_bytes=VMEM_LIMIT)


def _pick(n, cands):
    for c in cands:
        if n % c == 0:
            return c
    raise ValueError(f"no tile for {n}")


def _head_of_lane(width, group):
    return lax.broadcasted_iota(jnp.int32, (1, width), 1) // group


def _stack_heads(x):
    head = _head_of_lane(MIX_W, HEAD_DIM)
    xf = x.astype(F32)
    return jnp.concatenate([jnp.where(head == h, xf, 0.0).astype(BF16) for h in range(N_HEADS)], axis=0)


def _expand_heads(cols):
    head = _head_of_lane(MIX_W, HEAD_DIM)
    out = cols[:, N_HEADS - 1:N_HEADS]
    for h in range(N_HEADS - 2, -1, -1):
        out = jnp.where(head <= h, cols[:, h:h + 1], out)
    return out


def _head_sum(x, ones_bd):
    hi = x.astype(BF16)
    lo = (x - hi.astype(F32)).astype(BF16)
    return _dot(hi, ones_bd) + _dot(lo, ones_bd)


def _rot(x_bf, cos, sin, perm):
    return x_bf.astype(F32) * cos + _dot(x_bf, perm) * sin


def _norm_modulate(x, g, is_ctx, mc_ref, ml_ref, shift_row, scale_row):
    shift = jnp.where(is_ctx, mc_ref[0, shift_row:shift_row + 1, :], ml_ref[0, shift_row:shift_row + 1, :])
    scale = jnp.where(is_ctx, mc_ref[0, scale_row:scale_row + 1, :], ml_ref[0, scale_row:scale_row + 1, :])
    gain = g * (1.0 + scale)
    return x * lax.rsqrt(jnp.mean(x * x, axis=-1, keepdims=True) + EPS) * gain + shift


def _adaln_kernel(c_ref, w_ref, b_ref, o_ref):
    c = c_ref[...]
    s = c * jax.nn.sigmoid(c)
    o_ref[0] = _dot(s, w_ref[0], precision=HIGHEST) + b_ref[0]


def _adaln(cond, w_ada, b_ada):
    n_l, d, d6 = w_ada.shape
    r = cond.shape[0]
    tn = 1024
    return pl.pallas_call(
        _adaln_kernel,
        grid=(n_l, d6 // tn),
        in_specs=[pl.BlockSpec((r, d), lambda l, j: (0, 0)),
                  pl.BlockSpec((1, d, tn), lambda l, j: (l, 0, j)),
                  pl.BlockSpec((1, 1, tn), lambda l, j: (l, 0, j))],
        out_specs=pl.BlockSpec((1, r, tn), lambda l, j: (l, 0, j)),
        out_shape=jax.ShapeDtypeStruct((n_l, r, d6), F32),
        compiler_params=_params("arbitrary", "arbitrary"),
        name="adaln",
    )(cond, w_ada, b_ada.reshape(n_l, 1, d6))


def _inproj_kernel(x_ref, ctx_ref, ml_ref, mc_ref, g_ref, w_ref, wab_ref, wabc_ref, p_ref, ab_ref, abc_ref, xn_ref,
                   *, tm, rb, n_ctx, split):
    i = pl.program_id(1)
    j = pl.program_id(2)

    @pl.when(j == 0)
    def _():
        def blk(r, carry):
            r0 = pl.multiple_of(r * rb, rb)
            is_ctx = i * tm + r0 < n_ctx
            if split:
                x_off = pl.multiple_of(jnp.maximum(jnp.where(i == 0, r0 - n_ctx, r0), 0), rb)
                c_off = pl.multiple_of(jnp.minimum(r0, n_ctx - rb), rb)
                x = jnp.where(is_ctx, ctx_ref[0, pl.ds(c_off, rb), :], x_ref[0, pl.ds(x_off, rb), :])
            else:
                x = x_ref[0, pl.ds(r0, rb), :]
            hn = _norm_modulate(x, g_ref[...], is_ctx, mc_ref, ml_ref, 0, 1)
            xn_ref[pl.ds(r0, rb), :] = hn.astype(BF16)
            return carry

        lax.fori_loop(0, tm // rb, blk, 0)
        ab_ref[0] = _dot_nt(wab_ref[...], xn_ref[...])
        abc_ref[0] = _dot(xn_ref[...], wabc_ref[...])

    p_ref[0] = _dot(xn_ref[...], w_ref[...]).astype(BF16)


def _token_sources(tokens, n_ctx, tm):
    if isinstance(tokens, tuple):
        ctx, x = tokens
        d = x.shape[-1]
        if x.shape[1] >= tm and tm > n_ctx:
            x_spec = pl.BlockSpec(
                (pl.Element(1), pl.Element(tm), pl.Element(d)),
                lambda b, i, *_: (b, pl.multiple_of(jnp.maximum(i * tm - n_ctx, 0), 128), 0))
            ctx_spec = pl.BlockSpec((1, n_ctx, d), lambda b, i, *_: (b, 0, 0))
            return (x, ctx), (x_spec, ctx_spec), True
        tokens = jnp.concatenate([ctx, x], axis=1)
    nb, _, d = tokens.shape
    dummy = jnp.zeros((nb, 8, d), tokens.dtype)
    return ((tokens, dummy), (pl.BlockSpec((1, tm, d), lambda b, i, *_: (b, i, 0)),
                              pl.BlockSpec((1, 8, d), lambda b, i, *_: (b, 0, 0))), False)


def _inproj(tokens, nt, mod, g, w, wab, wabc, n_ctx):
    tm = _pick(nt, (1408, 768, 384, 256, 128))
    tn = 1792
    (xa, ctx), (x_spec, ctx_spec), split = _token_sources(tokens, n_ctx, tm)
    nb, _, d = xa.shape
    kern = functools.partial(_inproj_kernel, tm=tm, rb=128, n_ctx=n_ctx, split=split)
    return pl.pallas_call(
        kern,
        grid=(nb, nt // tm, P_COLS // tn),
        in_specs=[x_spec, ctx_spec,
                  pl.BlockSpec((1, 6, d), lambda b, i, j: (b, 0, 0)),
                  pl.BlockSpec((1, 6, d), lambda b, i, j: (nb, 0, 0)),
                  pl.BlockSpec((1, d), lambda b, i, j: (0, 0)),
                  pl.BlockSpec((d, tn), lambda b, i, j: (0, j)),
                  pl.BlockSpec((16, d), lambda b, i, j: (0, 0)),
                  pl.BlockSpec((d, 128), lambda b, i, j: (0, 0))],
        out_specs=[pl.BlockSpec((1, tm, tn), lambda b, i, j: (b, i, j)),
                   pl.BlockSpec((1, 16, tm), lambda b, i, j: (b, 0, i)),
                   pl.BlockSpec((1, tm, 128), lambda b, i, j: (b, i, 0))],
        out_shape=[jax.ShapeDtypeStruct((nb, nt, P_COLS), BF16),
                   jax.ShapeDtypeStruct((nb, 16, nt), F32),
                   jax.ShapeDtypeStruct((nb, nt, 128), F32)],
        scratch_shapes=[pltpu.VMEM((tm, d), BF16)],
        compiler_params=_params("arbitrary", "arbitrary", "arbitrary"),
        name="inproj",
    )(xa, ctx, mod, mod, g, w, wab, wabc)


def _bwd_chunk(t, ncc, nc):
    return jnp.where(t < ncc, ncc - 1 - t, nc - 1 - (t - ncc))


def _ret_state_kernel(pf_ref, pb_ref, cf_ref, sf_ref, cb_ref, sb_ref, perm_ref, kd_ref, cd_ref, bd_ref,
                      of_ref, ob_ref, st_f, st_b, *, cb):
    t = pl.program_id(1)

    @pl.when(t == 0)
    def _():
        st_f[...] = jnp.zeros_like(st_f)
        st_b[...] = jnp.zeros_like(st_b)

    def increments(p_ref, c_ref, s_ref, d):
        out = []
        for i in range(cb):
            r = slice(i * CHUNK, (i + 1) * CHUNK)
            kr = _rot(p_ref[0, r, 0:MIX_W], c_ref[r, :], s_ref[r, :], perm_ref[...]) * (HEAD_DIM ** -0.5)
            out.append(bd_ref[...] * _dot_tn((kr * kd_ref[d]).astype(BF16), p_ref[0, r, MIX_W:2 * MIX_W]))
        return out

    inc_f = increments(pf_ref, cf_ref, sf_ref, 0)
    inc_b = increments(pb_ref, cb_ref, sb_ref, 1)
    s = st_f[...]
    for i in range(cb):
        of_ref[0, i] = s.astype(BF16)
        s = cd_ref[0] * s + inc_f[i]
    st_f[...] = s
    s = st_b[...]
    for i in reversed(range(cb)):
        ob_ref[0, i] = s.astype(BF16)
        s = cd_ref[1] * s + inc_b[i]
    st_b[...] = s


def _gelu_tanh(x):
    return 0.5 * x * (1.0 + jnp.tanh(math.sqrt(2.0 / math.pi) * (x + 0.044715 * (x * x * x))))


def _mix_out_kernel(p_ref, pg_ref, c_ref, s_ref, sf_ref, sb_ref, perm_ref, dm_ref, qd_ref, ones_ref,
                    ng_ref, wg_ref, bg_ref, y_ref, ysg_ref, *, cb):
    chunks = range(cb)
    rows = [slice(i * CHUNK, (i + 1) * CHUNK) for i in chunks]
    perm, dm, ones_bd = perm_ref[...], dm_ref[...], ones_ref[...]
    p = [p_ref[0, r, :] for r in rows]
    cos = [c_ref[r, :] for r in rows]
    sin = [s_ref[r, :] for r in rows]
    qr = [_rot(p[i][:, 0:MIX_W], cos[i], sin[i], perm) for i in chunks]
    kr = [_rot(p[i][:, MIX_W:2 * MIX_W], cos[i], sin[i], perm) * (HEAD_DIM ** -0.5) for i in chunks]
    z = [_gelu_tanh(pg_ref[0, r, :].astype(F32)) for r in rows]
    vg = [x[:, MIX_W:] for x in z]
    mu_g = [jnp.mean(x, axis=-1, keepdims=True) for x in vg]
    vgc = [x - m for x, m in zip(vg, mu_g)]
    var_g = [jnp.mean(x * x, axis=-1, keepdims=True) for x in vgc]
    vn = [x * lax.rsqrt(s + EPS) * ng_ref[...] for x, s in zip(vgc, var_g)]
    sc = [_dot_nt(qr[i].astype(BF16), _stack_heads(kr[i])) * dm for i in chunks]
    mixed = [_dot(wg_ref[...], _stack_heads(x)) for x in vn]
    o = [_dot(sc[i].astype(BF16), _stack_heads(p[i][:, 2 * MIX_W:3 * MIX_W])) for i in chunks]
    qs = [jnp.concatenate([(qr[i] * qd_ref[0]).astype(BF16), (qr[i] * qd_ref[1]).astype(BF16)], axis=1)
          for i in chunks]
    ss = [jnp.concatenate([sf_ref[0, i], sb_ref[0, i]], axis=0) for i in chunks]
    o = [o[i] + _dot(qs[i], ss[i]) for i in chunks]
    for i in chunks:
        ysg_ref[0, rows[i], :] = (z[i][:, :MIX_W] * (mixed[i] + bg_ref[...])).astype(BF16)
    mu = [_head_sum(x, ones_bd) * (1.0 / HEAD_DIM) for x in o]
    oc = [x - m for x, m in zip(o, mu)]
    var = [_head_sum(x * x, ones_bd) * (1.0 / HEAD_DIM) for x in oc]
    for i in chunks:
        g = p[i][:, 3 * MIX_W:4 * MIX_W].astype(F32)
        y_ref[0, rows[i], :] = (oc[i] * lax.rsqrt(var[i] + EPS) * (g * jax.nn.sigmoid(g))).astype(BF16)


def _retention_and_sgate(p, cos, sin, perm, tabs, sg_ng, sg_w, sg_bias, ncc):
    nb, nt, _ = p.shape
    nc = nt // CHUNK
    kd, cd, qd, dm, bd, ones_bd = tabs
    cb = 2
    assert nc % cb == 0 and ncc % cb == 0
    nblk, ncb = nc // cb, ncc // cb
    fwd = lambda b, t: (b, t, 0)
    bwd = lambda b, t: (b, _bwd_chunk(t, ncb, nblk), 0)
    tab_f = lambda b, t: (t, 0)
    tab_b = lambda b, t: (_bwd_chunk(t, ncb, nblk), 0)
    c2 = lambda b, t: (0, 0)
    c3 = lambda b, t: (0, 0, 0)
    st_shape = jax.ShapeDtypeStruct((nb, nc, MIX_W, MIX_W), BF16)

    def kv_window(block_of):
        return pl.BlockSpec((pl.Element(1), pl.Element(cb * CHUNK), pl.Element(2 * MIX_W)),
                            lambda b, t: (b, pl.multiple_of(block_of(t) * (cb * CHUNK), 128), P_RET + MIX_W))

    st_f, st_b = pl.pallas_call(
        functools.partial(_ret_state_kernel, cb=cb),
        grid=(nb, nblk),
        in_specs=[kv_window(lambda t: t), kv_window(lambda t: _bwd_chunk(t, ncb, nblk)),
                  pl.BlockSpec((cb * CHUNK, MIX_W), tab_f), pl.BlockSpec((cb * CHUNK, MIX_W), tab_f),
                  pl.BlockSpec((cb * CHUNK, MIX_W), tab_b), pl.BlockSpec((cb * CHUNK, MIX_W), tab_b),
                  pl.BlockSpec((MIX_W, MIX_W), c2),
                  pl.BlockSpec((2, CHUNK, MIX_W), c3),
                  pl.BlockSpec((2, 1, MIX_W), c3),
                  pl.BlockSpec((MIX_W, MIX_W), c2)],
        out_specs=[pl.BlockSpec((1, cb, MIX_W, MIX_W), lambda b, t: (b, t, 0, 0)),
                   pl.BlockSpec((1, cb, MIX_W, MIX_W), lambda b, t: (b, _bwd_chunk(t, ncb, nblk), 0, 0))],
        out_shape=[st_shape, st_shape],
        scratch_shapes=[pltpu.VMEM((MIX_W, MIX_W), F32), pltpu.VMEM((MIX_W, MIX_W), F32)],
        compiler_params=_params("arbitrary", "arbitrary"),
        name="ret_state",
    )(p, p, cos, sin, cos, sin, perm, kd, cd, bd)
    blk = lambda b, t: (b, t, 0)
    y_shape = jax.ShapeDtypeStruct((nb, nt, MIX_W), BF16)
    cb = _pick(nc, (6, 3, 2, 1))
    return pl.pallas_call(
        functools.partial(_mix_out_kernel, cb=cb),
        grid=(nb, nc // cb),
        in_specs=[pl.BlockSpec((1, cb * CHUNK, RET_COLS), blk),
                  pl.BlockSpec((1, cb * CHUNK, SG_COLS), lambda b, t: (b, t, P_SG // SG_COLS)),
                  pl.BlockSpec((cb * CHUNK, MIX_W), tab_f), pl.BlockSpec((cb * CHUNK, MIX_W), tab_f),
                  pl.BlockSpec((1, cb, MIX_W, MIX_W), lambda b, t: (b, t, 0, 0)),
                  pl.BlockSpec((1, cb, MIX_W, MIX_W), lambda b, t: (b, t, 0, 0)),
                  pl.BlockSpec((MIX_W, MIX_W), c2),
                  pl.BlockSpec((CHUNK, N_HEADS * CHUNK), c2),
                  pl.BlockSpec((2, CHUNK, MIX_W), c3),
                  pl.BlockSpec((MIX_W, MIX_W), c2),
                  pl.BlockSpec((1, MIX_W), c2),
                  pl.BlockSpec((CHUNK, N_HEADS * CHUNK), c2),
                  pl.BlockSpec((CHUNK, MIX_W), c2)],
        out_specs=[pl.BlockSpec((1, cb * CHUNK, MIX_W), blk), pl.BlockSpec((1, cb * CHUNK, MIX_W), blk)],
        out_shape=[y_shape, y_shape],
        compiler_params=_params("arbitrary", "arbitrary"),
        name="mix_out",
    )(p, p, cos, sin, st_f, st_b, perm, dm, qd, ones_bd, sg_ng, sg_w, sg_bias)


def _softplus(a):
    return jnp.maximum(a, 0.0) + jnp.log1p(jnp.exp(-jnp.abs(a)))


def _dn_prep_kernel(pc_ref, pp_ref, pn_ref, ab_ref, abc_ref, cw_ref, na_ref, dtb_ref, nar_ref, dtbr_ref, ones_ref,
                    qkv_ref, gb_ref, gbc_ref, xe_ref, *, rows, ctx_blocks, n_blocks):
    t = pl.program_id(1)
    w3 = 3 * MIX_W
    prev_ok = jnp.where((t != 0) & (t != ctx_blocks), 1.0, 0.0)
    next_ok = jnp.where((t != ctx_blocks - 1) & (t != n_blocks - 1), 1.0, 0.0)
    tail = pp_ref[0, rows - 16:rows, 0:w3].astype(F32)
    head = pn_ref[0, 0:16, 0:w3].astype(F32)
    xe_ref[0:8, :] = tail[8:16, :] * prev_ok
    xe_ref[8:8 + rows, :] = pc_ref[0, :, 0:w3].astype(F32)
    xe_ref[8 + rows:16 + rows, :] = head[0:8, :] * next_ok
    pad = CONV_W // 2
    y = xe_ref[8 - pad:8 - pad + rows, :] * cw_ref[0:1, :]
    for i in range(1, CONV_W):
        y = y + xe_ref[8 - pad + i:8 - pad + i + rows, :] * cw_ref[i:i + 1, :]
    y = y * jax.nn.sigmoid(y)
    q = y[:, 0:MIX_W]
    k = y[:, MIX_W:2 * MIX_W]
    v = y[:, 2 * MIX_W:w3]
    ones_bd = ones_ref[...]
    qn = q * lax.rsqrt(_head_sum(q * q, ones_bd) + EPS) * (HEAD_DIM ** -0.5)
    kn = k * lax.rsqrt(_head_sum(k * k, ones_bd) + EPS)
    qkv_ref[0, :, 0:MIX_W] = qn.astype(BF16)
    qkv_ref[0, :, MIX_W:2 * MIX_W] = kn.astype(BF16)
    qkv_ref[0, :, 2 * MIX_W:w3] = v.astype(BF16)
    ab = ab_ref[0]
    gb_ref[0, 0:8, :] = na_ref[...] * _softplus(ab[0:8, :] + dtb_ref[...])
    gb_ref[0, 8:16, :] = jax.nn.sigmoid(ab[8:16, :])
    abc = abc_ref[0]
    lane = lax.broadcasted_iota(jnp.int32, (1, 128), 1)
    g_c = nar_ref[...] * _softplus(abc + dtbr_ref[...])
    gbc_ref[0] = jnp.where(lane < 8, g_c, jnp.where(lane < 16, jax.nn.sigmoid(abc), 0.0))


def _split3(x):
    hi = x.astype(BF16)
    r = x - hi.astype(F32)
    mid = r.astype(BF16)
    lo = (r - mid.astype(F32)).astype(BF16)
    return hi, mid, lo


def _tri_inverse(mats, ii, jj):
    eye = jnp.where(ii == jj, 1.0, 0.0)
    nd = [jnp.where((ii // 16) == (jj // 16), n, 0.0) for n in mats]
    p1 = [_mm(x, x) for x in nd]
    m = [eye - x for x in nd]
    p2 = [_mm(x, x) for x in p1]
    m = [x + _mm(x, y) for x, y in zip(m, p1)]
    p3 = [_mm(x, x) for x in p2]
    m = [x + _mm(x, y) for x, y in zip(m, p2)]
    m = [x + _mm(x, y) for x, y in zip(m, p3)]
    for lvl in (16, 32, 64):
        off_mask = ((ii // (2 * lvl)) == (jj // (2 * lvl))) & ((ii // lvl) != (jj // lvl))
        t = [_mm(jnp.where(off_mask, n, 0.0), x) for n, x in zip(mats, m)]
        m = [x - _mm(x, y) for x, y in zip(m, t)]
    return m


def _dn_pre(qkv, g, gbc, d, lower):
    c = CHUNK
    qn = qkv[:, 0:MIX_W]
    kn = qkv[:, MIX_W:2 * MIX_W]
    v = qkv[:, 2 * MIX_W:3 * MIX_W]
    ii = lax.broadcasted_iota(jnp.int32, (c, c), 0)
    jj = lax.broadcasted_iota(jnp.int32, (c, c), 1)
    incl = (ii >= jj) if lower else (ii <= jj)
    tri = jnp.where(incl, 1.0, 0.0).astype(BF16)
    g_row = sum(_dot_nt(part, tri) for part in _split3(g))[N_HEADS * d:N_HEADS * (d + 1), :]
    cum = sum(_dot(tri, part) for part in _split3(gbc))
    g_col = cum[:, N_HEADS * d:N_HEADS * (d + 1)]
    b_col = gbc[:, 2 * N_HEADS + N_HEADS * d:2 * N_HEADS + N_HEADS * (d + 1)]
    g_cols4 = jnp.concatenate([jnp.broadcast_to(g_col[:, h:h + 1], (c, c)) for h in range(N_HEADS)], axis=1)
    b_cols4 = jnp.concatenate([jnp.broadcast_to(b_col[:, h:h + 1], (c, c)) for h in range(N_HEADS)], axis=1)
    g_rows4 = jnp.concatenate([g_row[h:h + 1, :] for h in range(N_HEADS)], axis=1)
    incl4 = jnp.concatenate([incl] * N_HEADS, axis=1)
    diag4 = jnp.concatenate([ii == jj] * N_HEADS, axis=1)
    decay = jnp.where(incl4, jnp.exp(jnp.where(incl4, g_cols4 - g_rows4, 0.0)), 0.0)
    kstack = _stack_heads(kn)
    kk = _dot_nt(kn, kstack)
    qk = _dot_nt(qn, kstack)
    n_mat = jnp.where(diag4, 0.0, decay * kk * b_cols4)
    attn = (decay * qk).astype(BF16)
    g256 = _expand_heads(g_col)
    eg256 = jnp.exp(g256)
    b256 = _expand_heads(b_col)
    vb = v.astype(F32) * b256
    kbg = kn.astype(F32) * b256 * eg256
    rhs = jnp.concatenate([_stack_heads(vb), _stack_heads(kbg)], axis=1)
    g_last = g256[c - 1:c, :] if lower else g256[0:1, :]
    kdec = (kn.astype(F32) * jnp.exp(g_last - g256)).astype(BF16)
    n_heads = [n_mat[:, h * c:(h + 1) * c] for h in range(N_HEADS)]
    return n_heads, dict(qn=qn, attn=attn, rhs=rhs, eg=eg256, kdec=kdec, sdec=jnp.exp(g_last))


def _dn_post(z, s_prev, bd):
    s_bf = s_prev.astype(BF16)
    w = z["u"] - _dot(z["wk"], s_bf)
    o = z["eg"] * _dot(z["qn"], s_bf) + _dot(z["attn"], _stack_heads(w))
    s_next = z["sdec"] * s_prev + bd * _dot_tn(z["kdec"], w.astype(BF16))
    return o, s_next


def _dn_scan_kernel(qf_ref, qb_ref, gf_ref, gb_ref, gcf_ref, gcb_ref, bd_ref, of_ref, ob_ref, st_f, st_b, *, cb):
    t = pl.program_id(1)

    @pl.when(t == 0)
    def _():
        st_f[...] = jnp.zeros_like(st_f)
        st_b[...] = jnp.zeros_like(st_b)

    bd = bd_ref[...]
    rows = [slice(i * CHUNK, (i + 1) * CHUNK) for i in range(cb)]
    mats, pres = [], []
    for d, (q_ref, g_ref, gc_ref) in enumerate(((qf_ref, gf_ref, gcf_ref), (qb_ref, gb_ref, gcb_ref))):
        for r in rows:
            n_heads, pre = _dn_pre(q_ref[0, r, :], g_ref[0, :, r], gc_ref[0, r, :], d, d == 0)
            mats += n_heads
            pres.append(pre)
    ii = lax.broadcasted_iota(jnp.int32, (CHUNK, CHUNK), 0)
    jj = lax.broadcasted_iota(jnp.int32, (CHUNK, CHUNK), 1)
    inv = _tri_inverse(mats, ii, jj)
    for n, pre in enumerate(pres):
        a_inv = jnp.concatenate(inv[N_HEADS * n:N_HEADS * (n + 1)], axis=1).astype(BF16)
        uw = _dot(a_inv, pre["rhs"])
        pre["u"] = uw[:, 0:MIX_W]
        pre["wk"] = uw[:, MIX_W:2 * MIX_W].astype(BF16)
    s_f, s_b = st_f[...], st_b[...]
    for k in range(cb):
        o, s_f = _dn_post(pres[k], s_f, bd)
        of_ref[0, rows[k], :] = o
        o, s_b = _dn_post(pres[cb + cb - 1 - k], s_b, bd)
        ob_ref[0, rows[cb - 1 - k], :] = o
    st_f[...] = s_f
    st_b[...] = s_b


def _deltanet(p, ab_t, ab_c, conv_w, neg_a, dtb, bd, ones_bd, ncc):
    nb, nt, _ = p.shape
    nc = nt // CHUNK
    w3 = 3 * MIX_W
    c2 = lambda b, t: (0, 0)
    dn_blk = P_DN // RET_COLS
    pad_lanes = lambda col: jnp.concatenate([col.reshape(1, -1), jnp.zeros((1, 128 - col.size), F32)], axis=1)
    pr = math.gcd(math.gcd(ncc * CHUNK, nt), 256)
    n_pb, ctx_pb = nt // pr, ncc * CHUNK // pr
    qkv, gbeta, gbeta_c = pl.pallas_call(
        functools.partial(_dn_prep_kernel, rows=pr, ctx_blocks=ctx_pb, n_blocks=n_pb),
        grid=(nb, n_pb),
        in_specs=[pl.BlockSpec((1, pr, 4 * MIX_W), lambda b, t: (b, t, dn_blk)),
                  pl.BlockSpec((1, pr, 4 * MIX_W), lambda b, t: (b, jnp.maximum(t - 1, 0), dn_blk)),
                  pl.BlockSpec((1, pr, 4 * MIX_W), lambda b, t: (b, jnp.minimum(t + 1, n_pb - 1), dn_blk)),
                  pl.BlockSpec((1, 16, pr), lambda b, t: (b, 0, t)),
                  pl.BlockSpec((1, pr, 128), lambda b, t: (b, t, 0)),
                  pl.BlockSpec((8, w3), c2),
                  pl.BlockSpec((8, 1), c2),
                  pl.BlockSpec((8, 1), c2),
                  pl.BlockSpec((1, 128), c2),
                  pl.BlockSpec((1, 128), c2),
                  pl.BlockSpec((MIX_W, MIX_W), c2)],
        out_specs=[pl.BlockSpec((1, pr, w3), lambda b, t: (b, t, 0)),
                   pl.BlockSpec((1, 16, pr), lambda b, t: (b, 0, t)),
                   pl.BlockSpec((1, pr, 128), lambda b, t: (b, t, 0))],
        out_shape=[jax.ShapeDtypeStruct((nb, nt, w3), BF16),
                   jax.ShapeDtypeStruct((nb, 16, nt), F32),
                   jax.ShapeDtypeStruct((nb, nt, 128), F32)],
        scratch_shapes=[pltpu.VMEM((pr + 16, w3), F32)],
        compiler_params=_params("arbitrary", "arbitrary"),
        name="dn_prep",
    )(p, p, p, ab_t, ab_c, conv_w, neg_a, dtb, pad_lanes(neg_a), pad_lanes(dtb), ones_bd)
    cb = 2
    assert nc % cb == 0 and ncc % cb == 0
    rows = cb * CHUNK
    cur_b = lambda t: _bwd_chunk(t, ncc // cb, nc // cb)
    o_shape = jax.ShapeDtypeStruct((nb, nt, MIX_W), F32)
    return pl.pallas_call(
        functools.partial(_dn_scan_kernel, cb=cb),
        grid=(nb, nc // cb),
        in_specs=[pl.BlockSpec((1, rows, w3), lambda b, t: (b, t, 0)),
                  pl.BlockSpec((1, rows, w3), lambda b, t: (b, cur_b(t), 0)),
                  pl.BlockSpec((1, 16, rows), lambda b, t: (b, 0, t)),
                  pl.BlockSpec((1, 16, rows), lambda b, t: (b, 0, cur_b(t))),
                  pl.BlockSpec((1, rows, 128), lambda b, t: (b, t, 0)),
                  pl.BlockSpec((1, rows, 128), lambda b, t: (b, cur_b(t), 0)),
                  pl.BlockSpec((MIX_W, MIX_W), c2)],
        out_specs=[pl.BlockSpec((1, rows, MIX_W), lambda b, t: (b, t, 0)),
                   pl.BlockSpec((1, rows, MIX_W), lambda b, t: (b, cur_b(t), 0))],
        out_shape=[o_shape, o_shape],
        scratch_shapes=[pltpu.VMEM((MIX_W, MIX_W), F32), pltpu.VMEM((MIX_W, MIX_W), F32)],
        compiler_params=_params("arbitrary", "arbitrary"),
        name="dn_scan",
    )(qkv, qkv, gbeta, gbeta, gbeta_c, gbeta_c, bd)


QK_W = 256


VT_ROWS = 144


def _mla_prep_kernel(p_ref, c_ref, s_ref, perm_ref, qg_ref, kg_ref, wqn_ref, wqr_ref, wa_ref, selq_ref, selc_ref,
                     selr_ref, selv_ref, one_ref, qt_ref, kv_ref, vt_ref, *, scale):
    p = p_ref[0]
    cos, sin, perm = c_ref[...], s_ref[...], perm_ref[...]
    cq = p[:, 0:Q_LORA].astype(F32)
    cqn = (cq * lax.rsqrt(jnp.mean(cq * cq, axis=-1, keepdims=True) + EPS) * qg_ref[...]).astype(BF16)
    q_nope = _dot(cqn, wqn_ref[...]).astype(BF16)
    q_rope = _dot(cqn, wqr_ref[...]).astype(BF16)
    q_rot = (_rot(q_rope, cos, sin, perm) * scale).astype(BF16)
    q_nope_s = (q_nope.astype(F32) * scale).astype(BF16)
    for h in range(N_HEADS):
        qt_ref[0, h] = (_dot_nt(wa_ref[h], q_nope_s) + _dot_nt(selq_ref[h], q_rot)).astype(BF16)
    ckv = p[:, Q_LORA:Q_LORA + KV_LORA].astype(F32)
    ckvn = (ckv * lax.rsqrt(jnp.mean(ckv * ckv, axis=-1, keepdims=True) + EPS) * kg_ref[...]).astype(BF16)
    kr = p[:, Q_LORA + KV_LORA:MLA_PAD]
    kr_rot = _rot(kr, cos, sin, perm).astype(BF16)
    kv_ref[0] = (_dot(ckvn, selc_ref[...]) + _dot(kr_rot, selr_ref[...])).astype(BF16)
    vt_ref[0] = (_dot_nt(selv_ref[...], ckvn) + one_ref[...]).astype(BF16)


def _mla_attn_kernel(qt_ref, kv_ref, vt_ref, wuv_ref, y_ref, m_ref, acc_ref, s_ref, *, tk, n_ctx, nt, latent):
    heads = range(N_HEADS)
    m_ref[...] = jnp.full_like(m_ref, -jnp.inf)
    acc_ref[...] = jnp.zeros_like(acc_ref)

    def scores(j0, size, slot):
        k = kv_ref[0, pl.ds(j0, size), :]
        for h in heads:
            s_ref[slot, h, 0:size, :] = _dot(k, qt_ref[0, h])

    def softmax_pv(j0, size, slot):
        vt = vt_ref[0, :, pl.ds(j0, size)]
        s = [s_ref[slot, h, 0:size, :] for h in heads]
        m_old = [m_ref[h] for h in heads]
        m_new = [jnp.maximum(m_old[h], jnp.max(s[h], axis=0, keepdims=True)) for h in heads]
        pr = [jnp.exp2(s[h] - m_new[h]).astype(BF16) for h in heads]
        pv = [_dot(vt, pr[h]) for h in heads]
        for h in heads:
            acc_ref[h] = jnp.exp2(m_old[h] - m_new[h]) * acc_ref[h] + pv[h]
            m_ref[h] = m_new[h]

    scores(0, n_ctx, 0)
    if not latent:
        softmax_pv(0, n_ctx, 0)
    else:
        n_tiles = (nt - n_ctx) // tk
        last = n_ctx + (n_tiles - 1) * tk
        scores(n_ctx, tk, 1)
        softmax_pv(0, n_ctx, 0)

        def body(jj, carry):
            t0 = pl.multiple_of(n_ctx + 2 * jj * tk, 256)
            t1 = pl.multiple_of(jnp.minimum(t0 + tk, last), 256)
            t2 = pl.multiple_of(jnp.minimum(t0 + 2 * tk, last), 256)
            scores(t1, tk, 0)
            softmax_pv(t0, tk, 1)
            scores(t2, tk, 1)
            softmax_pv(t1, tk, 0)
            return carry

        lax.fori_loop(0, n_tiles // 2, body, 0)
        if n_tiles % 2:
            softmax_pv(last, tk, 1)

    y = None
    for h in range(N_HEADS):
        acc = acc_ref[h]
        o = (acc[0:KV_LORA, :] / acc[KV_LORA:KV_LORA + 1, :]).astype(BF16)
        term = _dot_tn(o, wuv_ref[h])
        y = term if y is None else y + term
    y_ref[0] = y.astype(BF16)


def _mla(p, cos, sin, perm, qg, kg, wqn, wqr, wa, selq, selc, selr, selv, one_col, wuv, n_ctx, ctx_out):
    nb, nt, _ = p.shape
    n_lat = nt - n_ctx
    tm = _pick(nt, (768, 384, 256, 128))
    scale = (NOPE_DIM + ROPE_DIM) ** -0.5 * math.log2(math.e)
    c2 = lambda b, i: (0, 0)
    c3 = lambda b, i: (0, 0, 0)
    qt, kv, vt = pl.pallas_call(
        functools.partial(_mla_prep_kernel, scale=scale),
        grid=(nb, nt // tm),
        in_specs=[pl.BlockSpec((1, tm, MLA_PAD), lambda b, i: (b, i, P_MLA // MLA_PAD)),
                  pl.BlockSpec((tm, 128), lambda b, i: (i, 0)),
                  pl.BlockSpec((tm, 128), lambda b, i: (i, 0)),
                  pl.BlockSpec((128, 128), c2),
                  pl.BlockSpec((1, Q_LORA), c2),
                  pl.BlockSpec((1, KV_LORA), c2),
                  pl.BlockSpec((Q_LORA, N_HEADS * NOPE_DIM), c2),
                  pl.BlockSpec((Q_LORA, N_HEADS * ROPE_DIM), c2),
                  pl.BlockSpec((N_HEADS, QK_W, N_HEADS * NOPE_DIM), c3),
                  pl.BlockSpec((N_HEADS, QK_W, N_HEADS * ROPE_DIM), c3),
                  pl.BlockSpec((KV_LORA, QK_W), c2),
                  pl.BlockSpec((128, QK_W), c2),
                  pl.BlockSpec((VT_ROWS, KV_LORA), c2),
                  pl.BlockSpec((VT_ROWS, 1), c2)],
        out_specs=[pl.BlockSpec((1, N_HEADS, QK_W, tm), lambda b, i: (b, 0, 0, i)),
                   pl.BlockSpec((1, tm, QK_W), lambda b, i: (b, i, 0)),
                   pl.BlockSpec((1, VT_ROWS, tm), lambda b, i: (b, 0, i))],
        out_shape=[jax.ShapeDtypeStruct((nb, N_HEADS, QK_W, nt), BF16),
                   jax.ShapeDtypeStruct((nb, nt, QK_W), BF16),
                   jax.ShapeDtypeStruct((nb, VT_ROWS, nt), BF16)],
        compiler_params=_params("arbitrary", "arbitrary"),
        name="mla_prep",
    )(p, cos, sin, perm, qg, kg, wqn, wqr, wa, selq, selc, selr, selv, one_col)
    tk = _pick(n_lat, (512, 256))

    def attend(tq, first_col, n_q, latent):
        if first_col % tq == 0:
            q_spec = pl.BlockSpec((1, N_HEADS, QK_W, tq), lambda b, i: (b, 0, 0, i + first_col // tq))
        else:
            q_spec = pl.BlockSpec((pl.Element(1), pl.Element(N_HEADS), pl.Element(QK_W), pl.Element(tq)),
                                  lambda b, i: (b, 0, 0, pl.multiple_of(first_col + i * tq, 128)))
        return pl.pallas_call(
            functools.partial(_mla_attn_kernel, tk=tk, n_ctx=n_ctx, nt=nt, latent=latent),
            grid=(nb, n_q // tq),
            in_specs=[q_spec,
                      pl.BlockSpec((1, nt, QK_W), lambda b, i: (b, 0, 0)),
                      pl.BlockSpec((1, VT_ROWS, nt), lambda b, i: (b, 0, 0)),
                      pl.BlockSpec((N_HEADS, KV_LORA, MIX_W), c3)],
            out_specs=pl.BlockSpec((1, tq, MIX_W), lambda b, i: (b, i, 0)),
            out_shape=jax.ShapeDtypeStruct((nb, n_q, MIX_W), BF16),
            scratch_shapes=[pltpu.VMEM((N_HEADS, 1, tq), F32), pltpu.VMEM((N_HEADS, VT_ROWS, tq), F32),
                            pltpu.VMEM((2, N_HEADS, max(tk, n_ctx) if latent else n_ctx, tq), F32)],
            compiler_params=_params("arbitrary", "arbitrary"),
            name="mla_attn" if latent else "mla_attn_ctx",
        )(qt, kv, vt, wuv)

    y_lat = attend(_pick(n_lat, (512, 256)), n_ctx, n_lat, True)
    y_ctx = attend(_pick(n_ctx, (256, 128)), 0, n_ctx, False) if ctx_out else None
    return y_lat, y_ctx


def _merge_kernel(x_ref, ctx_ref, yr_ref, ys_ref, of_ref, ob_ref, ym_ref, z_ref, g0_ref, g1_ref, g2_ref, g3_ref,
                  wb_ref, wo_ref, ng_ref, gp_ref, ml_ref, mc_ref, ones_ref, o_ref, *, tm, n_ctx, row0, split):
    i = pl.program_id(1)
    if split:
        first = jnp.concatenate([ctx_ref[0], x_ref[0, 0:tm - n_ctx, :]], axis=0)
        x_res = jnp.where(i == 0, first, x_ref[0])
    else:
        x_res = x_ref[0]
    od = of_ref[0] + ob_ref[0]
    ms = _head_sum(od * od, ones_ref[...]) * (1.0 / HEAD_DIM)
    z = z_ref[0].astype(F32)
    ydn = (od * lax.rsqrt(ms + EPS) * ng_ref[...]) * (z * jax.nn.sigmoid(z))
    ys = (yr_ref[0], ys_ref[0], ydn.astype(BF16), ym_ref[0])
    gates = (g0_ref, g1_ref, g2_ref, g3_ref)
    acc = None
    for b in range(N_BRANCH):
        term = jax.nn.sigmoid(gates[b][0].astype(F32)) * _dot(ys[b], wb_ref[b])
        acc = term if acc is None else acc + term
    y = _dot(acc.astype(BF16), wo_ref[...])
    r = y * lax.rsqrt(jnp.mean(y * y, axis=-1, keepdims=True) + EPS) * gp_ref[...]
    rows = lax.broadcasted_iota(jnp.int32, (tm, 1), 0) + (row0 + i * tm)
    gate = jnp.where(rows < n_ctx, mc_ref[0, 2:3, :], ml_ref[0, 2:3, :])
    o_ref[0] = x_res + gate * r


def _merge(tokens, y_ret, y_sg, o_f, o_b, y_mla, p, wb, wo, ng, gp, mod, ones_bd, n_ctx, row0):
    nb, nt, _ = p.shape
    d = D_MODEL
    n_rows = nt - row0
    tm = _pick(n_rows, (768, 512, 384, 256, 128))
    c2 = lambda b, i: (0, 0)
    if row0 == 0:
        def window(width, col):
            return pl.BlockSpec((1, tm, width), lambda b, i: (b, i, col // width))

        (xa, ctx), (x_spec, ctx_spec), split = _token_sources(tokens, n_ctx, tm)
    else:
        def window(width, col):
            return pl.BlockSpec((pl.Element(1), pl.Element(tm), pl.Element(width)),
                                lambda b, i: (b, pl.multiple_of(row0 + i * tm, 128), col))

        xa, x_spec, split = tokens, window(d, 0), False
        ctx, ctx_spec = jnp.zeros((nb, 8, d), F32), pl.BlockSpec((1, 8, d), lambda b, i: (b, 0, 0))
    y_spec = window(MIX_W, 0)
    assert y_mla.shape[1] == n_rows
    mla_spec = pl.BlockSpec((1, tm, MIX_W), lambda b, i: (b, i, 0))
    gate_specs = [window(d, P_GATE + k * d) for k in range(N_BRANCH)]
    return pl.pallas_call(
        functools.partial(_merge_kernel, tm=tm, n_ctx=n_ctx, row0=row0, split=split),
        grid=(nb, n_rows // tm),
        in_specs=[x_spec, ctx_spec, y_spec, y_spec, y_spec, y_spec, mla_spec,
                  window(MIX_W, P_DN + 3 * MIX_W),
                  *gate_specs,
                  pl.BlockSpec((N_BRANCH, MIX_W, d), lambda b, i: (0, 0, 0)),
                  pl.BlockSpec((d, d), c2),
                  pl.BlockSpec((1, MIX_W), c2),
                  pl.BlockSpec((1, d), c2),
                  pl.BlockSpec((1, 6, d), lambda b, i: (b, 0, 0)),
                  pl.BlockSpec((1, 6, d), lambda b, i: (nb, 0, 0)),
                  pl.BlockSpec((MIX_W, MIX_W), c2)],
        out_specs=pl.BlockSpec((1, tm, d), lambda b, i: (b, i, 0)),
        out_shape=jax.ShapeDtypeStruct((nb, n_rows, d), F32),
        compiler_params=_params("arbitrary", "arbitrary"),
        name="merge",
    )(xa, ctx, y_ret, y_sg, o_f, o_b, y_mla, p, p, p, p, p, wb, wo, ng, gp, mod, mod, ones_bd)


def _route(sel, aff):
    rows = [sel[e:e + 1, :] for e in range(N_EXPERTS)]
    pairs = [(a, b) for a in range(EXPERTS_PER_GROUP) for b in range(a + 1, EXPERTS_PER_GROUP)]
    grp_score, grp_pair = [], []
    for g in range(N_GROUPS):
        base = g * EXPERTS_PER_GROUP
        best = rows[base + pairs[0][0]] + rows[base + pairs[0][1]]
        best_p = jnp.zeros_like(best, dtype=jnp.int32)
        for pi in range(1, len(pairs)):
            s = rows[base + pairs[pi][0]] + rows[base + pairs[pi][1]]
            take = s > best
            best = jnp.where(take, s, best)
            best_p = jnp.where(take, pi, best_p)
        grp_score.append(best)
        grp_pair.append(best_p)
    top = grp_score[0]
    top_g = jnp.zeros_like(grp_pair[0])
    top_p = grp_pair[0]
    for g in range(1, N_GROUPS):
        take = grp_score[g] > top
        top = jnp.where(take, grp_score[g], top)
        top_g = jnp.where(take, g, top_g)
        top_p = jnp.where(take, grp_pair[g], top_p)
    picked = []
    for e in range(N_EXPERTS):
        g, k = divmod(e, EXPERTS_PER_GROUP)
        in_pair = None
        for pi, (a, b) in enumerate(pairs):
            if k in (a, b):
                hit = top_p == pi
                in_pair = hit if in_pair is None else (in_pair | hit)
        picked.append(jnp.where((top_g == g) & in_pair, aff[e:e + 1, :], 0.0))
    denom = picked[0]
    for e in range(1, N_EXPERTS):
        denom = denom + picked[e]
    return [pk / denom for pk in picked]


def _swiglu(hn, w1, w3, w2, scale):
    a = _dot(hn, w1.astype(BF16))
    h = (a * jax.nn.sigmoid(a)) * _dot(hn, w3.astype(BF16))
    if scale is not None:
        h = h * scale
    return _dot(h.astype(BF16), w2.astype(BF16))


def _moe_kernel(x_ref, ml_ref, mc_ref, g2_ref, gp_ref, rw_ref, rb_ref, ws1_ref, ws3_ref, ws2_ref,
                w1_ref, w3_ref, w2_ref, o_ref, hn_ref, comb_t_ref, comb_ref, acc_ref, *, tm, rb, n_ctx):
    i = pl.program_id(1)
    e = pl.program_id(2)

    @pl.when(e == 0)
    def _():
        def blk(r, carry):
            r0 = pl.multiple_of(r * rb, rb)
            x = x_ref[0, pl.ds(r0, rb), :]
            hn = _norm_modulate(x, g2_ref[...], i * tm + r0 < n_ctx, mc_ref, ml_ref, 3, 4)
            hn_ref[pl.ds(r0, rb), :] = hn.astype(BF16)
            return carry

        lax.fori_loop(0, tm // rb, blk, 0)
        hn = hn_ref[...]
        aff = jax.nn.sigmoid(_dot_nt(rw_ref[...], hn))
        comb = _route(aff + rb_ref[...], aff)
        comb_t_ref[...] = jnp.zeros_like(comb_t_ref)
        for k in range(N_EXPERTS):
            comb_t_ref[k:k + 1, :] = comb[k]
        comb_ref[...] = comb_t_ref[...].T
        acc_ref[...] = _swiglu(hn, ws1_ref[0], ws3_ref[0], ws2_ref[0], None)

    @pl.when(e > 0)
    def _():
        lane = lax.broadcasted_iota(jnp.int32, (1, 128), 1)
        comb = comb_ref[...]
        hn = hn_ref[...]
        first = 2 * (e - 1)
        y = None
        for k in range(2):
            c_k = jnp.sum(jnp.where(lane == first + k, comb, 0.0), axis=-1, keepdims=True)
            term = _swiglu(hn, w1_ref[0, k], w3_ref[0, k], w2_ref[0, k], c_k)
            y = term if y is None else y + term
        acc_ref[...] += y

    @pl.when(e == pl.num_programs(2) - 1)
    def _():
        y = acc_ref[...]
        r = y * lax.rsqrt(jnp.mean(y * y, axis=-1, keepdims=True) + EPS) * gp_ref[...]
        rows = lax.broadcasted_iota(jnp.int32, (tm, 1), 0) + i * tm
        gate = jnp.where(rows < n_ctx, mc_ref[0, 5:6, :], ml_ref[0, 5:6, :])
        o_ref[0] = x_ref[0] + gate * r


def _moe(xa, mod, g2, gp, rw_t, rbias, ws1, ws3, ws2, w1, w3, w2, layer, n_ctx):
    nb, nt, d = xa.shape
    tm = _pick(nt, (1024, 768, 512, 384, 256, 128))
    n_pairs = w1.shape[1] // 2
    row = lambda b, i, e: (b, i, 0)
    c2 = lambda b, i, e: (0, 0)
    shared_blk = lambda b, i, e: (layer, 0, 0)
    pair_blk = lambda b, i, e: (layer, jnp.maximum(e - 1, 0), 0, 0)
    return pl.pallas_call(
        functools.partial(_moe_kernel, tm=tm, rb=128, n_ctx=n_ctx),
        grid=(nb, nt // tm, n_pairs + 1),
        in_specs=[pl.BlockSpec((1, tm, d), row),
                  pl.BlockSpec((1, 6, d), lambda b, i, e: (b, 0, 0)),
                  pl.BlockSpec((1, 6, d), lambda b, i, e: (nb, 0, 0)),
                  pl.BlockSpec((1, d), c2),
                  pl.BlockSpec((1, d), c2),
                  pl.BlockSpec((N_EXPERTS, d), c2),
                  pl.BlockSpec((N_EXPERTS, 1), c2),
                  pl.BlockSpec((1, d, D_EXPERT), shared_blk),
                  pl.BlockSpec((1, d, D_EXPERT), shared_blk),
                  pl.BlockSpec((1, D_EXPERT, d), shared_blk),
                  pl.BlockSpec((1, 2, d, D_EXPERT), pair_blk),
                  pl.BlockSpec((1, 2, d, D_EXPERT), pair_blk),
                  pl.BlockSpec((1, 2, D_EXPERT, d), pair_blk)],
        out_specs=pl.BlockSpec((1, tm, d), row),
        out_shape=jax.ShapeDtypeStruct((nb, nt, d), F32),
        scratch_shapes=[pltpu.VMEM((tm, d), BF16), pltpu.VMEM((128, tm), F32), pltpu.VMEM((tm, 128), F32),
                        pltpu.VMEM((tm, d), F32)],
        compiler_params=_params("arbitrary", "arbitrary", "arbitrary"),
        name="moe",
    )(xa, mod, mod, g2, gp, rw_t, rbias, ws1, ws3, ws2, w1, w3, w2)


def _swap_perm(width, group):
    j = np.arange(width)
    src = np.where((j % group) < group // 2, j + group // 2, j - group // 2)
    return jnp.asarray(np.arange(width)[:, None] == src[None, :], BF16)


def _rope_tables(n_lat, n_ctx):
    def angles(pos, dim):
        half = dim // 2
        inv = ROPE_BASE ** (-jnp.arange(half, dtype=F32) / half)
        return pos.astype(F32)[:, None] * inv[None, :]

    def tables(cos_parts, sin_parts, reps):
        cos = jnp.tile(jnp.concatenate(cos_parts, axis=-1), (1, reps))
        sin = jnp.tile(jnp.concatenate(sin_parts, axis=-1), (1, reps))
        w = cos.shape[1]
        return (jnp.concatenate([jnp.ones((n_ctx, w), F32), cos], axis=0),
                jnp.concatenate([jnp.zeros((n_ctx, w), F32), sin], axis=0))

    rows = n_lat // GRID_W
    ang_t = angles(jnp.arange(n_lat), HEAD_DIM)
    ang_r = angles(jnp.repeat(jnp.arange(rows), GRID_W), ROPE_DIM // 2)
    ang_c = angles(jnp.tile(jnp.arange(GRID_W), rows), ROPE_DIM // 2)
    ct, st = jnp.cos(ang_t), jnp.sin(ang_t)
    ret = tables([ct, ct], [-st, st], N_HEADS)
    cr, sr, cc, sc = jnp.cos(ang_r), jnp.sin(ang_r), jnp.cos(ang_c), jnp.sin(ang_c)
    mla = tables([cr, cr, cc, cc], [-sr, sr, -sc, sc], N_HEADS)
    return ret, mla


def _ret_tables(logit):
    log_g = jax.nn.log_sigmoid(logit.astype(F32))
    lane_lg = jnp.repeat(log_g, HEAD_DIM, axis=1)
    idx = jnp.arange(CHUNK, dtype=F32)[:, None]
    kd = jnp.stack([jnp.exp(lane_lg[0][None, :] * (CHUNK - 1 - idx)), jnp.exp(lane_lg[1][None, :] * idx)])
    qd = jnp.stack([jnp.exp(lane_lg[0][None, :] * (idx + 1)), jnp.exp(lane_lg[1][None, :] * (CHUNK - idx))])
    cd = jnp.exp(lane_lg * CHUNK)[:, None, :]
    diff = idx - idx.T
    blocks = []
    for h in range(N_HEADS):
        f = jnp.exp(log_g[0, h] * jnp.where(diff >= 0, diff, 0.0))
        b = jnp.exp(log_g[1, h] * jnp.where(diff < 0, -diff, 0.0))
        blocks.append(jnp.where(diff >= 0, f, b))
    dm = jnp.concatenate(blocks, axis=1)
    return kd, cd, qd, dm


def _pack_w_in(w_in):
    d = w_in.shape[0]
    mla = jnp.concatenate([w_in[:, OFF_MLA:OFF_MLA + MLA_COLS], jnp.zeros((d, MLA_PAD - MLA_COLS), w_in.dtype)], 1)
    w = jnp.concatenate([w_in[:, OFF_RET:OFF_RET + RET_COLS], w_in[:, OFF_DN:OFF_DN + 4 * MIX_W],
                         w_in[:, OFF_SG:OFF_SG + SG_COLS], mla, w_in[:, OFF_GATE:OFF_GATE + GATE_COLS]], axis=1)
    wab = w_in[:, OFF_DN + 4 * MIX_W:OFF_DN + DN_COLS]
    wabc = jnp.concatenate([wab, jnp.zeros((d, 128 - 4 * N_HEADS), w_in.dtype)], axis=1)
    return w.astype(BF16), wab.T.astype(BF16), wabc.astype(BF16)


def _mla_weights(w_uq, w_ukv):
    dq = NOPE_DIM + ROPE_DIM
    dkv = NOPE_DIM + V_DIM
    wq = w_uq.reshape(Q_LORA, N_HEADS, dq)
    wqn = wq[:, :, :NOPE_DIM].reshape(Q_LORA, N_HEADS * NOPE_DIM)
    wqr = wq[:, :, NOPE_DIM:].reshape(Q_LORA, N_HEADS * ROPE_DIM)
    wkv = w_ukv.reshape(KV_LORA, N_HEADS, dkv)
    head_eye = jnp.eye(N_HEADS, dtype=F32)
    wa = jnp.einsum("chd,hg->hcgd", wkv[:, :, :NOPE_DIM], head_eye).reshape(N_HEADS, KV_LORA, N_HEADS * NOPE_DIM)
    wa = jnp.pad(wa, ((0, 0), (0, QK_W - KV_LORA), (0, 0)))
    wuv = jnp.einsum("chd,hg->hcgd", wkv[:, :, NOPE_DIM:], head_eye).reshape(N_HEADS, KV_LORA, MIX_W)
    selq = np.zeros((N_HEADS, QK_W, N_HEADS * ROPE_DIM), np.float32)
    for h in range(N_HEADS):
        selq[h, KV_LORA:KV_LORA + ROPE_DIM, h * ROPE_DIM:(h + 1) * ROPE_DIM] = np.eye(ROPE_DIM)
    selc = np.zeros((KV_LORA, QK_W), np.float32)
    selc[:, 0:KV_LORA] = np.eye(KV_LORA)
    selr = np.zeros((128, QK_W), np.float32)
    selr[0:ROPE_DIM, KV_LORA:KV_LORA + ROPE_DIM] = np.eye(ROPE_DIM)
    selv = np.zeros((VT_ROWS, KV_LORA), np.float32)
    selv[0:KV_LORA, :] = np.eye(KV_LORA)
    one_col = np.zeros((VT_ROWS, 1), np.float32)
    one_col[KV_LORA, 0] = 1.0
    return (tuple(jnp.asarray(a, BF16) for a in (wqn, wqr, wa, selq, selc, selr, selv))
            + (jnp.asarray(one_col), wuv.astype(BF16)))


def kernel(x, c, ctx, c_ctx, w_ada, b_ada, g_pre1, g_post1, g_pre2, g_post2, w_in, ret_decay_logit, sg_norm_g, sg_w, sg_b, dn_conv_w, dn_A_log, dn_dt_bias, dn_norm_g, mla_q_norm_g, mla_kv_norm_g, mla_w_uq, mla_w_ukv, w_branch, w_out, router_w, router_bias, moe_w1, moe_w3, moe_w2, shared_w1, shared_w3, shared_w2):
    nb, n_lat, d = x.shape
    n_ctx = ctx.shape[1]
    depth = w_in.shape[0]
    assert d == D_MODEL and n_lat % GRID_W == 0 and n_lat % CHUNK == 0 and n_ctx % 256 == 0
    ncc = n_ctx // CHUNK

    n_cond = -(-(nb + 1) // 8) * 8
    cond = jnp.concatenate([c, c_ctx[None], jnp.zeros((n_cond - nb - 1, d), F32)], axis=0)
    mod_all = _adaln(cond, w_ada, b_ada).reshape(depth, n_cond, 6, d)

    (ret_cos, ret_sin), (mla_cos, mla_sin) = _rope_tables(n_lat, n_ctx)
    perm_ret = _swap_perm(MIX_W, HEAD_DIM)
    perm_mla = _swap_perm(N_HEADS * ROPE_DIM, ROPE_DIM // 2)
    lane_head = jnp.arange(MIX_W) // HEAD_DIM
    bd = (lane_head[:, None] == lane_head[None, :]).astype(F32)
    ones_bd = bd.astype(BF16)
    rw_t = router_w.T.astype(BF16)
    rbias = router_bias.astype(F32)[:, None]

    xa = (ctx, x)
    for l in range(depth):
        last = l == depth - 1
        if last and isinstance(xa, tuple):
            xa = jnp.concatenate(xa, axis=1)
        mod = mod_all[l]
        w_l, wab_l, wabc_l = _pack_w_in(w_in[l])
        p, ab_t, ab_c = _inproj(xa, n_ctx + n_lat, mod, g_pre1[l][None], w_l, wab_l, wabc_l, n_ctx)

        kd, cd, qd, dm = _ret_tables(ret_decay_logit[l])
        wcat = jnp.concatenate([sg_w[l, h] for h in range(N_HEADS)], axis=1).astype(BF16)
        sg_bias = jnp.repeat(sg_b[l].T, HEAD_DIM, axis=1)
        y_ret, y_sg = _retention_and_sgate(p, ret_cos, ret_sin, perm_ret, (kd, cd, qd, dm, bd, ones_bd),
                                           sg_norm_g[l][None], wcat, sg_bias, ncc)

        neg_a = (-jnp.exp(dn_A_log[l].astype(F32))).reshape(2 * N_HEADS, 1)
        dtb = dn_dt_bias[l].astype(F32).reshape(2 * N_HEADS, 1)
        conv_w = jnp.concatenate([dn_conv_w[l], jnp.zeros((8 - CONV_W, 3 * MIX_W), F32)], axis=0)
        o_f, o_b = _deltanet(p, ab_t, ab_c, conv_w, neg_a, dtb, bd, ones_bd, ncc)

        row0 = n_ctx if last else 0
        y_mla, y_mla_ctx = _mla(p, mla_cos, mla_sin, perm_mla, mla_q_norm_g[l][None], mla_kv_norm_g[l][None],
                                *_mla_weights(mla_w_uq[l], mla_w_ukv[l]), n_ctx, not last)
        if not last:
            y_mla = jnp.concatenate([y_mla_ctx, y_mla], axis=1)

        xa = _merge(xa, y_ret, y_sg, o_f, o_b, y_mla, p, w_branch[l].astype(BF16), w_out[l].astype(BF16),
                    jnp.tile(dn_norm_g[l], N_HEADS)[None], g_post1[l][None], mod, ones_bd, n_ctx, row0)

        xa = _moe(xa, mod, g_pre2[l][None], g_post2[l][None], rw_t, rbias, shared_w1, shared_w3, shared_w2,
                  moe_w1, moe_w3, moe_w2, l, n_ctx - row0)
    return xa
```

```python
import functools
import math

import jax
import jax.numpy as jnp
import numpy as np
from jax import lax
from jax.experimental import pallas as pl
from jax.experimental.pallas import tpu as pltpu

F32 = jnp.float32
BF16 = jnp.bfloat16
HIGHEST = lax.Precision.HIGHEST

D_MODEL = 1024
GRID_W = 64
N_HEADS = 4
HEAD_DIM = 64
MIX_W = N_HEADS * HEAD_DIM
CHUNK = 128
ROPE_BASE = 10000.0
EPS = 1e-6
RET_DECAY_EXP0 = 5.0
CONV_W = 5
Q_LORA = 256
KV_LORA = 128
NOPE_DIM = 64
ROPE_DIM = 32
V_DIM = 64
N_EXPERTS = 16
N_GROUPS = 4
EXPERTS_PER_GROUP = N_EXPERTS // N_GROUPS
D_EXPERT = 256
N_BRANCH = 4

RET_COLS = 4 * MIX_W
SG_COLS = 2 * MIX_W
DN_COLS = 4 * MIX_W + 4 * N_HEADS
MLA_COLS = Q_LORA + KV_LORA + ROPE_DIM
GATE_COLS = N_BRANCH * D_MODEL
OFF_RET = 0
OFF_SG = OFF_RET + RET_COLS
OFF_DN = OFF_SG + SG_COLS
OFF_MLA = OFF_DN + DN_COLS
OFF_GATE = OFF_MLA + MLA_COLS

P_RET = 0
P_DN = 1024
P_SG = 2048
P_MLA = 2560
P_GATE = 3072
P_COLS = 7168
MLA_PAD = 512

VMEM_LIMIT = 56 * 1024 * 1024


def _dot(a, b, precision=None):
    return jnp.dot(a, b, preferred_element_type=F32, precision=precision)


def _dot_nt(a, b, precision=None):
    return lax.dot_general(a, b, (((1,), (1,)), ((), ())), preferred_element_type=F32, precision=precision)


def _dot_tn(a, b):
    return lax.dot_general(a, b, (((0,), (0,)), ((), ())), preferred_element_type=F32)


def _mm(a, b):
    return _dot(a.astype(BF16), b.astype(BF16))


def _params(*sem):
    return pltpu.CompilerParams(dimension_semantics=sem, vmem_limit_bytes=VMEM_LIMIT)


def _pick(n, cands):
    for c in cands:
        if n % c == 0:
            return c
    raise ValueError(f"no tile for {n}")


def _head_of_lane(width, group):
    return lax.broadcasted_iota(jnp.int32, (1, width), 1) // group


def _stack_heads(x):
    head = _head_of_lane(MIX_W, HEAD_DIM)
    xf = x.astype(F32)
    return jnp.concatenate([jnp.where(head == h, xf, 0.0).astype(BF16) for h in range(N_HEADS)], axis=0)


def _expand_heads(cols):
    head = _head_of_lane(MIX_W, HEAD_DIM)
    out = cols[:, N_HEADS - 1:N_HEADS]
    for h in range(N_HEADS - 2, -1, -1):
        out = jnp.where(head <= h, cols[:, h:h + 1], out)
    return out


def _head_sum(x, ones_bd):
    hi = x.astype(BF16)
    lo = (x - hi.astype(F32)).astype(BF16)
    return _dot(hi, ones_bd) + _dot(lo, ones_bd)


def _rot(x_bf, cos, sin, perm):
    return x_bf.astype(F32) * cos + _dot(x_bf, perm) * sin


def _norm_modulate(x, g, is_ctx, mc_ref, ml_ref, shift_row, scale_row):
    shift = jnp.where(is_ctx, mc_ref[0, shift_row:shift_row + 1, :], ml_ref[0, shift_row:shift_row + 1, :])
    scale = jnp.where(is_ctx, mc_ref[0, scale_row:scale_row + 1, :], ml_ref[0, scale_row:scale_row + 1, :])
    gain = g * (1.0 + scale)
    return x * lax.rsqrt(jnp.mean(x * x, axis=-1, keepdims=True) + EPS) * gain + shift


def _adaln_kernel(c_ref, w_ref, b_ref, o_ref):
    c = c_ref[...]
    s = c * jax.nn.sigmoid(c)
    o_ref[0] = _dot(s, w_ref[0], precision=HIGHEST) + b_ref[0]


def _adaln(cond, w_ada, b_ada):
    n_l, d, d6 = w_ada.shape
    r = cond.shape[0]
    tn = 1024
    return pl.pallas_call(
        _adaln_kernel,
        grid=(n_l, d6 // tn),
        in_specs=[pl.BlockSpec((r, d), lambda l, j: (0, 0)),
                  pl.BlockSpec((1, d, tn), lambda l, j: (l, 0, j)),
                  pl.BlockSpec((1, 1, tn), lambda l, j: (l, 0, j))],
        out_specs=pl.BlockSpec((1, r, tn), lambda l, j: (l, 0, j)),
        out_shape=jax.ShapeDtypeStruct((n_l, r, d6), F32),
        compiler_params=_params("arbitrary", "arbitrary"),
        name="adaln",
    )(cond, w_ada, b_ada.reshape(n_l, 1, d6))


def _inproj_kernel(x_ref, ctx_ref, ml_ref, mc_ref, g_ref, w_ref, wab_ref, wabc_ref, p_ref, ab_ref, abc_ref, xn_ref,
                   *, tm, tn, rb, n_ctx, split):
    i = pl.program_id(1)
    j = pl.program_id(2)

    @pl.when(j == 0)
    def _():
        def blk(r, carry):
            r0 = pl.multiple_of(r * rb, rb)
            is_ctx = i * tm + r0 < n_ctx
            if split:
                x_off = pl.multiple_of(jnp.maximum(jnp.where(i == 0, r0 - n_ctx, r0), 0), rb)
                c_off = pl.multiple_of(jnp.minimum(r0, n_ctx - rb), rb)
                x = jnp.where(is_ctx, ctx_ref[0, pl.ds(c_off, rb), :], x_ref[0, pl.ds(x_off, rb), :])
            else:
                x = x_ref[0, pl.ds(r0, rb), :]
            hn = _norm_modulate(x, g_ref[...], is_ctx, mc_ref, ml_ref, 0, 1)
            xn_ref[pl.ds(r0, rb), :] = hn.astype(BF16)
            return carry

        lax.fori_loop(0, tm // rb, blk, 0)
        ab_ref[0] = _dot_nt(wab_ref[...], xn_ref[...])
        abc_ref[0] = _dot(xn_ref[...], wabc_ref[...])

    first_gate = P_GATE // tn
    col0 = P_GATE - first_gate * tn

    @pl.when(j < first_gate)
    def _():
        p_ref[0] = _dot(xn_ref[...], w_ref[...]).astype(BF16)

    if col0:
        @pl.when(j == first_gate)
        def _():
            r = _dot(xn_ref[...], w_ref[...])
            p_ref[0, :, 0:col0] = r[:, 0:col0].astype(BF16)
            p_ref[0, :, col0:tn] = jax.nn.sigmoid(r[:, col0:tn]).astype(BF16)

    @pl.when(j > first_gate if col0 else j >= first_gate)
    def _():
        p_ref[0] = jax.nn.sigmoid(_dot(xn_ref[...], w_ref[...])).astype(BF16)


def _token_sources(tokens, n_ctx, tm):
    if isinstance(tokens, tuple):
        ctx, x = tokens
        d = x.shape[-1]
        if x.shape[1] >= tm and tm > n_ctx:
            x_spec = pl.BlockSpec(
                (pl.Element(1), pl.Element(tm), pl.Element(d)),
                lambda b, i, *_: (b, pl.multiple_of(jnp.maximum(i * tm - n_ctx, 0), 128), 0))
            ctx_spec = pl.BlockSpec((1, n_ctx, d), lambda b, i, *_: (b, 0, 0))
            return (x, ctx), (x_spec, ctx_spec), True
        tokens = jnp.concatenate([ctx, x], axis=1)
    nb, _, d = tokens.shape
    dummy = jnp.zeros((nb, 8, d), tokens.dtype)
    return ((tokens, dummy), (pl.BlockSpec((1, tm, d), lambda b, i, *_: (b, i, 0)),
                              pl.BlockSpec((1, 8, d), lambda b, i, *_: (b, 0, 0))), False)


def _inproj(tokens, nt, mod, g, w, wab, wabc, n_ctx):
    tm = _pick(nt, (1408, 768, 384, 256, 128))
    tn = 1792
    (xa, ctx), (x_spec, ctx_spec), split = _token_sources(tokens, n_ctx, tm)
    nb, _, d = xa.shape
    kern = functools.partial(_inproj_kernel, tm=tm, tn=tn, rb=128, n_ctx=n_ctx, split=split)
    return pl.pallas_call(
        kern,
        grid=(nb, nt // tm, P_COLS // tn),
        in_specs=[x_spec, ctx_spec,
                  pl.BlockSpec((1, 6, d), lambda b, i, j: (b, 0, 0)),
                  pl.BlockSpec((1, 6, d), lambda b, i, j: (nb, 0, 0)),
                  pl.BlockSpec((1, d), lambda b, i, j: (0, 0)),
                  pl.BlockSpec((d, tn), lambda b, i, j: (0, j)),
                  pl.BlockSpec((16, d), lambda b, i, j: (0, 0)),
                  pl.BlockSpec((d, 128), lambda b, i, j: (0, 0))],
        out_specs=[pl.BlockSpec((1, tm, tn), lambda b, i, j: (b, i, j)),
                   pl.BlockSpec((1, 16, tm), lambda b, i, j: (b, 0, i)),
                   pl.BlockSpec((1, tm, 128), lambda b, i, j: (b, i, 0))],
        out_shape=[jax.ShapeDtypeStruct((nb, nt, P_COLS), BF16),
                   jax.ShapeDtypeStruct((nb, 16, nt), F32),
                   jax.ShapeDtypeStruct((nb, nt, 128), F32)],
        scratch_shapes=[pltpu.VMEM((tm, d), BF16)],
        compiler_params=_params("arbitrary", "arbitrary", "arbitrary"),
        name="inproj",
    )(xa, ctx, mod, mod, g, w, wab, wabc)


def _bwd_chunk(t, ncc, nc):
    return jnp.where(t < ncc, ncc - 1 - t, nc - 1 - (t - ncc))


def _ret_state_kernel(pf_ref, pb_ref, cf_ref, sf_ref, cb_ref, sb_ref, perm_ref, kd_ref, cd_ref, bd_ref,
                      of_ref, ob_ref, st_f, st_b, *, cb):
    t = pl.program_id(1)

    @pl.when(t == 0)
    def _():
        st_f[...] = jnp.zeros_like(st_f)
        st_b[...] = jnp.zeros_like(st_b)

    def increments(p_ref, c_ref, s_ref, d):
        out = []
        for i in range(cb):
            r = slice(i * CHUNK, (i + 1) * CHUNK)
            kr = _rot(p_ref[0, r, 0:MIX_W], c_ref[r, :], s_ref[r, :], perm_ref[...]) * (HEAD_DIM ** -0.5)
            out.append(bd_ref[...] * _dot_tn((kr * kd_ref[d]).astype(BF16), p_ref[0, r, MIX_W:2 * MIX_W]))
        return out

    inc_f = increments(pf_ref, cf_ref, sf_ref, 0)
    inc_b = increments(pb_ref, cb_ref, sb_ref, 1)
    s = st_f[...]
    for i in range(cb):
        of_ref[0, i] = s.astype(BF16)
        s = cd_ref[0] * s + inc_f[i]
    st_f[...] = s
    s = st_b[...]
    for i in reversed(range(cb)):
        ob_ref[0, i] = s.astype(BF16)
        s = cd_ref[1] * s + inc_b[i]
    st_b[...] = s


def _gelu_tanh(x):
    return 0.5 * x * (1.0 + jnp.tanh(math.sqrt(2.0 / math.pi) * (x + 0.044715 * (x * x * x))))


def _mix_out_kernel(p_ref, pg_ref, c_ref, s_ref, sf_ref, sb_ref, perm_ref, dm_ref, qd_ref, ones_ref,
                    ng_ref, wg_ref, bg_ref, y_ref, ysg_ref, *, cb):
    chunks = range(cb)
    rows = [slice(i * CHUNK, (i + 1) * CHUNK) for i in chunks]
    perm, dm, ones_bd = perm_ref[...], dm_ref[...], ones_ref[...]
    p = [p_ref[0, r, :] for r in rows]
    cos = [c_ref[r, :] for r in rows]
    sin = [s_ref[r, :] for r in rows]
    qr = [_rot(p[i][:, 0:MIX_W], cos[i], sin[i], perm) for i in chunks]
    kr = [_rot(p[i][:, MIX_W:2 * MIX_W], cos[i], sin[i], perm) * (HEAD_DIM ** -0.5) for i in chunks]
    z = [_gelu_tanh(pg_ref[0, r, :].astype(F32)) for r in rows]
    vg = [x[:, MIX_W:] for x in z]
    mu_g = [jnp.mean(x, axis=-1, keepdims=True) for x in vg]
    vgc = [x - m for x, m in zip(vg, mu_g)]
    var_g = [jnp.mean(x * x, axis=-1, keepdims=True) for x in vgc]
    vn = [x * lax.rsqrt(s + EPS) * ng_ref[...] for x, s in zip(vgc, var_g)]
    sc = [_dot_nt(qr[i].astype(BF16), _stack_heads(kr[i])) * dm for i in chunks]
    mixed = [_dot(wg_ref[...], _stack_heads(x)) for x in vn]
    o = [_dot(sc[i].astype(BF16), _stack_heads(p[i][:, 2 * MIX_W:3 * MIX_W])) for i in chunks]
    qs = [jnp.concatenate([(qr[i] * qd_ref[0]).astype(BF16), (qr[i] * qd_ref[1]).astype(BF16)], axis=1)
          for i in chunks]
    ss = [jnp.concatenate([sf_ref[0, i], sb_ref[0, i]], axis=0) for i in chunks]
    o = [o[i] + _dot(qs[i], ss[i]) for i in chunks]
    for i in chunks:
        ysg_ref[0, rows[i], :] = (z[i][:, :MIX_W] * (mixed[i] + bg_ref[...])).astype(BF16)
    mu = [_head_sum(x, ones_bd) * (1.0 / HEAD_DIM) for x in o]
    oc = [x - m for x, m in zip(o, mu)]
    var = [_head_sum(x * x, ones_bd) * (1.0 / HEAD_DIM) for x in oc]
    for i in chunks:
        g = p[i][:, 3 * MIX_W:4 * MIX_W].astype(F32)
        y_ref[0, rows[i], :] = (oc[i] * lax.rsqrt(var[i] + EPS) * (g * jax.nn.sigmoid(g))).astype(BF16)


def _retention_and_sgate(p, cos, sin, perm, tabs, sg_ng, sg_w, sg_bias, ncc):
    nb, nt, _ = p.shape
    nc = nt // CHUNK
    kd, cd, qd, dm, bd, ones_bd = tabs
    cb = 2
    assert nc % cb == 0 and ncc % cb == 0
    nblk, ncb = nc // cb, ncc // cb
    fwd = lambda b, t: (b, t, 0)
    bwd = lambda b, t: (b, _bwd_chunk(t, ncb, nblk), 0)
    tab_f = lambda b, t: (t, 0)
    tab_b = lambda b, t: (_bwd_chunk(t, ncb, nblk), 0)
    c2 = lambda b, t: (0, 0)
    c3 = lambda b, t: (0, 0, 0)
    st_shape = jax.ShapeDtypeStruct((nb, nc, MIX_W, MIX_W), BF16)

    def kv_window(block_of):
        return pl.BlockSpec((pl.Element(1), pl.Element(cb * CHUNK), pl.Element(2 * MIX_W)),
                            lambda b, t: (b, pl.multiple_of(block_of(t) * (cb * CHUNK), 128), P_RET + MIX_W))

    st_f, st_b = pl.pallas_call(
        functools.partial(_ret_state_kernel, cb=cb),
        grid=(nb, nblk),
        in_specs=[kv_window(lambda t: t), kv_window(lambda t: _bwd_chunk(t, ncb, nblk)),
                  pl.BlockSpec((cb * CHUNK, MIX_W), tab_f), pl.BlockSpec((cb * CHUNK, MIX_W), tab_f),
                  pl.BlockSpec((cb * CHUNK, MIX_W), tab_b), pl.BlockSpec((cb * CHUNK, MIX_W), tab_b),
                  pl.BlockSpec((MIX_W, MIX_W), c2),
                  pl.BlockSpec((2, CHUNK, MIX_W), c3),
                  pl.BlockSpec((2, 1, MIX_W), c3),
                  pl.BlockSpec((MIX_W, MIX_W), c2)],
        out_specs=[pl.BlockSpec((1, cb, MIX_W, MIX_W), lambda b, t: (b, t, 0, 0)),
                   pl.BlockSpec((1, cb, MIX_W, MIX_W), lambda b, t: (b, _bwd_chunk(t, ncb, nblk), 0, 0))],
        out_shape=[st_shape, st_shape],
        scratch_shapes=[pltpu.VMEM((MIX_W, MIX_W), F32), pltpu.VMEM((MIX_W, MIX_W), F32)],
        compiler_params=_params("arbitrary", "arbitrary"),
        name="ret_state",
    )(p, p, cos, sin, cos, sin, perm, kd, cd, bd)
    blk = lambda b, t: (b, t, 0)
    y_shape = jax.ShapeDtypeStruct((nb, nt, MIX_W), BF16)
    cb = _pick(nc, (6, 3, 2, 1))
    return pl.pallas_call(
        functools.partial(_mix_out_kernel, cb=cb),
        grid=(nb, nc // cb),
        in_specs=[pl.BlockSpec((1, cb * CHUNK, RET_COLS), blk),
                  pl.BlockSpec((1, cb * CHUNK, SG_COLS), lambda b, t: (b, t, P_SG // SG_COLS)),
                  pl.BlockSpec((cb * CHUNK, MIX_W), tab_f), pl.BlockSpec((cb * CHUNK, MIX_W), tab_f),
                  pl.BlockSpec((1, cb, MIX_W, MIX_W), lambda b, t: (b, t, 0, 0)),
                  pl.BlockSpec((1, cb, MIX_W, MIX_W), lambda b, t: (b, t, 0, 0)),
                  pl.BlockSpec((MIX_W, MIX_W), c2),
                  pl.BlockSpec((CHUNK, N_HEADS * CHUNK), c2),
                  pl.BlockSpec((2, CHUNK, MIX_W), c3),
                  pl.BlockSpec((MIX_W, MIX_W), c2),
                  pl.BlockSpec((1, MIX_W), c2),
                  pl.BlockSpec((CHUNK, N_HEADS * CHUNK), c2),
                  pl.BlockSpec((CHUNK, MIX_W), c2)],
        out_specs=[pl.BlockSpec((1, cb * CHUNK, MIX_W), blk), pl.BlockSpec((1, cb * CHUNK, MIX_W), blk)],
        out_shape=[y_shape, y_shape],
        compiler_params=_params("arbitrary", "arbitrary"),
        name="mix_out",
    )(p, p, cos, sin, st_f, st_b, perm, dm, qd, ones_bd, sg_ng, sg_w, sg_bias)


def _softplus(a):
    return jnp.maximum(a, 0.0) + jnp.log1p(jnp.exp(-jnp.abs(a)))


def _dn_prep_kernel(pc_ref, pp_ref, pn_ref, ab_ref, abc_ref, cw_ref, na_ref, dtb_ref, nar_ref, dtbr_ref, ones_ref,
                    qkv_ref, gb_ref, gbc_ref, xe_ref, *, rows, ctx_blocks, n_blocks):
    t = pl.program_id(1)
    w3 = 3 * MIX_W
    prev_ok = jnp.where((t != 0) & (t != ctx_blocks), 1.0, 0.0)
    next_ok = jnp.where((t != ctx_blocks - 1) & (t != n_blocks - 1), 1.0, 0.0)
    tail = pp_ref[0, rows - 16:rows, 0:w3].astype(F32)
    head = pn_ref[0, 0:16, 0:w3].astype(F32)
    xe_ref[0:8, :] = tail[8:16, :] * prev_ok
    xe_ref[8:8 + rows, :] = pc_ref[0, :, 0:w3].astype(F32)
    xe_ref[8 + rows:16 + rows, :] = head[0:8, :] * next_ok
    pad = CONV_W // 2
    y = xe_ref[8 - pad:8 - pad + rows, :] * cw_ref[0:1, :]
    for i in range(1, CONV_W):
        y = y + xe_ref[8 - pad + i:8 - pad + i + rows, :] * cw_ref[i:i + 1, :]
    y = y * jax.nn.sigmoid(y)
    q = y[:, 0:MIX_W]
    k = y[:, MIX_W:2 * MIX_W]
    v = y[:, 2 * MIX_W:w3]
    ones_bd = ones_ref[...]
    qn = q * lax.rsqrt(_head_sum(q * q, ones_bd) + EPS) * (HEAD_DIM ** -0.5)
    kn = k * lax.rsqrt(_head_sum(k * k, ones_bd) + EPS)
    qkv_ref[0, :, 0:MIX_W] = qn.astype(BF16)
    qkv_ref[0, :, MIX_W:2 * MIX_W] = kn.astype(BF16)
    qkv_ref[0, :, 2 * MIX_W:w3] = v.astype(BF16)
    ab = ab_ref[0]
    gb_ref[0, 0:8, :] = na_ref[...] * _softplus(ab[0:8, :] + dtb_ref[...])
    gb_ref[0, 8:16, :] = jax.nn.sigmoid(ab[8:16, :])
    abc = abc_ref[0]
    lane = lax.broadcasted_iota(jnp.int32, (1, 128), 1)
    g_c = nar_ref[...] * _softplus(abc + dtbr_ref[...])
    gbc_ref[0] = jnp.where(lane < 8, g_c, jnp.where(lane < 16, jax.nn.sigmoid(abc), 0.0))


def _split3(x):
    hi = x.astype(BF16)
    r = x - hi.astype(F32)
    mid = r.astype(BF16)
    lo = (r - mid.astype(F32)).astype(BF16)
    return hi, mid, lo


def _tri_inverse(mats, ii, jj):
    eye = jnp.where(ii == jj, 1.0, 0.0)
    nd = [jnp.where((ii // 16) == (jj // 16), n, 0.0) for n in mats]
    p1 = [_mm(x, x) for x in nd]
    m = [eye - x for x in nd]
    p2 = [_mm(x, x) for x in p1]
    m = [x + _mm(x, y) for x, y in zip(m, p1)]
    p3 = [_mm(x, x) for x in p2]
    m = [x + _mm(x, y) for x, y in zip(m, p2)]
    m = [x + _mm(x, y) for x, y in zip(m, p3)]
    for lvl in (16, 32, 64):
        off_mask = ((ii // (2 * lvl)) == (jj // (2 * lvl))) & ((ii // lvl) != (jj // lvl))
        t = [_mm(jnp.where(off_mask, n, 0.0), x) for n, x in zip(mats, m)]
        m = [x - _mm(x, y) for x, y in zip(m, t)]
    return m


def _dn_pre(qkv, g, gbc, d, lower):
    c = CHUNK
    qn = qkv[:, 0:MIX_W]
    kn = qkv[:, MIX_W:2 * MIX_W]
    v = qkv[:, 2 * MIX_W:3 * MIX_W]
    ii = lax.broadcasted_iota(jnp.int32, (c, c), 0)
    jj = lax.broadcasted_iota(jnp.int32, (c, c), 1)
    incl = (ii >= jj) if lower else (ii <= jj)
    tri = jnp.where(incl, 1.0, 0.0).astype(BF16)
    g_row = sum(_dot_nt(part, tri) for part in _split3(g))[N_HEADS * d:N_HEADS * (d + 1), :]
    cum = sum(_dot(tri, part) for part in _split3(gbc))
    g_col = cum[:, N_HEADS * d:N_HEADS * (d + 1)]
    b_col = gbc[:, 2 * N_HEADS + N_HEADS * d:2 * N_HEADS + N_HEADS * (d + 1)]
    g_cols4 = jnp.concatenate([jnp.broadcast_to(g_col[:, h:h + 1], (c, c)) for h in range(N_HEADS)], axis=1)
    b_cols4 = jnp.concatenate([jnp.broadcast_to(b_col[:, h:h + 1], (c, c)) for h in range(N_HEADS)], axis=1)
    g_rows4 = jnp.concatenate([g_row[h:h + 1, :] for h in range(N_HEADS)], axis=1)
    incl4 = jnp.concatenate([incl] * N_HEADS, axis=1)
    diag4 = jnp.concatenate([ii == jj] * N_HEADS, axis=1)
    decay = jnp.where(incl4, jnp.exp(jnp.where(incl4, g_cols4 - g_rows4, 0.0)), 0.0)
    kstack = _stack_heads(kn)
    kk = _dot_nt(kn, kstack)
    qk = _dot_nt(qn, kstack)
    n_mat = jnp.where(diag4, 0.0, decay * kk * b_cols4)
    attn = (decay * qk).astype(BF16)
    g256 = _expand_heads(g_col)
    eg256 = jnp.exp(g256)
    b256 = _expand_heads(b_col)
    vb = v.astype(F32) * b256
    kbg = kn.astype(F32) * b256 * eg256
    rhs = jnp.concatenate([_stack_heads(vb), _stack_heads(kbg)], axis=1)
    g_last = g256[c - 1:c, :] if lower else g256[0:1, :]
    kdec = (kn.astype(F32) * jnp.exp(g_last - g256)).astype(BF16)
    n_heads = [n_mat[:, h * c:(h + 1) * c] for h in range(N_HEADS)]
    return n_heads, dict(qn=qn, attn=attn, rhs=rhs, eg=eg256, kdec=kdec, sdec=jnp.exp(g_last))


def _dn_post(z, s_prev, bd):
    s_bf = s_prev.astype(BF16)
    w = z["u"] - _dot(z["wk"], s_bf)
    o = z["eg"] * _dot(z["qn"], s_bf) + _dot(z["attn"], _stack_heads(w))
    s_next = z["sdec"] * s_prev + bd * _dot_tn(z["kdec"], w.astype(BF16))
    return o, s_next


def _dn_scan_kernel(qf_ref, qb_ref, gf_ref, gb_ref, gcf_ref, gcb_ref, bd_ref, of_ref, ob_ref, st_f, st_b, *, cb):
    t = pl.program_id(1)

    @pl.when(t == 0)
    def _():
        st_f[...] = jnp.zeros_like(st_f)
        st_b[...] = jnp.zeros_like(st_b)

    bd = bd_ref[...]
    rows = [slice(i * CHUNK, (i + 1) * CHUNK) for i in range(cb)]
    mats, pres = [], []
    for d, (q_ref, g_ref, gc_ref) in enumerate(((qf_ref, gf_ref, gcf_ref), (qb_ref, gb_ref, gcb_ref))):
        for r in rows:
            n_heads, pre = _dn_pre(q_ref[0, r, :], g_ref[0, :, r], gc_ref[0, r, :], d, d == 0)
            mats += n_heads
            pres.append(pre)
    ii = lax.broadcasted_iota(jnp.int32, (CHUNK, CHUNK), 0)
    jj = lax.broadcasted_iota(jnp.int32, (CHUNK, CHUNK), 1)
    inv = _tri_inverse(mats, ii, jj)
    for n, pre in enumerate(pres):
        a_inv = jnp.concatenate(inv[N_HEADS * n:N_HEADS * (n + 1)], axis=1).astype(BF16)
        uw = _dot(a_inv, pre["rhs"])
        pre["u"] = uw[:, 0:MIX_W]
        pre["wk"] = uw[:, MIX_W:2 * MIX_W].astype(BF16)
    s_f, s_b = st_f[...], st_b[...]
    for k in range(cb):
        o, s_f = _dn_post(pres[k], s_f, bd)
        of_ref[0, rows[k], :] = o
        o, s_b = _dn_post(pres[cb + cb - 1 - k], s_b, bd)
        ob_ref[0, rows[cb - 1 - k], :] = o
    st_f[...] = s_f
    st_b[...] = s_b


def _deltanet(p, ab_t, ab_c, conv_w, neg_a, dtb, bd, ones_bd, ncc):
    nb, nt, _ = p.shape
    nc = nt // CHUNK
    w3 = 3 * MIX_W
    c2 = lambda b, t: (0, 0)
    dn_blk = P_DN // RET_COLS
    pad_lanes = lambda col: jnp.concatenate([col.reshape(1, -1), jnp.zeros((1, 128 - col.size), F32)], axis=1)
    pr = math.gcd(math.gcd(ncc * CHUNK, nt), 256)
    n_pb, ctx_pb = nt // pr, ncc * CHUNK // pr
    qkv, gbeta, gbeta_c = pl.pallas_call(
        functools.partial(_dn_prep_kernel, rows=pr, ctx_blocks=ctx_pb, n_blocks=n_pb),
        grid=(nb, n_pb),
        in_specs=[pl.BlockSpec((1, pr, 4 * MIX_W), lambda b, t: (b, t, dn_blk)),
                  pl.BlockSpec((1, pr, 4 * MIX_W), lambda b, t: (b, jnp.maximum(t - 1, 0), dn_blk)),
                  pl.BlockSpec((1, pr, 4 * MIX_W), lambda b, t: (b, jnp.minimum(t + 1, n_pb - 1), dn_blk)),
                  pl.BlockSpec((1, 16, pr), lambda b, t: (b, 0, t)),
                  pl.BlockSpec((1, pr, 128), lambda b, t: (b, t, 0)),
                  pl.BlockSpec((8, w3), c2),
                  pl.BlockSpec((8, 1), c2),
                  pl.BlockSpec((8, 1), c2),
                  pl.BlockSpec((1, 128), c2),
                  pl.BlockSpec((1, 128), c2),
                  pl.BlockSpec((MIX_W, MIX_W), c2)],
        out_specs=[pl.BlockSpec((1, pr, w3), lambda b, t: (b, t, 0)),
                   pl.BlockSpec((1, 16, pr), lambda b, t: (b, 0, t)),
                   pl.BlockSpec((1, pr, 128), lambda b, t: (b, t, 0))],
        out_shape=[jax.ShapeDtypeStruct((nb, nt, w3), BF16),
                   jax.ShapeDtypeStruct((nb, 16, nt), F32),
                   jax.ShapeDtypeStruct((nb, nt, 128), F32)],
        scratch_shapes=[pltpu.VMEM((pr + 16, w3), F32)],
        compiler_params=_params("arbitrary", "arbitrary"),
        name="dn_prep",
    )(p, p, p, ab_t, ab_c, conv_w, neg_a, dtb, pad_lanes(neg_a), pad_lanes(dtb), ones_bd)
    cb = 2
    assert nc % cb == 0 and ncc % cb == 0
    rows = cb * CHUNK
    cur_b = lambda t: _bwd_chunk(t, ncc // cb, nc // cb)
    o_shape = jax.ShapeDtypeStruct((nb, nt, MIX_W), F32)
    return pl.pallas_call(
        functools.partial(_dn_scan_kernel, cb=cb),
        grid=(nb, nc // cb),
        in_specs=[pl.BlockSpec((1, rows, w3), lambda b, t: (b, t, 0)),
                  pl.BlockSpec((1, rows, w3), lambda b, t: (b, cur_b(t), 0)),
                  pl.BlockSpec((1, 16, rows), lambda b, t: (b, 0, t)),
                  pl.BlockSpec((1, 16, rows), lambda b, t: (b, 0, cur_b(t))),
                  pl.BlockSpec((1, rows, 128), lambda b, t: (b, t, 0)),
                  pl.BlockSpec((1, rows, 128), lambda b, t: (b, cur_b(t), 0)),
                  pl.BlockSpec((MIX_W, MIX_W), c2)],
        out_specs=[pl.BlockSpec((1, rows, MIX_W), lambda b, t: (b, t, 0)),
                   pl.BlockSpec((1, rows, MIX_W), lambda b, t: (b, cur_b(t), 0))],
        out_shape=[o_shape, o_shape],
        scratch_shapes=[pltpu.VMEM((MIX_W, MIX_W), F32), pltpu.VMEM((MIX_W, MIX_W), F32)],
        compiler_params=_params("arbitrary", "arbitrary"),
        name="dn_scan",
    )(qkv, qkv, gbeta, gbeta, gbeta_c, gbeta_c, bd)


QK_W = 256


VT_ROWS = 144


def _mla_prep_kernel(p_ref, c_ref, s_ref, perm_ref, qg_ref, kg_ref, wqn_ref, wqr_ref, wa_ref, selq_ref, selc_ref,
                     selr_ref, selv_ref, one_ref, qt_ref, kv_ref, vt_ref, *, scale):
    p = p_ref[0]
    cos, sin, perm = c_ref[...], s_ref[...], perm_ref[...]
    cq = p[:, 0:Q_LORA].astype(F32)
    cqn = (cq * lax.rsqrt(jnp.mean(cq * cq, axis=-1, keepdims=True) + EPS) * qg_ref[...]).astype(BF16)
    q_nope = _dot(cqn, wqn_ref[...]).astype(BF16)
    q_rope = _dot(cqn, wqr_ref[...]).astype(BF16)
    q_rot = (_rot(q_rope, cos, sin, perm) * scale).astype(BF16)
    q_nope_s = (q_nope.astype(F32) * scale).astype(BF16)
    for h in range(N_HEADS):
        qt_ref[0, h] = (_dot_nt(wa_ref[h], q_nope_s) + _dot_nt(selq_ref[h], q_rot)).astype(BF16)
    ckv = p[:, Q_LORA:Q_LORA + KV_LORA].astype(F32)
    ckvn = (ckv * lax.rsqrt(jnp.mean(ckv * ckv, axis=-1, keepdims=True) + EPS) * kg_ref[...]).astype(BF16)
    kr = p[:, Q_LORA + KV_LORA:MLA_PAD]
    kr_rot = _rot(kr, cos, sin, perm).astype(BF16)
    kv_ref[0] = (_dot(ckvn, selc_ref[...]) + _dot(kr_rot, selr_ref[...])).astype(BF16)
    vt_ref[0] = (_dot_nt(selv_ref[...], ckvn) + one_ref[...]).astype(BF16)


def _mla_attn_kernel(qt_ref, kv_ref, vt_ref, wuv_ref, y_ref, m_ref, acc_ref, s_ref, *, tk, n_ctx, nt, latent):
    heads = range(N_HEADS)
    m_ref[...] = jnp.full_like(m_ref, -jnp.inf)
    acc_ref[...] = jnp.zeros_like(acc_ref)

    def scores(j0, size, slot):
        k = kv_ref[0, pl.ds(j0, size), :]
        for h in heads:
            s_ref[slot, h, 0:size, :] = _dot(k, qt_ref[0, h])

    def softmax_pv(j0, size, slot):
        vt = vt_ref[0, :, pl.ds(j0, size)]
        s = [s_ref[slot, h, 0:size, :] for h in heads]
        m_old = [m_ref[h] for h in heads]
        m_new = [jnp.maximum(m_old[h], jnp.max(s[h], axis=0, keepdims=True)) for h in heads]
        pr = [jnp.exp2(s[h] - m_new[h]).astype(BF16) for h in heads]
        pv = [_dot(vt, pr[h]) for h in heads]
        for h in heads:
            acc_ref[h] = jnp.exp2(m_old[h] - m_new[h]) * acc_ref[h] + pv[h]
            m_ref[h] = m_new[h]

    scores(0, n_ctx, 0)
    if not latent:
        softmax_pv(0, n_ctx, 0)
    else:
        n_tiles = (nt - n_ctx) // tk
        last = n_ctx + (n_tiles - 1) * tk
        scores(n_ctx, tk, 1)
        softmax_pv(0, n_ctx, 0)

        def body(jj, carry):
            t0 = pl.multiple_of(n_ctx + 2 * jj * tk, 256)
            t1 = pl.multiple_of(jnp.minimum(t0 + tk, last), 256)
            t2 = pl.multiple_of(jnp.minimum(t0 + 2 * tk, last), 256)
            scores(t1, tk, 0)
            softmax_pv(t0, tk, 1)
            scores(t2, tk, 1)
            softmax_pv(t1, tk, 0)
            return carry

        lax.fori_loop(0, n_tiles // 2, body, 0)
        if n_tiles % 2:
            softmax_pv(last, tk, 1)

    y = None
    for h in range(N_HEADS):
        acc = acc_ref[h]
        o = (acc[0:KV_LORA, :] / acc[KV_LORA:KV_LORA + 1, :]).astype(BF16)
        term = _dot_tn(o, wuv_ref[h])
        y = term if y is None else y + term
    y_ref[0] = y.astype(BF16)


def _mla(p, cos, sin, perm, qg, kg, wqn, wqr, wa, selq, selc, selr, selv, one_col, wuv, n_ctx, ctx_out):
    nb, nt, _ = p.shape
    n_lat = nt - n_ctx
    tm = _pick(nt, (768, 384, 256, 128))
    scale = (NOPE_DIM + ROPE_DIM) ** -0.5 * math.log2(math.e)
    c2 = lambda b, i: (0, 0)
    c3 = lambda b, i: (0, 0, 0)
    qt, kv, vt = pl.pallas_call(
        functools.partial(_mla_prep_kernel, scale=scale),
        grid=(nb, nt // tm),
        in_specs=[pl.BlockSpec((1, tm, MLA_PAD), lambda b, i: (b, i, P_MLA // MLA_PAD)),
                  pl.BlockSpec((tm, 128), lambda b, i: (i, 0)),
                  pl.BlockSpec((tm, 128), lambda b, i: (i, 0)),
                  pl.BlockSpec((128, 128), c2),
                  pl.BlockSpec((1, Q_LORA), c2),
                  pl.BlockSpec((1, KV_LORA), c2),
                  pl.BlockSpec((Q_LORA, N_HEADS * NOPE_DIM), c2),
                  pl.BlockSpec((Q_LORA, N_HEADS * ROPE_DIM), c2),
                  pl.BlockSpec((N_HEADS, QK_W, N_HEADS * NOPE_DIM), c3),
                  pl.BlockSpec((N_HEADS, QK_W, N_HEADS * ROPE_DIM), c3),
                  pl.BlockSpec((KV_LORA, QK_W), c2),
                  pl.BlockSpec((128, QK_W), c2),
                  pl.BlockSpec((VT_ROWS, KV_LORA), c2),
                  pl.BlockSpec((VT_ROWS, 1), c2)],
        out_specs=[pl.BlockSpec((1, N_HEADS, QK_W, tm), lambda b, i: (b, 0, 0, i)),
                   pl.BlockSpec((1, tm, QK_W), lambda b, i: (b, i, 0)),
                   pl.BlockSpec((1, VT_ROWS, tm), lambda b, i: (b, 0, i))],
        out_shape=[jax.ShapeDtypeStruct((nb, N_HEADS, QK_W, nt), BF16),
                   jax.ShapeDtypeStruct((nb, nt, QK_W), BF16),
                   jax.ShapeDtypeStruct((nb, VT_ROWS, nt), BF16)],
        compiler_params=_params("arbitrary", "arbitrary"),
        name="mla_prep",
    )(p, cos, sin, perm, qg, kg, wqn, wqr, wa, selq, selc, selr, selv, one_col)
    tk = _pick(n_lat, (512, 256))

    def attend(tq, first_col, n_q, latent):
        if first_col % tq == 0:
            q_spec = pl.BlockSpec((1, N_HEADS, QK_W, tq), lambda b, i: (b, 0, 0, i + first_col // tq))
        else:
            q_spec = pl.BlockSpec((pl.Element(1), pl.Element(N_HEADS), pl.Element(QK_W), pl.Element(tq)),
                                  lambda b, i: (b, 0, 0, pl.multiple_of(first_col + i * tq, 128)))
        return pl.pallas_call(
            functools.partial(_mla_attn_kernel, tk=tk, n_ctx=n_ctx, nt=nt, latent=latent),
            grid=(nb, n_q // tq),
            in_specs=[q_spec,
                      pl.BlockSpec((1, nt, QK_W), lambda b, i: (b, 0, 0)),
                      pl.BlockSpec((1, VT_ROWS, nt), lambda b, i: (b, 0, 0)),
                      pl.BlockSpec((N_HEADS, KV_LORA, MIX_W), c3)],
            out_specs=pl.BlockSpec((1, tq, MIX_W), lambda b, i: (b, i, 0)),
            out_shape=jax.ShapeDtypeStruct((nb, n_q, MIX_W), BF16),
            scratch_shapes=[pltpu.VMEM((N_HEADS, 1, tq), F32), pltpu.VMEM((N_HEADS, VT_ROWS, tq), F32),
                            pltpu.VMEM((2, N_HEADS, max(tk, n_ctx) if latent else n_ctx, tq), F32)],
            compiler_params=_params("arbitrary", "arbitrary"),
            name="mla_attn" if latent else "mla_attn_ctx",
        )(qt, kv, vt, wuv)

    y_lat = attend(_pick(n_lat, (512, 256)), n_ctx, n_lat, True)
    y_ctx = attend(_pick(n_ctx, (256, 128)), 0, n_ctx, False) if ctx_out else None
    return y_lat, y_ctx


def _merge_kernel(x_ref, ctx_ref, yr_ref, ys_ref, of_ref, ob_ref, ym_ref, z_ref, g0_ref, g1_ref, g2_ref, g3_ref,
                  wb_ref, wo_ref, ng_ref, gp_ref, ml_ref, mc_ref, ones_ref, o_ref, *, tm, n_ctx, row0, split):
    i = pl.program_id(1)
    if split:
        first = jnp.concatenate([ctx_ref[0], x_ref[0, 0:tm - n_ctx, :]], axis=0)
        x_res = jnp.where(i == 0, first, x_ref[0])
    else:
        x_res = x_ref[0]
    od = of_ref[0] + ob_ref[0]
    ms = _head_sum(od * od, ones_ref[...]) * (1.0 / HEAD_DIM)
    z = z_ref[0].astype(F32)
    ydn = (od * lax.rsqrt(ms + EPS) * ng_ref[...]) * (z * jax.nn.sigmoid(z))
    ys = (yr_ref[0], ys_ref[0], ydn.astype(BF16), ym_ref[0])
    gates = (g0_ref, g1_ref, g2_ref, g3_ref)
    acc = None
    for b in range(N_BRANCH):
        term = gates[b][0].astype(F32) * _dot(ys[b], wb_ref[b])
        acc = term if acc is None else acc + term
    y = _dot(acc.astype(BF16), wo_ref[...])
    r = y * lax.rsqrt(jnp.mean(y * y, axis=-1, keepdims=True) + EPS) * gp_ref[...]
    rows = lax.broadcasted_iota(jnp.int32, (tm, 1), 0) + (row0 + i * tm)
    gate = jnp.where(rows < n_ctx, mc_ref[0, 2:3, :], ml_ref[0, 2:3, :])
    o_ref[0] = x_res + gate * r


def _merge(tokens, y_ret, y_sg, o_f, o_b, y_mla, p, wb, wo, ng, gp, mod, ones_bd, n_ctx, row0):
    nb, nt, _ = p.shape
    d = D_MODEL
    n_rows = nt - row0
    tm = _pick(n_rows, (768, 512, 384, 256, 128))
    c2 = lambda b, i: (0, 0)
    if row0 == 0:
        def window(width, col):
            return pl.BlockSpec((1, tm, width), lambda b, i: (b, i, col // width))

        (xa, ctx), (x_spec, ctx_spec), split = _token_sources(tokens, n_ctx, tm)
    else:
        def window(width, col):
            return pl.BlockSpec((pl.Element(1), pl.Element(tm), pl.Element(width)),
                                lambda b, i: (b, pl.multiple_of(row0 + i * tm, 128), col))

        xa, x_spec, split = tokens, window(d, 0), False
        ctx, ctx_spec = jnp.zeros((nb, 8, d), F32), pl.BlockSpec((1, 8, d), lambda b, i: (b, 0, 0))
    y_spec = window(MIX_W, 0)
    assert y_mla.shape[1] == n_rows
    mla_spec = pl.BlockSpec((1, tm, MIX_W), lambda b, i: (b, i, 0))
    gate_specs = [window(d, P_GATE + k * d) for k in range(N_BRANCH)]
    return pl.pallas_call(
        functools.partial(_merge_kernel, tm=tm, n_ctx=n_ctx, row0=row0, split=split),
        grid=(nb, n_rows // tm),
        in_specs=[x_spec, ctx_spec, y_spec, y_spec, y_spec, y_spec, mla_spec,
                  window(MIX_W, P_DN + 3 * MIX_W),
                  *gate_specs,
                  pl.BlockSpec((N_BRANCH, MIX_W, d), lambda b, i: (0, 0, 0)),
                  pl.BlockSpec((d, d), c2),
                  pl.BlockSpec((1, MIX_W), c2),
                  pl.BlockSpec((1, d), c2),
                  pl.BlockSpec((1, 6, d), lambda b, i: (b, 0, 0)),
                  pl.BlockSpec((1, 6, d), lambda b, i: (nb, 0, 0)),
                  pl.BlockSpec((MIX_W, MIX_W), c2)],
        out_specs=pl.BlockSpec((1, tm, d), lambda b, i: (b, i, 0)),
        out_shape=jax.ShapeDtypeStruct((nb, n_rows, d), F32),
        compiler_params=_params("arbitrary", "arbitrary"),
        name="merge",
    )(xa, ctx, y_ret, y_sg, o_f, o_b, y_mla, p, p, p, p, p, wb, wo, ng, gp, mod, mod, ones_bd)


def _route(sel, aff):
    rows = [sel[e:e + 1, :] for e in range(N_EXPERTS)]
    pairs = [(a, b) for a in range(EXPERTS_PER_GROUP) for b in range(a + 1, EXPERTS_PER_GROUP)]
    grp_score, grp_pair = [], []
    for g in range(N_GROUPS):
        base = g * EXPERTS_PER_GROUP
        best = rows[base + pairs[0][0]] + rows[base + pairs[0][1]]
        best_p = jnp.zeros_like(best, dtype=jnp.int32)
        for pi in range(1, len(pairs)):
            s = rows[base + pairs[pi][0]] + rows[base + pairs[pi][1]]
            take = s > best
            best = jnp.where(take, s, best)
            best_p = jnp.where(take, pi, best_p)
        grp_score.append(best)
        grp_pair.append(best_p)
    top = grp_score[0]
    top_g = jnp.zeros_like(grp_pair[0])
    top_p = grp_pair[0]
    for g in range(1, N_GROUPS):
        take = grp_score[g] > top
        top = jnp.where(take, grp_score[g], top)
        top_g = jnp.where(take, g, top_g)
        top_p = jnp.where(take, grp_pair[g], top_p)
    picked = []
    for e in range(N_EXPERTS):
        g, k = divmod(e, EXPERTS_PER_GROUP)
        in_pair = None
        for pi, (a, b) in enumerate(pairs):
            if k in (a, b):
                hit = top_p == pi
                in_pair = hit if in_pair is None else (in_pair | hit)
        picked.append(jnp.where((top_g == g) & in_pair, aff[e:e + 1, :], 0.0))
    denom = picked[0]
    for e in range(1, N_EXPERTS):
        denom = denom + picked[e]
    return [pk / denom for pk in picked]


def _swiglu(hn, w1, w3, w2, scale):
    a = _dot(hn, w1.astype(BF16))
    h = (a * jax.nn.sigmoid(a)) * _dot(hn, w3.astype(BF16))
    if scale is not None:
        h = h * scale
    return _dot(h.astype(BF16), w2.astype(BF16))


def _moe_kernel(x_ref, ml_ref, mc_ref, g2_ref, gp_ref, rw_ref, rb_ref, ws1_ref, ws3_ref, ws2_ref,
                w1_ref, w3_ref, w2_ref, o_ref, hn_ref, comb_t_ref, comb_ref, acc_ref, *, tm, rb, n_ctx):
    i = pl.program_id(1)
    e = pl.program_id(2)

    @pl.when(e == 0)
    def _():
        def blk(r, carry):
            r0 = pl.multiple_of(r * rb, rb)
            x = x_ref[0, pl.ds(r0, rb), :]
            hn = _norm_modulate(x, g2_ref[...], i * tm + r0 < n_ctx, mc_ref, ml_ref, 3, 4)
            hn_ref[pl.ds(r0, rb), :] = hn.astype(BF16)
            return carry

        lax.fori_loop(0, tm // rb, blk, 0)
        hn = hn_ref[...]
        aff = jax.nn.sigmoid(_dot_nt(rw_ref[...], hn))
        comb = _route(aff + rb_ref[...], aff)
        comb_t_ref[...] = jnp.zeros_like(comb_t_ref)
        for k in range(N_EXPERTS):
            comb_t_ref[k:k + 1, :] = comb[k]
        comb_ref[...] = comb_t_ref[...].T
        acc_ref[...] = _swiglu(hn, ws1_ref[0], ws3_ref[0], ws2_ref[0], None)

    @pl.when(e > 0)
    def _():
        lane = lax.broadcasted_iota(jnp.int32, (1, 128), 1)
        comb = comb_ref[...]
        hn = hn_ref[...]
        first = 2 * (e - 1)
        y = None
        for k in range(2):
            c_k = jnp.sum(jnp.where(lane == first + k, comb, 0.0), axis=-1, keepdims=True)
            term = _swiglu(hn, w1_ref[0, k], w3_ref[0, k], w2_ref[0, k], c_k)
            y = term if y is None else y + term
        acc_ref[...] += y

    @pl.when(e == pl.num_programs(2) - 1)
    def _():
        y = acc_ref[...]
        r = y * lax.rsqrt(jnp.mean(y * y, axis=-1, keepdims=True) + EPS) * gp_ref[...]
        rows = lax.broadcasted_iota(jnp.int32, (tm, 1), 0) + i * tm
        gate = jnp.where(rows < n_ctx, mc_ref[0, 5:6, :], ml_ref[0, 5:6, :])
        o_ref[0] = x_ref[0] + gate * r


def _moe(xa, mod, g2, gp, rw_t, rbias, ws1, ws3, ws2, w1, w3, w2, layer, n_ctx):
    nb, nt, d = xa.shape
    tm = _pick(nt, (1024, 768, 512, 384, 256, 128))
    n_pairs = w1.shape[1] // 2
    row = lambda b, i, e: (b, i, 0)
    c2 = lambda b, i, e: (0, 0)
    shared_blk = lambda b, i, e: (layer, 0, 0)
    pair_blk = lambda b, i, e: (layer, jnp.maximum(e - 1, 0), 0, 0)
    return pl.pallas_call(
        functools.partial(_moe_kernel, tm=tm, rb=128, n_ctx=n_ctx),
        grid=(nb, nt // tm, n_pairs + 1),
        in_specs=[pl.BlockSpec((1, tm, d), row),
                  pl.BlockSpec((1, 6, d), lambda b, i, e: (b, 0, 0)),
                  pl.BlockSpec((1, 6, d), lambda b, i, e: (nb, 0, 0)),
                  pl.BlockSpec((1, d), c2),
                  pl.BlockSpec((1, d), c2),
                  pl.BlockSpec((N_EXPERTS, d), c2),
                  pl.BlockSpec((N_EXPERTS, 1), c2),
                  pl.BlockSpec((1, d, D_EXPERT), shared_blk),
                  pl.BlockSpec((1, d, D_EXPERT), shared_blk),
                  pl.BlockSpec((1, D_EXPERT, d), shared_blk),
                  pl.BlockSpec((1, 2, d, D_EXPERT), pair_blk),
                  pl.BlockSpec((1, 2, d, D_EXPERT), pair_blk),
                  pl.BlockSpec((1, 2, D_EXPERT, d), pair_blk)],
        out_specs=pl.BlockSpec((1, tm, d), row),
        out_shape=jax.ShapeDtypeStruct((nb, nt, d), F32),
        scratch_shapes=[pltpu.VMEM((tm, d), BF16), pltpu.VMEM((128, tm), F32), pltpu.VMEM((tm, 128), F32),
                        pltpu.VMEM((tm, d), F32)],
        compiler_params=_params("arbitrary", "arbitrary", "arbitrary"),
        name="moe",
    )(xa, mod, mod, g2, gp, rw_t, rbias, ws1, ws3, ws2, w1, w3, w2)


def _swap_perm(width, group):
    j = np.arange(width)
    src = np.where((j % group) < group // 2, j + group // 2, j - group // 2)
    return jnp.asarray(np.arange(width)[:, None] == src[None, :], BF16)


def _rope_tables(n_lat, n_ctx):
    def angles(pos, dim):
        half = dim // 2
        inv = ROPE_BASE ** (-jnp.arange(half, dtype=F32) / half)
        return pos.astype(F32)[:, None] * inv[None, :]

    def tables(cos_parts, sin_parts, reps):
        cos = jnp.tile(jnp.concatenate(cos_parts, axis=-1), (1, reps))
        sin = jnp.tile(jnp.concatenate(sin_parts, axis=-1), (1, reps))
        w = cos.shape[1]
        return (jnp.concatenate([jnp.ones((n_ctx, w), F32), cos], axis=0),
                jnp.concatenate([jnp.zeros((n_ctx, w), F32), sin], axis=0))

    rows = n_lat // GRID_W
    ang_t = angles(jnp.arange(n_lat), HEAD_DIM)
    ang_r = angles(jnp.repeat(jnp.arange(rows), GRID_W), ROPE_DIM // 2)
    ang_c = angles(jnp.tile(jnp.arange(GRID_W), rows), ROPE_DIM // 2)
    ct, st = jnp.cos(ang_t), jnp.sin(ang_t)
    ret = tables([ct, ct], [-st, st], N_HEADS)
    cr, sr, cc, sc = jnp.cos(ang_r), jnp.sin(ang_r), jnp.cos(ang_c), jnp.sin(ang_c)
    mla = tables([cr, cr, cc, cc], [-sr, sr, -sc, sc], N_HEADS)
    return ret, mla


def _ret_tables(logit):
    log_g = jax.nn.log_sigmoid(logit.astype(F32))
    lane_lg = jnp.repeat(log_g, HEAD_DIM, axis=1)
    idx = jnp.arange(CHUNK, dtype=F32)[:, None]
    kd = jnp.stack([jnp.exp(lane_lg[0][None, :] * (CHUNK - 1 - idx)), jnp.exp(lane_lg[1][None, :] * idx)])
    qd = jnp.stack([jnp.exp(lane_lg[0][None, :] * (idx + 1)), jnp.exp(lane_lg[1][None, :] * (CHUNK - idx))])
    cd = jnp.exp(lane_lg * CHUNK)[:, None, :]
    diff = idx - idx.T
    blocks = []
    for h in range(N_HEADS):
        f = jnp.exp(log_g[0, h] * jnp.where(diff >= 0, diff, 0.0))
        b = jnp.exp(log_g[1, h] * jnp.where(diff < 0, -diff, 0.0))
        blocks.append(jnp.where(diff >= 0, f, b))
    dm = jnp.concatenate(blocks, axis=1)
    return kd, cd, qd, dm


def _pack_w_in(w_in):
    d = w_in.shape[0]
    mla = jnp.concatenate([w_in[:, OFF_MLA:OFF_MLA + MLA_COLS], jnp.zeros((d, MLA_PAD - MLA_COLS), w_in.dtype)], 1)
    w = jnp.concatenate([w_in[:, OFF_RET:OFF_RET + RET_COLS], w_in[:, OFF_DN:OFF_DN + 4 * MIX_W],
                         w_in[:, OFF_SG:OFF_SG + SG_COLS], mla, w_in[:, OFF_GATE:OFF_GATE + GATE_COLS]], axis=1)
    wab = w_in[:, OFF_DN + 4 * MIX_W:OFF_DN + DN_COLS]
    wabc = jnp.concatenate([wab, jnp.zeros((d, 128 - 4 * N_HEADS), w_in.dtype)], axis=1)
    return w.astype(BF16), wab.T.astype(BF16), wabc.astype(BF16)


def _mla_weights(w_uq, w_ukv):
    dq = NOPE_DIM + ROPE_DIM
    dkv = NOPE_DIM + V_DIM
    wq = w_uq.reshape(Q_LORA, N_HEADS, dq)
    wqn = wq[:, :, :NOPE_DIM].reshape(Q_LORA, N_HEADS * NOPE_DIM)
    wqr = wq[:, :, NOPE_DIM:].reshape(Q_LORA, N_HEADS * ROPE_DIM)
    wkv = w_ukv.reshape(KV_LORA, N_HEADS, dkv)
    head_eye = jnp.eye(N_HEADS, dtype=F32)
    wa = jnp.einsum("chd,hg->hcgd", wkv[:, :, :NOPE_DIM], head_eye).reshape(N_HEADS, KV_LORA, N_HEADS * NOPE_DIM)
    wa = jnp.pad(wa, ((0, 0), (0, QK_W - KV_LORA), (0, 0)))
    wuv = jnp.einsum("chd,hg->hcgd", wkv[:, :, NOPE_DIM:], head_eye).reshape(N_HEADS, KV_LORA, MIX_W)
    selq = np.zeros((N_HEADS, QK_W, N_HEADS * ROPE_DIM), np.float32)
    for h in range(N_HEADS):
        selq[h, KV_LORA:KV_LORA + ROPE_DIM, h * ROPE_DIM:(h + 1) * ROPE_DIM] = np.eye(ROPE_DIM)
    selc = np.zeros((KV_LORA, QK_W), np.float32)
    selc[:, 0:KV_LORA] = np.eye(KV_LORA)
    selr = np.zeros((128, QK_W), np.float32)
    selr[0:ROPE_DIM, KV_LORA:KV_LORA + ROPE_DIM] = np.eye(ROPE_DIM)
    selv = np.zeros((VT_ROWS, KV_LORA), np.float32)
    selv[0:KV_LORA, :] = np.eye(KV_LORA)
    one_col = np.zeros((VT_ROWS, 1), np.float32)
    one_col[KV_LORA, 0] = 1.0
    return (tuple(jnp.asarray(a, BF16) for a in (wqn, wqr, wa, selq, selc, selr, selv))
            + (jnp.asarray(one_col), wuv.astype(BF16)))


def kernel(x, c, ctx, c_ctx, w_ada, b_ada, g_pre1, g_post1, g_pre2, g_post2, w_in, ret_decay_logit, sg_norm_g, sg_w, sg_b, dn_conv_w, dn_A_log, dn_dt_bias, dn_norm_g, mla_q_norm_g, mla_kv_norm_g, mla_w_uq, mla_w_ukv, w_branch, w_out, router_w, router_bias, moe_w1, moe_w3, moe_w2, shared_w1, shared_w3, shared_w2):
    nb, n_lat, d = x.shape
    n_ctx = ctx.shape[1]
    depth = w_in.shape[0]
    assert d == D_MODEL and n_lat % GRID_W == 0 and n_lat % CHUNK == 0 and n_ctx % 256 == 0
    ncc = n_ctx // CHUNK

    n_cond = -(-(nb + 1) // 8) * 8
    cond = jnp.concatenate([c, c_ctx[None], jnp.zeros((n_cond - nb - 1, d), F32)], axis=0)
    mod_all = _adaln(cond, w_ada, b_ada).reshape(depth, n_cond, 6, d)

    (ret_cos, ret_sin), (mla_cos, mla_sin) = _rope_tables(n_lat, n_ctx)
    perm_ret = _swap_perm(MIX_W, HEAD_DIM)
    perm_mla = _swap_perm(N_HEADS * ROPE_DIM, ROPE_DIM // 2)
    lane_head = jnp.arange(MIX_W) // HEAD_DIM
    bd = (lane_head[:, None] == lane_head[None, :]).astype(F32)
    ones_bd = bd.astype(BF16)
    rw_t = router_w.T.astype(BF16)
    rbias = router_bias.astype(F32)[:, None]

    xa = (ctx, x)
    for l in range(depth):
        last = l == depth - 1
        if last and isinstance(xa, tuple):
            xa = jnp.concatenate(xa, axis=1)
        mod = mod_all[l]
        w_l, wab_l, wabc_l = _pack_w_in(w_in[l])
        p, ab_t, ab_c = _inproj(xa, n_ctx + n_lat, mod, g_pre1[l][None], w_l, wab_l, wabc_l, n_ctx)

        kd, cd, qd, dm = _ret_tables(ret_decay_logit[l])
        wcat = jnp.concatenate([sg_w[l, h] for h in range(N_HEADS)], axis=1).astype(BF16)
        sg_bias = jnp.repeat(sg_b[l].T, HEAD_DIM, axis=1)
        y_ret, y_sg = _retention_and_sgate(p, ret_cos, ret_sin, perm_ret, (kd, cd, qd, dm, bd, ones_bd),
                                           sg_norm_g[l][None], wcat, sg_bias, ncc)

        neg_a = (-jnp.exp(dn_A_log[l].astype(F32))).reshape(2 * N_HEADS, 1)
        dtb = dn_dt_bias[l].astype(F32).reshape(2 * N_HEADS, 1)
        conv_w = jnp.concatenate([dn_conv_w[l], jnp.zeros((8 - CONV_W, 3 * MIX_W), F32)], axis=0)
        o_f, o_b = _deltanet(p, ab_t, ab_c, conv_w, neg_a, dtb, bd, ones_bd, ncc)

        row0 = n_ctx if last else 0
        y_mla, y_mla_ctx = _mla(p, mla_cos, mla_sin, perm_mla, mla_q_norm_g[l][None], mla_kv_norm_g[l][None],
                                *_mla_weights(mla_w_uq[l], mla_w_ukv[l]), n_ctx, not last)
        if not last:
            y_mla = jnp.concatenate([y_mla_ctx, y_mla], axis=1)

        xa = _merge(xa, y_ret, y_sg, o_f, o_b, y_mla, p, w_branch[l].astype(BF16), w_out[l].astype(BF16),
                    jnp.tile(dn_norm_g[l], N_HEADS)[None], g_post1[l][None], mod, ones_bd, n_ctx, row0)

        xa = _moe(xa, mod, g_pre2[l][None], g_post2[l][None], rw_t, rbias, shared_w1, shared_w3, shared_w2,
                  moe_w1, moe_w3, moe_w2, l, n_ctx - row0)
    return xa
```

```python
import functools
import math

import jax
import jax.numpy as jnp
import numpy as np
from jax import lax
from jax.experimental import pallas as pl
from jax.experimental.pallas import tpu as pltpu

F32 = jnp.float32
BF16 = jnp.bfloat16
HIGHEST = lax.Precision.HIGHEST

D_MODEL = 1024
GRID_W = 64
N_HEADS = 4
HEAD_DIM = 64
MIX_W = N_HEADS * HEAD_DIM
CHUNK = 128
ROPE_BASE = 10000.0
EPS = 1e-6
RET_DECAY_EXP0 = 5.0
CONV_W = 5
Q_LORA = 256
KV_LORA = 128
NOPE_DIM = 64
ROPE_DIM = 32
V_DIM = 64
N_EXPERTS = 16
N_GROUPS = 4
EXPERTS_PER_GROUP = N_EXPERTS // N_GROUPS
D_EXPERT = 256
N_BRANCH = 4

RET_COLS = 4 * MIX_W
SG_COLS = 2 * MIX_W
DN_COLS = 4 * MIX_W + 4 * N_HEADS
MLA_COLS = Q_LORA + KV_LORA + ROPE_DIM
GATE_COLS = N_BRANCH * D_MODEL
OFF_RET = 0
OFF_SG = OFF_RET + RET_COLS
OFF_DN = OFF_SG + SG_COLS
OFF_MLA = OFF_DN + DN_COLS
OFF_GATE = OFF_MLA + MLA_COLS

P_RET = 0
P_DN = 1024
P_SG = 2048
P_MLA = 2560
P_GATE = 3072
P_COLS = 7168
MLA_PAD = 512

VMEM_LIMIT = 56 * 1024 * 1024


def _dot(a, b, precision=None):
    return jnp.dot(a, b, preferred_element_type=F32, precision=precision)


def _dot_nt(a, b, precision=None):
    return lax.dot_general(a, b, (((1,), (1,)), ((), ())), preferred_element_type=F32, precision=precision)


def _dot_tn(a, b):
    return lax.dot_general(a, b, (((0,), (0,)), ((), ())), preferred_element_type=F32)


def _mm(a, b):
    return _dot(a.astype(BF16), b.astype(BF16))


def _params(*sem):
    return pltpu.CompilerParams(dimension_semantics=sem, vmem_limit_bytes=VMEM_LIMIT)


def _pick(n, cands):
    for c in cands:
        if n % c == 0:
            return c
    raise ValueError(f"no tile for {n}")


def _head_of_lane(width, group):
    return lax.broadcasted_iota(jnp.int32, (1, width), 1) // group


def _stack_heads(x):
    head = _head_of_lane(MIX_W, HEAD_DIM)
    xf = x.astype(F32)
    return jnp.concatenate([jnp.where(head == h, xf, 0.0).astype(BF16) for h in range(N_HEADS)], axis=0)


def _expand_heads(cols):
    head = _head_of_lane(MIX_W, HEAD_DIM)
    out = cols[:, N_HEADS - 1:N_HEADS]
    for h in range(N_HEADS - 2, -1, -1):
        out = jnp.where(head <= h, cols[:, h:h + 1], out)
    return out


def _head_sum(x, ones_bd):
    hi = x.astype(BF16)
    lo = (x - hi.astype(F32)).astype(BF16)
    return _dot(hi, ones_bd) + _dot(lo, ones_bd)


def _rot(x_bf, cos, sin, perm):
    return x_bf.astype(F32) * cos + _dot(x_bf, perm) * sin


def _norm_modulate(x, g, is_ctx, mc_ref, ml_ref, shift_row, scale_row):
    shift = jnp.where(is_ctx, mc_ref[0, shift_row:shift_row + 1, :], ml_ref[0, shift_row:shift_row + 1, :])
    scale = jnp.where(is_ctx, mc_ref[0, scale_row:scale_row + 1, :], ml_ref[0, scale_row:scale_row + 1, :])
    gain = g * (1.0 + scale)
    return x * lax.rsqrt(jnp.mean(x * x, axis=-1, keepdims=True) + EPS) * gain + shift


def _adaln_kernel(c_ref, w_ref, b_ref, o_ref):
    c = c_ref[...]
    s = c * jax.nn.sigmoid(c)
    o_ref[0] = _dot(s, w_ref[0], precision=HIGHEST) + b_ref[0]


def _adaln(cond, w_ada, b_ada):
    n_l, d, d6 = w_ada.shape
    r = cond.shape[0]
    tn = 1024
    return pl.pallas_call(
        _adaln_kernel,
        grid=(n_l, d6 // tn),
        in_specs=[pl.BlockSpec((r, d), lambda l, j: (0, 0)),
                  pl.BlockSpec((1, d, tn), lambda l, j: (l, 0, j)),
                  pl.BlockSpec((1, 1, tn), lambda l, j: (l, 0, j))],
        out_specs=pl.BlockSpec((1, r, tn), lambda l, j: (l, 0, j)),
        out_shape=jax.ShapeDtypeStruct((n_l, r, d6), F32),
        compiler_params=_params("arbitrary", "arbitrary"),
        name="adaln",
    )(cond, w_ada, b_ada.reshape(n_l, 1, d6))


def _inproj_kernel(x_ref, ctx_ref, ml_ref, mc_ref, g_ref, w_ref, wab_ref, wabc_ref, p_ref, ab_ref, abc_ref, xn_ref,
                   *, tm, rb, n_ctx, split):
    i = pl.program_id(1)
    j = pl.program_id(2)

    @pl.when(j == 0)
    def _():
        def blk(r, carry):
            r0 = pl.multiple_of(r * rb, rb)
            is_ctx = i * tm + r0 < n_ctx
            if split:
                x_off = pl.multiple_of(jnp.maximum(jnp.where(i == 0, r0 - n_ctx, r0), 0), rb)
                c_off = pl.multiple_of(jnp.minimum(r0, n_ctx - rb), rb)
                x = jnp.where(is_ctx, ctx_ref[0, pl.ds(c_off, rb), :], x_ref[0, pl.ds(x_off, rb), :])
            else:
                x = x_ref[0, pl.ds(r0, rb), :]
            hn = _norm_modulate(x, g_ref[...], is_ctx, mc_ref, ml_ref, 0, 1)
            xn_ref[pl.ds(r0, rb), :] = hn.astype(BF16)
            return carry

        lax.fori_loop(0, tm // rb, blk, 0)
        ab_ref[0] = _dot_nt(wab_ref[...], xn_ref[...])
        abc_ref[0] = _dot(xn_ref[...], wabc_ref[...])

    p_ref[0] = _dot(xn_ref[...], w_ref[...]).astype(BF16)


def _token_sources(tokens, n_ctx, tm):
    if isinstance(tokens, tuple):
        ctx, x = tokens
        d = x.shape[-1]
        if x.shape[1] >= tm and tm > n_ctx:
            x_spec = pl.BlockSpec(
                (pl.Element(1), pl.Element(tm), pl.Element(d)),
                lambda b, i, *_: (b, pl.multiple_of(jnp.maximum(i * tm - n_ctx, 0), 128), 0))
            ctx_spec = pl.BlockSpec((1, n_ctx, d), lambda b, i, *_: (b, 0, 0))
            return (x, ctx), (x_spec, ctx_spec), True
        tokens = jnp.concatenate([ctx, x], axis=1)
    nb, _, d = tokens.shape
    dummy = jnp.zeros((nb, 8, d), tokens.dtype)
    return ((tokens, dummy), (pl.BlockSpec((1, tm, d), lambda b, i, *_: (b, i, 0)),
                              pl.BlockSpec((1, 8, d), lambda b, i, *_: (b, 0, 0))), False)


def _inproj(tokens, nt, mod, g, w, wab, wabc, n_ctx):
    tm = _pick(nt, (1408, 768, 384, 256, 128))
    tn = 1792
    (xa, ctx), (x_spec, ctx_spec), split = _token_sources(tokens, n_ctx, tm)
    nb, _, d = xa.shape
    kern = functools.partial(_inproj_kernel, tm=tm, rb=128, n_ctx=n_ctx, split=split)
    return pl.pallas_call(
        kern,
        grid=(nb, nt // tm, P_COLS // tn),
        in_specs=[x_spec, ctx_spec,
                  pl.BlockSpec((1, 6, d), lambda b, i, j: (b, 0, 0)),
                  pl.BlockSpec((1, 6, d), lambda b, i, j: (nb, 0, 0)),
                  pl.BlockSpec((1, d), lambda b, i, j: (0, 0)),
                  pl.BlockSpec((d, tn), lambda b, i, j: (0, j)),
                  pl.BlockSpec((16, d), lambda b, i, j: (0, 0)),
                  pl.BlockSpec((d, 128), lambda b, i, j: (0, 0))],
        out_specs=[pl.BlockSpec((1, tm, tn), lambda b, i, j: (b, i, j)),
                   pl.BlockSpec((1, 16, tm), lambda b, i, j: (b, 0, i)),
                   pl.BlockSpec((1, tm, 128), lambda b, i, j: (b, i, 0))],
        out_shape=[jax.ShapeDtypeStruct((nb, nt, P_COLS), BF16),
                   jax.ShapeDtypeStruct((nb, 16, nt), F32),
                   jax.ShapeDtypeStruct((nb, nt, 128), F32)],
        scratch_shapes=[pltpu.VMEM((tm, d), BF16)],
        compiler_params=_params("arbitrary", "arbitrary", "arbitrary"),
        name="inproj",
    )(xa, ctx, mod, mod, g, w, wab, wabc)


def _bwd_chunk(t, ncc, nc):
    return jnp.where(t < ncc, ncc - 1 - t, nc - 1 - (t - ncc))


def _ret_state_kernel(pf_ref, pb_ref, cf_ref, sf_ref, cb_ref, sb_ref, perm_ref, kd_ref, cd_ref, bd_ref,
                      of_ref, ob_ref, st_f, st_b, *, cb):
    t = pl.program_id(1)

    @pl.when(t == 0)
    def _():
        st_f[...] = jnp.zeros_like(st_f)
        st_b[...] = jnp.zeros_like(st_b)

    def increments(p_ref, c_ref, s_ref, d):
        out = []
        for i in range(cb):
            r = slice(i * CHUNK, (i + 1) * CHUNK)
            kr = _rot(p_ref[0, r, 0:MIX_W], c_ref[r, :], s_ref[r, :], perm_ref[...]) * (HEAD_DIM ** -0.5)
            out.append(bd_ref[...] * _dot_tn((kr * kd_ref[d]).astype(BF16), p_ref[0, r, MIX_W:2 * MIX_W]))
        return out

    inc_f = increments(pf_ref, cf_ref, sf_ref, 0)
    inc_b = increments(pb_ref, cb_ref, sb_ref, 1)
    s = st_f[...]
    for i in range(cb):
        of_ref[0, i] = s.astype(BF16)
        s = cd_ref[0] * s + inc_f[i]
    st_f[...] = s
    s = st_b[...]
    for i in reversed(range(cb)):
        ob_ref[0, i] = s.astype(BF16)
        s = cd_ref[1] * s + inc_b[i]
    st_b[...] = s


def _gelu_tanh(x):
    return 0.5 * x * (1.0 + jnp.tanh(math.sqrt(2.0 / math.pi) * (x + 0.044715 * (x * x * x))))


def _mix_out_kernel(p_ref, pg_ref, c_ref, s_ref, sf_ref, sb_ref, perm_ref, dm_ref, qd_ref, ones_ref,
                    ng_ref, wg_ref, bg_ref, y_ref, ysg_ref, *, cb):
    chunks = range(cb)
    rows = [slice(i * CHUNK, (i + 1) * CHUNK) for i in chunks]
    perm, dm, ones_bd = perm_ref[...], dm_ref[...], ones_ref[...]
    p = [p_ref[0, r, :] for r in rows]
    cos = [c_ref[r, :] for r in rows]
    sin = [s_ref[r, :] for r in rows]
    qr = [_rot(p[i][:, 0:MIX_W], cos[i], sin[i], perm) for i in chunks]
    kr = [_rot(p[i][:, MIX_W:2 * MIX_W], cos[i], sin[i], perm) * (HEAD_DIM ** -0.5) for i in chunks]
    z = [_gelu_tanh(pg_ref[0, r, :].astype(F32)) for r in rows]
    vg = [x[:, MIX_W:] for x in z]
    mu_g = [jnp.mean(x, axis=-1, keepdims=True) for x in vg]
    vgc = [x - m for x, m in zip(vg, mu_g)]
    var_g = [jnp.mean(x * x, axis=-1, keepdims=True) for x in vgc]
    vn = [x * lax.rsqrt(s + EPS) * ng_ref[...] for x, s in zip(vgc, var_g)]
    sc = [_dot_nt(qr[i].astype(BF16), _stack_heads(kr[i])) * dm for i in chunks]
    mixed = [_dot(wg_ref[...], _stack_heads(x)) for x in vn]
    o = [_dot(sc[i].astype(BF16), _stack_heads(p[i][:, 2 * MIX_W:3 * MIX_W])) for i in chunks]
    qs = [jnp.concatenate([(qr[i] * qd_ref[0]).astype(BF16), (qr[i] * qd_ref[1]).astype(BF16)], axis=1)
          for i in chunks]
    ss = [jnp.concatenate([sf_ref[0, i], sb_ref[0, i]], axis=0) for i in chunks]
    o = [o[i] + _dot(qs[i], ss[i]) for i in chunks]
    for i in chunks:
        ysg_ref[0, rows[i], :] = (z[i][:, :MIX_W] * (mixed[i] + bg_ref[...])).astype(BF16)
    mu = [_head_sum(x, ones_bd) * (1.0 / HEAD_DIM) for x in o]
    oc = [x - m for x, m in zip(o, mu)]
    var = [_head_sum(x * x, ones_bd) * (1.0 / HEAD_DIM) for x in oc]
    for i in chunks:
        g = p[i][:, 3 * MIX_W:4 * MIX_W].astype(F32)
        y_ref[0, rows[i], :] = (oc[i] * lax.rsqrt(var[i] + EPS) * (g * jax.nn.sigmoid(g))).astype(BF16)


def _retention_and_sgate(p, cos, sin, perm, tabs, sg_ng, sg_w, sg_bias, ncc):
    nb, nt, _ = p.shape
    nc = nt // CHUNK
    kd, cd, qd, dm, bd, ones_bd = tabs
    cb = 2
    assert nc % cb == 0 and ncc % cb == 0
    nblk, ncb = nc // cb, ncc // cb
    fwd = lambda b, t: (b, t, 0)
    bwd = lambda b, t: (b, _bwd_chunk(t, ncb, nblk), 0)
    tab_f = lambda b, t: (t, 0)
    tab_b = lambda b, t: (_bwd_chunk(t, ncb, nblk), 0)
    c2 = lambda b, t: (0, 0)
    c3 = lambda b, t: (0, 0, 0)
    st_shape = jax.ShapeDtypeStruct((nb, nc, MIX_W, MIX_W), BF16)

    def kv_window(block_of):
        return pl.BlockSpec((pl.Element(1), pl.Element(cb * CHUNK), pl.Element(2 * MIX_W)),
                            lambda b, t: (b, pl.multiple_of(block_of(t) * (cb * CHUNK), 128), P_RET + MIX_W))

    st_f, st_b = pl.pallas_call(
        functools.partial(_ret_state_kernel, cb=cb),
        grid=(nb, nblk),
        in_specs=[kv_window(lambda t: t), kv_window(lambda t: _bwd_chunk(t, ncb, nblk)),
                  pl.BlockSpec((cb * CHUNK, MIX_W), tab_f), pl.BlockSpec((cb * CHUNK, MIX_W), tab_f),
                  pl.BlockSpec((cb * CHUNK, MIX_W), tab_b), pl.BlockSpec((cb * CHUNK, MIX_W), tab_b),
                  pl.BlockSpec((MIX_W, MIX_W), c2),
                  pl.BlockSpec((2, CHUNK, MIX_W), c3),
                  pl.BlockSpec((2, 1, MIX_W), c3),
                  pl.BlockSpec((MIX_W, MIX_W), c2)],
        out_specs=[pl.BlockSpec((1, cb, MIX_W, MIX_W), lambda b, t: (b, t, 0, 0)),
                   pl.BlockSpec((1, cb, MIX_W, MIX_W), lambda b, t: (b, _bwd_chunk(t, ncb, nblk), 0, 0))],
        out_shape=[st_shape, st_shape],
        scratch_shapes=[pltpu.VMEM((MIX_W, MIX_W), F32), pltpu.VMEM((MIX_W, MIX_W), F32)],
        compiler_params=_params("arbitrary", "arbitrary"),
        name="ret_state",
    )(p, p, cos, sin, cos, sin, perm, kd, cd, bd)
    blk = lambda b, t: (b, t, 0)
    y_shape = jax.ShapeDtypeStruct((nb, nt, MIX_W), BF16)
    cb = _pick(nc, (6, 3, 2, 1))
    return pl.pallas_call(
        functools.partial(_mix_out_kernel, cb=cb),
        grid=(nb, nc // cb),
        in_specs=[pl.BlockSpec((1, cb * CHUNK, RET_COLS), blk),
                  pl.BlockSpec((1, cb * CHUNK, SG_COLS), lambda b, t: (b, t, P_SG // SG_COLS)),
                  pl.BlockSpec((cb * CHUNK, MIX_W), tab_f), pl.BlockSpec((cb * CHUNK, MIX_W), tab_f),
                  pl.BlockSpec((1, cb, MIX_W, MIX_W), lambda b, t: (b, t, 0, 0)),
                  pl.BlockSpec((1, cb, MIX_W, MIX_W), lambda b, t: (b, t, 0, 0)),
                  pl.BlockSpec((MIX_W, MIX_W), c2),
                  pl.BlockSpec((CHUNK, N_HEADS * CHUNK), c2),
                  pl.BlockSpec((2, CHUNK, MIX_W), c3),
                  pl.BlockSpec((MIX_W, MIX_W), c2),
                  pl.BlockSpec((1, MIX_W), c2),
                  pl.BlockSpec((CHUNK, N_HEADS * CHUNK), c2),
                  pl.BlockSpec((CHUNK, MIX_W), c2)],
        out_specs=[pl.BlockSpec((1, cb * CHUNK, MIX_W), blk), pl.BlockSpec((1, cb * CHUNK, MIX_W), blk)],
        out_shape=[y_shape, y_shape],
        compiler_params=_params("arbitrary", "arbitrary"),
        name="mix_out",
    )(p, p, cos, sin, st_f, st_b, perm, dm, qd, ones_bd, sg_ng, sg_w, sg_bias)


def _softplus(a):
    return jnp.maximum(a, 0.0) + jnp.log1p(jnp.exp(-jnp.abs(a)))


def _dn_prep_kernel(pc_ref, pp_ref, pn_ref, ab_ref, abc_ref, cw_ref, na_ref, dtb_ref, nar_ref, dtbr_ref, ones_ref,
                    qkv_ref, gb_ref, gbc_ref, xe_ref, *, rows, ctx_blocks, n_blocks):
    t = pl.program_id(1)
    w3 = 3 * MIX_W
    prev_ok = jnp.where((t != 0) & (t != ctx_blocks), 1.0, 0.0)
    next_ok = jnp.where((t != ctx_blocks - 1) & (t != n_blocks - 1), 1.0, 0.0)
    tail = pp_ref[0, rows - 16:rows, 0:w3].astype(F32)
    head = pn_ref[0, 0:16, 0:w3].astype(F32)
    xe_ref[0:8, :] = tail[8:16, :] * prev_ok
    xe_ref[8:8 + rows, :] = pc_ref[0, :, 0:w3].astype(F32)
    xe_ref[8 + rows:16 + rows, :] = head[0:8, :] * next_ok
    pad = CONV_W // 2
    y = xe_ref[8 - pad:8 - pad + rows, :] * cw_ref[0:1, :]
    for i in range(1, CONV_W):
        y = y + xe_ref[8 - pad + i:8 - pad + i + rows, :] * cw_ref[i:i + 1, :]
    y = y * jax.nn.sigmoid(y)
    q = y[:, 0:MIX_W]
    k = y[:, MIX_W:2 * MIX_W]
    v = y[:, 2 * MIX_W:w3]
    ones_bd = ones_ref[...]
    qn = q * lax.rsqrt(_head_sum(q * q, ones_bd) + EPS) * (HEAD_DIM ** -0.5)
    kn = k * lax.rsqrt(_head_sum(k * k, ones_bd) + EPS)
    qkv_ref[0, :, 0:MIX_W] = qn.astype(BF16)
    qkv_ref[0, :, MIX_W:2 * MIX_W] = kn.astype(BF16)
    qkv_ref[0, :, 2 * MIX_W:w3] = v.astype(BF16)
    ab = ab_ref[0]
    gb_ref[0, 0:8, :] = na_ref[...] * _softplus(ab[0:8, :] + dtb_ref[...])
    gb_ref[0, 8:16, :] = jax.nn.sigmoid(ab[8:16, :])
    abc = abc_ref[0]
    lane = lax.broadcasted_iota(jnp.int32, (1, 128), 1)
    g_c = nar_ref[...] * _softplus(abc + dtbr_ref[...])
    gbc_ref[0] = jnp.where(lane < 8, g_c, jnp.where(lane < 16, jax.nn.sigmoid(abc), 0.0))


def _split3(x):
    hi = x.astype(BF16)
    r = x - hi.astype(F32)
    mid = r.astype(BF16)
    lo = (r - mid.astype(F32)).astype(BF16)
    return hi, mid, lo


def _tri_inverse(mats, ii, jj):
    eye = jnp.where(ii == jj, 1.0, 0.0)
    nd = [jnp.where((ii // 16) == (jj // 16), n, 0.0) for n in mats]
    p1 = [_mm(x, x) for x in nd]
    m = [eye - x for x in nd]
    p2 = [_mm(x, x) for x in p1]
    m = [x + _mm(x, y) for x, y in zip(m, p1)]
    p3 = [_mm(x, x) for x in p2]
    m = [x + _mm(x, y) for x, y in zip(m, p2)]
    m = [x + _mm(x, y) for x, y in zip(m, p3)]
    for lvl in (16, 32, 64):
        off_mask = ((ii // (2 * lvl)) == (jj // (2 * lvl))) & ((ii // lvl) != (jj // lvl))
        t = [_mm(jnp.where(off_mask, n, 0.0), x) for n, x in zip(mats, m)]
        m = [x - _mm(x, y) for x, y in zip(m, t)]
    return m


def _dn_pre(qkv, g, gbc, d, lower):
    c = CHUNK
    qn = qkv[:, 0:MIX_W]
    kn = qkv[:, MIX_W:2 * MIX_W]
    v = qkv[:, 2 * MIX_W:3 * MIX_W]
    ii = lax.broadcasted_iota(jnp.int32, (c, c), 0)
    jj = lax.broadcasted_iota(jnp.int32, (c, c), 1)
    incl = (ii >= jj) if lower else (ii <= jj)
    tri = jnp.where(incl, 1.0, 0.0).astype(BF16)
    g_row = sum(_dot_nt(part, tri) for part in _split3(g))[N_HEADS * d:N_HEADS * (d + 1), :]
    cum = sum(_dot(tri, part) for part in _split3(gbc))
    g_col = cum[:, N_HEADS * d:N_HEADS * (d + 1)]
    b_col = gbc[:, 2 * N_HEADS + N_HEADS * d:2 * N_HEADS + N_HEADS * (d + 1)]
    g_cols4 = jnp.concatenate([jnp.broadcast_to(g_col[:, h:h + 1], (c, c)) for h in range(N_HEADS)], axis=1)
    b_cols4 = jnp.concatenate([jnp.broadcast_to(b_col[:, h:h + 1], (c, c)) for h in range(N_HEADS)], axis=1)
    g_rows4 = jnp.concatenate([g_row[h:h + 1, :] for h in range(N_HEADS)], axis=1)
    incl4 = jnp.concatenate([incl] * N_HEADS, axis=1)
    diag4 = jnp.concatenate([ii == jj] * N_HEADS, axis=1)
    decay = jnp.where(incl4, jnp.exp(jnp.where(incl4, g_cols4 - g_rows4, 0.0)), 0.0)
    kstack = _stack_heads(kn)
    kk = _dot_nt(kn, kstack)
    qk = _dot_nt(qn, kstack)
    n_mat = jnp.where(diag4, 0.0, decay * kk * b_cols4)
    attn = (decay * qk).astype(BF16)
    g256 = _expand_heads(g_col)
    eg256 = jnp.exp(g256)
    b256 = _expand_heads(b_col)
    vb = v.astype(F32) * b256
    kbg = kn.astype(F32) * b256 * eg256
    rhs = jnp.concatenate([_stack_heads(vb), _stack_heads(kbg)], axis=1)
    g_last = g256[c - 1:c, :] if lower else g256[0:1, :]
    kdec = (kn.astype(F32) * jnp.exp(g_last - g256)).astype(BF16)
    n_heads = [n_mat[:, h * c:(h + 1) * c] for h in range(N_HEADS)]
    return n_heads, dict(qn=qn, attn=attn, rhs=rhs, eg=eg256, kdec=kdec, sdec=jnp.exp(g_last))


def _dn_post(z, s_prev, bd):
    s_bf = s_prev.astype(BF16)
    w = z["u"] - _dot(z["wk"], s_bf)
    o = z["eg"] * _dot(z["qn"], s_bf) + _dot(z["attn"], _stack_heads(w))
    s_next = z["sdec"] * s_prev + bd * _dot_tn(z["kdec"], w.astype(BF16))
    return o, s_next


def _dn_scan_kernel(qf_ref, qb_ref, gf_ref, gb_ref, gcf_ref, gcb_ref, bd_ref, of_ref, ob_ref, st_f, st_b, *, cb):
    t = pl.program_id(1)

    @pl.when(t == 0)
    def _():
        st_f[...] = jnp.zeros_like(st_f)
        st_b[...] = jnp.zeros_like(st_b)

    bd = bd_ref[...]
    rows = [slice(i * CHUNK, (i + 1) * CHUNK) for i in range(cb)]
    mats, pres = [], []
    for d, (q_ref, g_ref, gc_ref) in enumerate(((qf_ref, gf_ref, gcf_ref), (qb_ref, gb_ref, gcb_ref))):
        for r in rows:
            n_heads, pre = _dn_pre(q_ref[0, r, :], g_ref[0, :, r], gc_ref[0, r, :], d, d == 0)
            mats += n_heads
            pres.append(pre)
    ii = lax.broadcasted_iota(jnp.int32, (CHUNK, CHUNK), 0)
    jj = lax.broadcasted_iota(jnp.int32, (CHUNK, CHUNK), 1)
    inv = _tri_inverse(mats, ii, jj)
    for n, pre in enumerate(pres):
        a_inv = jnp.concatenate(inv[N_HEADS * n:N_HEADS * (n + 1)], axis=1).astype(BF16)
        uw = _dot(a_inv, pre["rhs"])
        pre["u"] = uw[:, 0:MIX_W]
        pre["wk"] = uw[:, MIX_W:2 * MIX_W].astype(BF16)
    s_f, s_b = st_f[...], st_b[...]
    for k in range(cb):
        o, s_f = _dn_post(pres[k], s_f, bd)
        of_ref[0, rows[k], :] = o
        o, s_b = _dn_post(pres[cb + cb - 1 - k], s_b, bd)
        ob_ref[0, rows[cb - 1 - k], :] = o
    st_f[...] = s_f
    st_b[...] = s_b


def _deltanet(p, ab_t, ab_c, conv_w, neg_a, dtb, bd, ones_bd, ncc):
    nb, nt, _ = p.shape
    nc = nt // CHUNK
    w3 = 3 * MIX_W
    c2 = lambda b, t: (0, 0)
    dn_blk = P_DN // RET_COLS
    pad_lanes = lambda col: jnp.concatenate([col.reshape(1, -1), jnp.zeros((1, 128 - col.size), F32)], axis=1)
    pr = math.gcd(math.gcd(ncc * CHUNK, nt), 256)
    n_pb, ctx_pb = nt // pr, ncc * CHUNK // pr
    qkv, gbeta, gbeta_c = pl.pallas_call(
        functools.partial(_dn_prep_kernel, rows=pr, ctx_blocks=ctx_pb, n_blocks=n_pb),
        grid=(nb, n_pb),
        in_specs=[pl.BlockSpec((1, pr, 4 * MIX_W), lambda b, t: (b, t, dn_blk)),
                  pl.BlockSpec((1, pr, 4 * MIX_W), lambda b, t: (b, jnp.maximum(t - 1, 0), dn_blk)),
                  pl.BlockSpec((1, pr, 4 * MIX_W), lambda b, t: (b, jnp.minimum(t + 1, n_pb - 1), dn_blk)),
                  pl.BlockSpec((1, 16, pr), lambda b, t: (b, 0, t)),
                  pl.BlockSpec((1, pr, 128), lambda b, t: (b, t, 0)),
                  pl.BlockSpec((8, w3), c2),
                  pl.BlockSpec((8, 1), c2),
                  pl.BlockSpec((8, 1), c2),
                  pl.BlockSpec((1, 128), c2),
                  pl.BlockSpec((1, 128), c2),
                  pl.BlockSpec((MIX_W, MIX_W), c2)],
        out_specs=[pl.BlockSpec((1, pr, w3), lambda b, t: (b, t, 0)),
                   pl.BlockSpec((1, 16, pr), lambda b, t: (b, 0, t)),
                   pl.BlockSpec((1, pr, 128), lambda b, t: (b, t, 0))],
        out_shape=[jax.ShapeDtypeStruct((nb, nt, w3), BF16),
                   jax.ShapeDtypeStruct((nb, 16, nt), F32),
                   jax.ShapeDtypeStruct((nb, nt, 128), F32)],
        scratch_shapes=[pltpu.VMEM((pr + 16, w3), F32)],
        compiler_params=_params("arbitrary", "arbitrary"),
        name="dn_prep",
    )(p, p, p, ab_t, ab_c, conv_w, neg_a, dtb, pad_lanes(neg_a), pad_lanes(dtb), ones_bd)
    cb = 2
    assert nc % cb == 0 and ncc % cb == 0
    rows = cb * CHUNK
    cur_b = lambda t: _bwd_chunk(t, ncc // cb, nc // cb)
    o_shape = jax.ShapeDtypeStruct((nb, nt, MIX_W), F32)
    return pl.pallas_call(
        functools.partial(_dn_scan_kernel, cb=cb),
        grid=(nb, nc // cb),
        in_specs=[pl.BlockSpec((1, rows, w3), lambda b, t: (b, t, 0)),
                  pl.BlockSpec((1, rows, w3), lambda b, t: (b, cur_b(t), 0)),
                  pl.BlockSpec((1, 16, rows), lambda b, t: (b, 0, t)),
                  pl.BlockSpec((1, 16, rows), lambda b, t: (b, 0, cur_b(t))),
                  pl.BlockSpec((1, rows, 128), lambda b, t: (b, t, 0)),
                  pl.BlockSpec((1, rows, 128), lambda b, t: (b, cur_b(t), 0)),
                  pl.BlockSpec((MIX_W, MIX_W), c2)],
        out_specs=[pl.BlockSpec((1, rows, MIX_W), lambda b, t: (b, t, 0)),
                   pl.BlockSpec((1, rows, MIX_W), lambda b, t: (b, cur_b(t), 0))],
        out_shape=[o_shape, o_shape],
        scratch_shapes=[pltpu.VMEM((MIX_W, MIX_W), F32), pltpu.VMEM((MIX_W, MIX_W), F32)],
        compiler_params=_params("arbitrary", "arbitrary"),
        name="dn_scan",
    )(qkv, qkv, gbeta, gbeta, gbeta_c, gbeta_c, bd)


QK_W = 256


VT_ROWS = 144


def _mla_prep_kernel(p_ref, c_ref, s_ref, perm_ref, qg_ref, kg_ref, wqn_ref, wqr_ref, wa_ref, selq_ref, selc_ref,
                     selr_ref, selv_ref, one_ref, qt_ref, kv_ref, vt_ref, *, scale):
    p = p_ref[0]
    cos, sin, perm = c_ref[...], s_ref[...], perm_ref[...]
    cq = p[:, 0:Q_LORA].astype(F32)
    cqn = (cq * lax.rsqrt(jnp.mean(cq * cq, axis=-1, keepdims=True) + EPS) * qg_ref[...]).astype(BF16)
    q_nope = _dot(cqn, wqn_ref[...]).astype(BF16)
    q_rope = _dot(cqn, wqr_ref[...]).astype(BF16)
    q_rot = (_rot(q_rope, cos, sin, perm) * scale).astype(BF16)
    q_nope_s = (q_nope.astype(F32) * scale).astype(BF16)
    for h in range(N_HEADS):
        qt_ref[0, h] = (_dot_nt(wa_ref[h], q_nope_s) + _dot_nt(selq_ref[h], q_rot)).astype(BF16)
    ckv = p[:, Q_LORA:Q_LORA + KV_LORA].astype(F32)
    ckvn = (ckv * lax.rsqrt(jnp.mean(ckv * ckv, axis=-1, keepdims=True) + EPS) * kg_ref[...]).astype(BF16)
    kr = p[:, Q_LORA + KV_LORA:MLA_PAD]
    kr_rot = _rot(kr, cos, sin, perm).astype(BF16)
    kv_ref[0] = (_dot(ckvn, selc_ref[...]) + _dot(kr_rot, selr_ref[...])).astype(BF16)
    vt_ref[0] = (_dot_nt(selv_ref[...], ckvn) + one_ref[...]).astype(BF16)


def _mla_attn_kernel(qt_ref, kv_ref, vt_ref, wuv_ref, y_ref, m_ref, acc_ref, s_ref, *, tk, n_ctx, nt, latent):
    heads = range(N_HEADS)
    m_ref[...] = jnp.full_like(m_ref, -jnp.inf)
    acc_ref[...] = jnp.zeros_like(acc_ref)

    def scores(j0, size, slot):
        k = kv_ref[0, pl.ds(j0, size), :]
        for h in heads:
            s_ref[slot, h, 0:size, :] = _dot(k, qt_ref[0, h])

    def softmax_pv(j0, size, slot):
        vt = vt_ref[0, :, pl.ds(j0, size)]
        s = [s_ref[slot, h, 0:size, :] for h in heads]
        m_old = [m_ref[h] for h in heads]
        m_new = [jnp.maximum(m_old[h], jnp.max(s[h], axis=0, keepdims=True)) for h in heads]
        pr = [jnp.exp2(s[h] - m_new[h]).astype(BF16) for h in heads]
        pv = [_dot(vt, pr[h]) for h in heads]
        for h in heads:
            acc_ref[h] = jnp.exp2(m_old[h] - m_new[h]) * acc_ref[h] + pv[h]
            m_ref[h] = m_new[h]

    scores(0, n_ctx, 0)
    if not latent:
        softmax_pv(0, n_ctx, 0)
    else:
        n_tiles = (nt - n_ctx) // tk
        last = n_ctx + (n_tiles - 1) * tk
        scores(n_ctx, tk, 1)
        softmax_pv(0, n_ctx, 0)

        def body(jj, carry):
            t0 = pl.multiple_of(n_ctx + 2 * jj * tk, 256)
            t1 = pl.multiple_of(jnp.minimum(t0 + tk, last), 256)
            t2 = pl.multiple_of(jnp.minimum(t0 + 2 * tk, last), 256)
            scores(t1, tk, 0)
            softmax_pv(t0, tk, 1)
            scores(t2, tk, 1)
            softmax_pv(t1, tk, 0)
            return carry

        lax.fori_loop(0, n_tiles // 2, body, 0)
        if n_tiles % 2:
            softmax_pv(last, tk, 1)

    y = None
    for h in range(N_HEADS):
        acc = acc_ref[h]
        o = (acc[0:KV_LORA, :] / acc[KV_LORA:KV_LORA + 1, :]).astype(BF16)
        term = _dot_tn(o, wuv_ref[h])
        y = term if y is None else y + term
    y_ref[0] = y.astype(BF16)


def _mla(p, cos, sin, perm, qg, kg, wqn, wqr, wa, selq, selc, selr, selv, one_col, wuv, n_ctx, ctx_out):
    nb, nt, _ = p.shape
    n_lat = nt - n_ctx
    tm = _pick(nt, (768, 384, 256, 128))
    scale = (NOPE_DIM + ROPE_DIM) ** -0.5 * math.log2(math.e)
    c2 = lambda b, i: (0, 0)
    c3 = lambda b, i: (0, 0, 0)
    qt, kv, vt = pl.pallas_call(
        functools.partial(_mla_prep_kernel, scale=scale),
        grid=(nb, nt // tm),
        in_specs=[pl.BlockSpec((1, tm, MLA_PAD), lambda b, i: (b, i, P_MLA // MLA_PAD)),
                  pl.BlockSpec((tm, 128), lambda b, i: (i, 0)),
                  pl.BlockSpec((tm, 128), lambda b, i: (i, 0)),
                  pl.BlockSpec((128, 128), c2),
                  pl.BlockSpec((1, Q_LORA), c2),
                  pl.BlockSpec((1, KV_LORA), c2),
                  pl.BlockSpec((Q_LORA, N_HEADS * NOPE_DIM), c2),
                  pl.BlockSpec((Q_LORA, N_HEADS * ROPE_DIM), c2),
                  pl.BlockSpec((N_HEADS, QK_W, N_HEADS * NOPE_DIM), c3),
                  pl.BlockSpec((N_HEADS, QK_W, N_HEADS * ROPE_DIM), c3),
                  pl.BlockSpec((KV_LORA, QK_W), c2),
                  pl.BlockSpec((128, QK_W), c2),
                  pl.BlockSpec((VT_ROWS, KV_LORA), c2),
                  pl.BlockSpec((VT_ROWS, 1), c2)],
        out_specs=[pl.BlockSpec((1, N_HEADS, QK_W, tm), lambda b, i: (b, 0, 0, i)),
                   pl.BlockSpec((1, tm, QK_W), lambda b, i: (b, i, 0)),
                   pl.BlockSpec((1, VT_ROWS, tm), lambda b, i: (b, 0, i))],
        out_shape=[jax.ShapeDtypeStruct((nb, N_HEADS, QK_W, nt), BF16),
                   jax.ShapeDtypeStruct((nb, nt, QK_W), BF16),
                   jax.ShapeDtypeStruct((nb, VT_ROWS, nt), BF16)],
        compiler_params=_params("arbitrary", "arbitrary"),
        name="mla_prep",
    )(p, cos, sin, perm, qg, kg, wqn, wqr, wa, selq, selc, selr, selv, one_col)
    tk = _pick(n_lat, (512, 256))

    def attend(tq, first_col, n_q, latent):
        if first_col % tq == 0:
            q_spec = pl.BlockSpec((1, N_HEADS, QK_W, tq), lambda b, i: (b, 0, 0, i + first_col // tq))
        else:
            q_spec = pl.BlockSpec((pl.Element(1), pl.Element(N_HEADS), pl.Element(QK_W), pl.Element(tq)),
                                  lambda b, i: (b, 0, 0, pl.multiple_of(first_col + i * tq, 128)))
        return pl.pallas_call(
            functools.partial(_mla_attn_kernel, tk=tk, n_ctx=n_ctx, nt=nt, latent=latent),
            grid=(nb, n_q // tq),
            in_specs=[q_spec,
                      pl.BlockSpec((1, nt, QK_W), lambda b, i: (b, 0, 0)),
                      pl.BlockSpec((1, VT_ROWS, nt), lambda b, i: (b, 0, 0)),
                      pl.BlockSpec((N_HEADS, KV_LORA, MIX_W), c3)],
            out_specs=pl.BlockSpec((1, tq, MIX_W), lambda b, i: (b, i, 0)),
            out_shape=jax.ShapeDtypeStruct((nb, n_q, MIX_W), BF16),
            scratch_shapes=[pltpu.VMEM((N_HEADS, 1, tq), F32), pltpu.VMEM((N_HEADS, VT_ROWS, tq), F32),
                            pltpu.VMEM((2, N_HEADS, max(tk, n_ctx) if latent else n_ctx, tq), F32)],
            compiler_params=_params("arbitrary", "arbitrary"),
            name="mla_attn" if latent else "mla_attn_ctx",
        )(qt, kv, vt, wuv)

    y_lat = attend(_pick(n_lat, (512, 256)), n_ctx, n_lat, True)
    y_ctx = attend(_pick(n_ctx, (256, 128)), 0, n_ctx, False) if ctx_out else None
    return y_lat, y_ctx


def _merge_kernel(x_ref, ctx_ref, yr_ref, ys_ref, of_ref, ob_ref, ym_ref, z_ref, g0_ref, g1_ref, g2_ref, g3_ref,
                  wb_ref, wo_ref, ng_ref, gp_ref, ml_ref, mc_ref, ones_ref, o_ref, *, tm, n_ctx, row0, split):
    i = pl.program_id(1)
    if split:
        first = jnp.concatenate([ctx_ref[0], x_ref[0, 0:tm - n_ctx, :]], axis=0)
        x_res = jnp.where(i == 0, first, x_ref[0])
    else:
        x_res = x_ref[0]
    od = of_ref[0] + ob_ref[0]
    ms = _head_sum(od * od, ones_ref[...]) * (1.0 / HEAD_DIM)
    z = z_ref[0].astype(F32)
    ydn = (od * lax.rsqrt(ms + EPS) * ng_ref[...]) * (z * jax.nn.sigmoid(z))
    ys = (yr_ref[0], ys_ref[0], ydn.astype(BF16), ym_ref[0])
    gates = (g0_ref, g1_ref, g2_ref, g3_ref)
    acc = None
    for b in range(N_BRANCH):
        term = jax.nn.sigmoid(gates[b][0].astype(F32)) * _dot(ys[b], wb_ref[b])
        acc = term if acc is None else acc + term
    y = _dot(acc.astype(BF16), wo_ref[...])
    r = y * lax.rsqrt(jnp.mean(y * y, axis=-1, keepdims=True) + EPS) * gp_ref[...]
    rows = lax.broadcasted_iota(jnp.int32, (tm, 1), 0) + (row0 + i * tm)
    gate = jnp.where(rows < n_ctx, mc_ref[0, 2:3, :], ml_ref[0, 2:3, :])
    o_ref[0] = x_res + gate * r


def _merge(tokens, y_ret, y_sg, o_f, o_b, y_mla, p, wb, wo, ng, gp, mod, ones_bd, n_ctx, row0):
    nb, nt, _ = p.shape
    d = D_MODEL
    n_rows = nt - row0
    tm = _pick(n_rows, (768, 512, 384, 256, 128))
    c2 = lambda b, i: (0, 0)
    if row0 == 0:
        def window(width, col):
            return pl.BlockSpec((1, tm, width), lambda b, i: (b, i, col // width))

        (xa, ctx), (x_spec, ctx_spec), split = _token_sources(tokens, n_ctx, tm)
    else:
        def window(width, col):
            return pl.BlockSpec((pl.Element(1), pl.Element(tm), pl.Element(width)),
                                lambda b, i: (b, pl.multiple_of(row0 + i * tm, 128), col))

        xa, x_spec, split = tokens, window(d, 0), False
        ctx, ctx_spec = jnp.zeros((nb, 8, d), F32), pl.BlockSpec((1, 8, d), lambda b, i: (b, 0, 0))
    y_spec = window(MIX_W, 0)
    assert y_mla.shape[1] == n_rows
    mla_spec = pl.BlockSpec((1, tm, MIX_W), lambda b, i: (b, i, 0))
    gate_specs = [window(d, P_GATE + k * d) for k in range(N_BRANCH)]
    return pl.pallas_call(
        functools.partial(_merge_kernel, tm=tm, n_ctx=n_ctx, row0=row0, split=split),
        grid=(nb, n_rows // tm),
        in_specs=[x_spec, ctx_spec, y_spec, y_spec, y_spec, y_spec, mla_spec,
                  window(MIX_W, P_DN + 3 * MIX_W),
                  *gate_specs,
                  pl.BlockSpec((N_BRANCH, MIX_W, d), lambda b, i: (0, 0, 0)),
                  pl.BlockSpec((d, d), c2),
                  pl.BlockSpec((1, MIX_W), c2),
                  pl.BlockSpec((1, d), c2),
                  pl.BlockSpec((1, 6, d), lambda b, i: (b, 0, 0)),
                  pl.BlockSpec((1, 6, d), lambda b, i: (nb, 0, 0)),
                  pl.BlockSpec((MIX_W, MIX_W), c2)],
        out_specs=pl.BlockSpec((1, tm, d), lambda b, i: (b, i, 0)),
        out_shape=jax.ShapeDtypeStruct((nb, n_rows, d), F32),
        compiler_params=_params("arbitrary", "arbitrary"),
        name="merge",
    )(xa, ctx, y_ret, y_sg, o_f, o_b, y_mla, p, p, p, p, p, wb, wo, ng, gp, mod, mod, ones_bd)


def _route(sel, aff):
    rows = [sel[e:e + 1, :] for e in range(N_EXPERTS)]
    pairs = [(a, b) for a in range(EXPERTS_PER_GROUP) for b in range(a + 1, EXPERTS_PER_GROUP)]
    grp_score, grp_pair = [], []
    for g in range(N_GROUPS):
        base = g * EXPERTS_PER_GROUP
        best = rows[base + pairs[0][0]] + rows[base + pairs[0][1]]
        best_p = jnp.zeros_like(best, dtype=jnp.int32)
        for pi in range(1, len(pairs)):
            s = rows[base + pairs[pi][0]] + rows[base + pairs[pi][1]]
            take = s > best
            best = jnp.where(take, s, best)
            best_p = jnp.where(take, pi, best_p)
        grp_score.append(best)
        grp_pair.append(best_p)
    top = grp_score[0]
    top_g = jnp.zeros_like(grp_pair[0])
    top_p = grp_pair[0]
    for g in range(1, N_GROUPS):
        take = grp_score[g] > top
        top = jnp.where(take, grp_score[g], top)
        top_g = jnp.where(take, g, top_g)
        top_p = jnp.where(take, grp_pair[g], top_p)
    picked = []
    for e in range(N_EXPERTS):
        g, k = divmod(e, EXPERTS_PER_GROUP)
        in_pair = None
        for pi, (a, b) in enumerate(pairs):
            if k in (a, b):
                hit = top_p == pi
                in_pair = hit if in_pair is None else (in_pair | hit)
        picked.append(jnp.where((top_g == g) & in_pair, aff[e:e + 1, :], 0.0))
    denom = picked[0]
    for e in range(1, N_EXPERTS):
        denom = denom + picked[e]
    return [pk / denom for pk in picked]


def _swiglu(hn, w1, w3, w2, scale):
    a = _dot(hn, w1)
    h = (a * jax.nn.sigmoid(a)) * _dot(hn, w3)
    if scale is not None:
        h = h * scale
    return _dot(h.astype(BF16), w2)


def _moe_kernel(x_ref, ml_ref, mc_ref, g2_ref, gp_ref, rw_ref, rb_ref, ws1_ref, ws3_ref, ws2_ref,
                w1_ref, w3_ref, w2_ref, o_ref, hn_ref, comb_t_ref, comb_ref, acc_ref, *, tm, rb, n_ctx):
    i = pl.program_id(1)
    e = pl.program_id(2)

    @pl.when(e == 0)
    def _():
        def blk(r, carry):
            r0 = pl.multiple_of(r * rb, rb)
            x = x_ref[0, pl.ds(r0, rb), :]
            hn = _norm_modulate(x, g2_ref[...], i * tm + r0 < n_ctx, mc_ref, ml_ref, 3, 4)
            hn_ref[pl.ds(r0, rb), :] = hn.astype(BF16)
            return carry

        lax.fori_loop(0, tm // rb, blk, 0)
        hn = hn_ref[...]
        aff = jax.nn.sigmoid(_dot_nt(rw_ref[...], hn))
        comb = _route(aff + rb_ref[...], aff)
        comb_t_ref[...] = jnp.zeros_like(comb_t_ref)
        for k in range(N_EXPERTS):
            comb_t_ref[k:k + 1, :] = comb[k]
        comb_ref[...] = comb_t_ref[...].T
        acc_ref[...] = _swiglu(hn, ws1_ref[0], ws3_ref[0], ws2_ref[0], None)

    @pl.when(e > 0)
    def _():
        lane = lax.broadcasted_iota(jnp.int32, (1, 128), 1)
        comb = comb_ref[...]
        hn = hn_ref[...]
        first = 2 * (e - 1)
        y = None
        for k in range(2):
            c_k = jnp.sum(jnp.where(lane == first + k, comb, 0.0), axis=-1, keepdims=True)
            term = _swiglu(hn, w1_ref[0, k], w3_ref[0, k], w2_ref[0, k], c_k)
            y = term if y is None else y + term
        acc_ref[...] += y

    @pl.when(e == pl.num_programs(2) - 1)
    def _():
        y = acc_ref[...]
        r = y * lax.rsqrt(jnp.mean(y * y, axis=-1, keepdims=True) + EPS) * gp_ref[...]
        rows = lax.broadcasted_iota(jnp.int32, (tm, 1), 0) + i * tm
        gate = jnp.where(rows < n_ctx, mc_ref[0, 5:6, :], ml_ref[0, 5:6, :])
        o_ref[0] = x_ref[0] + gate * r


def _moe(xa, mod, g2, gp, rw_t, rbias, ws1, ws3, ws2, w1, w3, w2, layer, n_ctx):
    nb, nt, d = xa.shape
    tm = _pick(nt, (1024, 768, 512, 384, 256, 128))
    n_pairs = w1.shape[1] // 2
    row = lambda b, i, e: (b, i, 0)
    c2 = lambda b, i, e: (0, 0)
    shared_blk = lambda b, i, e: (layer, 0, 0)
    pair_blk = lambda b, i, e: (layer, jnp.maximum(e - 1, 0), 0, 0)
    return pl.pallas_call(
        functools.partial(_moe_kernel, tm=tm, rb=128, n_ctx=n_ctx),
        grid=(nb, nt // tm, n_pairs + 1),
        in_specs=[pl.BlockSpec((1, tm, d), row),
                  pl.BlockSpec((1, 6, d), lambda b, i, e: (b, 0, 0)),
                  pl.BlockSpec((1, 6, d), lambda b, i, e: (nb, 0, 0)),
                  pl.BlockSpec((1, d), c2),
                  pl.BlockSpec((1, d), c2),
                  pl.BlockSpec((N_EXPERTS, d), c2),
                  pl.BlockSpec((N_EXPERTS, 1), c2),
                  pl.BlockSpec((1, d, D_EXPERT), shared_blk),
                  pl.BlockSpec((1, d, D_EXPERT), shared_blk),
                  pl.BlockSpec((1, D_EXPERT, d), shared_blk),
                  pl.BlockSpec((1, 2, d, D_EXPERT), pair_blk),
                  pl.BlockSpec((1, 2, d, D_EXPERT), pair_blk),
                  pl.BlockSpec((1, 2, D_EXPERT, d), pair_blk)],
        out_specs=pl.BlockSpec((1, tm, d), row),
        out_shape=jax.ShapeDtypeStruct((nb, nt, d), F32),
        scratch_shapes=[pltpu.VMEM((tm, d), BF16), pltpu.VMEM((128, tm), F32), pltpu.VMEM((tm, 128), F32),
                        pltpu.VMEM((tm, d), F32)],
        compiler_params=_params("arbitrary", "arbitrary", "arbitrary"),
        name="moe",
    )(xa, mod, mod, g2, gp, rw_t, rbias, ws1, ws3, ws2, w1, w3, w2)


def _swap_perm(width, group):
    j = np.arange(width)
    src = np.where((j % group) < group // 2, j + group // 2, j - group // 2)
    return jnp.asarray(np.arange(width)[:, None] == src[None, :], BF16)


def _rope_tables(n_lat, n_ctx):
    def angles(pos, dim):
        half = dim // 2
        inv = ROPE_BASE ** (-jnp.arange(half, dtype=F32) / half)
        return pos.astype(F32)[:, None] * inv[None, :]

    def tables(cos_parts, sin_parts, reps):
        cos = jnp.tile(jnp.concatenate(cos_parts, axis=-1), (1, reps))
        sin = jnp.tile(jnp.concatenate(sin_parts, axis=-1), (1, reps))
        w = cos.shape[1]
        return (jnp.concatenate([jnp.ones((n_ctx, w), F32), cos], axis=0),
                jnp.concatenate([jnp.zeros((n_ctx, w), F32), sin], axis=0))

    rows = n_lat // GRID_W
    ang_t = angles(jnp.arange(n_lat), HEAD_DIM)
    ang_r = angles(jnp.repeat(jnp.arange(rows), GRID_W), ROPE_DIM // 2)
    ang_c = angles(jnp.tile(jnp.arange(GRID_W), rows), ROPE_DIM // 2)
    ct, st = jnp.cos(ang_t), jnp.sin(ang_t)
    ret = tables([ct, ct], [-st, st], N_HEADS)
    cr, sr, cc, sc = jnp.cos(ang_r), jnp.sin(ang_r), jnp.cos(ang_c), jnp.sin(ang_c)
    mla = tables([cr, cr, cc, cc], [-sr, sr, -sc, sc], N_HEADS)
    return ret, mla


def _ret_tables(logit):
    log_g = jax.nn.log_sigmoid(logit.astype(F32))
    lane_lg = jnp.repeat(log_g, HEAD_DIM, axis=1)
    idx = jnp.arange(CHUNK, dtype=F32)[:, None]
    kd = jnp.stack([jnp.exp(lane_lg[0][None, :] * (CHUNK - 1 - idx)), jnp.exp(lane_lg[1][None, :] * idx)])
    qd = jnp.stack([jnp.exp(lane_lg[0][None, :] * (idx + 1)), jnp.exp(lane_lg[1][None, :] * (CHUNK - idx))])
    cd = jnp.exp(lane_lg * CHUNK)[:, None, :]
    diff = idx - idx.T
    blocks = []
    for h in range(N_HEADS):
        f = jnp.exp(log_g[0, h] * jnp.where(diff >= 0, diff, 0.0))
        b = jnp.exp(log_g[1, h] * jnp.where(diff < 0, -diff, 0.0))
        blocks.append(jnp.where(diff >= 0, f, b))
    dm = jnp.concatenate(blocks, axis=1)
    return kd, cd, qd, dm


def _pack_w_in(w_in):
    d = w_in.shape[0]
    mla = jnp.concatenate([w_in[:, OFF_MLA:OFF_MLA + MLA_COLS], jnp.zeros((d, MLA_PAD - MLA_COLS), w_in.dtype)], 1)
    w = jnp.concatenate([w_in[:, OFF_RET:OFF_RET + RET_COLS], w_in[:, OFF_DN:OFF_DN + 4 * MIX_W],
                         w_in[:, OFF_SG:OFF_SG + SG_COLS], mla, w_in[:, OFF_GATE:OFF_GATE + GATE_COLS]], axis=1)
    wab = w_in[:, OFF_DN + 4 * MIX_W:OFF_DN + DN_COLS]
    wabc = jnp.concatenate([wab, jnp.zeros((d, 128 - 4 * N_HEADS), w_in.dtype)], axis=1)
    return w.astype(BF16), wab.T.astype(BF16), wabc.astype(BF16)


def _mla_weights(w_uq, w_ukv):
    dq = NOPE_DIM + ROPE_DIM
    dkv = NOPE_DIM + V_DIM
    wq = w_uq.reshape(Q_LORA, N_HEADS, dq)
    wqn = wq[:, :, :NOPE_DIM].reshape(Q_LORA, N_HEADS * NOPE_DIM)
    wqr = wq[:, :, NOPE_DIM:].reshape(Q_LORA, N_HEADS * ROPE_DIM)
    wkv = w_ukv.reshape(KV_LORA, N_HEADS, dkv)
    head_eye = jnp.eye(N_HEADS, dtype=F32)
    wa = jnp.einsum("chd,hg->hcgd", wkv[:, :, :NOPE_DIM], head_eye).reshape(N_HEADS, KV_LORA, N_HEADS * NOPE_DIM)
    wa = jnp.pad(wa, ((0, 0), (0, QK_W - KV_LORA), (0, 0)))
    wuv = jnp.einsum("chd,hg->hcgd", wkv[:, :, NOPE_DIM:], head_eye).reshape(N_HEADS, KV_LORA, MIX_W)
    selq = np.zeros((N_HEADS, QK_W, N_HEADS * ROPE_DIM), np.float32)
    for h in range(N_HEADS):
        selq[h, KV_LORA:KV_LORA + ROPE_DIM, h * ROPE_DIM:(h + 1) * ROPE_DIM] = np.eye(ROPE_DIM)
    selc = np.zeros((KV_LORA, QK_W), np.float32)
    selc[:, 0:KV_LORA] = np.eye(KV_LORA)
    selr = np.zeros((128, QK_W), np.float32)
    selr[0:ROPE_DIM, KV_LORA:KV_LORA + ROPE_DIM] = np.eye(ROPE_DIM)
    selv = np.zeros((VT_ROWS, KV_LORA), np.float32)
    selv[0:KV_LORA, :] = np.eye(KV_LORA)
    one_col = np.zeros((VT_ROWS, 1), np.float32)
    one_col[KV_LORA, 0] = 1.0
    return (tuple(jnp.asarray(a, BF16) for a in (wqn, wqr, wa, selq, selc, selr, selv))
            + (jnp.asarray(one_col), wuv.astype(BF16)))


def kernel(x, c, ctx, c_ctx, w_ada, b_ada, g_pre1, g_post1, g_pre2, g_post2, w_in, ret_decay_logit, sg_norm_g, sg_w, sg_b, dn_conv_w, dn_A_log, dn_dt_bias, dn_norm_g, mla_q_norm_g, mla_kv_norm_g, mla_w_uq, mla_w_ukv, w_branch, w_out, router_w, router_bias, moe_w1, moe_w3, moe_w2, shared_w1, shared_w3, shared_w2):
    nb, n_lat, d = x.shape
    n_ctx = ctx.shape[1]
    depth = w_in.shape[0]
    assert d == D_MODEL and n_lat % GRID_W == 0 and n_lat % CHUNK == 0 and n_ctx % 256 == 0
    ncc = n_ctx // CHUNK

    n_cond = -(-(nb + 1) // 8) * 8
    cond = jnp.concatenate([c, c_ctx[None], jnp.zeros((n_cond - nb - 1, d), F32)], axis=0)
    mod_all = _adaln(cond, w_ada, b_ada).reshape(depth, n_cond, 6, d)

    (ret_cos, ret_sin), (mla_cos, mla_sin) = _rope_tables(n_lat, n_ctx)
    perm_ret = _swap_perm(MIX_W, HEAD_DIM)
    perm_mla = _swap_perm(N_HEADS * ROPE_DIM, ROPE_DIM // 2)
    lane_head = jnp.arange(MIX_W) // HEAD_DIM
    bd = (lane_head[:, None] == lane_head[None, :]).astype(F32)
    ones_bd = bd.astype(BF16)
    rw_t = router_w.T.astype(BF16)
    rbias = router_bias.astype(F32)[:, None]
    expert_w = tuple(w.astype(BF16) for w in (shared_w1, shared_w3, shared_w2, moe_w1, moe_w3, moe_w2))

    xa = (ctx, x)
    for l in range(depth):
        last = l == depth - 1
        if last and isinstance(xa, tuple):
            xa = jnp.concatenate(xa, axis=1)
        mod = mod_all[l]
        w_l, wab_l, wabc_l = _pack_w_in(w_in[l])
        p, ab_t, ab_c = _inproj(xa, n_ctx + n_lat, mod, g_pre1[l][None], w_l, wab_l, wabc_l, n_ctx)

        kd, cd, qd, dm = _ret_tables(ret_decay_logit[l])
        wcat = jnp.concatenate([sg_w[l, h] for h in range(N_HEADS)], axis=1).astype(BF16)
        sg_bias = jnp.repeat(sg_b[l].T, HEAD_DIM, axis=1)
        y_ret, y_sg = _retention_and_sgate(p, ret_cos, ret_sin, perm_ret, (kd, cd, qd, dm, bd, ones_bd),
                                           sg_norm_g[l][None], wcat, sg_bias, ncc)

        neg_a = (-jnp.exp(dn_A_log[l].astype(F32))).reshape(2 * N_HEADS, 1)
        dtb = dn_dt_bias[l].astype(F32).reshape(2 * N_HEADS, 1)
        conv_w = jnp.concatenate([dn_conv_w[l], jnp.zeros((8 - CONV_W, 3 * MIX_W), F32)], axis=0)
        o_f, o_b = _deltanet(p, ab_t, ab_c, conv_w, neg_a, dtb, bd, ones_bd, ncc)

        row0 = n_ctx if last else 0
        y_mla, y_mla_ctx = _mla(p, mla_cos, mla_sin, perm_mla, mla_q_norm_g[l][None], mla_kv_norm_g[l][None],
                                *_mla_weights(mla_w_uq[l], mla_w_ukv[l]), n_ctx, not last)
        if not last:
            y_mla = jnp.concatenate([y_mla_ctx, y_mla], axis=1)

        xa = _merge(xa, y_ret, y_sg, o_f, o_b, y_mla, p, w_branch[l].astype(BF16), w_out[l].astype(BF16),
                    jnp.tile(dn_norm_g[l], N_HEADS)[None], g_post1[l][None], mod, ones_bd, n_ctx, row0)

        xa = _moe(xa, mod, g_pre2[l][None], g_post2[l][None], rw_t, rbias, *expert_w, l, n_ctx - row0)
    return xa
```

```python
import functools
import math

import jax
import jax.numpy as jnp
import numpy as np
from jax import lax
from jax.experimental import pallas as pl
from jax.experimental.pallas import tpu as pltpu

F32 = jnp.float32
BF16 = jnp.bfloat16
HIGHEST = lax.Precision.HIGHEST

D_MODEL = 1024
GRID_W = 64
N_HEADS = 4
HEAD_DIM = 64
MIX_W = N_HEADS * HEAD_DIM
CHUNK = 128
ROPE_BASE = 10000.0
EPS = 1e-6
RET_DECAY_EXP0 = 5.0
CONV_W = 5
Q_LORA = 256
KV_LORA = 128
NOPE_DIM = 64
ROPE_DIM = 32
V_DIM = 64
N_EXPERTS = 16
N_GROUPS = 4
EXPERTS_PER_GROUP = N_EXPERTS // N_GROUPS
D_EXPERT = 256
N_BRANCH = 4

RET_COLS = 4 * MIX_W
SG_COLS = 2 * MIX_W
DN_COLS = 4 * MIX_W + 4 * N_HEADS
MLA_COLS = Q_LORA + KV_LORA + ROPE_DIM
GATE_COLS = N_BRANCH * D_MODEL
OFF_RET = 0
OFF_SG = OFF_RET + RET_COLS
OFF_DN = OFF_SG + SG_COLS
OFF_MLA = OFF_DN + DN_COLS
OFF_GATE = OFF_MLA + MLA_COLS

P_RET = 0
P_DN = 1024
P_SG = 2048
P_MLA = 2560
P_GATE = 3072
P_COLS = 7168
MLA_PAD = 512

VMEM_LIMIT = 56 * 1024 * 1024


def _dot(a, b, precision=None):
    return jnp.dot(a, b, preferred_element_type=F32, precision=precision)


def _dot_nt(a, b, precision=None):
    return lax.dot_general(a, b, (((1,), (1,)), ((), ())), preferred_element_type=F32, precision=precision)


def _dot_tn(a, b):
    return lax.dot_general(a, b, (((0,), (0,)), ((), ())), preferred_element_type=F32)


def _mm(a, b):
    return _dot(a.astype(BF16), b.astype(BF16))


def _params(*sem):
    return pltpu.CompilerParams(dimension_semantics=sem, vmem_limit_bytes=VMEM_LIMIT)


def _pick(n, cands):
    for c in cands:
        if n % c == 0:
            return c
    raise ValueError(f"no tile for {n}")


def _head_of_lane(width, group):
    return lax.broadcasted_iota(jnp.int32, (1, width), 1) // group


def _stack_heads(x):
    head = _head_of_lane(MIX_W, HEAD_DIM)
    xf = x.astype(F32)
    return jnp.concatenate([jnp.where(head == h, xf, 0.0).astype(BF16) for h in range(N_HEADS)], axis=0)


def _expand_heads(cols):
    head = _head_of_lane(MIX_W, HEAD_DIM)
    out = cols[:, N_HEADS - 1:N_HEADS]
    for h in range(N_HEADS - 2, -1, -1):
        out = jnp.where(head <= h, cols[:, h:h + 1], out)
    return out


def _head_sum(x, ones_bd):
    hi = x.astype(BF16)
    lo = (x - hi.astype(F32)).astype(BF16)
    return _dot(hi, ones_bd) + _dot(lo, ones_bd)


def _rot(x_bf, cos, sin, perm):
    return x_bf.astype(F32) * cos + _dot(x_bf, perm) * sin


def _norm_modulate(x, g, is_ctx, mc_ref, ml_ref, shift_row, scale_row):
    shift = jnp.where(is_ctx, mc_ref[0, shift_row:shift_row + 1, :], ml_ref[0, shift_row:shift_row + 1, :])
    scale = jnp.where(is_ctx, mc_ref[0, scale_row:scale_row + 1, :], ml_ref[0, scale_row:scale_row + 1, :])
    gain = g * (1.0 + scale)
    return x * lax.rsqrt(jnp.mean(x * x, axis=-1, keepdims=True) + EPS) * gain + shift


def _adaln_kernel(c_ref, w_ref, b_ref, o_ref):
    c = c_ref[...]
    s = c * jax.nn.sigmoid(c)
    o_ref[0] = _dot(s, w_ref[0], precision=HIGHEST) + b_ref[0]


def _adaln(cond, w_ada, b_ada):
    n_l, d, d6 = w_ada.shape
    r = cond.shape[0]
    tn = 1024
    return pl.pallas_call(
        _adaln_kernel,
        grid=(n_l, d6 // tn),
        in_specs=[pl.BlockSpec((r, d), lambda l, j: (0, 0)),
                  pl.BlockSpec((1, d, tn), lambda l, j: (l, 0, j)),
                  pl.BlockSpec((1, 1, tn), lambda l, j: (l, 0, j))],
        out_specs=pl.BlockSpec((1, r, tn), lambda l, j: (l, 0, j)),
        out_shape=jax.ShapeDtypeStruct((n_l, r, d6), F32),
        compiler_params=_params("arbitrary", "arbitrary"),
        name="adaln",
    )(cond, w_ada, b_ada.reshape(n_l, 1, d6))


def _inproj_kernel(x_ref, ctx_ref, ml_ref, mc_ref, g_ref, w_ref, wab_ref, wabc_ref, p_ref, ab_ref, abc_ref, xn_ref,
                   *, tm, rb, n_ctx, split):
    i = pl.program_id(1)
    j = pl.program_id(2)

    @pl.when(j == 0)
    def _():
        def blk(r, carry):
            r0 = pl.multiple_of(r * rb, rb)
            is_ctx = i * tm + r0 < n_ctx
            if split:
                x_off = pl.multiple_of(jnp.maximum(jnp.where(i == 0, r0 - n_ctx, r0), 0), rb)
                c_off = pl.multiple_of(jnp.minimum(r0, n_ctx - rb), rb)
                x = jnp.where(is_ctx, ctx_ref[0, pl.ds(c_off, rb), :], x_ref[0, pl.ds(x_off, rb), :])
            else:
                x = x_ref[0, pl.ds(r0, rb), :]
            hn = _norm_modulate(x, g_ref[...], is_ctx, mc_ref, ml_ref, 0, 1)
            xn_ref[pl.ds(r0, rb), :] = hn.astype(BF16)
            return carry

        lax.fori_loop(0, tm // rb, blk, 0)
        ab_ref[0] = _dot_nt(wab_ref[...], xn_ref[...])
        abc_ref[0] = _dot(xn_ref[...], wabc_ref[...])

    p_ref[0] = _dot(xn_ref[...], w_ref[...]).astype(BF16)


def _token_sources(tokens, n_ctx, tm):
    if isinstance(tokens, tuple):
        ctx, x = tokens
        d = x.shape[-1]
        if x.shape[1] >= tm and tm > n_ctx:
            x_spec = pl.BlockSpec(
                (pl.Element(1), pl.Element(tm), pl.Element(d)),
                lambda b, i, *_: (b, pl.multiple_of(jnp.maximum(i * tm - n_ctx, 0), 128), 0))
            ctx_spec = pl.BlockSpec((1, n_ctx, d), lambda b, i, *_: (b, 0, 0))
            return (x, ctx), (x_spec, ctx_spec), True
        tokens = jnp.concatenate([ctx, x], axis=1)
    nb, _, d = tokens.shape
    dummy = jnp.zeros((nb, 8, d), tokens.dtype)
    return ((tokens, dummy), (pl.BlockSpec((1, tm, d), lambda b, i, *_: (b, i, 0)),
                              pl.BlockSpec((1, 8, d), lambda b, i, *_: (b, 0, 0))), False)


def _inproj(tokens, nt, mod, g, w, wab, wabc, n_ctx):
    tm = _pick(nt, (1408, 768, 384, 256, 128))
    tn = 1792
    (xa, ctx), (x_spec, ctx_spec), split = _token_sources(tokens, n_ctx, tm)
    nb, _, d = xa.shape
    kern = functools.partial(_inproj_kernel, tm=tm, rb=128, n_ctx=n_ctx, split=split)
    return pl.pallas_call(
        kern,
        grid=(nb, nt // tm, P_COLS // tn),
        in_specs=[x_spec, ctx_spec,
                  pl.BlockSpec((1, 6, d), lambda b, i, j: (b, 0, 0)),
                  pl.BlockSpec((1, 6, d), lambda b, i, j: (nb, 0, 0)),
                  pl.BlockSpec((1, d), lambda b, i, j: (0, 0)),
                  pl.BlockSpec((d, tn), lambda b, i, j: (0, j)),
                  pl.BlockSpec((16, d), lambda b, i, j: (0, 0)),
                  pl.BlockSpec((d, 128), lambda b, i, j: (0, 0))],
        out_specs=[pl.BlockSpec((1, tm, tn), lambda b, i, j: (b, i, j)),
                   pl.BlockSpec((1, 16, tm), lambda b, i, j: (b, 0, i)),
                   pl.BlockSpec((1, tm, 128), lambda b, i, j: (b, i, 0))],
        out_shape=[jax.ShapeDtypeStruct((nb, nt, P_COLS), BF16),
                   jax.ShapeDtypeStruct((nb, 16, nt), F32),
                   jax.ShapeDtypeStruct((nb, nt, 128), F32)],
        scratch_shapes=[pltpu.VMEM((tm, d), BF16)],
        compiler_params=_params("arbitrary", "arbitrary", "arbitrary"),
        name="inproj",
    )(xa, ctx, mod, mod, g, w, wab, wabc)


def _bwd_chunk(t, ncc, nc):
    return jnp.where(t < ncc, ncc - 1 - t, nc - 1 - (t - ncc))


def _ret_state_kernel(pf_ref, pb_ref, cf_ref, sf_ref, cb_ref, sb_ref, perm_ref, kd_ref, cd_ref, bd_ref,
                      of_ref, ob_ref, st_f, st_b, *, cb):
    t = pl.program_id(1)

    @pl.when(t == 0)
    def _():
        st_f[...] = jnp.zeros_like(st_f)
        st_b[...] = jnp.zeros_like(st_b)

    def increments(p_ref, c_ref, s_ref, d):
        out = []
        for i in range(cb):
            r = slice(i * CHUNK, (i + 1) * CHUNK)
            kr = _rot(p_ref[0, r, 0:MIX_W], c_ref[r, :], s_ref[r, :], perm_ref[...]) * (HEAD_DIM ** -0.5)
            out.append(bd_ref[...] * _dot_tn((kr * kd_ref[d]).astype(BF16), p_ref[0, r, MIX_W:2 * MIX_W]))
        return out

    inc_f = increments(pf_ref, cf_ref, sf_ref, 0)
    inc_b = increments(pb_ref, cb_ref, sb_ref, 1)
    s = st_f[...]
    for i in range(cb):
        of_ref[0, i] = s.astype(BF16)
        s = cd_ref[0] * s + inc_f[i]
    st_f[...] = s
    s = st_b[...]
    for i in reversed(range(cb)):
        ob_ref[0, i] = s.astype(BF16)
        s = cd_ref[1] * s + inc_b[i]
    st_b[...] = s


def _gelu_tanh(x):
    return 0.5 * x * (1.0 + jnp.tanh(math.sqrt(2.0 / math.pi) * (x + 0.044715 * (x * x * x))))


def _mix_out_kernel(p_ref, pg_ref, c_ref, s_ref, sf_ref, sb_ref, perm_ref, dm_ref, qd_ref, ones_ref,
                    ng_ref, wg_ref, bg_ref, y_ref, ysg_ref, *, cb):
    chunks = range(cb)
    rows = [slice(i * CHUNK, (i + 1) * CHUNK) for i in chunks]
    perm, dm, ones_bd = perm_ref[...], dm_ref[...], ones_ref[...]
    p = [p_ref[0, r, :] for r in rows]
    cos = [c_ref[r, :] for r in rows]
    sin = [s_ref[r, :] for r in rows]
    qr = [_rot(p[i][:, 0:MIX_W], cos[i], sin[i], perm) for i in chunks]
    kr = [_rot(p[i][:, MIX_W:2 * MIX_W], cos[i], sin[i], perm) * (HEAD_DIM ** -0.5) for i in chunks]
    z = [_gelu_tanh(pg_ref[0, r, :].astype(F32)) for r in rows]
    vg = [x[:, MIX_W:] for x in z]
    mu_g = [jnp.mean(x, axis=-1, keepdims=True) for x in vg]
    vgc = [x - m for x, m in zip(vg, mu_g)]
    var_g = [jnp.mean(x * x, axis=-1, keepdims=True) for x in vgc]
    vn = [x * lax.rsqrt(s + EPS) * ng_ref[...] for x, s in zip(vgc, var_g)]
    sc = [_dot_nt(qr[i].astype(BF16), _stack_heads(kr[i])) * dm for i in chunks]
    mixed = [_dot(wg_ref[...], _stack_heads(x)) for x in vn]
    o = [_dot(sc[i].astype(BF16), _stack_heads(p[i][:, 2 * MIX_W:3 * MIX_W])) for i in chunks]
    qs = [jnp.concatenate([(qr[i] * qd_ref[0]).astype(BF16), (qr[i] * qd_ref[1]).astype(BF16)], axis=1)
          for i in chunks]
    ss = [jnp.concatenate([sf_ref[0, i], sb_ref[0, i]], axis=0) for i in chunks]
    o = [o[i] + _dot(qs[i], ss[i]) for i in chunks]
    for i in chunks:
        ysg_ref[0, rows[i], :] = (z[i][:, :MIX_W] * (mixed[i] + bg_ref[...])).astype(BF16)
    mu = [_head_sum(x, ones_bd) * (1.0 / HEAD_DIM) for x in o]
    oc = [x - m for x, m in zip(o, mu)]
    var = [_head_sum(x * x, ones_bd) * (1.0 / HEAD_DIM) for x in oc]
    for i in chunks:
        g = p[i][:, 3 * MIX_W:4 * MIX_W].astype(F32)
        y_ref[0, rows[i], :] = (oc[i] * lax.rsqrt(var[i] + EPS) * (g * jax.nn.sigmoid(g))).astype(BF16)


def _retention_and_sgate(p, cos, sin, perm, tabs, sg_ng, sg_w, sg_bias, ncc):
    nb, nt, _ = p.shape
    nc = nt // CHUNK
    kd, cd, qd, dm, bd, ones_bd = tabs
    cb = 2
    assert nc % cb == 0 and ncc % cb == 0
    nblk, ncb = nc // cb, ncc // cb
    fwd = lambda b, t: (b, t, 0)
    bwd = lambda b, t: (b, _bwd_chunk(t, ncb, nblk), 0)
    tab_f = lambda b, t: (t, 0)
    tab_b = lambda b, t: (_bwd_chunk(t, ncb, nblk), 0)
    c2 = lambda b, t: (0, 0)
    c3 = lambda b, t: (0, 0, 0)
    st_shape = jax.ShapeDtypeStruct((nb, nc, MIX_W, MIX_W), BF16)

    def kv_window(block_of):
        return pl.BlockSpec((pl.Element(1), pl.Element(cb * CHUNK), pl.Element(2 * MIX_W)),
                            lambda b, t: (b, pl.multiple_of(block_of(t) * (cb * CHUNK), 128), P_RET + MIX_W))

    st_f, st_b = pl.pallas_call(
        functools.partial(_ret_state_kernel, cb=cb),
        grid=(nb, nblk),
        in_specs=[kv_window(lambda t: t), kv_window(lambda t: _bwd_chunk(t, ncb, nblk)),
                  pl.BlockSpec((cb * CHUNK, MIX_W), tab_f), pl.BlockSpec((cb * CHUNK, MIX_W), tab_f),
                  pl.BlockSpec((cb * CHUNK, MIX_W), tab_b), pl.BlockSpec((cb * CHUNK, MIX_W), tab_b),
                  pl.BlockSpec((MIX_W, MIX_W), c2),
                  pl.BlockSpec((2, CHUNK, MIX_W), c3),
                  pl.BlockSpec((2, 1, MIX_W), c3),
                  pl.BlockSpec((MIX_W, MIX_W), c2)],
        out_specs=[pl.BlockSpec((1, cb, MIX_W, MIX_W), lambda b, t: (b, t, 0, 0)),
                   pl.BlockSpec((1, cb, MIX_W, MIX_W), lambda b, t: (b, _bwd_chunk(t, ncb, nblk), 0, 0))],
        out_shape=[st_shape, st_shape],
        scratch_shapes=[pltpu.VMEM((MIX_W, MIX_W), F32), pltpu.VMEM((MIX_W, MIX_W), F32)],
        compiler_params=_params("arbitrary", "arbitrary"),
        name="ret_state",
    )(p, p, cos, sin, cos, sin, perm, kd, cd, bd)
    blk = lambda b, t: (b, t, 0)
    y_shape = jax.ShapeDtypeStruct((nb, nt, MIX_W), BF16)
    cb = _pick(nc, (6, 3, 2, 1))
    return pl.pallas_call(
        functools.partial(_mix_out_kernel, cb=cb),
        grid=(nb, nc // cb),
        in_specs=[pl.BlockSpec((1, cb * CHUNK, RET_COLS), blk),
                  pl.BlockSpec((1, cb * CHUNK, SG_COLS), lambda b, t: (b, t, P_SG // SG_COLS)),
                  pl.BlockSpec((cb * CHUNK, MIX_W), tab_f), pl.BlockSpec((cb * CHUNK, MIX_W), tab_f),
                  pl.BlockSpec((1, cb, MIX_W, MIX_W), lambda b, t: (b, t, 0, 0)),
                  pl.BlockSpec((1, cb, MIX_W, MIX_W), lambda b, t: (b, t, 0, 0)),
                  pl.BlockSpec((MIX_W, MIX_W), c2),
                  pl.BlockSpec((CHUNK, N_HEADS * CHUNK), c2),
                  pl.BlockSpec((2, CHUNK, MIX_W), c3),
                  pl.BlockSpec((MIX_W, MIX_W), c2),
                  pl.BlockSpec((1, MIX_W), c2),
                  pl.BlockSpec((CHUNK, N_HEADS * CHUNK), c2),
                  pl.BlockSpec((CHUNK, MIX_W), c2)],
        out_specs=[pl.BlockSpec((1, cb * CHUNK, MIX_W), blk), pl.BlockSpec((1, cb * CHUNK, MIX_W), blk)],
        out_shape=[y_shape, y_shape],
        compiler_params=_params("arbitrary", "arbitrary"),
        name="mix_out",
    )(p, p, cos, sin, st_f, st_b, perm, dm, qd, ones_bd, sg_ng, sg_w, sg_bias)


def _softplus(a):
    return jnp.maximum(a, 0.0) + jnp.log1p(jnp.exp(-jnp.abs(a)))


def _dn_prep_kernel(pc_ref, pp_ref, pn_ref, ab_ref, abc_ref, cw_ref, na_ref, dtb_ref, nar_ref, dtbr_ref, ones_ref,
                    qkv_ref, gb_ref, gbc_ref, xe_ref, *, rows, ctx_blocks, n_blocks):
    t = pl.program_id(1)
    w3 = 3 * MIX_W
    prev_ok = jnp.where((t != 0) & (t != ctx_blocks), 1.0, 0.0)
    next_ok = jnp.where((t != ctx_blocks - 1) & (t != n_blocks - 1), 1.0, 0.0)
    tail = pp_ref[0, rows - 16:rows, 0:w3].astype(F32)
    head = pn_ref[0, 0:16, 0:w3].astype(F32)
    xe_ref[0:8, :] = tail[8:16, :] * prev_ok
    xe_ref[8:8 + rows, :] = pc_ref[0, :, 0:w3].astype(F32)
    xe_ref[8 + rows:16 + rows, :] = head[0:8, :] * next_ok
    pad = CONV_W // 2
    y = xe_ref[8 - pad:8 - pad + rows, :] * cw_ref[0:1, :]
    for i in range(1, CONV_W):
        y = y + xe_ref[8 - pad + i:8 - pad + i + rows, :] * cw_ref[i:i + 1, :]
    y = y * jax.nn.sigmoid(y)
    q = y[:, 0:MIX_W]
    k = y[:, MIX_W:2 * MIX_W]
    v = y[:, 2 * MIX_W:w3]
    ones_bd = ones_ref[...]
    qn = q * lax.rsqrt(_head_sum(q * q, ones_bd) + EPS) * (HEAD_DIM ** -0.5)
    kn = k * lax.rsqrt(_head_sum(k * k, ones_bd) + EPS)
    qkv_ref[0, :, 0:MIX_W] = qn.astype(BF16)
    qkv_ref[0, :, MIX_W:2 * MIX_W] = kn.astype(BF16)
    qkv_ref[0, :, 2 * MIX_W:w3] = v.astype(BF16)
    ab = ab_ref[0]
    gb_ref[0, 0:8, :] = na_ref[...] * _softplus(ab[0:8, :] + dtb_ref[...])
    gb_ref[0, 8:16, :] = jax.nn.sigmoid(ab[8:16, :])
    abc = abc_ref[0]
    lane = lax.broadcasted_iota(jnp.int32, (1, 128), 1)
    g_c = nar_ref[...] * _softplus(abc + dtbr_ref[...])
    gbc_ref[0] = jnp.where(lane < 8, g_c, jnp.where(lane < 16, jax.nn.sigmoid(abc), 0.0))


def _split3(x):
    hi = x.astype(BF16)
    r = x - hi.astype(F32)
    mid = r.astype(BF16)
    lo = (r - mid.astype(F32)).astype(BF16)
    return hi, mid, lo


def _tri_inverse(mats, ii, jj):
    eye = jnp.where(ii == jj, 1.0, 0.0)
    nd = [jnp.where((ii // 16) == (jj // 16), n, 0.0) for n in mats]
    p1 = [_mm(x, x) for x in nd]
    m = [eye - x for x in nd]
    p2 = [_mm(x, x) for x in p1]
    m = [x + _mm(x, y) for x, y in zip(m, p1)]
    p3 = [_mm(x, x) for x in p2]
    m = [x + _mm(x, y) for x, y in zip(m, p2)]
    m = [x + _mm(x, y) for x, y in zip(m, p3)]
    for lvl in (16, 32, 64):
        off_mask = ((ii // (2 * lvl)) == (jj // (2 * lvl))) & ((ii // lvl) != (jj // lvl))
        t = [_mm(jnp.where(off_mask, n, 0.0), x) for n, x in zip(mats, m)]
        m = [x - _mm(x, y) for x, y in zip(m, t)]
    return m


def _dn_pre(qkv, g, gbc, d, lower):
    c = CHUNK
    qn = qkv[:, 0:MIX_W]
    kn = qkv[:, MIX_W:2 * MIX_W]
    v = qkv[:, 2 * MIX_W:3 * MIX_W]
    ii = lax.broadcasted_iota(jnp.int32, (c, c), 0)
    jj = lax.broadcasted_iota(jnp.int32, (c, c), 1)
    incl = (ii >= jj) if lower else (ii <= jj)
    tri = jnp.where(incl, 1.0, 0.0).astype(BF16)
    g_row = sum(_dot_nt(part, tri) for part in _split3(g))[N_HEADS * d:N_HEADS * (d + 1), :]
    cum = sum(_dot(tri, part) for part in _split3(gbc))
    g_col = cum[:, N_HEADS * d:N_HEADS * (d + 1)]
    b_col = gbc[:, 2 * N_HEADS + N_HEADS * d:2 * N_HEADS + N_HEADS * (d + 1)]
    g_cols4 = jnp.concatenate([jnp.broadcast_to(g_col[:, h:h + 1], (c, c)) for h in range(N_HEADS)], axis=1)
    b_cols4 = jnp.concatenate([jnp.broadcast_to(b_col[:, h:h + 1], (c, c)) for h in range(N_HEADS)], axis=1)
    g_rows4 = jnp.concatenate([g_row[h:h + 1, :] for h in range(N_HEADS)], axis=1)
    incl4 = jnp.concatenate([incl] * N_HEADS, axis=1)
    diag4 = jnp.concatenate([ii == jj] * N_HEADS, axis=1)
    decay = jnp.where(incl4, jnp.exp(jnp.where(incl4, g_cols4 - g_rows4, 0.0)), 0.0)
    kstack = _stack_heads(kn)
    kk = _dot_nt(kn, kstack)
    qk = _dot_nt(qn, kstack)
    n_mat = jnp.where(diag4, 0.0, decay * kk * b_cols4)
    attn = (decay * qk).astype(BF16)
    g256 = _expand_heads(g_col)
    eg256 = jnp.exp(g256)
    b256 = _expand_heads(b_col)
    vb = v.astype(F32) * b256
    kbg = kn.astype(F32) * b256 * eg256
    rhs = jnp.concatenate([_stack_heads(vb), _stack_heads(kbg)], axis=1)
    g_last = g256[c - 1:c, :] if lower else g256[0:1, :]
    kdec = (kn.astype(F32) * jnp.exp(g_last - g256)).astype(BF16)
    n_heads = [n_mat[:, h * c:(h + 1) * c] for h in range(N_HEADS)]
    return n_heads, dict(qn=qn, attn=attn, rhs=rhs, eg=eg256, kdec=kdec, sdec=jnp.exp(g_last))


def _dn_post(z, s_prev, bd):
    s_bf = s_prev.astype(BF16)
    w = z["u"] - _dot(z["wk"], s_bf)
    o = z["eg"] * _dot(z["qn"], s_bf) + _dot(z["attn"], _stack_heads(w))
    s_next = z["sdec"] * s_prev + bd * _dot_tn(z["kdec"], w.astype(BF16))
    return o, s_next


def _dn_scan_kernel(qf_ref, qb_ref, gf_ref, gb_ref, gcf_ref, gcb_ref, bd_ref, of_ref, ob_ref, st_f, st_b, *, cb):
    t = pl.program_id(1)

    @pl.when(t == 0)
    def _():
        st_f[...] = jnp.zeros_like(st_f)
        st_b[...] = jnp.zeros_like(st_b)

    bd = bd_ref[...]
    rows = [slice(i * CHUNK, (i + 1) * CHUNK) for i in range(cb)]
    mats, pres = [], []
    for d, (q_ref, g_ref, gc_ref) in enumerate(((qf_ref, gf_ref, gcf_ref), (qb_ref, gb_ref, gcb_ref))):
        for r in rows:
            n_heads, pre = _dn_pre(q_ref[0, r, :], g_ref[0, :, r], gc_ref[0, r, :], d, d == 0)
            mats += n_heads
            pres.append(pre)
    ii = lax.broadcasted_iota(jnp.int32, (CHUNK, CHUNK), 0)
    jj = lax.broadcasted_iota(jnp.int32, (CHUNK, CHUNK), 1)
    inv = _tri_inverse(mats, ii, jj)
    for n, pre in enumerate(pres):
        a_inv = jnp.concatenate(inv[N_HEADS * n:N_HEADS * (n + 1)], axis=1).astype(BF16)
        uw = _dot(a_inv, pre["rhs"])
        pre["u"] = uw[:, 0:MIX_W]
        pre["wk"] = uw[:, MIX_W:2 * MIX_W].astype(BF16)
    s_f, s_b = st_f[...], st_b[...]
    for k in range(cb):
        o, s_f = _dn_post(pres[k], s_f, bd)
        of_ref[0, rows[k], :] = o
        o, s_b = _dn_post(pres[cb + cb - 1 - k], s_b, bd)
        ob_ref[0, rows[cb - 1 - k], :] = o
    st_f[...] = s_f
    st_b[...] = s_b


def _deltanet(p, ab_t, ab_c, conv_w, neg_a, dtb, bd, ones_bd, ncc):
    nb, nt, _ = p.shape
    nc = nt // CHUNK
    w3 = 3 * MIX_W
    c2 = lambda b, t: (0, 0)
    dn_blk = P_DN // RET_COLS
    pad_lanes = lambda col: jnp.concatenate([col.reshape(1, -1), jnp.zeros((1, 128 - col.size), F32)], axis=1)
    pr = math.gcd(math.gcd(ncc * CHUNK, nt), 256)
    n_pb, ctx_pb = nt // pr, ncc * CHUNK // pr
    qkv, gbeta, gbeta_c = pl.pallas_call(
        functools.partial(_dn_prep_kernel, rows=pr, ctx_blocks=ctx_pb, n_blocks=n_pb),
        grid=(nb, n_pb),
        in_specs=[pl.BlockSpec((1, pr, 4 * MIX_W), lambda b, t: (b, t, dn_blk)),
                  pl.BlockSpec((1, pr, 4 * MIX_W), lambda b, t: (b, jnp.maximum(t - 1, 0), dn_blk)),
                  pl.BlockSpec((1, pr, 4 * MIX_W), lambda b, t: (b, jnp.minimum(t + 1, n_pb - 1), dn_blk)),
                  pl.BlockSpec((1, 16, pr), lambda b, t: (b, 0, t)),
                  pl.BlockSpec((1, pr, 128), lambda b, t: (b, t, 0)),
                  pl.BlockSpec((8, w3), c2),
                  pl.BlockSpec((8, 1), c2),
                  pl.BlockSpec((8, 1), c2),
                  pl.BlockSpec((1, 128), c2),
                  pl.BlockSpec((1, 128), c2),
                  pl.BlockSpec((MIX_W, MIX_W), c2)],
        out_specs=[pl.BlockSpec((1, pr, w3), lambda b, t: (b, t, 0)),
                   pl.BlockSpec((1, 16, pr), lambda b, t: (b, 0, t)),
                   pl.BlockSpec((1, pr, 128), lambda b, t: (b, t, 0))],
        out_shape=[jax.ShapeDtypeStruct((nb, nt, w3), BF16),
                   jax.ShapeDtypeStruct((nb, 16, nt), F32),
                   jax.ShapeDtypeStruct((nb, nt, 128), F32)],
        scratch_shapes=[pltpu.VMEM((pr + 16, w3), F32)],
        compiler_params=_params("arbitrary", "arbitrary"),
        name="dn_prep",
    )(p, p, p, ab_t, ab_c, conv_w, neg_a, dtb, pad_lanes(neg_a), pad_lanes(dtb), ones_bd)
    cb = 2
    assert nc % cb == 0 and ncc % cb == 0
    rows = cb * CHUNK
    cur_b = lambda t: _bwd_chunk(t, ncc // cb, nc // cb)
    o_shape = jax.ShapeDtypeStruct((nb, nt, MIX_W), F32)
    return pl.pallas_call(
        functools.partial(_dn_scan_kernel, cb=cb),
        grid=(nb, nc // cb),
        in_specs=[pl.BlockSpec((1, rows, w3), lambda b, t: (b, t, 0)),
                  pl.BlockSpec((1, rows, w3), lambda b, t: (b, cur_b(t), 0)),
                  pl.BlockSpec((1, 16, rows), lambda b, t: (b, 0, t)),
                  pl.BlockSpec((1, 16, rows), lambda b, t: (b, 0, cur_b(t))),
                  pl.BlockSpec((1, rows, 128), lambda b, t: (b, t, 0)),
                  pl.BlockSpec((1, rows, 128), lambda b, t: (b, cur_b(t), 0)),
                  pl.BlockSpec((MIX_W, MIX_W), c2)],
        out_specs=[pl.BlockSpec((1, rows, MIX_W), lambda b, t: (b, t, 0)),
                   pl.BlockSpec((1, rows, MIX_W), lambda b, t: (b, cur_b(t), 0))],
        out_shape=[o_shape, o_shape],
        scratch_shapes=[pltpu.VMEM((MIX_W, MIX_W), F32), pltpu.VMEM((MIX_W, MIX_W), F32)],
        compiler_params=_params("arbitrary", "arbitrary"),
        name="dn_scan",
    )(qkv, qkv, gbeta, gbeta, gbeta_c, gbeta_c, bd)


QK_W = 256


VT_ROWS = 144


def _mla_prep_kernel(p_ref, c_ref, s_ref, perm_ref, qg_ref, kg_ref, wqn_ref, wqr_ref, wa_ref, selq_ref, selc_ref,
                     selr_ref, selv_ref, one_ref, qt_ref, kv_ref, vt_ref, *, scale):
    p = p_ref[0]
    cos, sin, perm = c_ref[...], s_ref[...], perm_ref[...]
    cq = p[:, 0:Q_LORA].astype(F32)
    cqn = (cq * lax.rsqrt(jnp.mean(cq * cq, axis=-1, keepdims=True) + EPS) * qg_ref[...]).astype(BF16)
    q_nope = _dot(cqn, wqn_ref[...]).astype(BF16)
    q_rope = _dot(cqn, wqr_ref[...]).astype(BF16)
    q_rot = (_rot(q_rope, cos, sin, perm) * scale).astype(BF16)
    q_nope_s = (q_nope.astype(F32) * scale).astype(BF16)
    for h in range(N_HEADS):
        qt_ref[0, h] = (_dot_nt(wa_ref[h], q_nope_s) + _dot_nt(selq_ref[h], q_rot)).astype(BF16)
    ckv = p[:, Q_LORA:Q_LORA + KV_LORA].astype(F32)
    ckvn = (ckv * lax.rsqrt(jnp.mean(ckv * ckv, axis=-1, keepdims=True) + EPS) * kg_ref[...]).astype(BF16)
    kr = p[:, Q_LORA + KV_LORA:MLA_PAD]
    kr_rot = _rot(kr, cos, sin, perm).astype(BF16)
    kv_ref[0] = (_dot(ckvn, selc_ref[...]) + _dot(kr_rot, selr_ref[...])).astype(BF16)
    vt_ref[0] = (_dot_nt(selv_ref[...], ckvn) + one_ref[...]).astype(BF16)


def _mla_attn_kernel(qt_ref, kv_ref, vt_ref, wuv_ref, y_ref, m_ref, acc_ref, s_ref, *, tk, n_ctx, nt, latent):
    heads = range(N_HEADS)
    m_ref[...] = jnp.full_like(m_ref, -jnp.inf)
    acc_ref[...] = jnp.zeros_like(acc_ref)

    def scores(j0, size, slot):
        k = kv_ref[0, pl.ds(j0, size), :]
        for h in heads:
            s_ref[slot, h, 0:size, :] = _dot(k, qt_ref[0, h])

    def softmax_pv(j0, size, slot):
        vt = vt_ref[0, :, pl.ds(j0, size)]
        s = [s_ref[slot, h, 0:size, :] for h in heads]
        m_old = [m_ref[h] for h in heads]
        m_new = [jnp.maximum(m_old[h], jnp.max(s[h], axis=0, keepdims=True)) for h in heads]
        pr = [jnp.exp2(s[h] - m_new[h]).astype(BF16) for h in heads]
        pv = [_dot(vt, pr[h]) for h in heads]
        for h in heads:
            acc_ref[h] = jnp.exp2(m_old[h] - m_new[h]) * acc_ref[h] + pv[h]
            m_ref[h] = m_new[h]

    scores(0, n_ctx, 0)
    if not latent:
        softmax_pv(0, n_ctx, 0)
    else:
        n_tiles = (nt - n_ctx) // tk
        last = n_ctx + (n_tiles - 1) * tk
        scores(n_ctx, tk, 1)
        softmax_pv(0, n_ctx, 0)

        def body(jj, carry):
            t0 = pl.multiple_of(n_ctx + 2 * jj * tk, 256)
            t1 = pl.multiple_of(jnp.minimum(t0 + tk, last), 256)
            t2 = pl.multiple_of(jnp.minimum(t0 + 2 * tk, last), 256)
            scores(t1, tk, 0)
            softmax_pv(t0, tk, 1)
            scores(t2, tk, 1)
            softmax_pv(t1, tk, 0)
            return carry

        lax.fori_loop(0, n_tiles // 2, body, 0)
        if n_tiles % 2:
            softmax_pv(last, tk, 1)

    y = None
    for h in range(N_HEADS):
        acc = acc_ref[h]
        o = (acc[0:KV_LORA, :] / acc[KV_LORA:KV_LORA + 1, :]).astype(BF16)
        term = _dot_tn(o, wuv_ref[h])
        y = term if y is None else y + term
    y_ref[0] = y.astype(BF16)


def _mla(p, cos, sin, perm, qg, kg, wqn, wqr, wa, selq, selc, selr, selv, one_col, wuv, n_ctx, ctx_out):
    nb, nt, _ = p.shape
    n_lat = nt - n_ctx
    tm = _pick(nt, (768, 384, 256, 128))
    scale = (NOPE_DIM + ROPE_DIM) ** -0.5 * math.log2(math.e)
    c2 = lambda b, i: (0, 0)
    c3 = lambda b, i: (0, 0, 0)
    qt, kv, vt = pl.pallas_call(
        functools.partial(_mla_prep_kernel, scale=scale),
        grid=(nb, nt // tm),
        in_specs=[pl.BlockSpec((1, tm, MLA_PAD), lambda b, i: (b, i, P_MLA // MLA_PAD)),
                  pl.BlockSpec((tm, 128), lambda b, i: (i, 0)),
                  pl.BlockSpec((tm, 128), lambda b, i: (i, 0)),
                  pl.BlockSpec((128, 128), c2),
                  pl.BlockSpec((1, Q_LORA), c2),
                  pl.BlockSpec((1, KV_LORA), c2),
                  pl.BlockSpec((Q_LORA, N_HEADS * NOPE_DIM), c2),
                  pl.BlockSpec((Q_LORA, N_HEADS * ROPE_DIM), c2),
                  pl.BlockSpec((N_HEADS, QK_W, N_HEADS * NOPE_DIM), c3),
                  pl.BlockSpec((N_HEADS, QK_W, N_HEADS * ROPE_DIM), c3),
                  pl.BlockSpec((KV_LORA, QK_W), c2),
                  pl.BlockSpec((128, QK_W), c2),
                  pl.BlockSpec((VT_ROWS, KV_LORA), c2),
                  pl.BlockSpec((VT_ROWS, 1), c2)],
        out_specs=[pl.BlockSpec((1, N_HEADS, QK_W, tm), lambda b, i: (b, 0, 0, i)),
                   pl.BlockSpec((1, tm, QK_W), lambda b, i: (b, i, 0)),
                   pl.BlockSpec((1, VT_ROWS, tm), lambda b, i: (b, 0, i))],
        out_shape=[jax.ShapeDtypeStruct((nb, N_HEADS, QK_W, nt), BF16),
                   jax.ShapeDtypeStruct((nb, nt, QK_W), BF16),
                   jax.ShapeDtypeStruct((nb, VT_ROWS, nt), BF16)],
        compiler_params=_params("arbitrary", "arbitrary"),
        name="mla_prep",
    )(p, cos, sin, perm, qg, kg, wqn, wqr, wa, selq, selc, selr, selv, one_col)
    tk = _pick(n_lat, (512, 256))

    def attend(tq, first_col, n_q, latent):
        if first_col % tq == 0:
            q_spec = pl.BlockSpec((1, N_HEADS, QK_W, tq), lambda b, i: (b, 0, 0, i + first_col // tq))
        else:
            q_spec = pl.BlockSpec((pl.Element(1), pl.Element(N_HEADS), pl.Element(QK_W), pl.Element(tq)),
                                  lambda b, i: (b, 0, 0, pl.multiple_of(first_col + i * tq, 128)))
        return pl.pallas_call(
            functools.partial(_mla_attn_kernel, tk=tk, n_ctx=n_ctx, nt=nt, latent=latent),
            grid=(nb, n_q // tq),
            in_specs=[q_spec,
                      pl.BlockSpec((1, nt, QK_W), lambda b, i: (b, 0, 0)),
                      pl.BlockSpec((1, VT_ROWS, nt), lambda b, i: (b, 0, 0)),
                      pl.BlockSpec((N_HEADS, KV_LORA, MIX_W), c3)],
            out_specs=pl.BlockSpec((1, tq, MIX_W), lambda b, i: (b, i, 0)),
            out_shape=jax.ShapeDtypeStruct((nb, n_q, MIX_W), BF16),
            scratch_shapes=[pltpu.VMEM((N_HEADS, 1, tq), F32), pltpu.VMEM((N_HEADS, VT_ROWS, tq), F32),
                            pltpu.VMEM((2, N_HEADS, (max(tk, n_ctx) if latent else n_ctx) + 8, tq), F32)],
            compiler_params=_params("arbitrary", "arbitrary"),
            name="mla_attn" if latent else "mla_attn_ctx",
        )(qt, kv, vt, wuv)

    y_lat = attend(_pick(n_lat, (512, 256)), n_ctx, n_lat, True)
    y_ctx = attend(_pick(n_ctx, (256, 128)), 0, n_ctx, False) if ctx_out else None
    return y_lat, y_ctx


def _merge_kernel(x_ref, ctx_ref, yr_ref, ys_ref, of_ref, ob_ref, ym_ref, z_ref, g0_ref, g1_ref, g2_ref, g3_ref,
                  wb_ref, wo_ref, ng_ref, gp_ref, ml_ref, mc_ref, ones_ref, o_ref, *, tm, n_ctx, row0, split):
    i = pl.program_id(1)
    if split:
        first = jnp.concatenate([ctx_ref[0], x_ref[0, 0:tm - n_ctx, :]], axis=0)
        x_res = jnp.where(i == 0, first, x_ref[0])
    else:
        x_res = x_ref[0]
    od = of_ref[0] + ob_ref[0]
    ms = _head_sum(od * od, ones_ref[...]) * (1.0 / HEAD_DIM)
    z = z_ref[0].astype(F32)
    ydn = (od * lax.rsqrt(ms + EPS) * ng_ref[...]) * (z * jax.nn.sigmoid(z))
    ys = (yr_ref[0], ys_ref[0], ydn.astype(BF16), ym_ref[0])
    gates = (g0_ref, g1_ref, g2_ref, g3_ref)
    acc = None
    for b in range(N_BRANCH):
        term = jax.nn.sigmoid(gates[b][0].astype(F32)) * _dot(ys[b], wb_ref[b])
        acc = term if acc is None else acc + term
    y = _dot(acc.astype(BF16), wo_ref[...])
    r = y * lax.rsqrt(jnp.mean(y * y, axis=-1, keepdims=True) + EPS) * gp_ref[...]
    rows = lax.broadcasted_iota(jnp.int32, (tm, 1), 0) + (row0 + i * tm)
    gate = jnp.where(rows < n_ctx, mc_ref[0, 2:3, :], ml_ref[0, 2:3, :])
    o_ref[0] = x_res + gate * r


def _merge(tokens, y_ret, y_sg, o_f, o_b, y_mla, p, wb, wo, ng, gp, mod, ones_bd, n_ctx, row0):
    nb, nt, _ = p.shape
    d = D_MODEL
    n_rows = nt - row0
    tm = _pick(n_rows, (768, 512, 384, 256, 128))
    c2 = lambda b, i: (0, 0)
    if row0 == 0:
        def window(width, col):
            return pl.BlockSpec((1, tm, width), lambda b, i: (b, i, col // width))

        (xa, ctx), (x_spec, ctx_spec), split = _token_sources(tokens, n_ctx, tm)
    else:
        def window(width, col):
            return pl.BlockSpec((pl.Element(1), pl.Element(tm), pl.Element(width)),
                                lambda b, i: (b, pl.multiple_of(row0 + i * tm, 128), col))

        xa, x_spec, split = tokens, window(d, 0), False
        ctx, ctx_spec = jnp.zeros((nb, 8, d), F32), pl.BlockSpec((1, 8, d), lambda b, i: (b, 0, 0))
    y_spec = window(MIX_W, 0)
    assert y_mla.shape[1] == n_rows
    mla_spec = pl.BlockSpec((1, tm, MIX_W), lambda b, i: (b, i, 0))
    gate_specs = [window(d, P_GATE + k * d) for k in range(N_BRANCH)]
    return pl.pallas_call(
        functools.partial(_merge_kernel, tm=tm, n_ctx=n_ctx, row0=row0, split=split),
        grid=(nb, n_rows // tm),
        in_specs=[x_spec, ctx_spec, y_spec, y_spec, y_spec, y_spec, mla_spec,
                  window(MIX_W, P_DN + 3 * MIX_W),
                  *gate_specs,
                  pl.BlockSpec((N_BRANCH, MIX_W, d), lambda b, i: (0, 0, 0)),
                  pl.BlockSpec((d, d), c2),
                  pl.BlockSpec((1, MIX_W), c2),
                  pl.BlockSpec((1, d), c2),
                  pl.BlockSpec((1, 6, d), lambda b, i: (b, 0, 0)),
                  pl.BlockSpec((1, 6, d), lambda b, i: (nb, 0, 0)),
                  pl.BlockSpec((MIX_W, MIX_W), c2)],
        out_specs=pl.BlockSpec((1, tm, d), lambda b, i: (b, i, 0)),
        out_shape=jax.ShapeDtypeStruct((nb, n_rows, d), F32),
        compiler_params=_params("arbitrary", "arbitrary"),
        name="merge",
    )(xa, ctx, y_ret, y_sg, o_f, o_b, y_mla, p, p, p, p, p, wb, wo, ng, gp, mod, mod, ones_bd)


def _route(sel, aff):
    rows = [sel[e:e + 1, :] for e in range(N_EXPERTS)]
    pairs = [(a, b) for a in range(EXPERTS_PER_GROUP) for b in range(a + 1, EXPERTS_PER_GROUP)]
    grp_score, grp_pair = [], []
    for g in range(N_GROUPS):
        base = g * EXPERTS_PER_GROUP
        best = rows[base + pairs[0][0]] + rows[base + pairs[0][1]]
        best_p = jnp.zeros_like(best, dtype=jnp.int32)
        for pi in range(1, len(pairs)):
            s = rows[base + pairs[pi][0]] + rows[base + pairs[pi][1]]
            take = s > best
            best = jnp.where(take, s, best)
            best_p = jnp.where(take, pi, best_p)
        grp_score.append(best)
        grp_pair.append(best_p)
    top = grp_score[0]
    top_g = jnp.zeros_like(grp_pair[0])
    top_p = grp_pair[0]
    for g in range(1, N_GROUPS):
        take = grp_score[g] > top
        top = jnp.where(take, grp_score[g], top)
        top_g = jnp.where(take, g, top_g)
        top_p = jnp.where(take, grp_pair[g], top_p)
    picked = []
    for e in range(N_EXPERTS):
        g, k = divmod(e, EXPERTS_PER_GROUP)
        in_pair = None
        for pi, (a, b) in enumerate(pairs):
            if k in (a, b):
                hit = top_p == pi
                in_pair = hit if in_pair is None else (in_pair | hit)
        picked.append(jnp.where((top_g == g) & in_pair, aff[e:e + 1, :], 0.0))
    denom = picked[0]
    for e in range(1, N_EXPERTS):
        denom = denom + picked[e]
    return [pk / denom for pk in picked]


def _swiglu(hn, w1, w3, w2, scale):
    a = _dot(hn, w1)
    h = (a * jax.nn.sigmoid(a)) * _dot(hn, w3)
    if scale is not None:
        h = h * scale
    return _dot(h.astype(BF16), w2)


def _moe_kernel(x_ref, ml_ref, mc_ref, g2_ref, gp_ref, rw_ref, rb_ref, ws1_ref, ws3_ref, ws2_ref,
                w1_ref, w3_ref, w2_ref, o_ref, hn_ref, comb_t_ref, comb_ref, acc_ref, *, tm, rb, n_ctx):
    i = pl.program_id(1)
    e = pl.program_id(2)

    @pl.when(e == 0)
    def _():
        def blk(r, carry):
            r0 = pl.multiple_of(r * rb, rb)
            x = x_ref[0, pl.ds(r0, rb), :]
            hn = _norm_modulate(x, g2_ref[...], i * tm + r0 < n_ctx, mc_ref, ml_ref, 3, 4)
            hn_ref[pl.ds(r0, rb), :] = hn.astype(BF16)
            return carry

        lax.fori_loop(0, tm // rb, blk, 0)
        hn = hn_ref[...]
        aff = jax.nn.sigmoid(_dot_nt(rw_ref[...], hn))
        comb = _route(aff + rb_ref[...], aff)
        comb_t_ref[...] = jnp.zeros_like(comb_t_ref)
        for k in range(N_EXPERTS):
            comb_t_ref[k:k + 1, :] = comb[k]
        comb_ref[...] = comb_t_ref[...].T
        acc_ref[...] = _swiglu(hn, ws1_ref[0], ws3_ref[0], ws2_ref[0], None)

    @pl.when(e > 0)
    def _():
        lane = lax.broadcasted_iota(jnp.int32, (1, 128), 1)
        comb = comb_ref[...]
        hn = hn_ref[...]
        first = 2 * (e - 1)
        y = None
        for k in range(2):
            c_k = jnp.sum(jnp.where(lane == first + k, comb, 0.0), axis=-1, keepdims=True)
            term = _swiglu(hn, w1_ref[0, k], w3_ref[0, k], w2_ref[0, k], c_k)
            y = term if y is None else y + term
        acc_ref[...] += y

    @pl.when(e == pl.num_programs(2) - 1)
    def _():
        y = acc_ref[...]
        r = y * lax.rsqrt(jnp.mean(y * y, axis=-1, keepdims=True) + EPS) * gp_ref[...]
        rows = lax.broadcasted_iota(jnp.int32, (tm, 1), 0) + i * tm
        gate = jnp.where(rows < n_ctx, mc_ref[0, 5:6, :], ml_ref[0, 5:6, :])
        o_ref[0] = x_ref[0] + gate * r


def _moe(xa, mod, g2, gp, rw_t, rbias, ws1, ws3, ws2, w1, w3, w2, layer, n_ctx):
    nb, nt, d = xa.shape
    tm = _pick(nt, (1024, 768, 512, 384, 256, 128))
    n_pairs = w1.shape[1] // 2
    row = lambda b, i, e: (b, i, 0)
    c2 = lambda b, i, e: (0, 0)
    shared_blk = lambda b, i, e: (layer, 0, 0)
    pair_blk = lambda b, i, e: (layer, jnp.maximum(e - 1, 0), 0, 0)
    return pl.pallas_call(
        functools.partial(_moe_kernel, tm=tm, rb=128, n_ctx=n_ctx),
        grid=(nb, nt // tm, n_pairs + 1),
        in_specs=[pl.BlockSpec((1, tm, d), row),
                  pl.BlockSpec((1, 6, d), lambda b, i, e: (b, 0, 0)),
                  pl.BlockSpec((1, 6, d), lambda b, i, e: (nb, 0, 0)),
                  pl.BlockSpec((1, d), c2),
                  pl.BlockSpec((1, d), c2),
                  pl.BlockSpec((N_EXPERTS, d), c2),
                  pl.BlockSpec((N_EXPERTS, 1), c2),
                  pl.BlockSpec((1, d, D_EXPERT), shared_blk),
                  pl.BlockSpec((1, d, D_EXPERT), shared_blk),
                  pl.BlockSpec((1, D_EXPERT, d), shared_blk),
                  pl.BlockSpec((1, 2, d, D_EXPERT), pair_blk),
                  pl.BlockSpec((1, 2, d, D_EXPERT), pair_blk),
                  pl.BlockSpec((1, 2, D_EXPERT, d), pair_blk)],
        out_specs=pl.BlockSpec((1, tm, d), row),
        out_shape=jax.ShapeDtypeStruct((nb, nt, d), F32),
        scratch_shapes=[pltpu.VMEM((tm, d), BF16), pltpu.VMEM((128, tm), F32), pltpu.VMEM((tm, 128), F32),
                        pltpu.VMEM((tm, d), F32)],
        compiler_params=_params("arbitrary", "arbitrary", "arbitrary"),
        name="moe",
    )(xa, mod, mod, g2, gp, rw_t, rbias, ws1, ws3, ws2, w1, w3, w2)


def _swap_perm(width, group):
    j = np.arange(width)
    src = np.where((j % group) < group // 2, j + group // 2, j - group // 2)
    return jnp.asarray(np.arange(width)[:, None] == src[None, :], BF16)


def _rope_tables(n_lat, n_ctx):
    def angles(pos, dim):
        half = dim // 2
        inv = ROPE_BASE ** (-jnp.arange(half, dtype=F32) / half)
        return pos.astype(F32)[:, None] * inv[None, :]

    def tables(cos_parts, sin_parts, reps):
        cos = jnp.tile(jnp.concatenate(cos_parts, axis=-1), (1, reps))
        sin = jnp.tile(jnp.concatenate(sin_parts, axis=-1), (1, reps))
        w = cos.shape[1]
        return (jnp.concatenate([jnp.ones((n_ctx, w), F32), cos], axis=0),
                jnp.concatenate([jnp.zeros((n_ctx, w), F32), sin], axis=0))

    rows = n_lat // GRID_W
    ang_t = angles(jnp.arange(n_lat), HEAD_DIM)
    ang_r = angles(jnp.repeat(jnp.arange(rows), GRID_W), ROPE_DIM // 2)
    ang_c = angles(jnp.tile(jnp.arange(GRID_W), rows), ROPE_DIM // 2)
    ct, st = jnp.cos(ang_t), jnp.sin(ang_t)
    ret = tables([ct, ct], [-st, st], N_HEADS)
    cr, sr, cc, sc = jnp.cos(ang_r), jnp.sin(ang_r), jnp.cos(ang_c), jnp.sin(ang_c)
    mla = tables([cr, cr, cc, cc], [-sr, sr, -sc, sc], N_HEADS)
    return ret, mla


def _ret_tables(logit):
    log_g = jax.nn.log_sigmoid(logit.astype(F32))
    lane_lg = jnp.repeat(log_g, HEAD_DIM, axis=1)
    idx = jnp.arange(CHUNK, dtype=F32)[:, None]
    kd = jnp.stack([jnp.exp(lane_lg[0][None, :] * (CHUNK - 1 - idx)), jnp.exp(lane_lg[1][None, :] * idx)])
    qd = jnp.stack([jnp.exp(lane_lg[0][None, :] * (idx + 1)), jnp.exp(lane_lg[1][None, :] * (CHUNK - idx))])
    cd = jnp.exp(lane_lg * CHUNK)[:, None, :]
    diff = idx - idx.T
    blocks = []
    for h in range(N_HEADS):
        f = jnp.exp(log_g[0, h] * jnp.where(diff >= 0, diff, 0.0))
        b = jnp.exp(log_g[1, h] * jnp.where(diff < 0, -diff, 0.0))
        blocks.append(jnp.where(diff >= 0, f, b))
    dm = jnp.concatenate(blocks, axis=1)
    return kd, cd, qd, dm


def _pack_w_in(w_in):
    d = w_in.shape[0]
    mla = jnp.concatenate([w_in[:, OFF_MLA:OFF_MLA + MLA_COLS], jnp.zeros((d, MLA_PAD - MLA_COLS), w_in.dtype)], 1)
    w = jnp.concatenate([w_in[:, OFF_RET:OFF_RET + RET_COLS], w_in[:, OFF_DN:OFF_DN + 4 * MIX_W],
                         w_in[:, OFF_SG:OFF_SG + SG_COLS], mla, w_in[:, OFF_GATE:OFF_GATE + GATE_COLS]], axis=1)
    wab = w_in[:, OFF_DN + 4 * MIX_W:OFF_DN + DN_COLS]
    wabc = jnp.concatenate([wab, jnp.zeros((d, 128 - 4 * N_HEADS), w_in.dtype)], axis=1)
    return w.astype(BF16), wab.T.astype(BF16), wabc.astype(BF16)


def _mla_weights(w_uq, w_ukv):
    dq = NOPE_DIM + ROPE_DIM
    dkv = NOPE_DIM + V_DIM
    wq = w_uq.reshape(Q_LORA, N_HEADS, dq)
    wqn = wq[:, :, :NOPE_DIM].reshape(Q_LORA, N_HEADS * NOPE_DIM)
    wqr = wq[:, :, NOPE_DIM:].reshape(Q_LORA, N_HEADS * ROPE_DIM)
    wkv = w_ukv.reshape(KV_LORA, N_HEADS, dkv)
    head_eye = jnp.eye(N_HEADS, dtype=F32)
    wa = jnp.einsum("chd,hg->hcgd", wkv[:, :, :NOPE_DIM], head_eye).reshape(N_HEADS, KV_LORA, N_HEADS * NOPE_DIM)
    wa = jnp.pad(wa, ((0, 0), (0, QK_W - KV_LORA), (0, 0)))
    wuv = jnp.einsum("chd,hg->hcgd", wkv[:, :, NOPE_DIM:], head_eye).reshape(N_HEADS, KV_LORA, MIX_W)
    selq = np.zeros((N_HEADS, QK_W, N_HEADS * ROPE_DIM), np.float32)
    for h in range(N_HEADS):
        selq[h, KV_LORA:KV_LORA + ROPE_DIM, h * ROPE_DIM:(h + 1) * ROPE_DIM] = np.eye(ROPE_DIM)
    selc = np.zeros((KV_LORA, QK_W), np.float32)
    selc[:, 0:KV_LORA] = np.eye(KV_LORA)
    selr = np.zeros((128, QK_W), np.float32)
    selr[0:ROPE_DIM, KV_LORA:KV_LORA + ROPE_DIM] = np.eye(ROPE_DIM)
    selv = np.zeros((VT_ROWS, KV_LORA), np.float32)
    selv[0:KV_LORA, :] = np.eye(KV_LORA)
    one_col = np.zeros((VT_ROWS, 1), np.float32)
    one_col[KV_LORA, 0] = 1.0
    return (tuple(jnp.asarray(a, BF16) for a in (wqn, wqr, wa, selq, selc, selr, selv))
            + (jnp.asarray(one_col), wuv.astype(BF16)))


def kernel(x, c, ctx, c_ctx, w_ada, b_ada, g_pre1, g_post1, g_pre2, g_post2, w_in, ret_decay_logit, sg_norm_g, sg_w, sg_b, dn_conv_w, dn_A_log, dn_dt_bias, dn_norm_g, mla_q_norm_g, mla_kv_norm_g, mla_w_uq, mla_w_ukv, w_branch, w_out, router_w, router_bias, moe_w1, moe_w3, moe_w2, shared_w1, shared_w3, shared_w2):
    nb, n_lat, d = x.shape
    n_ctx = ctx.shape[1]
    depth = w_in.shape[0]
    assert d == D_MODEL and n_lat % GRID_W == 0 and n_lat % CHUNK == 0 and n_ctx % 256 == 0
    ncc = n_ctx // CHUNK

    n_cond = -(-(nb + 1) // 8) * 8
    cond = jnp.concatenate([c, c_ctx[None], jnp.zeros((n_cond - nb - 1, d), F32)], axis=0)
    mod_all = _adaln(cond, w_ada, b_ada).reshape(depth, n_cond, 6, d)

    (ret_cos, ret_sin), (mla_cos, mla_sin) = _rope_tables(n_lat, n_ctx)
    perm_ret = _swap_perm(MIX_W, HEAD_DIM)
    perm_mla = _swap_perm(N_HEADS * ROPE_DIM, ROPE_DIM // 2)
    lane_head = jnp.arange(MIX_W) // HEAD_DIM
    bd = (lane_head[:, None] == lane_head[None, :]).astype(F32)
    ones_bd = bd.astype(BF16)
    rw_t = router_w.T.astype(BF16)
    rbias = router_bias.astype(F32)[:, None]
    expert_w = tuple(w.astype(BF16) for w in (shared_w1, shared_w3, shared_w2, moe_w1, moe_w3, moe_w2))

    xa = (ctx, x)
    for l in range(depth):
        last = l == depth - 1
        if last and isinstance(xa, tuple):
            xa = jnp.concatenate(xa, axis=1)
        mod = mod_all[l]
        w_l, wab_l, wabc_l = _pack_w_in(w_in[l])
        p, ab_t, ab_c = _inproj(xa, n_ctx + n_lat, mod, g_pre1[l][None], w_l, wab_l, wabc_l, n_ctx)

        kd, cd, qd, dm = _ret_tables(ret_decay_logit[l])
        wcat = jnp.concatenate([sg_w[l, h] for h in range(N_HEADS)], axis=1).astype(BF16)
        sg_bias = jnp.repeat(sg_b[l].T, HEAD_DIM, axis=1)
        y_ret, y_sg = _retention_and_sgate(p, ret_cos, ret_sin, perm_ret, (kd, cd, qd, dm, bd, ones_bd),
                                           sg_norm_g[l][None], wcat, sg_bias, ncc)

        neg_a = (-jnp.exp(dn_A_log[l].astype(F32))).reshape(2 * N_HEADS, 1)
        dtb = dn_dt_bias[l].astype(F32).reshape(2 * N_HEADS, 1)
        conv_w = jnp.concatenate([dn_conv_w[l], jnp.zeros((8 - CONV_W, 3 * MIX_W), F32)], axis=0)
        o_f, o_b = _deltanet(p, ab_t, ab_c, conv_w, neg_a, dtb, bd, ones_bd, ncc)

        row0 = n_ctx if last else 0
        y_mla, y_mla_ctx = _mla(p, mla_cos, mla_sin, perm_mla, mla_q_norm_g[l][None], mla_kv_norm_g[l][None],
                                *_mla_weights(mla_w_uq[l], mla_w_ukv[l]), n_ctx, not last)
        if not last:
            y_mla = jnp.concatenate([y_mla_ctx, y_mla], axis=1)

        xa = _merge(xa, y_ret, y_sg, o_f, o_b, y_mla, p, w_branch[l].astype(BF16), w_out[l].astype(BF16),
                    jnp.tile(dn_norm_g[l], N_HEADS)[None], g_post1[l][None], mod, ones_bd, n_ctx, row0)

        xa = _moe(xa, mod, g_pre2[l][None], g_post2[l][None], rw_t, rbias, *expert_w, l, n_ctx - row0)
    return xa
```

```python
import functools
import math

import jax
import jax.numpy as jnp
import numpy as np
from jax import lax
from jax.experimental import pallas as pl
from jax.experimental.pallas import tpu as pltpu

F32 = jnp.float32
BF16 = jnp.bfloat16
HIGHEST = lax.Precision.HIGHEST

D_MODEL = 1024
GRID_W = 64
N_HEADS = 4
HEAD_DIM = 64
MIX_W = N_HEADS * HEAD_DIM
CHUNK = 128
ROPE_BASE = 10000.0
EPS = 1e-6
RET_DECAY_EXP0 = 5.0
CONV_W = 5
Q_LORA = 256
KV_LORA = 128
NOPE_DIM = 64
ROPE_DIM = 32
V_DIM = 64
N_EXPERTS = 16
N_GROUPS = 4
EXPERTS_PER_GROUP = N_EXPERTS // N_GROUPS
D_EXPERT = 256
N_BRANCH = 4

RET_COLS = 4 * MIX_W
SG_COLS = 2 * MIX_W
DN_COLS = 4 * MIX_W + 4 * N_HEADS
MLA_COLS = Q_LORA + KV_LORA + ROPE_DIM
GATE_COLS = N_BRANCH * D_MODEL
OFF_RET = 0
OFF_SG = OFF_RET + RET_COLS
OFF_DN = OFF_SG + SG_COLS
OFF_MLA = OFF_DN + DN_COLS
OFF_GATE = OFF_MLA + MLA_COLS

P_RET = 0
P_DN = 1024
P_SG = 2048
P_MLA = 2560
P_GATE = 3072
P_COLS = 7168
MLA_PAD = 512

VMEM_LIMIT = 56 * 1024 * 1024
LANES = 128
ROW_BLOCK = 128


def _dot(a, b, precision=None):
    return jnp.dot(a, b, preferred_element_type=F32, precision=precision)


def _dot_nt(a, b, precision=None):
    return lax.dot_general(a, b, (((1,), (1,)), ((), ())), preferred_element_type=F32, precision=precision)


def _dot_tn(a, b):
    return lax.dot_general(a, b, (((0,), (0,)), ((), ())), preferred_element_type=F32)


def _mm(a, b):
    return _dot(a.astype(BF16), b.astype(BF16))


def _params(*sem):
    return pltpu.CompilerParams(dimension_semantics=sem, vmem_limit_bytes=VMEM_LIMIT)


def _pick(n, cands):
    for c in cands:
        if n % c == 0:
            return c
    raise ValueError(f"no tile for {n}")


def _head_of_lane(width, group):
    return lax.broadcasted_iota(jnp.int32, (1, width), 1) // group


def _stack_heads(x):
    head = _head_of_lane(MIX_W, HEAD_DIM)
    xf = x.astype(F32)
    return jnp.concatenate([jnp.where(head == h, xf, 0.0).astype(BF16) for h in range(N_HEADS)], axis=0)


def _expand_heads(cols):
    head = _head_of_lane(MIX_W, HEAD_DIM)
    out = cols[:, N_HEADS - 1:N_HEADS]
    for h in range(N_HEADS - 2, -1, -1):
        out = jnp.where(head <= h, cols[:, h:h + 1], out)
    return out


def _head_sum(x, ones_bd):
    hi = x.astype(BF16)
    lo = (x - hi.astype(F32)).astype(BF16)
    return _dot(hi, ones_bd) + _dot(lo, ones_bd)


def _rot(x_bf, cos, sin, perm):
    return x_bf.astype(F32) * cos + _dot(x_bf, perm) * sin


def _norm_modulate(x, g, is_ctx, mc_ref, ml_ref, shift_row, scale_row):
    shift = jnp.where(is_ctx, mc_ref[0, shift_row:shift_row + 1, :], ml_ref[0, shift_row:shift_row + 1, :])
    scale = jnp.where(is_ctx, mc_ref[0, scale_row:scale_row + 1, :], ml_ref[0, scale_row:scale_row + 1, :])
    gain = g * (1.0 + scale)
    return x * lax.rsqrt(jnp.mean(x * x, axis=-1, keepdims=True) + EPS) * gain + shift


def _adaln_kernel(c_ref, w_ref, b_ref, o_ref):
    c = c_ref[...]
    s = c * jax.nn.sigmoid(c)
    o_ref[0] = _dot(s, w_ref[0], precision=HIGHEST) + b_ref[0]


def _adaln(cond, w_ada, b_ada):
    n_l, d, d6 = w_ada.shape
    r = cond.shape[0]
    tn = 1024
    return pl.pallas_call(
        _adaln_kernel,
        grid=(n_l, d6 // tn),
        in_specs=[pl.BlockSpec((r, d), lambda l, j: (0, 0)),
                  pl.BlockSpec((1, d, tn), lambda l, j: (l, 0, j)),
                  pl.BlockSpec((1, 1, tn), lambda l, j: (l, 0, j))],
        out_specs=pl.BlockSpec((1, r, tn), lambda l, j: (l, 0, j)),
        out_shape=jax.ShapeDtypeStruct((n_l, r, d6), F32),
        compiler_params=_params("arbitrary", "arbitrary"),
        name="adaln",
    )(cond, w_ada, b_ada.reshape(n_l, 1, d6))


def _inproj_kernel(x_ref, ctx_ref, ml_ref, mc_ref, g_ref, w_ref, wab_ref, wabc_ref, p_ref, ab_ref, abc_ref, xn_ref,
                   *, tm, rb, n_ctx, split):
    i = pl.program_id(1)
    j = pl.program_id(2)

    @pl.when(j == 0)
    def _():
        def blk(r, carry):
            r0 = pl.multiple_of(r * rb, rb)
            is_ctx = i * tm + r0 < n_ctx
            if split:
                x_off = pl.multiple_of(jnp.maximum(jnp.where(i == 0, r0 - n_ctx, r0), 0), rb)
                c_off = pl.multiple_of(jnp.minimum(r0, n_ctx - rb), rb)
                x = jnp.where(is_ctx, ctx_ref[0, pl.ds(c_off, rb), :], x_ref[0, pl.ds(x_off, rb), :])
            else:
                x = x_ref[0, pl.ds(r0, rb), :]
            hn = _norm_modulate(x, g_ref[...], is_ctx, mc_ref, ml_ref, 0, 1)
            xn_ref[pl.ds(r0, rb), :] = hn.astype(BF16)
            return carry

        lax.fori_loop(0, tm // rb, blk, 0)
        ab_ref[0] = _dot_nt(wab_ref[...], xn_ref[...])
        abc_ref[0] = _dot(xn_ref[...], wabc_ref[...])

    p_ref[0] = _dot(xn_ref[...], w_ref[...]).astype(BF16)


def _token_sources(tokens, n_ctx, tm):
    if isinstance(tokens, tuple):
        ctx, x = tokens
        d = x.shape[-1]
        if x.shape[1] >= tm and tm > n_ctx:
            x_spec = pl.BlockSpec(
                (pl.Element(1), pl.Element(tm), pl.Element(d)),
                lambda b, i, *_: (b, pl.multiple_of(jnp.maximum(i * tm - n_ctx, 0), LANES), 0))
            ctx_spec = pl.BlockSpec((1, n_ctx, d), lambda b, i, *_: (b, 0, 0))
            return (x, ctx), (x_spec, ctx_spec), True
        tokens = jnp.concatenate([ctx, x], axis=1)
    nb, _, d = tokens.shape
    dummy = jnp.zeros((nb, 8, d), tokens.dtype)
    return ((tokens, dummy), (pl.BlockSpec((1, tm, d), lambda b, i, *_: (b, i, 0)),
                              pl.BlockSpec((1, 8, d), lambda b, i, *_: (b, 0, 0))), False)


def _inproj(tokens, nt, mod, g, w, wab, wabc, n_ctx):
    tm = _pick(nt, (1408, 768, 384, 256, 128))
    tn = 1792
    (xa, ctx), (x_spec, ctx_spec), split = _token_sources(tokens, n_ctx, tm)
    nb, _, d = xa.shape
    kern = functools.partial(_inproj_kernel, tm=tm, rb=ROW_BLOCK, n_ctx=n_ctx, split=split)
    return pl.pallas_call(
        kern,
        grid=(nb, nt // tm, P_COLS // tn),
        in_specs=[x_spec, ctx_spec,
                  pl.BlockSpec((1, 6, d), lambda b, i, j: (b, 0, 0)),
                  pl.BlockSpec((1, 6, d), lambda b, i, j: (nb, 0, 0)),
                  pl.BlockSpec((1, d), lambda b, i, j: (0, 0)),
                  pl.BlockSpec((d, tn), lambda b, i, j: (0, j)),
                  pl.BlockSpec((16, d), lambda b, i, j: (0, 0)),
                  pl.BlockSpec((d, LANES), lambda b, i, j: (0, 0))],
        out_specs=[pl.BlockSpec((1, tm, tn), lambda b, i, j: (b, i, j)),
                   pl.BlockSpec((1, 16, tm), lambda b, i, j: (b, 0, i)),
                   pl.BlockSpec((1, tm, LANES), lambda b, i, j: (b, i, 0))],
        out_shape=[jax.ShapeDtypeStruct((nb, nt, P_COLS), BF16),
                   jax.ShapeDtypeStruct((nb, 16, nt), F32),
                   jax.ShapeDtypeStruct((nb, nt, LANES), F32)],
        scratch_shapes=[pltpu.VMEM((tm, d), BF16)],
        compiler_params=_params("arbitrary", "arbitrary", "arbitrary"),
        name="inproj",
    )(xa, ctx, mod, mod, g, w, wab, wabc)


def _bwd_chunk(t, ncc, nc):
    return jnp.where(t < ncc, ncc - 1 - t, nc - 1 - (t - ncc))


def _ret_state_kernel(pf_ref, pb_ref, cf_ref, sf_ref, cb_ref, sb_ref, perm_ref, kd_ref, cd_ref, bd_ref,
                      of_ref, ob_ref, st_f, st_b, *, cb):
    t = pl.program_id(1)

    @pl.when(t == 0)
    def _():
        st_f[...] = jnp.zeros_like(st_f)
        st_b[...] = jnp.zeros_like(st_b)

    def increments(p_ref, c_ref, s_ref, d):
        out = []
        for i in range(cb):
            r = slice(i * CHUNK, (i + 1) * CHUNK)
            kr = _rot(p_ref[0, r, 0:MIX_W], c_ref[r, :], s_ref[r, :], perm_ref[...]) * (HEAD_DIM ** -0.5)
            out.append(bd_ref[...] * _dot_tn((kr * kd_ref[d]).astype(BF16), p_ref[0, r, MIX_W:2 * MIX_W]))
        return out

    inc_f = increments(pf_ref, cf_ref, sf_ref, 0)
    inc_b = increments(pb_ref, cb_ref, sb_ref, 1)
    s = st_f[...]
    for i in range(cb):
        of_ref[0, i] = s.astype(BF16)
        s = cd_ref[0] * s + inc_f[i]
    st_f[...] = s
    s = st_b[...]
    for i in reversed(range(cb)):
        ob_ref[0, i] = s.astype(BF16)
        s = cd_ref[1] * s + inc_b[i]
    st_b[...] = s


def _gelu_tanh(x):
    return 0.5 * x * (1.0 + jnp.tanh(math.sqrt(2.0 / math.pi) * (x + 0.044715 * (x * x * x))))


def _mix_out_kernel(p_ref, pg_ref, c_ref, s_ref, sf_ref, sb_ref, perm_ref, dm_ref, qd_ref, ones_ref,
                    ng_ref, wg_ref, bg_ref, y_ref, ysg_ref, *, cb):
    chunks = range(cb)
    rows = [slice(i * CHUNK, (i + 1) * CHUNK) for i in chunks]
    perm, dm, ones_bd = perm_ref[...], dm_ref[...], ones_ref[...]
    p = [p_ref[0, r, :] for r in rows]
    cos = [c_ref[r, :] for r in rows]
    sin = [s_ref[r, :] for r in rows]
    qr = [_rot(p[i][:, 0:MIX_W], cos[i], sin[i], perm) for i in chunks]
    kr = [_rot(p[i][:, MIX_W:2 * MIX_W], cos[i], sin[i], perm) * (HEAD_DIM ** -0.5) for i in chunks]
    z = [_gelu_tanh(pg_ref[0, r, :].astype(F32)) for r in rows]
    vg = [x[:, MIX_W:] for x in z]
    mu_g = [jnp.mean(x, axis=-1, keepdims=True) for x in vg]
    vgc = [x - m for x, m in zip(vg, mu_g)]
    var_g = [jnp.mean(x * x, axis=-1, keepdims=True) for x in vgc]
    vn = [x * lax.rsqrt(s + EPS) * ng_ref[...] for x, s in zip(vgc, var_g)]
    sc = [_dot_nt(qr[i].astype(BF16), _stack_heads(kr[i])) * dm for i in chunks]
    mixed = [_dot(wg_ref[...], _stack_heads(x)) for x in vn]
    o = [_dot(sc[i].astype(BF16), _stack_heads(p[i][:, 2 * MIX_W:3 * MIX_W])) for i in chunks]
    qs = [jnp.concatenate([(qr[i] * qd_ref[0]).astype(BF16), (qr[i] * qd_ref[1]).astype(BF16)], axis=1)
          for i in chunks]
    ss = [jnp.concatenate([sf_ref[0, i], sb_ref[0, i]], axis=0) for i in chunks]
    o = [o[i] + _dot(qs[i], ss[i]) for i in chunks]
    for i in chunks:
        ysg_ref[0, rows[i], :] = (z[i][:, :MIX_W] * (mixed[i] + bg_ref[...])).astype(BF16)
    mu = [_head_sum(x, ones_bd) * (1.0 / HEAD_DIM) for x in o]
    oc = [x - m for x, m in zip(o, mu)]
    var = [_head_sum(x * x, ones_bd) * (1.0 / HEAD_DIM) for x in oc]
    for i in chunks:
        g = p[i][:, 3 * MIX_W:4 * MIX_W].astype(F32)
        y_ref[0, rows[i], :] = (oc[i] * lax.rsqrt(var[i] + EPS) * (g * jax.nn.sigmoid(g))).astype(BF16)


def _retention_and_sgate(p, cos, sin, perm, tabs, sg_ng, sg_w, sg_bias, ncc):
    nb, nt, _ = p.shape
    nc = nt // CHUNK
    kd, cd, qd, dm, bd, ones_bd = tabs
    cb = 2
    assert nc % cb == 0 and ncc % cb == 0
    nblk, ncb = nc // cb, ncc // cb
    fwd = lambda b, t: (b, t, 0)
    bwd = lambda b, t: (b, _bwd_chunk(t, ncb, nblk), 0)
    tab_f = lambda b, t: (t, 0)
    tab_b = lambda b, t: (_bwd_chunk(t, ncb, nblk), 0)
    c2 = lambda b, t: (0, 0)
    c3 = lambda b, t: (0, 0, 0)
    st_shape = jax.ShapeDtypeStruct((nb, nc, MIX_W, MIX_W), BF16)

    def kv_window(block_of):
        return pl.BlockSpec((pl.Element(1), pl.Element(cb * CHUNK), pl.Element(2 * MIX_W)),
                            lambda b, t: (b, pl.multiple_of(block_of(t) * (cb * CHUNK), LANES), P_RET + MIX_W))

    st_f, st_b = pl.pallas_call(
        functools.partial(_ret_state_kernel, cb=cb),
        grid=(nb, nblk),
        in_specs=[kv_window(lambda t: t), kv_window(lambda t: _bwd_chunk(t, ncb, nblk)),
                  pl.BlockSpec((cb * CHUNK, MIX_W), tab_f), pl.BlockSpec((cb * CHUNK, MIX_W), tab_f),
                  pl.BlockSpec((cb * CHUNK, MIX_W), tab_b), pl.BlockSpec((cb * CHUNK, MIX_W), tab_b),
                  pl.BlockSpec((MIX_W, MIX_W), c2),
                  pl.BlockSpec((2, CHUNK, MIX_W), c3),
                  pl.BlockSpec((2, 1, MIX_W), c3),
                  pl.BlockSpec((MIX_W, MIX_W), c2)],
        out_specs=[pl.BlockSpec((1, cb, MIX_W, MIX_W), lambda b, t: (b, t, 0, 0)),
                   pl.BlockSpec((1, cb, MIX_W, MIX_W), lambda b, t: (b, _bwd_chunk(t, ncb, nblk), 0, 0))],
        out_shape=[st_shape, st_shape],
        scratch_shapes=[pltpu.VMEM((MIX_W, MIX_W), F32), pltpu.VMEM((MIX_W, MIX_W), F32)],
        compiler_params=_params("arbitrary", "arbitrary"),
        name="ret_state",
    )(p, p, cos, sin, cos, sin, perm, kd, cd, bd)
    blk = lambda b, t: (b, t, 0)
    y_shape = jax.ShapeDtypeStruct((nb, nt, MIX_W), BF16)
    cb = _pick(nc, (6, 3, 2, 1))
    return pl.pallas_call(
        functools.partial(_mix_out_kernel, cb=cb),
        grid=(nb, nc // cb),
        in_specs=[pl.BlockSpec((1, cb * CHUNK, RET_COLS), blk),
                  pl.BlockSpec((1, cb * CHUNK, SG_COLS), lambda b, t: (b, t, P_SG // SG_COLS)),
                  pl.BlockSpec((cb * CHUNK, MIX_W), tab_f), pl.BlockSpec((cb * CHUNK, MIX_W), tab_f),
                  pl.BlockSpec((1, cb, MIX_W, MIX_W), lambda b, t: (b, t, 0, 0)),
                  pl.BlockSpec((1, cb, MIX_W, MIX_W), lambda b, t: (b, t, 0, 0)),
                  pl.BlockSpec((MIX_W, MIX_W), c2),
                  pl.BlockSpec((CHUNK, N_HEADS * CHUNK), c2),
                  pl.BlockSpec((2, CHUNK, MIX_W), c3),
                  pl.BlockSpec((MIX_W, MIX_W), c2),
                  pl.BlockSpec((1, MIX_W), c2),
                  pl.BlockSpec((CHUNK, N_HEADS * CHUNK), c2),
                  pl.BlockSpec((CHUNK, MIX_W), c2)],
        out_specs=[pl.BlockSpec((1, cb * CHUNK, MIX_W), blk), pl.BlockSpec((1, cb * CHUNK, MIX_W), blk)],
        out_shape=[y_shape, y_shape],
        compiler_params=_params("arbitrary", "arbitrary"),
        name="mix_out",
    )(p, p, cos, sin, st_f, st_b, perm, dm, qd, ones_bd, sg_ng, sg_w, sg_bias)


def _softplus(a):
    return jnp.maximum(a, 0.0) + jnp.log1p(jnp.exp(-jnp.abs(a)))


def _dn_prep_kernel(pc_ref, pp_ref, pn_ref, ab_ref, abc_ref, cw_ref, na_ref, dtb_ref, nar_ref, dtbr_ref, ones_ref,
                    qkv_ref, gb_ref, gbc_ref, xe_ref, *, rows, ctx_blocks, n_blocks):
    t = pl.program_id(1)
    w3 = 3 * MIX_W
    prev_ok = jnp.where((t != 0) & (t != ctx_blocks), 1.0, 0.0)
    next_ok = jnp.where((t != ctx_blocks - 1) & (t != n_blocks - 1), 1.0, 0.0)
    tail = pp_ref[0, rows - 16:rows, 0:w3].astype(F32)
    head = pn_ref[0, 0:16, 0:w3].astype(F32)
    xe_ref[0:8, :] = tail[8:16, :] * prev_ok
    xe_ref[8:8 + rows, :] = pc_ref[0, :, 0:w3].astype(F32)
    xe_ref[8 + rows:16 + rows, :] = head[0:8, :] * next_ok
    pad = CONV_W // 2
    y = xe_ref[8 - pad:8 - pad + rows, :] * cw_ref[0:1, :]
    for i in range(1, CONV_W):
        y = y + xe_ref[8 - pad + i:8 - pad + i + rows, :] * cw_ref[i:i + 1, :]
    y = y * jax.nn.sigmoid(y)
    q = y[:, 0:MIX_W]
    k = y[:, MIX_W:2 * MIX_W]
    v = y[:, 2 * MIX_W:w3]
    ones_bd = ones_ref[...]
    qn = q * lax.rsqrt(_head_sum(q * q, ones_bd) + EPS) * (HEAD_DIM ** -0.5)
    kn = k * lax.rsqrt(_head_sum(k * k, ones_bd) + EPS)
    qkv_ref[0, :, 0:MIX_W] = qn.astype(BF16)
    qkv_ref[0, :, MIX_W:2 * MIX_W] = kn.astype(BF16)
    qkv_ref[0, :, 2 * MIX_W:w3] = v.astype(BF16)
    ab = ab_ref[0]
    gb_ref[0, 0:8, :] = na_ref[...] * _softplus(ab[0:8, :] + dtb_ref[...])
    gb_ref[0, 8:16, :] = jax.nn.sigmoid(ab[8:16, :])
    abc = abc_ref[0]
    lane = lax.broadcasted_iota(jnp.int32, (1, LANES), 1)
    g_c = nar_ref[...] * _softplus(abc + dtbr_ref[...])
    gbc_ref[0] = jnp.where(lane < 8, g_c, jnp.where(lane < 16, jax.nn.sigmoid(abc), 0.0))


def _split3(x):
    hi = x.astype(BF16)
    r = x - hi.astype(F32)
    mid = r.astype(BF16)
    lo = (r - mid.astype(F32)).astype(BF16)
    return hi, mid, lo


def _tri_inverse(mats, ii, jj):
    eye = jnp.where(ii == jj, 1.0, 0.0)
    nd = [jnp.where((ii // 16) == (jj // 16), n, 0.0) for n in mats]
    p1 = [_mm(x, x) for x in nd]
    m = [eye - x for x in nd]
    p2 = [_mm(x, x) for x in p1]
    m = [x + _mm(x, y) for x, y in zip(m, p1)]
    p3 = [_mm(x, x) for x in p2]
    m = [x + _mm(x, y) for x, y in zip(m, p2)]
    m = [x + _mm(x, y) for x, y in zip(m, p3)]
    for lvl in (16, 32, 64):
        off_mask = ((ii // (2 * lvl)) == (jj // (2 * lvl))) & ((ii // lvl) != (jj // lvl))
        t = [_mm(jnp.where(off_mask, n, 0.0), x) for n, x in zip(mats, m)]
        m = [x - _mm(x, y) for x, y in zip(m, t)]
    return m


def _dn_pre(qkv, g, gbc, d, lower):
    c = CHUNK
    qn = qkv[:, 0:MIX_W]
    kn = qkv[:, MIX_W:2 * MIX_W]
    v = qkv[:, 2 * MIX_W:3 * MIX_W]
    ii = lax.broadcasted_iota(jnp.int32, (c, c), 0)
    jj = lax.broadcasted_iota(jnp.int32, (c, c), 1)
    incl = (ii >= jj) if lower else (ii <= jj)
    tri = jnp.where(incl, 1.0, 0.0).astype(BF16)
    g_row = sum(_dot_nt(part, tri) for part in _split3(g))[N_HEADS * d:N_HEADS * (d + 1), :]
    cum = sum(_dot(tri, part) for part in _split3(gbc))
    g_col = cum[:, N_HEADS * d:N_HEADS * (d + 1)]
    b_col = gbc[:, 2 * N_HEADS + N_HEADS * d:2 * N_HEADS + N_HEADS * (d + 1)]
    g_cols4 = jnp.concatenate([jnp.broadcast_to(g_col[:, h:h + 1], (c, c)) for h in range(N_HEADS)], axis=1)
    b_cols4 = jnp.concatenate([jnp.broadcast_to(b_col[:, h:h + 1], (c, c)) for h in range(N_HEADS)], axis=1)
    g_rows4 = jnp.concatenate([g_row[h:h + 1, :] for h in range(N_HEADS)], axis=1)
    incl4 = jnp.concatenate([incl] * N_HEADS, axis=1)
    diag4 = jnp.concatenate([ii == jj] * N_HEADS, axis=1)
    decay = jnp.where(incl4, jnp.exp(jnp.where(incl4, g_cols4 - g_rows4, 0.0)), 0.0)
    kstack = _stack_heads(kn)
    kk = _dot_nt(kn, kstack)
    qk = _dot_nt(qn, kstack)
    n_mat = jnp.where(diag4, 0.0, decay * kk * b_cols4)
    attn = (decay * qk).astype(BF16)
    g256 = _expand_heads(g_col)
    eg256 = jnp.exp(g256)
    b256 = _expand_heads(b_col)
    vb = v.astype(F32) * b256
    kbg = kn.astype(F32) * b256 * eg256
    rhs = jnp.concatenate([_stack_heads(vb), _stack_heads(kbg)], axis=1)
    g_last = g256[c - 1:c, :] if lower else g256[0:1, :]
    kdec = (kn.astype(F32) * jnp.exp(g_last - g256)).astype(BF16)
    n_heads = [n_mat[:, h * c:(h + 1) * c] for h in range(N_HEADS)]
    return n_heads, dict(qn=qn, attn=attn, rhs=rhs, eg=eg256, kdec=kdec, sdec=jnp.exp(g_last))


def _dn_post(z, s_prev, bd):
    s_bf = s_prev.astype(BF16)
    w = z["u"] - _dot(z["wk"], s_bf)
    o = z["eg"] * _dot(z["qn"], s_bf) + _dot(z["attn"], _stack_heads(w))
    s_next = z["sdec"] * s_prev + bd * _dot_tn(z["kdec"], w.astype(BF16))
    return o, s_next


def _dn_scan_kernel(qf_ref, qb_ref, gf_ref, gb_ref, gcf_ref, gcb_ref, bd_ref, of_ref, ob_ref, st_f, st_b, *, cb):
    t = pl.program_id(1)

    @pl.when(t == 0)
    def _():
        st_f[...] = jnp.zeros_like(st_f)
        st_b[...] = jnp.zeros_like(st_b)

    bd = bd_ref[...]
    rows = [slice(i * CHUNK, (i + 1) * CHUNK) for i in range(cb)]
    mats, pres = [], []
    for d, (q_ref, g_ref, gc_ref) in enumerate(((qf_ref, gf_ref, gcf_ref), (qb_ref, gb_ref, gcb_ref))):
        for r in rows:
            n_heads, pre = _dn_pre(q_ref[0, r, :], g_ref[0, :, r], gc_ref[0, r, :], d, d == 0)
            mats += n_heads
            pres.append(pre)
    ii = lax.broadcasted_iota(jnp.int32, (CHUNK, CHUNK), 0)
    jj = lax.broadcasted_iota(jnp.int32, (CHUNK, CHUNK), 1)
    inv = _tri_inverse(mats, ii, jj)
    for n, pre in enumerate(pres):
        a_inv = jnp.concatenate(inv[N_HEADS * n:N_HEADS * (n + 1)], axis=1).astype(BF16)
        uw = _dot(a_inv, pre["rhs"])
        pre["u"] = uw[:, 0:MIX_W]
        pre["wk"] = uw[:, MIX_W:2 * MIX_W].astype(BF16)
    s_f, s_b = st_f[...], st_b[...]
    for k in range(cb):
        o, s_f = _dn_post(pres[k], s_f, bd)
        of_ref[0, rows[k], :] = o
        o, s_b = _dn_post(pres[cb + cb - 1 - k], s_b, bd)
        ob_ref[0, rows[cb - 1 - k], :] = o
    st_f[...] = s_f
    st_b[...] = s_b


def _deltanet(p, ab_t, ab_c, conv_w, neg_a, dtb, bd, ones_bd, ncc):
    nb, nt, _ = p.shape
    nc = nt // CHUNK
    w3 = 3 * MIX_W
    c2 = lambda b, t: (0, 0)
    dn_blk = P_DN // RET_COLS
    pad_lanes = lambda col: jnp.concatenate([col.reshape(1, -1), jnp.zeros((1, LANES - col.size), F32)], axis=1)
    pr = math.gcd(math.gcd(ncc * CHUNK, nt), 256)
    n_pb, ctx_pb = nt // pr, ncc * CHUNK // pr
    qkv, gbeta, gbeta_c = pl.pallas_call(
        functools.partial(_dn_prep_kernel, rows=pr, ctx_blocks=ctx_pb, n_blocks=n_pb),
        grid=(nb, n_pb),
        in_specs=[pl.BlockSpec((1, pr, 4 * MIX_W), lambda b, t: (b, t, dn_blk)),
                  pl.BlockSpec((1, pr, 4 * MIX_W), lambda b, t: (b, jnp.maximum(t - 1, 0), dn_blk)),
                  pl.BlockSpec((1, pr, 4 * MIX_W), lambda b, t: (b, jnp.minimum(t + 1, n_pb - 1), dn_blk)),
                  pl.BlockSpec((1, 16, pr), lambda b, t: (b, 0, t)),
                  pl.BlockSpec((1, pr, LANES), lambda b, t: (b, t, 0)),
                  pl.BlockSpec((8, w3), c2),
                  pl.BlockSpec((8, 1), c2),
                  pl.BlockSpec((8, 1), c2),
                  pl.BlockSpec((1, LANES), c2),
                  pl.BlockSpec((1, LANES), c2),
                  pl.BlockSpec((MIX_W, MIX_W), c2)],
        out_specs=[pl.BlockSpec((1, pr, w3), lambda b, t: (b, t, 0)),
                   pl.BlockSpec((1, 16, pr), lambda b, t: (b, 0, t)),
                   pl.BlockSpec((1, pr, LANES), lambda b, t: (b, t, 0))],
        out_shape=[jax.ShapeDtypeStruct((nb, nt, w3), BF16),
                   jax.ShapeDtypeStruct((nb, 16, nt), F32),
                   jax.ShapeDtypeStruct((nb, nt, LANES), F32)],
        scratch_shapes=[pltpu.VMEM((pr + 16, w3), F32)],
        compiler_params=_params("arbitrary", "arbitrary"),
        name="dn_prep",
    )(p, p, p, ab_t, ab_c, conv_w, neg_a, dtb, pad_lanes(neg_a), pad_lanes(dtb), ones_bd)
    cb = 2
    assert nc % cb == 0 and ncc % cb == 0
    rows = cb * CHUNK
    cur_b = lambda t: _bwd_chunk(t, ncc // cb, nc // cb)
    o_shape = jax.ShapeDtypeStruct((nb, nt, MIX_W), F32)
    return pl.pallas_call(
        functools.partial(_dn_scan_kernel, cb=cb),
        grid=(nb, nc // cb),
        in_specs=[pl.BlockSpec((1, rows, w3), lambda b, t: (b, t, 0)),
                  pl.BlockSpec((1, rows, w3), lambda b, t: (b, cur_b(t), 0)),
                  pl.BlockSpec((1, 16, rows), lambda b, t: (b, 0, t)),
                  pl.BlockSpec((1, 16, rows), lambda b, t: (b, 0, cur_b(t))),
                  pl.BlockSpec((1, rows, LANES),lambda b, t: (b, t, 0)),
                  pl.BlockSpec((1, rows, LANES),lambda b, t: (b, cur_b(t), 0)),
                  pl.BlockSpec((MIX_W, MIX_W), c2)],
        out_specs=[pl.BlockSpec((1, rows, MIX_W), lambda b, t: (b, t, 0)),
                   pl.BlockSpec((1, rows, MIX_W), lambda b, t: (b, cur_b(t), 0))],
        out_shape=[o_shape, o_shape],
        scratch_shapes=[pltpu.VMEM((MIX_W, MIX_W), F32), pltpu.VMEM((MIX_W, MIX_W), F32)],
        compiler_params=_params("arbitrary", "arbitrary"),
        name="dn_scan",
    )(qkv, qkv, gbeta, gbeta, gbeta_c, gbeta_c, bd)


QK_W = 256


VT_ROWS = 144
ROPE_W = N_HEADS * ROPE_DIM
assert MLA_PAD - Q_LORA - KV_LORA == ROPE_W


def _mla_prep_kernel(p_ref, c_ref, s_ref, perm_ref, qg_ref, kg_ref, wqn_ref, wqr_ref, wa_ref, selq_ref, selc_ref,
                     selr_ref, selv_ref, one_ref, qt_ref, kv_ref, vt_ref, *, scale):
    p = p_ref[0]
    cos, sin, perm = c_ref[...], s_ref[...], perm_ref[...]
    cq = p[:, 0:Q_LORA].astype(F32)
    cqn = (cq * lax.rsqrt(jnp.mean(cq * cq, axis=-1, keepdims=True) + EPS) * qg_ref[...]).astype(BF16)
    q_nope = _dot(cqn, wqn_ref[...]).astype(BF16)
    q_rope = _dot(cqn, wqr_ref[...]).astype(BF16)
    q_rot = (_rot(q_rope, cos, sin, perm) * scale).astype(BF16)
    q_nope_s = (q_nope.astype(F32) * scale).astype(BF16)
    for h in range(N_HEADS):
        qt_ref[0, h] = (_dot_nt(wa_ref[h], q_nope_s) + _dot_nt(selq_ref[h], q_rot)).astype(BF16)
    ckv = p[:, Q_LORA:Q_LORA + KV_LORA].astype(F32)
    ckvn = (ckv * lax.rsqrt(jnp.mean(ckv * ckv, axis=-1, keepdims=True) + EPS) * kg_ref[...]).astype(BF16)
    kr = p[:, Q_LORA + KV_LORA:MLA_PAD]
    kr_rot = _rot(kr, cos, sin, perm).astype(BF16)
    kv_ref[0] = (_dot(ckvn, selc_ref[...]) + _dot(kr_rot, selr_ref[...])).astype(BF16)
    vt_ref[0] = (_dot_nt(selv_ref[...], ckvn) + one_ref[...]).astype(BF16)


def _mla_attn_kernel(qt_ref, kv_ref, vt_ref, wuv_ref, y_ref, m_ref, acc_ref, s_ref, *, tk, n_ctx, nt, latent):
    heads = range(N_HEADS)
    m_ref[...] = jnp.full_like(m_ref, -jnp.inf)
    acc_ref[...] = jnp.zeros_like(acc_ref)

    def scores(j0, size, slot):
        k = kv_ref[0, pl.ds(j0, size), :]
        for h in heads:
            s_ref[slot, h, 0:size, :] = _dot(k, qt_ref[0, h])

    def softmax_pv(j0, size, slot):
        vt = vt_ref[0, :, pl.ds(j0, size)]
        s = [s_ref[slot, h, 0:size, :] for h in heads]
        m_old = [m_ref[h] for h in heads]
        m_new = [jnp.maximum(m_old[h], jnp.max(s[h], axis=0, keepdims=True)) for h in heads]
        pr = [jnp.exp2(s[h] - m_new[h]).astype(BF16) for h in heads]
        pv = [_dot(vt, pr[h]) for h in heads]
        for h in heads:
            acc_ref[h] = jnp.exp2(m_old[h] - m_new[h]) * acc_ref[h] + pv[h]
            m_ref[h] = m_new[h]

    scores(0, n_ctx, 0)
    if not latent:
        softmax_pv(0, n_ctx, 0)
    else:
        n_tiles = (nt - n_ctx) // tk
        last = n_ctx + (n_tiles - 1) * tk
        scores(n_ctx, tk, 1)
        softmax_pv(0, n_ctx, 0)

        def body(jj, carry):
            t0 = pl.multiple_of(n_ctx + 2 * jj * tk, 256)
            t1 = pl.multiple_of(jnp.minimum(t0 + tk, last), 256)
            t2 = pl.multiple_of(jnp.minimum(t0 + 2 * tk, last), 256)
            scores(t1, tk, 0)
            softmax_pv(t0, tk, 1)
            scores(t2, tk, 1)
            softmax_pv(t1, tk, 0)
            return carry

        lax.fori_loop(0, n_tiles // 2, body, 0)
        if n_tiles % 2:
            softmax_pv(last, tk, 1)

    y = None
    for h in range(N_HEADS):
        acc = acc_ref[h]
        o = (acc[0:KV_LORA, :] / acc[KV_LORA:KV_LORA + 1, :]).astype(BF16)
        term = _dot_tn(o, wuv_ref[h])
        y = term if y is None else y + term
    y_ref[0] = y.astype(BF16)


def _mla(p, cos, sin, perm, qg, kg, wqn, wqr, wa, selq, selc, selr, selv, one_col, wuv, n_ctx, ctx_out):
    nb, nt, _ = p.shape
    n_lat = nt - n_ctx
    tm = _pick(nt, (768, 384, 256, 128))
    scale = (NOPE_DIM + ROPE_DIM) ** -0.5 * math.log2(math.e)
    c2 = lambda b, i: (0, 0)
    c3 = lambda b, i: (0, 0, 0)
    qt, kv, vt = pl.pallas_call(
        functools.partial(_mla_prep_kernel, scale=scale),
        grid=(nb, nt // tm),
        in_specs=[pl.BlockSpec((1, tm, MLA_PAD), lambda b, i: (b, i, P_MLA // MLA_PAD)),
                  pl.BlockSpec((tm, ROPE_W), lambda b, i: (i, 0)),
                  pl.BlockSpec((tm, ROPE_W), lambda b, i: (i, 0)),
                  pl.BlockSpec((ROPE_W, ROPE_W), c2),
                  pl.BlockSpec((1, Q_LORA), c2),
                  pl.BlockSpec((1, KV_LORA), c2),
                  pl.BlockSpec((Q_LORA, N_HEADS * NOPE_DIM), c2),
                  pl.BlockSpec((Q_LORA, N_HEADS * ROPE_DIM), c2),
                  pl.BlockSpec((N_HEADS, QK_W, N_HEADS * NOPE_DIM), c3),
                  pl.BlockSpec((N_HEADS, QK_W, N_HEADS * ROPE_DIM), c3),
                  pl.BlockSpec((KV_LORA, QK_W), c2),
                  pl.BlockSpec((ROPE_W, QK_W), c2),
                  pl.BlockSpec((VT_ROWS, KV_LORA), c2),
                  pl.BlockSpec((VT_ROWS, 1), c2)],
        out_specs=[pl.BlockSpec((1, N_HEADS, QK_W, tm), lambda b, i: (b, 0, 0, i)),
                   pl.BlockSpec((1, tm, QK_W), lambda b, i: (b, i, 0)),
                   pl.BlockSpec((1, VT_ROWS, tm), lambda b, i: (b, 0, i))],
        out_shape=[jax.ShapeDtypeStruct((nb, N_HEADS, QK_W, nt), BF16),
                   jax.ShapeDtypeStruct((nb, nt, QK_W), BF16),
                   jax.ShapeDtypeStruct((nb, VT_ROWS, nt), BF16)],
        compiler_params=_params("arbitrary", "arbitrary"),
        name="mla_prep",
    )(p, cos, sin, perm, qg, kg, wqn, wqr, wa, selq, selc, selr, selv, one_col)
    tk = _pick(n_lat, (512, 256))

    def attend(tq, first_col, n_q, latent):
        if first_col % tq == 0:
            q_spec = pl.BlockSpec((1, N_HEADS, QK_W, tq), lambda b, i: (b, 0, 0, i + first_col // tq))
        else:
            q_spec = pl.BlockSpec((pl.Element(1), pl.Element(N_HEADS), pl.Element(QK_W), pl.Element(tq)),
                                  lambda b, i: (b, 0, 0, pl.multiple_of(first_col + i * tq, LANES)))
        return pl.pallas_call(
            functools.partial(_mla_attn_kernel, tk=tk, n_ctx=n_ctx, nt=nt, latent=latent),
            grid=(nb, n_q // tq),
            in_specs=[q_spec,
                      pl.BlockSpec((1, nt, QK_W), lambda b, i: (b, 0, 0)),
                      pl.BlockSpec((1, VT_ROWS, nt), lambda b, i: (b, 0, 0)),
                      pl.BlockSpec((N_HEADS, KV_LORA, MIX_W), c3)],
            out_specs=pl.BlockSpec((1, tq, MIX_W), lambda b, i: (b, i, 0)),
            out_shape=jax.ShapeDtypeStruct((nb, n_q, MIX_W), BF16),
            scratch_shapes=[pltpu.VMEM((N_HEADS, 1, tq), F32), pltpu.VMEM((N_HEADS, VT_ROWS, tq), F32),
                            pltpu.VMEM((2, N_HEADS, max(tk, n_ctx) if latent else n_ctx, tq), F32)],
            compiler_params=_params("arbitrary", "arbitrary"),
            name="mla_attn" if latent else "mla_attn_ctx",
        )(qt, kv, vt, wuv)

    y_lat = attend(_pick(n_lat, (512, 256)), n_ctx, n_lat, True)
    y_ctx = attend(_pick(n_ctx, (256, 128)), 0, n_ctx, False) if ctx_out else None
    return y_lat, y_ctx


def _merge_kernel(x_ref, ctx_ref, yr_ref, ys_ref, of_ref, ob_ref, ym_ref, z_ref, g0_ref, g1_ref, g2_ref, g3_ref,
                  wb_ref, wo_ref, ng_ref, gp_ref, ml_ref, mc_ref, ones_ref, o_ref, *, tm, n_ctx, row0, split):
    i = pl.program_id(1)
    if split:
        first = jnp.concatenate([ctx_ref[0], x_ref[0, 0:tm - n_ctx, :]], axis=0)
        x_res = jnp.where(i == 0, first, x_ref[0])
    else:
        x_res = x_ref[0]
    od = of_ref[0] + ob_ref[0]
    ms = _head_sum(od * od, ones_ref[...]) * (1.0 / HEAD_DIM)
    z = z_ref[0].astype(F32)
    ydn = (od * lax.rsqrt(ms + EPS) * ng_ref[...]) * (z * jax.nn.sigmoid(z))
    ys = (yr_ref[0], ys_ref[0], ydn.astype(BF16), ym_ref[0])
    gates = (g0_ref, g1_ref, g2_ref, g3_ref)
    acc = None
    for b in range(N_BRANCH):
        term = jax.nn.sigmoid(gates[b][0].astype(F32)) * _dot(ys[b], wb_ref[b])
        acc = term if acc is None else acc + term
    y = _dot(acc.astype(BF16), wo_ref[...])
    r = y * lax.rsqrt(jnp.mean(y * y, axis=-1, keepdims=True) + EPS) * gp_ref[...]
    rows = lax.broadcasted_iota(jnp.int32, (tm, 1), 0) + (row0 + i * tm)
    gate = jnp.where(rows < n_ctx, mc_ref[0, 2:3, :], ml_ref[0, 2:3, :])
    o_ref[0] = x_res + gate * r


def _merge(tokens, y_ret, y_sg, o_f, o_b, y_mla, p, wb, wo, ng, gp, mod, ones_bd, n_ctx, row0):
    nb, nt, _ = p.shape
    d = D_MODEL
    n_rows = nt - row0
    tm = _pick(n_rows, (768, 512, 384, 256, 128))
    c2 = lambda b, i: (0, 0)
    if row0 == 0:
        def window(width, col):
            return pl.BlockSpec((1, tm, width), lambda b, i: (b, i, col // width))

        (xa, ctx), (x_spec, ctx_spec), split = _token_sources(tokens, n_ctx, tm)
    else:
        def window(width, col):
            return pl.BlockSpec((pl.Element(1), pl.Element(tm), pl.Element(width)),
                                lambda b, i: (b, pl.multiple_of(row0 + i * tm, LANES), col))

        xa, x_spec, split = tokens, window(d, 0), False
        ctx, ctx_spec = jnp.zeros((nb, 8, d), F32), pl.BlockSpec((1, 8, d), lambda b, i: (b, 0, 0))
    y_spec = window(MIX_W, 0)
    assert y_mla.shape[1] == n_rows
    mla_spec = pl.BlockSpec((1, tm, MIX_W), lambda b, i: (b, i, 0))
    gate_specs = [window(d, P_GATE + k * d) for k in range(N_BRANCH)]
    return pl.pallas_call(
        functools.partial(_merge_kernel, tm=tm, n_ctx=n_ctx, row0=row0, split=split),
        grid=(nb, n_rows // tm),
        in_specs=[x_spec, ctx_spec, y_spec, y_spec, y_spec, y_spec, mla_spec,
                  window(MIX_W, P_DN + 3 * MIX_W),
                  *gate_specs,
                  pl.BlockSpec((N_BRANCH, MIX_W, d), lambda b, i: (0, 0, 0)),
                  pl.BlockSpec((d, d), c2),
                  pl.BlockSpec((1, MIX_W), c2),
                  pl.BlockSpec((1, d), c2),
                  pl.BlockSpec((1, 6, d), lambda b, i: (b, 0, 0)),
                  pl.BlockSpec((1, 6, d), lambda b, i: (nb, 0, 0)),
                  pl.BlockSpec((MIX_W, MIX_W), c2)],
        out_specs=pl.BlockSpec((1, tm, d), lambda b, i: (b, i, 0)),
        out_shape=jax.ShapeDtypeStruct((nb, n_rows, d), F32),
        compiler_params=_params("arbitrary", "arbitrary"),
        name="merge",
    )(xa, ctx, y_ret, y_sg, o_f, o_b, y_mla, p, p, p, p, p, wb, wo, ng, gp, mod, mod, ones_bd)


def _route(sel, aff):
    rows = [sel[e:e + 1, :] for e in range(N_EXPERTS)]
    pairs = [(a, b) for a in range(EXPERTS_PER_GROUP) for b in range(a + 1, EXPERTS_PER_GROUP)]
    grp_score, grp_pair = [], []
    for g in range(N_GROUPS):
        base = g * EXPERTS_PER_GROUP
        best = rows[base + pairs[0][0]] + rows[base + pairs[0][1]]
        best_p = jnp.zeros_like(best, dtype=jnp.int32)
        for pi in range(1, len(pairs)):
            s = rows[base + pairs[pi][0]] + rows[base + pairs[pi][1]]
            take = s > best
            best = jnp.where(take, s, best)
            best_p = jnp.where(take, pi, best_p)
        grp_score.append(best)
        grp_pair.append(best_p)
    top = grp_score[0]
    top_g = jnp.zeros_like(grp_pair[0])
    top_p = grp_pair[0]
    for g in range(1, N_GROUPS):
        take = grp_score[g] > top
        top = jnp.where(take, grp_score[g], top)
        top_g = jnp.where(take, g, top_g)
        top_p = jnp.where(take, grp_pair[g], top_p)
    picked = []
    for e in range(N_EXPERTS):
        g, k = divmod(e, EXPERTS_PER_GROUP)
        in_pair = None
        for pi, (a, b) in enumerate(pairs):
            if k in (a, b):
                hit = top_p == pi
                in_pair = hit if in_pair is None else (in_pair | hit)
        picked.append(jnp.where((top_g == g) & in_pair, aff[e:e + 1, :], 0.0))
    denom = picked[0]
    for e in range(1, N_EXPERTS):
        denom = denom + picked[e]
    return [pk / denom for pk in picked]


def _swiglu(hn, w1, w3, w2, scale):
    a = _dot(hn, w1)
    h = (a * jax.nn.sigmoid(a)) * _dot(hn, w3)
    if scale is not None:
        h = h * scale
    return _dot(h.astype(BF16), w2)


def _moe_kernel(x_ref, ml_ref, mc_ref, g2_ref, gp_ref, rw_ref, rb_ref, ws1_ref, ws3_ref, ws2_ref,
                w1_ref, w3_ref, w2_ref, o_ref, hn_ref, comb_t_ref, comb_ref, acc_ref, *, tm, rb, n_ctx):
    i = pl.program_id(1)
    e = pl.program_id(2)

    @pl.when(e == 0)
    def _():
        def blk(r, carry):
            r0 = pl.multiple_of(r * rb, rb)
            x = x_ref[0, pl.ds(r0, rb), :]
            hn = _norm_modulate(x, g2_ref[...], i * tm + r0 < n_ctx, mc_ref, ml_ref, 3, 4)
            hn_ref[pl.ds(r0, rb), :] = hn.astype(BF16)
            return carry

        lax.fori_loop(0, tm // rb, blk, 0)
        hn = hn_ref[...]
        aff = jax.nn.sigmoid(_dot_nt(rw_ref[...], hn))
        comb = _route(aff + rb_ref[...], aff)
        comb_t_ref[...] = jnp.zeros_like(comb_t_ref)
        for k in range(N_EXPERTS):
            comb_t_ref[k:k + 1, :] = comb[k]
        comb_ref[...] = comb_t_ref[...].T
        acc_ref[...] = _swiglu(hn, ws1_ref[0], ws3_ref[0], ws2_ref[0], None)

    @pl.when(e > 0)
    def _():
        lane = lax.broadcasted_iota(jnp.int32, (1, LANES), 1)
        comb = comb_ref[...]
        hn = hn_ref[...]
        first = 2 * (e - 1)
        y = None
        for k in range(2):
            c_k = jnp.sum(jnp.where(lane == first + k, comb, 0.0), axis=-1, keepdims=True)
            term = _swiglu(hn, w1_ref[0, k], w3_ref[0, k], w2_ref[0, k], c_k)
            y = term if y is None else y + term
        acc_ref[...] += y

    @pl.when(e == pl.num_programs(2) - 1)
    def _():
        y = acc_ref[...]
        r = y * lax.rsqrt(jnp.mean(y * y, axis=-1, keepdims=True) + EPS) * gp_ref[...]
        rows = lax.broadcasted_iota(jnp.int32, (tm, 1), 0) + i * tm
        gate = jnp.where(rows < n_ctx, mc_ref[0, 5:6, :], ml_ref[0, 5:6, :])
        o_ref[0] = x_ref[0] + gate * r


def _moe(xa, mod, g2, gp, rw_t, rbias, ws1, ws3, ws2, w1, w3, w2, layer, n_ctx):
    nb, nt, d = xa.shape
    tm = _pick(nt, (1024, 768, 512, 384, 256, 128))
    n_pairs = w1.shape[1] // 2
    row = lambda b, i, e: (b, i, 0)
    c2 = lambda b, i, e: (0, 0)
    shared_blk = lambda b, i, e: (layer, 0, 0)
    pair_blk = lambda b, i, e: (layer, jnp.maximum(e - 1, 0), 0, 0)
    return pl.pallas_call(
        functools.partial(_moe_kernel, tm=tm, rb=ROW_BLOCK, n_ctx=n_ctx),
        grid=(nb, nt // tm, n_pairs + 1),
        in_specs=[pl.BlockSpec((1, tm, d), row),
                  pl.BlockSpec((1, 6, d), lambda b, i, e: (b, 0, 0)),
                  pl.BlockSpec((1, 6, d), lambda b, i, e: (nb, 0, 0)),
                  pl.BlockSpec((1, d), c2),
                  pl.BlockSpec((1, d), c2),
                  pl.BlockSpec((N_EXPERTS, d), c2),
                  pl.BlockSpec((N_EXPERTS, 1), c2),
                  pl.BlockSpec((1, d, D_EXPERT), shared_blk),
                  pl.BlockSpec((1, d, D_EXPERT), shared_blk),
                  pl.BlockSpec((1, D_EXPERT, d), shared_blk),
                  pl.BlockSpec((1, 2, d, D_EXPERT), pair_blk),
                  pl.BlockSpec((1, 2, d, D_EXPERT), pair_blk),
                  pl.BlockSpec((1, 2, D_EXPERT, d), pair_blk)],
        out_specs=pl.BlockSpec((1, tm, d), row),
        out_shape=jax.ShapeDtypeStruct((nb, nt, d), F32),
        scratch_shapes=[pltpu.VMEM((tm, d), BF16), pltpu.VMEM((LANES, tm), F32), pltpu.VMEM((tm, LANES), F32),
                        pltpu.VMEM((tm, d), F32)],
        compiler_params=_params("arbitrary", "arbitrary", "arbitrary"),
        name="moe",
    )(xa, mod, mod, g2, gp, rw_t, rbias, ws1, ws3, ws2, w1, w3, w2)


def _swap_perm(width, group):
    j = np.arange(width)
    src = np.where((j % group) < group // 2, j + group // 2, j - group // 2)
    return jnp.asarray(np.arange(width)[:, None] == src[None, :], BF16)


def _rope_tables(n_lat, n_ctx):
    def angles(pos, dim):
        half = dim // 2
        inv = ROPE_BASE ** (-jnp.arange(half, dtype=F32) / half)
        return pos.astype(F32)[:, None] * inv[None, :]

    def tables(cos_parts, sin_parts, reps):
        cos = jnp.tile(jnp.concatenate(cos_parts, axis=-1), (1, reps))
        sin = jnp.tile(jnp.concatenate(sin_parts, axis=-1), (1, reps))
        w = cos.shape[1]
        return (jnp.concatenate([jnp.ones((n_ctx, w), F32), cos], axis=0),
                jnp.concatenate([jnp.zeros((n_ctx, w), F32), sin], axis=0))

    rows = n_lat // GRID_W
    ang_t = angles(jnp.arange(n_lat), HEAD_DIM)
    ang_r = angles(jnp.repeat(jnp.arange(rows), GRID_W), ROPE_DIM // 2)
    ang_c = angles(jnp.tile(jnp.arange(GRID_W), rows), ROPE_DIM // 2)
    ct, st = jnp.cos(ang_t), jnp.sin(ang_t)
    ret = tables([ct, ct], [-st, st], N_HEADS)
    cr, sr, cc, sc = jnp.cos(ang_r), jnp.sin(ang_r), jnp.cos(ang_c), jnp.sin(ang_c)
    mla = tables([cr, cr, cc, cc], [-sr, sr, -sc, sc], N_HEADS)
    return ret, mla


def _ret_tables(logit):
    log_g = jax.nn.log_sigmoid(logit.astype(F32))
    lane_lg = jnp.repeat(log_g, HEAD_DIM, axis=1)
    idx = jnp.arange(CHUNK, dtype=F32)[:, None]
    kd = jnp.stack([jnp.exp(lane_lg[0][None, :] * (CHUNK - 1 - idx)), jnp.exp(lane_lg[1][None, :] * idx)])
    qd = jnp.stack([jnp.exp(lane_lg[0][None, :] * (idx + 1)), jnp.exp(lane_lg[1][None, :] * (CHUNK - idx))])
    cd = jnp.exp(lane_lg * CHUNK)[:, None, :]
    diff = idx - idx.T
    blocks = []
    for h in range(N_HEADS):
        f = jnp.exp(log_g[0, h] * jnp.where(diff >= 0, diff, 0.0))
        b = jnp.exp(log_g[1, h] * jnp.where(diff < 0, -diff, 0.0))
        blocks.append(jnp.where(diff >= 0, f, b))
    dm = jnp.concatenate(blocks, axis=1)
    return kd, cd, qd, dm


def _pack_w_in(w_in):
    d = w_in.shape[0]
    mla = jnp.concatenate([w_in[:, OFF_MLA:OFF_MLA + MLA_COLS], jnp.zeros((d, MLA_PAD - MLA_COLS), w_in.dtype)], 1)
    w = jnp.concatenate([w_in[:, OFF_RET:OFF_RET + RET_COLS], w_in[:, OFF_DN:OFF_DN + 4 * MIX_W],
                         w_in[:, OFF_SG:OFF_SG + SG_COLS], mla, w_in[:, OFF_GATE:OFF_GATE + GATE_COLS]], axis=1)
    wab = w_in[:, OFF_DN + 4 * MIX_W:OFF_DN + DN_COLS]
    wabc = jnp.concatenate([wab, jnp.zeros((d, LANES - 4 * N_HEADS), w_in.dtype)], axis=1)
    return w.astype(BF16), wab.T.astype(BF16), wabc.astype(BF16)


def _mla_weights(w_uq, w_ukv):
    dq = NOPE_DIM + ROPE_DIM
    dkv = NOPE_DIM + V_DIM
    wq = w_uq.reshape(Q_LORA, N_HEADS, dq)
    wqn = wq[:, :, :NOPE_DIM].reshape(Q_LORA, N_HEADS * NOPE_DIM)
    wqr = wq[:, :, NOPE_DIM:].reshape(Q_LORA, N_HEADS * ROPE_DIM)
    wkv = w_ukv.reshape(KV_LORA, N_HEADS, dkv)
    head_eye = jnp.eye(N_HEADS, dtype=F32)
    wa = jnp.einsum("chd,hg->hcgd", wkv[:, :, :NOPE_DIM], head_eye).reshape(N_HEADS, KV_LORA, N_HEADS * NOPE_DIM)
    wa = jnp.pad(wa, ((0, 0), (0, QK_W - KV_LORA), (0, 0)))
    wuv = jnp.einsum("chd,hg->hcgd", wkv[:, :, NOPE_DIM:], head_eye).reshape(N_HEADS, KV_LORA, MIX_W)
    selq = np.zeros((N_HEADS, QK_W, N_HEADS * ROPE_DIM), np.float32)
    for h in range(N_HEADS):
        selq[h, KV_LORA:KV_LORA + ROPE_DIM, h * ROPE_DIM:(h + 1) * ROPE_DIM] = np.eye(ROPE_DIM)
    selc = np.zeros((KV_LORA, QK_W), np.float32)
    selc[:, 0:KV_LORA] = np.eye(KV_LORA)
    selr = np.zeros((ROPE_W, QK_W), np.float32)
    selr[0:ROPE_DIM, KV_LORA:KV_LORA + ROPE_DIM] = np.eye(ROPE_DIM)
    selv = np.zeros((VT_ROWS, KV_LORA), np.float32)
    selv[0:KV_LORA, :] = np.eye(KV_LORA)
    one_col = np.zeros((VT_ROWS, 1), np.float32)
    one_col[KV_LORA, 0] = 1.0
    return (tuple(jnp.asarray(a, BF16) for a in (wqn, wqr, wa, selq, selc, selr, selv))
            + (jnp.asarray(one_col), wuv.astype(BF16)))


def kernel(x, c, ctx, c_ctx, w_ada, b_ada, g_pre1, g_post1, g_pre2, g_post2, w_in, ret_decay_logit, sg_norm_g, sg_w, sg_b, dn_conv_w, dn_A_log, dn_dt_bias, dn_norm_g, mla_q_norm_g, mla_kv_norm_g, mla_w_uq, mla_w_ukv, w_branch, w_out, router_w, router_bias, moe_w1, moe_w3, moe_w2, shared_w1, shared_w3, shared_w2):
    nb, n_lat, d = x.shape
    n_ctx = ctx.shape[1]
    depth = w_in.shape[0]
    assert d == D_MODEL and n_lat % GRID_W == 0 and n_lat % CHUNK == 0 and n_ctx % 256 == 0
    ncc = n_ctx // CHUNK

    n_cond = -(-(nb + 1) // 8) * 8
    cond = jnp.concatenate([c, c_ctx[None], jnp.zeros((n_cond - nb - 1, d), F32)], axis=0)
    mod_all = _adaln(cond, w_ada, b_ada).reshape(depth, n_cond, 6, d)

    (ret_cos, ret_sin), (mla_cos, mla_sin) = _rope_tables(n_lat, n_ctx)
    perm_ret = _swap_perm(MIX_W, HEAD_DIM)
    perm_mla = _swap_perm(N_HEADS * ROPE_DIM, ROPE_DIM // 2)
    lane_head = jnp.arange(MIX_W) // HEAD_DIM
    bd = (lane_head[:, None] == lane_head[None, :]).astype(F32)
    ones_bd = bd.astype(BF16)
    rw_t = router_w.T.astype(BF16)
    rbias = router_bias.astype(F32)[:, None]
    expert_w = tuple(w.astype(BF16) for w in (shared_w1, shared_w3, shared_w2, moe_w1, moe_w3, moe_w2))

    xa = (ctx, x)
    for l in range(depth):
        last = l == depth - 1
        if last and isinstance(xa, tuple):
            xa = jnp.concatenate(xa, axis=1)
        mod = mod_all[l]
        w_l, wab_l, wabc_l = _pack_w_in(w_in[l])
        p, ab_t, ab_c = _inproj(xa, n_ctx + n_lat, mod, g_pre1[l][None], w_l, wab_l, wabc_l, n_ctx)

        kd, cd, qd, dm = _ret_tables(ret_decay_logit[l])
        wcat = jnp.concatenate([sg_w[l, h] for h in range(N_HEADS)], axis=1).astype(BF16)
        sg_bias = jnp.repeat(sg_b[l].T, HEAD_DIM, axis=1)
        y_ret, y_sg = _retention_and_sgate(p, ret_cos, ret_sin, perm_ret, (kd, cd, qd, dm, bd, ones_bd),
                                           sg_norm_g[l][None], wcat, sg_bias, ncc)

        neg_a = (-jnp.exp(dn_A_log[l].astype(F32))).reshape(2 * N_HEADS, 1)
        dtb = dn_dt_bias[l].astype(F32).reshape(2 * N_HEADS, 1)
        conv_w = jnp.concatenate([dn_conv_w[l], jnp.zeros((8 - CONV_W, 3 * MIX_W), F32)], axis=0)
        o_f, o_b = _deltanet(p, ab_t, ab_c, conv_w, neg_a, dtb, bd, ones_bd, ncc)

        row0 = n_ctx if last else 0
        y_mla, y_mla_ctx = _mla(p, mla_cos, mla_sin, perm_mla, mla_q_norm_g[l][None], mla_kv_norm_g[l][None],
                                *_mla_weights(mla_w_uq[l], mla_w_ukv[l]), n_ctx, not last)
        if not last:
            y_mla = jnp.concatenate([y_mla_ctx, y_mla], axis=1)

        xa = _merge(xa, y_ret, y_sg, o_f, o_b, y_mla, p, w_branch[l].astype(BF16), w_out[l].astype(BF16),
                    jnp.tile(dn_norm_g[l], N_HEADS)[None], g_post1[l][None], mod, ones_bd, n_ctx, row0)

        xa = _moe(xa, mod, g_pre2[l][None], g_post2[l][None], rw_t, rbias, *expert_w, l, n_ctx - row0)
    return xa
```

```python
import functools
import math

import jax
import jax.numpy as jnp
import numpy as np
from jax import lax
from jax.experimental import pallas as pl
from jax.experimental.pallas import tpu as pltpu

F32 = jnp.float32
BF16 = jnp.bfloat16
HIGHEST = lax.Precision.HIGHEST

D_MODEL = 1024
GRID_W = 64
N_HEADS = 4
HEAD_DIM = 64
MIX_W = N_HEADS * HEAD_DIM
CHUNK = 128
ROPE_BASE = 10000.0
EPS = 1e-6
RET_DECAY_EXP0 = 5.0
CONV_W = 5
Q_LORA = 256
KV_LORA = 128
NOPE_DIM = 64
ROPE_DIM = 32
V_DIM = 64
N_EXPERTS = 16
N_GROUPS = 4
EXPERTS_PER_GROUP = N_EXPERTS // N_GROUPS
D_EXPERT = 256
N_BRANCH = 4

RET_COLS = 4 * MIX_W
SG_COLS = 2 * MIX_W
DN_COLS = 4 * MIX_W + 4 * N_HEADS
MLA_COLS = Q_LORA + KV_LORA + ROPE_DIM
GATE_COLS = N_BRANCH * D_MODEL
OFF_RET = 0
OFF_SG = OFF_RET + RET_COLS
OFF_DN = OFF_SG + SG_COLS
OFF_MLA = OFF_DN + DN_COLS
OFF_GATE = OFF_MLA + MLA_COLS

P_RET = 0
P_DN = 1024
P_SG = 2048
P_MLA = 2560
P_GATE = 3072
P_COLS = 7168
MLA_PAD = 512

VMEM_LIMIT = 56 * 1024 * 1024
LANES = 128
ROW_BLOCK = 128


def _dot(a, b, precision=None):
    return jnp.dot(a, b, preferred_element_type=F32, precision=precision)


def _dot_nt(a, b, precision=None):
    return lax.dot_general(a, b, (((1,), (1,)), ((), ())), preferred_element_type=F32, precision=precision)


def _dot_tn(a, b):
    return lax.dot_general(a, b, (((0,), (0,)), ((), ())), preferred_element_type=F32)


def _mm(a, b):
    return _dot(a.astype(BF16), b.astype(BF16))


def _params(*sem):
    return pltpu.CompilerParams(dimension_semantics=sem, vmem_limit_bytes=VMEM_LIMIT)


def _pick(n, cands):
    for c in cands:
        if n % c == 0:
            return c
    raise ValueError(f"no tile for {n}")


def _head_of_lane(width, group):
    return lax.broadcasted_iota(jnp.int32, (1, width), 1) // group


def _stack_heads(x):
    head = _head_of_lane(MIX_W, HEAD_DIM)
    xf = x.astype(F32)
    return jnp.concatenate([jnp.where(head == h, xf, 0.0).astype(BF16) for h in range(N_HEADS)], axis=0)


def _expand_heads(cols):
    head = _head_of_lane(MIX_W, HEAD_DIM)
    out = cols[:, N_HEADS - 1:N_HEADS]
    for h in range(N_HEADS - 2, -1, -1):
        out = jnp.where(head <= h, cols[:, h:h + 1], out)
    return out


def _head_sum(x, ones_bd):
    hi = x.astype(BF16)
    lo = (x - hi.astype(F32)).astype(BF16)
    return _dot(hi, ones_bd) + _dot(lo, ones_bd)


def _rot(x_bf, cos, sin, perm):
    return x_bf.astype(F32) * cos + _dot(x_bf, perm) * sin


def _norm_modulate(x, g, is_ctx, mc_ref, ml_ref, shift_row, scale_row):
    shift = jnp.where(is_ctx, mc_ref[0, shift_row:shift_row + 1, :], ml_ref[0, shift_row:shift_row + 1, :])
    scale = jnp.where(is_ctx, mc_ref[0, scale_row:scale_row + 1, :], ml_ref[0, scale_row:scale_row + 1, :])
    gain = g * (1.0 + scale)
    return x * lax.rsqrt(jnp.mean(x * x, axis=-1, keepdims=True) + EPS) * gain + shift


def _adaln_kernel(c_ref, w_ref, b_ref, o_ref):
    c = c_ref[...]
    s = c * jax.nn.sigmoid(c)
    o_ref[0] = _dot(s, w_ref[0], precision=HIGHEST) + b_ref[0]


def _adaln(cond, w_ada, b_ada):
    n_l, d, d6 = w_ada.shape
    r = cond.shape[0]
    tn = 1024
    return pl.pallas_call(
        _adaln_kernel,
        grid=(n_l, d6 // tn),
        in_specs=[pl.BlockSpec((r, d), lambda l, j: (0, 0)),
                  pl.BlockSpec((1, d, tn), lambda l, j: (l, 0, j)),
                  pl.BlockSpec((1, 1, tn), lambda l, j: (l, 0, j))],
        out_specs=pl.BlockSpec((1, r, tn), lambda l, j: (l, 0, j)),
        out_shape=jax.ShapeDtypeStruct((n_l, r, d6), F32),
        compiler_params=_params("arbitrary", "arbitrary"),
        name="adaln",
    )(cond, w_ada, b_ada.reshape(n_l, 1, d6))


def _inproj_kernel(x_ref, ctx_ref, ml_ref, mc_ref, g_ref, w_ref, wab_ref, wabc_ref, p_ref, ab_ref, abc_ref, xn_ref,
                   *, tm, rb, n_ctx, split):
    i = pl.program_id(1)
    j = pl.program_id(2)

    @pl.when(j == 0)
    def _():
        def blk(r, carry):
            r0 = pl.multiple_of(r * rb, rb)
            is_ctx = i * tm + r0 < n_ctx
            if split:
                x_off = pl.multiple_of(jnp.maximum(jnp.where(i == 0, r0 - n_ctx, r0), 0), rb)
                c_off = pl.multiple_of(jnp.minimum(r0, n_ctx - rb), rb)
                x = jnp.where(is_ctx, ctx_ref[0, pl.ds(c_off, rb), :], x_ref[0, pl.ds(x_off, rb), :])
            else:
                x = x_ref[0, pl.ds(r0, rb), :]
            hn = _norm_modulate(x, g_ref[...], is_ctx, mc_ref, ml_ref, 0, 1)
            xn_ref[pl.ds(r0, rb), :] = hn.astype(BF16)
            return carry

        lax.fori_loop(0, tm // rb, blk, 0)
        ab_ref[0] = _dot_nt(wab_ref[...], xn_ref[...])
        abc_ref[0] = _dot(xn_ref[...], wabc_ref[...])

    p_ref[0] = _dot(xn_ref[...], w_ref[...]).astype(BF16)


def _token_sources(tokens, n_ctx, tm):
    if isinstance(tokens, tuple):
        ctx, x = tokens
        d = x.shape[-1]
        if x.shape[1] >= tm and tm > n_ctx:
            x_spec = pl.BlockSpec(
                (pl.Element(1), pl.Element(tm), pl.Element(d)),
                lambda b, i, *_: (b, pl.multiple_of(jnp.maximum(i * tm - n_ctx, 0), LANES), 0))
            ctx_spec = pl.BlockSpec((1, n_ctx, d), lambda b, i, *_: (b, 0, 0))
            return (x, ctx), (x_spec, ctx_spec), True
        tokens = jnp.concatenate([ctx, x], axis=1)
    nb, _, d = tokens.shape
    dummy = jnp.zeros((nb, 8, d), tokens.dtype)
    return ((tokens, dummy), (pl.BlockSpec((1, tm, d), lambda b, i, *_: (b, i, 0)),
                              pl.BlockSpec((1, 8, d), lambda b, i, *_: (b, 0, 0))), False)


def _inproj(tokens, nt, mod, g, w, wab, wabc, n_ctx):
    tm = _pick(nt, (1408, 768, 384, 256, 128))
    tn = 1792
    (xa, ctx), (x_spec, ctx_spec), split = _token_sources(tokens, n_ctx, tm)
    nb, _, d = xa.shape
    kern = functools.partial(_inproj_kernel, tm=tm, rb=ROW_BLOCK, n_ctx=n_ctx, split=split)
    return pl.pallas_call(
        kern,
        grid=(nb, nt // tm, P_COLS // tn),
        in_specs=[x_spec, ctx_spec,
                  pl.BlockSpec((1, 6, d), lambda b, i, j: (b, 0, 0)),
                  pl.BlockSpec((1, 6, d), lambda b, i, j: (nb, 0, 0)),
                  pl.BlockSpec((1, d), lambda b, i, j: (0, 0)),
                  pl.BlockSpec((d, tn), lambda b, i, j: (0, j)),
                  pl.BlockSpec((16, d), lambda b, i, j: (0, 0)),
                  pl.BlockSpec((d, LANES), lambda b, i, j: (0, 0))],
        out_specs=[pl.BlockSpec((1, tm, tn), lambda b, i, j: (b, i, j)),
                   pl.BlockSpec((1, 16, tm), lambda b, i, j: (b, 0, i)),
                   pl.BlockSpec((1, tm, LANES), lambda b, i, j: (b, i, 0))],
        out_shape=[jax.ShapeDtypeStruct((nb, nt, P_COLS), BF16),
                   jax.ShapeDtypeStruct((nb, 16, nt), F32),
                   jax.ShapeDtypeStruct((nb, nt, LANES), F32)],
        scratch_shapes=[pltpu.VMEM((tm, d), BF16)],
        compiler_params=_params("arbitrary", "arbitrary", "arbitrary"),
        name="inproj",
    )(xa, ctx, mod, mod, g, w, wab, wabc)


def _bwd_chunk(t, ncc, nc):
    return jnp.where(t < ncc, ncc - 1 - t, nc - 1 - (t - ncc))


def _ret_state_kernel(pf_ref, pb_ref, cf_ref, sf_ref, cb_ref, sb_ref, perm_ref, kd_ref, cd_ref, bd_ref,
                      of_ref, ob_ref, st_f, st_b, *, cb):
    t = pl.program_id(1)

    @pl.when(t == 0)
    def _():
        st_f[...] = jnp.zeros_like(st_f)
        st_b[...] = jnp.zeros_like(st_b)

    def increments(p_ref, c_ref, s_ref, d):
        out = []
        for i in range(cb):
            r = slice(i * CHUNK, (i + 1) * CHUNK)
            kr = _rot(p_ref[0, r, 0:MIX_W], c_ref[r, :], s_ref[r, :], perm_ref[...]) * (HEAD_DIM ** -0.5)
            out.append(bd_ref[...] * _dot_tn((kr * kd_ref[d]).astype(BF16), p_ref[0, r, MIX_W:2 * MIX_W]))
        return out

    inc_f = increments(pf_ref, cf_ref, sf_ref, 0)
    inc_b = increments(pb_ref, cb_ref, sb_ref, 1)
    s = st_f[...]
    for i in range(cb):
        of_ref[0, i] = s.astype(BF16)
        s = cd_ref[0] * s + inc_f[i]
    st_f[...] = s
    s = st_b[...]
    for i in reversed(range(cb)):
        ob_ref[0, i] = s.astype(BF16)
        s = cd_ref[1] * s + inc_b[i]
    st_b[...] = s


def _gelu_tanh(x):
    return 0.5 * x * (1.0 + jnp.tanh(math.sqrt(2.0 / math.pi) * (x + 0.044715 * (x * x * x))))


def _mix_out_kernel(p_ref, pg_ref, c_ref, s_ref, sf_ref, sb_ref, perm_ref, dm_ref, qd_ref, ones_ref,
                    ng_ref, wg_ref, bg_ref, y_ref, ysg_ref, *, cb):
    chunks = range(cb)
    rows = [slice(i * CHUNK, (i + 1) * CHUNK) for i in chunks]
    perm, dm, ones_bd = perm_ref[...], dm_ref[...], ones_ref[...]
    p = [p_ref[0, r, :] for r in rows]
    cos = [c_ref[r, :] for r in rows]
    sin = [s_ref[r, :] for r in rows]
    qr = [_rot(p[i][:, 0:MIX_W], cos[i], sin[i], perm) for i in chunks]
    kr = [_rot(p[i][:, MIX_W:2 * MIX_W], cos[i], sin[i], perm) * (HEAD_DIM ** -0.5) for i in chunks]
    z = [_gelu_tanh(pg_ref[0, r, :].astype(F32)) for r in rows]
    vg = [x[:, MIX_W:] for x in z]
    mu_g = [jnp.mean(x, axis=-1, keepdims=True) for x in vg]
    vgc = [x - m for x, m in zip(vg, mu_g)]
    var_g = [jnp.mean(x * x, axis=-1, keepdims=True) for x in vgc]
    vn = [x * lax.rsqrt(s + EPS) * ng_ref[...] for x, s in zip(vgc, var_g)]
    sc = [_dot_nt(qr[i].astype(BF16), _stack_heads(kr[i])) * dm for i in chunks]
    mixed = [_dot(wg_ref[...], _stack_heads(x)) for x in vn]
    o = [_dot(sc[i].astype(BF16), _stack_heads(p[i][:, 2 * MIX_W:3 * MIX_W])) for i in chunks]
    qs = [jnp.concatenate([(qr[i] * qd_ref[0]).astype(BF16), (qr[i] * qd_ref[1]).astype(BF16)], axis=1)
          for i in chunks]
    ss = [jnp.concatenate([sf_ref[0, i], sb_ref[0, i]], axis=0) for i in chunks]
    o = [o[i] + _dot(qs[i], ss[i]) for i in chunks]
    for i in chunks:
        ysg_ref[0, rows[i], :] = (z[i][:, :MIX_W] * (mixed[i] + bg_ref[...])).astype(BF16)
    mu = [_head_sum(x, ones_bd) * (1.0 / HEAD_DIM) for x in o]
    oc = [x - m for x, m in zip(o, mu)]
    var = [_head_sum(x * x, ones_bd) * (1.0 / HEAD_DIM) for x in oc]
    for i in chunks:
        g = p[i][:, 3 * MIX_W:4 * MIX_W].astype(F32)
        y_ref[0, rows[i], :] = (oc[i] * lax.rsqrt(var[i] + EPS) * (g * jax.nn.sigmoid(g))).astype(BF16)


def _retention_and_sgate(p, cos, sin, perm, tabs, sg_ng, sg_w, sg_bias, ncc):
    nb, nt, _ = p.shape
    nc = nt // CHUNK
    kd, cd, qd, dm, bd, ones_bd = tabs
    cb = 2
    assert nc % cb == 0 and ncc % cb == 0
    nblk, ncb = nc // cb, ncc // cb
    fwd = lambda b, t: (b, t, 0)
    bwd = lambda b, t: (b, _bwd_chunk(t, ncb, nblk), 0)
    tab_f = lambda b, t: (t, 0)
    tab_b = lambda b, t: (_bwd_chunk(t, ncb, nblk), 0)
    c2 = lambda b, t: (0, 0)
    c3 = lambda b, t: (0, 0, 0)
    st_shape = jax.ShapeDtypeStruct((nb, nc, MIX_W, MIX_W), BF16)

    def kv_window(block_of):
        return pl.BlockSpec((pl.Element(1), pl.Element(cb * CHUNK), pl.Element(2 * MIX_W)),
                            lambda b, t: (b, pl.multiple_of(block_of(t) * (cb * CHUNK), LANES), P_RET + MIX_W))

    st_f, st_b = pl.pallas_call(
        functools.partial(_ret_state_kernel, cb=cb),
        grid=(nb, nblk),
        in_specs=[kv_window(lambda t: t), kv_window(lambda t: _bwd_chunk(t, ncb, nblk)),
                  pl.BlockSpec((cb * CHUNK, MIX_W), tab_f), pl.BlockSpec((cb * CHUNK, MIX_W), tab_f),
                  pl.BlockSpec((cb * CHUNK, MIX_W), tab_b), pl.BlockSpec((cb * CHUNK, MIX_W), tab_b),
                  pl.BlockSpec((MIX_W, MIX_W), c2),
                  pl.BlockSpec((2, CHUNK, MIX_W), c3),
                  pl.BlockSpec((2, 1, MIX_W), c3),
                  pl.BlockSpec((MIX_W, MIX_W), c2)],
        out_specs=[pl.BlockSpec((1, cb, MIX_W, MIX_W), lambda b, t: (b, t, 0, 0)),
                   pl.BlockSpec((1, cb, MIX_W, MIX_W), lambda b, t: (b, _bwd_chunk(t, ncb, nblk), 0, 0))],
        out_shape=[st_shape, st_shape],
        scratch_shapes=[pltpu.VMEM((MIX_W, MIX_W), F32), pltpu.VMEM((MIX_W, MIX_W), F32)],
        compiler_params=_params("arbitrary", "arbitrary"),
        name="ret_state",
    )(p, p, cos, sin, cos, sin, perm, kd, cd, bd)
    blk = lambda b, t: (b, t, 0)
    y_shape = jax.ShapeDtypeStruct((nb, nt, MIX_W), BF16)
    cb = _pick(nc, (6, 3, 2, 1))
    return pl.pallas_call(
        functools.partial(_mix_out_kernel, cb=cb),
        grid=(nb, nc // cb),
        in_specs=[pl.BlockSpec((1, cb * CHUNK, RET_COLS), blk),
                  pl.BlockSpec((1, cb * CHUNK, SG_COLS), lambda b, t: (b, t, P_SG // SG_COLS)),
                  pl.BlockSpec((cb * CHUNK, MIX_W), tab_f), pl.BlockSpec((cb * CHUNK, MIX_W), tab_f),
                  pl.BlockSpec((1, cb, MIX_W, MIX_W), lambda b, t: (b, t, 0, 0)),
                  pl.BlockSpec((1, cb, MIX_W, MIX_W), lambda b, t: (b, t, 0, 0)),
                  pl.BlockSpec((MIX_W, MIX_W), c2),
                  pl.BlockSpec((CHUNK, N_HEADS * CHUNK), c2),
                  pl.BlockSpec((2, CHUNK, MIX_W), c3),
                  pl.BlockSpec((MIX_W, MIX_W), c2),
                  pl.BlockSpec((1, MIX_W), c2),
                  pl.BlockSpec((CHUNK, N_HEADS * CHUNK), c2),
                  pl.BlockSpec((CHUNK, MIX_W), c2)],
        out_specs=[pl.BlockSpec((1, cb * CHUNK, MIX_W), blk), pl.BlockSpec((1, cb * CHUNK, MIX_W), blk)],
        out_shape=[y_shape, y_shape],
        compiler_params=_params("arbitrary", "arbitrary"),
        name="mix_out",
    )(p, p, cos, sin, st_f, st_b, perm, dm, qd, ones_bd, sg_ng, sg_w, sg_bias)


def _softplus(a):
    return jnp.maximum(a, 0.0) + jnp.log1p(jnp.exp(-jnp.abs(a)))


def _dn_prep_kernel(pc_ref, pp_ref, pn_ref, ab_ref, abc_ref, cw_ref, na_ref, dtb_ref, nar_ref, dtbr_ref, ones_ref,
                    qkv_ref, gb_ref, gbc_ref, xe_ref, *, rows, ctx_blocks, n_blocks):
    t = pl.program_id(1)
    w3 = 3 * MIX_W
    prev_ok = jnp.where((t != 0) & (t != ctx_blocks), 1.0, 0.0)
    next_ok = jnp.where((t != ctx_blocks - 1) & (t != n_blocks - 1), 1.0, 0.0)
    tail = pp_ref[0, rows - 16:rows, 0:w3].astype(F32)
    head = pn_ref[0, 0:16, 0:w3].astype(F32)
    xe_ref[0:8, :] = tail[8:16, :] * prev_ok
    xe_ref[8:8 + rows, :] = pc_ref[0, :, 0:w3].astype(F32)
    xe_ref[8 + rows:16 + rows, :] = head[0:8, :] * next_ok
    pad = CONV_W // 2
    y = xe_ref[8 - pad:8 - pad + rows, :] * cw_ref[0:1, :]
    for i in range(1, CONV_W):
        y = y + xe_ref[8 - pad + i:8 - pad + i + rows, :] * cw_ref[i:i + 1, :]
    y = y * jax.nn.sigmoid(y)
    q = y[:, 0:MIX_W]
    k = y[:, MIX_W:2 * MIX_W]
    v = y[:, 2 * MIX_W:w3]
    ones_bd = ones_ref[...]
    qn = q * lax.rsqrt(_head_sum(q * q, ones_bd) + EPS) * (HEAD_DIM ** -0.5)
    kn = k * lax.rsqrt(_head_sum(k * k, ones_bd) + EPS)
    qkv_ref[0, :, 0:MIX_W] = qn.astype(BF16)
    qkv_ref[0, :, MIX_W:2 * MIX_W] = kn.astype(BF16)
    qkv_ref[0, :, 2 * MIX_W:w3] = v.astype(BF16)
    ab = ab_ref[0]
    gb_ref[0, 0:8, :] = na_ref[...] * _softplus(ab[0:8, :] + dtb_ref[...])
    gb_ref[0, 8:16, :] = jax.nn.sigmoid(ab[8:16, :])
    abc = abc_ref[0]
    lane = lax.broadcasted_iota(jnp.int32, (1, LANES), 1)
    g_c = nar_ref[...] * _softplus(abc + dtbr_ref[...])
    gbc_ref[0] = jnp.where(lane < 8, g_c, jnp.where(lane < 16, jax.nn.sigmoid(abc), 0.0))


def _split3(x):
    hi = x.astype(BF16)
    r = x - hi.astype(F32)
    mid = r.astype(BF16)
    lo = (r - mid.astype(F32)).astype(BF16)
    return hi, mid, lo


def _tri_inverse(mats, ii, jj):
    eye = jnp.where(ii == jj, 1.0, 0.0)
    nd = [jnp.where((ii // 16) == (jj // 16), n, 0.0) for n in mats]
    p1 = [_mm(x, x) for x in nd]
    m = [eye - x for x in nd]
    p2 = [_mm(x, x) for x in p1]
    m = [x + _mm(x, y) for x, y in zip(m, p1)]
    p3 = [_mm(x, x) for x in p2]
    m = [x + _mm(x, y) for x, y in zip(m, p2)]
    m = [x + _mm(x, y) for x, y in zip(m, p3)]
    for lvl in (16, 32, 64):
        off_mask = ((ii // (2 * lvl)) == (jj // (2 * lvl))) & ((ii // lvl) != (jj // lvl))
        t = [_mm(jnp.where(off_mask, n, 0.0), x) for n, x in zip(mats, m)]
        m = [x - _mm(x, y) for x, y in zip(m, t)]
    return m


def _dn_pre(qkv, g, gbc, d, lower):
    c = CHUNK
    qn = qkv[:, 0:MIX_W]
    kn = qkv[:, MIX_W:2 * MIX_W]
    v = qkv[:, 2 * MIX_W:3 * MIX_W]
    ii = lax.broadcasted_iota(jnp.int32, (c, c), 0)
    jj = lax.broadcasted_iota(jnp.int32, (c, c), 1)
    incl = (ii >= jj) if lower else (ii <= jj)
    tri = jnp.where(incl, 1.0, 0.0).astype(BF16)
    g_row = sum(_dot_nt(part, tri) for part in _split3(g))[N_HEADS * d:N_HEADS * (d + 1), :]
    cum = sum(_dot(tri, part) for part in _split3(gbc))
    g_col = cum[:, N_HEADS * d:N_HEADS * (d + 1)]
    b_col = gbc[:, 2 * N_HEADS + N_HEADS * d:2 * N_HEADS + N_HEADS * (d + 1)]
    g_cols4 = jnp.concatenate([jnp.broadcast_to(g_col[:, h:h + 1], (c, c)) for h in range(N_HEADS)], axis=1)
    b_cols4 = jnp.concatenate([jnp.broadcast_to(b_col[:, h:h + 1], (c, c)) for h in range(N_HEADS)], axis=1)
    g_rows4 = jnp.concatenate([g_row[h:h + 1, :] for h in range(N_HEADS)], axis=1)
    incl4 = jnp.concatenate([incl] * N_HEADS, axis=1)
    diag4 = jnp.concatenate([ii == jj] * N_HEADS, axis=1)
    decay = jnp.where(incl4, jnp.exp(jnp.where(incl4, g_cols4 - g_rows4, 0.0)), 0.0)
    kstack = _stack_heads(kn)
    kk = _dot_nt(kn, kstack)
    qk = _dot_nt(qn, kstack)
    n_mat = jnp.where(diag4, 0.0, decay * kk * b_cols4)
    attn = (decay * qk).astype(BF16)
    g256 = _expand_heads(g_col)
    eg256 = jnp.exp(g256)
    b256 = _expand_heads(b_col)
    vb = v.astype(F32) * b256
    kbg = kn.astype(F32) * b256 * eg256
    rhs = jnp.concatenate([_stack_heads(vb), _stack_heads(kbg)], axis=1)
    g_last = g256[c - 1:c, :] if lower else g256[0:1, :]
    kdec = (kn.astype(F32) * jnp.exp(g_last - g256)).astype(BF16)
    n_heads = [n_mat[:, h * c:(h + 1) * c] for h in range(N_HEADS)]
    return n_heads, dict(qn=qn, attn=attn, rhs=rhs, eg=eg256, kdec=kdec, sdec=jnp.exp(g_last))


def _dn_post(z, s_prev, bd):
    s_bf = s_prev.astype(BF16)
    w = z["u"] - _dot(z["wk"], s_bf)
    o = z["eg"] * _dot(z["qn"], s_bf) + _dot(z["attn"], _stack_heads(w))
    s_next = z["sdec"] * s_prev + bd * _dot_tn(z["kdec"], w.astype(BF16))
    return o, s_next


def _dn_scan_kernel(qf_ref, qb_ref, gf_ref, gb_ref, gcf_ref, gcb_ref, bd_ref, of_ref, ob_ref, st_f, st_b, *, cb):
    t = pl.program_id(1)

    @pl.when(t == 0)
    def _():
        st_f[...] = jnp.zeros_like(st_f)
        st_b[...] = jnp.zeros_like(st_b)

    bd = bd_ref[...]
    rows = [slice(i * CHUNK, (i + 1) * CHUNK) for i in range(cb)]
    mats, pres = [], []
    for d, (q_ref, g_ref, gc_ref) in enumerate(((qf_ref, gf_ref, gcf_ref), (qb_ref, gb_ref, gcb_ref))):
        for r in rows:
            n_heads, pre = _dn_pre(q_ref[0, r, :], g_ref[0, :, r], gc_ref[0, r, :], d, d == 0)
            mats += n_heads
            pres.append(pre)
    ii = lax.broadcasted_iota(jnp.int32, (CHUNK, CHUNK), 0)
    jj = lax.broadcasted_iota(jnp.int32, (CHUNK, CHUNK), 1)
    inv = _tri_inverse(mats, ii, jj)
    for n, pre in enumerate(pres):
        a_inv = jnp.concatenate(inv[N_HEADS * n:N_HEADS * (n + 1)], axis=1).astype(BF16)
        uw = _dot(a_inv, pre["rhs"])
        pre["u"] = uw[:, 0:MIX_W]
        pre["wk"] = uw[:, MIX_W:2 * MIX_W].astype(BF16)
    s_f, s_b = st_f[...], st_b[...]
    for k in range(cb):
        o, s_f = _dn_post(pres[k], s_f, bd)
        of_ref[0, rows[k], :] = o
        o, s_b = _dn_post(pres[cb + cb - 1 - k], s_b, bd)
        ob_ref[0, rows[cb - 1 - k], :] = o
    st_f[...] = s_f
    st_b[...] = s_b


def _deltanet(p, ab_t, ab_c, conv_w, neg_a, dtb, bd, ones_bd, ncc):
    nb, nt, _ = p.shape
    nc = nt // CHUNK
    w3 = 3 * MIX_W
    c2 = lambda b, t: (0, 0)
    dn_blk = P_DN // RET_COLS
    pad_lanes = lambda col: jnp.concatenate([col.reshape(1, -1), jnp.zeros((1, LANES - col.size), F32)], axis=1)
    pr = math.gcd(math.gcd(ncc * CHUNK, nt), 256)
    n_pb, ctx_pb = nt // pr, ncc * CHUNK // pr
    qkv, gbeta, gbeta_c = pl.pallas_call(
        functools.partial(_dn_prep_kernel, rows=pr, ctx_blocks=ctx_pb, n_blocks=n_pb),
        grid=(nb, n_pb),
        in_specs=[pl.BlockSpec((1, pr, 4 * MIX_W), lambda b, t: (b, t, dn_blk)),
                  pl.BlockSpec((1, pr, 4 * MIX_W), lambda b, t: (b, jnp.maximum(t - 1, 0), dn_blk)),
                  pl.BlockSpec((1, pr, 4 * MIX_W), lambda b, t: (b, jnp.minimum(t + 1, n_pb - 1), dn_blk)),
                  pl.BlockSpec((1, 16, pr), lambda b, t: (b, 0, t)),
                  pl.BlockSpec((1, pr, LANES), lambda b, t: (b, t, 0)),
                  pl.BlockSpec((8, w3), c2),
                  pl.BlockSpec((8, 1), c2),
                  pl.BlockSpec((8, 1), c2),
                  pl.BlockSpec((1, LANES), c2),
                  pl.BlockSpec((1, LANES), c2),
                  pl.BlockSpec((MIX_W, MIX_W), c2)],
        out_specs=[pl.BlockSpec((1, pr, w3), lambda b, t: (b, t, 0)),
                   pl.BlockSpec((1, 16, pr), lambda b, t: (b, 0, t)),
                   pl.BlockSpec((1, pr, LANES), lambda b, t: (b, t, 0))],
        out_shape=[jax.ShapeDtypeStruct((nb, nt, w3), BF16),
                   jax.ShapeDtypeStruct((nb, 16, nt), F32),
                   jax.ShapeDtypeStruct((nb, nt, LANES), F32)],
        scratch_shapes=[pltpu.VMEM((pr + 16, w3), F32)],
        compiler_params=_params("arbitrary", "arbitrary"),
        name="dn_prep",
    )(p, p, p, ab_t, ab_c, conv_w, neg_a, dtb, pad_lanes(neg_a), pad_lanes(dtb), ones_bd)
    cb = 2
    assert nc % cb == 0 and ncc % cb == 0
    rows = cb * CHUNK
    cur_b = lambda t: _bwd_chunk(t, ncc // cb, nc // cb)
    o_shape = jax.ShapeDtypeStruct((nb, nt, MIX_W), F32)
    return pl.pallas_call(
        functools.partial(_dn_scan_kernel, cb=cb),
        grid=(nb, nc // cb),
        in_specs=[pl.BlockSpec((1, rows, w3), lambda b, t: (b, t, 0)),
                  pl.BlockSpec((1, rows, w3), lambda b, t: (b, cur_b(t), 0)),
                  pl.BlockSpec((1, 16, rows), lambda b, t: (b, 0, t)),
                  pl.BlockSpec((1, 16, rows), lambda b, t: (b, 0, cur_b(t))),
                  pl.BlockSpec((1, rows, LANES),lambda b, t: (b, t, 0)),
                  pl.BlockSpec((1, rows, LANES),lambda b, t: (b, cur_b(t), 0)),
                  pl.BlockSpec((MIX_W, MIX_W), c2)],
        out_specs=[pl.BlockSpec((1, rows, MIX_W), lambda b, t: (b, t, 0)),
                   pl.BlockSpec((1, rows, MIX_W), lambda b, t: (b, cur_b(t), 0))],
        out_shape=[o_shape, o_shape],
        scratch_shapes=[pltpu.VMEM((MIX_W, MIX_W), F32), pltpu.VMEM((MIX_W, MIX_W), F32)],
        compiler_params=_params("arbitrary", "arbitrary"),
        name="dn_scan",
    )(qkv, qkv, gbeta, gbeta, gbeta_c, gbeta_c, bd)


QK_W = 256


VT_ROWS = 144
ROPE_W = N_HEADS * ROPE_DIM
assert MLA_PAD - Q_LORA - KV_LORA == ROPE_W


def _mla_prep_kernel(p_ref, c_ref, s_ref, perm_ref, qg_ref, kg_ref, wqn_ref, wqr_ref, wa_ref, selq_ref, selc_ref,
                     selr_ref, selv_ref, one_ref, qt_ref, kv_ref, vt_ref, *, scale):
    p = p_ref[0]
    cos, sin, perm = c_ref[...], s_ref[...], perm_ref[...]
    cq = p[:, 0:Q_LORA].astype(F32)
    cqn = (cq * lax.rsqrt(jnp.mean(cq * cq, axis=-1, keepdims=True) + EPS) * qg_ref[...]).astype(BF16)
    q_nope = _dot(cqn, wqn_ref[...]).astype(BF16)
    q_rope = _dot(cqn, wqr_ref[...]).astype(BF16)
    q_rot = (_rot(q_rope, cos, sin, perm) * scale).astype(BF16)
    q_nope_s = (q_nope.astype(F32) * scale).astype(BF16)
    for h in range(N_HEADS):
        qt_ref[0, h] = (_dot_nt(wa_ref[h], q_nope_s) + _dot_nt(selq_ref[h], q_rot)).astype(BF16)
    ckv = p[:, Q_LORA:Q_LORA + KV_LORA].astype(F32)
    ckvn = (ckv * lax.rsqrt(jnp.mean(ckv * ckv, axis=-1, keepdims=True) + EPS) * kg_ref[...]).astype(BF16)
    kr = p[:, Q_LORA + KV_LORA:MLA_PAD]
    kr_rot = _rot(kr, cos, sin, perm).astype(BF16)
    kv_ref[0] = (_dot(ckvn, selc_ref[...]) + _dot(kr_rot, selr_ref[...])).astype(BF16)
    vt_ref[0] = (_dot_nt(selv_ref[...], ckvn) + one_ref[...]).astype(BF16)


def _mla_attn_kernel(qt_ref, kv_ref, vt_ref, wuv_ref, y_ref, m_ref, acc_ref, s_ref, *, tk, n_ctx, nt, latent):
    heads = range(N_HEADS)
    m_ref[...] = jnp.full_like(m_ref, -jnp.inf)
    acc_ref[...] = jnp.zeros_like(acc_ref)

    def scores(j0, size, slot):
        k = kv_ref[0, pl.ds(j0, size), :]
        for h in heads:
            s_ref[slot, h, 0:size, :] = _dot(k, qt_ref[0, h])

    def softmax_pv(j0, size, slot):
        vt = vt_ref[0, :, pl.ds(j0, size)]
        s = [s_ref[slot, h, 0:size, :] for h in heads]
        m_old = [m_ref[h] for h in heads]
        m_new = [jnp.maximum(m_old[h], jnp.max(s[h], axis=0, keepdims=True)) for h in heads]
        pr = [jnp.exp2(s[h] - m_new[h]).astype(BF16) for h in heads]
        pv = [_dot(vt, pr[h]) for h in heads]
        for h in heads:
            acc_ref[h] = jnp.exp2(m_old[h] - m_new[h]) * acc_ref[h] + pv[h]
            m_ref[h] = m_new[h]

    scores(0, n_ctx, 0)
    if not latent:
        softmax_pv(0, n_ctx, 0)
    else:
        n_tiles = (nt - n_ctx) // tk
        last = n_ctx + (n_tiles - 1) * tk
        scores(n_ctx, tk, 1)
        softmax_pv(0, n_ctx, 0)

        def body(jj, carry):
            t0 = pl.multiple_of(n_ctx + 2 * jj * tk, 256)
            t1 = pl.multiple_of(jnp.minimum(t0 + tk, last), 256)
            t2 = pl.multiple_of(jnp.minimum(t0 + 2 * tk, last), 256)
            scores(t1, tk, 0)
            softmax_pv(t0, tk, 1)
            scores(t2, tk, 1)
            softmax_pv(t1, tk, 0)
            return carry

        lax.fori_loop(0, n_tiles // 2, body, 0, unroll=2 if (n_tiles // 2) % 2 == 0 else 1)
        if n_tiles % 2:
            softmax_pv(last, tk, 1)

    y = None
    for h in range(N_HEADS):
        acc = acc_ref[h]
        o = (acc[0:KV_LORA, :] / acc[KV_LORA:KV_LORA + 1, :]).astype(BF16)
        term = _dot_tn(o, wuv_ref[h])
        y = term if y is None else y + term
    y_ref[0] = y.astype(BF16)


def _mla(p, cos, sin, perm, qg, kg, wqn, wqr, wa, selq, selc, selr, selv, one_col, wuv, n_ctx, ctx_out):
    nb, nt, _ = p.shape
    n_lat = nt - n_ctx
    tm = _pick(nt, (768, 384, 256, 128))
    scale = (NOPE_DIM + ROPE_DIM) ** -0.5 * math.log2(math.e)
    c2 = lambda b, i: (0, 0)
    c3 = lambda b, i: (0, 0, 0)
    qt, kv, vt = pl.pallas_call(
        functools.partial(_mla_prep_kernel, scale=scale),
        grid=(nb, nt // tm),
        in_specs=[pl.BlockSpec((1, tm, MLA_PAD), lambda b, i: (b, i, P_MLA // MLA_PAD)),
                  pl.BlockSpec((tm, ROPE_W), lambda b, i: (i, 0)),
                  pl.BlockSpec((tm, ROPE_W), lambda b, i: (i, 0)),
                  pl.BlockSpec((ROPE_W, ROPE_W), c2),
                  pl.BlockSpec((1, Q_LORA), c2),
                  pl.BlockSpec((1, KV_LORA), c2),
                  pl.BlockSpec((Q_LORA, N_HEADS * NOPE_DIM), c2),
                  pl.BlockSpec((Q_LORA, N_HEADS * ROPE_DIM), c2),
                  pl.BlockSpec((N_HEADS, QK_W, N_HEADS * NOPE_DIM), c3),
                  pl.BlockSpec((N_HEADS, QK_W, N_HEADS * ROPE_DIM), c3),
                  pl.BlockSpec((KV_LORA, QK_W), c2),
                  pl.BlockSpec((ROPE_W, QK_W), c2),
                  pl.BlockSpec((VT_ROWS, KV_LORA), c2),
                  pl.BlockSpec((VT_ROWS, 1), c2)],
        out_specs=[pl.BlockSpec((1, N_HEADS, QK_W, tm), lambda b, i: (b, 0, 0, i)),
                   pl.BlockSpec((1, tm, QK_W), lambda b, i: (b, i, 0)),
                   pl.BlockSpec((1, VT_ROWS, tm), lambda b, i: (b, 0, i))],
        out_shape=[jax.ShapeDtypeStruct((nb, N_HEADS, QK_W, nt), BF16),
                   jax.ShapeDtypeStruct((nb, nt, QK_W), BF16),
                   jax.ShapeDtypeStruct((nb, VT_ROWS, nt), BF16)],
        compiler_params=_params("arbitrary", "arbitrary"),
        name="mla_prep",
    )(p, cos, sin, perm, qg, kg, wqn, wqr, wa, selq, selc, selr, selv, one_col)
    tk = _pick(n_lat, (512, 256))

    def attend(tq, first_col, n_q, latent):
        if first_col % tq == 0:
            q_spec = pl.BlockSpec((1, N_HEADS, QK_W, tq), lambda b, i: (b, 0, 0, i + first_col // tq))
        else:
            q_spec = pl.BlockSpec((pl.Element(1), pl.Element(N_HEADS), pl.Element(QK_W), pl.Element(tq)),
                                  lambda b, i: (b, 0, 0, pl.multiple_of(first_col + i * tq, LANES)))
        return pl.pallas_call(
            functools.partial(_mla_attn_kernel, tk=tk, n_ctx=n_ctx, nt=nt, latent=latent),
            grid=(nb, n_q // tq),
            in_specs=[q_spec,
                      pl.BlockSpec((1, nt, QK_W), lambda b, i: (b, 0, 0)),
                      pl.BlockSpec((1, VT_ROWS, nt), lambda b, i: (b, 0, 0)),
                      pl.BlockSpec((N_HEADS, KV_LORA, MIX_W), c3)],
            out_specs=pl.BlockSpec((1, tq, MIX_W), lambda b, i: (b, i, 0)),
            out_shape=jax.ShapeDtypeStruct((nb, n_q, MIX_W), BF16),
            scratch_shapes=[pltpu.VMEM((N_HEADS, 1, tq), F32), pltpu.VMEM((N_HEADS, VT_ROWS, tq), F32),
                            pltpu.VMEM((2, N_HEADS, max(tk, n_ctx) if latent else n_ctx, tq), F32)],
            compiler_params=_params("arbitrary", "arbitrary"),
            name="mla_attn" if latent else "mla_attn_ctx",
        )(qt, kv, vt, wuv)

    y_lat = attend(_pick(n_lat, (512, 256)), n_ctx, n_lat, True)
    y_ctx = attend(_pick(n_ctx, (256, 128)), 0, n_ctx, False) if ctx_out else None
    return y_lat, y_ctx


def _merge_kernel(x_ref, ctx_ref, yr_ref, ys_ref, of_ref, ob_ref, ym_ref, z_ref, g0_ref, g1_ref, g2_ref, g3_ref,
                  wb_ref, wo_ref, ng_ref, gp_ref, ml_ref, mc_ref, ones_ref, o_ref, *, tm, n_ctx, row0, split):
    i = pl.program_id(1)
    if split:
        first = jnp.concatenate([ctx_ref[0], x_ref[0, 0:tm - n_ctx, :]], axis=0)
        x_res = jnp.where(i == 0, first, x_ref[0])
    else:
        x_res = x_ref[0]
    od = of_ref[0] + ob_ref[0]
    ms = _head_sum(od * od, ones_ref[...]) * (1.0 / HEAD_DIM)
    z = z_ref[0].astype(F32)
    ydn = (od * lax.rsqrt(ms + EPS) * ng_ref[...]) * (z * jax.nn.sigmoid(z))
    ys = (yr_ref[0], ys_ref[0], ydn.astype(BF16), ym_ref[0])
    gates = (g0_ref, g1_ref, g2_ref, g3_ref)
    acc = None
    for b in range(N_BRANCH):
        term = jax.nn.sigmoid(gates[b][0].astype(F32)) * _dot(ys[b], wb_ref[b])
        acc = term if acc is None else acc + term
    y = _dot(acc.astype(BF16), wo_ref[...])
    r = y * lax.rsqrt(jnp.mean(y * y, axis=-1, keepdims=True) + EPS) * gp_ref[...]
    rows = lax.broadcasted_iota(jnp.int32, (tm, 1), 0) + (row0 + i * tm)
    gate = jnp.where(rows < n_ctx, mc_ref[0, 2:3, :], ml_ref[0, 2:3, :])
    o_ref[0] = x_res + gate * r


def _merge(tokens, y_ret, y_sg, o_f, o_b, y_mla, p, wb, wo, ng, gp, mod, ones_bd, n_ctx, row0):
    nb, nt, _ = p.shape
    d = D_MODEL
    n_rows = nt - row0
    tm = _pick(n_rows, (768, 512, 384, 256, 128))
    c2 = lambda b, i: (0, 0)
    if row0 == 0:
        def window(width, col):
            return pl.BlockSpec((1, tm, width), lambda b, i: (b, i, col // width))

        (xa, ctx), (x_spec, ctx_spec), split = _token_sources(tokens, n_ctx, tm)
    else:
        def window(width, col):
            return pl.BlockSpec((pl.Element(1), pl.Element(tm), pl.Element(width)),
                                lambda b, i: (b, pl.multiple_of(row0 + i * tm, LANES), col))

        xa, x_spec, split = tokens, window(d, 0), False
        ctx, ctx_spec = jnp.zeros((nb, 8, d), F32), pl.BlockSpec((1, 8, d), lambda b, i: (b, 0, 0))
    y_spec = window(MIX_W, 0)
    assert y_mla.shape[1] == n_rows
    mla_spec = pl.BlockSpec((1, tm, MIX_W), lambda b, i: (b, i, 0))
    gate_specs = [window(d, P_GATE + k * d) for k in range(N_BRANCH)]
    return pl.pallas_call(
        functools.partial(_merge_kernel, tm=tm, n_ctx=n_ctx, row0=row0, split=split),
        grid=(nb, n_rows // tm),
        in_specs=[x_spec, ctx_spec, y_spec, y_spec, y_spec, y_spec, mla_spec,
                  window(MIX_W, P_DN + 3 * MIX_W),
                  *gate_specs,
                  pl.BlockSpec((N_BRANCH, MIX_W, d), lambda b, i: (0, 0, 0)),
                  pl.BlockSpec((d, d), c2),
                  pl.BlockSpec((1, MIX_W), c2),
                  pl.BlockSpec((1, d), c2),
                  pl.BlockSpec((1, 6, d), lambda b, i: (b, 0, 0)),
                  pl.BlockSpec((1, 6, d), lambda b, i: (nb, 0, 0)),
                  pl.BlockSpec((MIX_W, MIX_W), c2)],
        out_specs=pl.BlockSpec((1, tm, d), lambda b, i: (b, i, 0)),
        out_shape=jax.ShapeDtypeStruct((nb, n_rows, d), F32),
        compiler_params=_params("arbitrary", "arbitrary"),
        name="merge",
    )(xa, ctx, y_ret, y_sg, o_f, o_b, y_mla, p, p, p, p, p, wb, wo, ng, gp, mod, mod, ones_bd)


def _route(sel, aff):
    rows = [sel[e:e + 1, :] for e in range(N_EXPERTS)]
    pairs = [(a, b) for a in range(EXPERTS_PER_GROUP) for b in range(a + 1, EXPERTS_PER_GROUP)]
    grp_score, grp_pair = [], []
    for g in range(N_GROUPS):
        base = g * EXPERTS_PER_GROUP
        best = rows[base + pairs[0][0]] + rows[base + pairs[0][1]]
        best_p = jnp.zeros_like(best, dtype=jnp.int32)
        for pi in range(1, len(pairs)):
            s = rows[base + pairs[pi][0]] + rows[base + pairs[pi][1]]
            take = s > best
            best = jnp.where(take, s, best)
            best_p = jnp.where(take, pi, best_p)
        grp_score.append(best)
        grp_pair.append(best_p)
    top = grp_score[0]
    top_g = jnp.zeros_like(grp_pair[0])
    top_p = grp_pair[0]
    for g in range(1, N_GROUPS):
        take = grp_score[g] > top
        top = jnp.where(take, grp_score[g], top)
        top_g = jnp.where(take, g, top_g)
        top_p = jnp.where(take, grp_pair[g], top_p)
    picked = []
    for e in range(N_EXPERTS):
        g, k = divmod(e, EXPERTS_PER_GROUP)
        in_pair = None
        for pi, (a, b) in enumerate(pairs):
            if k in (a, b):
                hit = top_p == pi
                in_pair = hit if in_pair is None else (in_pair | hit)
        picked.append(jnp.where((top_g == g) & in_pair, aff[e:e + 1, :], 0.0))
    denom = picked[0]
    for e in range(1, N_EXPERTS):
        denom = denom + picked[e]
    return [pk / denom for pk in picked]


def _swiglu(hn, w1, w3, w2, scale):
    a = _dot(hn, w1)
    h = (a * jax.nn.sigmoid(a)) * _dot(hn, w3)
    if scale is not None:
        h = h * scale
    return _dot(h.astype(BF16), w2)


def _moe_kernel(x_ref, ml_ref, mc_ref, g2_ref, gp_ref, rw_ref, rb_ref, ws1_ref, ws3_ref, ws2_ref,
                w1_ref, w3_ref, w2_ref, o_ref, hn_ref, comb_t_ref, comb_ref, acc_ref, *, tm, rb, n_ctx):
    i = pl.program_id(1)
    e = pl.program_id(2)

    @pl.when(e == 0)
    def _():
        def blk(r, carry):
            r0 = pl.multiple_of(r * rb, rb)
            x = x_ref[0, pl.ds(r0, rb), :]
            hn = _norm_modulate(x, g2_ref[...], i * tm + r0 < n_ctx, mc_ref, ml_ref, 3, 4)
            hn_ref[pl.ds(r0, rb), :] = hn.astype(BF16)
            return carry

        lax.fori_loop(0, tm // rb, blk, 0)
        hn = hn_ref[...]
        aff = jax.nn.sigmoid(_dot_nt(rw_ref[...], hn))
        comb = _route(aff + rb_ref[...], aff)
        comb_t_ref[...] = jnp.zeros_like(comb_t_ref)
        for k in range(N_EXPERTS):
            comb_t_ref[k:k + 1, :] = comb[k]
        comb_ref[...] = comb_t_ref[...].T
        acc_ref[...] = _swiglu(hn, ws1_ref[0], ws3_ref[0], ws2_ref[0], None)

    @pl.when(e > 0)
    def _():
        lane = lax.broadcasted_iota(jnp.int32, (1, LANES), 1)
        comb = comb_ref[...]
        hn = hn_ref[...]
        first = 2 * (e - 1)
        y = None
        for k in range(2):
            c_k = jnp.sum(jnp.where(lane == first + k, comb, 0.0), axis=-1, keepdims=True)
            term = _swiglu(hn, w1_ref[0, k], w3_ref[0, k], w2_ref[0, k], c_k)
            y = term if y is None else y + term
        acc_ref[...] += y

    @pl.when(e == pl.num_programs(2) - 1)
    def _():
        y = acc_ref[...]
        r = y * lax.rsqrt(jnp.mean(y * y, axis=-1, keepdims=True) + EPS) * gp_ref[...]
        rows = lax.broadcasted_iota(jnp.int32, (tm, 1), 0) + i * tm
        gate = jnp.where(rows < n_ctx, mc_ref[0, 5:6, :], ml_ref[0, 5:6, :])
        o_ref[0] = x_ref[0] + gate * r


def _moe(xa, mod, g2, gp, rw_t, rbias, ws1, ws3, ws2, w1, w3, w2, layer, n_ctx):
    nb, nt, d = xa.shape
    tm = _pick(nt, (1024, 768, 512, 384, 256, 128))
    n_pairs = w1.shape[1] // 2
    row = lambda b, i, e: (b, i, 0)
    c2 = lambda b, i, e: (0, 0)
    shared_blk = lambda b, i, e: (layer, 0, 0)
    pair_blk = lambda b, i, e: (layer, jnp.maximum(e - 1, 0), 0, 0)
    return pl.pallas_call(
        functools.partial(_moe_kernel, tm=tm, rb=ROW_BLOCK, n_ctx=n_ctx),
        grid=(nb, nt // tm, n_pairs + 1),
        in_specs=[pl.BlockSpec((1, tm, d), row),
                  pl.BlockSpec((1, 6, d), lambda b, i, e: (b, 0, 0)),
                  pl.BlockSpec((1, 6, d), lambda b, i, e: (nb, 0, 0)),
                  pl.BlockSpec((1, d), c2),
                  pl.BlockSpec((1, d), c2),
                  pl.BlockSpec((N_EXPERTS, d), c2),
                  pl.BlockSpec((N_EXPERTS, 1), c2),
                  pl.BlockSpec((1, d, D_EXPERT), shared_blk),
                  pl.BlockSpec((1, d, D_EXPERT), shared_blk),
                  pl.BlockSpec((1, D_EXPERT, d), shared_blk),
                  pl.BlockSpec((1, 2, d, D_EXPERT), pair_blk),
                  pl.BlockSpec((1, 2, d, D_EXPERT), pair_blk),
                  pl.BlockSpec((1, 2, D_EXPERT, d), pair_blk)],
        out_specs=pl.BlockSpec((1, tm, d), row),
        out_shape=jax.ShapeDtypeStruct((nb, nt, d), F32),
        scratch_shapes=[pltpu.VMEM((tm, d), BF16), pltpu.VMEM((LANES, tm), F32), pltpu.VMEM((tm, LANES), F32),
                        pltpu.VMEM((tm, d), F32)],
        compiler_params=_params("arbitrary", "arbitrary", "arbitrary"),
        name="moe",
    )(xa, mod, mod, g2, gp, rw_t, rbias, ws1, ws3, ws2, w1, w3, w2)


def _swap_perm(width, group):
    j = np.arange(width)
    src = np.where((j % group) < group // 2, j + group // 2, j - group // 2)
    return jnp.asarray(np.arange(width)[:, None] == src[None, :], BF16)


def _rope_tables(n_lat, n_ctx):
    def angles(pos, dim):
        half = dim // 2
        inv = ROPE_BASE ** (-jnp.arange(half, dtype=F32) / half)
        return pos.astype(F32)[:, None] * inv[None, :]

    def tables(cos_parts, sin_parts, reps):
        cos = jnp.tile(jnp.concatenate(cos_parts, axis=-1), (1, reps))
        sin = jnp.tile(jnp.concatenate(sin_parts, axis=-1), (1, reps))
        w = cos.shape[1]
        return (jnp.concatenate([jnp.ones((n_ctx, w), F32), cos], axis=0),
                jnp.concatenate([jnp.zeros((n_ctx, w), F32), sin], axis=0))

    rows = n_lat // GRID_W
    ang_t = angles(jnp.arange(n_lat), HEAD_DIM)
    ang_r = angles(jnp.repeat(jnp.arange(rows), GRID_W), ROPE_DIM // 2)
    ang_c = angles(jnp.tile(jnp.arange(GRID_W), rows), ROPE_DIM // 2)
    ct, st = jnp.cos(ang_t), jnp.sin(ang_t)
    ret = tables([ct, ct], [-st, st], N_HEADS)
    cr, sr, cc, sc = jnp.cos(ang_r), jnp.sin(ang_r), jnp.cos(ang_c), jnp.sin(ang_c)
    mla = tables([cr, cr, cc, cc], [-sr, sr, -sc, sc], N_HEADS)
    return ret, mla


def _ret_tables(logit):
    log_g = jax.nn.log_sigmoid(logit.astype(F32))
    lane_lg = jnp.repeat(log_g, HEAD_DIM, axis=1)
    idx = jnp.arange(CHUNK, dtype=F32)[:, None]
    kd = jnp.stack([jnp.exp(lane_lg[0][None, :] * (CHUNK - 1 - idx)), jnp.exp(lane_lg[1][None, :] * idx)])
    qd = jnp.stack([jnp.exp(lane_lg[0][None, :] * (idx + 1)), jnp.exp(lane_lg[1][None, :] * (CHUNK - idx))])
    cd = jnp.exp(lane_lg * CHUNK)[:, None, :]
    diff = idx - idx.T
    blocks = []
    for h in range(N_HEADS):
        f = jnp.exp(log_g[0, h] * jnp.where(diff >= 0, diff, 0.0))
        b = jnp.exp(log_g[1, h] * jnp.where(diff < 0, -diff, 0.0))
        blocks.append(jnp.where(diff >= 0, f, b))
    dm = jnp.concatenate(blocks, axis=1)
    return kd, cd, qd, dm


def _pack_w_in(w_in):
    d = w_in.shape[0]
    mla = jnp.concatenate([w_in[:, OFF_MLA:OFF_MLA + MLA_COLS], jnp.zeros((d, MLA_PAD - MLA_COLS), w_in.dtype)], 1)
    w = jnp.concatenate([w_in[:, OFF_RET:OFF_RET + RET_COLS], w_in[:, OFF_DN:OFF_DN + 4 * MIX_W],
                         w_in[:, OFF_SG:OFF_SG + SG_COLS], mla, w_in[:, OFF_GATE:OFF_GATE + GATE_COLS]], axis=1)
    wab = w_in[:, OFF_DN + 4 * MIX_W:OFF_DN + DN_COLS]
    wabc = jnp.concatenate([wab, jnp.zeros((d, LANES - 4 * N_HEADS), w_in.dtype)], axis=1)
    return w.astype(BF16), wab.T.astype(BF16), wabc.astype(BF16)


def _mla_weights(w_uq, w_ukv):
    dq = NOPE_DIM + ROPE_DIM
    dkv = NOPE_DIM + V_DIM
    wq = w_uq.reshape(Q_LORA, N_HEADS, dq)
    wqn = wq[:, :, :NOPE_DIM].reshape(Q_LORA, N_HEADS * NOPE_DIM)
    wqr = wq[:, :, NOPE_DIM:].reshape(Q_LORA, N_HEADS * ROPE_DIM)
    wkv = w_ukv.reshape(KV_LORA, N_HEADS, dkv)
    head_eye = jnp.eye(N_HEADS, dtype=F32)
    wa = jnp.einsum("chd,hg->hcgd", wkv[:, :, :NOPE_DIM], head_eye).reshape(N_HEADS, KV_LORA, N_HEADS * NOPE_DIM)
    wa = jnp.pad(wa, ((0, 0), (0, QK_W - KV_LORA), (0, 0)))
    wuv = jnp.einsum("chd,hg->hcgd", wkv[:, :, NOPE_DIM:], head_eye).reshape(N_HEADS, KV_LORA, MIX_W)
    selq = np.zeros((N_HEADS, QK_W, N_HEADS * ROPE_DIM), np.float32)
    for h in range(N_HEADS):
        selq[h, KV_LORA:KV_LORA + ROPE_DIM, h * ROPE_DIM:(h + 1) * ROPE_DIM] = np.eye(ROPE_DIM)
    selc = np.zeros((KV_LORA, QK_W), np.float32)
    selc[:, 0:KV_LORA] = np.eye(KV_LORA)
    selr = np.zeros((ROPE_W, QK_W), np.float32)
    selr[0:ROPE_DIM, KV_LORA:KV_LORA + ROPE_DIM] = np.eye(ROPE_DIM)
    selv = np.zeros((VT_ROWS, KV_LORA), np.float32)
    selv[0:KV_LORA, :] = np.eye(KV_LORA)
    one_col = np.zeros((VT_ROWS, 1), np.float32)
    one_col[KV_LORA, 0] = 1.0
    return (tuple(jnp.asarray(a, BF16) for a in (wqn, wqr, wa, selq, selc, selr, selv))
            + (jnp.asarray(one_col), wuv.astype(BF16)))


def kernel(x, c, ctx, c_ctx, w_ada, b_ada, g_pre1, g_post1, g_pre2, g_post2, w_in, ret_decay_logit, sg_norm_g, sg_w, sg_b, dn_conv_w, dn_A_log, dn_dt_bias, dn_norm_g, mla_q_norm_g, mla_kv_norm_g, mla_w_uq, mla_w_ukv, w_branch, w_out, router_w, router_bias, moe_w1, moe_w3, moe_w2, shared_w1, shared_w3, shared_w2):
    nb, n_lat, d = x.shape
    n_ctx = ctx.shape[1]
    depth = w_in.shape[0]
    assert d == D_MODEL and n_lat % GRID_W == 0 and n_lat % CHUNK == 0 and n_ctx % 256 == 0
    ncc = n_ctx // CHUNK

    n_cond = -(-(nb + 1) // 8) * 8
    cond = jnp.concatenate([c, c_ctx[None], jnp.zeros((n_cond - nb - 1, d), F32)], axis=0)
    mod_all = _adaln(cond, w_ada, b_ada).reshape(depth, n_cond, 6, d)

    (ret_cos, ret_sin), (mla_cos, mla_sin) = _rope_tables(n_lat, n_ctx)
    perm_ret = _swap_perm(MIX_W, HEAD_DIM)
    perm_mla = _swap_perm(N_HEADS * ROPE_DIM, ROPE_DIM // 2)
    lane_head = jnp.arange(MIX_W) // HEAD_DIM
    bd = (lane_head[:, None] == lane_head[None, :]).astype(F32)
    ones_bd = bd.astype(BF16)
    rw_t = router_w.T.astype(BF16)
    rbias = router_bias.astype(F32)[:, None]
    expert_w = tuple(w.astype(BF16) for w in (shared_w1, shared_w3, shared_w2, moe_w1, moe_w3, moe_w2))

    xa = (ctx, x)
    for l in range(depth):
        last = l == depth - 1
        if last and isinstance(xa, tuple):
            xa = jnp.concatenate(xa, axis=1)
        mod = mod_all[l]
        w_l, wab_l, wabc_l = _pack_w_in(w_in[l])
        p, ab_t, ab_c = _inproj(xa, n_ctx + n_lat, mod, g_pre1[l][None], w_l, wab_l, wabc_l, n_ctx)

        kd, cd, qd, dm = _ret_tables(ret_decay_logit[l])
        wcat = jnp.concatenate([sg_w[l, h] for h in range(N_HEADS)], axis=1).astype(BF16)
        sg_bias = jnp.repeat(sg_b[l].T, HEAD_DIM, axis=1)
        y_ret, y_sg = _retention_and_sgate(p, ret_cos, ret_sin, perm_ret, (kd, cd, qd, dm, bd, ones_bd),
                                           sg_norm_g[l][None], wcat, sg_bias, ncc)

        neg_a = (-jnp.exp(dn_A_log[l].astype(F32))).reshape(2 * N_HEADS, 1)
        dtb = dn_dt_bias[l].astype(F32).reshape(2 * N_HEADS, 1)
        conv_w = jnp.concatenate([dn_conv_w[l], jnp.zeros((8 - CONV_W, 3 * MIX_W), F32)], axis=0)
        o_f, o_b = _deltanet(p, ab_t, ab_c, conv_w, neg_a, dtb, bd, ones_bd, ncc)

        row0 = n_ctx if last else 0
        y_mla, y_mla_ctx = _mla(p, mla_cos, mla_sin, perm_mla, mla_q_norm_g[l][None], mla_kv_norm_g[l][None],
                                *_mla_weights(mla_w_uq[l], mla_w_ukv[l]), n_ctx, not last)
        if not last:
            y_mla = jnp.concatenate([y_mla_ctx, y_mla], axis=1)

        xa = _merge(xa, y_ret, y_sg, o_f, o_b, y_mla, p, w_branch[l].astype(BF16), w_out[l].astype(BF16),
                    jnp.tile(dn_norm_g[l], N_HEADS)[None], g_post1[l][None], mod, ones_bd, n_ctx, row0)

        xa = _moe(xa, mod, g_pre2[l][None], g_post2[l][None], rw_t, rbias, *expert_w, l, n_ctx - row0)
    return xa
```
